```python
import jax, jax.numpy as jnp
from jax import lax
import numpy as np

D_MODEL = 1024
BATCH = 16
SEQ = 2048
DEPTH = 2

GRID_W = 64
D_ATTN = D_MODEL // 2
D_HGRN = D_MODEL // 4
D_CONV = D_MODEL - D_ATTN - D_HGRN
HEAD_DIM = 64
N_HEADS = D_ATTN // HEAD_DIM
N_KV_HEADS = 2
KV_GROUP = N_HEADS // N_KV_HEADS
ROPE_THETA = 10000.0
Q_BLOCK = 128
HGRN_HEAD_DIM = 64
HGRN_HEADS = D_HGRN // HGRN_HEAD_DIM
HGRN_CHUNK = 64
F_MIN = 1e-6
CONV_WIDTH = 31
CONV_PAD = (CONV_WIDTH - 1) // 2
D_FF = ((8 * D_MODEL // 3 + 255) // 256) * 256
EPS = 1e-6
LN_EPS = 1e-5
SPLITS = (D_ATTN, N_KV_HEADS * HEAD_DIM, N_KV_HEADS * HEAD_DIM,
          D_HGRN, D_HGRN, D_HGRN, D_HGRN, D_HGRN, D_CONV, D_CONV)
D_IN_PROJ = int(sum(SPLITS))
SPLIT_IDX = tuple(int(c) for c in np.cumsum(SPLITS)[:-1])

kernel_name = "hybrid_attn_hgrn2_conformer_encoder"


def _rms_norm(x, w, eps=EPS):
    xf = x.astype(jnp.float32)
    y = xf * lax.rsqrt(jnp.mean(xf * xf, axis=-1, keepdims=True) + eps)
    return (y * w.astype(jnp.float32)).astype(x.dtype)


def _layer_norm(x, w, b, eps=LN_EPS):
    xf = x.astype(jnp.float32)
    mu = jnp.mean(xf, axis=-1, keepdims=True)
    xc = xf - mu
    var = jnp.mean(xc * xc, axis=-1, keepdims=True)
    y = xc * lax.rsqrt(var + eps) * w.astype(jnp.float32) + b.astype(jnp.float32)
    return y.astype(x.dtype)


def _axial_rope_tables(seq_len, dtype):
    rows = seq_len // GRID_W
    row_id = jnp.repeat(jnp.arange(rows, dtype=jnp.float32), GRID_W)
    col_id = jnp.tile(jnp.arange(GRID_W, dtype=jnp.float32), rows)
    half = HEAD_DIM // 2
    inv_freq = ROPE_THETA ** (-jnp.arange(0, half, 2, dtype=jnp.float32) / half)
    ang_r = row_id[:, None] * inv_freq[None, :]
    ang_c = col_id[:, None] * inv_freq[None, :]
    ang = jnp.concatenate([ang_r, ang_r, ang_c, ang_c], axis=-1)
    return jnp.cos(ang).astype(dtype), jnp.sin(ang).astype(dtype)


def _axial_rope(x, cos, sin):
    x_r1, x_r2, x_c1, x_c2 = jnp.split(x, 4, axis=-1)
    rot = jnp.concatenate([-x_r2, x_r1, -x_c2, x_c1], axis=-1)
    return x * cos[None, :, None, :] + rot * sin[None, :, None, :]


def _gqa_attention(q, k, v, cos, sin):
    B, S = q.shape[0], q.shape[1]
    q = _axial_rope(q, cos, sin) * (HEAD_DIM ** -0.5)
    k = _axial_rope(k, cos, sin)
    qb = q.reshape(B, S // Q_BLOCK, Q_BLOCK, N_KV_HEADS, KV_GROUP, HEAD_DIM)
    qb = qb.transpose(1, 0, 2, 3, 4, 5)

    def one_block(q_blk):
        s = jnp.einsum('bqkgd,bskd->bkgqs', q_blk, k, preferred_element_type=jnp.float32)
        p = jax.nn.softmax(s, axis=-1).astype(v.dtype)
        return jnp.einsum('bkgqs,bskd->bqkgd', p, v)

    o = lax.map(one_block, qb)
    return o.transpose(1, 0, 2, 3, 4, 5).reshape(B, S, D_ATTN)


def _hgrn2_scan(q, k, v, log_f):
    B, S = q.shape[0], q.shape[1]
    n_chunks = S // HGRN_CHUNK

    def to_chunks(a):
        return a.reshape(B, n_chunks, HGRN_CHUNK, HGRN_HEADS, a.shape[-1]).transpose(1, 0, 3, 2, 4)

    in_chunk_mask = jnp.tril(jnp.ones((HGRN_CHUNK, HGRN_CHUNK), dtype=bool))
    mask5 = in_chunk_mask[None, None, :, :, None]

    def step(state, xs):
        qc, kc, vc, gc = xs
        b = jnp.cumsum(gc, axis=2)
        diff = b[:, :, :, None, :] - b[:, :, None, :, :]
        decay = jnp.where(mask5, jnp.exp(jnp.where(mask5, diff, 0.0)), 0.0)
        scores = jnp.einsum('bhtd,bhsd,bhtsd->bhts', qc, kc, decay)
        o = (jnp.einsum('bhts,bhse->bhte', scores, vc)
             + jnp.einsum('bhtd,bhde->bhte', qc * jnp.exp(b), state))
        b_last = b[:, :, -1:, :]
        new_state = (jnp.exp(b_last[:, :, 0, :])[..., None] * state
                     + jnp.einsum('bhsd,bhse->bhde', kc * jnp.exp(b_last - b), vc))
        return new_state, o

    state0 = jnp.zeros((B, HGRN_HEADS, q.shape[-1], v.shape[-1]), jnp.float32)
    _, o = lax.scan(step, state0, (to_chunks(q), to_chunks(k), to_chunks(v), to_chunks(log_f)))
    return o.transpose(1, 0, 3, 2, 4).reshape(B, S, HGRN_HEADS, v.shape[-1])


def _hgrn2_gates(z, lb):
    z = z.astype(jnp.float32)
    lb = lb.reshape(1, 1, HGRN_HEADS, HGRN_HEAD_DIM)
    f = lb + (1.0 - lb) * jax.nn.sigmoid(z)
    log_f = jnp.log(jnp.maximum(f, F_MIN))
    k = (1.0 - lb) * jax.nn.sigmoid(-z)
    return log_f, k


def _hgrn2_mixer(hq, hf_fwd, hf_bwd, hi, hg, lb_fwd, lb_bwd, gnorm_w):
    B, S = hq.shape[0], hq.shape[1]
    shp = (B, S, HGRN_HEADS, HGRN_HEAD_DIM)
    q = hq.reshape(shp).astype(jnp.float32)
    v = hi.reshape(shp).astype(jnp.float32)
    log_f_fw, k_fw = _hgrn2_gates(hf_fwd.reshape(shp), lb_fwd)
    log_f_bw, k_bw = _hgrn2_gates(hf_bwd.reshape(shp), lb_bwd)
    o_fw = _hgrn2_scan(q, k_fw, v, log_f_fw)
    o_bw = _hgrn2_scan(q[:, ::-1], k_bw[:, ::-1], v[:, ::-1], log_f_bw[:, ::-1])[:, ::-1]
    o = (o_fw + o_bw).astype(hq.dtype)
    g = hg.reshape(shp)
    o = _rms_norm(o, gnorm_w) * jax.nn.silu(g)
    return o.reshape(B, S, D_HGRN)


def _conformer_conv(a, b, dw_w, dw_b, ln_w, ln_b, pw_w, pw_b):
    u = a * jax.nn.sigmoid(b)
    u = lax.conv_general_dilated(
        u, dw_w[:, None, :].astype(u.dtype), window_strides=(1,),
        padding=[(CONV_PAD, CONV_PAD)], dimension_numbers=('NWC', 'WIO', 'NWC'),
        feature_group_count=D_CONV) + dw_b
    u = jax.nn.silu(_layer_norm(u, ln_w, ln_b))
    return u @ pw_w + pw_b


def _fwd_setup_inputs(seed: int = 0) -> dict:
    key = jax.random.key(seed)
    ks = jax.random.split(key, 24)
    f32 = jnp.float32

    def gain(k, shape):
        return 1.0 + 0.02 * jax.random.normal(k, shape, f32)

    def bias(k, shape):
        return 0.02 * jax.random.normal(k, shape, f32)

    def dense(k, shape, fan_in):
        return jax.random.normal(k, shape, f32) * fan_in ** -0.5

    return {
        "x": jax.random.normal(ks[0], (BATCH, SEQ, D_MODEL), f32),
        "mix_norm_w": gain(ks[1], (DEPTH, D_MODEL)),
        "w_in": dense(ks[2], (DEPTH, D_MODEL, D_IN_PROJ), D_MODEL),
        "q_norm_w": gain(ks[3], (DEPTH, HEAD_DIM)),
        "k_norm_w": gain(ks[4], (DEPTH, HEAD_DIM)),
        "hgrn_lb_logits": 0.5 * jax.random.normal(ks[5], (DEPTH, 2, D_HGRN), f32),
        "hgrn_gnorm_w": gain(ks[6], (DEPTH, HGRN_HEAD_DIM)),
        "conv_dw_w": dense(ks[7], (DEPTH, CONV_WIDTH, D_CONV), CONV_WIDTH),
        "conv_dw_b": bias(ks[8], (DEPTH, D_CONV)),
        "conv_ln_w": gain(ks[9], (DEPTH, D_CONV)),
        "conv_ln_b": bias(ks[10], (DEPTH, D_CONV)),
        "conv_pw_w": dense(ks[11], (DEPTH, D_CONV, D_CONV), D_CONV),
        "conv_pw_b": bias(ks[12], (DEPTH, D_CONV)),
        "attn_out_norm_w": gain(ks[13], (DEPTH, D_ATTN)),
        "conv_out_norm_w": gain(ks[14], (DEPTH, D_CONV)),
        "w_out": dense(ks[15], (DEPTH, D_MODEL, D_MODEL), D_MODEL),
        "ffn_norm_w": gain(ks[16], (DEPTH, D_MODEL)),
        "w_gate": dense(ks[17], (DEPTH, D_MODEL, D_FF), D_MODEL),
        "w_up": dense(ks[18], (DEPTH, D_MODEL, D_FF), D_MODEL),
        "w_down": dense(ks[19], (DEPTH, D_FF, D_MODEL), D_FF),
    }


def _fwd_reference(x, mix_norm_w, w_in, q_norm_w, k_norm_w, hgrn_lb_logits, hgrn_gnorm_w,
              conv_dw_w, conv_dw_b, conv_ln_w, conv_ln_b, conv_pw_w, conv_pw_b,
              attn_out_norm_w, conv_out_norm_w, w_out, ffn_norm_w, w_gate, w_up, w_down):
    B, S = x.shape[0], x.shape[1]
    cos, sin = _axial_rope_tables(S, x.dtype)
    sm = jax.nn.softmax(hgrn_lb_logits.astype(jnp.float32), axis=0)
    lower_bounds = jnp.cumsum(sm, axis=0) - sm[0:1]

    for l in range(DEPTH):
        h = _rms_norm(x, mix_norm_w[l])
        proj = h @ w_in[l]
        q, k, v, hq, hf_fw, hf_bw, hi, hg, ca, cb = jnp.split(proj, SPLIT_IDX, axis=-1)

        q = _rms_norm(q.reshape(B, S, N_HEADS, HEAD_DIM), q_norm_w[l])
        k = _rms_norm(k.reshape(B, S, N_KV_HEADS, HEAD_DIM), k_norm_w[l])
        v = v.reshape(B, S, N_KV_HEADS, HEAD_DIM)
        y_attn = _gqa_attention(q, k, v, cos, sin)

        y_hgrn = _hgrn2_mixer(hq, hf_fw, hf_bw, hi, hg,
                              lower_bounds[l, 0], lower_bounds[l, 1], hgrn_gnorm_w[l])

        y_conv = _conformer_conv(ca, cb, conv_dw_w[l], conv_dw_b[l], conv_ln_w[l],
                                 conv_ln_b[l], conv_pw_w[l], conv_pw_b[l])

        mixed = jnp.concatenate([_rms_norm(y_attn, attn_out_norm_w[l]),
                                 y_hgrn,
                                 _rms_norm(y_conv, conv_out_norm_w[l])], axis=-1)
        x = x + mixed @ w_out[l]

        h = _rms_norm(x, ffn_norm_w[l])
        x = x + (jax.nn.silu(h @ w_gate[l]) * (h @ w_up[l])) @ w_down[l]
    return x


import jax as _jax
import jax.numpy as _jnp

TWIN_FORMAT = 'train_step'
FWD_PARAMS = ['x', 'mix_norm_w', 'w_in', 'q_norm_w', 'k_norm_w', 'hgrn_lb_logits', 'hgrn_gnorm_w', 'conv_dw_w', 'conv_dw_b', 'conv_ln_w', 'conv_ln_b', 'conv_pw_w', 'conv_pw_b', 'attn_out_norm_w', 'conv_out_norm_w', 'w_out', 'ffn_norm_w', 'w_gate', 'w_up', 'w_down']
TWIN_WEIGHTS = ['mix_norm_w', 'w_in', 'q_norm_w', 'k_norm_w', 'hgrn_lb_logits', 'hgrn_gnorm_w', 'conv_dw_w', 'conv_dw_b', 'conv_ln_w', 'conv_ln_b', 'conv_pw_w', 'conv_pw_b', 'attn_out_norm_w', 'conv_out_norm_w', 'w_out', 'ffn_norm_w', 'w_gate', 'w_up', 'w_down']
TWIN_DIFF_INPUT = 'x'
TWIN_INPUTS = ['x', 'mix_norm_w', 'w_in', 'q_norm_w', 'k_norm_w', 'hgrn_lb_logits', 'hgrn_gnorm_w', 'conv_dw_w', 'conv_dw_b', 'conv_ln_w', 'conv_ln_b', 'conv_pw_w', 'conv_pw_b', 'attn_out_norm_w', 'conv_out_norm_w', 'w_out', 'ffn_norm_w', 'w_gate', 'w_up', 'w_down', 'loss_target', 'm_mix_norm_w', 'm_w_in', 'm_q_norm_w', 'm_k_norm_w', 'm_hgrn_lb_logits', 'm_hgrn_gnorm_w', 'm_conv_dw_w', 'm_conv_dw_b', 'm_conv_ln_w', 'm_conv_ln_b', 'm_conv_pw_w', 'm_conv_pw_b', 'm_attn_out_norm_w', 'm_conv_out_norm_w', 'm_w_out', 'm_ffn_norm_w', 'm_w_gate', 'm_w_up', 'm_w_down', 'v_mix_norm_w', 'v_w_in', 'v_q_norm_w', 'v_k_norm_w', 'v_hgrn_lb_logits', 'v_hgrn_gnorm_w', 'v_conv_dw_w', 'v_conv_dw_b', 'v_conv_ln_w', 'v_conv_ln_b', 'v_conv_pw_w', 'v_conv_pw_b', 'v_attn_out_norm_w', 'v_conv_out_norm_w', 'v_w_out', 'v_ffn_norm_w', 'v_w_gate', 'v_w_up', 'v_w_down']
TWIN_OUTPUTS = ['loss', 'grad_x', 'grad_mix_norm_w', 'grad_w_in', 'grad_q_norm_w', 'grad_k_norm_w', 'grad_hgrn_lb_logits', 'grad_hgrn_gnorm_w', 'grad_conv_dw_w', 'grad_conv_dw_b', 'grad_conv_ln_w', 'grad_conv_ln_b', 'grad_conv_pw_w', 'grad_conv_pw_b', 'grad_attn_out_norm_w', 'grad_conv_out_norm_w', 'grad_w_out', 'grad_ffn_norm_w', 'grad_w_gate', 'grad_w_up', 'grad_w_down', 'delta_mix_norm_w', 'delta_w_in', 'delta_q_norm_w', 'delta_k_norm_w', 'delta_hgrn_lb_logits', 'delta_hgrn_gnorm_w', 'delta_conv_dw_w', 'delta_conv_dw_b', 'delta_conv_ln_w', 'delta_conv_ln_b', 'delta_conv_pw_w', 'delta_conv_pw_b', 'delta_attn_out_norm_w', 'delta_conv_out_norm_w', 'delta_w_out', 'delta_ffn_norm_w', 'delta_w_gate', 'delta_w_up', 'delta_w_down', 'new_m_mix_norm_w', 'new_m_w_in', 'new_m_q_norm_w', 'new_m_k_norm_w', 'new_m_hgrn_lb_logits', 'new_m_hgrn_gnorm_w', 'new_m_conv_dw_w', 'new_m_conv_dw_b', 'new_m_conv_ln_w', 'new_m_conv_ln_b', 'new_m_conv_pw_w', 'new_m_conv_pw_b', 'new_m_attn_out_norm_w', 'new_m_conv_out_norm_w', 'new_m_w_out', 'new_m_ffn_norm_w', 'new_m_w_gate', 'new_m_w_up', 'new_m_w_down', 'new_v_mix_norm_w', 'new_v_w_in', 'new_v_q_norm_w', 'new_v_k_norm_w', 'new_v_hgrn_lb_logits', 'new_v_hgrn_gnorm_w', 'new_v_conv_dw_w', 'new_v_conv_dw_b', 'new_v_conv_ln_w', 'new_v_conv_ln_b', 'new_v_conv_pw_w', 'new_v_conv_pw_b', 'new_v_attn_out_norm_w', 'new_v_conv_out_norm_w', 'new_v_w_out', 'new_v_ffn_norm_w', 'new_v_w_gate', 'new_v_w_up', 'new_v_w_down']
TWIN_LEAF_KINDS = {'loss': 'loss', 'grad_x': 'grad_x', 'grad_mix_norm_w': 'grad_w', 'grad_w_in': 'grad_w', 'grad_q_norm_w': 'grad_w', 'grad_k_norm_w': 'grad_w', 'grad_hgrn_lb_logits': 'grad_w', 'grad_hgrn_gnorm_w': 'grad_w', 'grad_conv_dw_w': 'grad_w', 'grad_conv_dw_b': 'grad_w', 'grad_conv_ln_w': 'grad_w', 'grad_conv_ln_b': 'grad_w', 'grad_conv_pw_w': 'grad_w', 'grad_conv_pw_b': 'grad_w', 'grad_attn_out_norm_w': 'grad_w', 'grad_conv_out_norm_w': 'grad_w', 'grad_w_out': 'grad_w', 'grad_ffn_norm_w': 'grad_w', 'grad_w_gate': 'grad_w', 'grad_w_up': 'grad_w', 'grad_w_down': 'grad_w', 'delta_mix_norm_w': 'delta_w', 'delta_w_in': 'delta_w', 'delta_q_norm_w': 'delta_w', 'delta_k_norm_w': 'delta_w', 'delta_hgrn_lb_logits': 'delta_w', 'delta_hgrn_gnorm_w': 'delta_w', 'delta_conv_dw_w': 'delta_w', 'delta_conv_dw_b': 'delta_w', 'delta_conv_ln_w': 'delta_w', 'delta_conv_ln_b': 'delta_w', 'delta_conv_pw_w': 'delta_w', 'delta_conv_pw_b': 'delta_w', 'delta_attn_out_norm_w': 'delta_w', 'delta_conv_out_norm_w': 'delta_w', 'delta_w_out': 'delta_w', 'delta_ffn_norm_w': 'delta_w', 'delta_w_gate': 'delta_w', 'delta_w_up': 'delta_w', 'delta_w_down': 'delta_w', 'new_m_mix_norm_w': 'new_m', 'new_m_w_in': 'new_m', 'new_m_q_norm_w': 'new_m', 'new_m_k_norm_w': 'new_m', 'new_m_hgrn_lb_logits': 'new_m', 'new_m_hgrn_gnorm_w': 'new_m', 'new_m_conv_dw_w': 'new_m', 'new_m_conv_dw_b': 'new_m', 'new_m_conv_ln_w': 'new_m', 'new_m_conv_ln_b': 'new_m', 'new_m_conv_pw_w': 'new_m', 'new_m_conv_pw_b': 'new_m', 'new_m_attn_out_norm_w': 'new_m', 'new_m_conv_out_norm_w': 'new_m', 'new_m_w_out': 'new_m', 'new_m_ffn_norm_w': 'new_m', 'new_m_w_gate': 'new_m', 'new_m_w_up': 'new_m', 'new_m_w_down': 'new_m', 'new_v_mix_norm_w': 'new_v', 'new_v_w_in': 'new_v', 'new_v_q_norm_w': 'new_v', 'new_v_k_norm_w': 'new_v', 'new_v_hgrn_lb_logits': 'new_v', 'new_v_hgrn_gnorm_w': 'new_v', 'new_v_conv_dw_w': 'new_v', 'new_v_conv_dw_b': 'new_v', 'new_v_conv_ln_w': 'new_v', 'new_v_conv_ln_b': 'new_v', 'new_v_conv_pw_w': 'new_v', 'new_v_conv_pw_b': 'new_v', 'new_v_attn_out_norm_w': 'new_v', 'new_v_conv_out_norm_w': 'new_v', 'new_v_w_out': 'new_v', 'new_v_ffn_norm_w': 'new_v', 'new_v_w_gate': 'new_v', 'new_v_w_up': 'new_v', 'new_v_w_down': 'new_v'}


def _forward(args):
    return _fwd_reference(*[args[k] for k in FWD_PARAMS])


def _output_shape():
    out = _jax.eval_shape(lambda: _forward(_fwd_setup_inputs(0)))
    return out.shape, out.dtype

N_MICROBATCH = 1
ADAM_LR = 0.001
ADAM_B1 = 0.9
ADAM_B2 = 0.999
ADAM_EPS = 1e-08
ADAM_WD = 0.01
ADAM_STEP = 10
PER_EXAMPLE_BATCH_AXIS = {'x': 0, 'loss_target': 0}
SHARED_INPUTS = []
_WEIGHT_DTYPES = {'mix_norm_w': _jnp.float32, 'w_in': _jnp.float32, 'q_norm_w': _jnp.float32, 'k_norm_w': _jnp.float32, 'hgrn_lb_logits': _jnp.float32, 'hgrn_gnorm_w': _jnp.float32, 'conv_dw_w': _jnp.float32, 'conv_dw_b': _jnp.float32, 'conv_ln_w': _jnp.float32, 'conv_ln_b': _jnp.float32, 'conv_pw_w': _jnp.float32, 'conv_pw_b': _jnp.float32, 'attn_out_norm_w': _jnp.float32, 'conv_out_norm_w': _jnp.float32, 'w_out': _jnp.float32, 'ffn_norm_w': _jnp.float32, 'w_gate': _jnp.float32, 'w_up': _jnp.float32, 'w_down': _jnp.float32}
MOMENT_SCALE = {'mix_norm_w': 1.052646e+01, 'w_in': 5.934892e+00, 'q_norm_w': 5.139027e+00, 'k_norm_w': 5.919691e+00, 'hgrn_lb_logits': 1.078433e-01, 'hgrn_gnorm_w': 4.587032e+01, 'conv_dw_w': 2.633006e+00, 'conv_dw_b': 3.838037e+01, 'conv_ln_w': 1.452286e+01, 'conv_ln_b': 2.007988e+01, 'conv_pw_w': 9.731278e+00, 'conv_pw_b': 4.166102e+01, 'attn_out_norm_w': 3.826758e+01, 'conv_out_norm_w': 3.302794e+01, 'w_out': 1.075763e+01, 'ffn_norm_w': 2.502165e+01, 'w_gate': 1.146686e+00, 'w_up': 1.267306e+00, 'w_down': 1.958058e+00}


def _to_microbatches(a, axis):
    t = _jnp.moveaxis(a, axis, 0)
    t = t.reshape((N_MICROBATCH, t.shape[0] // N_MICROBATCH) + t.shape[1:])
    return _jnp.moveaxis(t, 1, axis + 1)


def setup_inputs(seed: int = 0) -> dict:
    inp = _fwd_setup_inputs(seed)
    key = _jax.random.fold_in(_jax.random.key(seed), 7919)
    shape, _ = _output_shape()
    out = dict(inp)
    out["loss_target"] = _jax.random.normal(_jax.random.fold_in(key, 0), shape, _jnp.float32)
    for i, name in enumerate(TWIN_WEIGHTS):
        w = inp[name].astype(_jnp.float32)
        if MOMENT_SCALE is None:
            s = _jnp.sqrt(_jnp.mean(_jnp.square(w)) + 1e-30)
        else:
            s = MOMENT_SCALE[name]
        km, kv = _jax.random.split(_jax.random.fold_in(key, i + 1))
        out[name] = w
        out["m_" + name] = s * _jax.random.normal(km, w.shape, _jnp.float32)
        out["v_" + name] = (s * s) * _jax.random.uniform(kv, w.shape, _jnp.float32, 0.5, 1.5)
    if N_MICROBATCH > 1:
        for name, axis in PER_EXAMPLE_BATCH_AXIS.items():
            out[name] = _to_microbatches(out[name], axis)
    return {'x': out['x'], 'mix_norm_w': out['mix_norm_w'], 'w_in': out['w_in'], 'q_norm_w': out['q_norm_w'], 'k_norm_w': out['k_norm_w'], 'hgrn_lb_logits': out['hgrn_lb_logits'], 'hgrn_gnorm_w': out['hgrn_gnorm_w'], 'conv_dw_w': out['conv_dw_w'], 'conv_dw_b': out['conv_dw_b'], 'conv_ln_w': out['conv_ln_w'], 'conv_ln_b': out['conv_ln_b'], 'conv_pw_w': out['conv_pw_w'], 'conv_pw_b': out['conv_pw_b'], 'attn_out_norm_w': out['attn_out_norm_w'], 'conv_out_norm_w': out['conv_out_norm_w'], 'w_out': out['w_out'], 'ffn_norm_w': out['ffn_norm_w'], 'w_gate': out['w_gate'], 'w_up': out['w_up'], 'w_down': out['w_down'], 'loss_target': out['loss_target'], 'm_mix_norm_w': out['m_mix_norm_w'], 'm_w_in': out['m_w_in'], 'm_q_norm_w': out['m_q_norm_w'], 'm_k_norm_w': out['m_k_norm_w'], 'm_hgrn_lb_logits': out['m_hgrn_lb_logits'], 'm_hgrn_gnorm_w': out['m_hgrn_gnorm_w'], 'm_conv_dw_w': out['m_conv_dw_w'], 'm_conv_dw_b': out['m_conv_dw_b'], 'm_conv_ln_w': out['m_conv_ln_w'], 'm_conv_ln_b': out['m_conv_ln_b'], 'm_conv_pw_w': out['m_conv_pw_w'], 'm_conv_pw_b': out['m_conv_pw_b'], 'm_attn_out_norm_w': out['m_attn_out_norm_w'], 'm_conv_out_norm_w': out['m_conv_out_norm_w'], 'm_w_out': out['m_w_out'], 'm_ffn_norm_w': out['m_ffn_norm_w'], 'm_w_gate': out['m_w_gate'], 'm_w_up': out['m_w_up'], 'm_w_down': out['m_w_down'], 'v_mix_norm_w': out['v_mix_norm_w'], 'v_w_in': out['v_w_in'], 'v_q_norm_w': out['v_q_norm_w'], 'v_k_norm_w': out['v_k_norm_w'], 'v_hgrn_lb_logits': out['v_hgrn_lb_logits'], 'v_hgrn_gnorm_w': out['v_hgrn_gnorm_w'], 'v_conv_dw_w': out['v_conv_dw_w'], 'v_conv_dw_b': out['v_conv_dw_b'], 'v_conv_ln_w': out['v_conv_ln_w'], 'v_conv_ln_b': out['v_conv_ln_b'], 'v_conv_pw_w': out['v_conv_pw_w'], 'v_conv_pw_b': out['v_conv_pw_b'], 'v_attn_out_norm_w': out['v_attn_out_norm_w'], 'v_conv_out_norm_w': out['v_conv_out_norm_w'], 'v_w_out': out['v_w_out'], 'v_ffn_norm_w': out['v_ffn_norm_w'], 'v_w_gate': out['v_w_gate'], 'v_w_up': out['v_w_up'], 'v_w_down': out['v_w_down']}


def _loss(weights, diff, rest, loss_target):
    with _jax.named_scope("forward"):
        args = {**rest, TWIN_DIFF_INPUT: diff, **{k: w.astype(_WEIGHT_DTYPES[k]) for k, w in weights.items()}}
        y = _forward(args)
    with _jax.named_scope("loss_head"):
        err = _jnp.square(y.astype(_jnp.float32) - loss_target)
        return 0.5 * _jnp.sum(_jnp.mean(err, axis=-1)) if err.ndim else 0.5 * err


def _adamw(w, g, m, v):
    m = ADAM_B1 * m + (1.0 - ADAM_B1) * g
    v = ADAM_B2 * v + (1.0 - ADAM_B2) * _jnp.square(g)
    m_hat = m / (1.0 - ADAM_B1 ** ADAM_STEP)
    v_hat = v / (1.0 - ADAM_B2 ** ADAM_STEP)
    delta = -ADAM_LR * (m_hat / (_jnp.sqrt(v_hat) + ADAM_EPS) + ADAM_WD * w)
    return delta, m, v


def reference(x, mix_norm_w, w_in, q_norm_w, k_norm_w, hgrn_lb_logits, hgrn_gnorm_w, conv_dw_w, conv_dw_b, conv_ln_w, conv_ln_b, conv_pw_w, conv_pw_b, attn_out_norm_w, conv_out_norm_w, w_out, ffn_norm_w, w_gate, w_up, w_down, loss_target, m_mix_norm_w, m_w_in, m_q_norm_w, m_k_norm_w, m_hgrn_lb_logits, m_hgrn_gnorm_w, m_conv_dw_w, m_conv_dw_b, m_conv_ln_w, m_conv_ln_b, m_conv_pw_w, m_conv_pw_b, m_attn_out_norm_w, m_conv_out_norm_w, m_w_out, m_ffn_norm_w, m_w_gate, m_w_up, m_w_down, v_mix_norm_w, v_w_in, v_q_norm_w, v_k_norm_w, v_hgrn_lb_logits, v_hgrn_gnorm_w, v_conv_dw_w, v_conv_dw_b, v_conv_ln_w, v_conv_ln_b, v_conv_pw_w, v_conv_pw_b, v_attn_out_norm_w, v_conv_out_norm_w, v_w_out, v_ffn_norm_w, v_w_gate, v_w_up, v_w_down):
    given = dict(x=x, mix_norm_w=mix_norm_w, w_in=w_in, q_norm_w=q_norm_w, k_norm_w=k_norm_w, hgrn_lb_logits=hgrn_lb_logits, hgrn_gnorm_w=hgrn_gnorm_w, conv_dw_w=conv_dw_w, conv_dw_b=conv_dw_b, conv_ln_w=conv_ln_w, conv_ln_b=conv_ln_b, conv_pw_w=conv_pw_w, conv_pw_b=conv_pw_b, attn_out_norm_w=attn_out_norm_w, conv_out_norm_w=conv_out_norm_w, w_out=w_out, ffn_norm_w=ffn_norm_w, w_gate=w_gate, w_up=w_up, w_down=w_down, loss_target=loss_target, m_mix_norm_w=m_mix_norm_w, m_w_in=m_w_in, m_q_norm_w=m_q_norm_w, m_k_norm_w=m_k_norm_w, m_hgrn_lb_logits=m_hgrn_lb_logits, m_hgrn_gnorm_w=m_hgrn_gnorm_w, m_conv_dw_w=m_conv_dw_w, m_conv_dw_b=m_conv_dw_b, m_conv_ln_w=m_conv_ln_w, m_conv_ln_b=m_conv_ln_b, m_conv_pw_w=m_conv_pw_w, m_conv_pw_b=m_conv_pw_b, m_attn_out_norm_w=m_attn_out_norm_w, m_conv_out_norm_w=m_conv_out_norm_w, m_w_out=m_w_out, m_ffn_norm_w=m_ffn_norm_w, m_w_gate=m_w_gate, m_w_up=m_w_up, m_w_down=m_w_down, v_mix_norm_w=v_mix_norm_w, v_w_in=v_w_in, v_q_norm_w=v_q_norm_w, v_k_norm_w=v_k_norm_w, v_hgrn_lb_logits=v_hgrn_lb_logits, v_hgrn_gnorm_w=v_hgrn_gnorm_w, v_conv_dw_w=v_conv_dw_w, v_conv_dw_b=v_conv_dw_b, v_conv_ln_w=v_conv_ln_w, v_conv_ln_b=v_conv_ln_b, v_conv_pw_w=v_conv_pw_w, v_conv_pw_b=v_conv_pw_b, v_attn_out_norm_w=v_attn_out_norm_w, v_conv_out_norm_w=v_conv_out_norm_w, v_w_out=v_w_out, v_ffn_norm_w=v_ffn_norm_w, v_w_gate=v_w_gate, v_w_up=v_w_up, v_w_down=v_w_down)
    weights = {n: given[n] for n in TWIN_WEIGHTS}
    shared = {n: given[n] for n in SHARED_INPUTS}
    per_example = {n: given[n] for n in ['x']}
    grad_fn = _jax.value_and_grad(_loss, argnums=(0, 1))

    def one_microbatch(ex, loss_target):
        ex = dict(ex)
        diff = ex.pop(TWIN_DIFF_INPUT)
        return grad_fn(weights, diff, {**shared, **ex}, loss_target)

    if N_MICROBATCH == 1:
        loss, (grad_w, grad_x) = one_microbatch(per_example, given["loss_target"])
    else:
        def body(carry, xs):
            loss_sum, grad_sum = carry
            l_k, (gw_k, gx_k) = one_microbatch(xs[0], xs[1])
            with _jax.named_scope("update"):
                return (loss_sum + l_k, _jax.tree.map(_jnp.add, grad_sum, gw_k)), gx_k

        init = (_jnp.zeros((), _jnp.float32), _jax.tree.map(_jnp.zeros_like, weights))
        (loss, grad_w), grad_x = _jax.lax.scan(body, init, (per_example, given["loss_target"]))
    with _jax.named_scope("update"):
        delta_w, new_m, new_v = {}, {}, {}
        for n in TWIN_WEIGHTS:
            delta_w[n], new_m[n], new_v[n] = _adamw(weights[n], grad_w[n], given["m_" + n], given["v_" + n])
    return (loss, grad_x, *[grad_w[n] for n in TWIN_WEIGHTS], *[delta_w[n] for n in TWIN_WEIGHTS],
            *[new_m[n] for n in TWIN_WEIGHTS], *[new_v[n] for n in TWIN_WEIGHTS])
```

```python
import functools

import numpy as np
import jax
import jax.numpy as jnp
from jax import lax
from jax.experimental import pallas as pl
from jax.experimental.pallas import tpu as pltpu

F32, BF16 = jnp.float32, jnp.bfloat16

D_MODEL = 1024
DEPTH = 2
GRID_W = 64
D_ATTN, D_HGRN, D_CONV = 512, 256, 256
HEAD_DIM = 64
N_KV = 2
KV_LANES = D_ATTN // N_KV
ROPE_THETA = 10000.0
F_MIN = 1e-6
CONV_W = 31
CONV_PAD = 15
D_FF = 2816
D_IN = 2560
N_CHIP = 4
N_DEV = 8
IN_BLK = D_IN // N_CHIP
FF_BLK = D_FF // N_CHIP
OUT_BLK = D_MODEL // N_CHIP
EPS = 1e-6
LN_EPS = 1e-5
LR, B1, B2, ADAM_EPS, WD, STEP = 0.001, 0.9, 0.999, 1e-08, 0.01, 10
CHUNK = 16
HBLK = 256
CONV_TILE = 128
VMEM_LIMIT = 56 * 1024 * 1024

COL_Q, COL_K, COL_V = 0, 4, 5
COL_HQ, COL_FF, COL_FB, COL_HI, COL_HG, COL_CA, COL_CB = 3, 4, 5, 6, 7, 8, 9


def _params(sem=None):
    return pltpu.CompilerParams(dimension_semantics=sem, vmem_limit_bytes=VMEM_LIMIT)


def _sds(shape, dtype):
    return jax.ShapeDtypeStruct(tuple(shape), dtype)


def _full(shape):
    n = len(shape)
    return pl.BlockSpec(tuple(shape), lambda *_: (0,) * n)


def _sigmoid(x):
    return 1.0 / (1.0 + jnp.exp(-x))


def _silu(x):
    return x * _sigmoid(x)


def _dsilu(x):
    s = _sigmoid(x)
    return s * (1.0 + x * (1.0 - s))


def _rowgroups(v):
    m, c = v.shape
    return v.reshape(m // 8, 8, c).sum(axis=0)


def _split2(x):
    hi = x.astype(BF16)
    lo = (x - hi.astype(F32)).astype(BF16)
    return hi, lo


def _rdot2(x, m):
    hi, lo = _split2(x)
    return (jnp.dot(hi, m, preferred_element_type=F32) + jnp.dot(lo, m, preferred_element_type=F32))


def _ldot3(m, x):
    hi = x.astype(BF16)
    r1 = x - hi.astype(F32)
    mid = r1.astype(BF16)
    lo = (r1 - mid.astype(F32)).astype(BF16)
    return (jnp.dot(m, hi, preferred_element_type=F32) + jnp.dot(m, mid, preferred_element_type=F32)
            + jnp.dot(m, lo, preferred_element_type=F32))


def _dot_nt(a, b):
    return lax.dot_general(a, b, (((1,), (1,)), ((), ())), preferred_element_type=F32)


def _dot_tn(a, b):
    return lax.dot_general(a, b, (((0,), (0,)), ((), ())), preferred_element_type=F32)


def _seg_matrix(n, seg, val):
    i = np.arange(n)
    return ((i[:, None] // seg) == (i[None, :] // seg)).astype(np.float32) * val


def _rot_matrix(n):
    r = np.zeros((n, n), np.float32)
    for i in range(n):
        if (i % 32) < 16:
            r[i + 16, i] = -1.0
        else:
            r[i - 16, i] = 1.0
    return r


def _rep_matrix():
    r = np.zeros((N_KV * HEAD_DIM, D_ATTN), np.float32)
    for kv in range(N_KV):
        for g in range(KV_LANES // HEAD_DIM):
            for d in range(HEAD_DIM):
                r[HEAD_DIM * kv + d, KV_LANES * kv + HEAD_DIM * g + d] = 1.0
    return r


def _cumsum_matrix(rev):
    i = np.arange(HBLK)
    same = (i[:, None] // CHUNK) == (i[None, :] // CHUNK)
    tri = (i[None, :] >= i[:, None]) if rev else (i[None, :] <= i[:, None])
    return (same & tri).astype(np.float32)


def _sel_matrices():
    sel = np.zeros((CHUNK, CHUNK * CHUNK), np.float32)
    selt = np.zeros((CHUNK, CHUNK * CHUNK), np.float32)
    for t in range(CHUNK):
        for s in range(CHUNK):
            sel[t, t * CHUNK + s] = 1.0
            selt[s, t * CHUNK + s] = 1.0
    return sel, selt


def _bf(a):
    return jnp.asarray(a, dtype=BF16)


def _mm(name, grid, pairs, extras, outs, epilogue, acc=None, sem=None):
    n_p, n_e, n_o = len(pairs), len(extras), len(outs)

    def body(*refs):
        ab = refs[:2 * n_p]
        ex = refs[2 * n_p:2 * n_p + n_e]
        out = refs[2 * n_p + n_e:2 * n_p + n_e + n_o]
        scr = refs[2 * n_p + n_e + n_o:]
        tot = None
        for i in range(n_p):
            a = ab[2 * i][...]
            b = ab[2 * i + 1][...]
            if a.ndim == 3:
                a = a.reshape(-1, a.shape[-1])
            if b.ndim == 3:
                b = b.reshape(-1, b.shape[-1])
            r = lax.dot_general(a.astype(BF16), b.astype(BF16), pairs[i][4], preferred_element_type=F32)
            tot = r if tot is None else tot + r

        def finish(total):
            res = epilogue(total, *[e[...] for e in ex])
            for o_ref, val in zip(out, res):
                o_ref[...] = val.astype(o_ref.dtype)

        if acc is None:
            finish(tot)
        else:
            k = pl.program_id(acc[0])

            @pl.when(k == 0)
            def _():
                scr[0][...] = tot

            @pl.when(k > 0)
            def _():
                scr[0][...] += tot

            @pl.when(k == grid[acc[0]] - 1)
            def _():
                finish(scr[0][...])

    args, in_specs = [], []
    for a, a_spec, b, b_spec, _ in pairs:
        args += [a, b]
        in_specs += [a_spec, b_spec]
    for e, e_spec in extras:
        args.append(e)
        in_specs.append(e_spec)
    if sem is None:
        sem = tuple("arbitrary" if (acc is not None and i == acc[0]) else "parallel" for i in range(len(grid)))
    return pl.pallas_call(
        body, name=name, grid=grid, in_specs=in_specs,
        out_specs=[o[1] for o in outs], out_shape=[o[0] for o in outs],
        scratch_shapes=[] if acc is None else [pltpu.VMEM(acc[1], F32)],
        compiler_params=_params(sem),
    )(*args)


NN = (((1,), (0,)), ((), ()))
NT = (((1,), (1,)), ((), ()))
TN = (((0,), (0,)), ((), ()))


def _row_tile(t):
    return min(256, t)


def _rms_fwd(name, x, w):
    t, d = x.shape
    tm = _row_tile(t)

    def body(x_ref, w_ref, o_ref):
        xv = x_ref[...]
        r = lax.rsqrt(jnp.mean(xv * xv, axis=-1, keepdims=True) + EPS)
        o_ref[...] = (xv * r * w_ref[...]).astype(BF16)

    return pl.pallas_call(
        body, name=name, grid=(t // tm,),
        in_specs=[pl.BlockSpec((tm, d), lambda i: (i, 0)), _full((1, d))],
        out_specs=pl.BlockSpec((tm, d), lambda i: (i, 0)), out_shape=_sds((t, d), BF16),
        compiler_params=_params(("parallel",)),
    )(x, w)


def _rms_bwd(name, x, w, dh, dres):
    t, d = x.shape
    tm = _row_tile(t)

    def body(x_ref, w_ref, dh_ref, dres_ref, dx_ref, dw_ref):
        xv = x_ref[...]
        r = lax.rsqrt(jnp.mean(xv * xv, axis=-1, keepdims=True) + EPS)
        dy = dh_ref[...]
        gw = dy * w_ref[...]
        dx_ref[...] = dres_ref[...] + r * gw - xv * (r * r * r) * jnp.mean(gw * xv, axis=-1, keepdims=True)

        @pl.when(pl.program_id(0) == 0)
        def _():
            dw_ref[...] = jnp.zeros_like(dw_ref)

        dw_ref[...] += _rowgroups(dy * xv * r)

    tile = pl.BlockSpec((tm, d), lambda i: (i, 0))
    return pl.pallas_call(
        body, name=name, grid=(t // tm,),
        in_specs=[tile, _full((1, d)), tile, tile],
        out_specs=[tile, _full((8, d))], out_shape=[_sds((t, d), F32), _sds((8, d), F32)],
        compiler_params=_params(("arbitrary",)),
    )(x, w, dh, dres)


def _loss_kernel(y, target):
    t, d = y.shape
    tm = _row_tile(t)

    def body(y_ref, t_ref, dy_ref, acc_ref):
        e = y_ref[...] - t_ref[...]
        dy_ref[...] = e * (1.0 / d)

        @pl.when(pl.program_id(0) == 0)
        def _():
            acc_ref[...] = jnp.zeros_like(acc_ref)

        acc_ref[...] += _rowgroups(e * e)

    tile = pl.BlockSpec((tm, d), lambda i: (i, 0))
    return pl.pallas_call(
        body, name="loss_head", grid=(t // tm,), in_specs=[tile, tile],
        out_specs=[tile, _full((8, d))], out_shape=[_sds((t, d), F32), _sds((8, d), F32)],
        compiler_params=_params(("arbitrary",)),
    )(y, target)


def _rope_tables(s):
    rows = s // GRID_W
    row_id = jnp.repeat(jnp.arange(rows, dtype=F32), GRID_W)
    col_id = jnp.tile(jnp.arange(GRID_W, dtype=F32), rows)
    half = HEAD_DIM // 2
    inv_freq = ROPE_THETA ** (-jnp.arange(0, half, 2, dtype=F32) / half)
    ang_r = row_id[:, None] * inv_freq[None, :]
    ang_c = col_id[:, None] * inv_freq[None, :]
    ang = jnp.concatenate([ang_r, ang_r, ang_c, ang_c], axis=-1)
    return jnp.cos(ang).astype(F32), jnp.sin(ang).astype(F32)


def _attn_consts():
    return dict(
        seg_q=_bf(_seg_matrix(D_ATTN, HEAD_DIM, 1.0 / HEAD_DIM)),
        seg_k=_bf(_seg_matrix(N_KV * HEAD_DIM, HEAD_DIM, 1.0 / HEAD_DIM)),
        rot_q=_bf(_rot_matrix(D_ATTN)), rot_k=_bf(_rot_matrix(N_KV * HEAD_DIM)),
        rep=_bf(_rep_matrix()), rep_t=_bf(_rep_matrix().T))


def _attn_prep(name, proj, s, tabs, qw, kw, ac):
    t = proj.shape[0]
    tm = _row_tile(s)
    nst = s // tm
    kw_ = N_KV * HEAD_DIM

    def body(q_ref, k_ref, v_ref, cq_ref, sq_ref, ck_ref, sk_ref, qw_ref, kw_ref,
             segq_ref, segk_ref, rotq_ref, rotk_ref, rep_ref, qn_ref, kr_ref, vr_ref):
        q = q_ref[...]
        r = lax.rsqrt(jnp.dot((q * q).astype(BF16), segq_ref[...], preferred_element_type=F32) + EPS)
        qn = q * r * qw_ref[...]
        qr = qn * cq_ref[...] + _rdot2(qn, rotq_ref[...]) * sq_ref[...]
        qn_ref[...] = (qr * (HEAD_DIM ** -0.5)).astype(BF16)
        k = k_ref[...]
        rk = lax.rsqrt(jnp.dot((k * k).astype(BF16), segk_ref[...], preferred_element_type=F32) + EPS)
        kn = k * rk * kw_ref[...]
        kr = kn * ck_ref[...] + _rdot2(kn, rotk_ref[...]) * sk_ref[...]
        kr_ref[...] = jnp.dot(kr.astype(BF16), rep_ref[...], preferred_element_type=F32).astype(BF16)
        vr_ref[...] = jnp.dot(v_ref[...].astype(BF16), rep_ref[...], preferred_element_type=F32).astype(BF16)

    wide = pl.BlockSpec((tm, D_ATTN), lambda i: (i, 0))
    tabq = pl.BlockSpec((tm, D_ATTN), lambda i: (i % nst, 0))
    tabk = pl.BlockSpec((tm, kw_), lambda i: (i % nst, 0))
    return pl.pallas_call(
        body, name=name, grid=(t // tm,),
        in_specs=[pl.BlockSpec((tm, D_ATTN), lambda i: (i, COL_Q)), pl.BlockSpec((tm, kw_), lambda i: (i, COL_K)),
                  pl.BlockSpec((tm, kw_), lambda i: (i, COL_V)), tabq, tabq, tabk, tabk,
                  _full((1, D_ATTN)), _full((1, kw_)), _full((D_ATTN, D_ATTN)), _full((kw_, kw_)),
                  _full((D_ATTN, D_ATTN)), _full((kw_, kw_)), _full((kw_, D_ATTN))],
        out_specs=[wide, wide, wide], out_shape=[_sds((t, D_ATTN), BF16)] * 3,
        compiler_params=_params(("parallel",)),
    )(proj, proj, proj, tabs["cq"], tabs["sq"], tabs["ck"], tabs["sk"], qw, kw,
      ac["seg_q"], ac["seg_k"], ac["rot_q"], ac["rot_k"], ac["rep"])


def _attn_prep_bwd(name, proj, s, tabs, qw, kw, ac, dqs, dkr, dvr):
    t = proj.shape[0]
    tm = _row_tile(s)
    nst = s // tm
    kw_ = N_KV * HEAD_DIM
    wout = D_ATTN + 2 * kw_

    def norm_rope_bwd(x, w, cos, sin, seg, rot, d_roped):
        dn = d_roped * cos - _rdot2(d_roped * sin, rot)
        r = lax.rsqrt(jnp.dot((x * x).astype(BF16), seg, preferred_element_type=F32) + EPS)
        gw = dn * w
        dx = r * gw - x * (r * r * r) * _rdot2(gw * x, seg)
        return dx, _rowgroups(dn * x * r)

    def body(q_ref, k_ref, cq_ref, sq_ref, ck_ref, sk_ref, qw_ref, kw_ref, segq_ref, segk_ref, rotq_ref, rotk_ref,
             rept_ref, dqs_ref, dkr_ref, dvr_ref, dp_ref, dqw_ref, dkw_ref):
        dq, dqw = norm_rope_bwd(q_ref[...], qw_ref[...], cq_ref[...], sq_ref[...], segq_ref[...], rotq_ref[...],
                                dqs_ref[...] * (HEAD_DIM ** -0.5))
        dk_roped = _rdot2(dkr_ref[...], rept_ref[...])
        dk, dkw = norm_rope_bwd(k_ref[...], kw_ref[...], ck_ref[...], sk_ref[...], segk_ref[...], rotk_ref[...],
                                dk_roped)
        dv = _rdot2(dvr_ref[...], rept_ref[...])
        dp_ref[:, 0:D_ATTN] = dq.astype(BF16)
        dp_ref[:, D_ATTN:D_ATTN + kw_] = dk.astype(BF16)
        dp_ref[:, D_ATTN + kw_:wout] = dv.astype(BF16)

        @pl.when(pl.program_id(0) == 0)
        def _():
            dqw_ref[...] = jnp.zeros_like(dqw_ref)
            dkw_ref[...] = jnp.zeros_like(dkw_ref)

        dqw_ref[...] += dqw
        dkw_ref[...] += dkw

    wide = pl.BlockSpec((tm, D_ATTN), lambda i: (i, 0))
    tabq = pl.BlockSpec((tm, D_ATTN), lambda i: (i % nst, 0))
    tabk = pl.BlockSpec((tm, kw_), lambda i: (i % nst, 0))
    return pl.pallas_call(
        body, name=name, grid=(t // tm,),
        in_specs=[pl.BlockSpec((tm, D_ATTN), lambda i: (i, COL_Q)), pl.BlockSpec((tm, kw_), lambda i: (i, COL_K)),
                  tabq, tabq, tabk, tabk, _full((1, D_ATTN)), _full((1, kw_)),
                  _full((D_ATTN, D_ATTN)), _full((kw_, kw_)), _full((D_ATTN, D_ATTN)), _full((kw_, kw_)),
                  _full((D_ATTN, kw_)), wide, wide, wide],
        out_specs=[pl.BlockSpec((tm, wout), lambda i: (i, 0)), _full((8, D_ATTN)), _full((8, kw_))],
        out_shape=[_sds((t, wout), BF16), _sds((8, D_ATTN), F32), _sds((8, kw_), F32)],
        compiler_params=_params(("arbitrary",)),
    )(proj, proj, tabs["cq"], tabs["sq"], tabs["ck"], tabs["sk"], qw, kw,
      ac["seg_q"], ac["seg_k"], ac["rot_q"], ac["rot_k"], ac["rep_t"], dqs, dkr, dvr)


def _attn_tile(s):
    return min(256, s)


def _head_masks(shape):
    lane = lax.broadcasted_iota(jnp.int32, shape, 1)
    return [(lane // HEAD_DIM) == g for g in range(KV_LANES // HEAD_DIM)]


def _attn_fwd(name, qn, kr, vr, b, s):
    t = qn.shape[0]
    tq = _attn_tile(s)
    nq = s // tq

    def body(q_ref, k_ref, v_ref, o_ref):
        q = q_ref[...]
        k = k_ref[...]
        v = v_ref[...]
        acc = jnp.zeros((tq, KV_LANES), F32)
        for mask in _head_masks((tq, KV_LANES)):
            sc = _dot_nt(jnp.where(mask, q, jnp.zeros_like(q)), k)
            p = jnp.exp(sc - jnp.max(sc, axis=-1, keepdims=True))
            inv = 1.0 / jnp.sum(p, axis=-1, keepdims=True)
            og = jnp.dot(p.astype(BF16), v, preferred_element_type=F32) * inv
            acc = jnp.where(mask, og, acc)
        o_ref[...] = acc

    return pl.pallas_call(
        body, name=name, grid=(b, N_KV, nq),
        in_specs=[pl.BlockSpec((tq, KV_LANES), lambda bi, kv, i: (bi * nq + i, kv)),
                  pl.BlockSpec((s, KV_LANES), lambda bi, kv, i: (bi, kv)),
                  pl.BlockSpec((s, KV_LANES), lambda bi, kv, i: (bi, kv))],
        out_specs=pl.BlockSpec((tq, KV_LANES), lambda bi, kv, i: (bi * nq + i, kv)),
        out_shape=_sds((t, D_ATTN), F32),
        compiler_params=_params(("parallel", "parallel", "parallel")),
    )(qn, kr, vr)


def _attn_bwd(name, qn, kr, vr, do, b, s):
    t = qn.shape[0]
    tq = _attn_tile(s)
    nq = s // tq

    def body(q_ref, k_ref, v_ref, do_ref, dq_ref, dk_ref, dv_ref):
        @pl.when(pl.program_id(2) == 0)
        def _():
            dk_ref[...] = jnp.zeros_like(dk_ref)
            dv_ref[...] = jnp.zeros_like(dv_ref)

        q = q_ref[...]
        k = k_ref[...]
        v = v_ref[...]
        dout = do_ref[...].astype(BF16)
        dq = jnp.zeros((tq, KV_LANES), F32)
        for mask in _head_masks((tq, KV_LANES)):
            qg = jnp.where(mask, q, jnp.zeros_like(q))
            dog = jnp.where(mask, dout, jnp.zeros_like(dout))
            sc = _dot_nt(qg, k)
            p = jnp.exp(sc - jnp.max(sc, axis=-1, keepdims=True))
            p = p * (1.0 / jnp.sum(p, axis=-1, keepdims=True))
            dp = _dot_nt(dog, v)
            ds = (p * (dp - jnp.sum(p * dp, axis=-1, keepdims=True))).astype(BF16)
            dq = jnp.where(mask, jnp.dot(ds, k, preferred_element_type=F32), dq)
            dk_ref[...] += _dot_tn(ds, qg)
            dv_ref[...] += _dot_tn(p.astype(BF16), dog)
        dq_ref[...] = dq

    qspec = pl.BlockSpec((tq, KV_LANES), lambda bi, kv, i: (bi * nq + i, kv))
    kspec = pl.BlockSpec((s, KV_LANES), lambda bi, kv, i: (bi, kv))
    return pl.pallas_call(
        body, name=name, grid=(b, N_KV, nq),
        in_specs=[qspec, kspec, kspec, qspec],
        out_specs=[qspec, kspec, kspec], out_shape=[_sds((t, D_ATTN), F32)] * 3,
        compiler_params=_params(("parallel", "parallel", "arbitrary")),
    )(qn, kr, vr, do)


def _hgrn_consts(rev):
    sel, selt = _sel_matrices()
    cs = _cumsum_matrix(rev)
    return dict(cs=_bf(cs), cs_t=_bf(cs.T), seg=_bf(_seg_matrix(D_HGRN, HEAD_DIM, 1.0)),
                bd=jnp.asarray(_seg_matrix(D_HGRN, HEAD_DIM, 1.0), F32),
                sel=_bf(sel), selt=_bf(selt), seld=_bf(sel - selt))


def _gates(z, lb):
    sig = _sigmoid(z)
    f = lb + (1.0 - lb) * sig
    g = jnp.log(jnp.maximum(f, F_MIN))
    sn = _sigmoid(-z)
    return sig, f, g, sn, (1.0 - lb) * sn


def _pair_decay(b, rev):
    row = lax.broadcasted_iota(jnp.int32, (CHUNK, D_HGRN), 0)
    parts = []
    for t in range(CHUNK):
        m = (row >= t) if rev else (row <= t)
        parts.append(jnp.where(m, jnp.exp(jnp.minimum(b[t:t + 1, :] - b, 0.0)), 0.0))
    return jnp.concatenate(parts, axis=0)


def _rows_rep(a):
    return jnp.concatenate([jnp.broadcast_to(a[t:t + 1, :], a.shape) for t in range(CHUNK)], axis=0)


def _tile_rows(a):
    return jnp.concatenate([a] * CHUNK, axis=0)


def _hgrn_specs(b, s, rev):
    nb = s // HBLK

    def blk(j):
        return (nb - 1 - j) if rev else j

    def col(c):
        return pl.BlockSpec((HBLK, D_HGRN), lambda bi, j: (bi * nb + blk(j), c))

    return nb, blk, col


def _hgrn_fwd(name, proj, lb, b, s, rev, hc):
    t = proj.shape[0]
    nb, blk, col = _hgrn_specs(b, s, rev)
    n_ch = HBLK // CHUNK
    last = 0 if rev else CHUNK - 1

    def body(q_ref, z_ref, v_ref, lb_ref, cs_ref, seg_ref, bd_ref, sel_ref, o_ref, st_ref, state, b_scr, k_scr):
        @pl.when(pl.program_id(1) == 0)
        def _():
            state[...] = jnp.zeros_like(state)

        st_ref[...] = state[...]
        _, _, g, _, kk = _gates(z_ref[...], lb_ref[...])
        k_scr[...] = kk
        b_scr[...] = _ldot3(cs_ref[...], g)

        def chunk(i, carry):
            c = (n_ch - 1 - i) if rev else i
            rows = pl.ds(pl.multiple_of(c * CHUNK, CHUNK), CHUNK)
            q = q_ref[rows, :]
            k = k_scr[rows, :]
            v = v_ref[rows, :]
            bb = b_scr[rows, :]
            bl = bb[last:last + 1, :]
            pairs = _pair_decay(bb, rev) * _rows_rep(q) * _tile_rows(k)
            a = jnp.dot(pairs.astype(BF16), seg_ref[...], preferred_element_type=F32)
            o_intra = jnp.dot(sel_ref[...], (a * _tile_rows(v)).astype(BF16), preferred_element_type=F32)
            st = state[...]
            o_inter = _dot_nt((q * jnp.exp(bb)).astype(BF16), st.astype(BF16))
            o_ref[rows, :] = o_intra + o_inter
            ke = k * jnp.exp(bl - bb)
            state[...] = st * jnp.exp(bl) + bd_ref[...] * _dot_tn(v.astype(BF16), ke.astype(BF16))
            return carry

        lax.fori_loop(0, n_ch, chunk, 0)

    sq = (D_HGRN, D_HGRN)
    return pl.pallas_call(
        body, name=name, grid=(b, nb),
        in_specs=[col(COL_HQ), col(COL_FB if rev else COL_FF), col(COL_HI), _full((1, D_HGRN)),
                  _full((HBLK, HBLK)), _full(sq), _full(sq), _full((CHUNK, CHUNK * CHUNK))],
        out_specs=[pl.BlockSpec((HBLK, D_HGRN), lambda bi, j: (bi * nb + blk(j), 0)),
                   pl.BlockSpec((None,) + sq, lambda bi, j: (bi * nb + blk(j), 0, 0))],
        out_shape=[_sds((t, D_HGRN), F32), _sds((b * nb,) + sq, F32)],
        scratch_shapes=[pltpu.VMEM(sq, F32), pltpu.VMEM((HBLK, D_HGRN), F32), pltpu.VMEM((HBLK, D_HGRN), F32)],
        compiler_params=_params(("parallel", "arbitrary")),
    )(proj, proj, proj, lb, hc["cs"], hc["seg"], hc["bd"], hc["sel"])


def _hgrn_bwd(name, proj, lb, st_blk, do, dq_prev, dv_prev, b, s, rev, hc):
    t = proj.shape[0]
    nb = s // HBLK
    n_ch = HBLK // CHUNK
    last = 0 if rev else CHUNK - 1

    def blk(j):
        return j if rev else (nb - 1 - j)

    def col(c):
        return pl.BlockSpec((HBLK, D_HGRN), lambda bi, j: (bi * nb + blk(j), c))

    def body(q_ref, z_ref, v_ref, lb_ref, st_ref, do_ref, dqp_ref, dvp_ref, cs_ref, cst_ref, seg_ref, bd_ref,
             sel_ref, selt_ref, seld_ref, dq_ref, dv_ref, dz_ref, dlb_ref,
             dstate, states, b_scr, k_scr, db_scr, dk_scr):
        first = jnp.logical_and(pl.program_id(0) == 0, pl.program_id(1) == 0)

        @pl.when(first)
        def _():
            dlb_ref[...] = jnp.zeros_like(dlb_ref)

        @pl.when(pl.program_id(1) == 0)
        def _():
            dstate[...] = jnp.zeros_like(dstate)

        lbv = lb_ref[...]
        z = z_ref[...]
        sig, f, g, sn, kk = _gates(z, lbv)
        k_scr[...] = kk
        b_scr[...] = _ldot3(cs_ref[...], g)

        def rows_of(c):
            return pl.ds(pl.multiple_of(c * CHUNK, CHUNK), CHUNK)

        def replay(i, st):
            c = (n_ch - 1 - i) if rev else i
            rows = rows_of(c)
            states[c] = st
            bb = b_scr[rows, :]
            bl = bb[last:last + 1, :]
            ke = k_scr[rows, :] * jnp.exp(bl - bb)
            return st * jnp.exp(bl) + bd_ref[...] * _dot_tn(v_ref[rows, :].astype(BF16), ke.astype(BF16))

        lax.fori_loop(0, n_ch, replay, st_ref[...])
        row = lax.broadcasted_iota(jnp.int32, (CHUNK, D_HGRN), 0)

        def chunk(i, carry):
            c = i if rev else (n_ch - 1 - i)
            rows = rows_of(c)
            q = q_ref[rows, :]
            k = k_scr[rows, :]
            v = v_ref[rows, :]
            bb = b_scr[rows, :]
            dout = do_ref[rows, :]
            bl = bb[last:last + 1, :]
            st_p = states[c]
            dst_n = dstate[...]
            eb = jnp.exp(bb)
            ebl = jnp.exp(bl - bb)
            ebl_last = jnp.exp(bl)
            qe = q * eb
            ke = k * ebl
            dob = dout.astype(BF16)
            dstb = dst_n.astype(BF16)
            dqe = jnp.dot(dob, st_p.astype(BF16), preferred_element_type=F32)
            dke = jnp.dot(v.astype(BF16), dstb, preferred_element_type=F32)
            dv = _dot_nt(ke.astype(BF16), dstb)
            dbl = jnp.sum(dst_n * st_p, axis=0, keepdims=True) * ebl_last + jnp.sum(dke * ke, axis=0, keepdims=True)
            dq = dqe * eb
            dk = dke * ebl
            db = dqe * qe - dke * ke
            dec = _pair_decay(bb, rev)
            q_rep = _rows_rep(q)
            k_til = _tile_rows(k)
            do_rep = _rows_rep(dout)
            pairs = dec * q_rep * k_til
            a = jnp.dot(pairs.astype(BF16), seg_ref[...], preferred_element_type=F32)
            wb = jnp.dot((_tile_rows(v) * do_rep).astype(BF16), seg_ref[...], preferred_element_type=F32)
            gdec = wb * dec
            dq = dq + jnp.dot(sel_ref[...], (gdec * k_til).astype(BF16), preferred_element_type=F32)
            dk = dk + jnp.dot(selt_ref[...], (gdec * q_rep).astype(BF16), preferred_element_type=F32)
            dv = dv + jnp.dot(selt_ref[...], (a * do_rep).astype(BF16), preferred_element_type=F32)
            db = db + jnp.dot(seld_ref[...], (wb * pairs).astype(BF16), preferred_element_type=F32)
            db = db + jnp.where(row == last, dbl, 0.0)
            dq_ref[rows, :] = dq + dqp_ref[rows, :]
            dv_ref[rows, :] = dv + dvp_ref[rows, :]
            dk_scr[rows, :] = dk
            db_scr[rows, :] = db
            dstate[...] = dst_n * ebl_last + bd_ref[...] * _dot_tn(dob, qe.astype(BF16))
            return carry

        lax.fori_loop(0, n_ch, chunk, 0)
        hi, lo = _split2(db_scr[...])
        dg = (jnp.dot(cst_ref[...], hi, preferred_element_type=F32)
              + jnp.dot(cst_ref[...], lo, preferred_element_type=F32))
        dgf = jnp.where(f > F_MIN, dg / f, 0.0)
        dk = dk_scr[...]
        dz_ref[...] = dgf * (1.0 - lbv) * sig * (1.0 - sig) - dk * (1.0 - lbv) * sn * (1.0 - sn)
        dlb_ref[...] += _rowgroups(dgf * (1.0 - sig) - dk * sn)

    sq = (D_HGRN, D_HGRN)
    blk0 = pl.BlockSpec((HBLK, D_HGRN), lambda bi, j: (bi * nb + blk(j), 0))
    pairs_shape = (CHUNK, CHUNK * CHUNK)
    return pl.pallas_call(
        body, name=name, grid=(b, nb),
        in_specs=[col(COL_HQ), col(COL_FB if rev else COL_FF), col(COL_HI), _full((1, D_HGRN)),
                  pl.BlockSpec((None,) + sq, lambda bi, j: (bi * nb + blk(j), 0, 0)), blk0, blk0, blk0,
                  _full((HBLK, HBLK)), _full((HBLK, HBLK)), _full(sq), _full(sq),
                  _full(pairs_shape), _full(pairs_shape), _full(pairs_shape)],
        out_specs=[blk0, blk0, blk0, _full((8, D_HGRN))],
        out_shape=[_sds((t, D_HGRN), F32)] * 3 + [_sds((8, D_HGRN), F32)],
        scratch_shapes=[pltpu.VMEM(sq, F32), pltpu.VMEM((n_ch,) + sq, F32)] + [pltpu.VMEM((HBLK, D_HGRN), F32)] * 4,
        compiler_params=_params(("arbitrary", "arbitrary")),
    )(proj, proj, proj, lb, st_blk, do, dq_prev, dv_prev,
      hc["cs"], hc["cs_t"], hc["seg"], hc["bd"], hc["sel"], hc["selt"], hc["seld"])


def _lower_bounds(logits):
    n = logits.shape[1]

    def body(x_ref, o_ref):
        x = x_ref[...]
        for d in range(2):
            rows = [x[l * 2 + d:l * 2 + d + 1, :] for l in range(DEPTH)]
            mx = functools.reduce(jnp.maximum, rows)
            ex = [jnp.exp(r - mx) for r in rows]
            tot = functools.reduce(lambda a, c: a + c, ex)
            sm = [e / tot for e in ex]
            run = jnp.zeros_like(sm[0])
            for l in range(DEPTH):
                run = run + sm[l]
                o_ref[l * 2 + d:l * 2 + d + 1, :] = run - sm[0]

    return pl.pallas_call(body, name="hgrn_lower_bounds", out_shape=_sds(logits.shape, F32),
                          in_specs=[_full(logits.shape)], out_specs=_full(logits.shape), grid=(1,),
                          compiler_params=_params(("arbitrary",)))(logits)


def _lower_bounds_bwd(logits, dlb):
    def body(x_ref, g_ref, o_ref):
        x = x_ref[...]
        gv = g_ref[...]
        for d in range(2):
            rows = [x[l * 2 + d:l * 2 + d + 1, :] for l in range(DEPTH)]
            gr = [gv[l * 2 + d:l * 2 + d + 1, :] for l in range(DEPTH)]
            mx = functools.reduce(jnp.maximum, rows)
            ex = [jnp.exp(r - mx) for r in rows]
            tot = functools.reduce(lambda a, c: a + c, ex)
            sm = [e / tot for e in ex]
            dsm = []
            for i in range(DEPTH):
                acc = functools.reduce(lambda a, c: a + c, gr[i:])
                if i == 0:
                    acc = acc - functools.reduce(lambda a, c: a + c, gr)
                dsm.append(acc)
            inner = functools.reduce(lambda a, c: a + c, [sm[i] * dsm[i] for i in range(DEPTH)])
            for i in range(DEPTH):
                o_ref[i * 2 + d:i * 2 + d + 1, :] = sm[i] * (dsm[i] - inner)

    return pl.pallas_call(body, name="hgrn_lower_bounds_bwd", out_shape=_sds(logits.shape, F32),
                          in_specs=[_full(logits.shape), _full(logits.shape)], out_specs=_full(logits.shape),
                          grid=(1,), compiler_params=_params(("arbitrary",)))(logits, dlb)


def _conv_rows(s):
    return s + 2 * (CONV_PAD + 1)


def _conv_fwd(name, proj, dw_w, dw_b, ln_w, ln_b, pw_w, pw_b, b, s):
    t = proj.shape[0]
    pad = CONV_PAD + 1
    nt = s // CONV_TILE

    def body(a_ref, g_ref, w_ref, dwb_ref, lnw_ref, lnb_ref, pw_ref, pwb_ref, y_ref, upad, win):
        upad[0:pad, :] = jnp.zeros((pad, D_CONV), F32)
        upad[s + pad:s + 2 * pad, :] = jnp.zeros((pad, D_CONV), F32)

        def fill(i, carry):
            rows = pl.ds(pl.multiple_of(i * CONV_TILE, CONV_TILE), CONV_TILE)
            upad[pl.ds(pl.multiple_of(i * CONV_TILE + pad, pad), CONV_TILE), :] = a_ref[rows, :] * _sigmoid(g_ref[rows, :])
            return carry

        lax.fori_loop(0, nt, fill, 0)

        def tile(i, carry):
            r0 = pl.multiple_of(i * CONV_TILE, CONV_TILE)
            win[...] = upad[pl.ds(r0, CONV_TILE + 2 * pad), :]
            acc = jnp.zeros((CONV_TILE, D_CONV), F32)
            for j in range(CONV_W):
                acc = acc + win[j + 1:j + 1 + CONV_TILE, :] * w_ref[j:j + 1, :]
            c = acc + dwb_ref[...]
            mu = jnp.mean(c, axis=-1, keepdims=True)
            xc = c - mu
            rstd = lax.rsqrt(jnp.mean(xc * xc, axis=-1, keepdims=True) + LN_EPS)
            n = xc * rstd * lnw_ref[...] + lnb_ref[...]
            y_ref[pl.ds(r0, CONV_TILE), :] = (jnp.dot(_silu(n).astype(BF16), pw_ref[...].astype(BF16),
                                                      preferred_element_type=F32) + pwb_ref[...])
            return carry

        lax.fori_loop(0, nt, tile, 0)

    vec = _full((1, D_CONV))
    return pl.pallas_call(
        body, name=name, grid=(b,),
        in_specs=[pl.BlockSpec((s, D_CONV), lambda bi: (bi, COL_CA)), pl.BlockSpec((s, D_CONV), lambda bi: (bi, COL_CB)),
                  _full((CONV_W + 1, D_CONV)), vec, vec, vec, _full((D_CONV, D_CONV)), vec],
        out_specs=pl.BlockSpec((s, D_CONV), lambda bi: (bi, 0)), out_shape=_sds((t, D_CONV), F32),
        scratch_shapes=[pltpu.VMEM((_conv_rows(s), D_CONV), F32), pltpu.VMEM((CONV_TILE + 2 * pad, D_CONV), F32)],
        compiler_params=_params(("parallel",)),
    )(proj, proj, dw_w, dw_b, ln_w, ln_b, pw_w, pw_b)


def _conv_bwd(name, proj, dw_w, dw_b, ln_w, ln_b, pw_w, dy, b, s):
    t = proj.shape[0]
    pad = CONV_PAD + 1
    nt = s // CONV_TILE

    def body(a_ref, g_ref, w_ref, dwb_ref, lnw_ref, lnb_ref, pw_ref, dy_ref, dab_ref, dpw_ref, ddw_ref, dvec_ref,
             upad, dcpad, tap_acc, win, dwin):
        @pl.when(pl.program_id(0) == 0)
        def _():
            dpw_ref[...] = jnp.zeros_like(dpw_ref)
            ddw_ref[...] = jnp.zeros_like(ddw_ref)
            dvec_ref[...] = jnp.zeros_like(dvec_ref)

        zeros = jnp.zeros((pad, D_CONV), F32)
        upad[0:pad, :] = zeros
        upad[s + pad:s + 2 * pad, :] = zeros
        dcpad[0:pad, :] = zeros
        dcpad[s + pad:s + 2 * pad, :] = zeros
        tap_acc[...] = jnp.zeros_like(tap_acc)

        def inner(i):
            return pl.ds(pl.multiple_of(i * CONV_TILE + pad, pad), CONV_TILE)

        def fill(i, carry):
            rows = pl.ds(pl.multiple_of(i * CONV_TILE, CONV_TILE), CONV_TILE)
            upad[inner(i), :] = a_ref[rows, :] * _sigmoid(g_ref[rows, :])
            return carry

        lax.fori_loop(0, nt, fill, 0)

        def tile_a(i, carry):
            r0 = pl.multiple_of(i * CONV_TILE, CONV_TILE)
            win[...] = upad[pl.ds(r0, CONV_TILE + 2 * pad), :]
            acc = jnp.zeros((CONV_TILE, D_CONV), F32)
            for j in range(CONV_W):
                acc = acc + win[j + 1:j + 1 + CONV_TILE, :] * w_ref[j:j + 1, :]
            c = acc + dwb_ref[...]
            mu = jnp.mean(c, axis=-1, keepdims=True)
            xc = c - mu
            rstd = lax.rsqrt(jnp.mean(xc * xc, axis=-1, keepdims=True) + LN_EPS)
            xhat = xc * rstd
            n = xhat * lnw_ref[...] + lnb_ref[...]
            dyt = dy_ref[pl.ds(r0, CONV_TILE), :]
            dyb = dyt.astype(BF16)
            dpw_ref[...] += _dot_tn(_silu(n).astype(BF16), dyb)
            dn = _dot_nt(dyb, pw_ref[...].astype(BF16)) * _dsilu(n)
            dxh = dn * lnw_ref[...]
            dc = rstd * (dxh - jnp.mean(dxh, axis=-1, keepdims=True)
                         - xhat * jnp.mean(dxh * xhat, axis=-1, keepdims=True))
            dcpad[inner(i), :] = dc
            dvec_ref[0:1, :] += jnp.sum(dyt, axis=0, keepdims=True)
            dvec_ref[1:2, :] += jnp.sum(dn * xhat, axis=0, keepdims=True)
            dvec_ref[2:3, :] += jnp.sum(dn, axis=0, keepdims=True)
            dvec_ref[3:4, :] += jnp.sum(dc, axis=0, keepdims=True)
            return carry

        lax.fori_loop(0, nt, tile_a, 0)

        def tile_b(i, carry):
            r0 = pl.multiple_of(i * CONV_TILE, CONV_TILE)
            win[...] = upad[pl.ds(r0, CONV_TILE + 2 * pad), :]
            dwin[...] = dcpad[pl.ds(r0, CONV_TILE + 2 * pad), :]
            dct = dwin[pad:pad + CONV_TILE, :]
            du = jnp.zeros((CONV_TILE, D_CONV), F32)
            for j in range(CONV_W):
                du = du + dwin[2 * pad - 1 - j:2 * pad - 1 - j + CONV_TILE, :] * w_ref[j:j + 1, :]
                tap_acc[8 * j:8 * j + 8, :] += _rowgroups(dct * win[j + 1:j + 1 + CONV_TILE, :])
            rows = pl.ds(r0, CONV_TILE)
            sg = _sigmoid(g_ref[rows, :])
            dab_ref[rows, 0:D_CONV] = (du * sg).astype(BF16)
            dab_ref[rows, D_CONV:2 * D_CONV] = (du * a_ref[rows, :] * sg * (1.0 - sg)).astype(BF16)
            return carry

        lax.fori_loop(0, nt, tile_b, 0)
        for j in range(CONV_W):
            ddw_ref[j:j + 1, :] += jnp.sum(tap_acc[8 * j:8 * j + 8, :], axis=0, keepdims=True)

    vec = _full((1, D_CONV))
    return pl.pallas_call(
        body, name=name, grid=(b,),
        in_specs=[pl.BlockSpec((s, D_CONV), lambda bi: (bi, COL_CA)), pl.BlockSpec((s, D_CONV), lambda bi: (bi, COL_CB)),
                  _full((CONV_W + 1, D_CONV)), vec, vec, vec, _full((D_CONV, D_CONV)),
                  pl.BlockSpec((s, D_CONV), lambda bi: (bi, 0))],
        out_specs=[pl.BlockSpec((s, 2 * D_CONV), lambda bi: (bi, 0)), _full((D_CONV, D_CONV)),
                   _full((CONV_W + 1, D_CONV)), _full((8, D_CONV))],
        out_shape=[_sds((t, 2 * D_CONV), BF16), _sds((D_CONV, D_CONV), F32), _sds((CONV_W + 1, D_CONV), F32),
                   _sds((8, D_CONV), F32)],
        scratch_shapes=[pltpu.VMEM((_conv_rows(s), D_CONV), F32), pltpu.VMEM((_conv_rows(s), D_CONV), F32),
                        pltpu.VMEM((8 * CONV_W, D_CONV), F32), pltpu.VMEM((CONV_TILE + 2 * pad, D_CONV), F32),
                        pltpu.VMEM((CONV_TILE + 2 * pad, D_CONV), F32)],
        compiler_params=_params(("arbitrary",)),
    )(proj, proj, dw_w, dw_b, ln_w, ln_b, pw_w, dy)


def _mix_fwd(name, y_attn, o_fw, o_bw, proj, y_conv, aw, gw, cw, seg):
    t = y_attn.shape[0]
    tm = _row_tile(t)

    def body(ya_ref, of_ref, ob_ref, hg_ref, yc_ref, aw_ref, gw_ref, cw_ref, seg_ref, o_ref):
        ya = ya_ref[...]
        ra = lax.rsqrt(jnp.mean(ya * ya, axis=-1, keepdims=True) + EPS)
        o_ref[:, 0:D_ATTN] = (ya * ra * aw_ref[...]).astype(BF16)
        o = of_ref[...] + ob_ref[...]
        ro = lax.rsqrt(jnp.dot((o * o).astype(BF16), seg_ref[...], preferred_element_type=F32) + EPS)
        o_ref[:, D_ATTN:D_ATTN + D_HGRN] = (o * ro * gw_ref[...] * _silu(hg_ref[...])).astype(BF16)
        yc = yc_ref[...]
        rc = lax.rsqrt(jnp.mean(yc * yc, axis=-1, keepdims=True) + EPS)
        o_ref[:, D_ATTN + D_HGRN:D_MODEL] = (yc * rc * cw_ref[...]).astype(BF16)

    def tile(w, c=0):
        return pl.BlockSpec((tm, w), lambda i: (i, c))

    return pl.pallas_call(
        body, name=name, grid=(t // tm,),
        in_specs=[tile(D_ATTN), tile(D_HGRN), tile(D_HGRN), tile(D_HGRN, COL_HG), tile(D_CONV),
                  _full((1, D_ATTN)), _full((1, D_HGRN)), _full((1, D_CONV)), _full((D_HGRN, D_HGRN))],
        out_specs=tile(D_MODEL), out_shape=_sds((t, D_MODEL), BF16),
        compiler_params=_params(("parallel",)),
    )(y_attn, o_fw, o_bw, proj, y_conv, aw, gw, cw, seg)


def _mix_bwd(name, dmix, y_attn, o_fw, o_bw, proj, y_conv, aw, gw, cw, seg):
    t = y_attn.shape[0]
    tm = _row_tile(t)

    def rms_bwd(x, w, dy):
        r = lax.rsqrt(jnp.mean(x * x, axis=-1, keepdims=True) + EPS)
        gwv = dy * w
        return r * gwv - x * (r * r * r) * jnp.mean(gwv * x, axis=-1, keepdims=True), _rowgroups(dy * x * r)

    def body(dm_ref, ya_ref, of_ref, ob_ref, hg_ref, yc_ref, aw_ref, gw_ref, cw_ref, seg_ref,
             dya_ref, do_ref, dhg_ref, dyc_ref, daw_ref, dgw_ref, dcw_ref):
        @pl.when(pl.program_id(0) == 0)
        def _():
            daw_ref[...] = jnp.zeros_like(daw_ref)
            dgw_ref[...] = jnp.zeros_like(dgw_ref)
            dcw_ref[...] = jnp.zeros_like(dcw_ref)

        dya, daw = rms_bwd(ya_ref[...], aw_ref[...], dm_ref[:, 0:D_ATTN])
        dya_ref[...] = dya
        daw_ref[...] += daw
        dyc, dcw = rms_bwd(yc_ref[...], cw_ref[...], dm_ref[:, D_ATTN + D_HGRN:D_MODEL])
        dyc_ref[...] = dyc
        dcw_ref[...] += dcw
        d2 = dm_ref[:, D_ATTN:D_ATTN + D_HGRN]
        o = of_ref[...] + ob_ref[...]
        hg = hg_ref[...]
        ro = lax.rsqrt(jnp.dot((o * o).astype(BF16), seg_ref[...], preferred_element_type=F32) + EPS)
        dn = d2 * _silu(hg)
        dhg_ref[...] = (d2 * o * ro * gw_ref[...] * _dsilu(hg)).astype(BF16)
        gwv = dn * gw_ref[...]
        do_ref[...] = ro * gwv - o * (ro * ro * ro) * _rdot2(gwv * o, seg_ref[...])
        dgw_ref[...] += _rowgroups(dn * o * ro)

    def tile(w, c=0):
        return pl.BlockSpec((tm, w), lambda i: (i, c))

    return pl.pallas_call(
        body, name=name, grid=(t // tm,),
        in_specs=[tile(D_MODEL), tile(D_ATTN), tile(D_HGRN), tile(D_HGRN), tile(D_HGRN, COL_HG), tile(D_CONV),
                  _full((1, D_ATTN)), _full((1, D_HGRN)), _full((1, D_CONV)), _full((D_HGRN, D_HGRN))],
        out_specs=[tile(D_ATTN), tile(D_HGRN), tile(D_HGRN), tile(D_CONV),
                   _full((8, D_ATTN)), _full((8, D_HGRN)), _full((8, D_CONV))],
        out_shape=[_sds((t, D_ATTN), F32), _sds((t, D_HGRN), F32), _sds((t, D_HGRN), BF16), _sds((t, D_CONV), F32),
                   _sds((8, D_ATTN), F32), _sds((8, D_HGRN), F32), _sds((8, D_CONV), F32)],
        compiler_params=_params(("arbitrary",)),
    )(dmix, y_attn, o_fw, o_bw, proj, y_conv, aw, gw, cw, seg)


def _hgrn_dproj(name, dq, dz_fw, dz_bw, dv, dhg):
    t = dq.shape[0]
    tm = _row_tile(t)

    def body(a_ref, b_ref, c_ref, d_ref, e_ref, o_ref):
        for i, r in enumerate((a_ref, b_ref, c_ref, d_ref, e_ref)):
            o_ref[:, i * D_HGRN:(i + 1) * D_HGRN] = r[...].astype(BF16)

    tile = pl.BlockSpec((tm, D_HGRN), lambda i: (i, 0))
    return pl.pallas_call(
        body, name=name, grid=(t // tm,), in_specs=[tile] * 5,
        out_specs=pl.BlockSpec((tm, 5 * D_HGRN), lambda i: (i, 0)), out_shape=_sds((t, 5 * D_HGRN), BF16),
        compiler_params=_params(("parallel",)),
    )(dq, dz_fw, dz_bw, dv, dhg)


def _mm_tile(t):
    return min(512, t)


def _w_blk(rows, cols, l, j_of):
    return pl.BlockSpec((None, None, rows, cols), lambda *g: (j_of(*g), l, 0, 0))


def _layer_fwd(l, x, wt, sm, tabs, cst, b, s):
    t = x.shape[0]
    tm = _mm_tile(t)
    nt = t // tm
    pre = "l%d_" % l
    row = lambda w: pl.BlockSpec((tm, w), lambda i, *_: (i, 0))

    h1 = _rms_fwd(pre + "mix_norm", x, sm["mix_norm_w"][l])
    (proj,) = _mm(pre + "in_proj", (nt, N_CHIP),
                  [(h1, row(D_MODEL), wt["w_in"], _w_blk(D_MODEL, IN_BLK, l, lambda i, j: j), NN)], [],
                  [(_sds((t, D_IN), F32), pl.BlockSpec((tm, IN_BLK), lambda i, j: (i, j)))],
                  lambda tot: (tot,))
    qn, kr, vr = _attn_prep(pre + "attn_prep", proj, s, tabs, sm["q_norm_w"][l], sm["k_norm_w"][l], cst["attn"])
    y_attn = _attn_fwd(pre + "attn", qn, kr, vr, b, s)
    o_fw, st_fw = _hgrn_fwd(pre + "hgrn_fw", proj, sm["lb"][l][0], b, s, False, cst["hg_fw"])
    o_bw, st_bw = _hgrn_fwd(pre + "hgrn_bw", proj, sm["lb"][l][1], b, s, True, cst["hg_bw"])
    y_conv = _conv_fwd(pre + "conv", proj, sm["conv_dw_w"][l], sm["conv_dw_b"][l], sm["conv_ln_w"][l],
                       sm["conv_ln_b"][l], sm["conv_pw_w"][l], sm["conv_pw_b"][l], b, s)
    mixed = _mix_fwd(pre + "mix", y_attn, o_fw, o_bw, proj, y_conv, sm["attn_out_norm_w"][l], sm["gnorm_w"][l],
                     sm["conv_out_norm_w"][l], cst["seg_h"])
    (x1,) = _mm(pre + "out_proj", (nt,),
                [(mixed, row(D_MODEL), wt["w_out"],
                  pl.BlockSpec((N_CHIP, None, OUT_BLK, D_MODEL), lambda i: (0, l, 0, 0)), NN)],
                [(x, row(D_MODEL))], [(_sds((t, D_MODEL), F32), row(D_MODEL))],
                lambda tot, xr: (xr + tot,))
    h2 = _rms_fwd(pre + "ffn_norm", x1, sm["ffn_norm_w"][l])
    ffb = pl.BlockSpec((None, tm, FF_BLK), lambda i, j: (j, i, 0))
    ffs = _sds((N_CHIP, t, FF_BLK), BF16)

    def gu_body(h_ref, wg_ref, wu_ref, g_ref, u_ref, a_ref):
        hv = h_ref[...]
        gv = jnp.dot(hv, wg_ref[...], preferred_element_type=F32)
        uv = jnp.dot(hv, wu_ref[...], preferred_element_type=F32)
        g_ref[...] = gv.astype(BF16)
        u_ref[...] = uv.astype(BF16)
        a_ref[...] = (_silu(gv) * uv).astype(BF16)

    gate, up, act = pl.pallas_call(
        gu_body, name=pre + "ffn_gate_up", grid=(nt, N_CHIP),
        in_specs=[row(D_MODEL), _w_blk(D_MODEL, FF_BLK, l, lambda i, j: j), _w_blk(D_MODEL, FF_BLK, l, lambda i, j: j)],
        out_specs=[ffb, ffb, ffb], out_shape=[ffs, ffs, ffs],
        compiler_params=_params(("parallel", "parallel")),
    )(h2, wt["w_gate"], wt["w_up"])
    (x2,) = _mm(pre + "ffn_down", (nt, N_CHIP),
                [(act, ffb, wt["w_down"], _w_blk(FF_BLK, D_MODEL, l, lambda i, j: j), NN)],
                [(x1, row(D_MODEL))], [(_sds((t, D_MODEL), F32), row(D_MODEL))],
                lambda tot, xr: (xr + tot,), acc=(1, (tm, D_MODEL)))
    saved = dict(x=x, h1=h1, proj=proj, qn=qn, kr=kr, vr=vr, y_attn=y_attn, o_fw=o_fw, o_bw=o_bw, st_fw=st_fw,
                 st_bw=st_bw, y_conv=y_conv, mixed=mixed, x1=x1, h2=h2, gate=gate, up=up, act=act)
    return x2, saved


def _layer_bwd(l, dx2, sv, wt, sm, tabs, cst, b, s):
    t = dx2.shape[0]
    tm = _mm_tile(t)
    nt = t // tm
    pre = "l%d_" % l
    row = lambda w: pl.BlockSpec((tm, w), lambda i, *_: (i, 0))
    ffb = pl.BlockSpec((None, tm, FF_BLK), lambda i, j: (j, i, 0))
    ffs = _sds((N_CHIP, t, FF_BLK), BF16)

    dgate, dup = _mm(pre + "ffn_down_dx", (nt, N_CHIP),
                     [(dx2, row(D_MODEL), wt["w_down"], _w_blk(FF_BLK, D_MODEL, l, lambda i, j: j), NT)],
                     [(sv["gate"], ffb), (sv["up"], ffb)], [(ffs, ffb), (ffs, ffb)],
                     lambda da, g, u: (da * u.astype(F32) * _dsilu(g.astype(F32)), da * _silu(g.astype(F32))))
    colt = lambda w: pl.BlockSpec((tm, w), lambda j, k: (k, 0))
    fft = pl.BlockSpec((None, tm, FF_BLK), lambda j, k: (j, k, 0))
    (g_down,) = _mm(pre + "ffn_down_dw", (N_CHIP, nt), [(sv["act"], fft, dx2, colt(D_MODEL), TN)], [],
                    [(_sds((N_CHIP, FF_BLK, D_MODEL), BF16), pl.BlockSpec((None, FF_BLK, D_MODEL), lambda j, k: (j, 0, 0)))],
                    lambda tot: (tot,), acc=(1, (FF_BLK, D_MODEL)))
    wff = pl.BlockSpec((None, D_MODEL, FF_BLK), lambda j, k: (j, 0, 0))
    (g_gate,) = _mm(pre + "ffn_gate_dw", (N_CHIP, nt), [(sv["h2"], colt(D_MODEL), dgate, fft, TN)], [],
                    [(_sds((N_CHIP, D_MODEL, FF_BLK), BF16), wff)], lambda tot: (tot,), acc=(1, (D_MODEL, FF_BLK)))
    (g_up,) = _mm(pre + "ffn_up_dw", (N_CHIP, nt), [(sv["h2"], colt(D_MODEL), dup, fft, TN)], [],
                  [(_sds((N_CHIP, D_MODEL, FF_BLK), BF16), wff)], lambda tot: (tot,), acc=(1, (D_MODEL, FF_BLK)))
    (dh2,) = _mm(pre + "ffn_dh", (nt, N_CHIP),
                 [(dgate, ffb, wt["w_gate"], _w_blk(D_MODEL, FF_BLK, l, lambda i, j: j), NT),
                  (dup, ffb, wt["w_up"], _w_blk(D_MODEL, FF_BLK, l, lambda i, j: j), NT)], [],
                 [(_sds((t, D_MODEL), F32), row(D_MODEL))], lambda tot: (tot,), acc=(1, (tm, D_MODEL)))
    dx1, d_ffn_norm = _rms_bwd(pre + "ffn_norm_bwd", sv["x1"], sm["ffn_norm_w"][l], dh2, dx2)

    (dmix,) = _mm(pre + "out_proj_dx", (nt,),
                  [(dx1, row(D_MODEL), wt["w_out"],
                    pl.BlockSpec((N_CHIP, None, OUT_BLK, D_MODEL), lambda i: (0, l, 0, 0)), NT)], [],
                  [(_sds((t, D_MODEL), F32), row(D_MODEL))], lambda tot: (tot,))
    (g_out,) = _mm(pre + "out_proj_dw", (N_CHIP, nt),
                   [(sv["mixed"], pl.BlockSpec((tm, OUT_BLK), lambda j, k: (k, j)), dx1, colt(D_MODEL), TN)], [],
                   [(_sds((N_CHIP, OUT_BLK, D_MODEL), BF16), pl.BlockSpec((None, OUT_BLK, D_MODEL), lambda j, k: (j, 0, 0)))],
                   lambda tot: (tot,), acc=(1, (OUT_BLK, D_MODEL)))
    proj = sv["proj"]
    dya, do_h, dhg, dyc, d_aw, d_gw, d_cw = _mix_bwd(
        pre + "mix_bwd", dmix, sv["y_attn"], sv["o_fw"], sv["o_bw"], proj, sv["y_conv"],
        sm["attn_out_norm_w"][l], sm["gnorm_w"][l], sm["conv_out_norm_w"][l], cst["seg_h"])
    dqs, dkr, dvr = _attn_bwd(pre + "attn_bwd", sv["qn"], sv["kr"], sv["vr"], dya, b, s)
    dp_attn, d_qw, d_kw = _attn_prep_bwd(pre + "attn_prep_bwd", proj, s, tabs, sm["q_norm_w"][l], sm["k_norm_w"][l],
                                         cst["attn"], dqs, dkr, dvr)
    zero = jnp.zeros((t, D_HGRN), F32)
    dq1, dv1, dz_fw, dlb_fw = _hgrn_bwd(pre + "hgrn_fw_bwd", proj, sm["lb"][l][0], sv["st_fw"], do_h, zero, zero,
                                        b, s, False, cst["hg_fw"])
    dq2, dv2, dz_bw, dlb_bw = _hgrn_bwd(pre + "hgrn_bw_bwd", proj, sm["lb"][l][1], sv["st_bw"], do_h, dq1, dv1,
                                        b, s, True, cst["hg_bw"])
    dp_hgrn = _hgrn_dproj(pre + "hgrn_dproj", dq2, dz_fw, dz_bw, dv2, dhg)
    dp_conv, d_pw, d_dw, d_cvec = _conv_bwd(pre + "conv_bwd", proj, sm["conv_dw_w"][l], sm["conv_dw_b"][l],
                                            sm["conv_ln_w"][l], sm["conv_ln_b"][l], sm["conv_pw_w"][l], dyc, b, s)
    dproj = jnp.concatenate([dp_attn, dp_hgrn, dp_conv], axis=1)

    (g_in,) = _mm(pre + "in_proj_dw", (N_CHIP, nt),
                  [(sv["h1"], colt(D_MODEL), dproj, pl.BlockSpec((tm, IN_BLK), lambda j, k: (k, j)), TN)], [],
                  [(_sds((N_CHIP, D_MODEL, IN_BLK), BF16), pl.BlockSpec((None, D_MODEL, IN_BLK), lambda j, k: (j, 0, 0)))],
                  lambda tot: (tot,), acc=(1, (D_MODEL, IN_BLK)))
    (dh1,) = _mm(pre + "in_proj_dx", (nt, N_CHIP),
                 [(dproj, pl.BlockSpec((tm, IN_BLK), lambda i, j: (i, j)), wt["w_in"],
                   _w_blk(D_MODEL, IN_BLK, l, lambda i, j: j), NT)], [],
                 [(_sds((t, D_MODEL), F32), row(D_MODEL))], lambda tot: (tot,), acc=(1, (tm, D_MODEL)))
    dx, d_mix_norm = _rms_bwd(pre + "mix_norm_bwd", sv["x"], sm["mix_norm_w"][l], dh1, dx1)

    big = dict(w_in=g_in, w_out=g_out, w_gate=g_gate, w_up=g_up, w_down=g_down)
    heads = lambda v, n: v.sum(axis=0).reshape(n, HEAD_DIM).sum(axis=0)
    small = dict(
        mix_norm_w=d_mix_norm.sum(axis=0), q_norm_w=heads(d_qw, D_ATTN // HEAD_DIM), k_norm_w=heads(d_kw, N_KV),
        lb=jnp.stack([dlb_fw.sum(axis=0), dlb_bw.sum(axis=0)]), hgrn_gnorm_w=heads(d_gw, D_HGRN // HEAD_DIM),
        conv_dw_w=d_dw[:CONV_W], conv_dw_b=d_cvec[3], conv_ln_w=d_cvec[1], conv_ln_b=d_cvec[2], conv_pw_w=d_pw,
        conv_pw_b=d_cvec[0], attn_out_norm_w=d_aw.sum(axis=0), conv_out_norm_w=d_cw.sum(axis=0),
        ffn_norm_w=d_ffn_norm.sum(axis=0))
    return dx, big, small


SMALL_ORDER = ("mix_norm_w", "q_norm_w", "k_norm_w", "lb", "hgrn_gnorm_w", "conv_dw_w", "conv_dw_b", "conv_ln_w",
               "conv_ln_b", "conv_pw_w", "conv_pw_b", "attn_out_norm_w", "conv_out_norm_w", "ffn_norm_w")
BIG_ORDER = ("w_in", "w_out", "w_gate", "w_up", "w_down")


def _local_step(x, target, wt, sm):
    b, s, d = x.shape
    t = b * s
    cos, sin = _rope_tables(s)
    tabs = dict(cq=jnp.tile(cos, (1, D_ATTN // HEAD_DIM)), sq=jnp.tile(sin, (1, D_ATTN // HEAD_DIM)),
                ck=jnp.tile(cos, (1, N_KV)), sk=jnp.tile(sin, (1, N_KV)))
    cst = dict(attn=_attn_consts(), hg_fw=_hgrn_consts(False), hg_bw=_hgrn_consts(True),
               seg_h=_bf(_seg_matrix(D_HGRN, HEAD_DIM, 1.0 / HEAD_DIM)))
    vec = lambda a: a.reshape(DEPTH, 1, -1)
    smk = dict(sm)
    for n in ("mix_norm_w", "conv_dw_b", "conv_ln_w", "conv_ln_b", "conv_pw_b", "attn_out_norm_w", "conv_out_norm_w",
              "ffn_norm_w"):
        smk[n] = vec(sm[n])
    smk["q_norm_w"] = vec(jnp.tile(sm["q_norm_w"], (1, D_ATTN // HEAD_DIM)))
    smk["k_norm_w"] = vec(jnp.tile(sm["k_norm_w"], (1, N_KV)))
    smk["gnorm_w"] = vec(jnp.tile(sm["hgrn_gnorm_w"], (1, D_HGRN // HEAD_DIM)))
    smk["lb"] = sm["lb"].reshape(DEPTH, 2, 1, D_HGRN)
    smk["conv_dw_w"] = jnp.pad(sm["conv_dw_w"], ((0, 0), (0, 1), (0, 0)))

    h = x.reshape(t, d)
    saved = []
    for l in range(DEPTH):
        h, sv = _layer_fwd(l, h, wt, smk, tabs, cst, b, s)
        saved.append(sv)
    dy, sq = _loss_kernel(h, target.reshape(t, d))
    sq_sum = jnp.sum(sq)
    dh = dy
    bigs, smalls = [None] * DEPTH, [None] * DEPTH
    for l in reversed(range(DEPTH)):
        dh, bigs[l], smalls[l] = _layer_bwd(l, dh, saved[l], wt, smk, tabs, cst, b, s)
    return sq_sum, dh.reshape(b, s, d), bigs, smalls


HBM_SPEC = pl.BlockSpec(memory_space=pltpu.HBM)


def _exchange(name, arrs, mode):
    n = len(arrs)
    if mode == "gather8":
        flips = [(fx, fy, fc) for fx in (0, 1) for fy in (0, 1) for fc in (0, 1)][1:]
    elif mode == "sibling":
        flips = [(0, 0, 1)]
    else:
        flips = [(1, 0, 0), (0, 1, 0), (1, 1, 0)]
    n_f = len(flips)

    def body(*refs):
        ins, outs = refs[:n], refs[n:2 * n]
        send_sems, recv_sems, local_sems = refs[2 * n:]
        x, y, c = lax.axis_index("x"), lax.axis_index("y"), lax.axis_index("c")

        def slot_of(px, py, pc):
            return (2 * px + py) if mode != "gather8" else (4 * px + 2 * py + pc)

        me = slot_of(x, y, c)
        started = []
        for i in range(n):
            if mode != "sibling":
                src = ins[i].at[me] if mode == "scatter4" else ins[i]
                loc = pltpu.make_async_copy(src, outs[i].at[me], local_sems.at[i])
                loc.start()
                started.append(loc)
        sends, recvs = [], []
        for i in range(n):
            for f, (fx, fy, fc) in enumerate(flips):
                peer = (x ^ fx, y ^ fy, c ^ fc)
                ps = slot_of(*peer)
                if mode == "sibling":
                    src, dst, landed = ins[i], outs[i], outs[i]
                elif mode == "scatter4":
                    src, dst, landed = ins[i].at[ps], outs[i].at[me], outs[i].at[ps]
                else:
                    src, dst, landed = ins[i], outs[i].at[me], outs[i].at[ps]
                k = i * n_f + f
                cp = pltpu.make_async_remote_copy(src_ref=src, dst_ref=dst, send_sem=send_sems.at[k],
                                                  recv_sem=recv_sems.at[k], device_id=peer,
                                                  device_id_type=pl.DeviceIdType.MESH)
                cp.start()
                sends.append(cp)
                recvs.append(pltpu.make_async_remote_copy(src_ref=src, dst_ref=landed, send_sem=send_sems.at[k],
                                                          recv_sem=recv_sems.at[k], device_id=peer,
                                                          device_id_type=pl.DeviceIdType.MESH))
        for cp in sends:
            cp.wait_send()
        for cp in recvs:
            cp.wait_recv()
        for loc in started:
            loc.wait()

    def out_sds(a):
        if mode == "gather4":
            return _sds((N_CHIP,) + a.shape, a.dtype)
        if mode == "gather8":
            return _sds((N_DEV,) + a.shape, a.dtype)
        return _sds(a.shape, a.dtype)

    res = pl.pallas_call(
        body, name=name, in_specs=[HBM_SPEC] * n, out_specs=[HBM_SPEC] * n, out_shape=[out_sds(a) for a in arrs],
        scratch_shapes=[pltpu.SemaphoreType.DMA((n * n_f,)), pltpu.SemaphoreType.DMA((n * n_f,)),
                        pltpu.SemaphoreType.DMA((max(n, 1),))],
    )(*arrs)
    return list(res)


def _flat_tile(rows):
    for cand in (512, 256, 128, 64, 32, 16, 8):
        if rows % cand == 0:
            return cand
    return rows


def _cast_bf16(name, a):
    r, c = a.shape
    tr = _flat_tile(r)

    def body(a_ref, o_ref):
        o_ref[...] = a_ref[...].astype(BF16)

    spec = pl.BlockSpec((tr, c), lambda i: (i, 0))
    return pl.pallas_call(body, name=name, grid=(r // tr,), in_specs=[spec], out_specs=spec,
                          out_shape=_sds((r, c), BF16), compiler_params=_params(("parallel",)))(a)


def _sum_slots(name, a, scale=None):
    n, r, c = a.shape
    tr = _flat_tile(r)

    def body(a_ref, o_ref):
        tot = a_ref[0].astype(F32)
        for i in range(1, n):
            tot = tot + a_ref[i].astype(F32)
        o_ref[...] = tot

    return pl.pallas_call(body, name=name, grid=(r // tr,),
                          in_specs=[pl.BlockSpec((n, tr, c), lambda i: (0, i, 0))],
                          out_specs=pl.BlockSpec((tr, c), lambda i: (i, 0)), out_shape=_sds((r, c), F32),
                          compiler_params=_params(("parallel",)))(a)


def _adamw(name, w, ga, gb, m, v):
    r, c = w.shape
    tr = _flat_tile(r)
    c1 = 1.0 - B1 ** STEP
    c2 = 1.0 - B2 ** STEP
    two = gb is not None

    def body(*refs):
        if two:
            w_ref, ga_ref, gb_ref, m_ref, v_ref, g_out, d_out, m_out, v_out = refs
            g = ga_ref[...] + gb_ref[...]
        else:
            w_ref, ga_ref, m_ref, v_ref, g_out, d_out, m_out, v_out = refs
            g = ga_ref[...]
        mn = B1 * m_ref[...] + (1.0 - B1) * g
        vn = B2 * v_ref[...] + (1.0 - B2) * (g * g)
        g_out[...] = g
        m_out[...] = mn
        v_out[...] = vn
        d_out[...] = -LR * ((mn / c1) / (jnp.sqrt(vn / c2) + ADAM_EPS) + WD * w_ref[...])

    spec = pl.BlockSpec((tr, c), lambda i: (i, 0))
    ins = [w, ga, gb, m, v] if two else [w, ga, m, v]
    return pl.pallas_call(body, name=name, grid=(r // tr,), in_specs=[spec] * len(ins), out_specs=[spec] * 4,
                          out_shape=[_sds((r, c), F32)] * 4, compiler_params=_params(("parallel",)))(*ins)


WEIGHTS = ('mix_norm_w', 'w_in', 'q_norm_w', 'k_norm_w', 'hgrn_lb_logits', 'hgrn_gnorm_w', 'conv_dw_w', 'conv_dw_b',
           'conv_ln_w', 'conv_ln_b', 'conv_pw_w', 'conv_pw_b', 'attn_out_norm_w', 'conv_out_norm_w', 'w_out',
           'ffn_norm_w', 'w_gate', 'w_up', 'w_down')
SHARDED_SMALL = {"hgrn_lb_logits": 2, "conv_dw_w": 2, "conv_pw_w": 1}
LANES = 128


def _pack(parts):
    flat = jnp.concatenate([p.reshape(-1) for p in parts])
    n = flat.shape[0]
    rows = -(-n // (8 * LANES)) * 8
    return jnp.pad(flat, (0, rows * LANES - n)).reshape(rows, LANES)


def _unpack(packed, shapes):
    flat = packed.reshape(-1)
    out, off = [], 0
    for shp in shapes:
        n = int(np.prod(shp))
        out.append(flat[off:off + n].reshape(shp))
        off += n
    return out


def kernel(x, mix_norm_w, w_in, q_norm_w, k_norm_w, hgrn_lb_logits, hgrn_gnorm_w, conv_dw_w, conv_dw_b, conv_ln_w, conv_ln_b, conv_pw_w, conv_pw_b, attn_out_norm_w, conv_out_norm_w, w_out, ffn_norm_w, w_gate, w_up, w_down, loss_target, m_mix_norm_w, m_w_in, m_q_norm_w, m_k_norm_w, m_hgrn_lb_logits, m_hgrn_gnorm_w, m_conv_dw_w, m_conv_dw_b, m_conv_ln_w, m_conv_ln_b, m_conv_pw_w, m_conv_pw_b, m_attn_out_norm_w, m_conv_out_norm_w, m_w_out, m_ffn_norm_w, m_w_gate, m_w_up, m_w_down, v_mix_norm_w, v_w_in, v_q_norm_w, v_k_norm_w, v_hgrn_lb_logits, v_hgrn_gnorm_w, v_conv_dw_w, v_conv_dw_b, v_conv_ln_w, v_conv_ln_b, v_conv_pw_w, v_conv_pw_b, v_attn_out_norm_w, v_conv_out_norm_w, v_w_out, v_ffn_norm_w, v_w_gate, v_w_up, v_w_down):
    w = dict(mix_norm_w=mix_norm_w, w_in=w_in, q_norm_w=q_norm_w, k_norm_w=k_norm_w, hgrn_lb_logits=hgrn_lb_logits,
             hgrn_gnorm_w=hgrn_gnorm_w, conv_dw_w=conv_dw_w, conv_dw_b=conv_dw_b, conv_ln_w=conv_ln_w,
             conv_ln_b=conv_ln_b, conv_pw_w=conv_pw_w, conv_pw_b=conv_pw_b, attn_out_norm_w=attn_out_norm_w,
             conv_out_norm_w=conv_out_norm_w, w_out=w_out, ffn_norm_w=ffn_norm_w, w_gate=w_gate, w_up=w_up,
             w_down=w_down)
    m = dict(mix_norm_w=m_mix_norm_w, w_in=m_w_in, q_norm_w=m_q_norm_w, k_norm_w=m_k_norm_w,
             hgrn_lb_logits=m_hgrn_lb_logits, hgrn_gnorm_w=m_hgrn_gnorm_w, conv_dw_w=m_conv_dw_w,
             conv_dw_b=m_conv_dw_b, conv_ln_w=m_conv_ln_w, conv_ln_b=m_conv_ln_b, conv_pw_w=m_conv_pw_w,
             conv_pw_b=m_conv_pw_b, attn_out_norm_w=m_attn_out_norm_w, conv_out_norm_w=m_conv_out_norm_w,
             w_out=m_w_out, ffn_norm_w=m_ffn_norm_w, w_gate=m_w_gate, w_up=m_w_up, w_down=m_w_down)
    v = dict(mix_norm_w=v_mix_norm_w, w_in=v_w_in, q_norm_w=v_q_norm_w, k_norm_w=v_k_norm_w,
             hgrn_lb_logits=v_hgrn_lb_logits, hgrn_gnorm_w=v_hgrn_gnorm_w, conv_dw_w=v_conv_dw_w,
             conv_dw_b=v_conv_dw_b, conv_ln_w=v_conv_ln_w, conv_ln_b=v_conv_ln_b, conv_pw_w=v_conv_pw_w,
             conv_pw_b=v_conv_pw_b, attn_out_norm_w=v_attn_out_norm_w, conv_out_norm_w=v_conv_out_norm_w,
             w_out=v_w_out, ffn_norm_w=v_ffn_norm_w, w_gate=v_w_gate, w_up=v_w_up, w_down=v_w_down)
    chip = 2 * lax.axis_index("x") + lax.axis_index("y")

    flat2 = lambda a: a.reshape(-1, a.shape[-1])
    big_bf = [_cast_bf16("cast_" + n, flat2(w[n])).reshape(w[n].shape) for n in BIG_ORDER]
    small_pack = _pack([w[n] for n in SHARDED_SMALL])
    gathered = _exchange("gather_weights", big_bf + [small_pack], "gather4")
    wt = dict(zip(BIG_ORDER, gathered[:len(BIG_ORDER)]))
    parts = [_unpack(gathered[-1][j], [w[n].shape for n in SHARDED_SMALL]) for j in range(N_CHIP)]
    full_small = {n: jnp.concatenate([parts[j][i] for j in range(N_CHIP)], axis=ax)
                  for i, (n, ax) in enumerate(SHARDED_SMALL.items())}
    sm = {n: w[n] for n in WEIGHTS if n not in BIG_ORDER and n not in SHARDED_SMALL}
    sm["conv_dw_w"] = full_small["conv_dw_w"]
    sm["conv_pw_w"] = full_small["conv_pw_w"]
    logits = full_small["hgrn_lb_logits"].reshape(DEPTH * 2, D_HGRN)
    sm["lb"] = _lower_bounds(logits).reshape(DEPTH, 2, D_HGRN)

    sq_sum, grad_x, bigs, smalls = _local_step(x, loss_target, wt, sm)
    loss = lax.psum(0.5 * sq_sum / D_MODEL, ("x", "y", "c"))

    send = [jnp.stack([bigs[l][n] for l in range(DEPTH)], axis=1) for n in BIG_ORDER]
    landed = _exchange("scatter_grads", send, "scatter4")
    sums = [_sum_slots("sum_" + n, a.reshape(N_CHIP, -1, a.shape[-1])) for n, a in zip(BIG_ORDER, landed)]
    sib = _exchange("sibling_grads", sums, "sibling")
    out = {}
    for n, ga, gb in zip(BIG_ORDER, sums, sib):
        res = _adamw("adamw_" + n, flat2(w[n]), ga, gb, flat2(m[n]), flat2(v[n]))
        out[n] = [r.reshape(w[n].shape) for r in res]

    small_names = [n for n in WEIGHTS if n not in BIG_ORDER]
    g_pack = _pack([jnp.stack([smalls[l][n] for l in range(DEPTH)]) for n in SMALL_ORDER])
    g_all = _exchange("gather_small_grads", [g_pack], "gather8")[0]
    g_tot = _sum_slots("sum_small", g_all)
    shapes = [(DEPTH,) + tuple(smalls[0][n].shape) for n in SMALL_ORDER]
    g_small = dict(zip(SMALL_ORDER, _unpack(g_tot, shapes)))
    lb_shard = lax.dynamic_slice_in_dim(g_small.pop("lb").reshape(DEPTH * 2, D_HGRN), chip * HEAD_DIM, HEAD_DIM, 1)
    g_small["hgrn_lb_logits"] = _lower_bounds_bwd(hgrn_lb_logits.reshape(DEPTH * 2, HEAD_DIM), lb_shard).reshape(
        hgrn_lb_logits.shape)
    g_small["conv_dw_w"] = lax.dynamic_slice_in_dim(g_small["conv_dw_w"], chip * HEAD_DIM, HEAD_DIM, 2)
    g_small["conv_pw_w"] = lax.dynamic_slice_in_dim(g_small["conv_pw_w"], chip * HEAD_DIM, HEAD_DIM, 1)
    res = _adamw("adamw_small", _pack([w[n] for n in small_names]), _pack([g_small[n] for n in small_names]), None,
                 _pack([m[n] for n in small_names]), _pack([v[n] for n in small_names]))
    unpacked = [_unpack(r, [w[n].shape for n in small_names]) for r in res]
    for i, n in enumerate(small_names):
        out[n] = [unpacked[k][i] for k in range(4)]

    return (loss, grad_x, *[out[n][0] for n in WEIGHTS], *[out[n][1] for n in WEIGHTS],
            *[out[n][2] for n in WEIGHTS], *[out[n][3] for n in WEIGHTS])
```

```python
import functools

import numpy as np
import jax
import jax.numpy as jnp
from jax import lax
from jax.experimental import pallas as pl
from jax.experimental.pallas import tpu as pltpu

F32, BF16 = jnp.float32, jnp.bfloat16

D_MODEL = 1024
DEPTH = 2
GRID_W = 64
D_ATTN, D_HGRN, D_CONV = 512, 256, 256
HEAD_DIM = 64
N_KV = 2
KV_LANES = D_ATTN // N_KV
ROPE_THETA = 10000.0
F_MIN = 1e-6
CONV_W = 31
CONV_PAD = 15
D_FF = 2816
D_IN = 2560
N_CHIP = 4
N_DEV = 8
IN_BLK = D_IN // N_CHIP
FF_BLK = D_FF // N_CHIP
OUT_BLK = D_MODEL // N_CHIP
EPS = 1e-6
LN_EPS = 1e-5
LR, B1, B2, ADAM_EPS, WD, STEP = 0.001, 0.9, 0.999, 1e-08, 0.01, 10
CHUNK = 16
HBLK = 256
CONV_TILE = 128
VMEM_LIMIT = 56 * 1024 * 1024

COL_Q, COL_K, COL_V = 0, 4, 5
COL_HQ, COL_FF, COL_FB, COL_HI, COL_HG, COL_CA, COL_CB = 3, 4, 5, 6, 7, 8, 9


def _params(sem=None):
    return pltpu.CompilerParams(dimension_semantics=sem, vmem_limit_bytes=VMEM_LIMIT)


def _sds(shape, dtype):
    return jax.ShapeDtypeStruct(tuple(shape), dtype)


def _full(shape):
    n = len(shape)
    return pl.BlockSpec(tuple(shape), lambda *_: (0,) * n)


def _sigmoid(x):
    return 1.0 / (1.0 + jnp.exp(-x))


def _silu(x):
    return x * _sigmoid(x)


def _dsilu(x):
    s = _sigmoid(x)
    return s * (1.0 + x * (1.0 - s))


def _rowgroups(v):
    m, c = v.shape
    return v.reshape(m // 8, 8, c).sum(axis=0)


def _split2(x):
    hi = x.astype(BF16)
    lo = (x - hi.astype(F32)).astype(BF16)
    return hi, lo


def _rdot2(x, m):
    hi, lo = _split2(x)
    return (jnp.dot(hi, m, preferred_element_type=F32) + jnp.dot(lo, m, preferred_element_type=F32))


def _ldot3(m, x):
    hi = x.astype(BF16)
    r1 = x - hi.astype(F32)
    mid = r1.astype(BF16)
    lo = (r1 - mid.astype(F32)).astype(BF16)
    return (jnp.dot(m, hi, preferred_element_type=F32) + jnp.dot(m, mid, preferred_element_type=F32)
            + jnp.dot(m, lo, preferred_element_type=F32))


def _dot_nt(a, b):
    return lax.dot_general(a, b, (((1,), (1,)), ((), ())), preferred_element_type=F32)


def _dot_tn(a, b):
    return lax.dot_general(a, b, (((0,), (0,)), ((), ())), preferred_element_type=F32)


def _seg_matrix(n, seg, val):
    i = np.arange(n)
    return ((i[:, None] // seg) == (i[None, :] // seg)).astype(np.float32) * val


def _rot_matrix(n):
    r = np.zeros((n, n), np.float32)
    for i in range(n):
        if (i % 32) < 16:
            r[i + 16, i] = -1.0
        else:
            r[i - 16, i] = 1.0
    return r


def _rep_matrix():
    r = np.zeros((N_KV * HEAD_DIM, D_ATTN), np.float32)
    for kv in range(N_KV):
        for g in range(KV_LANES // HEAD_DIM):
            for d in range(HEAD_DIM):
                r[HEAD_DIM * kv + d, KV_LANES * kv + HEAD_DIM * g + d] = 1.0
    return r


def _cumsum_matrix(rev):
    i = np.arange(HBLK)
    same = (i[:, None] // CHUNK) == (i[None, :] // CHUNK)
    tri = (i[None, :] >= i[:, None]) if rev else (i[None, :] <= i[:, None])
    return (same & tri).astype(np.float32)


def _sel_matrices():
    sel = np.zeros((CHUNK, CHUNK * CHUNK), np.float32)
    selt = np.zeros((CHUNK, CHUNK * CHUNK), np.float32)
    for t in range(CHUNK):
        for s in range(CHUNK):
            sel[t, t * CHUNK + s] = 1.0
            selt[s, t * CHUNK + s] = 1.0
    return sel, selt


def _bf(a):
    return jnp.asarray(a, dtype=BF16)


def _mm(name, grid, pairs, extras, outs, epilogue, acc=None, sem=None):
    n_p, n_e, n_o = len(pairs), len(extras), len(outs)

    def body(*refs):
        ab = refs[:2 * n_p]
        ex = refs[2 * n_p:2 * n_p + n_e]
        out = refs[2 * n_p + n_e:2 * n_p + n_e + n_o]
        scr = refs[2 * n_p + n_e + n_o:]
        tot = None
        for i in range(n_p):
            a = ab[2 * i][...]
            b = ab[2 * i + 1][...]
            if a.ndim == 3:
                a = a.reshape(-1, a.shape[-1])
            if b.ndim == 3:
                b = b.reshape(-1, b.shape[-1])
            r = lax.dot_general(a.astype(BF16), b.astype(BF16), pairs[i][4], preferred_element_type=F32)
            tot = r if tot is None else tot + r

        def finish(total):
            res = epilogue(total, *[e[...] for e in ex])
            for o_ref, val in zip(out, res):
                o_ref[...] = val.astype(o_ref.dtype)

        if acc is None:
            finish(tot)
        else:
            k = pl.program_id(acc[0])

            @pl.when(k == 0)
            def _():
                scr[0][...] = tot

            @pl.when(k > 0)
            def _():
                scr[0][...] += tot

            @pl.when(k == grid[acc[0]] - 1)
            def _():
                finish(scr[0][...])

    args, in_specs = [], []
    for a, a_spec, b, b_spec, _ in pairs:
        args += [a, b]
        in_specs += [a_spec, b_spec]
    for e, e_spec in extras:
        args.append(e)
        in_specs.append(e_spec)
    if sem is None:
        sem = tuple("arbitrary" if (acc is not None and i == acc[0]) else "parallel" for i in range(len(grid)))
    return pl.pallas_call(
        body, name=name, grid=grid, in_specs=in_specs,
        out_specs=[o[1] for o in outs], out_shape=[o[0] for o in outs],
        scratch_shapes=[] if acc is None else [pltpu.VMEM(acc[1], F32)],
        compiler_params=_params(sem),
    )(*args)


NN = (((1,), (0,)), ((), ()))
NT = (((1,), (1,)), ((), ()))
TN = (((0,), (0,)), ((), ()))


def _row_tile(t):
    return min(256, t)


def _rms_fwd(name, x, w, deps=()):
    t, d = x.shape
    tm = _row_tile(t)

    def body(x_ref, w_ref, *rest):
        o_ref = rest[-1]
        xv = x_ref[...]
        r = lax.rsqrt(jnp.mean(xv * xv, axis=-1, keepdims=True) + EPS)
        o_ref[...] = (xv * r * w_ref[...]).astype(BF16)

    return pl.pallas_call(
        body, name=name, grid=(t // tm,),
        in_specs=[pl.BlockSpec((tm, d), lambda i: (i, 0)), _full((1, d))] + [_full(a.shape) for a in deps],
        out_specs=pl.BlockSpec((tm, d), lambda i: (i, 0)), out_shape=_sds((t, d), BF16),
        compiler_params=_params(("parallel",)),
    )(x, w, *deps)


def _rms_bwd(name, x, w, dh, dres, deps=()):
    t, d = x.shape
    tm = _row_tile(t)

    def body(x_ref, w_ref, dh_ref, dres_ref, *rest):
        dx_ref, dw_ref = rest[-2:]
        xv = x_ref[...]
        r = lax.rsqrt(jnp.mean(xv * xv, axis=-1, keepdims=True) + EPS)
        dy = dh_ref[...]
        gw = dy * w_ref[...]
        dx_ref[...] = dres_ref[...] + r * gw - xv * (r * r * r) * jnp.mean(gw * xv, axis=-1, keepdims=True)

        @pl.when(pl.program_id(0) == 0)
        def _():
            dw_ref[...] = jnp.zeros_like(dw_ref)

        dw_ref[...] += _rowgroups(dy * xv * r)

    tile = pl.BlockSpec((tm, d), lambda i: (i, 0))
    return pl.pallas_call(
        body, name=name, grid=(t // tm,),
        in_specs=[tile, _full((1, d)), tile, tile] + [_full(a.shape) for a in deps],
        out_specs=[tile, _full((8, d))], out_shape=[_sds((t, d), F32), _sds((8, d), F32)],
        compiler_params=_params(("arbitrary",)),
    )(x, w, dh, dres, *deps)


def _loss_kernel(y, target):
    t, d = y.shape
    tm = _row_tile(t)

    def body(y_ref, t_ref, dy_ref, acc_ref):
        e = y_ref[...] - t_ref[...]
        dy_ref[...] = e * (1.0 / d)

        @pl.when(pl.program_id(0) == 0)
        def _():
            acc_ref[...] = jnp.zeros_like(acc_ref)

        acc_ref[...] += _rowgroups(e * e)

    tile = pl.BlockSpec((tm, d), lambda i: (i, 0))
    return pl.pallas_call(
        body, name="loss_head", grid=(t // tm,), in_specs=[tile, tile],
        out_specs=[tile, _full((8, d))], out_shape=[_sds((t, d), F32), _sds((8, d), F32)],
        compiler_params=_params(("arbitrary",)),
    )(y, target)


def _rope_tables(s):
    rows = s // GRID_W
    row_id = jnp.repeat(jnp.arange(rows, dtype=F32), GRID_W)
    col_id = jnp.tile(jnp.arange(GRID_W, dtype=F32), rows)
    half = HEAD_DIM // 2
    inv_freq = ROPE_THETA ** (-jnp.arange(0, half, 2, dtype=F32) / half)
    ang_r = row_id[:, None] * inv_freq[None, :]
    ang_c = col_id[:, None] * inv_freq[None, :]
    ang = jnp.concatenate([ang_r, ang_r, ang_c, ang_c], axis=-1)
    return jnp.cos(ang).astype(F32), jnp.sin(ang).astype(F32)


def _attn_consts():
    return dict(
        seg_q=_bf(_seg_matrix(D_ATTN, HEAD_DIM, 1.0 / HEAD_DIM)),
        seg_k=_bf(_seg_matrix(N_KV * HEAD_DIM, HEAD_DIM, 1.0 / HEAD_DIM)),
        rot_q=_bf(_rot_matrix(D_ATTN)), rot_k=_bf(_rot_matrix(N_KV * HEAD_DIM)),
        rep=_bf(_rep_matrix()), rep_t=_bf(_rep_matrix().T))


def _attn_prep(name, proj, s, tabs, qw, kw, ac):
    t = proj.shape[0]
    tm = _row_tile(s)
    nst = s // tm
    kw_ = N_KV * HEAD_DIM

    def body(q_ref, k_ref, v_ref, cq_ref, sq_ref, ck_ref, sk_ref, qw_ref, kw_ref,
             segq_ref, segk_ref, rotq_ref, rotk_ref, rep_ref, qn_ref, kr_ref, vr_ref):
        q = q_ref[...]
        r = lax.rsqrt(jnp.dot((q * q).astype(BF16), segq_ref[...], preferred_element_type=F32) + EPS)
        qn = q * r * qw_ref[...]
        qr = qn * cq_ref[...] + _rdot2(qn, rotq_ref[...]) * sq_ref[...]
        qn_ref[...] = (qr * (HEAD_DIM ** -0.5)).astype(BF16)
        k = k_ref[...]
        rk = lax.rsqrt(jnp.dot((k * k).astype(BF16), segk_ref[...], preferred_element_type=F32) + EPS)
        kn = k * rk * kw_ref[...]
        kr = kn * ck_ref[...] + _rdot2(kn, rotk_ref[...]) * sk_ref[...]
        kr_ref[...] = jnp.dot(kr.astype(BF16), rep_ref[...], preferred_element_type=F32).astype(BF16)
        vr_ref[...] = jnp.dot(v_ref[...].astype(BF16), rep_ref[...], preferred_element_type=F32).astype(BF16)

    wide = pl.BlockSpec((tm, D_ATTN), lambda i: (i, 0))
    tabq = pl.BlockSpec((tm, D_ATTN), lambda i: (i % nst, 0))
    tabk = pl.BlockSpec((tm, kw_), lambda i: (i % nst, 0))
    return pl.pallas_call(
        body, name=name, grid=(t // tm,),
        in_specs=[pl.BlockSpec((tm, D_ATTN), lambda i: (i, COL_Q)), pl.BlockSpec((tm, kw_), lambda i: (i, COL_K)),
                  pl.BlockSpec((tm, kw_), lambda i: (i, COL_V)), tabq, tabq, tabk, tabk,
                  _full((1, D_ATTN)), _full((1, kw_)), _full((D_ATTN, D_ATTN)), _full((kw_, kw_)),
                  _full((D_ATTN, D_ATTN)), _full((kw_, kw_)), _full((kw_, D_ATTN))],
        out_specs=[wide, wide, wide], out_shape=[_sds((t, D_ATTN), BF16)] * 3,
        compiler_params=_params(("parallel",)),
    )(proj, proj, proj, tabs["cq"], tabs["sq"], tabs["ck"], tabs["sk"], qw, kw,
      ac["seg_q"], ac["seg_k"], ac["rot_q"], ac["rot_k"], ac["rep"])


def _attn_prep_bwd(name, proj, s, tabs, qw, kw, ac, dqs, dkr, dvr):
    t = proj.shape[0]
    tm = _row_tile(s)
    nst = s // tm
    kw_ = N_KV * HEAD_DIM
    wout = D_ATTN + 2 * kw_

    def norm_rope_bwd(x, w, cos, sin, seg, rot, d_roped):
        dn = d_roped * cos - _rdot2(d_roped * sin, rot)
        r = lax.rsqrt(jnp.dot((x * x).astype(BF16), seg, preferred_element_type=F32) + EPS)
        gw = dn * w
        dx = r * gw - x * (r * r * r) * _rdot2(gw * x, seg)
        return dx, _rowgroups(dn * x * r)

    def body(q_ref, k_ref, cq_ref, sq_ref, ck_ref, sk_ref, qw_ref, kw_ref, segq_ref, segk_ref, rotq_ref, rotk_ref,
             rept_ref, dqs_ref, dkr_ref, dvr_ref, dp_ref, dqw_ref, dkw_ref):
        dq, dqw = norm_rope_bwd(q_ref[...], qw_ref[...], cq_ref[...], sq_ref[...], segq_ref[...], rotq_ref[...],
                                dqs_ref[...] * (HEAD_DIM ** -0.5))
        dk_roped = _rdot2(dkr_ref[...], rept_ref[...])
        dk, dkw = norm_rope_bwd(k_ref[...], kw_ref[...], ck_ref[...], sk_ref[...], segk_ref[...], rotk_ref[...],
                                dk_roped)
        dv = _rdot2(dvr_ref[...], rept_ref[...])
        dp_ref[:, 0:D_ATTN] = dq.astype(BF16)
        dp_ref[:, D_ATTN:D_ATTN + kw_] = dk.astype(BF16)
        dp_ref[:, D_ATTN + kw_:wout] = dv.astype(BF16)

        @pl.when(pl.program_id(0) == 0)
        def _():
            dqw_ref[...] = jnp.zeros_like(dqw_ref)
            dkw_ref[...] = jnp.zeros_like(dkw_ref)

        dqw_ref[...] += dqw
        dkw_ref[...] += dkw

    wide = pl.BlockSpec((tm, D_ATTN), lambda i: (i, 0))
    tabq = pl.BlockSpec((tm, D_ATTN), lambda i: (i % nst, 0))
    tabk = pl.BlockSpec((tm, kw_), lambda i: (i % nst, 0))
    return pl.pallas_call(
        body, name=name, grid=(t // tm,),
        in_specs=[pl.BlockSpec((tm, D_ATTN), lambda i: (i, COL_Q)), pl.BlockSpec((tm, kw_), lambda i: (i, COL_K)),
                  tabq, tabq, tabk, tabk, _full((1, D_ATTN)), _full((1, kw_)),
                  _full((D_ATTN, D_ATTN)), _full((kw_, kw_)), _full((D_ATTN, D_ATTN)), _full((kw_, kw_)),
                  _full((D_ATTN, kw_)), wide, wide, wide],
        out_specs=[pl.BlockSpec((tm, wout), lambda i: (i, 0)), _full((8, D_ATTN)), _full((8, kw_))],
        out_shape=[_sds((t, wout), BF16), _sds((8, D_ATTN), F32), _sds((8, kw_), F32)],
        compiler_params=_params(("arbitrary",)),
    )(proj, proj, tabs["cq"], tabs["sq"], tabs["ck"], tabs["sk"], qw, kw,
      ac["seg_q"], ac["seg_k"], ac["rot_q"], ac["rot_k"], ac["rep_t"], dqs, dkr, dvr)


def _attn_tile(s):
    return min(256, s)


def _head_masks(shape):
    lane = lax.broadcasted_iota(jnp.int32, shape, 1)
    return [(lane // HEAD_DIM) == g for g in range(KV_LANES // HEAD_DIM)]


def _attn_fwd(name, qn, kr, vr, b, s):
    t = qn.shape[0]
    tq = _attn_tile(s)
    nq = s // tq

    def body(q_ref, k_ref, v_ref, o_ref):
        q = q_ref[...]
        k = k_ref[...]
        v = v_ref[...]
        acc = jnp.zeros((tq, KV_LANES), F32)
        for mask in _head_masks((tq, KV_LANES)):
            sc = _dot_nt(jnp.where(mask, q, jnp.zeros_like(q)), k)
            p = jnp.exp(sc - jnp.max(sc, axis=-1, keepdims=True))
            inv = 1.0 / jnp.sum(p, axis=-1, keepdims=True)
            og = jnp.dot(p.astype(BF16), v, preferred_element_type=F32) * inv
            acc = jnp.where(mask, og, acc)
        o_ref[...] = acc

    return pl.pallas_call(
        body, name=name, grid=(b, N_KV, nq),
        in_specs=[pl.BlockSpec((tq, KV_LANES), lambda bi, kv, i: (bi * nq + i, kv)),
                  pl.BlockSpec((s, KV_LANES), lambda bi, kv, i: (bi, kv)),
                  pl.BlockSpec((s, KV_LANES), lambda bi, kv, i: (bi, kv))],
        out_specs=pl.BlockSpec((tq, KV_LANES), lambda bi, kv, i: (bi * nq + i, kv)),
        out_shape=_sds((t, D_ATTN), F32),
        compiler_params=_params(("parallel", "parallel", "parallel")),
    )(qn, kr, vr)


def _attn_bwd(name, qn, kr, vr, do, b, s):
    t = qn.shape[0]
    tq = _attn_tile(s)
    nq = s // tq

    def body(q_ref, k_ref, v_ref, do_ref, dq_ref, dk_ref, dv_ref):
        @pl.when(pl.program_id(2) == 0)
        def _():
            dk_ref[...] = jnp.zeros_like(dk_ref)
            dv_ref[...] = jnp.zeros_like(dv_ref)

        q = q_ref[...]
        k = k_ref[...]
        v = v_ref[...]
        dout = do_ref[...].astype(BF16)
        dq = jnp.zeros((tq, KV_LANES), F32)
        for mask in _head_masks((tq, KV_LANES)):
            qg = jnp.where(mask, q, jnp.zeros_like(q))
            dog = jnp.where(mask, dout, jnp.zeros_like(dout))
            sc = _dot_nt(qg, k)
            p = jnp.exp(sc - jnp.max(sc, axis=-1, keepdims=True))
            p = p * (1.0 / jnp.sum(p, axis=-1, keepdims=True))
            dp = _dot_nt(dog, v)
            ds = (p * (dp - jnp.sum(p * dp, axis=-1, keepdims=True))).astype(BF16)
            dq = jnp.where(mask, jnp.dot(ds, k, preferred_element_type=F32), dq)
            dk_ref[...] += _dot_tn(ds, qg)
            dv_ref[...] += _dot_tn(p.astype(BF16), dog)
        dq_ref[...] = dq

    qspec = pl.BlockSpec((tq, KV_LANES), lambda bi, kv, i: (bi * nq + i, kv))
    kspec = pl.BlockSpec((s, KV_LANES), lambda bi, kv, i: (bi, kv))
    return pl.pallas_call(
        body, name=name, grid=(b, N_KV, nq),
        in_specs=[qspec, kspec, kspec, qspec],
        out_specs=[qspec, kspec, kspec], out_shape=[_sds((t, D_ATTN), F32)] * 3,
        compiler_params=_params(("parallel", "parallel", "arbitrary")),
    )(qn, kr, vr, do)


def _hgrn_consts(rev):
    sel, selt = _sel_matrices()
    cs = _cumsum_matrix(rev)
    return dict(cs=_bf(cs), cs_t=_bf(cs.T), seg=_bf(_seg_matrix(D_HGRN, HEAD_DIM, 1.0)),
                bd=jnp.asarray(_seg_matrix(D_HGRN, HEAD_DIM, 1.0), F32),
                sel=_bf(sel), selt=_bf(selt), seld=_bf(sel - selt))


def _gates(z, lb):
    sig = _sigmoid(z)
    f = lb + (1.0 - lb) * sig
    g = jnp.log(jnp.maximum(f, F_MIN))
    sn = _sigmoid(-z)
    return sig, f, g, sn, (1.0 - lb) * sn


def _pair_decay(b, rev):
    row = lax.broadcasted_iota(jnp.int32, (CHUNK, D_HGRN), 0)
    parts = []
    for t in range(CHUNK):
        m = (row >= t) if rev else (row <= t)
        parts.append(jnp.where(m, jnp.exp(jnp.minimum(b[t:t + 1, :] - b, 0.0)), 0.0))
    return jnp.concatenate(parts, axis=0)


def _rows_rep(a):
    return jnp.concatenate([jnp.broadcast_to(a[t:t + 1, :], a.shape) for t in range(CHUNK)], axis=0)


def _tile_rows(a):
    return jnp.concatenate([a] * CHUNK, axis=0)


def _hgrn_specs(b, s, rev):
    nb = s // HBLK

    def blk(j):
        return (nb - 1 - j) if rev else j

    def col(c):
        return pl.BlockSpec((HBLK, D_HGRN), lambda bi, j: (bi * nb + blk(j), c))

    return nb, blk, col


def _hgrn_fwd(name, proj, lb, b, s, rev, hc):
    t = proj.shape[0]
    nb, blk, col = _hgrn_specs(b, s, rev)
    n_ch = HBLK // CHUNK
    last = 0 if rev else CHUNK - 1

    def body(q_ref, z_ref, v_ref, lb_ref, cs_ref, seg_ref, bd_ref, sel_ref, o_ref, st_ref, state, b_scr, k_scr):
        @pl.when(pl.program_id(1) == 0)
        def _():
            state[...] = jnp.zeros_like(state)

        st_ref[...] = state[...]
        _, _, g, _, kk = _gates(z_ref[...], lb_ref[...])
        k_scr[...] = kk
        b_scr[...] = _ldot3(cs_ref[...], g)

        def chunk(i, carry):
            c = (n_ch - 1 - i) if rev else i
            rows = pl.ds(pl.multiple_of(c * CHUNK, CHUNK), CHUNK)
            q = q_ref[rows, :]
            k = k_scr[rows, :]
            v = v_ref[rows, :]
            bb = b_scr[rows, :]
            bl = bb[last:last + 1, :]
            pairs = _pair_decay(bb, rev) * _rows_rep(q) * _tile_rows(k)
            a = jnp.dot(pairs.astype(BF16), seg_ref[...], preferred_element_type=F32)
            o_intra = jnp.dot(sel_ref[...], (a * _tile_rows(v)).astype(BF16), preferred_element_type=F32)
            st = state[...]
            o_inter = _dot_nt((q * jnp.exp(bb)).astype(BF16), st.astype(BF16))
            o_ref[rows, :] = o_intra + o_inter
            ke = k * jnp.exp(bl - bb)
            state[...] = st * jnp.exp(bl) + bd_ref[...] * _dot_tn(v.astype(BF16), ke.astype(BF16))
            return carry

        lax.fori_loop(0, n_ch, chunk, 0)

    sq = (D_HGRN, D_HGRN)
    return pl.pallas_call(
        body, name=name, grid=(b, nb),
        in_specs=[col(COL_HQ), col(COL_FB if rev else COL_FF), col(COL_HI), _full((1, D_HGRN)),
                  _full((HBLK, HBLK)), _full(sq), _full(sq), _full((CHUNK, CHUNK * CHUNK))],
        out_specs=[pl.BlockSpec((HBLK, D_HGRN), lambda bi, j: (bi * nb + blk(j), 0)),
                   pl.BlockSpec((None,) + sq, lambda bi, j: (bi * nb + blk(j), 0, 0))],
        out_shape=[_sds((t, D_HGRN), F32), _sds((b * nb,) + sq, F32)],
        scratch_shapes=[pltpu.VMEM(sq, F32), pltpu.VMEM((HBLK, D_HGRN), F32), pltpu.VMEM((HBLK, D_HGRN), F32)],
        compiler_params=_params(("parallel", "arbitrary")),
    )(proj, proj, proj, lb, hc["cs"], hc["seg"], hc["bd"], hc["sel"])


def _hgrn_bwd(name, proj, lb, st_blk, do, dq_prev, dv_prev, b, s, rev, hc):
    t = proj.shape[0]
    nb = s // HBLK
    n_ch = HBLK // CHUNK
    last = 0 if rev else CHUNK - 1

    def blk(j):
        return j if rev else (nb - 1 - j)

    def col(c):
        return pl.BlockSpec((HBLK, D_HGRN), lambda bi, j: (bi * nb + blk(j), c))

    def body(q_ref, z_ref, v_ref, lb_ref, st_ref, do_ref, dqp_ref, dvp_ref, cs_ref, cst_ref, seg_ref, bd_ref,
             sel_ref, selt_ref, seld_ref, dq_ref, dv_ref, dz_ref, dlb_ref,
             dstate, states, b_scr, k_scr, db_scr, dk_scr):
        first = jnp.logical_and(pl.program_id(0) == 0, pl.program_id(1) == 0)

        @pl.when(first)
        def _():
            dlb_ref[...] = jnp.zeros_like(dlb_ref)

        @pl.when(pl.program_id(1) == 0)
        def _():
            dstate[...] = jnp.zeros_like(dstate)

        lbv = lb_ref[...]
        z = z_ref[...]
        sig, f, g, sn, kk = _gates(z, lbv)
        k_scr[...] = kk
        b_scr[...] = _ldot3(cs_ref[...], g)

        def rows_of(c):
            return pl.ds(pl.multiple_of(c * CHUNK, CHUNK), CHUNK)

        def replay(i, st):
            c = (n_ch - 1 - i) if rev else i
            rows = rows_of(c)
            states[c] = st
            bb = b_scr[rows, :]
            bl = bb[last:last + 1, :]
            ke = k_scr[rows, :] * jnp.exp(bl - bb)
            return st * jnp.exp(bl) + bd_ref[...] * _dot_tn(v_ref[rows, :].astype(BF16), ke.astype(BF16))

        lax.fori_loop(0, n_ch, replay, st_ref[...])
        row = lax.broadcasted_iota(jnp.int32, (CHUNK, D_HGRN), 0)

        def chunk(i, carry):
            c = i if rev else (n_ch - 1 - i)
            rows = rows_of(c)
            q = q_ref[rows, :]
            k = k_scr[rows, :]
            v = v_ref[rows, :]
            bb = b_scr[rows, :]
            dout = do_ref[rows, :]
            bl = bb[last:last + 1, :]
            st_p = states[c]
            dst_n = dstate[...]
            eb = jnp.exp(bb)
            ebl = jnp.exp(bl - bb)
            ebl_last = jnp.exp(bl)
            qe = q * eb
            ke = k * ebl
            dob = dout.astype(BF16)
            dstb = dst_n.astype(BF16)
            dqe = jnp.dot(dob, st_p.astype(BF16), preferred_element_type=F32)
            dke = jnp.dot(v.astype(BF16), dstb, preferred_element_type=F32)
            dv = _dot_nt(ke.astype(BF16), dstb)
            dbl = jnp.sum(dst_n * st_p, axis=0, keepdims=True) * ebl_last + jnp.sum(dke * ke, axis=0, keepdims=True)
            dq = dqe * eb
            dk = dke * ebl
            db = dqe * qe - dke * ke
            dec = _pair_decay(bb, rev)
            q_rep = _rows_rep(q)
            k_til = _tile_rows(k)
            do_rep = _rows_rep(dout)
            pairs = dec * q_rep * k_til
            a = jnp.dot(pairs.astype(BF16), seg_ref[...], preferred_element_type=F32)
            wb = jnp.dot((_tile_rows(v) * do_rep).astype(BF16), seg_ref[...], preferred_element_type=F32)
            gdec = wb * dec
            dq = dq + jnp.dot(sel_ref[...], (gdec * k_til).astype(BF16), preferred_element_type=F32)
            dk = dk + jnp.dot(selt_ref[...], (gdec * q_rep).astype(BF16), preferred_element_type=F32)
            dv = dv + jnp.dot(selt_ref[...], (a * do_rep).astype(BF16), preferred_element_type=F32)
            db = db + jnp.dot(seld_ref[...], (wb * pairs).astype(BF16), preferred_element_type=F32)
            db = db + jnp.where(row == last, dbl, 0.0)
            dq_ref[rows, :] = dq + dqp_ref[rows, :]
            dv_ref[rows, :] = dv + dvp_ref[rows, :]
            dk_scr[rows, :] = dk
            db_scr[rows, :] = db
            dstate[...] = dst_n * ebl_last + bd_ref[...] * _dot_tn(dob, qe.astype(BF16))
            return carry

        lax.fori_loop(0, n_ch, chunk, 0)
        hi, lo = _split2(db_scr[...])
        dg = (jnp.dot(cst_ref[...], hi, preferred_element_type=F32)
              + jnp.dot(cst_ref[...], lo, preferred_element_type=F32))
        dgf = jnp.where(f > F_MIN, dg / f, 0.0)
        dk = dk_scr[...]
        dz_ref[...] = dgf * (1.0 - lbv) * sig * (1.0 - sig) - dk * (1.0 - lbv) * sn * (1.0 - sn)
        dlb_ref[...] += _rowgroups(dgf * (1.0 - sig) - dk * sn)

    sq = (D_HGRN, D_HGRN)
    blk0 = pl.BlockSpec((HBLK, D_HGRN), lambda bi, j: (bi * nb + blk(j), 0))
    pairs_shape = (CHUNK, CHUNK * CHUNK)
    return pl.pallas_call(
        body, name=name, grid=(b, nb),
        in_specs=[col(COL_HQ), col(COL_FB if rev else COL_FF), col(COL_HI), _full((1, D_HGRN)),
                  pl.BlockSpec((None,) + sq, lambda bi, j: (bi * nb + blk(j), 0, 0)), blk0, blk0, blk0,
                  _full((HBLK, HBLK)), _full((HBLK, HBLK)), _full(sq), _full(sq),
                  _full(pairs_shape), _full(pairs_shape), _full(pairs_shape)],
        out_specs=[blk0, blk0, blk0, _full((8, D_HGRN))],
        out_shape=[_sds((t, D_HGRN), F32)] * 3 + [_sds((8, D_HGRN), F32)],
        scratch_shapes=[pltpu.VMEM(sq, F32), pltpu.VMEM((n_ch,) + sq, F32)] + [pltpu.VMEM((HBLK, D_HGRN), F32)] * 4,
        compiler_params=_params(("arbitrary", "arbitrary")),
    )(proj, proj, proj, lb, st_blk, do, dq_prev, dv_prev,
      hc["cs"], hc["cs_t"], hc["seg"], hc["bd"], hc["sel"], hc["selt"], hc["seld"])


def _lower_bounds(logits):
    n = logits.shape[1]

    def body(x_ref, o_ref):
        x = x_ref[...]
        for d in range(2):
            rows = [x[l * 2 + d:l * 2 + d + 1, :] for l in range(DEPTH)]
            mx = functools.reduce(jnp.maximum, rows)
            ex = [jnp.exp(r - mx) for r in rows]
            tot = functools.reduce(lambda a, c: a + c, ex)
            sm = [e / tot for e in ex]
            run = jnp.zeros_like(sm[0])
            for l in range(DEPTH):
                run = run + sm[l]
                o_ref[l * 2 + d:l * 2 + d + 1, :] = run - sm[0]

    return pl.pallas_call(body, name="hgrn_lower_bounds", out_shape=_sds(logits.shape, F32),
                          in_specs=[_full(logits.shape)], out_specs=_full(logits.shape), grid=(1,),
                          compiler_params=_params(("arbitrary",)))(logits)


def _lower_bounds_bwd(logits, dlb):
    def body(x_ref, g_ref, o_ref):
        x = x_ref[...]
        gv = g_ref[...]
        for d in range(2):
            rows = [x[l * 2 + d:l * 2 + d + 1, :] for l in range(DEPTH)]
            gr = [gv[l * 2 + d:l * 2 + d + 1, :] for l in range(DEPTH)]
            mx = functools.reduce(jnp.maximum, rows)
            ex = [jnp.exp(r - mx) for r in rows]
            tot = functools.reduce(lambda a, c: a + c, ex)
            sm = [e / tot for e in ex]
            dsm = []
            for i in range(DEPTH):
                acc = functools.reduce(lambda a, c: a + c, gr[i:])
                if i == 0:
                    acc = acc - functools.reduce(lambda a, c: a + c, gr)
                dsm.append(acc)
            inner = functools.reduce(lambda a, c: a + c, [sm[i] * dsm[i] for i in range(DEPTH)])
            for i in range(DEPTH):
                o_ref[i * 2 + d:i * 2 + d + 1, :] = sm[i] * (dsm[i] - inner)

    return pl.pallas_call(body, name="hgrn_lower_bounds_bwd", out_shape=_sds(logits.shape, F32),
                          in_specs=[_full(logits.shape), _full(logits.shape)], out_specs=_full(logits.shape),
                          grid=(1,), compiler_params=_params(("arbitrary",)))(logits, dlb)


def _conv_rows(s):
    return s + 2 * (CONV_PAD + 1)


def _conv_fwd(name, proj, dw_w, dw_b, ln_w, ln_b, pw_w, pw_b, b, s):
    t = proj.shape[0]
    pad = CONV_PAD + 1
    nt = s // CONV_TILE

    def body(a_ref, g_ref, w_ref, dwb_ref, lnw_ref, lnb_ref, pw_ref, pwb_ref, y_ref, upad, win):
        upad[0:pad, :] = jnp.zeros((pad, D_CONV), F32)
        upad[s + pad:s + 2 * pad, :] = jnp.zeros((pad, D_CONV), F32)

        def fill(i, carry):
            rows = pl.ds(pl.multiple_of(i * CONV_TILE, CONV_TILE), CONV_TILE)
            upad[pl.ds(pl.multiple_of(i * CONV_TILE + pad, pad), CONV_TILE), :] = a_ref[rows, :] * _sigmoid(g_ref[rows, :])
            return carry

        lax.fori_loop(0, nt, fill, 0)

        def tile(i, carry):
            r0 = pl.multiple_of(i * CONV_TILE, CONV_TILE)
            win[...] = upad[pl.ds(r0, CONV_TILE + 2 * pad), :]
            acc = jnp.zeros((CONV_TILE, D_CONV), F32)
            for j in range(CONV_W):
                acc = acc + win[j + 1:j + 1 + CONV_TILE, :] * w_ref[j:j + 1, :]
            c = acc + dwb_ref[...]
            mu = jnp.mean(c, axis=-1, keepdims=True)
            xc = c - mu
            rstd = lax.rsqrt(jnp.mean(xc * xc, axis=-1, keepdims=True) + LN_EPS)
            n = xc * rstd * lnw_ref[...] + lnb_ref[...]
            y_ref[pl.ds(r0, CONV_TILE), :] = (jnp.dot(_silu(n).astype(BF16), pw_ref[...].astype(BF16),
                                                      preferred_element_type=F32) + pwb_ref[...])
            return carry

        lax.fori_loop(0, nt, tile, 0)

    vec = _full((1, D_CONV))
    return pl.pallas_call(
        body, name=name, grid=(b,),
        in_specs=[pl.BlockSpec((s, D_CONV), lambda bi: (bi, COL_CA)), pl.BlockSpec((s, D_CONV), lambda bi: (bi, COL_CB)),
                  _full((CONV_W + 1, D_CONV)), vec, vec, vec, _full((D_CONV, D_CONV)), vec],
        out_specs=pl.BlockSpec((s, D_CONV), lambda bi: (bi, 0)), out_shape=_sds((t, D_CONV), F32),
        scratch_shapes=[pltpu.VMEM((_conv_rows(s), D_CONV), F32), pltpu.VMEM((CONV_TILE + 2 * pad, D_CONV), F32)],
        compiler_params=_params(("parallel",)),
    )(proj, proj, dw_w, dw_b, ln_w, ln_b, pw_w, pw_b)


def _conv_bwd(name, proj, dw_w, dw_b, ln_w, ln_b, pw_w, dy, b, s):
    t = proj.shape[0]
    pad = CONV_PAD + 1
    nt = s // CONV_TILE

    def body(a_ref, g_ref, w_ref, dwb_ref, lnw_ref, lnb_ref, pw_ref, dy_ref, dab_ref, dpw_ref, ddw_ref, dvec_ref,
             upad, dcpad, tap_acc, win, dwin):
        @pl.when(pl.program_id(0) == 0)
        def _():
            dpw_ref[...] = jnp.zeros_like(dpw_ref)
            ddw_ref[...] = jnp.zeros_like(ddw_ref)
            dvec_ref[...] = jnp.zeros_like(dvec_ref)

        zeros = jnp.zeros((pad, D_CONV), F32)
        upad[0:pad, :] = zeros
        upad[s + pad:s + 2 * pad, :] = zeros
        dcpad[0:pad, :] = zeros
        dcpad[s + pad:s + 2 * pad, :] = zeros
        tap_acc[...] = jnp.zeros_like(tap_acc)

        def inner(i):
            return pl.ds(pl.multiple_of(i * CONV_TILE + pad, pad), CONV_TILE)

        def fill(i, carry):
            rows = pl.ds(pl.multiple_of(i * CONV_TILE, CONV_TILE), CONV_TILE)
            upad[inner(i), :] = a_ref[rows, :] * _sigmoid(g_ref[rows, :])
            return carry

        lax.fori_loop(0, nt, fill, 0)

        def tile_a(i, carry):
            r0 = pl.multiple_of(i * CONV_TILE, CONV_TILE)
            win[...] = upad[pl.ds(r0, CONV_TILE + 2 * pad), :]
            acc = jnp.zeros((CONV_TILE, D_CONV), F32)
            for j in range(CONV_W):
                acc = acc + win[j + 1:j + 1 + CONV_TILE, :] * w_ref[j:j + 1, :]
            c = acc + dwb_ref[...]
            mu = jnp.mean(c, axis=-1, keepdims=True)
            xc = c - mu
            rstd = lax.rsqrt(jnp.mean(xc * xc, axis=-1, keepdims=True) + LN_EPS)
            xhat = xc * rstd
            n = xhat * lnw_ref[...] + lnb_ref[...]
            dyt = dy_ref[pl.ds(r0, CONV_TILE), :]
            dyb = dyt.astype(BF16)
            dpw_ref[...] += _dot_tn(_silu(n).astype(BF16), dyb)
            dn = _dot_nt(dyb, pw_ref[...].astype(BF16)) * _dsilu(n)
            dxh = dn * lnw_ref[...]
            dc = rstd * (dxh - jnp.mean(dxh, axis=-1, keepdims=True)
                         - xhat * jnp.mean(dxh * xhat, axis=-1, keepdims=True))
            dcpad[inner(i), :] = dc
            dvec_ref[0:1, :] += jnp.sum(dyt, axis=0, keepdims=True)
            dvec_ref[1:2, :] += jnp.sum(dn * xhat, axis=0, keepdims=True)
            dvec_ref[2:3, :] += jnp.sum(dn, axis=0, keepdims=True)
            dvec_ref[3:4, :] += jnp.sum(dc, axis=0, keepdims=True)
            return carry

        lax.fori_loop(0, nt, tile_a, 0)

        def tile_b(i, carry):
            r0 = pl.multiple_of(i * CONV_TILE, CONV_TILE)
            win[...] = upad[pl.ds(r0, CONV_TILE + 2 * pad), :]
            dwin[...] = dcpad[pl.ds(r0, CONV_TILE + 2 * pad), :]
            dct = dwin[pad:pad + CONV_TILE, :]
            du = jnp.zeros((CONV_TILE, D_CONV), F32)
            for j in range(CONV_W):
                du = du + dwin[2 * pad - 1 - j:2 * pad - 1 - j + CONV_TILE, :] * w_ref[j:j + 1, :]
                tap_acc[8 * j:8 * j + 8, :] += _rowgroups(dct * win[j + 1:j + 1 + CONV_TILE, :])
            rows = pl.ds(r0, CONV_TILE)
            sg = _sigmoid(g_ref[rows, :])
            dab_ref[rows, 0:D_CONV] = (du * sg).astype(BF16)
            dab_ref[rows, D_CONV:2 * D_CONV] = (du * a_ref[rows, :] * sg * (1.0 - sg)).astype(BF16)
            return carry

        lax.fori_loop(0, nt, tile_b, 0)
        for j in range(CONV_W):
            ddw_ref[j:j + 1, :] += jnp.sum(tap_acc[8 * j:8 * j + 8, :], axis=0, keepdims=True)

    vec = _full((1, D_CONV))
    return pl.pallas_call(
        body, name=name, grid=(b,),
        in_specs=[pl.BlockSpec((s, D_CONV), lambda bi: (bi, COL_CA)), pl.BlockSpec((s, D_CONV), lambda bi: (bi, COL_CB)),
                  _full((CONV_W + 1, D_CONV)), vec, vec, vec, _full((D_CONV, D_CONV)),
                  pl.BlockSpec((s, D_CONV), lambda bi: (bi, 0))],
        out_specs=[pl.BlockSpec((s, 2 * D_CONV), lambda bi: (bi, 0)), _full((D_CONV, D_CONV)),
                   _full((CONV_W + 1, D_CONV)), _full((8, D_CONV))],
        out_shape=[_sds((t, 2 * D_CONV), BF16), _sds((D_CONV, D_CONV), F32), _sds((CONV_W + 1, D_CONV), F32),
                   _sds((8, D_CONV), F32)],
        scratch_shapes=[pltpu.VMEM((_conv_rows(s), D_CONV), F32), pltpu.VMEM((_conv_rows(s), D_CONV), F32),
                        pltpu.VMEM((8 * CONV_W, D_CONV), F32), pltpu.VMEM((CONV_TILE + 2 * pad, D_CONV), F32),
                        pltpu.VMEM((CONV_TILE + 2 * pad, D_CONV), F32)],
        compiler_params=_params(("arbitrary",)),
    )(proj, proj, dw_w, dw_b, ln_w, ln_b, pw_w, dy)


def _mix_fwd(name, y_attn, o_fw, o_bw, proj, y_conv, aw, gw, cw, seg):
    t = y_attn.shape[0]
    tm = _row_tile(t)

    def body(ya_ref, of_ref, ob_ref, hg_ref, yc_ref, aw_ref, gw_ref, cw_ref, seg_ref, o_ref):
        ya = ya_ref[...]
        ra = lax.rsqrt(jnp.mean(ya * ya, axis=-1, keepdims=True) + EPS)
        o_ref[:, 0:D_ATTN] = (ya * ra * aw_ref[...]).astype(BF16)
        o = of_ref[...] + ob_ref[...]
        ro = lax.rsqrt(jnp.dot((o * o).astype(BF16), seg_ref[...], preferred_element_type=F32) + EPS)
        o_ref[:, D_ATTN:D_ATTN + D_HGRN] = (o * ro * gw_ref[...] * _silu(hg_ref[...])).astype(BF16)
        yc = yc_ref[...]
        rc = lax.rsqrt(jnp.mean(yc * yc, axis=-1, keepdims=True) + EPS)
        o_ref[:, D_ATTN + D_HGRN:D_MODEL] = (yc * rc * cw_ref[...]).astype(BF16)

    def tile(w, c=0):
        return pl.BlockSpec((tm, w), lambda i: (i, c))

    return pl.pallas_call(
        body, name=name, grid=(t // tm,),
        in_specs=[tile(D_ATTN), tile(D_HGRN), tile(D_HGRN), tile(D_HGRN, COL_HG), tile(D_CONV),
                  _full((1, D_ATTN)), _full((1, D_HGRN)), _full((1, D_CONV)), _full((D_HGRN, D_HGRN))],
        out_specs=tile(D_MODEL), out_shape=_sds((t, D_MODEL), BF16),
        compiler_params=_params(("parallel",)),
    )(y_attn, o_fw, o_bw, proj, y_conv, aw, gw, cw, seg)


def _mix_bwd(name, dmix, y_attn, o_fw, o_bw, proj, y_conv, aw, gw, cw, seg):
    t = y_attn.shape[0]
    tm = _row_tile(t)

    def rms_bwd(x, w, dy):
        r = lax.rsqrt(jnp.mean(x * x, axis=-1, keepdims=True) + EPS)
        gwv = dy * w
        return r * gwv - x * (r * r * r) * jnp.mean(gwv * x, axis=-1, keepdims=True), _rowgroups(dy * x * r)

    def body(dm_ref, ya_ref, of_ref, ob_ref, hg_ref, yc_ref, aw_ref, gw_ref, cw_ref, seg_ref,
             dya_ref, do_ref, dhg_ref, dyc_ref, daw_ref, dgw_ref, dcw_ref):
        @pl.when(pl.program_id(0) == 0)
        def _():
            daw_ref[...] = jnp.zeros_like(daw_ref)
            dgw_ref[...] = jnp.zeros_like(dgw_ref)
            dcw_ref[...] = jnp.zeros_like(dcw_ref)

        dya, daw = rms_bwd(ya_ref[...], aw_ref[...], dm_ref[:, 0:D_ATTN])
        dya_ref[...] = dya
        daw_ref[...] += daw
        dyc, dcw = rms_bwd(yc_ref[...], cw_ref[...], dm_ref[:, D_ATTN + D_HGRN:D_MODEL])
        dyc_ref[...] = dyc
        dcw_ref[...] += dcw
        d2 = dm_ref[:, D_ATTN:D_ATTN + D_HGRN]
        o = of_ref[...] + ob_ref[...]
        hg = hg_ref[...]
        ro = lax.rsqrt(jnp.dot((o * o).astype(BF16), seg_ref[...], preferred_element_type=F32) + EPS)
        dn = d2 * _silu(hg)
        dhg_ref[...] = (d2 * o * ro * gw_ref[...] * _dsilu(hg)).astype(BF16)
        gwv = dn * gw_ref[...]
        do_ref[...] = ro * gwv - o * (ro * ro * ro) * _rdot2(gwv * o, seg_ref[...])
        dgw_ref[...] += _rowgroups(dn * o * ro)

    def tile(w, c=0):
        return pl.BlockSpec((tm, w), lambda i: (i, c))

    return pl.pallas_call(
        body, name=name, grid=(t // tm,),
        in_specs=[tile(D_MODEL), tile(D_ATTN), tile(D_HGRN), tile(D_HGRN), tile(D_HGRN, COL_HG), tile(D_CONV),
                  _full((1, D_ATTN)), _full((1, D_HGRN)), _full((1, D_CONV)), _full((D_HGRN, D_HGRN))],
        out_specs=[tile(D_ATTN), tile(D_HGRN), tile(D_HGRN), tile(D_CONV),
                   _full((8, D_ATTN)), _full((8, D_HGRN)), _full((8, D_CONV))],
        out_shape=[_sds((t, D_ATTN), F32), _sds((t, D_HGRN), F32), _sds((t, D_HGRN), BF16), _sds((t, D_CONV), F32),
                   _sds((8, D_ATTN), F32), _sds((8, D_HGRN), F32), _sds((8, D_CONV), F32)],
        compiler_params=_params(("arbitrary",)),
    )(dmix, y_attn, o_fw, o_bw, proj, y_conv, aw, gw, cw, seg)


def _hgrn_dproj(name, dq, dz_fw, dz_bw, dv, dhg):
    t = dq.shape[0]
    tm = _row_tile(t)

    def body(a_ref, b_ref, c_ref, d_ref, e_ref, o_ref):
        for i, r in enumerate((a_ref, b_ref, c_ref, d_ref, e_ref)):
            o_ref[:, i * D_HGRN:(i + 1) * D_HGRN] = r[...].astype(BF16)

    tile = pl.BlockSpec((tm, D_HGRN), lambda i: (i, 0))
    return pl.pallas_call(
        body, name=name, grid=(t // tm,), in_specs=[tile] * 5,
        out_specs=pl.BlockSpec((tm, 5 * D_HGRN), lambda i: (i, 0)), out_shape=_sds((t, 5 * D_HGRN), BF16),
        compiler_params=_params(("parallel",)),
    )(dq, dz_fw, dz_bw, dv, dhg)


def _mm_tile(t):
    return min(512, t)


def _w_blk(rows, cols, j_of):
    return pl.BlockSpec((None, rows, cols), lambda *g: (j_of(*g), 0, 0))


def _layer_fwd(l, x, wget, sm, tabs, cst, b, s, deps):
    t = x.shape[0]
    tm = _mm_tile(t)
    nt = t // tm
    pre = "l%d_" % l
    row = lambda w: pl.BlockSpec((tm, w), lambda i, *_: (i, 0))

    h1 = _rms_fwd(pre + "mix_norm", x, sm["mix_norm_w"][l], deps)
    (proj,) = _mm(pre + "in_proj", (nt, N_CHIP),
                  [(h1, row(D_MODEL), wget(l, "w_in", h1),_w_blk(D_MODEL, IN_BLK,lambda i, j: j), NN)], [],
                  [(_sds((t, D_IN), F32), pl.BlockSpec((tm, IN_BLK), lambda i, j: (i, j)))],
                  lambda tot: (tot,))
    qn, kr, vr = _attn_prep(pre + "attn_prep", proj, s, tabs, sm["q_norm_w"][l], sm["k_norm_w"][l], cst["attn"])
    y_attn = _attn_fwd(pre + "attn", qn, kr, vr, b, s)
    o_fw, st_fw = _hgrn_fwd(pre + "hgrn_fw", proj, sm["lb"][l][0], b, s, False, cst["hg_fw"])
    o_bw, st_bw = _hgrn_fwd(pre + "hgrn_bw", proj, sm["lb"][l][1], b, s, True, cst["hg_bw"])
    y_conv = _conv_fwd(pre + "conv", proj, sm["conv_dw_w"][l], sm["conv_dw_b"][l], sm["conv_ln_w"][l],
                       sm["conv_ln_b"][l], sm["conv_pw_w"][l], sm["conv_pw_b"][l], b, s)
    mixed = _mix_fwd(pre + "mix", y_attn, o_fw, o_bw, proj, y_conv, sm["attn_out_norm_w"][l], sm["gnorm_w"][l],
                     sm["conv_out_norm_w"][l], cst["seg_h"])
    (x1,) = _mm(pre + "out_proj", (nt,),
                [(mixed, row(D_MODEL), wget(l, "w_out", mixed),
                  pl.BlockSpec((N_CHIP, OUT_BLK, D_MODEL), lambda i: (0, 0, 0)), NN)],
                [(x, row(D_MODEL))], [(_sds((t, D_MODEL), F32), row(D_MODEL))],
                lambda tot, xr: (xr + tot,))
    h2 = _rms_fwd(pre + "ffn_norm", x1, sm["ffn_norm_w"][l])
    ffb = pl.BlockSpec((None, tm, FF_BLK), lambda i, j: (j, i, 0))
    ffs = _sds((N_CHIP, t, FF_BLK), BF16)

    def gu_body(h_ref, wg_ref, wu_ref, g_ref, u_ref, a_ref):
        hv = h_ref[...]
        gv = jnp.dot(hv, wg_ref[...], preferred_element_type=F32)
        uv = jnp.dot(hv, wu_ref[...], preferred_element_type=F32)
        g_ref[...] = gv.astype(BF16)
        u_ref[...] = uv.astype(BF16)
        a_ref[...] = (_silu(gv) * uv).astype(BF16)

    gate, up, act = pl.pallas_call(
        gu_body, name=pre + "ffn_gate_up", grid=(nt, N_CHIP),
        in_specs=[row(D_MODEL), _w_blk(D_MODEL, FF_BLK,lambda i, j: j), _w_blk(D_MODEL, FF_BLK,lambda i, j: j)],
        out_specs=[ffb, ffb, ffb], out_shape=[ffs, ffs, ffs],
        compiler_params=_params(("parallel", "parallel")),
    )(h2, wget(l, "w_gate", h2), wget(l, "w_up", h2))
    (x2,) = _mm(pre + "ffn_down", (nt, N_CHIP),
                [(act, ffb, wget(l, "w_down", act),_w_blk(FF_BLK, D_MODEL,lambda i, j: j), NN)],
                [(x1, row(D_MODEL))], [(_sds((t, D_MODEL), F32), row(D_MODEL))],
                lambda tot, xr: (xr + tot,), acc=(1, (tm, D_MODEL)))
    saved = dict(x=x, h1=h1, proj=proj, qn=qn, kr=kr, vr=vr, y_attn=y_attn, o_fw=o_fw, o_bw=o_bw, st_fw=st_fw,
                 st_bw=st_bw, y_conv=y_conv, mixed=mixed, x1=x1, h2=h2, gate=gate, up=up, act=act)
    return x2, saved


def _layer_bwd(l, dx2, sv, wget, sm, tabs, cst, b, s, on_grads):
    t = dx2.shape[0]
    tm = _mm_tile(t)
    nt = t // tm
    pre = "l%d_" % l
    row = lambda w: pl.BlockSpec((tm, w), lambda i, *_: (i, 0))
    ffb = pl.BlockSpec((None, tm, FF_BLK), lambda i, j: (j, i, 0))
    ffs = _sds((N_CHIP, t, FF_BLK), BF16)

    dgate, dup = _mm(pre + "ffn_down_dx", (nt, N_CHIP),
                     [(dx2, row(D_MODEL), wget(l, "w_down", dx2),_w_blk(FF_BLK, D_MODEL,lambda i, j: j), NT)],
                     [(sv["gate"], ffb), (sv["up"], ffb)], [(ffs, ffb), (ffs, ffb)],
                     lambda da, g, u: (da * u.astype(F32) * _dsilu(g.astype(F32)), da * _silu(g.astype(F32))))
    colt = lambda w: pl.BlockSpec((tm, w), lambda j, k: (k, 0))
    fft = pl.BlockSpec((None, tm, FF_BLK), lambda j, k: (j, k, 0))
    (g_down,) = _mm(pre + "ffn_down_dw", (N_CHIP, nt), [(sv["act"], fft, dx2, colt(D_MODEL), TN)], [],
                    [(_sds((N_CHIP, FF_BLK, D_MODEL), BF16), pl.BlockSpec((None, FF_BLK, D_MODEL), lambda j, k: (j, 0, 0)))],
                    lambda tot: (tot,), acc=(1, (FF_BLK, D_MODEL)))
    wff = pl.BlockSpec((None, D_MODEL, FF_BLK), lambda j, k: (j, 0, 0))
    (g_gate,) = _mm(pre + "ffn_gate_dw", (N_CHIP, nt), [(sv["h2"], colt(D_MODEL), dgate, fft, TN)], [],
                    [(_sds((N_CHIP, D_MODEL, FF_BLK), BF16), wff)], lambda tot: (tot,), acc=(1, (D_MODEL, FF_BLK)))
    (g_up,) = _mm(pre + "ffn_up_dw", (N_CHIP, nt), [(sv["h2"], colt(D_MODEL), dup, fft, TN)], [],
                  [(_sds((N_CHIP, D_MODEL, FF_BLK), BF16), wff)], lambda tot: (tot,), acc=(1, (D_MODEL, FF_BLK)))
    (dh2,) = _mm(pre + "ffn_dh", (nt, N_CHIP),
                 [(dgate, ffb, wget(l, "w_gate", dx2),_w_blk(D_MODEL, FF_BLK,lambda i, j: j), NT),
                  (dup, ffb, wget(l, "w_up", dx2),_w_blk(D_MODEL, FF_BLK,lambda i, j: j), NT)], [],
                 [(_sds((t, D_MODEL), F32), row(D_MODEL))], lambda tot: (tot,), acc=(1, (tm, D_MODEL)))
    deps = on_grads(l, dict(w_gate=g_gate, w_up=g_up, w_down=g_down))
    dx1, d_ffn_norm = _rms_bwd(pre + "ffn_norm_bwd", sv["x1"], sm["ffn_norm_w"][l], dh2, dx2, deps)

    (dmix,) = _mm(pre + "out_proj_dx", (nt,),
                  [(dx1, row(D_MODEL), wget(l, "w_out", dx2),
                    pl.BlockSpec((N_CHIP, OUT_BLK, D_MODEL), lambda i: (0, 0, 0)), NT)], [],
                  [(_sds((t, D_MODEL), F32), row(D_MODEL))], lambda tot: (tot,))
    (g_out,) = _mm(pre + "out_proj_dw", (N_CHIP, nt),
                   [(sv["mixed"], pl.BlockSpec((tm, OUT_BLK), lambda j, k: (k, j)), dx1, colt(D_MODEL), TN)], [],
                   [(_sds((N_CHIP, OUT_BLK, D_MODEL), BF16), pl.BlockSpec((None, OUT_BLK, D_MODEL), lambda j, k: (j, 0, 0)))],
                   lambda tot: (tot,), acc=(1, (OUT_BLK, D_MODEL)))
    proj = sv["proj"]
    dya, do_h, dhg, dyc, d_aw, d_gw, d_cw = _mix_bwd(
        pre + "mix_bwd", dmix, sv["y_attn"], sv["o_fw"], sv["o_bw"], proj, sv["y_conv"],
        sm["attn_out_norm_w"][l], sm["gnorm_w"][l], sm["conv_out_norm_w"][l], cst["seg_h"])
    dqs, dkr, dvr = _attn_bwd(pre + "attn_bwd", sv["qn"], sv["kr"], sv["vr"], dya, b, s)
    dp_attn, d_qw, d_kw = _attn_prep_bwd(pre + "attn_prep_bwd", proj, s, tabs, sm["q_norm_w"][l], sm["k_norm_w"][l],
                                         cst["attn"], dqs, dkr, dvr)
    zero = jnp.zeros((t, D_HGRN), F32)
    dq1, dv1, dz_fw, dlb_fw = _hgrn_bwd(pre + "hgrn_fw_bwd", proj, sm["lb"][l][0], sv["st_fw"], do_h, zero, zero,
                                        b, s, False, cst["hg_fw"])
    dq2, dv2, dz_bw, dlb_bw = _hgrn_bwd(pre + "hgrn_bw_bwd", proj, sm["lb"][l][1], sv["st_bw"], do_h, dq1, dv1,
                                        b, s, True, cst["hg_bw"])
    dp_hgrn = _hgrn_dproj(pre + "hgrn_dproj", dq2, dz_fw, dz_bw, dv2, dhg)
    dp_conv, d_pw, d_dw, d_cvec = _conv_bwd(pre + "conv_bwd", proj, sm["conv_dw_w"][l], sm["conv_dw_b"][l],
                                            sm["conv_ln_w"][l], sm["conv_ln_b"][l], sm["conv_pw_w"][l], dyc, b, s)
    dproj = jnp.concatenate([dp_attn, dp_hgrn, dp_conv], axis=1)

    (g_in,) = _mm(pre + "in_proj_dw", (N_CHIP, nt),
                  [(sv["h1"], colt(D_MODEL), dproj, pl.BlockSpec((tm, IN_BLK), lambda j, k: (k, j)), TN)], [],
                  [(_sds((N_CHIP, D_MODEL, IN_BLK), BF16), pl.BlockSpec((None, D_MODEL, IN_BLK), lambda j, k: (j, 0, 0)))],
                  lambda tot: (tot,), acc=(1, (D_MODEL, IN_BLK)))
    (dh1,) = _mm(pre + "in_proj_dx", (nt, N_CHIP),
                 [(dproj, pl.BlockSpec((tm, IN_BLK), lambda i, j: (i, j)), wget(l, "w_in", dx2),
                   _w_blk(D_MODEL, IN_BLK,lambda i, j: j), NT)], [],
                 [(_sds((t, D_MODEL), F32), row(D_MODEL))], lambda tot: (tot,), acc=(1, (tm, D_MODEL)))
    deps = on_grads(l, dict(w_in=g_in, w_out=g_out))
    dx, d_mix_norm = _rms_bwd(pre + "mix_norm_bwd", sv["x"], sm["mix_norm_w"][l], dh1, dx1, deps)
    heads = lambda v, n: v.sum(axis=0).reshape(n, HEAD_DIM).sum(axis=0)
    small = dict(
        mix_norm_w=d_mix_norm.sum(axis=0), q_norm_w=heads(d_qw, D_ATTN // HEAD_DIM), k_norm_w=heads(d_kw, N_KV),
        lb=jnp.stack([dlb_fw.sum(axis=0), dlb_bw.sum(axis=0)]), hgrn_gnorm_w=heads(d_gw, D_HGRN // HEAD_DIM),
        conv_dw_w=d_dw[:CONV_W], conv_dw_b=d_cvec[3], conv_ln_w=d_cvec[1], conv_ln_b=d_cvec[2], conv_pw_w=d_pw,
        conv_pw_b=d_cvec[0], attn_out_norm_w=d_aw.sum(axis=0), conv_out_norm_w=d_cw.sum(axis=0),
        ffn_norm_w=d_ffn_norm.sum(axis=0))
    return dx, small


SMALL_ORDER = ("mix_norm_w", "q_norm_w", "k_norm_w", "lb", "hgrn_gnorm_w", "conv_dw_w", "conv_dw_b", "conv_ln_w",
               "conv_ln_b", "conv_pw_w", "conv_pw_b", "attn_out_norm_w", "conv_out_norm_w", "ffn_norm_w")
BIG_ORDER = ("w_in", "w_out", "w_gate", "w_up", "w_down")


def _local_step(x, target, wget, sm, deps, on_grads):
    b, s, d = x.shape
    t = b * s
    cos, sin = _rope_tables(s)
    tabs = dict(cq=jnp.tile(cos, (1, D_ATTN // HEAD_DIM)), sq=jnp.tile(sin, (1, D_ATTN // HEAD_DIM)),
                ck=jnp.tile(cos, (1, N_KV)), sk=jnp.tile(sin, (1, N_KV)))
    cst = dict(attn=_attn_consts(), hg_fw=_hgrn_consts(False), hg_bw=_hgrn_consts(True),
               seg_h=_bf(_seg_matrix(D_HGRN, HEAD_DIM, 1.0 / HEAD_DIM)))
    vec = lambda a: a.reshape(DEPTH, 1, -1)
    smk = dict(sm)
    for n in ("mix_norm_w", "conv_dw_b", "conv_ln_w", "conv_ln_b", "conv_pw_b", "attn_out_norm_w", "conv_out_norm_w",
              "ffn_norm_w"):
        smk[n] = vec(sm[n])
    smk["q_norm_w"] = vec(jnp.tile(sm["q_norm_w"], (1, D_ATTN // HEAD_DIM)))
    smk["k_norm_w"] = vec(jnp.tile(sm["k_norm_w"], (1, N_KV)))
    smk["gnorm_w"] = vec(jnp.tile(sm["hgrn_gnorm_w"], (1, D_HGRN // HEAD_DIM)))
    smk["lb"] = sm["lb"].reshape(DEPTH, 2, 1, D_HGRN)
    smk["conv_dw_w"] = jnp.pad(sm["conv_dw_w"], ((0, 0), (0, 1), (0, 0)))

    h = x.reshape(t, d)
    saved = []
    for l in range(DEPTH):
        h, sv = _layer_fwd(l, h, wget, smk, tabs, cst, b, s, deps if l == 0 else ())
        saved.append(sv)
    dy, sq = _loss_kernel(h, target.reshape(t, d))
    sq_sum = jnp.sum(sq)
    dh = dy
    smalls = [None] * DEPTH
    for l in reversed(range(DEPTH)):
        dh, smalls[l] = _layer_bwd(l, dh, saved[l], wget, smk, tabs, cst, b, s, on_grads)
    return sq_sum, dh.reshape(b, s, d), smalls


HBM_SPEC = pl.BlockSpec(memory_space=pltpu.HBM)


def _exchange(name, arrs, mode):
    n = len(arrs)
    if mode == "gather8":
        flips = [(fx, fy, fc) for fx in (0, 1) for fy in (0, 1) for fc in (0, 1)][1:]
    elif mode == "sibling":
        flips = [(0, 0, 1)]
    else:
        flips = [(1, 0, 0), (0, 1, 0), (1, 1, 0)]
    n_f = len(flips)

    def body(*refs):
        ins, outs = refs[:n], refs[n:2 * n]
        send_sems, recv_sems, local_sems = refs[2 * n:]
        x, y, c = lax.axis_index("x"), lax.axis_index("y"), lax.axis_index("c")

        def slot_of(px, py, pc):
            return (2 * px + py) if mode != "gather8" else (4 * px + 2 * py + pc)

        me = slot_of(x, y, c)
        started = []
        for i in range(n):
            if mode != "sibling":
                src = ins[i].at[me] if mode == "scatter4" else ins[i]
                loc = pltpu.make_async_copy(src, outs[i].at[me], local_sems.at[i])
                loc.start()
                started.append(loc)
        sends, recvs = [], []
        for i in range(n):
            for f, (fx, fy, fc) in enumerate(flips):
                peer = (x ^ fx, y ^ fy, c ^ fc)
                ps = slot_of(*peer)
                if mode == "sibling":
                    src, dst, landed = ins[i], outs[i], outs[i]
                elif mode == "scatter4":
                    src, dst, landed = ins[i].at[ps], outs[i].at[me], outs[i].at[ps]
                else:
                    src, dst, landed = ins[i], outs[i].at[me], outs[i].at[ps]
                k = i * n_f + f
                cp = pltpu.make_async_remote_copy(src_ref=src, dst_ref=dst, send_sem=send_sems.at[k],
                                                  recv_sem=recv_sems.at[k], device_id=peer,
                                                  device_id_type=pl.DeviceIdType.MESH)
                cp.start()
                sends.append(cp)
                recvs.append(pltpu.make_async_remote_copy(src_ref=src, dst_ref=landed, send_sem=send_sems.at[k],
                                                          recv_sem=recv_sems.at[k], device_id=peer,
                                                          device_id_type=pl.DeviceIdType.MESH))
        for cp in sends:
            cp.wait_send()
        for cp in recvs:
            cp.wait_recv()
        for loc in started:
            loc.wait()

    def out_sds(a):
        if mode == "gather4":
            return _sds((N_CHIP,) + a.shape, a.dtype)
        if mode == "gather8":
            return _sds((N_DEV,) + a.shape, a.dtype)
        return _sds(a.shape, a.dtype)

    res = pl.pallas_call(
        body, name=name, in_specs=[HBM_SPEC] * n, out_specs=[HBM_SPEC] * n, out_shape=[out_sds(a) for a in arrs],
        scratch_shapes=[pltpu.SemaphoreType.DMA((n * n_f,)), pltpu.SemaphoreType.DMA((n * n_f,)),
                        pltpu.SemaphoreType.DMA((max(n, 1),))],
    )(*arrs)
    return list(res)


SEM_SPEC = pl.BlockSpec(memory_space=pltpu.SEMAPHORE)
SPLIT_EFFECT = pltpu.SideEffectType.DATAFLOW_SIDE_EFFECTING
CHIP_FLIPS = ((1, 0), (0, 1), (1, 1))


def _chip_copies(src_refs, land_refs, send_sems, recv_sems, scatter):
    x, y, c = lax.axis_index("x"), lax.axis_index("y"), lax.axis_index("c")
    me = 2 * x + y
    out = []
    for i, land in enumerate(land_refs):
        for f, (fx, fy) in enumerate(CHIP_FLIPS):
            peer = (x ^ fx, y ^ fy, c)
            ps = 2 * (x ^ fx) + (y ^ fy)
            src = src_refs[i].at[ps] if scatter else land.at[me]
            k = i * len(CHIP_FLIPS) + f
            kw = dict(send_sem=send_sems.at[k], recv_sem=recv_sems.at[k], device_id=peer,
                      device_id_type=pl.DeviceIdType.MESH)
            out.append((pltpu.make_async_remote_copy(src_ref=src, dst_ref=land.at[me], **kw),
                        pltpu.make_async_remote_copy(src_ref=src, dst_ref=land.at[ps], **kw)))
    return out


def _split_start(name, srcs, lands, scatter):
    n = len(lands)
    n_src = len(srcs)
    n_sem = n * len(CHIP_FLIPS)

    def body(*refs):
        src_refs = refs[:n_src]
        land_refs = refs[n_src:n_src + n]
        send_sems, recv_sems = refs[n_src + n], refs[n_src + n + 1]
        token = refs[-1]
        for start, _ in _chip_copies(src_refs, land_refs, send_sems, recv_sems, scatter):
            start.start()
        token[...] = jnp.zeros_like(token)

    arrs = list(srcs) + list(lands)
    res = pl.pallas_call(
        body, name=name,
        out_shape=(pltpu.SemaphoreType.DMA((n_sem,)), pltpu.SemaphoreType.DMA((n_sem,)),
                   *[pltpu.HBM(a.shape, a.dtype) for a in arrs], _sds((8, LANES), F32)),
        in_specs=[HBM_SPEC] * len(arrs),
        out_specs=(SEM_SPEC, SEM_SPEC, *[HBM_SPEC] * len(arrs), pl.BlockSpec(memory_space=pltpu.VMEM)),
        input_output_aliases={i: 2 + i for i in range(len(arrs))},
        compiler_params=pltpu.CompilerParams(has_side_effects=SPLIT_EFFECT),
    )(*[pltpu.with_memory_space_constraint(a, pltpu.HBM) for a in arrs])
    return dict(send=res[0], recv=res[1], srcs=list(res[2:2 + n_src]), lands=list(res[2 + n_src:2 + n_src + n]),
                token=res[-1], scatter=scatter)


def _split_wait(name, started, after):
    srcs, lands, scatter = started["srcs"], started["lands"], started["scatter"]
    n, n_src = len(lands), len(srcs)

    def body(*refs):
        src_refs = refs[:n_src]
        land_refs = refs[n_src:n_src + n]
        send_sems, recv_sems = refs[n_src + n], refs[n_src + n + 1]
        for _, wait in _chip_copies(src_refs, land_refs, send_sems, recv_sems, scatter):
            wait.wait_send()
            wait.wait_recv()

    arrs = list(srcs) + list(lands)
    res = pl.pallas_call(
        body, name=name, out_shape=tuple(pltpu.HBM(a.shape, a.dtype) for a in arrs),
        in_specs=[HBM_SPEC] * len(arrs) + [SEM_SPEC, SEM_SPEC, pl.BlockSpec(memory_space=pl.ANY)],
        out_specs=tuple([HBM_SPEC] * len(arrs)), input_output_aliases={i: i for i in range(len(arrs))},
        compiler_params=pltpu.CompilerParams(has_side_effects=SPLIT_EFFECT),
    )(*arrs, started["send"], started["recv"], after)
    return list(res[n_src:])


def _flat_tile(rows):
    for cand in (512, 256, 128, 64, 32, 16, 8):
        if rows % cand == 0:
            return cand
    return rows


def _cast_slot(name, a, l, chip):
    r, c = a.shape[0] // DEPTH, a.shape[1]
    tr = _flat_tile(r)

    def body(chip_ref, a_ref, o_ref):
        o_ref[...] = a_ref[...].astype(BF16)

    return pl.pallas_call(
        body, name=name, out_shape=_sds((N_CHIP, r, c), BF16),
        grid_spec=pltpu.PrefetchScalarGridSpec(
            num_scalar_prefetch=1, grid=(r // tr,),
            in_specs=[pl.BlockSpec((tr, c), lambda i, ch: (l * (r // tr) + i, 0))],
            out_specs=pl.BlockSpec((None, tr, c), lambda i, ch: (ch[0], i, 0))),
        compiler_params=_params(("parallel",)))(chip, a)


def _own_slot(name, g, chip):
    n, r, c = g.shape
    tr = _flat_tile(r)

    def body(chip_ref, g_ref, o_ref):
        o_ref[...] = g_ref[...]

    spec = pl.BlockSpec((None, tr, c), lambda i, ch: (ch[0], i, 0))
    return pl.pallas_call(
        body, name=name, out_shape=_sds(g.shape, g.dtype),
        grid_spec=pltpu.PrefetchScalarGridSpec(num_scalar_prefetch=1, grid=(r // tr,), in_specs=[spec], out_specs=spec),
        compiler_params=_params(("parallel",)))(chip, g)


def _sum_layers(name, lands):
    n, r, c = lands[0].shape
    tr = _flat_tile(r)
    nl = len(lands)

    def body(*refs):
        o_ref = refs[-1]
        for k in range(nl):
            @pl.when(pl.program_id(0) == k)
            def _():
                tot = refs[k][0].astype(F32)
                for i in range(1, n):
                    tot = tot + refs[k][i].astype(F32)
                o_ref[...] = tot

    return pl.pallas_call(
        body, name=name, grid=(nl, r // tr),
        in_specs=[pl.BlockSpec((n, tr, c), lambda l, i, k=k: (0, jnp.where(l == k, i, 0), 0)) for k in range(nl)],
        out_specs=pl.BlockSpec((tr, c), lambda l, i: (l * (r // tr) + i, 0)), out_shape=_sds((nl * r, c), F32),
        compiler_params=_params(("arbitrary", "arbitrary")))(*lands)


def _sum_slots(name, a, scale=None):
    n, r, c = a.shape
    tr = _flat_tile(r)

    def body(a_ref, o_ref):
        tot = a_ref[0].astype(F32)
        for i in range(1, n):
            tot = tot + a_ref[i].astype(F32)
        o_ref[...] = tot

    return pl.pallas_call(body, name=name, grid=(r // tr,),
                          in_specs=[pl.BlockSpec((n, tr, c), lambda i: (0, i, 0))],
                          out_specs=pl.BlockSpec((tr, c), lambda i: (i, 0)), out_shape=_sds((r, c), F32),
                          compiler_params=_params(("parallel",)))(a)


def _adamw(name, w, ga, gb, m, v):
    r, c = w.shape
    tr = _flat_tile(r)
    c1 = 1.0 - B1 ** STEP
    c2 = 1.0 - B2 ** STEP
    two = gb is not None

    def body(*refs):
        if two:
            w_ref, ga_ref, gb_ref, m_ref, v_ref, g_out, d_out, m_out, v_out = refs
            g = ga_ref[...] + gb_ref[...]
        else:
            w_ref, ga_ref, m_ref, v_ref, g_out, d_out, m_out, v_out = refs
            g = ga_ref[...]
        mn = B1 * m_ref[...] + (1.0 - B1) * g
        vn = B2 * v_ref[...] + (1.0 - B2) * (g * g)
        g_out[...] = g
        m_out[...] = mn
        v_out[...] = vn
        d_out[...] = -LR * ((mn / c1) / (jnp.sqrt(vn / c2) + ADAM_EPS) + WD * w_ref[...])

    spec = pl.BlockSpec((tr, c), lambda i: (i, 0))
    ins = [w, ga, gb, m, v] if two else [w, ga, m, v]
    return pl.pallas_call(body, name=name, grid=(r // tr,), in_specs=[spec] * len(ins), out_specs=[spec] * 4,
                          out_shape=[_sds((r, c), F32)] * 4, compiler_params=_params(("parallel",)))(*ins)


WEIGHTS = ('mix_norm_w', 'w_in', 'q_norm_w', 'k_norm_w', 'hgrn_lb_logits', 'hgrn_gnorm_w', 'conv_dw_w', 'conv_dw_b',
           'conv_ln_w', 'conv_ln_b', 'conv_pw_w', 'conv_pw_b', 'attn_out_norm_w', 'conv_out_norm_w', 'w_out',
           'ffn_norm_w', 'w_gate', 'w_up', 'w_down')
SHARDED_SMALL = {"hgrn_lb_logits": 2, "conv_dw_w": 2, "conv_pw_w": 1}
LANES = 128


def _pack(parts):
    flat = jnp.concatenate([p.reshape(-1) for p in parts])
    n = flat.shape[0]
    rows = -(-n // (8 * LANES)) * 8
    return jnp.pad(flat, (0, rows * LANES - n)).reshape(rows, LANES)


def _unpack(packed, shapes):
    flat = packed.reshape(-1)
    out, off = [], 0
    for shp in shapes:
        n = int(np.prod(shp))
        out.append(flat[off:off + n].reshape(shp))
        off += n
    return out


def kernel(x, mix_norm_w, w_in, q_norm_w, k_norm_w, hgrn_lb_logits, hgrn_gnorm_w, conv_dw_w, conv_dw_b, conv_ln_w, conv_ln_b, conv_pw_w, conv_pw_b, attn_out_norm_w, conv_out_norm_w, w_out, ffn_norm_w, w_gate, w_up, w_down, loss_target, m_mix_norm_w, m_w_in, m_q_norm_w, m_k_norm_w, m_hgrn_lb_logits, m_hgrn_gnorm_w, m_conv_dw_w, m_conv_dw_b, m_conv_ln_w, m_conv_ln_b, m_conv_pw_w, m_conv_pw_b, m_attn_out_norm_w, m_conv_out_norm_w, m_w_out, m_ffn_norm_w, m_w_gate, m_w_up, m_w_down, v_mix_norm_w, v_w_in, v_q_norm_w, v_k_norm_w, v_hgrn_lb_logits, v_hgrn_gnorm_w, v_conv_dw_w, v_conv_dw_b, v_conv_ln_w, v_conv_ln_b, v_conv_pw_w, v_conv_pw_b, v_attn_out_norm_w, v_conv_out_norm_w, v_w_out, v_ffn_norm_w, v_w_gate, v_w_up, v_w_down):
    w = dict(mix_norm_w=mix_norm_w, w_in=w_in, q_norm_w=q_norm_w, k_norm_w=k_norm_w, hgrn_lb_logits=hgrn_lb_logits,
             hgrn_gnorm_w=hgrn_gnorm_w, conv_dw_w=conv_dw_w, conv_dw_b=conv_dw_b, conv_ln_w=conv_ln_w,
             conv_ln_b=conv_ln_b, conv_pw_w=conv_pw_w, conv_pw_b=conv_pw_b, attn_out_norm_w=attn_out_norm_w,
             conv_out_norm_w=conv_out_norm_w, w_out=w_out, ffn_norm_w=ffn_norm_w, w_gate=w_gate, w_up=w_up,
             w_down=w_down)
    m = dict(mix_norm_w=m_mix_norm_w, w_in=m_w_in, q_norm_w=m_q_norm_w, k_norm_w=m_k_norm_w,
             hgrn_lb_logits=m_hgrn_lb_logits, hgrn_gnorm_w=m_hgrn_gnorm_w, conv_dw_w=m_conv_dw_w,
             conv_dw_b=m_conv_dw_b, conv_ln_w=m_conv_ln_w, conv_ln_b=m_conv_ln_b, conv_pw_w=m_conv_pw_w,
             conv_pw_b=m_conv_pw_b, attn_out_norm_w=m_attn_out_norm_w, conv_out_norm_w=m_conv_out_norm_w,
             w_out=m_w_out, ffn_norm_w=m_ffn_norm_w, w_gate=m_w_gate, w_up=m_w_up, w_down=m_w_down)
    v = dict(mix_norm_w=v_mix_norm_w, w_in=v_w_in, q_norm_w=v_q_norm_w, k_norm_w=v_k_norm_w,
             hgrn_lb_logits=v_hgrn_lb_logits, hgrn_gnorm_w=v_hgrn_gnorm_w, conv_dw_w=v_conv_dw_w,
             conv_dw_b=v_conv_dw_b, conv_ln_w=v_conv_ln_w, conv_ln_b=v_conv_ln_b, conv_pw_w=v_conv_pw_w,
             conv_pw_b=v_conv_pw_b, attn_out_norm_w=v_attn_out_norm_w, conv_out_norm_w=v_conv_out_norm_w,
             w_out=v_w_out, ffn_norm_w=v_ffn_norm_w, w_gate=v_w_gate, w_up=v_w_up, w_down=v_w_down)
    chip = 2 * lax.axis_index("x") + lax.axis_index("y")

    chip1 = chip.reshape(1).astype(jnp.int32)

    flat2 = lambda a: a.reshape(-1, a.shape[-1])
    small_pack = _pack([w[n] for n in SHARDED_SMALL])
    gathered = _exchange("gather_small_weights", [small_pack], "gather4")
    groups = [[(0, "w_in")], [(0, n) for n in BIG_ORDER[1:]], [(1, n) for n in BIG_ORDER]]
    group_of = {key: g for g, keys in enumerate(groups) for key in keys}
    slots = {(l, n): _cast_slot("cast_%s_l%d" % (n, l), flat2(w[n]), l, chip1) for l in range(DEPTH) for n in BIG_ORDER}
    starts = [_split_start("gather_start_g%d" % g, [], [slots[key] for key in keys], False)
              for g, keys in enumerate(groups)]
    got = {}

    def wget(l, name, after):
        if (l, name) not in got:
            g = group_of[(l, name)]
            for key, arr in zip(groups[g], _split_wait("gather_wait_g%d" % g, starts[g], after)):
                got[key] = arr
        return got[(l, name)]

    pending = []

    def on_grads(l, grads):
        names = [n for n in BIG_ORDER if n in grads]
        own = [_own_slot("own_%s_l%d" % (n, l), grads[n], chip1) for n in names]
        st = _split_start("scatter_start_l%d_%s" % (l, names[0]), [grads[n] for n in names], own, True)
        pending.append((l, names, st))
        return [st["token"]]

    parts = [_unpack(gathered[-1][j], [w[n].shape for n in SHARDED_SMALL]) for j in range(N_CHIP)]
    full_small = {n: jnp.concatenate([parts[j][i] for j in range(N_CHIP)], axis=ax)
                  for i, (n, ax) in enumerate(SHARDED_SMALL.items())}
    sm = {n: w[n] for n in WEIGHTS if n not in BIG_ORDER and n not in SHARDED_SMALL}
    sm["conv_dw_w"] = full_small["conv_dw_w"]
    sm["conv_pw_w"] = full_small["conv_pw_w"]
    logits = full_small["hgrn_lb_logits"].reshape(DEPTH * 2, D_HGRN)
    sm["lb"] = _lower_bounds(logits).reshape(DEPTH, 2, D_HGRN)

    sq_sum, grad_x, smalls = _local_step(x, loss_target, wget, sm, [st["token"] for st in starts], on_grads)
    loss = lax.psum(0.5 * sq_sum / D_MODEL, ("x", "y", "c"))

    landed = {}
    for l, names, st in pending:
        for n, arr in zip(names, _split_wait("scatter_wait_l%d_%s" % (l, names[0]), st, grad_x)):
            landed[(l, n)] = arr
    sums = [_sum_layers("sum_" + n, [landed[(l, n)] for l in range(DEPTH)]) for n in BIG_ORDER]
    sib = _exchange("sibling_grads", sums, "sibling")
    out = {}
    for n, ga, gb in zip(BIG_ORDER, sums, sib):
        res = _adamw("adamw_" + n, flat2(w[n]), ga, gb, flat2(m[n]), flat2(v[n]))
        out[n] = [r.reshape(w[n].shape) for r in res]

    small_names = [n for n in WEIGHTS if n not in BIG_ORDER]
    g_pack = _pack([jnp.stack([smalls[l][n] for l in range(DEPTH)]) for n in SMALL_ORDER])
    g_all = _exchange("gather_small_grads", [g_pack], "gather8")[0]
    g_tot = _sum_slots("sum_small", g_all)
    shapes = [(DEPTH,) + tuple(smalls[0][n].shape) for n in SMALL_ORDER]
    g_small = dict(zip(SMALL_ORDER, _unpack(g_tot, shapes)))
    lb_shard = lax.dynamic_slice_in_dim(g_small.pop("lb").reshape(DEPTH * 2, D_HGRN), chip * HEAD_DIM, HEAD_DIM, 1)
    g_small["hgrn_lb_logits"] = _lower_bounds_bwd(hgrn_lb_logits.reshape(DEPTH * 2, HEAD_DIM), lb_shard).reshape(
        hgrn_lb_logits.shape)
    g_small["conv_dw_w"] = lax.dynamic_slice_in_dim(g_small["conv_dw_w"], chip * HEAD_DIM, HEAD_DIM, 2)
    g_small["conv_pw_w"] = lax.dynamic_slice_in_dim(g_small["conv_pw_w"], chip * HEAD_DIM, HEAD_DIM, 1)
    res = _adamw("adamw_small", _pack([w[n] for n in small_names]), _pack([g_small[n] for n in small_names]), None,
                 _pack([m[n] for n in small_names]), _pack([v[n] for n in small_names]))
    unpacked = [_unpack(r, [w[n].shape for n in small_names]) for r in res]
    for i, n in enumerate(small_names):
        out[n] = [unpacked[k][i] for k in range(4)]

    return (loss, grad_x, *[out[n][0] for n in WEIGHTS], *[out[n][1] for n in WEIGHTS],
            *[out[n][2] for n in WEIGHTS], *[out[n][3] for n in WEIGHTS])
```

```python
import functools

import numpy as np
import jax
import jax.numpy as jnp
from jax import lax
from jax.experimental import pallas as pl
from jax.experimental.pallas import tpu as pltpu

F32, BF16 = jnp.float32, jnp.bfloat16

D_MODEL = 1024
DEPTH = 2
GRID_W = 64
D_ATTN, D_HGRN, D_CONV = 512, 256, 256
HEAD_DIM = 64
N_KV = 2
KV_LANES = D_ATTN // N_KV
ROPE_THETA = 10000.0
F_MIN = 1e-6
CONV_W = 31
CONV_PAD = 15
D_FF = 2816
D_IN = 2560
N_CHIP = 4
N_DEV = 8
IN_BLK = D_IN // N_CHIP
FF_BLK = D_FF // N_CHIP
OUT_BLK = D_MODEL // N_CHIP
EPS = 1e-6
LN_EPS = 1e-5
LR, B1, B2, ADAM_EPS, WD, STEP = 0.001, 0.9, 0.999, 1e-08, 0.01, 10
CHUNK = 16
HBLK = 256
CONV_TILE = 128
VMEM_LIMIT = 56 * 1024 * 1024

COL_Q, COL_K, COL_V = 0, 4, 5
COL_HQ, COL_FF, COL_FB, COL_HI, COL_HG, COL_CA, COL_CB = 3, 4, 5, 6, 7, 8, 9


def _params(sem=None):
    return pltpu.CompilerParams(dimension_semantics=sem, vmem_limit_bytes=VMEM_LIMIT)


def _sds(shape, dtype):
    return jax.ShapeDtypeStruct(tuple(shape), dtype)


def _full(shape):
    n = len(shape)
    return pl.BlockSpec(tuple(shape), lambda *_: (0,) * n)


def _sigmoid(x):
    return 1.0 / (1.0 + jnp.exp(-x))


def _silu(x):
    return x * _sigmoid(x)


def _dsilu(x):
    s = _sigmoid(x)
    return s * (1.0 + x * (1.0 - s))


def _rowgroups(v):
    m, c = v.shape
    return v.reshape(m // 8, 8, c).sum(axis=0)


def _split2(x):
    hi = x.astype(BF16)
    lo = (x - hi.astype(F32)).astype(BF16)
    return hi, lo


def _rdot2(x, m):
    hi, lo = _split2(x)
    return (jnp.dot(hi, m, preferred_element_type=F32) + jnp.dot(lo, m, preferred_element_type=F32))


def _ldot3(m, x):
    hi = x.astype(BF16)
    r1 = x - hi.astype(F32)
    mid = r1.astype(BF16)
    lo = (r1 - mid.astype(F32)).astype(BF16)
    return (jnp.dot(m, hi, preferred_element_type=F32) + jnp.dot(m, mid, preferred_element_type=F32)
            + jnp.dot(m, lo, preferred_element_type=F32))


def _dot_nt(a, b):
    return lax.dot_general(a, b, (((1,), (1,)), ((), ())), preferred_element_type=F32)


def _dot_tn(a, b):
    return lax.dot_general(a, b, (((0,), (0,)), ((), ())), preferred_element_type=F32)


def _seg_matrix(n, seg, val):
    i = np.arange(n)
    return ((i[:, None] // seg) == (i[None, :] // seg)).astype(np.float32) * val


def _rot_matrix(n):
    r = np.zeros((n, n), np.float32)
    for i in range(n):
        if (i % 32) < 16:
            r[i + 16, i] = -1.0
        else:
            r[i - 16, i] = 1.0
    return r


def _rep_matrix():
    r = np.zeros((N_KV * HEAD_DIM, D_ATTN), np.float32)
    for kv in range(N_KV):
        for g in range(KV_LANES // HEAD_DIM):
            for d in range(HEAD_DIM):
                r[HEAD_DIM * kv + d, KV_LANES * kv + HEAD_DIM * g + d] = 1.0
    return r


def _cumsum_matrix(rev):
    i = np.arange(HBLK)
    same = (i[:, None] // CHUNK) == (i[None, :] // CHUNK)
    tri = (i[None, :] >= i[:, None]) if rev else (i[None, :] <= i[:, None])
    return (same & tri).astype(np.float32)


def _sel_matrices():
    sel = np.zeros((CHUNK, CHUNK * CHUNK), np.float32)
    selt = np.zeros((CHUNK, CHUNK * CHUNK), np.float32)
    for t in range(CHUNK):
        for s in range(CHUNK):
            sel[t, t * CHUNK + s] = 1.0
            selt[s, t * CHUNK + s] = 1.0
    return sel, selt


def _bf(a):
    return jnp.asarray(a, dtype=BF16)


def _mm(name, grid, pairs, extras, outs, epilogue, acc=None, sem=None):
    n_p, n_e, n_o = len(pairs), len(extras), len(outs)

    def body(*refs):
        ab = refs[:2 * n_p]
        ex = refs[2 * n_p:2 * n_p + n_e]
        out = refs[2 * n_p + n_e:2 * n_p + n_e + n_o]
        scr = refs[2 * n_p + n_e + n_o:]
        tot = None
        for i in range(n_p):
            a = ab[2 * i][...]
            b = ab[2 * i + 1][...]
            if a.ndim == 3:
                a = a.reshape(-1, a.shape[-1])
            if b.ndim == 3:
                b = b.reshape(-1, b.shape[-1])
            r = lax.dot_general(a.astype(BF16), b.astype(BF16), pairs[i][4], preferred_element_type=F32)
            tot = r if tot is None else tot + r

        def finish(total):
            res = epilogue(total, *[e[...] for e in ex])
            for o_ref, val in zip(out, res):
                o_ref[...] = val.astype(o_ref.dtype)

        if acc is None:
            finish(tot)
        else:
            k = pl.program_id(acc[0])

            @pl.when(k == 0)
            def _():
                scr[0][...] = tot

            @pl.when(k > 0)
            def _():
                scr[0][...] += tot

            @pl.when(k == grid[acc[0]] - 1)
            def _():
                finish(scr[0][...])

    args, in_specs = [], []
    for a, a_spec, b, b_spec, _ in pairs:
        args += [a, b]
        in_specs += [a_spec, b_spec]
    for e, e_spec in extras:
        args.append(e)
        in_specs.append(e_spec)
    if sem is None:
        sem = tuple("arbitrary" if (acc is not None and i == acc[0]) else "parallel" for i in range(len(grid)))
    return pl.pallas_call(
        body, name=name, grid=grid, in_specs=in_specs,
        out_specs=[o[1] for o in outs], out_shape=[o[0] for o in outs],
        scratch_shapes=[] if acc is None else [pltpu.VMEM(acc[1], F32)],
        compiler_params=_params(sem),
    )(*args)


NN = (((1,), (0,)), ((), ()))
NT = (((1,), (1,)), ((), ()))
TN = (((0,), (0,)), ((), ()))


def _row_tile(t):
    return min(256, t)


def _rms_fwd(name, x, w, deps=()):
    t, d = x.shape
    tm = _row_tile(t)

    def body(x_ref, w_ref, *rest):
        o_ref = rest[-1]
        xv = x_ref[...]
        r = lax.rsqrt(jnp.mean(xv * xv, axis=-1, keepdims=True) + EPS)
        o_ref[...] = (xv * r * w_ref[...]).astype(BF16)

    return pl.pallas_call(
        body, name=name, grid=(t // tm,),
        in_specs=[pl.BlockSpec((tm, d), lambda i: (i, 0)), _full((1, d))] + [_full(a.shape) for a in deps],
        out_specs=pl.BlockSpec((tm, d), lambda i: (i, 0)), out_shape=_sds((t, d), BF16),
        compiler_params=_params(("parallel",)),
    )(x, w, *deps)


def _rms_bwd(name, x, w, dh, dres, deps=()):
    t, d = x.shape
    tm = _row_tile(t)

    def body(x_ref, w_ref, dh_ref, dres_ref, *rest):
        dx_ref, dw_ref = rest[-2:]
        xv = x_ref[...]
        r = lax.rsqrt(jnp.mean(xv * xv, axis=-1, keepdims=True) + EPS)
        dy = dh_ref[...]
        gw = dy * w_ref[...]
        dx_ref[...] = dres_ref[...] + r * gw - xv * (r * r * r) * jnp.mean(gw * xv, axis=-1, keepdims=True)

        @pl.when(pl.program_id(0) == 0)
        def _():
            dw_ref[...] = jnp.zeros_like(dw_ref)

        dw_ref[...] += _rowgroups(dy * xv * r)

    tile = pl.BlockSpec((tm, d), lambda i: (i, 0))
    return pl.pallas_call(
        body, name=name, grid=(t // tm,),
        in_specs=[tile, _full((1, d)), tile, tile] + [_full(a.shape) for a in deps],
        out_specs=[tile, _full((8, d))], out_shape=[_sds((t, d), F32), _sds((8, d), F32)],
        compiler_params=_params(("arbitrary",)),
    )(x, w, dh, dres, *deps)


def _loss_kernel(y, target):
    t, d = y.shape
    tm = _row_tile(t)

    def body(y_ref, t_ref, dy_ref, acc_ref):
        e = y_ref[...] - t_ref[...]
        dy_ref[...] = e * (1.0 / d)

        @pl.when(pl.program_id(0) == 0)
        def _():
            acc_ref[...] = jnp.zeros_like(acc_ref)

        acc_ref[...] += _rowgroups(e * e)

    tile = pl.BlockSpec((tm, d), lambda i: (i, 0))
    return pl.pallas_call(
        body, name="loss_head", grid=(t // tm,), in_specs=[tile, tile],
        out_specs=[tile, _full((8, d))], out_shape=[_sds((t, d), F32), _sds((8, d), F32)],
        compiler_params=_params(("arbitrary",)),
    )(y, target)


def _rope_tables(s):
    rows = s // GRID_W
    row_id = jnp.repeat(jnp.arange(rows, dtype=F32), GRID_W)
    col_id = jnp.tile(jnp.arange(GRID_W, dtype=F32), rows)
    half = HEAD_DIM // 2
    inv_freq = ROPE_THETA ** (-jnp.arange(0, half, 2, dtype=F32) / half)
    ang_r = row_id[:, None] * inv_freq[None, :]
    ang_c = col_id[:, None] * inv_freq[None, :]
    ang = jnp.concatenate([ang_r, ang_r, ang_c, ang_c], axis=-1)
    return jnp.cos(ang).astype(F32), jnp.sin(ang).astype(F32)


def _attn_consts():
    return dict(
        seg_q=_bf(_seg_matrix(D_ATTN, HEAD_DIM, 1.0 / HEAD_DIM)),
        seg_k=_bf(_seg_matrix(N_KV * HEAD_DIM, HEAD_DIM, 1.0 / HEAD_DIM)),
        rot_q=_bf(_rot_matrix(D_ATTN)), rot_k=_bf(_rot_matrix(N_KV * HEAD_DIM)),
        rep=_bf(_rep_matrix()), rep_t=_bf(_rep_matrix().T))


def _attn_prep(name, proj, s, tabs, qw, kw, ac):
    t = proj.shape[0]
    tm = _row_tile(s)
    nst = s // tm
    kw_ = N_KV * HEAD_DIM

    def body(q_ref, k_ref, v_ref, cq_ref, sq_ref, ck_ref, sk_ref, qw_ref, kw_ref,
             segq_ref, segk_ref, rotq_ref, rotk_ref, rep_ref, qn_ref, kr_ref, vr_ref):
        q = q_ref[...]
        r = lax.rsqrt(jnp.dot((q * q).astype(BF16), segq_ref[...], preferred_element_type=F32) + EPS)
        qn = q * r * qw_ref[...]
        qr = qn * cq_ref[...] + _rdot2(qn, rotq_ref[...]) * sq_ref[...]
        qn_ref[...] = (qr * (HEAD_DIM ** -0.5)).astype(BF16)
        k = k_ref[...]
        rk = lax.rsqrt(jnp.dot((k * k).astype(BF16), segk_ref[...], preferred_element_type=F32) + EPS)
        kn = k * rk * kw_ref[...]
        kr = kn * ck_ref[...] + _rdot2(kn, rotk_ref[...]) * sk_ref[...]
        kr_ref[...] = jnp.dot(kr.astype(BF16), rep_ref[...], preferred_element_type=F32).astype(BF16)
        vr_ref[...] = jnp.dot(v_ref[...].astype(BF16), rep_ref[...], preferred_element_type=F32).astype(BF16)

    wide = pl.BlockSpec((tm, D_ATTN), lambda i: (i, 0))
    tabq = pl.BlockSpec((tm, D_ATTN), lambda i: (i % nst, 0))
    tabk = pl.BlockSpec((tm, kw_), lambda i: (i % nst, 0))
    return pl.pallas_call(
        body, name=name, grid=(t // tm,),
        in_specs=[pl.BlockSpec((tm, D_ATTN), lambda i: (i, COL_Q)), pl.BlockSpec((tm, kw_), lambda i: (i, COL_K)),
                  pl.BlockSpec((tm, kw_), lambda i: (i, COL_V)), tabq, tabq, tabk, tabk,
                  _full((1, D_ATTN)), _full((1, kw_)), _full((D_ATTN, D_ATTN)), _full((kw_, kw_)),
                  _full((D_ATTN, D_ATTN)), _full((kw_, kw_)), _full((kw_, D_ATTN))],
        out_specs=[wide, wide, wide], out_shape=[_sds((t, D_ATTN), BF16)] * 3,
        compiler_params=_params(("parallel",)),
    )(proj, proj, proj, tabs["cq"], tabs["sq"], tabs["ck"], tabs["sk"], qw, kw,
      ac["seg_q"], ac["seg_k"], ac["rot_q"], ac["rot_k"], ac["rep"])


def _attn_prep_bwd(name, proj, s, tabs, qw, kw, ac, dqs, dkr, dvr):
    t = proj.shape[0]
    tm = _row_tile(s)
    nst = s // tm
    kw_ = N_KV * HEAD_DIM
    wout = D_ATTN + 2 * kw_

    def norm_rope_bwd(x, w, cos, sin, seg, rot, d_roped):
        dn = d_roped * cos - _rdot2(d_roped * sin, rot)
        r = lax.rsqrt(jnp.dot((x * x).astype(BF16), seg, preferred_element_type=F32) + EPS)
        gw = dn * w
        dx = r * gw - x * (r * r * r) * _rdot2(gw * x, seg)
        return dx, _rowgroups(dn * x * r)

    def body(q_ref, k_ref, cq_ref, sq_ref, ck_ref, sk_ref, qw_ref, kw_ref, segq_ref, segk_ref, rotq_ref, rotk_ref,
             rept_ref, dqs_ref, dkr_ref, dvr_ref, dp_ref, dqw_ref, dkw_ref):
        dq, dqw = norm_rope_bwd(q_ref[...], qw_ref[...], cq_ref[...], sq_ref[...], segq_ref[...], rotq_ref[...],
                                dqs_ref[...] * (HEAD_DIM ** -0.5))
        dk_roped = _rdot2(dkr_ref[...], rept_ref[...])
        dk, dkw = norm_rope_bwd(k_ref[...], kw_ref[...], ck_ref[...], sk_ref[...], segk_ref[...], rotk_ref[...],
                                dk_roped)
        dv = _rdot2(dvr_ref[...], rept_ref[...])
        dp_ref[:, 0:D_ATTN] = dq.astype(BF16)
        dp_ref[:, D_ATTN:D_ATTN + kw_] = dk.astype(BF16)
        dp_ref[:, D_ATTN + kw_:wout] = dv.astype(BF16)

        @pl.when(pl.program_id(0) == 0)
        def _():
            dqw_ref[...] = jnp.zeros_like(dqw_ref)
            dkw_ref[...] = jnp.zeros_like(dkw_ref)

        dqw_ref[...] += dqw
        dkw_ref[...] += dkw

    wide = pl.BlockSpec((tm, D_ATTN), lambda i: (i, 0))
    tabq = pl.BlockSpec((tm, D_ATTN), lambda i: (i % nst, 0))
    tabk = pl.BlockSpec((tm, kw_), lambda i: (i % nst, 0))
    return pl.pallas_call(
        body, name=name, grid=(t // tm,),
        in_specs=[pl.BlockSpec((tm, D_ATTN), lambda i: (i, COL_Q)), pl.BlockSpec((tm, kw_), lambda i: (i, COL_K)),
                  tabq, tabq, tabk, tabk, _full((1, D_ATTN)), _full((1, kw_)),
                  _full((D_ATTN, D_ATTN)), _full((kw_, kw_)), _full((D_ATTN, D_ATTN)), _full((kw_, kw_)),
                  _full((D_ATTN, kw_)), wide, wide, wide],
        out_specs=[pl.BlockSpec((tm, wout), lambda i: (i, 0)), _full((8, D_ATTN)), _full((8, kw_))],
        out_shape=[_sds((t, wout), BF16), _sds((8, D_ATTN), F32), _sds((8, kw_), F32)],
        compiler_params=_params(("arbitrary",)),
    )(proj, proj, tabs["cq"], tabs["sq"], tabs["ck"], tabs["sk"], qw, kw,
      ac["seg_q"], ac["seg_k"], ac["rot_q"], ac["rot_k"], ac["rep_t"], dqs, dkr, dvr)


def _attn_tile(s):
    return min(256, s)


def _head_masks(shape):
    lane = lax.broadcasted_iota(jnp.int32, shape, 1)
    return [(lane // HEAD_DIM) == g for g in range(KV_LANES // HEAD_DIM)]


def _attn_fwd(name, qn, kr, vr, b, s):
    t = qn.shape[0]
    tq = _attn_tile(s)
    nq = s // tq

    def body(q_ref, k_ref, v_ref, o_ref):
        q = q_ref[...]
        k = k_ref[...]
        v = v_ref[...]
        acc = jnp.zeros((tq, KV_LANES), F32)
        for mask in _head_masks((tq, KV_LANES)):
            sc = _dot_nt(jnp.where(mask, q, jnp.zeros_like(q)), k)
            p = jnp.exp(sc - jnp.max(sc, axis=-1, keepdims=True))
            inv = 1.0 / jnp.sum(p, axis=-1, keepdims=True)
            og = jnp.dot(p.astype(BF16), v, preferred_element_type=F32) * inv
            acc = jnp.where(mask, og, acc)
        o_ref[...] = acc

    return pl.pallas_call(
        body, name=name, grid=(b, N_KV, nq),
        in_specs=[pl.BlockSpec((tq, KV_LANES), lambda bi, kv, i: (bi * nq + i, kv)),
                  pl.BlockSpec((s, KV_LANES), lambda bi, kv, i: (bi, kv)),
                  pl.BlockSpec((s, KV_LANES), lambda bi, kv, i: (bi, kv))],
        out_specs=pl.BlockSpec((tq, KV_LANES), lambda bi, kv, i: (bi * nq + i, kv)),
        out_shape=_sds((t, D_ATTN), F32),
        compiler_params=_params(("parallel", "parallel", "parallel")),
    )(qn, kr, vr)


def _attn_bwd(name, qn, kr, vr, do, b, s):
    t = qn.shape[0]
    tq = _attn_tile(s)
    nq = s // tq

    def body(q_ref, k_ref, v_ref, do_ref, dq_ref, dk_ref, dv_ref):
        @pl.when(pl.program_id(2) == 0)
        def _():
            dk_ref[...] = jnp.zeros_like(dk_ref)
            dv_ref[...] = jnp.zeros_like(dv_ref)

        q = q_ref[...]
        k = k_ref[...]
        v = v_ref[...]
        dout = do_ref[...].astype(BF16)
        dq = jnp.zeros((tq, KV_LANES), F32)
        for mask in _head_masks((tq, KV_LANES)):
            qg = jnp.where(mask, q, jnp.zeros_like(q))
            dog = jnp.where(mask, dout, jnp.zeros_like(dout))
            sc = _dot_nt(qg, k)
            p = jnp.exp(sc - jnp.max(sc, axis=-1, keepdims=True))
            p = p * (1.0 / jnp.sum(p, axis=-1, keepdims=True))
            dp = _dot_nt(dog, v)
            ds = (p * (dp - jnp.sum(p * dp, axis=-1, keepdims=True))).astype(BF16)
            dq = jnp.where(mask, jnp.dot(ds, k, preferred_element_type=F32), dq)
            dk_ref[...] += _dot_tn(ds, qg)
            dv_ref[...] += _dot_tn(p.astype(BF16), dog)
        dq_ref[...] = dq

    qspec = pl.BlockSpec((tq, KV_LANES), lambda bi, kv, i: (bi * nq + i, kv))
    kspec = pl.BlockSpec((s, KV_LANES), lambda bi, kv, i: (bi, kv))
    return pl.pallas_call(
        body, name=name, grid=(b, N_KV, nq),
        in_specs=[qspec, kspec, kspec, qspec],
        out_specs=[qspec, kspec, kspec], out_shape=[_sds((t, D_ATTN), F32)] * 3,
        compiler_params=_params(("parallel", "parallel", "arbitrary")),
    )(qn, kr, vr, do)


def _hgrn_consts(rev):
    sel, selt = _sel_matrices()
    cs = _cumsum_matrix(rev)
    return dict(cs=_bf(cs), cs_t=_bf(cs.T), seg=_bf(_seg_matrix(D_HGRN, HEAD_DIM, 1.0)),
                bd=jnp.asarray(_seg_matrix(D_HGRN, HEAD_DIM, 1.0), F32),
                sel=_bf(sel), selt=_bf(selt), seld=_bf(sel - selt))


def _gates(z, lb):
    sig = _sigmoid(z)
    f = lb + (1.0 - lb) * sig
    g = jnp.log(jnp.maximum(f, F_MIN))
    sn = _sigmoid(-z)
    return sig, f, g, sn, (1.0 - lb) * sn


def _pair_decay(b, rev):
    row = lax.broadcasted_iota(jnp.int32, (CHUNK, D_HGRN), 0)
    parts = []
    for t in range(CHUNK):
        m = (row >= t) if rev else (row <= t)
        parts.append(jnp.where(m, jnp.exp(jnp.minimum(b[t:t + 1, :] - b, 0.0)), 0.0))
    return jnp.concatenate(parts, axis=0)


def _rows_rep(a):
    return jnp.concatenate([jnp.broadcast_to(a[t:t + 1, :], a.shape) for t in range(CHUNK)], axis=0)


def _tile_rows(a):
    return jnp.concatenate([a] * CHUNK, axis=0)


def _hgrn_specs(b, s, rev):
    nb = s // HBLK

    def blk(j):
        return (nb - 1 - j) if rev else j

    def col(c):
        return pl.BlockSpec((HBLK, D_HGRN), lambda bi, j: (bi * nb + blk(j), c))

    return nb, blk, col


def _hgrn_fwd(name, proj, lb, b, s, rev, hc):
    t = proj.shape[0]
    nb, blk, col = _hgrn_specs(b, s, rev)
    n_ch = HBLK // CHUNK
    last = 0 if rev else CHUNK - 1

    def body(q_ref, z_ref, v_ref, lb_ref, cs_ref, seg_ref, bd_ref, sel_ref, o_ref, st_ref, state, b_scr, k_scr):
        @pl.when(pl.program_id(1) == 0)
        def _():
            state[...] = jnp.zeros_like(state)

        st_ref[...] = state[...]
        _, _, g, _, kk = _gates(z_ref[...], lb_ref[...])
        k_scr[...] = kk
        b_scr[...] = _ldot3(cs_ref[...], g)

        def chunk(i, carry):
            c = (n_ch - 1 - i) if rev else i
            rows = pl.ds(pl.multiple_of(c * CHUNK, CHUNK), CHUNK)
            q = q_ref[rows, :]
            k = k_scr[rows, :]
            v = v_ref[rows, :]
            bb = b_scr[rows, :]
            bl = bb[last:last + 1, :]
            pairs = _pair_decay(bb, rev) * _rows_rep(q) * _tile_rows(k)
            a = jnp.dot(pairs.astype(BF16), seg_ref[...], preferred_element_type=F32)
            o_intra = jnp.dot(sel_ref[...], (a * _tile_rows(v)).astype(BF16), preferred_element_type=F32)
            st = state[...]
            o_inter = _dot_nt((q * jnp.exp(bb)).astype(BF16), st.astype(BF16))
            o_ref[rows, :] = o_intra + o_inter
            ke = k * jnp.exp(bl - bb)
            state[...] = st * jnp.exp(bl) + bd_ref[...] * _dot_tn(v.astype(BF16), ke.astype(BF16))
            return carry

        lax.fori_loop(0, n_ch, chunk, 0)

    sq = (D_HGRN, D_HGRN)
    return pl.pallas_call(
        body, name=name, grid=(b, nb),
        in_specs=[col(COL_HQ), col(COL_FB if rev else COL_FF), col(COL_HI), _full((1, D_HGRN)),
                  _full((HBLK, HBLK)), _full(sq), _full(sq), _full((CHUNK, CHUNK * CHUNK))],
        out_specs=[pl.BlockSpec((HBLK, D_HGRN), lambda bi, j: (bi * nb + blk(j), 0)),
                   pl.BlockSpec((None,) + sq, lambda bi, j: (bi * nb + blk(j), 0, 0))],
        out_shape=[_sds((t, D_HGRN), F32), _sds((b * nb,) + sq, F32)],
        scratch_shapes=[pltpu.VMEM(sq, F32), pltpu.VMEM((HBLK, D_HGRN), F32), pltpu.VMEM((HBLK, D_HGRN), F32)],
        compiler_params=_params(("parallel", "arbitrary")),
    )(proj, proj, proj, lb, hc["cs"], hc["seg"], hc["bd"], hc["sel"])


def _hgrn_bwd(name, proj, lb, st_blk, do, dq_prev, dv_prev, b, s, rev, hc):
    t = proj.shape[0]
    nb = s // HBLK
    n_ch = HBLK // CHUNK
    last = 0 if rev else CHUNK - 1

    def blk(j):
        return j if rev else (nb - 1 - j)

    def col(c):
        return pl.BlockSpec((HBLK, D_HGRN), lambda bi, j: (bi * nb + blk(j), c))

    def body(q_ref, z_ref, v_ref, lb_ref, st_ref, do_ref, dqp_ref, dvp_ref, cs_ref, cst_ref, seg_ref, bd_ref,
             sel_ref, selt_ref, seld_ref, dq_ref, dv_ref, dz_ref, dlb_ref,
             dstate, states, b_scr, k_scr, db_scr, dk_scr):
        first = jnp.logical_and(pl.program_id(0) == 0, pl.program_id(1) == 0)

        @pl.when(first)
        def _():
            dlb_ref[...] = jnp.zeros_like(dlb_ref)

        @pl.when(pl.program_id(1) == 0)
        def _():
            dstate[...] = jnp.zeros_like(dstate)

        lbv = lb_ref[...]
        z = z_ref[...]
        sig, f, g, sn, kk = _gates(z, lbv)
        k_scr[...] = kk
        b_scr[...] = _ldot3(cs_ref[...], g)

        def rows_of(c):
            return pl.ds(pl.multiple_of(c * CHUNK, CHUNK), CHUNK)

        def replay(i, st):
            c = (n_ch - 1 - i) if rev else i
            rows = rows_of(c)
            states[c] = st
            bb = b_scr[rows, :]
            bl = bb[last:last + 1, :]
            ke = k_scr[rows, :] * jnp.exp(bl - bb)
            return st * jnp.exp(bl) + bd_ref[...] * _dot_tn(v_ref[rows, :].astype(BF16), ke.astype(BF16))

        lax.fori_loop(0, n_ch, replay, st_ref[...])
        row = lax.broadcasted_iota(jnp.int32, (CHUNK, D_HGRN), 0)

        def chunk(i, carry):
            c = i if rev else (n_ch - 1 - i)
            rows = rows_of(c)
            q = q_ref[rows, :]
            k = k_scr[rows, :]
            v = v_ref[rows, :]
            bb = b_scr[rows, :]
            dout = do_ref[rows, :]
            bl = bb[last:last + 1, :]
            st_p = states[c]
            dst_n = dstate[...]
            eb = jnp.exp(bb)
            ebl = jnp.exp(bl - bb)
            ebl_last = jnp.exp(bl)
            qe = q * eb
            ke = k * ebl
            dob = dout.astype(BF16)
            dstb = dst_n.astype(BF16)
            dqe = jnp.dot(dob, st_p.astype(BF16), preferred_element_type=F32)
            dke = jnp.dot(v.astype(BF16), dstb, preferred_element_type=F32)
            dv = _dot_nt(ke.astype(BF16), dstb)
            dbl = jnp.sum(dst_n * st_p, axis=0, keepdims=True) * ebl_last + jnp.sum(dke * ke, axis=0, keepdims=True)
            dq = dqe * eb
            dk = dke * ebl
            db = dqe * qe - dke * ke
            dec = _pair_decay(bb, rev)
            q_rep = _rows_rep(q)
            k_til = _tile_rows(k)
            do_rep = _rows_rep(dout)
            pairs = dec * q_rep * k_til
            a = jnp.dot(pairs.astype(BF16), seg_ref[...], preferred_element_type=F32)
            wb = jnp.dot((_tile_rows(v) * do_rep).astype(BF16), seg_ref[...], preferred_element_type=F32)
            gdec = wb * dec
            dq = dq + jnp.dot(sel_ref[...], (gdec * k_til).astype(BF16), preferred_element_type=F32)
            dk = dk + jnp.dot(selt_ref[...], (gdec * q_rep).astype(BF16), preferred_element_type=F32)
            dv = dv + jnp.dot(selt_ref[...], (a * do_rep).astype(BF16), preferred_element_type=F32)
            db = db + jnp.dot(seld_ref[...], (wb * pairs).astype(BF16), preferred_element_type=F32)
            db = db + jnp.where(row == last, dbl, 0.0)
            dq_ref[rows, :] = dq + dqp_ref[rows, :]
            dv_ref[rows, :] = dv + dvp_ref[rows, :]
            dk_scr[rows, :] = dk
            db_scr[rows, :] = db
            dstate[...] = dst_n * ebl_last + bd_ref[...] * _dot_tn(dob, qe.astype(BF16))
            return carry

        lax.fori_loop(0, n_ch, chunk, 0)
        hi, lo = _split2(db_scr[...])
        dg = (jnp.dot(cst_ref[...], hi, preferred_element_type=F32)
              + jnp.dot(cst_ref[...], lo, preferred_element_type=F32))
        dgf = jnp.where(f > F_MIN, dg / f, 0.0)
        dk = dk_scr[...]
        dz_ref[...] = dgf * (1.0 - lbv) * sig * (1.0 - sig) - dk * (1.0 - lbv) * sn * (1.0 - sn)
        dlb_ref[...] += _rowgroups(dgf * (1.0 - sig) - dk * sn)

    sq = (D_HGRN, D_HGRN)
    blk0 = pl.BlockSpec((HBLK, D_HGRN), lambda bi, j: (bi * nb + blk(j), 0))
    pairs_shape = (CHUNK, CHUNK * CHUNK)
    return pl.pallas_call(
        body, name=name, grid=(b, nb),
        in_specs=[col(COL_HQ), col(COL_FB if rev else COL_FF), col(COL_HI), _full((1, D_HGRN)),
                  pl.BlockSpec((None,) + sq, lambda bi, j: (bi * nb + blk(j), 0, 0)), blk0, blk0, blk0,
                  _full((HBLK, HBLK)), _full((HBLK, HBLK)), _full(sq), _full(sq),
                  _full(pairs_shape), _full(pairs_shape), _full(pairs_shape)],
        out_specs=[blk0, blk0, blk0, _full((8, D_HGRN))],
        out_shape=[_sds((t, D_HGRN), F32)] * 3 + [_sds((8, D_HGRN), F32)],
        scratch_shapes=[pltpu.VMEM(sq, F32), pltpu.VMEM((n_ch,) + sq, F32)] + [pltpu.VMEM((HBLK, D_HGRN), F32)] * 4,
        compiler_params=_params(("arbitrary", "arbitrary")),
    )(proj, proj, proj, lb, st_blk, do, dq_prev, dv_prev,
      hc["cs"], hc["cs_t"], hc["seg"], hc["bd"], hc["sel"], hc["selt"], hc["seld"])


def _scan_chunk_fwd(c, rev, q_ref, v_ref, k_scr, b_scr, state, o_ref, seg_ref, bd_ref, sel_ref):
    last = 0 if rev else CHUNK - 1
    rows = pl.ds(pl.multiple_of(c * CHUNK, CHUNK), CHUNK)
    q = q_ref[rows, :]
    k = k_scr[rows, :]
    v = v_ref[rows, :]
    bb = b_scr[rows, :]
    bl = bb[last:last + 1, :]
    pairs = _pair_decay(bb, rev) * _rows_rep(q) * _tile_rows(k)
    a = jnp.dot(pairs.astype(BF16), seg_ref[...], preferred_element_type=F32)
    o_intra = jnp.dot(sel_ref[...], (a * _tile_rows(v)).astype(BF16), preferred_element_type=F32)
    st = state[...]
    o_inter = _dot_nt((q * jnp.exp(bb)).astype(BF16), st.astype(BF16))
    o_ref[rows, :] = o_intra + o_inter
    ke = k * jnp.exp(bl - bb)
    state[...] = st * jnp.exp(bl) + bd_ref[...] * _dot_tn(v.astype(BF16), ke.astype(BF16))


def _hgrn_fwd2(name, proj, lb_f, lb_b, b, s, hc_f, hc_b):
    t = proj.shape[0]
    nb = s // HBLK
    n_ch = HBLK // CHUNK

    def body(qf_ref, zf_ref, vf_ref, qb_ref, zb_ref, vb_ref, lbf_ref, lbb_ref, csf_ref, csb_ref, seg_ref, bd_ref,
             sel_ref, of_ref, ob_ref, stf_ref, stb_ref, state_f, state_b, bf_scr, bb_scr, kf_scr, kb_scr):
        @pl.when(pl.program_id(1) == 0)
        def _():
            state_f[...] = jnp.zeros_like(state_f)
            state_b[...] = jnp.zeros_like(state_b)

        stf_ref[...] = state_f[...]
        stb_ref[...] = state_b[...]
        for z_ref, lb_ref, cs_ref, k_scr, b_scr in ((zf_ref, lbf_ref, csf_ref, kf_scr, bf_scr),
                                                     (zb_ref, lbb_ref, csb_ref, kb_scr, bb_scr)):
            _, _, g, _, kk = _gates(z_ref[...], lb_ref[...])
            k_scr[...] = kk
            b_scr[...] = _ldot3(cs_ref[...], g)

        def chunk(i, carry):
            _scan_chunk_fwd(i, False, qf_ref, vf_ref, kf_scr, bf_scr, state_f, of_ref, seg_ref, bd_ref, sel_ref)
            _scan_chunk_fwd(n_ch - 1 - i, True, qb_ref, vb_ref, kb_scr, bb_scr, state_b, ob_ref, seg_ref, bd_ref,
                            sel_ref)
            return carry

        lax.fori_loop(0, n_ch, chunk, 0)

    def col(c, rev):
        return pl.BlockSpec((HBLK, D_HGRN), lambda bi, j: (bi * nb + ((nb - 1 - j) if rev else j), c))

    def st_spec(rev):
        return pl.BlockSpec((None, D_HGRN, D_HGRN), lambda bi, j: (bi * nb + ((nb - 1 - j) if rev else j), 0, 0))

    sq = (D_HGRN, D_HGRN)
    blk = (HBLK, D_HGRN)
    return pl.pallas_call(
        body, name=name, grid=(b, nb),
        in_specs=[col(COL_HQ, False), col(COL_FF, False), col(COL_HI, False),
                  col(COL_HQ, True), col(COL_FB, True), col(COL_HI, True),
                  _full((1, D_HGRN)), _full((1, D_HGRN)), _full((HBLK, HBLK)), _full((HBLK, HBLK)),
                  _full(sq), _full(sq), _full((CHUNK, CHUNK * CHUNK))],
        out_specs=[col(0, False), col(0, True), st_spec(False), st_spec(True)],
        out_shape=[_sds((t, D_HGRN), F32)] * 2 + [_sds((b * nb,) + sq, F32)] * 2,
        scratch_shapes=[pltpu.VMEM(sq, F32)] * 2 + [pltpu.VMEM(blk, F32)] * 4,
        compiler_params=_params(("parallel", "arbitrary")),
    )(proj, proj, proj, proj, proj, proj, lb_f, lb_b, hc_f["cs"], hc_b["cs"], hc_f["seg"], hc_f["bd"], hc_f["sel"])


def _scan_replay(c, rev, st, v_ref, k_scr, b_scr, states, bd_ref):
    last = 0 if rev else CHUNK - 1
    rows = pl.ds(pl.multiple_of(c * CHUNK, CHUNK), CHUNK)
    states[c] = st
    bb = b_scr[rows, :]
    bl = bb[last:last + 1, :]
    ke = k_scr[rows, :] * jnp.exp(bl - bb)
    return st * jnp.exp(bl) + bd_ref[...] * _dot_tn(v_ref[rows, :].astype(BF16), ke.astype(BF16))


def _scan_chunk_bwd(c, rev, q_ref, v_ref, do_ref, k_scr, b_scr, states, dstate, dq_ref, dv_ref, dk_scr, db_scr,
                    seg_ref, bd_ref, sel_ref, selt_ref, seld_ref):
    last = 0 if rev else CHUNK - 1
    row = lax.broadcasted_iota(jnp.int32, (CHUNK, D_HGRN), 0)
    rows = pl.ds(pl.multiple_of(c * CHUNK, CHUNK), CHUNK)
    q = q_ref[rows, :]
    k = k_scr[rows, :]
    v = v_ref[rows, :]
    bb = b_scr[rows, :]
    dout = do_ref[rows, :]
    bl = bb[last:last + 1, :]
    st_p = states[c]
    dst_n = dstate[...]
    eb = jnp.exp(bb)
    ebl = jnp.exp(bl - bb)
    ebl_last = jnp.exp(bl)
    qe = q * eb
    ke = k * ebl
    dob = dout.astype(BF16)
    dstb = dst_n.astype(BF16)
    dqe = jnp.dot(dob, st_p.astype(BF16), preferred_element_type=F32)
    dke = jnp.dot(v.astype(BF16), dstb, preferred_element_type=F32)
    dv = _dot_nt(ke.astype(BF16), dstb)
    dbl = jnp.sum(dst_n * st_p, axis=0, keepdims=True) * ebl_last + jnp.sum(dke * ke, axis=0, keepdims=True)
    dq = dqe * eb
    dk = dke * ebl
    db = dqe * qe - dke * ke
    dec = _pair_decay(bb, rev)
    q_rep = _rows_rep(q)
    k_til = _tile_rows(k)
    do_rep = _rows_rep(dout)
    pairs = dec * q_rep * k_til
    a = jnp.dot(pairs.astype(BF16), seg_ref[...], preferred_element_type=F32)
    wb = jnp.dot((_tile_rows(v) * do_rep).astype(BF16), seg_ref[...], preferred_element_type=F32)
    gdec = wb * dec
    dq = dq + jnp.dot(sel_ref[...], (gdec * k_til).astype(BF16), preferred_element_type=F32)
    dk = dk + jnp.dot(selt_ref[...], (gdec * q_rep).astype(BF16), preferred_element_type=F32)
    dv = dv + jnp.dot(selt_ref[...], (a * do_rep).astype(BF16), preferred_element_type=F32)
    db = db + jnp.dot(seld_ref[...], (wb * pairs).astype(BF16), preferred_element_type=F32)
    db = db + jnp.where(row == last, dbl, 0.0)
    dq_ref[rows, :] = dq
    dv_ref[rows, :] = dv
    dk_scr[rows, :] = dk
    db_scr[rows, :] = db
    dstate[...] = dst_n * ebl_last + bd_ref[...] * _dot_tn(dob, qe.astype(BF16))


def _hgrn_bwd2(name, proj, lb_f, lb_b, st_f, st_b, do, b, s, hc_f, hc_b):
    t = proj.shape[0]
    nb = s // HBLK
    n_ch = HBLK // CHUNK

    def body(qf_ref, zf_ref, vf_ref, dof_ref, stf_ref, qb_ref, zb_ref, vb_ref, dob_ref, stb_ref, lbf_ref, lbb_ref,
             csf_ref, csb_ref, cstf_ref, cstb_ref, seg_ref, bd_ref, sel_ref, selt_ref, seld_ref,
             dqf_ref, dvf_ref, dzf_ref, dqb_ref, dvb_ref, dzb_ref, dlbf_ref, dlbb_ref,
             dstate_f, dstate_b, states_f, states_b, bf_scr, bb_scr, kf_scr, kb_scr, dbf_scr, dbb_scr, dkf_scr,
             dkb_scr):
        first = jnp.logical_and(pl.program_id(0) == 0, pl.program_id(1) == 0)

        @pl.when(first)
        def _():
            dlbf_ref[...] = jnp.zeros_like(dlbf_ref)
            dlbb_ref[...] = jnp.zeros_like(dlbb_ref)

        @pl.when(pl.program_id(1) == 0)
        def _():
            dstate_f[...] = jnp.zeros_like(dstate_f)
            dstate_b[...] = jnp.zeros_like(dstate_b)

        gates = []
        for z_ref, lb_ref, cs_ref, k_scr, b_scr in ((zf_ref, lbf_ref, csf_ref, kf_scr, bf_scr),
                                                     (zb_ref, lbb_ref, csb_ref, kb_scr, bb_scr)):
            sig, f, g, sn, kk = _gates(z_ref[...], lb_ref[...])
            k_scr[...] = kk
            b_scr[...] = _ldot3(cs_ref[...], g)
            gates.append((sig, f, sn))

        def replay(i, carry):
            return (_scan_replay(i, False, carry[0], vf_ref, kf_scr, bf_scr, states_f, bd_ref),
                    _scan_replay(n_ch - 1 - i, True, carry[1], vb_ref, kb_scr, bb_scr, states_b, bd_ref))

        lax.fori_loop(0, n_ch, replay, (stf_ref[...], stb_ref[...]))

        def chunk(i, carry):
            _scan_chunk_bwd(n_ch - 1 - i, False, qf_ref, vf_ref, dof_ref, kf_scr, bf_scr, states_f, dstate_f, dqf_ref,
                            dvf_ref, dkf_scr, dbf_scr, seg_ref, bd_ref, sel_ref, selt_ref, seld_ref)
            _scan_chunk_bwd(i, True, qb_ref, vb_ref, dob_ref, kb_scr, bb_scr, states_b, dstate_b, dqb_ref,
                            dvb_ref, dkb_scr, dbb_scr, seg_ref, bd_ref, sel_ref, selt_ref, seld_ref)
            return carry

        lax.fori_loop(0, n_ch, chunk, 0)
        for (sig, f, sn), lb_ref, cst_ref, db_scr, dk_scr, dz_ref, dlb_ref in (
                (gates[0], lbf_ref, cstf_ref, dbf_scr, dkf_scr, dzf_ref, dlbf_ref),
                (gates[1], lbb_ref, cstb_ref, dbb_scr, dkb_scr, dzb_ref, dlbb_ref)):
            lbv = lb_ref[...]
            hi, lo = _split2(db_scr[...])
            dg = (jnp.dot(cst_ref[...], hi, preferred_element_type=F32)
                  + jnp.dot(cst_ref[...], lo, preferred_element_type=F32))
            dgf = jnp.where(f > F_MIN, dg / f, 0.0)
            dk = dk_scr[...]
            dz_ref[...] = dgf * (1.0 - lbv) * sig * (1.0 - sig) - dk * (1.0 - lbv) * sn * (1.0 - sn)
            dlb_ref[...] += _rowgroups(dgf * (1.0 - sig) - dk * sn)

    def col(c, rev):
        return pl.BlockSpec((HBLK, D_HGRN), lambda bi, j: (bi * nb + (j if rev else (nb - 1 - j)), c))

    def st_spec(rev):
        return pl.BlockSpec((None, D_HGRN, D_HGRN), lambda bi, j: (bi * nb + (j if rev else (nb - 1 - j)), 0, 0))

    sq = (D_HGRN, D_HGRN)
    blk = (HBLK, D_HGRN)
    pairs_shape = (CHUNK, CHUNK * CHUNK)
    return pl.pallas_call(
        body, name=name, grid=(b, nb),
        in_specs=[col(COL_HQ, False), col(COL_FF, False), col(COL_HI, False), col(0, False), st_spec(False),
                  col(COL_HQ, True), col(COL_FB, True), col(COL_HI, True), col(0, True), st_spec(True),
                  _full((1, D_HGRN)), _full((1, D_HGRN)), _full((HBLK, HBLK)), _full((HBLK, HBLK)),
                  _full((HBLK, HBLK)), _full((HBLK, HBLK)), _full(sq), _full(sq),
                  _full(pairs_shape), _full(pairs_shape), _full(pairs_shape)],
        out_specs=[col(0, False)] * 3 + [col(0, True)] * 3 + [_full((8, D_HGRN))] * 2,
        out_shape=[_sds((t, D_HGRN), F32)] * 6 + [_sds((8, D_HGRN), F32)] * 2,
        scratch_shapes=[pltpu.VMEM(sq, F32)] * 2 + [pltpu.VMEM((n_ch,) + sq, F32)] * 2 + [pltpu.VMEM(blk, F32)] * 8,
        compiler_params=_params(("arbitrary", "arbitrary")),
    )(proj, proj, proj, do, st_f, proj, proj, proj, do, st_b, lb_f, lb_b, hc_f["cs"], hc_b["cs"], hc_f["cs_t"],
      hc_b["cs_t"], hc_f["seg"], hc_f["bd"], hc_f["sel"], hc_f["selt"], hc_f["seld"])


def _lower_bounds(logits):
    n = logits.shape[1]

    def body(x_ref, o_ref):
        x = x_ref[...]
        for d in range(2):
            rows = [x[l * 2 + d:l * 2 + d + 1, :] for l in range(DEPTH)]
            mx = functools.reduce(jnp.maximum, rows)
            ex = [jnp.exp(r - mx) for r in rows]
            tot = functools.reduce(lambda a, c: a + c, ex)
            sm = [e / tot for e in ex]
            run = jnp.zeros_like(sm[0])
            for l in range(DEPTH):
                run = run + sm[l]
                o_ref[l * 2 + d:l * 2 + d + 1, :] = run - sm[0]

    return pl.pallas_call(body, name="hgrn_lower_bounds", out_shape=_sds(logits.shape, F32),
                          in_specs=[_full(logits.shape)], out_specs=_full(logits.shape), grid=(1,),
                          compiler_params=_params(("arbitrary",)))(logits)


def _lower_bounds_bwd(logits, dlb):
    def body(x_ref, g_ref, o_ref):
        x = x_ref[...]
        gv = g_ref[...]
        for d in range(2):
            rows = [x[l * 2 + d:l * 2 + d + 1, :] for l in range(DEPTH)]
            gr = [gv[l * 2 + d:l * 2 + d + 1, :] for l in range(DEPTH)]
            mx = functools.reduce(jnp.maximum, rows)
            ex = [jnp.exp(r - mx) for r in rows]
            tot = functools.reduce(lambda a, c: a + c, ex)
            sm = [e / tot for e in ex]
            dsm = []
            for i in range(DEPTH):
                acc = functools.reduce(lambda a, c: a + c, gr[i:])
                if i == 0:
                    acc = acc - functools.reduce(lambda a, c: a + c, gr)
                dsm.append(acc)
            inner = functools.reduce(lambda a, c: a + c, [sm[i] * dsm[i] for i in range(DEPTH)])
            for i in range(DEPTH):
                o_ref[i * 2 + d:i * 2 + d + 1, :] = sm[i] * (dsm[i] - inner)

    return pl.pallas_call(body, name="hgrn_lower_bounds_bwd", out_shape=_sds(logits.shape, F32),
                          in_specs=[_full(logits.shape), _full(logits.shape)], out_specs=_full(logits.shape),
                          grid=(1,), compiler_params=_params(("arbitrary",)))(logits, dlb)


def _conv_rows(s):
    return s + 2 * (CONV_PAD + 1)


def _conv_fwd(name, proj, dw_w, dw_b, ln_w, ln_b, pw_w, pw_b, b, s):
    t = proj.shape[0]
    pad = CONV_PAD + 1
    nt = s // CONV_TILE

    def body(a_ref, g_ref, w_ref, dwb_ref, lnw_ref, lnb_ref, pw_ref, pwb_ref, y_ref, upad, win):
        upad[0:pad, :] = jnp.zeros((pad, D_CONV), F32)
        upad[s + pad:s + 2 * pad, :] = jnp.zeros((pad, D_CONV), F32)

        def fill(i, carry):
            rows = pl.ds(pl.multiple_of(i * CONV_TILE, CONV_TILE), CONV_TILE)
            upad[pl.ds(pl.multiple_of(i * CONV_TILE + pad, pad), CONV_TILE), :] = a_ref[rows, :] * _sigmoid(g_ref[rows, :])
            return carry

        lax.fori_loop(0, nt, fill, 0)

        def tile(i, carry):
            r0 = pl.multiple_of(i * CONV_TILE, CONV_TILE)
            win[...] = upad[pl.ds(r0, CONV_TILE + 2 * pad), :]
            acc = jnp.zeros((CONV_TILE, D_CONV), F32)
            for j in range(CONV_W):
                acc = acc + win[j + 1:j + 1 + CONV_TILE, :] * w_ref[j:j + 1, :]
            c = acc + dwb_ref[...]
            mu = jnp.mean(c, axis=-1, keepdims=True)
            xc = c - mu
            rstd = lax.rsqrt(jnp.mean(xc * xc, axis=-1, keepdims=True) + LN_EPS)
            n = xc * rstd * lnw_ref[...] + lnb_ref[...]
            y_ref[pl.ds(r0, CONV_TILE), :] = (jnp.dot(_silu(n).astype(BF16), pw_ref[...].astype(BF16),
                                                      preferred_element_type=F32) + pwb_ref[...])
            return carry

        lax.fori_loop(0, nt, tile, 0)

    vec = _full((1, D_CONV))
    return pl.pallas_call(
        body, name=name, grid=(b,),
        in_specs=[pl.BlockSpec((s, D_CONV), lambda bi: (bi, COL_CA)), pl.BlockSpec((s, D_CONV), lambda bi: (bi, COL_CB)),
                  _full((CONV_W + 1, D_CONV)), vec, vec, vec, _full((D_CONV, D_CONV)), vec],
        out_specs=pl.BlockSpec((s, D_CONV), lambda bi: (bi, 0)), out_shape=_sds((t, D_CONV), F32),
        scratch_shapes=[pltpu.VMEM((_conv_rows(s), D_CONV), F32), pltpu.VMEM((CONV_TILE + 2 * pad, D_CONV), F32)],
        compiler_params=_params(("parallel",)),
    )(proj, proj, dw_w, dw_b, ln_w, ln_b, pw_w, pw_b)


def _conv_bwd(name, proj, dw_w, dw_b, ln_w, ln_b, pw_w, dy, b, s):
    t = proj.shape[0]
    pad = CONV_PAD + 1
    nt = s // CONV_TILE

    def body(a_ref, g_ref, w_ref, dwb_ref, lnw_ref, lnb_ref, pw_ref, dy_ref, dab_ref, dpw_ref, ddw_ref, dvec_ref,
             upad, dcpad, tap_acc, win, dwin):
        @pl.when(pl.program_id(0) == 0)
        def _():
            dpw_ref[...] = jnp.zeros_like(dpw_ref)
            ddw_ref[...] = jnp.zeros_like(ddw_ref)
            dvec_ref[...] = jnp.zeros_like(dvec_ref)

        zeros = jnp.zeros((pad, D_CONV), F32)
        upad[0:pad, :] = zeros
        upad[s + pad:s + 2 * pad, :] = zeros
        dcpad[0:pad, :] = zeros
        dcpad[s + pad:s + 2 * pad, :] = zeros
        tap_acc[...] = jnp.zeros_like(tap_acc)

        def inner(i):
            return pl.ds(pl.multiple_of(i * CONV_TILE + pad, pad), CONV_TILE)

        def fill(i, carry):
            rows = pl.ds(pl.multiple_of(i * CONV_TILE, CONV_TILE), CONV_TILE)
            upad[inner(i), :] = a_ref[rows, :] * _sigmoid(g_ref[rows, :])
            return carry

        lax.fori_loop(0, nt, fill, 0)

        def tile_a(i, carry):
            r0 = pl.multiple_of(i * CONV_TILE, CONV_TILE)
            win[...] = upad[pl.ds(r0, CONV_TILE + 2 * pad), :]
            acc = jnp.zeros((CONV_TILE, D_CONV), F32)
            for j in range(CONV_W):
                acc = acc + win[j + 1:j + 1 + CONV_TILE, :] * w_ref[j:j + 1, :]
            c = acc + dwb_ref[...]
            mu = jnp.mean(c, axis=-1, keepdims=True)
            xc = c - mu
            rstd = lax.rsqrt(jnp.mean(xc * xc, axis=-1, keepdims=True) + LN_EPS)
            xhat = xc * rstd
            n = xhat * lnw_ref[...] + lnb_ref[...]
            dyt = dy_ref[pl.ds(r0, CONV_TILE), :]
            dyb = dyt.astype(BF16)
            dpw_ref[...] += _dot_tn(_silu(n).astype(BF16), dyb)
            dn = _dot_nt(dyb, pw_ref[...].astype(BF16)) * _dsilu(n)
            dxh = dn * lnw_ref[...]
            dc = rstd * (dxh - jnp.mean(dxh, axis=-1, keepdims=True)
                         - xhat * jnp.mean(dxh * xhat, axis=-1, keepdims=True))
            dcpad[inner(i), :] = dc
            dvec_ref[0:1, :] += jnp.sum(dyt, axis=0, keepdims=True)
            dvec_ref[1:2, :] += jnp.sum(dn * xhat, axis=0, keepdims=True)
            dvec_ref[2:3, :] += jnp.sum(dn, axis=0, keepdims=True)
            dvec_ref[3:4, :] += jnp.sum(dc, axis=0, keepdims=True)
            return carry

        lax.fori_loop(0, nt, tile_a, 0)

        def tile_b(i, carry):
            r0 = pl.multiple_of(i * CONV_TILE, CONV_TILE)
            win[...] = upad[pl.ds(r0, CONV_TILE + 2 * pad), :]
            dwin[...] = dcpad[pl.ds(r0, CONV_TILE + 2 * pad), :]
            dct = dwin[pad:pad + CONV_TILE, :]
            du = jnp.zeros((CONV_TILE, D_CONV), F32)
            for j in range(CONV_W):
                du = du + dwin[2 * pad - 1 - j:2 * pad - 1 - j + CONV_TILE, :] * w_ref[j:j + 1, :]
                tap_acc[8 * j:8 * j + 8, :] += _rowgroups(dct * win[j + 1:j + 1 + CONV_TILE, :])
            rows = pl.ds(r0, CONV_TILE)
            sg = _sigmoid(g_ref[rows, :])
            dab_ref[rows, 0:D_CONV] = (du * sg).astype(BF16)
            dab_ref[rows, D_CONV:2 * D_CONV] = (du * a_ref[rows, :] * sg * (1.0 - sg)).astype(BF16)
            return carry

        lax.fori_loop(0, nt, tile_b, 0)
        for j in range(CONV_W):
            ddw_ref[j:j + 1, :] += jnp.sum(tap_acc[8 * j:8 * j + 8, :], axis=0, keepdims=True)

    vec = _full((1, D_CONV))
    return pl.pallas_call(
        body, name=name, grid=(b,),
        in_specs=[pl.BlockSpec((s, D_CONV), lambda bi: (bi, COL_CA)), pl.BlockSpec((s, D_CONV), lambda bi: (bi, COL_CB)),
                  _full((CONV_W + 1, D_CONV)), vec, vec, vec, _full((D_CONV, D_CONV)),
                  pl.BlockSpec((s, D_CONV), lambda bi: (bi, 0))],
        out_specs=[pl.BlockSpec((s, 2 * D_CONV), lambda bi: (bi, 0)), _full((D_CONV, D_CONV)),
                   _full((CONV_W + 1, D_CONV)), _full((8, D_CONV))],
        out_shape=[_sds((t, 2 * D_CONV), BF16), _sds((D_CONV, D_CONV), F32), _sds((CONV_W + 1, D_CONV), F32),
                   _sds((8, D_CONV), F32)],
        scratch_shapes=[pltpu.VMEM((_conv_rows(s), D_CONV), F32), pltpu.VMEM((_conv_rows(s), D_CONV), F32),
                        pltpu.VMEM((8 * CONV_W, D_CONV), F32), pltpu.VMEM((CONV_TILE + 2 * pad, D_CONV), F32),
                        pltpu.VMEM((CONV_TILE + 2 * pad, D_CONV), F32)],
        compiler_params=_params(("arbitrary",)),
    )(proj, proj, dw_w, dw_b, ln_w, ln_b, pw_w, dy)


def _mix_fwd(name, y_attn, o_fw, o_bw, proj, y_conv, aw, gw, cw, seg):
    t = y_attn.shape[0]
    tm = _row_tile(t)

    def body(ya_ref, of_ref, ob_ref, hg_ref, yc_ref, aw_ref, gw_ref, cw_ref, seg_ref, o_ref):
        ya = ya_ref[...]
        ra = lax.rsqrt(jnp.mean(ya * ya, axis=-1, keepdims=True) + EPS)
        o_ref[:, 0:D_ATTN] = (ya * ra * aw_ref[...]).astype(BF16)
        o = of_ref[...] + ob_ref[...]
        ro = lax.rsqrt(jnp.dot((o * o).astype(BF16), seg_ref[...], preferred_element_type=F32) + EPS)
        o_ref[:, D_ATTN:D_ATTN + D_HGRN] = (o * ro * gw_ref[...] * _silu(hg_ref[...])).astype(BF16)
        yc = yc_ref[...]
        rc = lax.rsqrt(jnp.mean(yc * yc, axis=-1, keepdims=True) + EPS)
        o_ref[:, D_ATTN + D_HGRN:D_MODEL] = (yc * rc * cw_ref[...]).astype(BF16)

    def tile(w, c=0):
        return pl.BlockSpec((tm, w), lambda i: (i, c))

    return pl.pallas_call(
        body, name=name, grid=(t // tm,),
        in_specs=[tile(D_ATTN), tile(D_HGRN), tile(D_HGRN), tile(D_HGRN, COL_HG), tile(D_CONV),
                  _full((1, D_ATTN)), _full((1, D_HGRN)), _full((1, D_CONV)), _full((D_HGRN, D_HGRN))],
        out_specs=tile(D_MODEL), out_shape=_sds((t, D_MODEL), BF16),
        compiler_params=_params(("parallel",)),
    )(y_attn, o_fw, o_bw, proj, y_conv, aw, gw, cw, seg)


def _mix_bwd(name, dmix, y_attn, o_fw, o_bw, proj, y_conv, aw, gw, cw, seg):
    t = y_attn.shape[0]
    tm = _row_tile(t)

    def rms_bwd(x, w, dy):
        r = lax.rsqrt(jnp.mean(x * x, axis=-1, keepdims=True) + EPS)
        gwv = dy * w
        return r * gwv - x * (r * r * r) * jnp.mean(gwv * x, axis=-1, keepdims=True), _rowgroups(dy * x * r)

    def body(dm_ref, ya_ref, of_ref, ob_ref, hg_ref, yc_ref, aw_ref, gw_ref, cw_ref, seg_ref,
             dya_ref, do_ref, dhg_ref, dyc_ref, daw_ref, dgw_ref, dcw_ref):
        @pl.when(pl.program_id(0) == 0)
        def _():
            daw_ref[...] = jnp.zeros_like(daw_ref)
            dgw_ref[...] = jnp.zeros_like(dgw_ref)
            dcw_ref[...] = jnp.zeros_like(dcw_ref)

        dya, daw = rms_bwd(ya_ref[...], aw_ref[...], dm_ref[:, 0:D_ATTN])
        dya_ref[...] = dya
        daw_ref[...] += daw
        dyc, dcw = rms_bwd(yc_ref[...], cw_ref[...], dm_ref[:, D_ATTN + D_HGRN:D_MODEL])
        dyc_ref[...] = dyc
        dcw_ref[...] += dcw
        d2 = dm_ref[:, D_ATTN:D_ATTN + D_HGRN]
        o = of_ref[...] + ob_ref[...]
        hg = hg_ref[...]
        ro = lax.rsqrt(jnp.dot((o * o).astype(BF16), seg_ref[...], preferred_element_type=F32) + EPS)
        dn = d2 * _silu(hg)
        dhg_ref[...] = (d2 * o * ro * gw_ref[...] * _dsilu(hg)).astype(BF16)
        gwv = dn * gw_ref[...]
        do_ref[...] = ro * gwv - o * (ro * ro * ro) * _rdot2(gwv * o, seg_ref[...])
        dgw_ref[...] += _rowgroups(dn * o * ro)

    def tile(w, c=0):
        return pl.BlockSpec((tm, w), lambda i: (i, c))

    return pl.pallas_call(
        body, name=name, grid=(t // tm,),
        in_specs=[tile(D_MODEL), tile(D_ATTN), tile(D_HGRN), tile(D_HGRN), tile(D_HGRN, COL_HG), tile(D_CONV),
                  _full((1, D_ATTN)), _full((1, D_HGRN)), _full((1, D_CONV)), _full((D_HGRN, D_HGRN))],
        out_specs=[tile(D_ATTN), tile(D_HGRN), tile(D_HGRN), tile(D_CONV),
                   _full((8, D_ATTN)), _full((8, D_HGRN)), _full((8, D_CONV))],
        out_shape=[_sds((t, D_ATTN), F32), _sds((t, D_HGRN), F32), _sds((t, D_HGRN), BF16), _sds((t, D_CONV), F32),
                   _sds((8, D_ATTN), F32), _sds((8, D_HGRN), F32), _sds((8, D_CONV), F32)],
        compiler_params=_params(("arbitrary",)),
    )(dmix, y_attn, o_fw, o_bw, proj, y_conv, aw, gw, cw, seg)


def _hgrn_dproj(name, dq_f, dq_b, dz_fw, dz_bw, dv_f, dv_b, dhg):
    t = dq_f.shape[0]
    tm = _row_tile(t)

    def body(qf_ref, qb_ref, zf_ref, zb_ref, vf_ref, vb_ref, hg_ref, o_ref):
        cols = (qf_ref[...] + qb_ref[...], zf_ref[...], zb_ref[...], vf_ref[...] + vb_ref[...], hg_ref[...])
        for i, val in enumerate(cols):
            o_ref[:, i * D_HGRN:(i + 1) * D_HGRN] = val.astype(BF16)

    tile = pl.BlockSpec((tm, D_HGRN), lambda i: (i, 0))
    return pl.pallas_call(
        body, name=name, grid=(t // tm,), in_specs=[tile] * 7,
        out_specs=pl.BlockSpec((tm, 5 * D_HGRN), lambda i: (i, 0)), out_shape=_sds((t, 5 * D_HGRN), BF16),
        compiler_params=_params(("parallel",)),
    )(dq_f, dq_b, dz_fw, dz_bw, dv_f, dv_b, dhg)


def _mm_tile(t):
    return min(512, t)


def _resident(shape):
    n = len(shape)
    return pl.BlockSpec(tuple(shape), lambda *_: (0,) * n, pipeline_mode=pl.Buffered(1))


def _w_blk(rows, cols, j_of):
    return pl.BlockSpec((None, rows, cols), lambda *g: (j_of(*g), 0, 0))


def _layer_fwd(l, x, wget, sm, tabs, cst, b, s, deps):
    t = x.shape[0]
    tm = _mm_tile(t)
    nt = t // tm
    pre = "l%d_" % l
    row = lambda w: pl.BlockSpec((tm, w), lambda i, *_: (i, 0))

    h1 = _rms_fwd(pre + "mix_norm", x, sm["mix_norm_w"][l], deps)
    def in_body(h_ref, w_ref, o_ref):
        hv = h_ref[...]
        for j in range(N_CHIP):
            o_ref[:, j * IN_BLK:(j + 1) * IN_BLK] = jnp.dot(hv, w_ref[j], preferred_element_type=F32)

    w_in = wget(l, "w_in", h1)
    proj = pl.pallas_call(
        in_body, name=pre + "in_proj", grid=(nt,), in_specs=[row(D_MODEL), _resident(w_in.shape)],
        out_specs=row(D_IN), out_shape=_sds((t, D_IN), F32), compiler_params=_params(("parallel",)),
    )(h1, w_in)
    qn, kr, vr = _attn_prep(pre + "attn_prep", proj, s, tabs, sm["q_norm_w"][l], sm["k_norm_w"][l], cst["attn"])
    y_attn = _attn_fwd(pre + "attn", qn, kr, vr, b, s)
    o_fw, o_bw, st_fw, st_bw = _hgrn_fwd2(pre + "hgrn", proj, sm["lb"][l][0], sm["lb"][l][1], b, s, cst["hg_fw"],
                                          cst["hg_bw"])
    y_conv = _conv_fwd(pre + "conv", proj, sm["conv_dw_w"][l], sm["conv_dw_b"][l], sm["conv_ln_w"][l],
                       sm["conv_ln_b"][l], sm["conv_pw_w"][l], sm["conv_pw_b"][l], b, s)
    mixed = _mix_fwd(pre + "mix", y_attn, o_fw, o_bw, proj, y_conv, sm["attn_out_norm_w"][l], sm["gnorm_w"][l],
                     sm["conv_out_norm_w"][l], cst["seg_h"])
    (x1,) = _mm(pre + "out_proj", (nt,),
                [(mixed, row(D_MODEL), wget(l, "w_out", mixed),
                  pl.BlockSpec((N_CHIP, OUT_BLK, D_MODEL), lambda i: (0, 0, 0)), NN)],
                [(x, row(D_MODEL))], [(_sds((t, D_MODEL), F32), row(D_MODEL))],
                lambda tot, xr: (xr + tot,))
    h2 = _rms_fwd(pre + "ffn_norm", x1, sm["ffn_norm_w"][l])
    ff3 = pl.BlockSpec((N_CHIP, tm, FF_BLK), lambda i: (0, i, 0))
    ffs = _sds((N_CHIP, t, FF_BLK), BF16)

    def gu_body(h_ref, wg_ref, wu_ref, g_ref, u_ref, a_ref):
        hv = h_ref[...]
        for j in range(N_CHIP):
            gv = jnp.dot(hv, wg_ref[j], preferred_element_type=F32)
            uv = jnp.dot(hv, wu_ref[j], preferred_element_type=F32)
            g_ref[j] = gv.astype(BF16)
            u_ref[j] = uv.astype(BF16)
            a_ref[j] = (_silu(gv) * uv).astype(BF16)

    w_gate, w_up = wget(l, "w_gate", h2), wget(l, "w_up", h2)
    gate, up, act = pl.pallas_call(
        gu_body, name=pre + "ffn_gate_up", grid=(nt,),
        in_specs=[row(D_MODEL), _resident(w_gate.shape), _resident(w_up.shape)],
        out_specs=[ff3, ff3, ff3], out_shape=[ffs, ffs, ffs], compiler_params=_params(("parallel",)),
    )(h2, w_gate, w_up)

    def down_body(a_ref, w_ref, x_ref, o_ref):
        tot = x_ref[...]
        for j in range(N_CHIP):
            tot = tot + jnp.dot(a_ref[j], w_ref[j], preferred_element_type=F32)
        o_ref[...] = tot

    w_down = wget(l, "w_down", act)
    x2 = pl.pallas_call(
        down_body, name=pre + "ffn_down", grid=(nt,), in_specs=[ff3, _resident(w_down.shape), row(D_MODEL)],
        out_specs=row(D_MODEL), out_shape=_sds((t, D_MODEL), F32), compiler_params=_params(("parallel",)),
    )(act, w_down, x1)
    saved = dict(x=x, h1=h1, proj=proj, qn=qn, kr=kr, vr=vr, y_attn=y_attn, o_fw=o_fw, o_bw=o_bw, st_fw=st_fw,
                 st_bw=st_bw, y_conv=y_conv, mixed=mixed, x1=x1, h2=h2, gate=gate, up=up, act=act)
    return x2, saved


def _layer_bwd(l, dx2, sv, wget, sm, tabs, cst, b, s, on_grads):
    t = dx2.shape[0]
    tm = _mm_tile(t)
    nt = t // tm
    pre = "l%d_" % l
    tk = min(1024, t)
    nk = t // tk
    row = lambda w: pl.BlockSpec((tm, w), lambda i, *_: (i, 0))
    ff3 = pl.BlockSpec((N_CHIP, tm, FF_BLK), lambda i: (0, i, 0))
    ffs = _sds((N_CHIP, t, FF_BLK), BF16)

    w_down, w_gate, w_up = wget(l, "w_down", dx2), wget(l, "w_gate", dx2), wget(l, "w_up", dx2)

    def ddx_body(dx_ref, w_ref, g_ref, u_ref, dg_ref, du_ref):
        dxb = dx_ref[...].astype(BF16)
        for j in range(N_CHIP):
            da = _dot_nt(dxb, w_ref[j])
            g = g_ref[j].astype(F32)
            dg_ref[j] = (da * u_ref[j].astype(F32) * _dsilu(g)).astype(BF16)
            du_ref[j] = (da * _silu(g)).astype(BF16)

    dgate, dup = pl.pallas_call(
        ddx_body, name=pre + "ffn_down_dx", grid=(nt,), in_specs=[row(D_MODEL), _resident(w_down.shape), ff3, ff3],
        out_specs=[ff3, ff3], out_shape=[ffs, ffs], compiler_params=_params(("parallel",)),
    )(dx2, w_down, sv["gate"], sv["up"])
    colt = lambda w: pl.BlockSpec((tk, w), lambda j, k: (k, 0))
    fft = pl.BlockSpec((None, tk, FF_BLK), lambda j, k: (j, k, 0))
    (g_down,) = _mm(pre + "ffn_down_dw", (N_CHIP, nk), [(sv["act"], fft, dx2, colt(D_MODEL), TN)], [],
                    [(_sds((N_CHIP, FF_BLK, D_MODEL), BF16), pl.BlockSpec((None, FF_BLK, D_MODEL), lambda j, k: (j, 0, 0)))],
                    lambda tot: (tot,), acc=(1, (FF_BLK, D_MODEL)))
    wff = pl.BlockSpec((None, D_MODEL, FF_BLK), lambda j, k: (j, 0, 0))
    (g_gate,) = _mm(pre + "ffn_gate_dw", (N_CHIP, nk), [(sv["h2"], colt(D_MODEL), dgate, fft, TN)], [],
                    [(_sds((N_CHIP, D_MODEL, FF_BLK), BF16), wff)], lambda tot: (tot,), acc=(1, (D_MODEL, FF_BLK)))
    (g_up,) = _mm(pre + "ffn_up_dw", (N_CHIP, nk), [(sv["h2"], colt(D_MODEL), dup, fft, TN)], [],
                  [(_sds((N_CHIP, D_MODEL, FF_BLK), BF16), wff)], lambda tot: (tot,), acc=(1, (D_MODEL, FF_BLK)))

    def dh_body(dg_ref, du_ref, wg_ref, wu_ref, o_ref):
        tot = None
        for j in range(N_CHIP):
            r = _dot_nt(dg_ref[j], wg_ref[j]) + _dot_nt(du_ref[j], wu_ref[j])
            tot = r if tot is None else tot + r
        o_ref[...] = tot

    dh2 = pl.pallas_call(
        dh_body, name=pre + "ffn_dh", grid=(nt,),
        in_specs=[ff3, ff3, _resident(w_gate.shape), _resident(w_up.shape)],
        out_specs=row(D_MODEL), out_shape=_sds((t, D_MODEL), F32), compiler_params=_params(("parallel",)),
    )(dgate, dup, w_gate, w_up)
    deps = on_grads(l, dict(w_gate=g_gate, w_up=g_up, w_down=g_down))
    dx1, d_ffn_norm = _rms_bwd(pre + "ffn_norm_bwd", sv["x1"], sm["ffn_norm_w"][l], dh2, dx2, deps)

    (dmix,) = _mm(pre + "out_proj_dx", (nt,),
                  [(dx1, row(D_MODEL), wget(l, "w_out", dx2),
                    pl.BlockSpec((N_CHIP, OUT_BLK, D_MODEL), lambda i: (0, 0, 0)), NT)], [],
                  [(_sds((t, D_MODEL), F32), row(D_MODEL))], lambda tot: (tot,))
    (g_out,) = _mm(pre + "out_proj_dw", (N_CHIP, nk),
                   [(sv["mixed"], pl.BlockSpec((tk, OUT_BLK), lambda j, k: (k, j)), dx1, colt(D_MODEL), TN)], [],
                   [(_sds((N_CHIP, OUT_BLK, D_MODEL), BF16), pl.BlockSpec((None, OUT_BLK, D_MODEL), lambda j, k: (j, 0, 0)))],
                   lambda tot: (tot,), acc=(1, (OUT_BLK, D_MODEL)))
    proj = sv["proj"]
    dya, do_h, dhg, dyc, d_aw, d_gw, d_cw = _mix_bwd(
        pre + "mix_bwd", dmix, sv["y_attn"], sv["o_fw"], sv["o_bw"], proj, sv["y_conv"],
        sm["attn_out_norm_w"][l], sm["gnorm_w"][l], sm["conv_out_norm_w"][l], cst["seg_h"])
    dqs, dkr, dvr = _attn_bwd(pre + "attn_bwd", sv["qn"], sv["kr"], sv["vr"], dya, b, s)
    dp_attn, d_qw, d_kw = _attn_prep_bwd(pre + "attn_prep_bwd", proj, s, tabs, sm["q_norm_w"][l], sm["k_norm_w"][l],
                                         cst["attn"], dqs, dkr, dvr)
    dq_f, dv_f, dz_fw, dq_b, dv_b, dz_bw, dlb_fw, dlb_bw = _hgrn_bwd2(
        pre + "hgrn_bwd", proj, sm["lb"][l][0], sm["lb"][l][1], sv["st_fw"], sv["st_bw"], do_h, b, s,
        cst["hg_fw"], cst["hg_bw"])
    dp_hgrn = _hgrn_dproj(pre + "hgrn_dproj", dq_f, dq_b, dz_fw, dz_bw, dv_f, dv_b, dhg)
    dp_conv, d_pw, d_dw, d_cvec = _conv_bwd(pre + "conv_bwd", proj, sm["conv_dw_w"][l], sm["conv_dw_b"][l],
                                            sm["conv_ln_w"][l], sm["conv_ln_b"][l], sm["conv_pw_w"][l], dyc, b, s)
    dproj = jnp.concatenate([dp_attn, dp_hgrn, dp_conv], axis=1)

    (g_in,) = _mm(pre + "in_proj_dw", (N_CHIP, nk),
                  [(sv["h1"], colt(D_MODEL), dproj, pl.BlockSpec((tk, IN_BLK), lambda j, k: (k, j)), TN)], [],
                  [(_sds((N_CHIP, D_MODEL, IN_BLK), BF16), pl.BlockSpec((None, D_MODEL, IN_BLK), lambda j, k: (j, 0, 0)))],
                  lambda tot: (tot,), acc=(1, (D_MODEL, IN_BLK)))

    def indx_body(dp_ref, w_ref, o_ref):
        tot = None
        for j in range(N_CHIP):
            r = _dot_nt(dp_ref[:, j * IN_BLK:(j + 1) * IN_BLK], w_ref[j])
            tot = r if tot is None else tot + r
        o_ref[...] = tot

    w_in = wget(l, "w_in", dx2)
    dh1 = pl.pallas_call(
        indx_body, name=pre + "in_proj_dx", grid=(nt,), in_specs=[row(D_IN), _resident(w_in.shape)],
        out_specs=row(D_MODEL), out_shape=_sds((t, D_MODEL), F32), compiler_params=_params(("parallel",)),
    )(dproj, w_in)
    deps = on_grads(l, dict(w_in=g_in, w_out=g_out))
    dx, d_mix_norm = _rms_bwd(pre + "mix_norm_bwd", sv["x"], sm["mix_norm_w"][l], dh1, dx1, deps)
    heads = lambda v, n: v.sum(axis=0).reshape(n, HEAD_DIM).sum(axis=0)
    small = dict(
        mix_norm_w=d_mix_norm.sum(axis=0), q_norm_w=heads(d_qw, D_ATTN // HEAD_DIM), k_norm_w=heads(d_kw, N_KV),
        lb=jnp.stack([dlb_fw.sum(axis=0), dlb_bw.sum(axis=0)]), hgrn_gnorm_w=heads(d_gw, D_HGRN // HEAD_DIM),
        conv_dw_w=d_dw[:CONV_W], conv_dw_b=d_cvec[3], conv_ln_w=d_cvec[1], conv_ln_b=d_cvec[2], conv_pw_w=d_pw,
        conv_pw_b=d_cvec[0], attn_out_norm_w=d_aw.sum(axis=0), conv_out_norm_w=d_cw.sum(axis=0),
        ffn_norm_w=d_ffn_norm.sum(axis=0))
    return dx, small


SMALL_ORDER = ("mix_norm_w", "q_norm_w", "k_norm_w", "lb", "hgrn_gnorm_w", "conv_dw_w", "conv_dw_b", "conv_ln_w",
               "conv_ln_b", "conv_pw_w", "conv_pw_b", "attn_out_norm_w", "conv_out_norm_w", "ffn_norm_w")
BIG_ORDER = ("w_in", "w_out", "w_gate", "w_up", "w_down")


def _local_step(x, target, wget, sm, deps, on_grads):
    b, s, d = x.shape
    t = b * s
    cos, sin = _rope_tables(s)
    tabs = dict(cq=jnp.tile(cos, (1, D_ATTN // HEAD_DIM)), sq=jnp.tile(sin, (1, D_ATTN // HEAD_DIM)),
                ck=jnp.tile(cos, (1, N_KV)), sk=jnp.tile(sin, (1, N_KV)))
    cst = dict(attn=_attn_consts(), hg_fw=_hgrn_consts(False), hg_bw=_hgrn_consts(True),
               seg_h=_bf(_seg_matrix(D_HGRN, HEAD_DIM, 1.0 / HEAD_DIM)))
    vec = lambda a: a.reshape(DEPTH, 1, -1)
    smk = dict(sm)
    for n in ("mix_norm_w", "conv_dw_b", "conv_ln_w", "conv_ln_b", "conv_pw_b", "attn_out_norm_w", "conv_out_norm_w",
              "ffn_norm_w"):
        smk[n] = vec(sm[n])
    smk["q_norm_w"] = vec(jnp.tile(sm["q_norm_w"], (1, D_ATTN // HEAD_DIM)))
    smk["k_norm_w"] = vec(jnp.tile(sm["k_norm_w"], (1, N_KV)))
    smk["gnorm_w"] = vec(jnp.tile(sm["hgrn_gnorm_w"], (1, D_HGRN // HEAD_DIM)))
    smk["lb"] = sm["lb"].reshape(DEPTH, 2, 1, D_HGRN)
    smk["conv_dw_w"] = jnp.pad(sm["conv_dw_w"], ((0, 0), (0, 1), (0, 0)))

    h = x.reshape(t, d)
    saved = []
    for l in range(DEPTH):
        h, sv = _layer_fwd(l, h, wget, smk, tabs, cst, b, s, deps if l == 0 else ())
        saved.append(sv)
    dy, sq = _loss_kernel(h, target.reshape(t, d))
    sq_sum = jnp.sum(sq)
    dh = dy
    smalls = [None] * DEPTH
    for l in reversed(range(DEPTH)):
        dh, smalls[l] = _layer_bwd(l, dh, saved[l], wget, smk, tabs, cst, b, s, on_grads)
    return sq_sum, dh.reshape(b, s, d), smalls


HBM_SPEC = pl.BlockSpec(memory_space=pltpu.HBM)


def _exchange(name, arrs, mode):
    n = len(arrs)
    if mode == "gather8":
        flips = [(fx, fy, fc) for fx in (0, 1) for fy in (0, 1) for fc in (0, 1)][1:]
    elif mode == "sibling":
        flips = [(0, 0, 1)]
    else:
        flips = [(1, 0, 0), (0, 1, 0), (1, 1, 0)]
    n_f = len(flips)

    def body(*refs):
        ins, outs = refs[:n], refs[n:2 * n]
        send_sems, recv_sems, local_sems = refs[2 * n:]
        x, y, c = lax.axis_index("x"), lax.axis_index("y"), lax.axis_index("c")

        def slot_of(px, py, pc):
            return (2 * px + py) if mode != "gather8" else (4 * px + 2 * py + pc)

        me = slot_of(x, y, c)
        started = []
        for i in range(n):
            if mode != "sibling":
                src = ins[i].at[me] if mode == "scatter4" else ins[i]
                loc = pltpu.make_async_copy(src, outs[i].at[me], local_sems.at[i])
                loc.start()
                started.append(loc)
        sends, recvs = [], []
        for i in range(n):
            for f, (fx, fy, fc) in enumerate(flips):
                peer = (x ^ fx, y ^ fy, c ^ fc)
                ps = slot_of(*peer)
                if mode == "sibling":
                    src, dst, landed = ins[i], outs[i], outs[i]
                elif mode == "scatter4":
                    src, dst, landed = ins[i].at[ps], outs[i].at[me], outs[i].at[ps]
                else:
                    src, dst, landed = ins[i], outs[i].at[me], outs[i].at[ps]
                k = i * n_f + f
                cp = pltpu.make_async_remote_copy(src_ref=src, dst_ref=dst, send_sem=send_sems.at[k],
                                                  recv_sem=recv_sems.at[k], device_id=peer,
                                                  device_id_type=pl.DeviceIdType.MESH)
                cp.start()
                sends.append(cp)
                recvs.append(pltpu.make_async_remote_copy(src_ref=src, dst_ref=landed, send_sem=send_sems.at[k],
                                                          recv_sem=recv_sems.at[k], device_id=peer,
                                                          device_id_type=pl.DeviceIdType.MESH))
        for cp in sends:
            cp.wait_send()
        for cp in recvs:
            cp.wait_recv()
        for loc in started:
            loc.wait()

    def out_sds(a):
        if mode == "gather4":
            return _sds((N_CHIP,) + a.shape, a.dtype)
        if mode == "gather8":
            return _sds((N_DEV,) + a.shape, a.dtype)
        return _sds(a.shape, a.dtype)

    res = pl.pallas_call(
        body, name=name, in_specs=[HBM_SPEC] * n, out_specs=[HBM_SPEC] * n, out_shape=[out_sds(a) for a in arrs],
        scratch_shapes=[pltpu.SemaphoreType.DMA((n * n_f,)), pltpu.SemaphoreType.DMA((n * n_f,)),
                        pltpu.SemaphoreType.DMA((max(n, 1),))],
    )(*arrs)
    return list(res)


SEM_SPEC = pl.BlockSpec(memory_space=pltpu.SEMAPHORE)
SPLIT_EFFECT = pltpu.SideEffectType.DATAFLOW_SIDE_EFFECTING
CHIP_FLIPS = ((1, 0), (0, 1), (1, 1))


def _chip_copies(src_refs, land_refs, send_sems, recv_sems, scatter):
    x, y, c = lax.axis_index("x"), lax.axis_index("y"), lax.axis_index("c")
    me = 2 * x + y
    out = []
    for i, land in enumerate(land_refs):
        for f, (fx, fy) in enumerate(CHIP_FLIPS):
            peer = (x ^ fx, y ^ fy, c)
            ps = 2 * (x ^ fx) + (y ^ fy)
            src = src_refs[i].at[ps] if scatter else land.at[me]
            k = i * len(CHIP_FLIPS) + f
            kw = dict(send_sem=send_sems.at[k], recv_sem=recv_sems.at[k], device_id=peer,
                      device_id_type=pl.DeviceIdType.MESH)
            out.append((pltpu.make_async_remote_copy(src_ref=src, dst_ref=land.at[me], **kw),
                        pltpu.make_async_remote_copy(src_ref=src, dst_ref=land.at[ps], **kw)))
    return out


def _split_start(name, srcs, lands, scatter):
    n = len(lands)
    n_src = len(srcs)
    n_sem = n * len(CHIP_FLIPS)

    def body(*refs):
        src_refs = refs[:n_src]
        land_refs = refs[n_src:n_src + n]
        send_sems, recv_sems = refs[n_src + n], refs[n_src + n + 1]
        token = refs[-1]
        for start, _ in _chip_copies(src_refs, land_refs, send_sems, recv_sems, scatter):
            start.start()
        token[...] = jnp.zeros_like(token)

    arrs = list(srcs) + list(lands)
    res = pl.pallas_call(
        body, name=name,
        out_shape=(pltpu.SemaphoreType.DMA((n_sem,)), pltpu.SemaphoreType.DMA((n_sem,)),
                   *[pltpu.HBM(a.shape, a.dtype) for a in arrs], _sds((8, LANES), F32)),
        in_specs=[HBM_SPEC] * len(arrs),
        out_specs=(SEM_SPEC, SEM_SPEC, *[HBM_SPEC] * len(arrs), pl.BlockSpec(memory_space=pltpu.VMEM)),
        input_output_aliases={i: 2 + i for i in range(len(arrs))},
        compiler_params=pltpu.CompilerParams(has_side_effects=SPLIT_EFFECT),
    )(*[pltpu.with_memory_space_constraint(a, pltpu.HBM) for a in arrs])
    return dict(send=res[0], recv=res[1], srcs=list(res[2:2 + n_src]), lands=list(res[2 + n_src:2 + n_src + n]),
                token=res[-1], scatter=scatter)


def _split_wait(name, started, after):
    srcs, lands, scatter = started["srcs"], started["lands"], started["scatter"]
    n, n_src = len(lands), len(srcs)

    def body(*refs):
        src_refs = refs[:n_src]
        land_refs = refs[n_src:n_src + n]
        send_sems, recv_sems = refs[n_src + n], refs[n_src + n + 1]
        for _, wait in _chip_copies(src_refs, land_refs, send_sems, recv_sems, scatter):
            wait.wait_send()
            wait.wait_recv()

    arrs = list(srcs) + list(lands)
    res = pl.pallas_call(
        body, name=name, out_shape=tuple(pltpu.HBM(a.shape, a.dtype) for a in arrs),
        in_specs=[HBM_SPEC] * len(arrs) + [SEM_SPEC, SEM_SPEC, pl.BlockSpec(memory_space=pl.ANY)],
        out_specs=tuple([HBM_SPEC] * len(arrs)), input_output_aliases={i: i for i in range(len(arrs))},
        compiler_params=pltpu.CompilerParams(has_side_effects=SPLIT_EFFECT),
    )(*arrs, started["send"], started["recv"], after)
    return list(res[n_src:])


def _flat_tile(rows):
    for cand in (512, 256, 128, 64, 32, 16, 8):
        if rows % cand == 0:
            return cand
    return rows


def _cast_slot(name, a, l, chip):
    r, c = a.shape[0] // DEPTH, a.shape[1]
    tr = _flat_tile(r)

    def body(chip_ref, a_ref, o_ref):
        o_ref[...] = a_ref[...].astype(BF16)

    return pl.pallas_call(
        body, name=name, out_shape=_sds((N_CHIP, r, c), BF16),
        grid_spec=pltpu.PrefetchScalarGridSpec(
            num_scalar_prefetch=1, grid=(r // tr,),
            in_specs=[pl.BlockSpec((tr, c), lambda i, ch: (l * (r // tr) + i, 0))],
            out_specs=pl.BlockSpec((None, tr, c), lambda i, ch: (ch[0], i, 0))),
        compiler_params=_params(("parallel",)))(chip, a)


def _own_slot(name, g, chip):
    n, r, c = g.shape
    tr = _flat_tile(r)

    def body(chip_ref, g_ref, o_ref):
        o_ref[...] = g_ref[...]

    spec = pl.BlockSpec((None, tr, c), lambda i, ch: (ch[0], i, 0))
    return pl.pallas_call(
        body, name=name, out_shape=_sds(g.shape, g.dtype),
        grid_spec=pltpu.PrefetchScalarGridSpec(num_scalar_prefetch=1, grid=(r // tr,), in_specs=[spec], out_specs=spec),
        compiler_params=_params(("parallel",)))(chip, g)


def _sum_layers(name, lands):
    n, r, c = lands[0].shape
    tr = _flat_tile(r)
    nl = len(lands)

    def body(*refs):
        o_ref = refs[-1]
        for k in range(nl):
            @pl.when(pl.program_id(0) == k)
            def _():
                tot = refs[k][0].astype(F32)
                for i in range(1, n):
                    tot = tot + refs[k][i].astype(F32)
                o_ref[...] = tot

    return pl.pallas_call(
        body, name=name, grid=(nl, r // tr),
        in_specs=[pl.BlockSpec((n, tr, c), lambda l, i, k=k: (0, jnp.where(l == k, i, 0), 0)) for k in range(nl)],
        out_specs=pl.BlockSpec((tr, c), lambda l, i: (l * (r // tr) + i, 0)), out_shape=_sds((nl * r, c), F32),
        compiler_params=_params(("arbitrary", "arbitrary")))(*lands)


def _sum_slots(name, a, scale=None):
    n, r, c = a.shape
    tr = _flat_tile(r)

    def body(a_ref, o_ref):
        tot = a_ref[0].astype(F32)
        for i in range(1, n):
            tot = tot + a_ref[i].astype(F32)
        o_ref[...] = tot

    return pl.pallas_call(body, name=name, grid=(r // tr,),
                          in_specs=[pl.BlockSpec((n, tr, c), lambda i: (0, i, 0))],
                          out_specs=pl.BlockSpec((tr, c), lambda i: (i, 0)), out_shape=_sds((r, c), F32),
                          compiler_params=_params(("parallel",)))(a)


def _adamw(name, w, ga, gb, m, v):
    r, c = w.shape
    tr = _flat_tile(r)
    c1 = 1.0 - B1 ** STEP
    c2 = 1.0 - B2 ** STEP
    two = gb is not None

    def body(*refs):
        if two:
            w_ref, ga_ref, gb_ref, m_ref, v_ref, g_out, d_out, m_out, v_out = refs
            g = ga_ref[...] + gb_ref[...]
        else:
            w_ref, ga_ref, m_ref, v_ref, g_out, d_out, m_out, v_out = refs
            g = ga_ref[...]
        mn = B1 * m_ref[...] + (1.0 - B1) * g
        vn = B2 * v_ref[...] + (1.0 - B2) * (g * g)
        g_out[...] = g
        m_out[...] = mn
        v_out[...] = vn
        d_out[...] = -LR * ((mn / c1) / (jnp.sqrt(vn / c2) + ADAM_EPS) + WD * w_ref[...])

    spec = pl.BlockSpec((tr, c), lambda i: (i, 0))
    ins = [w, ga, gb, m, v] if two else [w, ga, m, v]
    return pl.pallas_call(body, name=name, grid=(r // tr,), in_specs=[spec] * len(ins), out_specs=[spec] * 4,
                          out_shape=[_sds((r, c), F32)] * 4, compiler_params=_params(("parallel",)))(*ins)


WEIGHTS = ('mix_norm_w', 'w_in', 'q_norm_w', 'k_norm_w', 'hgrn_lb_logits', 'hgrn_gnorm_w', 'conv_dw_w', 'conv_dw_b',
           'conv_ln_w', 'conv_ln_b', 'conv_pw_w', 'conv_pw_b', 'attn_out_norm_w', 'conv_out_norm_w', 'w_out',
           'ffn_norm_w', 'w_gate', 'w_up', 'w_down')
SHARDED_SMALL = {"hgrn_lb_logits": 2, "conv_dw_w": 2, "conv_pw_w": 1}
LANES = 128
PACK_ROWS = 256


def _pack(parts):
    flat = jnp.concatenate([p.reshape(-1) for p in parts])
    n = flat.shape[0]
    rows = -(-n // (PACK_ROWS * LANES)) * PACK_ROWS
    return jnp.pad(flat, (0, rows * LANES - n)).reshape(rows, LANES)


def _unpack(packed, shapes):
    flat = packed.reshape(-1)
    out, off = [], 0
    for shp in shapes:
        n = int(np.prod(shp))
        out.append(flat[off:off + n].reshape(shp))
        off += n
    return out


def kernel(x, mix_norm_w, w_in, q_norm_w, k_norm_w, hgrn_lb_logits, hgrn_gnorm_w, conv_dw_w, conv_dw_b, conv_ln_w, conv_ln_b, conv_pw_w, conv_pw_b, attn_out_norm_w, conv_out_norm_w, w_out, ffn_norm_w, w_gate, w_up, w_down, loss_target, m_mix_norm_w, m_w_in, m_q_norm_w, m_k_norm_w, m_hgrn_lb_logits, m_hgrn_gnorm_w, m_conv_dw_w, m_conv_dw_b, m_conv_ln_w, m_conv_ln_b, m_conv_pw_w, m_conv_pw_b, m_attn_out_norm_w, m_conv_out_norm_w, m_w_out, m_ffn_norm_w, m_w_gate, m_w_up, m_w_down, v_mix_norm_w, v_w_in, v_q_norm_w, v_k_norm_w, v_hgrn_lb_logits, v_hgrn_gnorm_w, v_conv_dw_w, v_conv_dw_b, v_conv_ln_w, v_conv_ln_b, v_conv_pw_w, v_conv_pw_b, v_attn_out_norm_w, v_conv_out_norm_w, v_w_out, v_ffn_norm_w, v_w_gate, v_w_up, v_w_down):
    w = dict(mix_norm_w=mix_norm_w, w_in=w_in, q_norm_w=q_norm_w, k_norm_w=k_norm_w, hgrn_lb_logits=hgrn_lb_logits,
             hgrn_gnorm_w=hgrn_gnorm_w, conv_dw_w=conv_dw_w, conv_dw_b=conv_dw_b, conv_ln_w=conv_ln_w,
             conv_ln_b=conv_ln_b, conv_pw_w=conv_pw_w, conv_pw_b=conv_pw_b, attn_out_norm_w=attn_out_norm_w,
             conv_out_norm_w=conv_out_norm_w, w_out=w_out, ffn_norm_w=ffn_norm_w, w_gate=w_gate, w_up=w_up,
             w_down=w_down)
    m = dict(mix_norm_w=m_mix_norm_w, w_in=m_w_in, q_norm_w=m_q_norm_w, k_norm_w=m_k_norm_w,
             hgrn_lb_logits=m_hgrn_lb_logits, hgrn_gnorm_w=m_hgrn_gnorm_w, conv_dw_w=m_conv_dw_w,
             conv_dw_b=m_conv_dw_b, conv_ln_w=m_conv_ln_w, conv_ln_b=m_conv_ln_b, conv_pw_w=m_conv_pw_w,
             conv_pw_b=m_conv_pw_b, attn_out_norm_w=m_attn_out_norm_w, conv_out_norm_w=m_conv_out_norm_w,
             w_out=m_w_out, ffn_norm_w=m_ffn_norm_w, w_gate=m_w_gate, w_up=m_w_up, w_down=m_w_down)
    v = dict(mix_norm_w=v_mix_norm_w, w_in=v_w_in, q_norm_w=v_q_norm_w, k_norm_w=v_k_norm_w,
             hgrn_lb_logits=v_hgrn_lb_logits, hgrn_gnorm_w=v_hgrn_gnorm_w, conv_dw_w=v_conv_dw_w,
             conv_dw_b=v_conv_dw_b, conv_ln_w=v_conv_ln_w, conv_ln_b=v_conv_ln_b, conv_pw_w=v_conv_pw_w,
             conv_pw_b=v_conv_pw_b, attn_out_norm_w=v_attn_out_norm_w, conv_out_norm_w=v_conv_out_norm_w,
             w_out=v_w_out, ffn_norm_w=v_ffn_norm_w, w_gate=v_w_gate, w_up=v_w_up, w_down=v_w_down)
    chip = 2 * lax.axis_index("x") + lax.axis_index("y")

    chip1 = chip.reshape(1).astype(jnp.int32)

    flat2 = lambda a: a.reshape(-1, a.shape[-1])
    small_pack = _pack([w[n] for n in SHARDED_SMALL])
    gathered = _exchange("gather_small_weights", [small_pack], "gather4")
    groups = [[(0, "w_in")], [(0, n) for n in BIG_ORDER[1:]], [(1, n) for n in BIG_ORDER]]
    group_of = {key: g for g, keys in enumerate(groups) for key in keys}
    slots = {(l, n): _cast_slot("cast_%s_l%d" % (n, l), flat2(w[n]), l, chip1) for l in range(DEPTH) for n in BIG_ORDER}
    starts = [_split_start("gather_start_g%d" % g, [], [slots[key] for key in keys], False)
              for g, keys in enumerate(groups)]
    got = {}

    def wget(l, name, after):
        if (l, name) not in got:
            g = group_of[(l, name)]
            for key, arr in zip(groups[g], _split_wait("gather_wait_g%d" % g, starts[g], after)):
                got[key] = arr
        return got[(l, name)]

    pending = []

    def on_grads(l, grads):
        names = [n for n in BIG_ORDER if n in grads]
        own = [_own_slot("own_%s_l%d" % (n, l), grads[n], chip1) for n in names]
        st = _split_start("scatter_start_l%d_%s" % (l, names[0]), [grads[n] for n in names], own, True)
        pending.append((l, names, st))
        return [st["token"]]

    parts = [_unpack(gathered[-1][j], [w[n].shape for n in SHARDED_SMALL]) for j in range(N_CHIP)]
    full_small = {n: jnp.concatenate([parts[j][i] for j in range(N_CHIP)], axis=ax)
                  for i, (n, ax) in enumerate(SHARDED_SMALL.items())}
    sm = {n: w[n] for n in WEIGHTS if n not in BIG_ORDER and n not in SHARDED_SMALL}
    sm["conv_dw_w"] = full_small["conv_dw_w"]
    sm["conv_pw_w"] = full_small["conv_pw_w"]
    logits = full_small["hgrn_lb_logits"].reshape(DEPTH * 2, D_HGRN)
    sm["lb"] = _lower_bounds(logits).reshape(DEPTH, 2, D_HGRN)

    sq_sum, grad_x, smalls = _local_step(x, loss_target, wget, sm, [st["token"] for st in starts], on_grads)
    loss = lax.psum(0.5 * sq_sum / D_MODEL, ("x", "y", "c"))

    landed = {}
    for l, names, st in pending:
        for n, arr in zip(names, _split_wait("scatter_wait_l%d_%s" % (l, names[0]), st, grad_x)):
            landed[(l, n)] = arr
    sums = [_sum_layers("sum_" + n, [landed[(l, n)] for l in range(DEPTH)]) for n in BIG_ORDER]
    sib = _exchange("sibling_grads", sums, "sibling")
    out = {}
    for n, ga, gb in zip(BIG_ORDER, sums, sib):
        res = _adamw("adamw_" + n, flat2(w[n]), ga, gb, flat2(m[n]), flat2(v[n]))
        out[n] = [r.reshape(w[n].shape) for r in res]

    small_names = [n for n in WEIGHTS if n not in BIG_ORDER]
    g_pack = _pack([jnp.stack([smalls[l][n] for l in range(DEPTH)]) for n in SMALL_ORDER])
    g_all = _exchange("gather_small_grads", [g_pack], "gather8")[0]
    g_tot = _sum_slots("sum_small", g_all)
    shapes = [(DEPTH,) + tuple(smalls[0][n].shape) for n in SMALL_ORDER]
    g_small = dict(zip(SMALL_ORDER, _unpack(g_tot, shapes)))
    lb_shard = lax.dynamic_slice_in_dim(g_small.pop("lb").reshape(DEPTH * 2, D_HGRN), chip * HEAD_DIM, HEAD_DIM, 1)
    g_small["hgrn_lb_logits"] = _lower_bounds_bwd(hgrn_lb_logits.reshape(DEPTH * 2, HEAD_DIM), lb_shard).reshape(
        hgrn_lb_logits.shape)
    g_small["conv_dw_w"] = lax.dynamic_slice_in_dim(g_small["conv_dw_w"], chip * HEAD_DIM, HEAD_DIM, 2)
    g_small["conv_pw_w"] = lax.dynamic_slice_in_dim(g_small["conv_pw_w"], chip * HEAD_DIM, HEAD_DIM, 1)
    res = _adamw("adamw_small", _pack([w[n] for n in small_names]), _pack([g_small[n] for n in small_names]), None,
                 _pack([m[n] for n in small_names]), _pack([v[n] for n in small_names]))
    unpacked = [_unpack(r, [w[n].shape for n in small_names]) for r in res]
    for i, n in enumerate(small_names):
        out[n] = [unpacked[k][i] for k in range(4)]

    return (loss, grad_x, *[out[n][0] for n in WEIGHTS], *[out[n][1] for n in WEIGHTS],
            *[out[n][2] for n in WEIGHTS], *[out[n][3] for n in WEIGHTS])
```

```python
import functools

import numpy as np
import jax
import jax.numpy as jnp
from jax import lax
from jax.experimental import pallas as pl
from jax.experimental.pallas import tpu as pltpu

F32, BF16 = jnp.float32, jnp.bfloat16

D_MODEL = 1024
DEPTH = 2
GRID_W = 64
D_ATTN, D_HGRN, D_CONV = 512, 256, 256
HEAD_DIM = 64
N_KV = 2
KV_LANES = D_ATTN // N_KV
ROPE_THETA = 10000.0
F_MIN = 1e-6
CONV_W = 31
CONV_PAD = 15
D_FF = 2816
D_IN = 2560
N_CHIP = 4
N_DEV = 8
IN_BLK = D_IN // N_CHIP
FF_BLK = D_FF // N_CHIP
OUT_BLK = D_MODEL // N_CHIP
EPS = 1e-6
LN_EPS = 1e-5
LR, B1, B2, ADAM_EPS, WD, STEP = 0.001, 0.9, 0.999, 1e-08, 0.01, 10
CHUNK = 16
HBLK = 256
CONV_TILE = 128
VMEM_LIMIT = 56 * 1024 * 1024

COL_Q, COL_K, COL_V = 0, 4, 5
COL_HQ, COL_FF, COL_FB, COL_HI, COL_HG, COL_CA, COL_CB = 3, 4, 5, 6, 7, 8, 9


def _params(sem=None):
    return pltpu.CompilerParams(dimension_semantics=sem, vmem_limit_bytes=VMEM_LIMIT)


def _sds(shape, dtype):
    return jax.ShapeDtypeStruct(tuple(shape), dtype)


def _full(shape):
    n = len(shape)
    return pl.BlockSpec(tuple(shape), lambda *_: (0,) * n)


def _sigmoid(x):
    return 1.0 / (1.0 + jnp.exp(-x))


def _silu(x):
    return x * _sigmoid(x)


def _dsilu(x):
    s = _sigmoid(x)
    return s * (1.0 + x * (1.0 - s))


def _rowgroups(v):
    m, c = v.shape
    return v.reshape(m // 8, 8, c).sum(axis=0)


def _split2(x):
    hi = x.astype(BF16)
    lo = (x - hi.astype(F32)).astype(BF16)
    return hi, lo


def _rdot2(x, m):
    hi, lo = _split2(x)
    return (jnp.dot(hi, m, preferred_element_type=F32) + jnp.dot(lo, m, preferred_element_type=F32))


def _ldot3(m, x):
    hi = x.astype(BF16)
    r1 = x - hi.astype(F32)
    mid = r1.astype(BF16)
    lo = (r1 - mid.astype(F32)).astype(BF16)
    return (jnp.dot(m, hi, preferred_element_type=F32) + jnp.dot(m, mid, preferred_element_type=F32)
            + jnp.dot(m, lo, preferred_element_type=F32))


def _dot_nt(a, b):
    return lax.dot_general(a, b, (((1,), (1,)), ((), ())), preferred_element_type=F32)


def _dot_tn(a, b):
    return lax.dot_general(a, b, (((0,), (0,)), ((), ())), preferred_element_type=F32)


def _seg_matrix(n, seg, val):
    i = np.arange(n)
    return ((i[:, None] // seg) == (i[None, :] // seg)).astype(np.float32) * val


def _rot_matrix(n):
    r = np.zeros((n, n), np.float32)
    for i in range(n):
        if (i % 32) < 16:
            r[i + 16, i] = -1.0
        else:
            r[i - 16, i] = 1.0
    return r


def _rep_matrix():
    r = np.zeros((N_KV * HEAD_DIM, D_ATTN), np.float32)
    for kv in range(N_KV):
        for g in range(KV_LANES // HEAD_DIM):
            for d in range(HEAD_DIM):
                r[HEAD_DIM * kv + d, KV_LANES * kv + HEAD_DIM * g + d] = 1.0
    return r


def _cumsum_matrix(rev):
    i = np.arange(HBLK)
    same = (i[:, None] // CHUNK) == (i[None, :] // CHUNK)
    tri = (i[None, :] >= i[:, None]) if rev else (i[None, :] <= i[:, None])
    return (same & tri).astype(np.float32)


def _sel_matrices():
    sel = np.zeros((CHUNK, CHUNK * CHUNK), np.float32)
    selt = np.zeros((CHUNK, CHUNK * CHUNK), np.float32)
    for t in range(CHUNK):
        for s in range(CHUNK):
            sel[t, t * CHUNK + s] = 1.0
            selt[s, t * CHUNK + s] = 1.0
    return sel, selt


def _bf(a):
    return jnp.asarray(a, dtype=BF16)


def _mm(name, grid, pairs, extras, outs, epilogue, acc=None, sem=None):
    n_p, n_e, n_o = len(pairs), len(extras), len(outs)

    def body(*refs):
        ab = refs[:2 * n_p]
        ex = refs[2 * n_p:2 * n_p + n_e]
        out = refs[2 * n_p + n_e:2 * n_p + n_e + n_o]
        scr = refs[2 * n_p + n_e + n_o:]
        tot = None
        for i in range(n_p):
            a = ab[2 * i][...]
            b = ab[2 * i + 1][...]
            if a.ndim == 3:
                a = a.reshape(-1, a.shape[-1])
            if b.ndim == 3:
                b = b.reshape(-1, b.shape[-1])
            r = lax.dot_general(a.astype(BF16), b.astype(BF16), pairs[i][4], preferred_element_type=F32)
            tot = r if tot is None else tot + r

        def finish(total):
            res = epilogue(total, *[e[...] for e in ex])
            for o_ref, val in zip(out, res):
                o_ref[...] = val.astype(o_ref.dtype)

        if acc is None:
            finish(tot)
        else:
            k = pl.program_id(acc[0])

            @pl.when(k == 0)
            def _():
                scr[0][...] = tot

            @pl.when(k > 0)
            def _():
                scr[0][...] += tot

            @pl.when(k == grid[acc[0]] - 1)
            def _():
                finish(scr[0][...])

    args, in_specs = [], []
    for a, a_spec, b, b_spec, _ in pairs:
        args += [a, b]
        in_specs += [a_spec, b_spec]
    for e, e_spec in extras:
        args.append(e)
        in_specs.append(e_spec)
    if sem is None:
        sem = tuple("arbitrary" if (acc is not None and i == acc[0]) else "parallel" for i in range(len(grid)))
    return pl.pallas_call(
        body, name=name, grid=grid, in_specs=in_specs,
        out_specs=[o[1] for o in outs], out_shape=[o[0] for o in outs],
        scratch_shapes=[] if acc is None else [pltpu.VMEM(acc[1], F32)],
        compiler_params=_params(sem),
    )(*args)


NN = (((1,), (0,)), ((), ()))
NT = (((1,), (1,)), ((), ()))
TN = (((0,), (0,)), ((), ()))


def _row_tile(t):
    return min(256, t)


def _rms_fwd(name, x, w, deps=()):
    t, d = x.shape
    tm = _row_tile(t)

    def body(x_ref, w_ref, *rest):
        o_ref = rest[-1]
        xv = x_ref[...]
        r = lax.rsqrt(jnp.mean(xv * xv, axis=-1, keepdims=True) + EPS)
        o_ref[...] = (xv * r * w_ref[...]).astype(BF16)

    return pl.pallas_call(
        body, name=name, grid=(t // tm,),
        in_specs=[pl.BlockSpec((tm, d), lambda i: (i, 0)), _full((1, d))] + [_full(a.shape) for a in deps],
        out_specs=pl.BlockSpec((tm, d), lambda i: (i, 0)), out_shape=_sds((t, d), BF16),
        compiler_params=_params(("parallel",)),
    )(x, w, *deps)


def _rms_bwd(name, x, w, dh, dres, deps=()):
    t, d = x.shape
    tm = _row_tile(t)

    def body(x_ref, w_ref, dh_ref, dres_ref, *rest):
        dx_ref, dw_ref = rest[-2:]
        xv = x_ref[...]
        r = lax.rsqrt(jnp.mean(xv * xv, axis=-1, keepdims=True) + EPS)
        dy = dh_ref[...]
        gw = dy * w_ref[...]
        dx_ref[...] = dres_ref[...] + r * gw - xv * (r * r * r) * jnp.mean(gw * xv, axis=-1, keepdims=True)

        @pl.when(pl.program_id(0) == 0)
        def _():
            dw_ref[...] = jnp.zeros_like(dw_ref)

        dw_ref[...] += _rowgroups(dy * xv * r)

    tile = pl.BlockSpec((tm, d), lambda i: (i, 0))
    return pl.pallas_call(
        body, name=name, grid=(t // tm,),
        in_specs=[tile, _full((1, d)), tile, tile] + [_full(a.shape) for a in deps],
        out_specs=[tile, _full((8, d))], out_shape=[_sds((t, d), F32), _sds((8, d), F32)],
        compiler_params=_params(("arbitrary",)),
    )(x, w, dh, dres, *deps)


def _loss_kernel(y, target):
    t, d = y.shape
    tm = _row_tile(t)

    def body(y_ref, t_ref, dy_ref, acc_ref):
        e = y_ref[...] - t_ref[...]
        dy_ref[...] = e * (1.0 / d)

        @pl.when(pl.program_id(0) == 0)
        def _():
            acc_ref[...] = jnp.zeros_like(acc_ref)

        acc_ref[...] += _rowgroups(e * e)

    tile = pl.BlockSpec((tm, d), lambda i: (i, 0))
    return pl.pallas_call(
        body, name="loss_head", grid=(t // tm,), in_specs=[tile, tile],
        out_specs=[tile, _full((8, d))], out_shape=[_sds((t, d), F32), _sds((8, d), F32)],
        compiler_params=_params(("arbitrary",)),
    )(y, target)


def _rope_tables(s):
    rows = s // GRID_W
    row_id = jnp.repeat(jnp.arange(rows, dtype=F32), GRID_W)
    col_id = jnp.tile(jnp.arange(GRID_W, dtype=F32), rows)
    half = HEAD_DIM // 2
    inv_freq = ROPE_THETA ** (-jnp.arange(0, half, 2, dtype=F32) / half)
    ang_r = row_id[:, None] * inv_freq[None, :]
    ang_c = col_id[:, None] * inv_freq[None, :]
    ang = jnp.concatenate([ang_r, ang_r, ang_c, ang_c], axis=-1)
    return jnp.cos(ang).astype(F32), jnp.sin(ang).astype(F32)


def _attn_consts():
    return dict(
        seg_q=_bf(_seg_matrix(D_ATTN, HEAD_DIM, 1.0 / HEAD_DIM)),
        seg_k=_bf(_seg_matrix(N_KV * HEAD_DIM, HEAD_DIM, 1.0 / HEAD_DIM)),
        rot_q=_bf(_rot_matrix(D_ATTN)), rot_k=_bf(_rot_matrix(N_KV * HEAD_DIM)),
        rep=_bf(_rep_matrix()), rep_t=_bf(_rep_matrix().T))


def _attn_prep(name, proj, s, tabs, qw, kw, ac):
    t = proj.shape[0]
    tm = _row_tile(s)
    nst = s // tm
    kw_ = N_KV * HEAD_DIM

    def body(q_ref, k_ref, v_ref, cq_ref, sq_ref, ck_ref, sk_ref, qw_ref, kw_ref,
             segq_ref, segk_ref, rotq_ref, rotk_ref, rep_ref, qn_ref, kr_ref, vr_ref):
        q = q_ref[...]
        r = lax.rsqrt(jnp.dot((q * q).astype(BF16), segq_ref[...], preferred_element_type=F32) + EPS)
        qn = q * r * qw_ref[...]
        qr = qn * cq_ref[...] + _rdot2(qn, rotq_ref[...]) * sq_ref[...]
        qn_ref[...] = (qr * (HEAD_DIM ** -0.5)).astype(BF16)
        k = k_ref[...]
        rk = lax.rsqrt(jnp.dot((k * k).astype(BF16), segk_ref[...], preferred_element_type=F32) + EPS)
        kn = k * rk * kw_ref[...]
        kr = kn * ck_ref[...] + _rdot2(kn, rotk_ref[...]) * sk_ref[...]
        kr_ref[...] = jnp.dot(kr.astype(BF16), rep_ref[...], preferred_element_type=F32).astype(BF16)
        vr_ref[...] = jnp.dot(v_ref[...].astype(BF16), rep_ref[...], preferred_element_type=F32).astype(BF16)

    wide = pl.BlockSpec((tm, D_ATTN), lambda i: (i, 0))
    tabq = pl.BlockSpec((tm, D_ATTN), lambda i: (i % nst, 0))
    tabk = pl.BlockSpec((tm, kw_), lambda i: (i % nst, 0))
    return pl.pallas_call(
        body, name=name, grid=(t // tm,),
        in_specs=[pl.BlockSpec((tm, D_ATTN), lambda i: (i, COL_Q)), pl.BlockSpec((tm, kw_), lambda i: (i, COL_K)),
                  pl.BlockSpec((tm, kw_), lambda i: (i, COL_V)), tabq, tabq, tabk, tabk,
                  _full((1, D_ATTN)), _full((1, kw_)), _full((D_ATTN, D_ATTN)), _full((kw_, kw_)),
                  _full((D_ATTN, D_ATTN)), _full((kw_, kw_)), _full((kw_, D_ATTN))],
        out_specs=[wide, wide, wide], out_shape=[_sds((t, D_ATTN), BF16)] * 3,
        compiler_params=_params(("parallel",)),
    )(proj, proj, proj, tabs["cq"], tabs["sq"], tabs["ck"], tabs["sk"], qw, kw,
      ac["seg_q"], ac["seg_k"], ac["rot_q"], ac["rot_k"], ac["rep"])


def _attn_prep_bwd(name, proj, s, tabs, qw, kw, ac, dqs, dkr, dvr):
    t = proj.shape[0]
    tm = _row_tile(s)
    nst = s // tm
    kw_ = N_KV * HEAD_DIM
    wout = D_ATTN + 2 * kw_

    def norm_rope_bwd(x, w, cos, sin, seg, rot, d_roped):
        dn = d_roped * cos - _rdot2(d_roped * sin, rot)
        r = lax.rsqrt(jnp.dot((x * x).astype(BF16), seg, preferred_element_type=F32) + EPS)
        gw = dn * w
        dx = r * gw - x * (r * r * r) * _rdot2(gw * x, seg)
        return dx, _rowgroups(dn * x * r)

    def body(q_ref, k_ref, cq_ref, sq_ref, ck_ref, sk_ref, qw_ref, kw_ref, segq_ref, segk_ref, rotq_ref, rotk_ref,
             rept_ref, dqs_ref, dkr_ref, dvr_ref, dp_ref, dqw_ref, dkw_ref):
        dq, dqw = norm_rope_bwd(q_ref[...], qw_ref[...], cq_ref[...], sq_ref[...], segq_ref[...], rotq_ref[...],
                                dqs_ref[...] * (HEAD_DIM ** -0.5))
        dk_roped = _rdot2(dkr_ref[...], rept_ref[...])
        dk, dkw = norm_rope_bwd(k_ref[...], kw_ref[...], ck_ref[...], sk_ref[...], segk_ref[...], rotk_ref[...],
                                dk_roped)
        dv = _rdot2(dvr_ref[...], rept_ref[...])
        dp_ref[:, 0:D_ATTN] = dq.astype(BF16)
        dp_ref[:, D_ATTN:D_ATTN + kw_] = dk.astype(BF16)
        dp_ref[:, D_ATTN + kw_:wout] = dv.astype(BF16)

        @pl.when(pl.program_id(0) == 0)
        def _():
            dqw_ref[...] = jnp.zeros_like(dqw_ref)
            dkw_ref[...] = jnp.zeros_like(dkw_ref)

        dqw_ref[...] += dqw
        dkw_ref[...] += dkw

    wide = pl.BlockSpec((tm, D_ATTN), lambda i: (i, 0))
    tabq = pl.BlockSpec((tm, D_ATTN), lambda i: (i % nst, 0))
    tabk = pl.BlockSpec((tm, kw_), lambda i: (i % nst, 0))
    return pl.pallas_call(
        body, name=name, grid=(t // tm,),
        in_specs=[pl.BlockSpec((tm, D_ATTN), lambda i: (i, COL_Q)), pl.BlockSpec((tm, kw_), lambda i: (i, COL_K)),
                  tabq, tabq, tabk, tabk, _full((1, D_ATTN)), _full((1, kw_)),
                  _full((D_ATTN, D_ATTN)), _full((kw_, kw_)), _full((D_ATTN, D_ATTN)), _full((kw_, kw_)),
                  _full((D_ATTN, kw_)), wide, wide, wide],
        out_specs=[pl.BlockSpec((tm, wout), lambda i: (i, 0)), _full((8, D_ATTN)), _full((8, kw_))],
        out_shape=[_sds((t, wout), BF16), _sds((8, D_ATTN), F32), _sds((8, kw_), F32)],
        compiler_params=_params(("arbitrary",)),
    )(proj, proj, tabs["cq"], tabs["sq"], tabs["ck"], tabs["sk"], qw, kw,
      ac["seg_q"], ac["seg_k"], ac["rot_q"], ac["rot_k"], ac["rep_t"], dqs, dkr, dvr)


def _attn_tile(s):
    return min(256, s)


def _head_masks(shape):
    lane = lax.broadcasted_iota(jnp.int32, shape, 1)
    return [(lane // HEAD_DIM) == g for g in range(KV_LANES // HEAD_DIM)]


def _attn_fwd(name, qn, kr, vr, b, s):
    t = qn.shape[0]
    tq = _attn_tile(s)
    nq = s // tq

    def body(q_ref, k_ref, v_ref, o_ref):
        q = q_ref[...]
        masks = _head_masks((tq, KV_LANES))
        q4 = jnp.concatenate([jnp.where(m, q, jnp.zeros_like(q)) for m in masks], axis=0)
        sc = _dot_nt(q4, k_ref[...])
        p = jnp.exp(sc - jnp.max(sc, axis=-1, keepdims=True))
        inv = 1.0 / jnp.sum(p, axis=-1, keepdims=True)
        o4 = jnp.dot(p.astype(BF16), v_ref[...], preferred_element_type=F32) * inv
        acc = jnp.zeros((tq, KV_LANES), F32)
        for g, m in enumerate(masks):
            acc = jnp.where(m, o4[g * tq:(g + 1) * tq, :], acc)
        o_ref[...] = acc

    return pl.pallas_call(
        body, name=name, grid=(b, N_KV, nq),
        in_specs=[pl.BlockSpec((tq, KV_LANES), lambda bi, kv, i: (bi * nq + i, kv)),
                  pl.BlockSpec((s, KV_LANES), lambda bi, kv, i: (bi, kv)),
                  pl.BlockSpec((s, KV_LANES), lambda bi, kv, i: (bi, kv))],
        out_specs=pl.BlockSpec((tq, KV_LANES), lambda bi, kv, i: (bi * nq + i, kv)),
        out_shape=_sds((t, D_ATTN), F32),
        compiler_params=_params(("parallel", "parallel", "parallel")),
    )(qn, kr, vr)


def _attn_bwd(name, qn, kr, vr, do, b, s):
    t = qn.shape[0]
    tq = _attn_tile(s)
    nq = s // tq

    def body(q_ref, k_ref, v_ref, do_ref, dq_ref, dk_ref, dv_ref):
        @pl.when(pl.program_id(2) == 0)
        def _():
            dk_ref[...] = jnp.zeros_like(dk_ref)
            dv_ref[...] = jnp.zeros_like(dv_ref)

        q = q_ref[...]
        k = k_ref[...]
        v = v_ref[...]
        dout = do_ref[...].astype(BF16)
        masks = _head_masks((tq, KV_LANES))
        q4 = jnp.concatenate([jnp.where(m, q, jnp.zeros_like(q)) for m in masks], axis=0)
        do4 = jnp.concatenate([jnp.where(m, dout, jnp.zeros_like(dout)) for m in masks], axis=0)
        sc = _dot_nt(q4, k)
        p = jnp.exp(sc - jnp.max(sc, axis=-1, keepdims=True))
        p = p * (1.0 / jnp.sum(p, axis=-1, keepdims=True))
        dp = _dot_nt(do4, v)
        ds = (p * (dp - jnp.sum(p * dp, axis=-1, keepdims=True))).astype(BF16)
        dq4 = jnp.dot(ds, k, preferred_element_type=F32)
        dq = jnp.zeros((tq, KV_LANES), F32)
        for g, m in enumerate(masks):
            dq = jnp.where(m, dq4[g * tq:(g + 1) * tq, :], dq)
        dq_ref[...] = dq
        dk_ref[...] += _dot_tn(ds, q4)
        dv_ref[...] += _dot_tn(p.astype(BF16), do4)

    qspec = pl.BlockSpec((tq, KV_LANES), lambda bi, kv, i: (bi * nq + i, kv))
    kspec = pl.BlockSpec((s, KV_LANES), lambda bi, kv, i: (bi, kv))
    return pl.pallas_call(
        body, name=name, grid=(b, N_KV, nq),
        in_specs=[qspec, kspec, kspec, qspec],
        out_specs=[qspec, kspec, kspec], out_shape=[_sds((t, D_ATTN), F32)] * 3,
        compiler_params=_params(("parallel", "parallel", "arbitrary")),
    )(qn, kr, vr, do)


def _hgrn_consts(rev):
    sel, selt = _sel_matrices()
    cs = _cumsum_matrix(rev)
    return dict(cs=_bf(cs), cs_t=_bf(cs.T), seg=_bf(_seg_matrix(D_HGRN, HEAD_DIM, 1.0)),
                bd=jnp.asarray(_seg_matrix(D_HGRN, HEAD_DIM, 1.0), F32),
                sel=_bf(sel), selt=_bf(selt), seld=_bf(sel - selt))


def _gates(z, lb):
    sig = _sigmoid(z)
    f = lb + (1.0 - lb) * sig
    g = jnp.log(jnp.maximum(f, F_MIN))
    sn = _sigmoid(-z)
    return sig, f, g, sn, (1.0 - lb) * sn


def _pair_decay(b, rev):
    row = lax.broadcasted_iota(jnp.int32, (CHUNK, D_HGRN), 0)
    parts = []
    for t in range(CHUNK):
        m = (row >= t) if rev else (row <= t)
        parts.append(jnp.where(m, jnp.exp(jnp.minimum(b[t:t + 1, :] - b, 0.0)), 0.0))
    return jnp.concatenate(parts, axis=0)


def _rows_rep(a):
    return jnp.concatenate([jnp.broadcast_to(a[t:t + 1, :], a.shape) for t in range(CHUNK)], axis=0)


def _tile_rows(a):
    return jnp.concatenate([a] * CHUNK, axis=0)


def _hgrn_specs(b, s, rev):
    nb = s // HBLK

    def blk(j):
        return (nb - 1 - j) if rev else j

    def col(c):
        return pl.BlockSpec((HBLK, D_HGRN), lambda bi, j: (bi * nb + blk(j), c))

    return nb, blk, col


def _hgrn_fwd(name, proj, lb, b, s, rev, hc):
    t = proj.shape[0]
    nb, blk, col = _hgrn_specs(b, s, rev)
    n_ch = HBLK // CHUNK
    last = 0 if rev else CHUNK - 1

    def body(q_ref, z_ref, v_ref, lb_ref, cs_ref, seg_ref, bd_ref, sel_ref, o_ref, st_ref, state, b_scr, k_scr):
        @pl.when(pl.program_id(1) == 0)
        def _():
            state[...] = jnp.zeros_like(state)

        st_ref[...] = state[...]
        _, _, g, _, kk = _gates(z_ref[...], lb_ref[...])
        k_scr[...] = kk
        b_scr[...] = _ldot3(cs_ref[...], g)

        def chunk(i, carry):
            c = (n_ch - 1 - i) if rev else i
            rows = pl.ds(pl.multiple_of(c * CHUNK, CHUNK), CHUNK)
            q = q_ref[rows, :]
            k = k_scr[rows, :]
            v = v_ref[rows, :]
            bb = b_scr[rows, :]
            bl = bb[last:last + 1, :]
            pairs = _pair_decay(bb, rev) * _rows_rep(q) * _tile_rows(k)
            a = jnp.dot(pairs.astype(BF16), seg_ref[...], preferred_element_type=F32)
            o_intra = jnp.dot(sel_ref[...], (a * _tile_rows(v)).astype(BF16), preferred_element_type=F32)
            st = state[...]
            o_inter = _dot_nt((q * jnp.exp(bb)).astype(BF16), st.astype(BF16))
            o_ref[rows, :] = o_intra + o_inter
            ke = k * jnp.exp(bl - bb)
            state[...] = st * jnp.exp(bl) + bd_ref[...] * _dot_tn(v.astype(BF16), ke.astype(BF16))
            return carry

        lax.fori_loop(0, n_ch, chunk, 0)

    sq = (D_HGRN, D_HGRN)
    return pl.pallas_call(
        body, name=name, grid=(b, nb),
        in_specs=[col(COL_HQ), col(COL_FB if rev else COL_FF), col(COL_HI), _full((1, D_HGRN)),
                  _full((HBLK, HBLK)), _full(sq), _full(sq), _full((CHUNK, CHUNK * CHUNK))],
        out_specs=[pl.BlockSpec((HBLK, D_HGRN), lambda bi, j: (bi * nb + blk(j), 0)),
                   pl.BlockSpec((None,) + sq, lambda bi, j: (bi * nb + blk(j), 0, 0))],
        out_shape=[_sds((t, D_HGRN), F32), _sds((b * nb,) + sq, F32)],
        scratch_shapes=[pltpu.VMEM(sq, F32), pltpu.VMEM((HBLK, D_HGRN), F32), pltpu.VMEM((HBLK, D_HGRN), F32)],
        compiler_params=_params(("parallel", "arbitrary")),
    )(proj, proj, proj, lb, hc["cs"], hc["seg"], hc["bd"], hc["sel"])


def _hgrn_bwd(name, proj, lb, st_blk, do, dq_prev, dv_prev, b, s, rev, hc):
    t = proj.shape[0]
    nb = s // HBLK
    n_ch = HBLK // CHUNK
    last = 0 if rev else CHUNK - 1

    def blk(j):
        return j if rev else (nb - 1 - j)

    def col(c):
        return pl.BlockSpec((HBLK, D_HGRN), lambda bi, j: (bi * nb + blk(j), c))

    def body(q_ref, z_ref, v_ref, lb_ref, st_ref, do_ref, dqp_ref, dvp_ref, cs_ref, cst_ref, seg_ref, bd_ref,
             sel_ref, selt_ref, seld_ref, dq_ref, dv_ref, dz_ref, dlb_ref,
             dstate, states, b_scr, k_scr, db_scr, dk_scr):
        first = jnp.logical_and(pl.program_id(0) == 0, pl.program_id(1) == 0)

        @pl.when(first)
        def _():
            dlb_ref[...] = jnp.zeros_like(dlb_ref)

        @pl.when(pl.program_id(1) == 0)
        def _():
            dstate[...] = jnp.zeros_like(dstate)

        lbv = lb_ref[...]
        z = z_ref[...]
        sig, f, g, sn, kk = _gates(z, lbv)
        k_scr[...] = kk
        b_scr[...] = _ldot3(cs_ref[...], g)

        def rows_of(c):
            return pl.ds(pl.multiple_of(c * CHUNK, CHUNK), CHUNK)

        def replay(i, st):
            c = (n_ch - 1 - i) if rev else i
            rows = rows_of(c)
            states[c] = st
            bb = b_scr[rows, :]
            bl = bb[last:last + 1, :]
            ke = k_scr[rows, :] * jnp.exp(bl - bb)
            return st * jnp.exp(bl) + bd_ref[...] * _dot_tn(v_ref[rows, :].astype(BF16), ke.astype(BF16))

        lax.fori_loop(0, n_ch, replay, st_ref[...])
        row = lax.broadcasted_iota(jnp.int32, (CHUNK, D_HGRN), 0)

        def chunk(i, carry):
            c = i if rev else (n_ch - 1 - i)
            rows = rows_of(c)
            q = q_ref[rows, :]
            k = k_scr[rows, :]
            v = v_ref[rows, :]
            bb = b_scr[rows, :]
            dout = do_ref[rows, :]
            bl = bb[last:last + 1, :]
            st_p = states[c]
            dst_n = dstate[...]
            eb = jnp.exp(bb)
            ebl = jnp.exp(bl - bb)
            ebl_last = jnp.exp(bl)
            qe = q * eb
            ke = k * ebl
            dob = dout.astype(BF16)
            dstb = dst_n.astype(BF16)
            dqe = jnp.dot(dob, st_p.astype(BF16), preferred_element_type=F32)
            dke = jnp.dot(v.astype(BF16), dstb, preferred_element_type=F32)
            dv = _dot_nt(ke.astype(BF16), dstb)
            dbl = jnp.sum(dst_n * st_p, axis=0, keepdims=True) * ebl_last + jnp.sum(dke * ke, axis=0, keepdims=True)
            dq = dqe * eb
            dk = dke * ebl
            db = dqe * qe - dke * ke
            dec = _pair_decay(bb, rev)
            q_rep = _rows_rep(q)
            k_til = _tile_rows(k)
            do_rep = _rows_rep(dout)
            pairs = dec * q_rep * k_til
            a = jnp.dot(pairs.astype(BF16), seg_ref[...], preferred_element_type=F32)
            wb = jnp.dot((_tile_rows(v) * do_rep).astype(BF16), seg_ref[...], preferred_element_type=F32)
            gdec = wb * dec
            dq = dq + jnp.dot(sel_ref[...], (gdec * k_til).astype(BF16), preferred_element_type=F32)
            dk = dk + jnp.dot(selt_ref[...], (gdec * q_rep).astype(BF16), preferred_element_type=F32)
            dv = dv + jnp.dot(selt_ref[...], (a * do_rep).astype(BF16), preferred_element_type=F32)
            db = db + jnp.dot(seld_ref[...], (wb * pairs).astype(BF16), preferred_element_type=F32)
            db = db + jnp.where(row == last, dbl, 0.0)
            dq_ref[rows, :] = dq + dqp_ref[rows, :]
            dv_ref[rows, :] = dv + dvp_ref[rows, :]
            dk_scr[rows, :] = dk
            db_scr[rows, :] = db
            dstate[...] = dst_n * ebl_last + bd_ref[...] * _dot_tn(dob, qe.astype(BF16))
            return carry

        lax.fori_loop(0, n_ch, chunk, 0)
        hi, lo = _split2(db_scr[...])
        dg = (jnp.dot(cst_ref[...], hi, preferred_element_type=F32)
              + jnp.dot(cst_ref[...], lo, preferred_element_type=F32))
        dgf = jnp.where(f > F_MIN, dg / f, 0.0)
        dk = dk_scr[...]
        dz_ref[...] = dgf * (1.0 - lbv) * sig * (1.0 - sig) - dk * (1.0 - lbv) * sn * (1.0 - sn)
        dlb_ref[...] += _rowgroups(dgf * (1.0 - sig) - dk * sn)

    sq = (D_HGRN, D_HGRN)
    blk0 = pl.BlockSpec((HBLK, D_HGRN), lambda bi, j: (bi * nb + blk(j), 0))
    pairs_shape = (CHUNK, CHUNK * CHUNK)
    return pl.pallas_call(
        body, name=name, grid=(b, nb),
        in_specs=[col(COL_HQ), col(COL_FB if rev else COL_FF), col(COL_HI), _full((1, D_HGRN)),
                  pl.BlockSpec((None,) + sq, lambda bi, j: (bi * nb + blk(j), 0, 0)), blk0, blk0, blk0,
                  _full((HBLK, HBLK)), _full((HBLK, HBLK)), _full(sq), _full(sq),
                  _full(pairs_shape), _full(pairs_shape), _full(pairs_shape)],
        out_specs=[blk0, blk0, blk0, _full((8, D_HGRN))],
        out_shape=[_sds((t, D_HGRN), F32)] * 3 + [_sds((8, D_HGRN), F32)],
        scratch_shapes=[pltpu.VMEM(sq, F32), pltpu.VMEM((n_ch,) + sq, F32)] + [pltpu.VMEM((HBLK, D_HGRN), F32)] * 4,
        compiler_params=_params(("arbitrary", "arbitrary")),
    )(proj, proj, proj, lb, st_blk, do, dq_prev, dv_prev,
      hc["cs"], hc["cs_t"], hc["seg"], hc["bd"], hc["sel"], hc["selt"], hc["seld"])


def _scan_chunk_fwd(c, rev, q_ref, v_ref, k_scr, b_scr, state, o_ref, seg_ref, bd_ref, sel_ref):
    last = 0 if rev else CHUNK - 1
    rows = pl.ds(pl.multiple_of(c * CHUNK, CHUNK), CHUNK)
    q = q_ref[rows, :]
    k = k_scr[rows, :]
    v = v_ref[rows, :]
    bb = b_scr[rows, :]
    bl = bb[last:last + 1, :]
    pairs = _pair_decay(bb, rev) * _rows_rep(q) * _tile_rows(k)
    a = jnp.dot(pairs.astype(BF16), seg_ref[...], preferred_element_type=F32)
    o_intra = jnp.dot(sel_ref[...], (a * _tile_rows(v)).astype(BF16), preferred_element_type=F32)
    st = state[...]
    o_inter = _dot_nt((q * jnp.exp(bb)).astype(BF16), st.astype(BF16))
    o_ref[rows, :] = o_intra + o_inter
    ke = k * jnp.exp(bl - bb)
    state[...] = st * jnp.exp(bl) + bd_ref[...] * _dot_tn(v.astype(BF16), ke.astype(BF16))


def _hgrn_fwd2(name, proj, lb_f, lb_b, b, s, hc_f, hc_b):
    t = proj.shape[0]
    nb = s // HBLK
    n_ch = HBLK // CHUNK

    def body(qf_ref, zf_ref, vf_ref, qb_ref, zb_ref, vb_ref, lbf_ref, lbb_ref, csf_ref, csb_ref, seg_ref, bd_ref,
             sel_ref, of_ref, ob_ref, stf_ref, stb_ref, state_f, state_b, bf_scr, bb_scr, kf_scr, kb_scr):
        @pl.when(pl.program_id(1) == 0)
        def _():
            state_f[...] = jnp.zeros_like(state_f)
            state_b[...] = jnp.zeros_like(state_b)

        stf_ref[...] = state_f[...]
        stb_ref[...] = state_b[...]
        for z_ref, lb_ref, cs_ref, k_scr, b_scr in ((zf_ref, lbf_ref, csf_ref, kf_scr, bf_scr),
                                                     (zb_ref, lbb_ref, csb_ref, kb_scr, bb_scr)):
            _, _, g, _, kk = _gates(z_ref[...], lb_ref[...])
            k_scr[...] = kk
            b_scr[...] = _ldot3(cs_ref[...], g)

        def chunk(i, carry):
            _scan_chunk_fwd(i, False, qf_ref, vf_ref, kf_scr, bf_scr, state_f, of_ref, seg_ref, bd_ref, sel_ref)
            _scan_chunk_fwd(n_ch - 1 - i, True, qb_ref, vb_ref, kb_scr, bb_scr, state_b, ob_ref, seg_ref, bd_ref,
                            sel_ref)
            return carry

        lax.fori_loop(0, n_ch, chunk, 0)

    def col(c, rev):
        return pl.BlockSpec((HBLK, D_HGRN), lambda bi, j: (bi * nb + ((nb - 1 - j) if rev else j), c))

    def st_spec(rev):
        return pl.BlockSpec((None, D_HGRN, D_HGRN), lambda bi, j: (bi * nb + ((nb - 1 - j) if rev else j), 0, 0))

    sq = (D_HGRN, D_HGRN)
    blk = (HBLK, D_HGRN)
    return pl.pallas_call(
        body, name=name, grid=(b, nb),
        in_specs=[col(COL_HQ, False), col(COL_FF, False), col(COL_HI, False),
                  col(COL_HQ, True), col(COL_FB, True), col(COL_HI, True),
                  _full((1, D_HGRN)), _full((1, D_HGRN)), _full((HBLK, HBLK)), _full((HBLK, HBLK)),
                  _full(sq), _full(sq), _full((CHUNK, CHUNK * CHUNK))],
        out_specs=[col(0, False), col(0, True), st_spec(False), st_spec(True)],
        out_shape=[_sds((t, D_HGRN), F32)] * 2 + [_sds((b * nb,) + sq, F32)] * 2,
        scratch_shapes=[pltpu.VMEM(sq, F32)] * 2 + [pltpu.VMEM(blk, F32)] * 4,
        compiler_params=_params(("parallel", "arbitrary")),
    )(proj, proj, proj, proj, proj, proj, lb_f, lb_b, hc_f["cs"], hc_b["cs"], hc_f["seg"], hc_f["bd"], hc_f["sel"])


def _scan_replay(c, rev, st, v_ref, k_scr, b_scr, states, bd_ref):
    last = 0 if rev else CHUNK - 1
    rows = pl.ds(pl.multiple_of(c * CHUNK, CHUNK), CHUNK)
    states[c] = st
    bb = b_scr[rows, :]
    bl = bb[last:last + 1, :]
    ke = k_scr[rows, :] * jnp.exp(bl - bb)
    return st * jnp.exp(bl) + bd_ref[...] * _dot_tn(v_ref[rows, :].astype(BF16), ke.astype(BF16))


def _scan_chunk_bwd(c, rev, q_ref, v_ref, do_ref, k_scr, b_scr, states, dstate, dq_ref, dv_ref, dk_scr, db_scr,
                    seg_ref, bd_ref, sel_ref, selt_ref, seld_ref):
    last = 0 if rev else CHUNK - 1
    row = lax.broadcasted_iota(jnp.int32, (CHUNK, D_HGRN), 0)
    rows = pl.ds(pl.multiple_of(c * CHUNK, CHUNK), CHUNK)
    q = q_ref[rows, :]
    k = k_scr[rows, :]
    v = v_ref[rows, :]
    bb = b_scr[rows, :]
    dout = do_ref[rows, :]
    bl = bb[last:last + 1, :]
    st_p = states[c]
    dst_n = dstate[...]
    eb = jnp.exp(bb)
    ebl = jnp.exp(bl - bb)
    ebl_last = jnp.exp(bl)
    qe = q * eb
    ke = k * ebl
    dob = dout.astype(BF16)
    dstb = dst_n.astype(BF16)
    dqe = jnp.dot(dob, st_p.astype(BF16), preferred_element_type=F32)
    dke = jnp.dot(v.astype(BF16), dstb, preferred_element_type=F32)
    dv = _dot_nt(ke.astype(BF16), dstb)
    dbl = jnp.sum(dst_n * st_p, axis=0, keepdims=True) * ebl_last + jnp.sum(dke * ke, axis=0, keepdims=True)
    dq = dqe * eb
    dk = dke * ebl
    db = dqe * qe - dke * ke
    dec = _pair_decay(bb, rev)
    q_rep = _rows_rep(q)
    k_til = _tile_rows(k)
    do_rep = _rows_rep(dout)
    pairs = dec * q_rep * k_til
    a = jnp.dot(pairs.astype(BF16), seg_ref[...], preferred_element_type=F32)
    wb = jnp.dot((_tile_rows(v) * do_rep).astype(BF16), seg_ref[...], preferred_element_type=F32)
    gdec = wb * dec
    dq = dq + jnp.dot(sel_ref[...], (gdec * k_til).astype(BF16), preferred_element_type=F32)
    dk = dk + jnp.dot(selt_ref[...], (gdec * q_rep).astype(BF16), preferred_element_type=F32)
    dv = dv + jnp.dot(selt_ref[...], (a * do_rep).astype(BF16), preferred_element_type=F32)
    db = db + jnp.dot(seld_ref[...], (wb * pairs).astype(BF16), preferred_element_type=F32)
    db = db + jnp.where(row == last, dbl, 0.0)
    dq_ref[rows, :] = dq
    dv_ref[rows, :] = dv
    dk_scr[rows, :] = dk
    db_scr[rows, :] = db
    dstate[...] = dst_n * ebl_last + bd_ref[...] * _dot_tn(dob, qe.astype(BF16))


def _hgrn_bwd2(name, proj, lb_f, lb_b, st_f, st_b, do, b, s, hc_f, hc_b):
    t = proj.shape[0]
    nb = s // HBLK
    n_ch = HBLK // CHUNK

    def body(qf_ref, zf_ref, vf_ref, dof_ref, stf_ref, qb_ref, zb_ref, vb_ref, dob_ref, stb_ref, lbf_ref, lbb_ref,
             csf_ref, csb_ref, cstf_ref, cstb_ref, seg_ref, bd_ref, sel_ref, selt_ref, seld_ref,
             dqf_ref, dvf_ref, dzf_ref, dqb_ref, dvb_ref, dzb_ref, dlbf_ref, dlbb_ref,
             dstate_f, dstate_b, states_f, states_b, bf_scr, bb_scr, kf_scr, kb_scr, dbf_scr, dbb_scr, dkf_scr,
             dkb_scr):
        first = jnp.logical_and(pl.program_id(0) == 0, pl.program_id(1) == 0)

        @pl.when(first)
        def _():
            dlbf_ref[...] = jnp.zeros_like(dlbf_ref)
            dlbb_ref[...] = jnp.zeros_like(dlbb_ref)

        @pl.when(pl.program_id(1) == 0)
        def _():
            dstate_f[...] = jnp.zeros_like(dstate_f)
            dstate_b[...] = jnp.zeros_like(dstate_b)

        gates = []
        for z_ref, lb_ref, cs_ref, k_scr, b_scr in ((zf_ref, lbf_ref, csf_ref, kf_scr, bf_scr),
                                                     (zb_ref, lbb_ref, csb_ref, kb_scr, bb_scr)):
            sig, f, g, sn, kk = _gates(z_ref[...], lb_ref[...])
            k_scr[...] = kk
            b_scr[...] = _ldot3(cs_ref[...], g)
            gates.append((sig, f, sn))

        def replay(i, carry):
            return (_scan_replay(i, False, carry[0], vf_ref, kf_scr, bf_scr, states_f, bd_ref),
                    _scan_replay(n_ch - 1 - i, True, carry[1], vb_ref, kb_scr, bb_scr, states_b, bd_ref))

        lax.fori_loop(0, n_ch, replay, (stf_ref[...], stb_ref[...]))

        def chunk(i, carry):
            _scan_chunk_bwd(n_ch - 1 - i, False, qf_ref, vf_ref, dof_ref, kf_scr, bf_scr, states_f, dstate_f, dqf_ref,
                            dvf_ref, dkf_scr, dbf_scr, seg_ref, bd_ref, sel_ref, selt_ref, seld_ref)
            _scan_chunk_bwd(i, True, qb_ref, vb_ref, dob_ref, kb_scr, bb_scr, states_b, dstate_b, dqb_ref,
                            dvb_ref, dkb_scr, dbb_scr, seg_ref, bd_ref, sel_ref, selt_ref, seld_ref)
            return carry

        lax.fori_loop(0, n_ch, chunk, 0)
        for (sig, f, sn), lb_ref, cst_ref, db_scr, dk_scr, dz_ref, dlb_ref in (
                (gates[0], lbf_ref, cstf_ref, dbf_scr, dkf_scr, dzf_ref, dlbf_ref),
                (gates[1], lbb_ref, cstb_ref, dbb_scr, dkb_scr, dzb_ref, dlbb_ref)):
            lbv = lb_ref[...]
            hi, lo = _split2(db_scr[...])
            dg = (jnp.dot(cst_ref[...], hi, preferred_element_type=F32)
                  + jnp.dot(cst_ref[...], lo, preferred_element_type=F32))
            dgf = jnp.where(f > F_MIN, dg / f, 0.0)
            dk = dk_scr[...]
            dz_ref[...] = dgf * (1.0 - lbv) * sig * (1.0 - sig) - dk * (1.0 - lbv) * sn * (1.0 - sn)
            dlb_ref[...] += _rowgroups(dgf * (1.0 - sig) - dk * sn)

    def col(c, rev):
        return pl.BlockSpec((HBLK, D_HGRN), lambda bi, j: (bi * nb + (j if rev else (nb - 1 - j)), c))

    def st_spec(rev):
        return pl.BlockSpec((None, D_HGRN, D_HGRN), lambda bi, j: (bi * nb + (j if rev else (nb - 1 - j)), 0, 0))

    sq = (D_HGRN, D_HGRN)
    blk = (HBLK, D_HGRN)
    pairs_shape = (CHUNK, CHUNK * CHUNK)
    return pl.pallas_call(
        body, name=name, grid=(b, nb),
        in_specs=[col(COL_HQ, False), col(COL_FF, False), col(COL_HI, False), col(0, False), st_spec(False),
                  col(COL_HQ, True), col(COL_FB, True), col(COL_HI, True), col(0, True), st_spec(True),
                  _full((1, D_HGRN)), _full((1, D_HGRN)), _full((HBLK, HBLK)), _full((HBLK, HBLK)),
                  _full((HBLK, HBLK)), _full((HBLK, HBLK)), _full(sq), _full(sq),
                  _full(pairs_shape), _full(pairs_shape), _full(pairs_shape)],
        out_specs=[col(0, False)] * 3 + [col(0, True)] * 3 + [_full((8, D_HGRN))] * 2,
        out_shape=[_sds((t, D_HGRN), F32)] * 6 + [_sds((8, D_HGRN), F32)] * 2,
        scratch_shapes=[pltpu.VMEM(sq, F32)] * 2 + [pltpu.VMEM((n_ch,) + sq, F32)] * 2 + [pltpu.VMEM(blk, F32)] * 8,
        compiler_params=_params(("arbitrary", "arbitrary")),
    )(proj, proj, proj, do, st_f, proj, proj, proj, do, st_b, lb_f, lb_b, hc_f["cs"], hc_b["cs"], hc_f["cs_t"],
      hc_b["cs_t"], hc_f["seg"], hc_f["bd"], hc_f["sel"], hc_f["selt"], hc_f["seld"])


def _lower_bounds(logits):
    n = logits.shape[1]

    def body(x_ref, o_ref):
        x = x_ref[...]
        for d in range(2):
            rows = [x[l * 2 + d:l * 2 + d + 1, :] for l in range(DEPTH)]
            mx = functools.reduce(jnp.maximum, rows)
            ex = [jnp.exp(r - mx) for r in rows]
            tot = functools.reduce(lambda a, c: a + c, ex)
            sm = [e / tot for e in ex]
            run = jnp.zeros_like(sm[0])
            for l in range(DEPTH):
                run = run + sm[l]
                o_ref[l * 2 + d:l * 2 + d + 1, :] = run - sm[0]

    return pl.pallas_call(body, name="hgrn_lower_bounds", out_shape=_sds(logits.shape, F32),
                          in_specs=[_full(logits.shape)], out_specs=_full(logits.shape), grid=(1,),
                          compiler_params=_params(("arbitrary",)))(logits)


def _lower_bounds_bwd(logits, dlb):
    def body(x_ref, g_ref, o_ref):
        x = x_ref[...]
        gv = g_ref[...]
        for d in range(2):
            rows = [x[l * 2 + d:l * 2 + d + 1, :] for l in range(DEPTH)]
            gr = [gv[l * 2 + d:l * 2 + d + 1, :] for l in range(DEPTH)]
            mx = functools.reduce(jnp.maximum, rows)
            ex = [jnp.exp(r - mx) for r in rows]
            tot = functools.reduce(lambda a, c: a + c, ex)
            sm = [e / tot for e in ex]
            dsm = []
            for i in range(DEPTH):
                acc = functools.reduce(lambda a, c: a + c, gr[i:])
                if i == 0:
                    acc = acc - functools.reduce(lambda a, c: a + c, gr)
                dsm.append(acc)
            inner = functools.reduce(lambda a, c: a + c, [sm[i] * dsm[i] for i in range(DEPTH)])
            for i in range(DEPTH):
                o_ref[i * 2 + d:i * 2 + d + 1, :] = sm[i] * (dsm[i] - inner)

    return pl.pallas_call(body, name="hgrn_lower_bounds_bwd", out_shape=_sds(logits.shape, F32),
                          in_specs=[_full(logits.shape), _full(logits.shape)], out_specs=_full(logits.shape),
                          grid=(1,), compiler_params=_params(("arbitrary",)))(logits, dlb)


def _conv_rows(s):
    return s + 2 * (CONV_PAD + 1)


def _conv_fwd(name, proj, dw_w, dw_b, ln_w, ln_b, pw_w, pw_b, b, s):
    t = proj.shape[0]
    pad = CONV_PAD + 1
    nt = s // CONV_TILE

    def body(a_ref, g_ref, w_ref, dwb_ref, lnw_ref, lnb_ref, pw_ref, pwb_ref, y_ref, upad, win):
        upad[0:pad, :] = jnp.zeros((pad, D_CONV), F32)
        upad[s + pad:s + 2 * pad, :] = jnp.zeros((pad, D_CONV), F32)

        def fill(i, carry):
            rows = pl.ds(pl.multiple_of(i * CONV_TILE, CONV_TILE), CONV_TILE)
            upad[pl.ds(pl.multiple_of(i * CONV_TILE + pad, pad), CONV_TILE), :] = a_ref[rows, :] * _sigmoid(g_ref[rows, :])
            return carry

        lax.fori_loop(0, nt, fill, 0)

        def tile(i, carry):
            r0 = pl.multiple_of(i * CONV_TILE, CONV_TILE)
            win[...] = upad[pl.ds(r0, CONV_TILE + 2 * pad), :]
            acc = jnp.zeros((CONV_TILE, D_CONV), F32)
            for j in range(CONV_W):
                acc = acc + win[j + 1:j + 1 + CONV_TILE, :] * w_ref[j:j + 1, :]
            c = acc + dwb_ref[...]
            mu = jnp.mean(c, axis=-1, keepdims=True)
            xc = c - mu
            rstd = lax.rsqrt(jnp.mean(xc * xc, axis=-1, keepdims=True) + LN_EPS)
            n = xc * rstd * lnw_ref[...] + lnb_ref[...]
            y_ref[pl.ds(r0, CONV_TILE), :] = (jnp.dot(_silu(n).astype(BF16), pw_ref[...].astype(BF16),
                                                      preferred_element_type=F32) + pwb_ref[...])
            return carry

        lax.fori_loop(0, nt, tile, 0)

    vec = _full((1, D_CONV))
    return pl.pallas_call(
        body, name=name, grid=(b,),
        in_specs=[pl.BlockSpec((s, D_CONV), lambda bi: (bi, COL_CA)), pl.BlockSpec((s, D_CONV), lambda bi: (bi, COL_CB)),
                  _full((CONV_W + 1, D_CONV)), vec, vec, vec, _full((D_CONV, D_CONV)), vec],
        out_specs=pl.BlockSpec((s, D_CONV), lambda bi: (bi, 0)), out_shape=_sds((t, D_CONV), F32),
        scratch_shapes=[pltpu.VMEM((_conv_rows(s), D_CONV), F32), pltpu.VMEM((CONV_TILE + 2 * pad, D_CONV), F32)],
        compiler_params=_params(("parallel",)),
    )(proj, proj, dw_w, dw_b, ln_w, ln_b, pw_w, pw_b)


def _conv_bwd(name, proj, dw_w, dw_b, ln_w, ln_b, pw_w, dy, b, s):
    t = proj.shape[0]
    pad = CONV_PAD + 1
    nt = s // CONV_TILE

    def body(a_ref, g_ref, w_ref, dwb_ref, lnw_ref, lnb_ref, pw_ref, dy_ref, dab_ref, dpw_ref, ddw_ref, dvec_ref,
             upad, dcpad, tap_acc, win, dwin):
        @pl.when(pl.program_id(0) == 0)
        def _():
            dpw_ref[...] = jnp.zeros_like(dpw_ref)
            ddw_ref[...] = jnp.zeros_like(ddw_ref)
            dvec_ref[...] = jnp.zeros_like(dvec_ref)

        zeros = jnp.zeros((pad, D_CONV), F32)
        upad[0:pad, :] = zeros
        upad[s + pad:s + 2 * pad, :] = zeros
        dcpad[0:pad, :] = zeros
        dcpad[s + pad:s + 2 * pad, :] = zeros
        tap_acc[...] = jnp.zeros_like(tap_acc)

        def inner(i):
            return pl.ds(pl.multiple_of(i * CONV_TILE + pad, pad), CONV_TILE)

        def fill(i, carry):
            rows = pl.ds(pl.multiple_of(i * CONV_TILE, CONV_TILE), CONV_TILE)
            upad[inner(i), :] = a_ref[rows, :] * _sigmoid(g_ref[rows, :])
            return carry

        lax.fori_loop(0, nt, fill, 0)

        def tile_a(i, carry):
            r0 = pl.multiple_of(i * CONV_TILE, CONV_TILE)
            win[...] = upad[pl.ds(r0, CONV_TILE + 2 * pad), :]
            acc = jnp.zeros((CONV_TILE, D_CONV), F32)
            for j in range(CONV_W):
                acc = acc + win[j + 1:j + 1 + CONV_TILE, :] * w_ref[j:j + 1, :]
            c = acc + dwb_ref[...]
            mu = jnp.mean(c, axis=-1, keepdims=True)
            xc = c - mu
            rstd = lax.rsqrt(jnp.mean(xc * xc, axis=-1, keepdims=True) + LN_EPS)
            xhat = xc * rstd
            n = xhat * lnw_ref[...] + lnb_ref[...]
            dyt = dy_ref[pl.ds(r0, CONV_TILE), :]
            dyb = dyt.astype(BF16)
            dpw_ref[...] += _dot_tn(_silu(n).astype(BF16), dyb)
            dn = _dot_nt(dyb, pw_ref[...].astype(BF16)) * _dsilu(n)
            dxh = dn * lnw_ref[...]
            dc = rstd * (dxh - jnp.mean(dxh, axis=-1, keepdims=True)
                         - xhat * jnp.mean(dxh * xhat, axis=-1, keepdims=True))
            dcpad[inner(i), :] = dc
            dvec_ref[0:1, :] += jnp.sum(dyt, axis=0, keepdims=True)
            dvec_ref[1:2, :] += jnp.sum(dn * xhat, axis=0, keepdims=True)
            dvec_ref[2:3, :] += jnp.sum(dn, axis=0, keepdims=True)
            dvec_ref[3:4, :] += jnp.sum(dc, axis=0, keepdims=True)
            return carry

        lax.fori_loop(0, nt, tile_a, 0)

        def tile_b(i, carry):
            r0 = pl.multiple_of(i * CONV_TILE, CONV_TILE)
            win[...] = upad[pl.ds(r0, CONV_TILE + 2 * pad), :]
            dwin[...] = dcpad[pl.ds(r0, CONV_TILE + 2 * pad), :]
            dct = dwin[pad:pad + CONV_TILE, :]
            du = jnp.zeros((CONV_TILE, D_CONV), F32)
            for j in range(CONV_W):
                du = du + dwin[2 * pad - 1 - j:2 * pad - 1 - j + CONV_TILE, :] * w_ref[j:j + 1, :]
                tap_acc[8 * j:8 * j + 8, :] += _rowgroups(dct * win[j + 1:j + 1 + CONV_TILE, :])
            rows = pl.ds(r0, CONV_TILE)
            sg = _sigmoid(g_ref[rows, :])
            dab_ref[rows, 0:D_CONV] = (du * sg).astype(BF16)
            dab_ref[rows, D_CONV:2 * D_CONV] = (du * a_ref[rows, :] * sg * (1.0 - sg)).astype(BF16)
            return carry

        lax.fori_loop(0, nt, tile_b, 0)
        for j in range(CONV_W):
            ddw_ref[j:j + 1, :] += jnp.sum(tap_acc[8 * j:8 * j + 8, :], axis=0, keepdims=True)

    vec = _full((1, D_CONV))
    return pl.pallas_call(
        body, name=name, grid=(b,),
        in_specs=[pl.BlockSpec((s, D_CONV), lambda bi: (bi, COL_CA)), pl.BlockSpec((s, D_CONV), lambda bi: (bi, COL_CB)),
                  _full((CONV_W + 1, D_CONV)), vec, vec, vec, _full((D_CONV, D_CONV)),
                  pl.BlockSpec((s, D_CONV), lambda bi: (bi, 0))],
        out_specs=[pl.BlockSpec((s, 2 * D_CONV), lambda bi: (bi, 0)), _full((D_CONV, D_CONV)),
                   _full((CONV_W + 1, D_CONV)), _full((8, D_CONV))],
        out_shape=[_sds((t, 2 * D_CONV), BF16), _sds((D_CONV, D_CONV), F32), _sds((CONV_W + 1, D_CONV), F32),
                   _sds((8, D_CONV), F32)],
        scratch_shapes=[pltpu.VMEM((_conv_rows(s), D_CONV), F32), pltpu.VMEM((_conv_rows(s), D_CONV), F32),
                        pltpu.VMEM((8 * CONV_W, D_CONV), F32), pltpu.VMEM((CONV_TILE + 2 * pad, D_CONV), F32),
                        pltpu.VMEM((CONV_TILE + 2 * pad, D_CONV), F32)],
        compiler_params=_params(("arbitrary",)),
    )(proj, proj, dw_w, dw_b, ln_w, ln_b, pw_w, dy)


def _mix_fwd(name, y_attn, o_fw, o_bw, proj, y_conv, aw, gw, cw, seg):
    t = y_attn.shape[0]
    tm = _row_tile(t)

    def body(ya_ref, of_ref, ob_ref, hg_ref, yc_ref, aw_ref, gw_ref, cw_ref, seg_ref, o_ref):
        ya = ya_ref[...]
        ra = lax.rsqrt(jnp.mean(ya * ya, axis=-1, keepdims=True) + EPS)
        o_ref[:, 0:D_ATTN] = (ya * ra * aw_ref[...]).astype(BF16)
        o = of_ref[...] + ob_ref[...]
        ro = lax.rsqrt(jnp.dot((o * o).astype(BF16), seg_ref[...], preferred_element_type=F32) + EPS)
        o_ref[:, D_ATTN:D_ATTN + D_HGRN] = (o * ro * gw_ref[...] * _silu(hg_ref[...])).astype(BF16)
        yc = yc_ref[...]
        rc = lax.rsqrt(jnp.mean(yc * yc, axis=-1, keepdims=True) + EPS)
        o_ref[:, D_ATTN + D_HGRN:D_MODEL] = (yc * rc * cw_ref[...]).astype(BF16)

    def tile(w, c=0):
        return pl.BlockSpec((tm, w), lambda i: (i, c))

    return pl.pallas_call(
        body, name=name, grid=(t // tm,),
        in_specs=[tile(D_ATTN), tile(D_HGRN), tile(D_HGRN), tile(D_HGRN, COL_HG), tile(D_CONV),
                  _full((1, D_ATTN)), _full((1, D_HGRN)), _full((1, D_CONV)), _full((D_HGRN, D_HGRN))],
        out_specs=tile(D_MODEL), out_shape=_sds((t, D_MODEL), BF16),
        compiler_params=_params(("parallel",)),
    )(y_attn, o_fw, o_bw, proj, y_conv, aw, gw, cw, seg)


def _mix_bwd(name, dmix, y_attn, o_fw, o_bw, proj, y_conv, aw, gw, cw, seg):
    t = y_attn.shape[0]
    tm = _row_tile(t)

    def rms_bwd(x, w, dy):
        r = lax.rsqrt(jnp.mean(x * x, axis=-1, keepdims=True) + EPS)
        gwv = dy * w
        return r * gwv - x * (r * r * r) * jnp.mean(gwv * x, axis=-1, keepdims=True), _rowgroups(dy * x * r)

    def body(dm_ref, ya_ref, of_ref, ob_ref, hg_ref, yc_ref, aw_ref, gw_ref, cw_ref, seg_ref,
             dya_ref, do_ref, dhg_ref, dyc_ref, daw_ref, dgw_ref, dcw_ref):
        @pl.when(pl.program_id(0) == 0)
        def _():
            daw_ref[...] = jnp.zeros_like(daw_ref)
            dgw_ref[...] = jnp.zeros_like(dgw_ref)
            dcw_ref[...] = jnp.zeros_like(dcw_ref)

        dya, daw = rms_bwd(ya_ref[...], aw_ref[...], dm_ref[:, 0:D_ATTN])
        dya_ref[...] = dya
        daw_ref[...] += daw
        dyc, dcw = rms_bwd(yc_ref[...], cw_ref[...], dm_ref[:, D_ATTN + D_HGRN:D_MODEL])
        dyc_ref[...] = dyc
        dcw_ref[...] += dcw
        d2 = dm_ref[:, D_ATTN:D_ATTN + D_HGRN]
        o = of_ref[...] + ob_ref[...]
        hg = hg_ref[...]
        ro = lax.rsqrt(jnp.dot((o * o).astype(BF16), seg_ref[...], preferred_element_type=F32) + EPS)
        dn = d2 * _silu(hg)
        dhg_ref[...] = (d2 * o * ro * gw_ref[...] * _dsilu(hg)).astype(BF16)
        gwv = dn * gw_ref[...]
        do_ref[...] = ro * gwv - o * (ro * ro * ro) * _rdot2(gwv * o, seg_ref[...])
        dgw_ref[...] += _rowgroups(dn * o * ro)

    def tile(w, c=0):
        return pl.BlockSpec((tm, w), lambda i: (i, c))

    return pl.pallas_call(
        body, name=name, grid=(t // tm,),
        in_specs=[tile(D_MODEL), tile(D_ATTN), tile(D_HGRN), tile(D_HGRN), tile(D_HGRN, COL_HG), tile(D_CONV),
                  _full((1, D_ATTN)), _full((1, D_HGRN)), _full((1, D_CONV)), _full((D_HGRN, D_HGRN))],
        out_specs=[tile(D_ATTN), tile(D_HGRN), tile(D_HGRN), tile(D_CONV),
                   _full((8, D_ATTN)), _full((8, D_HGRN)), _full((8, D_CONV))],
        out_shape=[_sds((t, D_ATTN), F32), _sds((t, D_HGRN), F32), _sds((t, D_HGRN), BF16), _sds((t, D_CONV), F32),
                   _sds((8, D_ATTN), F32), _sds((8, D_HGRN), F32), _sds((8, D_CONV), F32)],
        compiler_params=_params(("arbitrary",)),
    )(dmix, y_attn, o_fw, o_bw, proj, y_conv, aw, gw, cw, seg)


def _hgrn_dproj(name, dq_f, dq_b, dz_fw, dz_bw, dv_f, dv_b, dhg):
    t = dq_f.shape[0]
    tm = _row_tile(t)

    def body(qf_ref, qb_ref, zf_ref, zb_ref, vf_ref, vb_ref, hg_ref, o_ref):
        cols = (qf_ref[...] + qb_ref[...], zf_ref[...], zb_ref[...], vf_ref[...] + vb_ref[...], hg_ref[...])
        for i, val in enumerate(cols):
            o_ref[:, i * D_HGRN:(i + 1) * D_HGRN] = val.astype(BF16)

    tile = pl.BlockSpec((tm, D_HGRN), lambda i: (i, 0))
    return pl.pallas_call(
        body, name=name, grid=(t // tm,), in_specs=[tile] * 7,
        out_specs=pl.BlockSpec((tm, 5 * D_HGRN), lambda i: (i, 0)), out_shape=_sds((t, 5 * D_HGRN), BF16),
        compiler_params=_params(("parallel",)),
    )(dq_f, dq_b, dz_fw, dz_bw, dv_f, dv_b, dhg)


def _mm_tile(t):
    return min(512, t)


def _resident(shape):
    n = len(shape)
    return pl.BlockSpec(tuple(shape), lambda *_: (0,) * n, pipeline_mode=pl.Buffered(1))


def _w_blk(rows, cols, j_of):
    return pl.BlockSpec((None, rows, cols), lambda *g: (j_of(*g), 0, 0))


def _layer_fwd(l, x, wget, sm, tabs, cst, b, s, deps):
    t = x.shape[0]
    tm = _mm_tile(t)
    nt = t // tm
    pre = "l%d_" % l
    row = lambda w: pl.BlockSpec((tm, w), lambda i, *_: (i, 0))

    h1 = _rms_fwd(pre + "mix_norm", x, sm["mix_norm_w"][l], deps)
    def in_body(h_ref, w_ref, o_ref):
        hv = h_ref[...]
        for j in range(N_CHIP):
            o_ref[:, j * IN_BLK:(j + 1) * IN_BLK] = jnp.dot(hv, w_ref[j], preferred_element_type=F32)

    w_in = wget(l, "w_in", h1)
    proj = pl.pallas_call(
        in_body, name=pre + "in_proj", grid=(nt,), in_specs=[row(D_MODEL), _resident(w_in.shape)],
        out_specs=row(D_IN), out_shape=_sds((t, D_IN), F32), compiler_params=_params(("parallel",)),
    )(h1, w_in)
    qn, kr, vr = _attn_prep(pre + "attn_prep", proj, s, tabs, sm["q_norm_w"][l], sm["k_norm_w"][l], cst["attn"])
    y_attn = _attn_fwd(pre + "attn", qn, kr, vr, b, s)
    o_fw, o_bw, st_fw, st_bw = _hgrn_fwd2(pre + "hgrn", proj, sm["lb"][l][0], sm["lb"][l][1], b, s, cst["hg_fw"],
                                          cst["hg_bw"])
    y_conv = _conv_fwd(pre + "conv", proj, sm["conv_dw_w"][l], sm["conv_dw_b"][l], sm["conv_ln_w"][l],
                       sm["conv_ln_b"][l], sm["conv_pw_w"][l], sm["conv_pw_b"][l], b, s)
    mixed = _mix_fwd(pre + "mix", y_attn, o_fw, o_bw, proj, y_conv, sm["attn_out_norm_w"][l], sm["gnorm_w"][l],
                     sm["conv_out_norm_w"][l], cst["seg_h"])
    (x1,) = _mm(pre + "out_proj", (nt,),
                [(mixed, row(D_MODEL), wget(l, "w_out", mixed),
                  pl.BlockSpec((N_CHIP, OUT_BLK, D_MODEL), lambda i: (0, 0, 0)), NN)],
                [(x, row(D_MODEL))], [(_sds((t, D_MODEL), F32), row(D_MODEL))],
                lambda tot, xr: (xr + tot,))
    h2 = _rms_fwd(pre + "ffn_norm", x1, sm["ffn_norm_w"][l])
    ff3 = pl.BlockSpec((N_CHIP, tm, FF_BLK), lambda i: (0, i, 0))
    ffs = _sds((N_CHIP, t, FF_BLK), BF16)

    def gu_body(h_ref, wg_ref, wu_ref, g_ref, u_ref, a_ref):
        hv = h_ref[...]
        for j in range(N_CHIP):
            gv = jnp.dot(hv, wg_ref[j], preferred_element_type=F32)
            uv = jnp.dot(hv, wu_ref[j], preferred_element_type=F32)
            g_ref[j] = gv.astype(BF16)
            u_ref[j] = uv.astype(BF16)
            a_ref[j] = (_silu(gv) * uv).astype(BF16)

    w_gate, w_up = wget(l, "w_gate", h2), wget(l, "w_up", h2)
    gate, up, act = pl.pallas_call(
        gu_body, name=pre + "ffn_gate_up", grid=(nt,),
        in_specs=[row(D_MODEL), _resident(w_gate.shape), _resident(w_up.shape)],
        out_specs=[ff3, ff3, ff3], out_shape=[ffs, ffs, ffs], compiler_params=_params(("parallel",)),
    )(h2, w_gate, w_up)

    def down_body(a_ref, w_ref, x_ref, o_ref):
        tot = x_ref[...]
        for j in range(N_CHIP):
            tot = tot + jnp.dot(a_ref[j], w_ref[j], preferred_element_type=F32)
        o_ref[...] = tot

    w_down = wget(l, "w_down", act)
    x2 = pl.pallas_call(
        down_body, name=pre + "ffn_down", grid=(nt,), in_specs=[ff3, _resident(w_down.shape), row(D_MODEL)],
        out_specs=row(D_MODEL), out_shape=_sds((t, D_MODEL), F32), compiler_params=_params(("parallel",)),
    )(act, w_down, x1)
    saved = dict(x=x, h1=h1, proj=proj, qn=qn, kr=kr, vr=vr, y_attn=y_attn, o_fw=o_fw, o_bw=o_bw, st_fw=st_fw,
                 st_bw=st_bw, y_conv=y_conv, mixed=mixed, x1=x1, h2=h2, gate=gate, up=up, act=act)
    return x2, saved


def _layer_bwd(l, dx2, sv, wget, sm, tabs, cst, b, s, on_grads):
    t = dx2.shape[0]
    tm = _mm_tile(t)
    nt = t // tm
    pre = "l%d_" % l
    tk = min(2048, t)
    nk = t // tk
    row = lambda w: pl.BlockSpec((tm, w), lambda i, *_: (i, 0))
    ff3 = pl.BlockSpec((N_CHIP, tm, FF_BLK), lambda i: (0, i, 0))
    ffs = _sds((N_CHIP, t, FF_BLK), BF16)

    w_down, w_gate, w_up = wget(l, "w_down", dx2), wget(l, "w_gate", dx2), wget(l, "w_up", dx2)

    def ddx_body(dx_ref, w_ref, g_ref, u_ref, dg_ref, du_ref):
        dxb = dx_ref[...].astype(BF16)
        for j in range(N_CHIP):
            da = _dot_nt(dxb, w_ref[j])
            g = g_ref[j].astype(F32)
            dg_ref[j] = (da * u_ref[j].astype(F32) * _dsilu(g)).astype(BF16)
            du_ref[j] = (da * _silu(g)).astype(BF16)

    dgate, dup = pl.pallas_call(
        ddx_body, name=pre + "ffn_down_dx", grid=(nt,), in_specs=[row(D_MODEL), _resident(w_down.shape), ff3, ff3],
        out_specs=[ff3, ff3], out_shape=[ffs, ffs], compiler_params=_params(("parallel",)),
    )(dx2, w_down, sv["gate"], sv["up"])
    colt = lambda w: pl.BlockSpec((tk, w), lambda j, k: (k, 0))
    fft = pl.BlockSpec((None, tk, FF_BLK), lambda j, k: (j, k, 0))
    (g_down,) = _mm(pre + "ffn_down_dw", (N_CHIP, nk), [(sv["act"], fft, dx2, colt(D_MODEL), TN)], [],
                    [(_sds((N_CHIP, FF_BLK, D_MODEL), BF16), pl.BlockSpec((None, FF_BLK, D_MODEL), lambda j, k: (j, 0, 0)))],
                    lambda tot: (tot,), acc=(1, (FF_BLK, D_MODEL)))
    wff = pl.BlockSpec((None, D_MODEL, FF_BLK), lambda j, k: (j, 0, 0))
    (g_gate,) = _mm(pre + "ffn_gate_dw", (N_CHIP, nk), [(sv["h2"], colt(D_MODEL), dgate, fft, TN)], [],
                    [(_sds((N_CHIP, D_MODEL, FF_BLK), BF16), wff)], lambda tot: (tot,), acc=(1, (D_MODEL, FF_BLK)))
    (g_up,) = _mm(pre + "ffn_up_dw", (N_CHIP, nk), [(sv["h2"], colt(D_MODEL), dup, fft, TN)], [],
                  [(_sds((N_CHIP, D_MODEL, FF_BLK), BF16), wff)], lambda tot: (tot,), acc=(1, (D_MODEL, FF_BLK)))

    def norm_bwd_tail(dh, x_ref, nw_ref, dres_ref, dx_ref, dw_ref):
        xv = x_ref[...]
        r = lax.rsqrt(jnp.mean(xv * xv, axis=-1, keepdims=True) + EPS)
        gw = dh * nw_ref[...]
        dx_ref[...] = dres_ref[...] + r * gw - xv * (r * r * r) * jnp.mean(gw * xv, axis=-1, keepdims=True)

        @pl.when(pl.program_id(0) == 0)
        def _():
            dw_ref[...] = jnp.zeros_like(dw_ref)

        dw_ref[...] += _rowgroups(dh * xv * r)

    def dh_body(dg_ref, du_ref, wg_ref, wu_ref, x_ref, nw_ref, dres_ref, *rest):
        tot = None
        for j in range(N_CHIP):
            r = _dot_nt(dg_ref[j], wg_ref[j]) + _dot_nt(du_ref[j], wu_ref[j])
            tot = r if tot is None else tot + r
        norm_bwd_tail(tot, x_ref, nw_ref, dres_ref, *rest[-2:])

    deps = on_grads(l, dict(w_gate=g_gate, w_up=g_up, w_down=g_down))
    dx1, d_ffn_norm = pl.pallas_call(
        dh_body, name=pre + "ffn_dh", grid=(nt,),
        in_specs=[ff3, ff3, _resident(w_gate.shape), _resident(w_up.shape), row(D_MODEL), _full((1, D_MODEL)),
                  row(D_MODEL)] + [_full(a.shape) for a in deps],
        out_specs=[row(D_MODEL), _full((8, D_MODEL))], out_shape=[_sds((t, D_MODEL), F32), _sds((8, D_MODEL), F32)],
        compiler_params=_params(("arbitrary",)),
    )(dgate, dup, w_gate, w_up, sv["x1"], sm["ffn_norm_w"][l], dx2, *deps)

    (dmix,) = _mm(pre + "out_proj_dx", (nt,),
                  [(dx1, row(D_MODEL), wget(l, "w_out", dx2),
                    pl.BlockSpec((N_CHIP, OUT_BLK, D_MODEL), lambda i: (0, 0, 0)), NT)], [],
                  [(_sds((t, D_MODEL), F32), row(D_MODEL))], lambda tot: (tot,))
    (g_out,) = _mm(pre + "out_proj_dw", (N_CHIP, nk),
                   [(sv["mixed"], pl.BlockSpec((tk, OUT_BLK), lambda j, k: (k, j)), dx1, colt(D_MODEL), TN)], [],
                   [(_sds((N_CHIP, OUT_BLK, D_MODEL), BF16), pl.BlockSpec((None, OUT_BLK, D_MODEL), lambda j, k: (j, 0, 0)))],
                   lambda tot: (tot,), acc=(1, (OUT_BLK, D_MODEL)))
    proj = sv["proj"]
    dya, do_h, dhg, dyc, d_aw, d_gw, d_cw = _mix_bwd(
        pre + "mix_bwd", dmix, sv["y_attn"], sv["o_fw"], sv["o_bw"], proj, sv["y_conv"],
        sm["attn_out_norm_w"][l], sm["gnorm_w"][l], sm["conv_out_norm_w"][l], cst["seg_h"])
    dqs, dkr, dvr = _attn_bwd(pre + "attn_bwd", sv["qn"], sv["kr"], sv["vr"], dya, b, s)
    dp_attn, d_qw, d_kw = _attn_prep_bwd(pre + "attn_prep_bwd", proj, s, tabs, sm["q_norm_w"][l], sm["k_norm_w"][l],
                                         cst["attn"], dqs, dkr, dvr)
    dq_f, dv_f, dz_fw, dq_b, dv_b, dz_bw, dlb_fw, dlb_bw = _hgrn_bwd2(
        pre + "hgrn_bwd", proj, sm["lb"][l][0], sm["lb"][l][1], sv["st_fw"], sv["st_bw"], do_h, b, s,
        cst["hg_fw"], cst["hg_bw"])
    dp_hgrn = _hgrn_dproj(pre + "hgrn_dproj", dq_f, dq_b, dz_fw, dz_bw, dv_f, dv_b, dhg)
    dp_conv, d_pw, d_dw, d_cvec = _conv_bwd(pre + "conv_bwd", proj, sm["conv_dw_w"][l], sm["conv_dw_b"][l],
                                            sm["conv_ln_w"][l], sm["conv_ln_b"][l], sm["conv_pw_w"][l], dyc, b, s)
    dproj = jnp.concatenate([dp_attn, dp_hgrn, dp_conv], axis=1)

    (g_in,) = _mm(pre + "in_proj_dw", (N_CHIP, nk),
                  [(sv["h1"], colt(D_MODEL), dproj, pl.BlockSpec((tk, IN_BLK), lambda j, k: (k, j)), TN)], [],
                  [(_sds((N_CHIP, D_MODEL, IN_BLK), BF16), pl.BlockSpec((None, D_MODEL, IN_BLK), lambda j, k: (j, 0, 0)))],
                  lambda tot: (tot,), acc=(1, (D_MODEL, IN_BLK)))

    def indx_body(dp_ref, w_ref, x_ref, nw_ref, dres_ref, *rest):
        tot = None
        for j in range(N_CHIP):
            r = _dot_nt(dp_ref[:, j * IN_BLK:(j + 1) * IN_BLK], w_ref[j])
            tot = r if tot is None else tot + r
        norm_bwd_tail(tot, x_ref, nw_ref, dres_ref, *rest[-2:])

    w_in = wget(l, "w_in", dx2)
    deps = on_grads(l, dict(w_in=g_in, w_out=g_out))
    dx, d_mix_norm = pl.pallas_call(
        indx_body, name=pre + "in_proj_dx", grid=(nt,),
        in_specs=[row(D_IN), _resident(w_in.shape), row(D_MODEL), _full((1, D_MODEL)), row(D_MODEL)]
        + [_full(a.shape) for a in deps],
        out_specs=[row(D_MODEL), _full((8, D_MODEL))], out_shape=[_sds((t, D_MODEL), F32), _sds((8, D_MODEL), F32)],
        compiler_params=_params(("arbitrary",)),
    )(dproj, w_in, sv["x"], sm["mix_norm_w"][l], dx1, *deps)
    heads = lambda v, n: v.sum(axis=0).reshape(n, HEAD_DIM).sum(axis=0)
    small = dict(
        mix_norm_w=d_mix_norm.sum(axis=0), q_norm_w=heads(d_qw, D_ATTN // HEAD_DIM), k_norm_w=heads(d_kw, N_KV),
        lb=jnp.stack([dlb_fw.sum(axis=0), dlb_bw.sum(axis=0)]), hgrn_gnorm_w=heads(d_gw, D_HGRN // HEAD_DIM),
        conv_dw_w=d_dw[:CONV_W], conv_dw_b=d_cvec[3], conv_ln_w=d_cvec[1], conv_ln_b=d_cvec[2], conv_pw_w=d_pw,
        conv_pw_b=d_cvec[0], attn_out_norm_w=d_aw.sum(axis=0), conv_out_norm_w=d_cw.sum(axis=0),
        ffn_norm_w=d_ffn_norm.sum(axis=0))
    return dx, small


SMALL_ORDER = ("mix_norm_w", "q_norm_w", "k_norm_w", "lb", "hgrn_gnorm_w", "conv_dw_w", "conv_dw_b", "conv_ln_w",
               "conv_ln_b", "conv_pw_w", "conv_pw_b", "attn_out_norm_w", "conv_out_norm_w", "ffn_norm_w")
BIG_ORDER = ("w_in", "w_out", "w_gate", "w_up", "w_down")


def _local_step(x, target, wget, sm, deps, on_grads):
    b, s, d = x.shape
    t = b * s
    cos, sin = _rope_tables(s)
    tabs = dict(cq=jnp.tile(cos, (1, D_ATTN // HEAD_DIM)), sq=jnp.tile(sin, (1, D_ATTN // HEAD_DIM)),
                ck=jnp.tile(cos, (1, N_KV)), sk=jnp.tile(sin, (1, N_KV)))
    cst = dict(attn=_attn_consts(), hg_fw=_hgrn_consts(False), hg_bw=_hgrn_consts(True),
               seg_h=_bf(_seg_matrix(D_HGRN, HEAD_DIM, 1.0 / HEAD_DIM)))
    vec = lambda a: a.reshape(DEPTH, 1, -1)
    smk = dict(sm)
    for n in ("mix_norm_w", "conv_dw_b", "conv_ln_w", "conv_ln_b", "conv_pw_b", "attn_out_norm_w", "conv_out_norm_w",
              "ffn_norm_w"):
        smk[n] = vec(sm[n])
    smk["q_norm_w"] = vec(jnp.tile(sm["q_norm_w"], (1, D_ATTN // HEAD_DIM)))
    smk["k_norm_w"] = vec(jnp.tile(sm["k_norm_w"], (1, N_KV)))
    smk["gnorm_w"] = vec(jnp.tile(sm["hgrn_gnorm_w"], (1, D_HGRN // HEAD_DIM)))
    smk["lb"] = sm["lb"].reshape(DEPTH, 2, 1, D_HGRN)
    smk["conv_dw_w"] = jnp.pad(sm["conv_dw_w"], ((0, 0), (0, 1), (0, 0)))

    h = x.reshape(t, d)
    saved = []
    for l in range(DEPTH):
        h, sv = _layer_fwd(l, h, wget, smk, tabs, cst, b, s, deps if l == 0 else ())
        saved.append(sv)
    dy, sq = _loss_kernel(h, target.reshape(t, d))
    sq_sum = jnp.sum(sq)
    dh = dy
    smalls = [None] * DEPTH
    for l in reversed(range(DEPTH)):
        dh, smalls[l] = _layer_bwd(l, dh, saved[l], wget, smk, tabs, cst, b, s, on_grads)
    return sq_sum, dh.reshape(b, s, d), smalls


HBM_SPEC = pl.BlockSpec(memory_space=pltpu.HBM)


def _exchange(name, arrs, mode):
    n = len(arrs)
    if mode == "gather8":
        flips = [(fx, fy, fc) for fx in (0, 1) for fy in (0, 1) for fc in (0, 1)][1:]
    elif mode == "sibling":
        flips = [(0, 0, 1)]
    else:
        flips = [(1, 0, 0), (0, 1, 0), (1, 1, 0)]
    n_f = len(flips)

    def body(*refs):
        ins, outs = refs[:n], refs[n:2 * n]
        send_sems, recv_sems, local_sems = refs[2 * n:]
        x, y, c = lax.axis_index("x"), lax.axis_index("y"), lax.axis_index("c")

        def slot_of(px, py, pc):
            return (2 * px + py) if mode != "gather8" else (4 * px + 2 * py + pc)

        me = slot_of(x, y, c)
        started = []
        for i in range(n):
            if mode != "sibling":
                src = ins[i].at[me] if mode == "scatter4" else ins[i]
                loc = pltpu.make_async_copy(src, outs[i].at[me], local_sems.at[i])
                loc.start()
                started.append(loc)
        sends, recvs = [], []
        for i in range(n):
            for f, (fx, fy, fc) in enumerate(flips):
                peer = (x ^ fx, y ^ fy, c ^ fc)
                ps = slot_of(*peer)
                if mode == "sibling":
                    src, dst, landed = ins[i], outs[i], outs[i]
                elif mode == "scatter4":
                    src, dst, landed = ins[i].at[ps], outs[i].at[me], outs[i].at[ps]
                else:
                    src, dst, landed = ins[i], outs[i].at[me], outs[i].at[ps]
                k = i * n_f + f
                cp = pltpu.make_async_remote_copy(src_ref=src, dst_ref=dst, send_sem=send_sems.at[k],
                                                  recv_sem=recv_sems.at[k], device_id=peer,
                                                  device_id_type=pl.DeviceIdType.MESH)
                cp.start()
                sends.append(cp)
                recvs.append(pltpu.make_async_remote_copy(src_ref=src, dst_ref=landed, send_sem=send_sems.at[k],
                                                          recv_sem=recv_sems.at[k], device_id=peer,
                                                          device_id_type=pl.DeviceIdType.MESH))
        for cp in sends:
            cp.wait_send()
        for cp in recvs:
            cp.wait_recv()
        for loc in started:
            loc.wait()

    def out_sds(a):
        if mode == "gather4":
            return _sds((N_CHIP,) + a.shape, a.dtype)
        if mode == "gather8":
            return _sds((N_DEV,) + a.shape, a.dtype)
        return _sds(a.shape, a.dtype)

    res = pl.pallas_call(
        body, name=name, in_specs=[HBM_SPEC] * n, out_specs=[HBM_SPEC] * n, out_shape=[out_sds(a) for a in arrs],
        scratch_shapes=[pltpu.SemaphoreType.DMA((n * n_f,)), pltpu.SemaphoreType.DMA((n * n_f,)),
                        pltpu.SemaphoreType.DMA((max(n, 1),))],
    )(*arrs)
    return list(res)


SEM_SPEC = pl.BlockSpec(memory_space=pltpu.SEMAPHORE)
SPLIT_EFFECT = pltpu.SideEffectType.DATAFLOW_SIDE_EFFECTING
CHIP_FLIPS = ((1, 0), (0, 1), (1, 1))


def _chip_copies(src_refs, land_refs, send_sems, recv_sems, scatter):
    x, y, c = lax.axis_index("x"), lax.axis_index("y"), lax.axis_index("c")
    me = 2 * x + y
    out = []
    for i, land in enumerate(land_refs):
        for f, (fx, fy) in enumerate(CHIP_FLIPS):
            peer = (x ^ fx, y ^ fy, c)
            ps = 2 * (x ^ fx) + (y ^ fy)
            src = src_refs[i].at[ps] if scatter else land.at[me]
            k = i * len(CHIP_FLIPS) + f
            kw = dict(send_sem=send_sems.at[k], recv_sem=recv_sems.at[k], device_id=peer,
                      device_id_type=pl.DeviceIdType.MESH)
            out.append((pltpu.make_async_remote_copy(src_ref=src, dst_ref=land.at[me], **kw),
                        pltpu.make_async_remote_copy(src_ref=src, dst_ref=land.at[ps], **kw)))
    return out


def _split_start(name, srcs, lands, scatter):
    n = len(lands)
    n_src = len(srcs)
    n_sem = n * len(CHIP_FLIPS)

    def body(*refs):
        src_refs = refs[:n_src]
        land_refs = refs[n_src:n_src + n]
        send_sems, recv_sems = refs[n_src + n], refs[n_src + n + 1]
        token = refs[-1]
        for start, _ in _chip_copies(src_refs, land_refs, send_sems, recv_sems, scatter):
            start.start()
        token[...] = jnp.zeros_like(token)

    arrs = list(srcs) + list(lands)
    res = pl.pallas_call(
        body, name=name,
        out_shape=(pltpu.SemaphoreType.DMA((n_sem,)), pltpu.SemaphoreType.DMA((n_sem,)),
                   *[pltpu.HBM(a.shape, a.dtype) for a in arrs], _sds((8, LANES), F32)),
        in_specs=[HBM_SPEC] * len(arrs),
        out_specs=(SEM_SPEC, SEM_SPEC, *[HBM_SPEC] * len(arrs), pl.BlockSpec(memory_space=pltpu.VMEM)),
        input_output_aliases={i: 2 + i for i in range(len(arrs))},
        compiler_params=pltpu.CompilerParams(has_side_effects=SPLIT_EFFECT),
    )(*[pltpu.with_memory_space_constraint(a, pltpu.HBM) for a in arrs])
    return dict(send=res[0], recv=res[1], srcs=list(res[2:2 + n_src]), lands=list(res[2 + n_src:2 + n_src + n]),
                token=res[-1], scatter=scatter)


def _split_wait(name, started, after):
    srcs, lands, scatter = started["srcs"], started["lands"], started["scatter"]
    n, n_src = len(lands), len(srcs)

    def body(*refs):
        src_refs = refs[:n_src]
        land_refs = refs[n_src:n_src + n]
        send_sems, recv_sems = refs[n_src + n], refs[n_src + n + 1]
        for _, wait in _chip_copies(src_refs, land_refs, send_sems, recv_sems, scatter):
            wait.wait_send()
            wait.wait_recv()

    arrs = list(srcs) + list(lands)
    res = pl.pallas_call(
        body, name=name, out_shape=tuple(pltpu.HBM(a.shape, a.dtype) for a in arrs),
        in_specs=[HBM_SPEC] * len(arrs) + [SEM_SPEC, SEM_SPEC, pl.BlockSpec(memory_space=pl.ANY)],
        out_specs=tuple([HBM_SPEC] * len(arrs)), input_output_aliases={i: i for i in range(len(arrs))},
        compiler_params=pltpu.CompilerParams(has_side_effects=SPLIT_EFFECT),
    )(*arrs, started["send"], started["recv"], after)
    return list(res[n_src:])


def _flat_tile(rows):
    for cand in (512, 256, 128, 64, 32, 16, 8):
        if rows % cand == 0:
            return cand
    return rows


def _cast_slot(name, a, l, chip):
    r, c = a.shape[0] // DEPTH, a.shape[1]
    tr = _flat_tile(r)

    def body(chip_ref, a_ref, o_ref):
        o_ref[...] = a_ref[...].astype(BF16)

    return pl.pallas_call(
        body, name=name, out_shape=_sds((N_CHIP, r, c), BF16),
        grid_spec=pltpu.PrefetchScalarGridSpec(
            num_scalar_prefetch=1, grid=(r // tr,),
            in_specs=[pl.BlockSpec((tr, c), lambda i, ch: (l * (r // tr) + i, 0))],
            out_specs=pl.BlockSpec((None, tr, c), lambda i, ch: (ch[0], i, 0))),
        compiler_params=_params(("parallel",)))(chip, a)


def _own_slot(name, g, chip):
    n, r, c = g.shape
    tr = _flat_tile(r)

    def body(chip_ref, g_ref, o_ref):
        o_ref[...] = g_ref[...]

    spec = pl.BlockSpec((None, tr, c), lambda i, ch: (ch[0], i, 0))
    return pl.pallas_call(
        body, name=name, out_shape=_sds(g.shape, g.dtype),
        grid_spec=pltpu.PrefetchScalarGridSpec(num_scalar_prefetch=1, grid=(r // tr,), in_specs=[spec], out_specs=spec),
        compiler_params=_params(("parallel",)))(chip, g)


def _sum_layers(name, lands):
    n, r, c = lands[0].shape
    tr = _flat_tile(r)
    nl = len(lands)

    def body(*refs):
        o_ref = refs[-1]
        for k in range(nl):
            @pl.when(pl.program_id(0) == k)
            def _():
                tot = refs[k][0].astype(F32)
                for i in range(1, n):
                    tot = tot + refs[k][i].astype(F32)
                o_ref[...] = tot

    return pl.pallas_call(
        body, name=name, grid=(nl, r // tr),
        in_specs=[pl.BlockSpec((n, tr, c), lambda l, i, k=k: (0, jnp.where(l == k, i, 0), 0)) for k in range(nl)],
        out_specs=pl.BlockSpec((tr, c), lambda l, i: (l * (r // tr) + i, 0)), out_shape=_sds((nl * r, c), F32),
        compiler_params=_params(("arbitrary", "arbitrary")))(*lands)


def _sum_slots(name, a, scale=None):
    n, r, c = a.shape
    tr = _flat_tile(r)

    def body(a_ref, o_ref):
        tot = a_ref[0].astype(F32)
        for i in range(1, n):
            tot = tot + a_ref[i].astype(F32)
        o_ref[...] = tot

    return pl.pallas_call(body, name=name, grid=(r // tr,),
                          in_specs=[pl.BlockSpec((n, tr, c), lambda i: (0, i, 0))],
                          out_specs=pl.BlockSpec((tr, c), lambda i: (i, 0)), out_shape=_sds((r, c), F32),
                          compiler_params=_params(("parallel",)))(a)


def _adamw(name, w, ga, gb, m, v):
    r, c = w.shape
    tr = _flat_tile(r)
    c1 = 1.0 - B1 ** STEP
    c2 = 1.0 - B2 ** STEP
    two = gb is not None

    def body(*refs):
        if two:
            w_ref, ga_ref, gb_ref, m_ref, v_ref, g_out, d_out, m_out, v_out = refs
            g = ga_ref[...] + gb_ref[...]
        else:
            w_ref, ga_ref, m_ref, v_ref, g_out, d_out, m_out, v_out = refs
            g = ga_ref[...]
        mn = B1 * m_ref[...] + (1.0 - B1) * g
        vn = B2 * v_ref[...] + (1.0 - B2) * (g * g)
        g_out[...] = g
        m_out[...] = mn
        v_out[...] = vn
        d_out[...] = -LR * ((mn / c1) / (jnp.sqrt(vn / c2) + ADAM_EPS) + WD * w_ref[...])

    spec = pl.BlockSpec((tr, c), lambda i: (i, 0))
    ins = [w, ga, gb, m, v] if two else [w, ga, m, v]
    return pl.pallas_call(body, name=name, grid=(r // tr,), in_specs=[spec] * len(ins), out_specs=[spec] * 4,
                          out_shape=[_sds((r, c), F32)] * 4, compiler_params=_params(("parallel",)))(*ins)


WEIGHTS = ('mix_norm_w', 'w_in', 'q_norm_w', 'k_norm_w', 'hgrn_lb_logits', 'hgrn_gnorm_w', 'conv_dw_w', 'conv_dw_b',
           'conv_ln_w', 'conv_ln_b', 'conv_pw_w', 'conv_pw_b', 'attn_out_norm_w', 'conv_out_norm_w', 'w_out',
           'ffn_norm_w', 'w_gate', 'w_up', 'w_down')
SHARDED_SMALL = {"hgrn_lb_logits": 2, "conv_dw_w": 2, "conv_pw_w": 1}
LANES = 128
PACK_ROWS = 256


def _pack(parts):
    flat = jnp.concatenate([p.reshape(-1) for p in parts])
    n = flat.shape[0]
    rows = -(-n // (PACK_ROWS * LANES)) * PACK_ROWS
    return jnp.pad(flat, (0, rows * LANES - n)).reshape(rows, LANES)


def _unpack(packed, shapes):
    flat = packed.reshape(-1)
    out, off = [], 0
    for shp in shapes:
        n = int(np.prod(shp))
        out.append(flat[off:off + n].reshape(shp))
        off += n
    return out


def kernel(x, mix_norm_w, w_in, q_norm_w, k_norm_w, hgrn_lb_logits, hgrn_gnorm_w, conv_dw_w, conv_dw_b, conv_ln_w, conv_ln_b, conv_pw_w, conv_pw_b, attn_out_norm_w, conv_out_norm_w, w_out, ffn_norm_w, w_gate, w_up, w_down, loss_target, m_mix_norm_w, m_w_in, m_q_norm_w, m_k_norm_w, m_hgrn_lb_logits, m_hgrn_gnorm_w, m_conv_dw_w, m_conv_dw_b, m_conv_ln_w, m_conv_ln_b, m_conv_pw_w, m_conv_pw_b, m_attn_out_norm_w, m_conv_out_norm_w, m_w_out, m_ffn_norm_w, m_w_gate, m_w_up, m_w_down, v_mix_norm_w, v_w_in, v_q_norm_w, v_k_norm_w, v_hgrn_lb_logits, v_hgrn_gnorm_w, v_conv_dw_w, v_conv_dw_b, v_conv_ln_w, v_conv_ln_b, v_conv_pw_w, v_conv_pw_b, v_attn_out_norm_w, v_conv_out_norm_w, v_w_out, v_ffn_norm_w, v_w_gate, v_w_up, v_w_down):
    w = dict(mix_norm_w=mix_norm_w, w_in=w_in, q_norm_w=q_norm_w, k_norm_w=k_norm_w, hgrn_lb_logits=hgrn_lb_logits,
             hgrn_gnorm_w=hgrn_gnorm_w, conv_dw_w=conv_dw_w, conv_dw_b=conv_dw_b, conv_ln_w=conv_ln_w,
             conv_ln_b=conv_ln_b, conv_pw_w=conv_pw_w, conv_pw_b=conv_pw_b, attn_out_norm_w=attn_out_norm_w,
             conv_out_norm_w=conv_out_norm_w, w_out=w_out, ffn_norm_w=ffn_norm_w, w_gate=w_gate, w_up=w_up,
             w_down=w_down)
    m = dict(mix_norm_w=m_mix_norm_w, w_in=m_w_in, q_norm_w=m_q_norm_w, k_norm_w=m_k_norm_w,
             hgrn_lb_logits=m_hgrn_lb_logits, hgrn_gnorm_w=m_hgrn_gnorm_w, conv_dw_w=m_conv_dw_w,
             conv_dw_b=m_conv_dw_b, conv_ln_w=m_conv_ln_w, conv_ln_b=m_conv_ln_b, conv_pw_w=m_conv_pw_w,
             conv_pw_b=m_conv_pw_b, attn_out_norm_w=m_attn_out_norm_w, conv_out_norm_w=m_conv_out_norm_w,
             w_out=m_w_out, ffn_norm_w=m_ffn_norm_w, w_gate=m_w_gate, w_up=m_w_up, w_down=m_w_down)
    v = dict(mix_norm_w=v_mix_norm_w, w_in=v_w_in, q_norm_w=v_q_norm_w, k_norm_w=v_k_norm_w,
             hgrn_lb_logits=v_hgrn_lb_logits, hgrn_gnorm_w=v_hgrn_gnorm_w, conv_dw_w=v_conv_dw_w,
             conv_dw_b=v_conv_dw_b, conv_ln_w=v_conv_ln_w, conv_ln_b=v_conv_ln_b, conv_pw_w=v_conv_pw_w,
             conv_pw_b=v_conv_pw_b, attn_out_norm_w=v_attn_out_norm_w, conv_out_norm_w=v_conv_out_norm_w,
             w_out=v_w_out, ffn_norm_w=v_ffn_norm_w, w_gate=v_w_gate, w_up=v_w_up, w_down=v_w_down)
    chip = 2 * lax.axis_index("x") + lax.axis_index("y")

    chip1 = chip.reshape(1).astype(jnp.int32)

    flat2 = lambda a: a.reshape(-1, a.shape[-1])
    small_pack = _pack([w[n] for n in SHARDED_SMALL])
    gathered = _exchange("gather_small_weights", [small_pack], "gather4")
    groups = [[(0, "w_in")], [(0, n) for n in BIG_ORDER[1:]], [(1, n) for n in BIG_ORDER]]
    group_of = {key: g for g, keys in enumerate(groups) for key in keys}
    starts = []
    for g, keys in enumerate(groups):
        slots = [_cast_slot("cast_%s_l%d" % (n, l), flat2(w[n]), l, chip1) for l, n in keys]
        starts.append(_split_start("gather_start_g%d" % g, [], slots, False))
    got = {}

    def wget(l, name, after):
        if (l, name) not in got:
            g = group_of[(l, name)]
            for key, arr in zip(groups[g], _split_wait("gather_wait_g%d" % g, starts[g], after)):
                got[key] = arr
        return got[(l, name)]

    pending = []

    def on_grads(l, grads):
        names = [n for n in BIG_ORDER if n in grads]
        own = [_own_slot("own_%s_l%d" % (n, l), grads[n], chip1) for n in names]
        st = _split_start("scatter_start_l%d_%s" % (l, names[0]), [grads[n] for n in names], own, True)
        pending.append((l, names, st))
        return [st["token"]]

    parts = [_unpack(gathered[-1][j], [w[n].shape for n in SHARDED_SMALL]) for j in range(N_CHIP)]
    full_small = {n: jnp.concatenate([parts[j][i] for j in range(N_CHIP)], axis=ax)
                  for i, (n, ax) in enumerate(SHARDED_SMALL.items())}
    sm = {n: w[n] for n in WEIGHTS if n not in BIG_ORDER and n not in SHARDED_SMALL}
    sm["conv_dw_w"] = full_small["conv_dw_w"]
    sm["conv_pw_w"] = full_small["conv_pw_w"]
    logits = full_small["hgrn_lb_logits"].reshape(DEPTH * 2, D_HGRN)
    sm["lb"] = _lower_bounds(logits).reshape(DEPTH, 2, D_HGRN)

    sq_sum, grad_x, smalls = _local_step(x, loss_target, wget, sm, [st["token"] for st in starts], on_grads)
    loss = lax.psum(0.5 * sq_sum / D_MODEL, ("x", "y", "c"))

    landed = {}
    for l, names, st in pending:
        for n, arr in zip(names, _split_wait("scatter_wait_l%d_%s" % (l, names[0]), st, grad_x)):
            landed[(l, n)] = arr
    sums = [_sum_layers("sum_" + n, [landed[(l, n)] for l in range(DEPTH)]) for n in BIG_ORDER]
    sib = _exchange("sibling_grads", sums, "sibling")
    out = {}
    for n, ga, gb in zip(BIG_ORDER, sums, sib):
        res = _adamw("adamw_" + n, flat2(w[n]), ga, gb, flat2(m[n]), flat2(v[n]))
        out[n] = [r.reshape(w[n].shape) for r in res]

    small_names = [n for n in WEIGHTS if n not in BIG_ORDER]
    g_pack = _pack([jnp.stack([smalls[l][n] for l in range(DEPTH)]) for n in SMALL_ORDER])
    g_all = _exchange("gather_small_grads", [g_pack], "gather8")[0]
    g_tot = _sum_slots("sum_small", g_all)
    shapes = [(DEPTH,) + tuple(smalls[0][n].shape) for n in SMALL_ORDER]
    g_small = dict(zip(SMALL_ORDER, _unpack(g_tot, shapes)))
    lb_shard = lax.dynamic_slice_in_dim(g_small.pop("lb").reshape(DEPTH * 2, D_HGRN), chip * HEAD_DIM, HEAD_DIM, 1)
    g_small["hgrn_lb_logits"] = _lower_bounds_bwd(hgrn_lb_logits.reshape(DEPTH * 2, HEAD_DIM), lb_shard).reshape(
        hgrn_lb_logits.shape)
    g_small["conv_dw_w"] = lax.dynamic_slice_in_dim(g_small["conv_dw_w"], chip * HEAD_DIM, HEAD_DIM, 2)
    g_small["conv_pw_w"] = lax.dynamic_slice_in_dim(g_small["conv_pw_w"], chip * HEAD_DIM, HEAD_DIM, 1)
    res = _adamw("adamw_small", _pack([w[n] for n in small_names]), _pack([g_small[n] for n in small_names]), None,
                 _pack([m[n] for n in small_names]), _pack([v[n] for n in small_names]))
    unpacked = [_unpack(r, [w[n].shape for n in small_names]) for r in res]
    for i, n in enumerate(small_names):
        out[n] = [unpacked[k][i] for k in range(4)]

    return (loss, grad_x, *[out[n][0] for n in WEIGHTS], *[out[n][1] for n in WEIGHTS],
            *[out[n][2] for n in WEIGHTS], *[out[n][3] for n in WEIGHTS])
```

```python
import functools

import numpy as np
import jax
import jax.numpy as jnp
from jax import lax
from jax.experimental import pallas as pl
from jax.experimental.pallas import tpu as pltpu

F32, BF16 = jnp.float32, jnp.bfloat16

D_MODEL = 1024
DEPTH = 2
GRID_W = 64
D_ATTN, D_HGRN, D_CONV = 512, 256, 256
HEAD_DIM = 64
N_KV = 2
KV_LANES = D_ATTN // N_KV
ROPE_THETA = 10000.0
F_MIN = 1e-6
CONV_W = 31
CONV_PAD = 15
D_FF = 2816
D_IN = 2560
N_CHIP = 4
N_DEV = 8
IN_BLK = D_IN // N_CHIP
FF_BLK = D_FF // N_CHIP
OUT_BLK = D_MODEL // N_CHIP
EPS = 1e-6
LN_EPS = 1e-5
LR, B1, B2, ADAM_EPS, WD, STEP = 0.001, 0.9, 0.999, 1e-08, 0.01, 10
CHUNK = 16
HBLK = 256
CONV_TILE = 128
VMEM_LIMIT = 56 * 1024 * 1024

COL_Q, COL_K, COL_V = 0, 4, 5
COL_HQ, COL_FF, COL_FB, COL_HI, COL_HG, COL_CA, COL_CB = 3, 4, 5, 6, 7, 8, 9


def _params(sem=None):
    return pltpu.CompilerParams(dimension_semantics=sem, vmem_limit_bytes=VMEM_LIMIT)


def _sds(shape, dtype):
    return jax.ShapeDtypeStruct(tuple(shape), dtype)


def _full(shape):
    n = len(shape)
    return pl.BlockSpec(tuple(shape), lambda *_: (0,) * n)


def _sigmoid(x):
    return 1.0 / (1.0 + jnp.exp(-x))


def _silu(x):
    return x * _sigmoid(x)


def _dsilu(x):
    s = _sigmoid(x)
    return s * (1.0 + x * (1.0 - s))


def _rowgroups(v):
    m, c = v.shape
    return v.reshape(m // 8, 8, c).sum(axis=0)


def _split2(x):
    hi = x.astype(BF16)
    lo = (x - hi.astype(F32)).astype(BF16)
    return hi, lo


def _rdot2(x, m):
    hi, lo = _split2(x)
    return (jnp.dot(hi, m, preferred_element_type=F32) + jnp.dot(lo, m, preferred_element_type=F32))


def _ldot3(m, x):
    hi = x.astype(BF16)
    r1 = x - hi.astype(F32)
    mid = r1.astype(BF16)
    lo = (r1 - mid.astype(F32)).astype(BF16)
    return (jnp.dot(m, hi, preferred_element_type=F32) + jnp.dot(m, mid, preferred_element_type=F32)
            + jnp.dot(m, lo, preferred_element_type=F32))


def _dot_nt(a, b):
    return lax.dot_general(a, b, (((1,), (1,)), ((), ())), preferred_element_type=F32)


def _dot_tn(a, b):
    return lax.dot_general(a, b, (((0,), (0,)), ((), ())), preferred_element_type=F32)


def _seg_matrix(n, seg, val):
    i = np.arange(n)
    return ((i[:, None] // seg) == (i[None, :] // seg)).astype(np.float32) * val


def _rot_matrix(n):
    r = np.zeros((n, n), np.float32)
    for i in range(n):
        if (i % 32) < 16:
            r[i + 16, i] = -1.0
        else:
            r[i - 16, i] = 1.0
    return r


def _rep_matrix():
    r = np.zeros((N_KV * HEAD_DIM, D_ATTN), np.float32)
    for kv in range(N_KV):
        for g in range(KV_LANES // HEAD_DIM):
            for d in range(HEAD_DIM):
                r[HEAD_DIM * kv + d, KV_LANES * kv + HEAD_DIM * g + d] = 1.0
    return r


def _cumsum_matrix(rev):
    i = np.arange(HBLK)
    same = (i[:, None] // CHUNK) == (i[None, :] // CHUNK)
    tri = (i[None, :] >= i[:, None]) if rev else (i[None, :] <= i[:, None])
    return (same & tri).astype(np.float32)


def _sel_matrices():
    sel = np.zeros((CHUNK, CHUNK * CHUNK), np.float32)
    selt = np.zeros((CHUNK, CHUNK * CHUNK), np.float32)
    for t in range(CHUNK):
        for s in range(CHUNK):
            sel[t, t * CHUNK + s] = 1.0
            selt[s, t * CHUNK + s] = 1.0
    return sel, selt


def _bf(a):
    return jnp.asarray(a, dtype=BF16)


def _mm(name, grid, pairs, extras, outs, epilogue, acc=None, sem=None):
    n_p, n_e, n_o = len(pairs), len(extras), len(outs)

    def body(*refs):
        ab = refs[:2 * n_p]
        ex = refs[2 * n_p:2 * n_p + n_e]
        out = refs[2 * n_p + n_e:2 * n_p + n_e + n_o]
        scr = refs[2 * n_p + n_e + n_o:]
        tot = None
        for i in range(n_p):
            a = ab[2 * i][...]
            b = ab[2 * i + 1][...]
            if a.ndim == 3:
                a = a.reshape(-1, a.shape[-1])
            if b.ndim == 3:
                b = b.reshape(-1, b.shape[-1])
            r = lax.dot_general(a.astype(BF16), b.astype(BF16), pairs[i][4], preferred_element_type=F32)
            tot = r if tot is None else tot + r

        def finish(total):
            res = epilogue(total, *[e[...] for e in ex])
            for o_ref, val in zip(out, res):
                o_ref[...] = val.astype(o_ref.dtype)

        if acc is None:
            finish(tot)
        else:
            k = pl.program_id(acc[0])

            @pl.when(k == 0)
            def _():
                scr[0][...] = tot

            @pl.when(k > 0)
            def _():
                scr[0][...] += tot

            @pl.when(k == grid[acc[0]] - 1)
            def _():
                finish(scr[0][...])

    args, in_specs = [], []
    for a, a_spec, b, b_spec, _ in pairs:
        args += [a, b]
        in_specs += [a_spec, b_spec]
    for e, e_spec in extras:
        args.append(e)
        in_specs.append(e_spec)
    if sem is None:
        sem = tuple("arbitrary" if (acc is not None and i == acc[0]) else "parallel" for i in range(len(grid)))
    return pl.pallas_call(
        body, name=name, grid=grid, in_specs=in_specs,
        out_specs=[o[1] for o in outs], out_shape=[o[0] for o in outs],
        scratch_shapes=[] if acc is None else [pltpu.VMEM(acc[1], F32)],
        compiler_params=_params(sem),
    )(*args)


NN = (((1,), (0,)), ((), ()))
NT = (((1,), (1,)), ((), ()))
TN = (((0,), (0,)), ((), ()))


def _row_tile(t):
    return min(256, t)


def _rms_fwd(name, x, w, deps=()):
    t, d = x.shape
    tm = _row_tile(t)

    def body(x_ref, w_ref, *rest):
        o_ref = rest[-1]
        xv = x_ref[...]
        r = lax.rsqrt(jnp.mean(xv * xv, axis=-1, keepdims=True) + EPS)
        o_ref[...] = (xv * r * w_ref[...]).astype(BF16)

    return pl.pallas_call(
        body, name=name, grid=(t // tm,),
        in_specs=[pl.BlockSpec((tm, d), lambda i: (i, 0)), _full((1, d))] + [_full(a.shape) for a in deps],
        out_specs=pl.BlockSpec((tm, d), lambda i: (i, 0)), out_shape=_sds((t, d), BF16),
        compiler_params=_params(("parallel",)),
    )(x, w, *deps)


def _rms_bwd(name, x, w, dh, dres, deps=()):
    t, d = x.shape
    tm = _row_tile(t)

    def body(x_ref, w_ref, dh_ref, dres_ref, *rest):
        dx_ref, dw_ref = rest[-2:]
        xv = x_ref[...]
        r = lax.rsqrt(jnp.mean(xv * xv, axis=-1, keepdims=True) + EPS)
        dy = dh_ref[...]
        gw = dy * w_ref[...]
        dx_ref[...] = dres_ref[...] + r * gw - xv * (r * r * r) * jnp.mean(gw * xv, axis=-1, keepdims=True)

        @pl.when(pl.program_id(0) == 0)
        def _():
            dw_ref[...] = jnp.zeros_like(dw_ref)

        dw_ref[...] += _rowgroups(dy * xv * r)

    tile = pl.BlockSpec((tm, d), lambda i: (i, 0))
    return pl.pallas_call(
        body, name=name, grid=(t // tm,),
        in_specs=[tile, _full((1, d)), tile, tile] + [_full(a.shape) for a in deps],
        out_specs=[tile, _full((8, d))], out_shape=[_sds((t, d), F32), _sds((8, d), F32)],
        compiler_params=_params(("arbitrary",)),
    )(x, w, dh, dres, *deps)


def _loss_kernel(y, target):
    t, d = y.shape
    tm = _row_tile(t)

    def body(y_ref, t_ref, dy_ref, acc_ref):
        e = y_ref[...] - t_ref[...]
        dy_ref[...] = e * (1.0 / d)

        @pl.when(pl.program_id(0) == 0)
        def _():
            acc_ref[...] = jnp.zeros_like(acc_ref)

        acc_ref[...] += _rowgroups(e * e)

    tile = pl.BlockSpec((tm, d), lambda i: (i, 0))
    return pl.pallas_call(
        body, name="loss_head", grid=(t // tm,), in_specs=[tile, tile],
        out_specs=[tile, _full((8, d))], out_shape=[_sds((t, d), F32), _sds((8, d), F32)],
        compiler_params=_params(("arbitrary",)),
    )(y, target)


def _rope_tables(s):
    rows = s // GRID_W
    row_id = jnp.repeat(jnp.arange(rows, dtype=F32), GRID_W)
    col_id = jnp.tile(jnp.arange(GRID_W, dtype=F32), rows)
    half = HEAD_DIM // 2
    inv_freq = ROPE_THETA ** (-jnp.arange(0, half, 2, dtype=F32) / half)
    ang_r = row_id[:, None] * inv_freq[None, :]
    ang_c = col_id[:, None] * inv_freq[None, :]
    ang = jnp.concatenate([ang_r, ang_r, ang_c, ang_c], axis=-1)
    return jnp.cos(ang).astype(F32), jnp.sin(ang).astype(F32)


def _attn_consts():
    return dict(
        seg_q=_bf(_seg_matrix(D_ATTN, HEAD_DIM, 1.0 / HEAD_DIM)),
        seg_k=_bf(_seg_matrix(N_KV * HEAD_DIM, HEAD_DIM, 1.0 / HEAD_DIM)),
        rot_q=_bf(_rot_matrix(D_ATTN)), rot_k=_bf(_rot_matrix(N_KV * HEAD_DIM)),
        rep=_bf(_rep_matrix()), rep_t=_bf(_rep_matrix().T))


def _attn_prep(name, proj, s, tabs, qw, kw, ac):
    t = proj.shape[0]
    tm = _row_tile(s)
    nst = s // tm
    kw_ = N_KV * HEAD_DIM

    def body(q_ref, k_ref, v_ref, cq_ref, sq_ref, ck_ref, sk_ref, qw_ref, kw_ref,
             segq_ref, segk_ref, rotq_ref, rotk_ref, rep_ref, qn_ref, kr_ref, vr_ref):
        q = q_ref[...]
        r = lax.rsqrt(jnp.dot((q * q).astype(BF16), segq_ref[...], preferred_element_type=F32) + EPS)
        qn = q * r * qw_ref[...]
        qr = qn * cq_ref[...] + _rdot2(qn, rotq_ref[...]) * sq_ref[...]
        qn_ref[...] = (qr * (HEAD_DIM ** -0.5)).astype(BF16)
        k = k_ref[...]
        rk = lax.rsqrt(jnp.dot((k * k).astype(BF16), segk_ref[...], preferred_element_type=F32) + EPS)
        kn = k * rk * kw_ref[...]
        kr = kn * ck_ref[...] + _rdot2(kn, rotk_ref[...]) * sk_ref[...]
        kr_ref[...] = jnp.dot(kr.astype(BF16), rep_ref[...], preferred_element_type=F32).astype(BF16)
        vr_ref[...] = jnp.dot(v_ref[...].astype(BF16), rep_ref[...], preferred_element_type=F32).astype(BF16)

    wide = pl.BlockSpec((tm, D_ATTN), lambda i: (i, 0))
    tabq = pl.BlockSpec((tm, D_ATTN), lambda i: (i % nst, 0))
    tabk = pl.BlockSpec((tm, kw_), lambda i: (i % nst, 0))
    return pl.pallas_call(
        body, name=name, grid=(t // tm,),
        in_specs=[pl.BlockSpec((tm, D_ATTN), lambda i: (i, COL_Q)), pl.BlockSpec((tm, kw_), lambda i: (i, COL_K)),
                  pl.BlockSpec((tm, kw_), lambda i: (i, COL_V)), tabq, tabq, tabk, tabk,
                  _full((1, D_ATTN)), _full((1, kw_)), _full((D_ATTN, D_ATTN)), _full((kw_, kw_)),
                  _full((D_ATTN, D_ATTN)), _full((kw_, kw_)), _full((kw_, D_ATTN))],
        out_specs=[wide, wide, wide], out_shape=[_sds((t, D_ATTN), BF16)] * 3,
        compiler_params=_params(("parallel",)),
    )(proj, proj, proj, tabs["cq"], tabs["sq"], tabs["ck"], tabs["sk"], qw, kw,
      ac["seg_q"], ac["seg_k"], ac["rot_q"], ac["rot_k"], ac["rep"])


def _attn_prep_bwd(name, proj, s, tabs, qw, kw, ac, dqs, dkr, dvr):
    t = proj.shape[0]
    tm = _row_tile(s)
    nst = s // tm
    kw_ = N_KV * HEAD_DIM
    wout = D_ATTN + 2 * kw_

    def norm_rope_bwd(x, w, cos, sin, seg, rot, d_roped):
        dn = d_roped * cos - _rdot2(d_roped * sin, rot)
        r = lax.rsqrt(jnp.dot((x * x).astype(BF16), seg, preferred_element_type=F32) + EPS)
        gw = dn * w
        dx = r * gw - x * (r * r * r) * _rdot2(gw * x, seg)
        return dx, _rowgroups(dn * x * r)

    def body(q_ref, k_ref, cq_ref, sq_ref, ck_ref, sk_ref, qw_ref, kw_ref, segq_ref, segk_ref, rotq_ref, rotk_ref,
             rept_ref, dqs_ref, dkr_ref, dvr_ref, dp_ref, dqw_ref, dkw_ref):
        dq, dqw = norm_rope_bwd(q_ref[...], qw_ref[...], cq_ref[...], sq_ref[...], segq_ref[...], rotq_ref[...],
                                dqs_ref[...] * (HEAD_DIM ** -0.5))
        dk_roped = _rdot2(dkr_ref[...], rept_ref[...])
        dk, dkw = norm_rope_bwd(k_ref[...], kw_ref[...], ck_ref[...], sk_ref[...], segk_ref[...], rotk_ref[...],
                                dk_roped)
        dv = _rdot2(dvr_ref[...], rept_ref[...])
        dp_ref[:, 0:D_ATTN] = dq.astype(BF16)
        dp_ref[:, D_ATTN:D_ATTN + kw_] = dk.astype(BF16)
        dp_ref[:, D_ATTN + kw_:wout] = dv.astype(BF16)

        @pl.when(pl.program_id(0) == 0)
        def _():
            dqw_ref[...] = jnp.zeros_like(dqw_ref)
            dkw_ref[...] = jnp.zeros_like(dkw_ref)

        dqw_ref[...] += dqw
        dkw_ref[...] += dkw

    wide = pl.BlockSpec((tm, D_ATTN), lambda i: (i, 0))
    tabq = pl.BlockSpec((tm, D_ATTN), lambda i: (i % nst, 0))
    tabk = pl.BlockSpec((tm, kw_), lambda i: (i % nst, 0))
    return pl.pallas_call(
        body, name=name, grid=(t // tm,),
        in_specs=[pl.BlockSpec((tm, D_ATTN), lambda i: (i, COL_Q)), pl.BlockSpec((tm, kw_), lambda i: (i, COL_K)),
                  tabq, tabq, tabk, tabk, _full((1, D_ATTN)), _full((1, kw_)),
                  _full((D_ATTN, D_ATTN)), _full((kw_, kw_)), _full((D_ATTN, D_ATTN)), _full((kw_, kw_)),
                  _full((D_ATTN, kw_)), wide, wide, wide],
        out_specs=[pl.BlockSpec((tm, wout), lambda i: (i, 0)), _full((8, D_ATTN)), _full((8, kw_))],
        out_shape=[_sds((t, wout), BF16), _sds((8, D_ATTN), F32), _sds((8, kw_), F32)],
        compiler_params=_params(("arbitrary",)),
    )(proj, proj, tabs["cq"], tabs["sq"], tabs["ck"], tabs["sk"], qw, kw,
      ac["seg_q"], ac["seg_k"], ac["rot_q"], ac["rot_k"], ac["rep_t"], dqs, dkr, dvr)


def _attn_tile(s):
    return min(256, s)


def _head_masks(shape):
    lane = lax.broadcasted_iota(jnp.int32, shape, 1)
    return [(lane // HEAD_DIM) == g for g in range(KV_LANES // HEAD_DIM)]


def _attn_fwd(name, qn, kr, vr, b, s):
    t = qn.shape[0]
    tq = _attn_tile(s)
    nq = s // tq

    def body(q_ref, k_ref, v_ref, o_ref):
        q = q_ref[...]
        k = k_ref[...]
        v = v_ref[...]
        acc = jnp.zeros((tq, KV_LANES), F32)
        for mask in _head_masks((tq, KV_LANES)):
            sc = _dot_nt(jnp.where(mask, q, jnp.zeros_like(q)), k)
            p = jnp.exp(sc - jnp.max(sc, axis=-1, keepdims=True))
            inv = 1.0 / jnp.sum(p, axis=-1, keepdims=True)
            og = jnp.dot(p.astype(BF16), v, preferred_element_type=F32) * inv
            acc = jnp.where(mask, og, acc)
        o_ref[...] = acc

    return pl.pallas_call(
        body, name=name, grid=(b, N_KV, nq),
        in_specs=[pl.BlockSpec((tq, KV_LANES), lambda bi, kv, i: (bi * nq + i, kv)),
                  pl.BlockSpec((s, KV_LANES), lambda bi, kv, i: (bi, kv)),
                  pl.BlockSpec((s, KV_LANES), lambda bi, kv, i: (bi, kv))],
        out_specs=pl.BlockSpec((tq, KV_LANES), lambda bi, kv, i: (bi * nq + i, kv)),
        out_shape=_sds((t, D_ATTN), F32),
        compiler_params=_params(("parallel", "parallel", "parallel")),
    )(qn, kr, vr)


def _attn_bwd(name, qn, kr, vr, do, b, s):
    t = qn.shape[0]
    tq = _attn_tile(s)
    nq = s // tq

    def body(q_ref, k_ref, v_ref, do_ref, dq_ref, dk_ref, dv_ref):
        @pl.when(pl.program_id(2) == 0)
        def _():
            dk_ref[...] = jnp.zeros_like(dk_ref)
            dv_ref[...] = jnp.zeros_like(dv_ref)

        q = q_ref[...]
        k = k_ref[...]
        v = v_ref[...]
        dout = do_ref[...].astype(BF16)
        masks = _head_masks((tq, KV_LANES))
        q4 = jnp.concatenate([jnp.where(m, q, jnp.zeros_like(q)) for m in masks], axis=0)
        do4 = jnp.concatenate([jnp.where(m, dout, jnp.zeros_like(dout)) for m in masks], axis=0)
        sc = _dot_nt(q4, k)
        p = jnp.exp(sc - jnp.max(sc, axis=-1, keepdims=True))
        p = p * (1.0 / jnp.sum(p, axis=-1, keepdims=True))
        dp = _dot_nt(do4, v)
        ds = (p * (dp - jnp.sum(p * dp, axis=-1, keepdims=True))).astype(BF16)
        dq4 = jnp.dot(ds, k, preferred_element_type=F32)
        dq = jnp.zeros((tq, KV_LANES), F32)
        for g, m in enumerate(masks):
            dq = jnp.where(m, dq4[g * tq:(g + 1) * tq, :], dq)
        dq_ref[...] = dq
        dk_ref[...] += _dot_tn(ds, q4)
        dv_ref[...] += _dot_tn(p.astype(BF16), do4)

    qspec = pl.BlockSpec((tq, KV_LANES), lambda bi, kv, i: (bi * nq + i, kv))
    kspec = pl.BlockSpec((s, KV_LANES), lambda bi, kv, i: (bi, kv))
    return pl.pallas_call(
        body, name=name, grid=(b, N_KV, nq),
        in_specs=[qspec, kspec, kspec, qspec],
        out_specs=[qspec, kspec, kspec], out_shape=[_sds((t, D_ATTN), F32)] * 3,
        compiler_params=_params(("parallel", "parallel", "arbitrary")),
    )(qn, kr, vr, do)


def _hgrn_consts(rev):
    sel, selt = _sel_matrices()
    cs = _cumsum_matrix(rev)
    return dict(cs=_bf(cs), cs_t=_bf(cs.T), seg=_bf(_seg_matrix(D_HGRN, HEAD_DIM, 1.0)),
                bd=jnp.asarray(_seg_matrix(D_HGRN, HEAD_DIM, 1.0), F32),
                sel=_bf(sel), selt=_bf(selt), seld=_bf(sel - selt))


def _gates(z, lb):
    sig = _sigmoid(z)
    f = lb + (1.0 - lb) * sig
    g = jnp.log(jnp.maximum(f, F_MIN))
    sn = _sigmoid(-z)
    return sig, f, g, sn, (1.0 - lb) * sn


def _pair_decay(b, rev):
    row = lax.broadcasted_iota(jnp.int32, (CHUNK, D_HGRN), 0)
    parts = []
    for t in range(CHUNK):
        m = (row >= t) if rev else (row <= t)
        parts.append(jnp.where(m, jnp.exp(jnp.minimum(b[t:t + 1, :] - b, 0.0)), 0.0))
    return jnp.concatenate(parts, axis=0)


def _rows_rep(a):
    return jnp.concatenate([jnp.broadcast_to(a[t:t + 1, :], a.shape) for t in range(CHUNK)], axis=0)


def _tile_rows(a):
    return jnp.concatenate([a] * CHUNK, axis=0)


def _hgrn_specs(b, s, rev):
    nb = s // HBLK

    def blk(j):
        return (nb - 1 - j) if rev else j

    def col(c):
        return pl.BlockSpec((HBLK, D_HGRN), lambda bi, j: (bi * nb + blk(j), c))

    return nb, blk, col


def _hgrn_fwd(name, proj, lb, b, s, rev, hc):
    t = proj.shape[0]
    nb, blk, col = _hgrn_specs(b, s, rev)
    n_ch = HBLK // CHUNK
    last = 0 if rev else CHUNK - 1

    def body(q_ref, z_ref, v_ref, lb_ref, cs_ref, seg_ref, bd_ref, sel_ref, o_ref, st_ref, state, b_scr, k_scr):
        @pl.when(pl.program_id(1) == 0)
        def _():
            state[...] = jnp.zeros_like(state)

        st_ref[...] = state[...]
        _, _, g, _, kk = _gates(z_ref[...], lb_ref[...])
        k_scr[...] = kk
        b_scr[...] = _ldot3(cs_ref[...], g)

        def chunk(i, carry):
            c = (n_ch - 1 - i) if rev else i
            rows = pl.ds(pl.multiple_of(c * CHUNK, CHUNK), CHUNK)
            q = q_ref[rows, :]
            k = k_scr[rows, :]
            v = v_ref[rows, :]
            bb = b_scr[rows, :]
            bl = bb[last:last + 1, :]
            pairs = _pair_decay(bb, rev) * _rows_rep(q) * _tile_rows(k)
            a = jnp.dot(pairs.astype(BF16), seg_ref[...], preferred_element_type=F32)
            o_intra = jnp.dot(sel_ref[...], (a * _tile_rows(v)).astype(BF16), preferred_element_type=F32)
            st = state[...]
            o_inter = _dot_nt((q * jnp.exp(bb)).astype(BF16), st.astype(BF16))
            o_ref[rows, :] = o_intra + o_inter
            ke = k * jnp.exp(bl - bb)
            state[...] = st * jnp.exp(bl) + bd_ref[...] * _dot_tn(v.astype(BF16), ke.astype(BF16))
            return carry

        lax.fori_loop(0, n_ch, chunk, 0)

    sq = (D_HGRN, D_HGRN)
    return pl.pallas_call(
        body, name=name, grid=(b, nb),
        in_specs=[col(COL_HQ), col(COL_FB if rev else COL_FF), col(COL_HI), _full((1, D_HGRN)),
                  _full((HBLK, HBLK)), _full(sq), _full(sq), _full((CHUNK, CHUNK * CHUNK))],
        out_specs=[pl.BlockSpec((HBLK, D_HGRN), lambda bi, j: (bi * nb + blk(j), 0)),
                   pl.BlockSpec((None,) + sq, lambda bi, j: (bi * nb + blk(j), 0, 0))],
        out_shape=[_sds((t, D_HGRN), F32), _sds((b * nb,) + sq, F32)],
        scratch_shapes=[pltpu.VMEM(sq, F32), pltpu.VMEM((HBLK, D_HGRN), F32), pltpu.VMEM((HBLK, D_HGRN), F32)],
        compiler_params=_params(("parallel", "arbitrary")),
    )(proj, proj, proj, lb, hc["cs"], hc["seg"], hc["bd"], hc["sel"])


def _hgrn_bwd(name, proj, lb, st_blk, do, dq_prev, dv_prev, b, s, rev, hc):
    t = proj.shape[0]
    nb = s // HBLK
    n_ch = HBLK // CHUNK
    last = 0 if rev else CHUNK - 1

    def blk(j):
        return j if rev else (nb - 1 - j)

    def col(c):
        return pl.BlockSpec((HBLK, D_HGRN), lambda bi, j: (bi * nb + blk(j), c))

    def body(q_ref, z_ref, v_ref, lb_ref, st_ref, do_ref, dqp_ref, dvp_ref, cs_ref, cst_ref, seg_ref, bd_ref,
             sel_ref, selt_ref, seld_ref, dq_ref, dv_ref, dz_ref, dlb_ref,
             dstate, states, b_scr, k_scr, db_scr, dk_scr):
        first = jnp.logical_and(pl.program_id(0) == 0, pl.program_id(1) == 0)

        @pl.when(first)
        def _():
            dlb_ref[...] = jnp.zeros_like(dlb_ref)

        @pl.when(pl.program_id(1) == 0)
        def _():
            dstate[...] = jnp.zeros_like(dstate)

        lbv = lb_ref[...]
        z = z_ref[...]
        sig, f, g, sn, kk = _gates(z, lbv)
        k_scr[...] = kk
        b_scr[...] = _ldot3(cs_ref[...], g)

        def rows_of(c):
            return pl.ds(pl.multiple_of(c * CHUNK, CHUNK), CHUNK)

        def replay(i, st):
            c = (n_ch - 1 - i) if rev else i
            rows = rows_of(c)
            states[c] = st
            bb = b_scr[rows, :]
            bl = bb[last:last + 1, :]
            ke = k_scr[rows, :] * jnp.exp(bl - bb)
            return st * jnp.exp(bl) + bd_ref[...] * _dot_tn(v_ref[rows, :].astype(BF16), ke.astype(BF16))

        lax.fori_loop(0, n_ch, replay, st_ref[...])
        row = lax.broadcasted_iota(jnp.int32, (CHUNK, D_HGRN), 0)

        def chunk(i, carry):
            c = i if rev else (n_ch - 1 - i)
            rows = rows_of(c)
            q = q_ref[rows, :]
            k = k_scr[rows, :]
            v = v_ref[rows, :]
            bb = b_scr[rows, :]
            dout = do_ref[rows, :]
            bl = bb[last:last + 1, :]
            st_p = states[c]
            dst_n = dstate[...]
            eb = jnp.exp(bb)
            ebl = jnp.exp(bl - bb)
            ebl_last = jnp.exp(bl)
            qe = q * eb
            ke = k * ebl
            dob = dout.astype(BF16)
            dstb = dst_n.astype(BF16)
            dqe = jnp.dot(dob, st_p.astype(BF16), preferred_element_type=F32)
            dke = jnp.dot(v.astype(BF16), dstb, preferred_element_type=F32)
            dv = _dot_nt(ke.astype(BF16), dstb)
            dbl = jnp.sum(dst_n * st_p, axis=0, keepdims=True) * ebl_last + jnp.sum(dke * ke, axis=0, keepdims=True)
            dq = dqe * eb
            dk = dke * ebl
            db = dqe * qe - dke * ke
            dec = _pair_decay(bb, rev)
            q_rep = _rows_rep(q)
            k_til = _tile_rows(k)
            do_rep = _rows_rep(dout)
            pairs = dec * q_rep * k_til
            a = jnp.dot(pairs.astype(BF16), seg_ref[...], preferred_element_type=F32)
            wb = jnp.dot((_tile_rows(v) * do_rep).astype(BF16), seg_ref[...], preferred_element_type=F32)
            gdec = wb * dec
            dq = dq + jnp.dot(sel_ref[...], (gdec * k_til).astype(BF16), preferred_element_type=F32)
            dk = dk + jnp.dot(selt_ref[...], (gdec * q_rep).astype(BF16), preferred_element_type=F32)
            dv = dv + jnp.dot(selt_ref[...], (a * do_rep).astype(BF16), preferred_element_type=F32)
            db = db + jnp.dot(seld_ref[...], (wb * pairs).astype(BF16), preferred_element_type=F32)
            db = db + jnp.where(row == last, dbl, 0.0)
            dq_ref[rows, :] = dq + dqp_ref[rows, :]
            dv_ref[rows, :] = dv + dvp_ref[rows, :]
            dk_scr[rows, :] = dk
            db_scr[rows, :] = db
            dstate[...] = dst_n * ebl_last + bd_ref[...] * _dot_tn(dob, qe.astype(BF16))
            return carry

        lax.fori_loop(0, n_ch, chunk, 0)
        hi, lo = _split2(db_scr[...])
        dg = (jnp.dot(cst_ref[...], hi, preferred_element_type=F32)
              + jnp.dot(cst_ref[...], lo, preferred_element_type=F32))
        dgf = jnp.where(f > F_MIN, dg / f, 0.0)
        dk = dk_scr[...]
        dz_ref[...] = dgf * (1.0 - lbv) * sig * (1.0 - sig) - dk * (1.0 - lbv) * sn * (1.0 - sn)
        dlb_ref[...] += _rowgroups(dgf * (1.0 - sig) - dk * sn)

    sq = (D_HGRN, D_HGRN)
    blk0 = pl.BlockSpec((HBLK, D_HGRN), lambda bi, j: (bi * nb + blk(j), 0))
    pairs_shape = (CHUNK, CHUNK * CHUNK)
    return pl.pallas_call(
        body, name=name, grid=(b, nb),
        in_specs=[col(COL_HQ), col(COL_FB if rev else COL_FF), col(COL_HI), _full((1, D_HGRN)),
                  pl.BlockSpec((None,) + sq, lambda bi, j: (bi * nb + blk(j), 0, 0)), blk0, blk0, blk0,
                  _full((HBLK, HBLK)), _full((HBLK, HBLK)), _full(sq), _full(sq),
                  _full(pairs_shape), _full(pairs_shape), _full(pairs_shape)],
        out_specs=[blk0, blk0, blk0, _full((8, D_HGRN))],
        out_shape=[_sds((t, D_HGRN), F32)] * 3 + [_sds((8, D_HGRN), F32)],
        scratch_shapes=[pltpu.VMEM(sq, F32), pltpu.VMEM((n_ch,) + sq, F32)] + [pltpu.VMEM((HBLK, D_HGRN), F32)] * 4,
        compiler_params=_params(("arbitrary", "arbitrary")),
    )(proj, proj, proj, lb, st_blk, do, dq_prev, dv_prev,
      hc["cs"], hc["cs_t"], hc["seg"], hc["bd"], hc["sel"], hc["selt"], hc["seld"])


def _scan_chunk_fwd(c, rev, q_ref, v_ref, k_scr, b_scr, state, o_ref, seg_ref, bd_ref, sel_ref):
    last = 0 if rev else CHUNK - 1
    rows = pl.ds(pl.multiple_of(c * CHUNK, CHUNK), CHUNK)
    q = q_ref[rows, :]
    k = k_scr[rows, :]
    v = v_ref[rows, :]
    bb = b_scr[rows, :]
    bl = bb[last:last + 1, :]
    pairs = _pair_decay(bb, rev) * _rows_rep(q) * _tile_rows(k)
    a = jnp.dot(pairs.astype(BF16), seg_ref[...], preferred_element_type=F32)
    o_intra = jnp.dot(sel_ref[...], (a * _tile_rows(v)).astype(BF16), preferred_element_type=F32)
    st = state[...]
    o_inter = _dot_nt((q * jnp.exp(bb)).astype(BF16), st.astype(BF16))
    o_ref[rows, :] = o_intra + o_inter
    ke = k * jnp.exp(bl - bb)
    state[...] = st * jnp.exp(bl) + bd_ref[...] * _dot_tn(v.astype(BF16), ke.astype(BF16))


def _hgrn_fwd2(name, proj, lb_f, lb_b, b, s, hc_f, hc_b):
    t = proj.shape[0]
    nb = s // HBLK
    n_ch = HBLK // CHUNK

    def body(qf_ref, zf_ref, vf_ref, qb_ref, zb_ref, vb_ref, lbf_ref, lbb_ref, csf_ref, csb_ref, seg_ref, bd_ref,
             sel_ref, of_ref, ob_ref, stf_ref, stb_ref, state_f, state_b, bf_scr, bb_scr, kf_scr, kb_scr):
        @pl.when(pl.program_id(1) == 0)
        def _():
            state_f[...] = jnp.zeros_like(state_f)
            state_b[...] = jnp.zeros_like(state_b)

        stf_ref[...] = state_f[...]
        stb_ref[...] = state_b[...]
        for z_ref, lb_ref, cs_ref, k_scr, b_scr in ((zf_ref, lbf_ref, csf_ref, kf_scr, bf_scr),
                                                     (zb_ref, lbb_ref, csb_ref, kb_scr, bb_scr)):
            _, _, g, _, kk = _gates(z_ref[...], lb_ref[...])
            k_scr[...] = kk
            b_scr[...] = _ldot3(cs_ref[...], g)

        def chunk(i, carry):
            _scan_chunk_fwd(i, False, qf_ref, vf_ref, kf_scr, bf_scr, state_f, of_ref, seg_ref, bd_ref, sel_ref)
            _scan_chunk_fwd(n_ch - 1 - i, True, qb_ref, vb_ref, kb_scr, bb_scr, state_b, ob_ref, seg_ref, bd_ref,
                            sel_ref)
            return carry

        lax.fori_loop(0, n_ch, chunk, 0)

    def col(c, rev):
        return pl.BlockSpec((HBLK, D_HGRN), lambda bi, j: (bi * nb + ((nb - 1 - j) if rev else j), c))

    def st_spec(rev):
        return pl.BlockSpec((None, D_HGRN, D_HGRN), lambda bi, j: (bi * nb + ((nb - 1 - j) if rev else j), 0, 0))

    sq = (D_HGRN, D_HGRN)
    blk = (HBLK, D_HGRN)
    return pl.pallas_call(
        body, name=name, grid=(b, nb),
        in_specs=[col(COL_HQ, False), col(COL_FF, False), col(COL_HI, False),
                  col(COL_HQ, True), col(COL_FB, True), col(COL_HI, True),
                  _full((1, D_HGRN)), _full((1, D_HGRN)), _full((HBLK, HBLK)), _full((HBLK, HBLK)),
                  _full(sq), _full(sq), _full((CHUNK, CHUNK * CHUNK))],
        out_specs=[col(0, False), col(0, True), st_spec(False), st_spec(True)],
        out_shape=[_sds((t, D_HGRN), F32)] * 2 + [_sds((b * nb,) + sq, F32)] * 2,
        scratch_shapes=[pltpu.VMEM(sq, F32)] * 2 + [pltpu.VMEM(blk, F32)] * 4,
        compiler_params=_params(("parallel", "arbitrary")),
    )(proj, proj, proj, proj, proj, proj, lb_f, lb_b, hc_f["cs"], hc_b["cs"], hc_f["seg"], hc_f["bd"], hc_f["sel"])


def _scan_replay(c, rev, st, v_ref, k_scr, b_scr, states, bd_ref):
    last = 0 if rev else CHUNK - 1
    rows = pl.ds(pl.multiple_of(c * CHUNK, CHUNK), CHUNK)
    states[c] = st
    bb = b_scr[rows, :]
    bl = bb[last:last + 1, :]
    ke = k_scr[rows, :] * jnp.exp(bl - bb)
    return st * jnp.exp(bl) + bd_ref[...] * _dot_tn(v_ref[rows, :].astype(BF16), ke.astype(BF16))


def _scan_chunk_bwd(c, rev, q_ref, v_ref, do_ref, k_scr, b_scr, states, dstate, dq_ref, dv_ref, dk_scr, db_scr,
                    seg_ref, bd_ref, sel_ref, selt_ref, seld_ref):
    last = 0 if rev else CHUNK - 1
    row = lax.broadcasted_iota(jnp.int32, (CHUNK, D_HGRN), 0)
    rows = pl.ds(pl.multiple_of(c * CHUNK, CHUNK), CHUNK)
    q = q_ref[rows, :]
    k = k_scr[rows, :]
    v = v_ref[rows, :]
    bb = b_scr[rows, :]
    dout = do_ref[rows, :]
    bl = bb[last:last + 1, :]
    st_p = states[c]
    dst_n = dstate[...]
    eb = jnp.exp(bb)
    ebl = jnp.exp(bl - bb)
    ebl_last = jnp.exp(bl)
    qe = q * eb
    ke = k * ebl
    dob = dout.astype(BF16)
    dstb = dst_n.astype(BF16)
    dqe = jnp.dot(dob, st_p.astype(BF16), preferred_element_type=F32)
    dke = jnp.dot(v.astype(BF16), dstb, preferred_element_type=F32)
    dv = _dot_nt(ke.astype(BF16), dstb)
    dbl = jnp.sum(dst_n * st_p, axis=0, keepdims=True) * ebl_last + jnp.sum(dke * ke, axis=0, keepdims=True)
    dq = dqe * eb
    dk = dke * ebl
    db = dqe * qe - dke * ke
    dec = _pair_decay(bb, rev)
    q_rep = _rows_rep(q)
    k_til = _tile_rows(k)
    do_rep = _rows_rep(dout)
    pairs = dec * q_rep * k_til
    a = jnp.dot(pairs.astype(BF16), seg_ref[...], preferred_element_type=F32)
    wb = jnp.dot((_tile_rows(v) * do_rep).astype(BF16), seg_ref[...], preferred_element_type=F32)
    gdec = wb * dec
    dq = dq + jnp.dot(sel_ref[...], (gdec * k_til).astype(BF16), preferred_element_type=F32)
    dk = dk + jnp.dot(selt_ref[...], (gdec * q_rep).astype(BF16), preferred_element_type=F32)
    dv = dv + jnp.dot(selt_ref[...], (a * do_rep).astype(BF16), preferred_element_type=F32)
    db = db + jnp.dot(seld_ref[...], (wb * pairs).astype(BF16), preferred_element_type=F32)
    db = db + jnp.where(row == last, dbl, 0.0)
    dq_ref[rows, :] = dq
    dv_ref[rows, :] = dv
    dk_scr[rows, :] = dk
    db_scr[rows, :] = db
    dstate[...] = dst_n * ebl_last + bd_ref[...] * _dot_tn(dob, qe.astype(BF16))


def _hgrn_bwd2(name, proj, lb_f, lb_b, st_f, st_b, do, b, s, hc_f, hc_b):
    t = proj.shape[0]
    nb = s // HBLK
    n_ch = HBLK // CHUNK

    def body(qf_ref, zf_ref, vf_ref, dof_ref, stf_ref, qb_ref, zb_ref, vb_ref, dob_ref, stb_ref, lbf_ref, lbb_ref,
             csf_ref, csb_ref, cstf_ref, cstb_ref, seg_ref, bd_ref, sel_ref, selt_ref, seld_ref,
             dqf_ref, dvf_ref, dzf_ref, dqb_ref, dvb_ref, dzb_ref, dlbf_ref, dlbb_ref,
             dstate_f, dstate_b, states_f, states_b, bf_scr, bb_scr, kf_scr, kb_scr, dbf_scr, dbb_scr, dkf_scr,
             dkb_scr):
        first = jnp.logical_and(pl.program_id(0) == 0, pl.program_id(1) == 0)

        @pl.when(first)
        def _():
            dlbf_ref[...] = jnp.zeros_like(dlbf_ref)
            dlbb_ref[...] = jnp.zeros_like(dlbb_ref)

        @pl.when(pl.program_id(1) == 0)
        def _():
            dstate_f[...] = jnp.zeros_like(dstate_f)
            dstate_b[...] = jnp.zeros_like(dstate_b)

        gates = []
        for z_ref, lb_ref, cs_ref, k_scr, b_scr in ((zf_ref, lbf_ref, csf_ref, kf_scr, bf_scr),
                                                     (zb_ref, lbb_ref, csb_ref, kb_scr, bb_scr)):
            sig, f, g, sn, kk = _gates(z_ref[...], lb_ref[...])
            k_scr[...] = kk
            b_scr[...] = _ldot3(cs_ref[...], g)
            gates.append((sig, f, sn))

        def replay(i, carry):
            return (_scan_replay(i, False, carry[0], vf_ref, kf_scr, bf_scr, states_f, bd_ref),
                    _scan_replay(n_ch - 1 - i, True, carry[1], vb_ref, kb_scr, bb_scr, states_b, bd_ref))

        lax.fori_loop(0, n_ch, replay, (stf_ref[...], stb_ref[...]))

        def chunk(i, carry):
            _scan_chunk_bwd(n_ch - 1 - i, False, qf_ref, vf_ref, dof_ref, kf_scr, bf_scr, states_f, dstate_f, dqf_ref,
                            dvf_ref, dkf_scr, dbf_scr, seg_ref, bd_ref, sel_ref, selt_ref, seld_ref)
            _scan_chunk_bwd(i, True, qb_ref, vb_ref, dob_ref, kb_scr, bb_scr, states_b, dstate_b, dqb_ref,
                            dvb_ref, dkb_scr, dbb_scr, seg_ref, bd_ref, sel_ref, selt_ref, seld_ref)
            return carry

        lax.fori_loop(0, n_ch, chunk, 0)
        for (sig, f, sn), lb_ref, cst_ref, db_scr, dk_scr, dz_ref, dlb_ref in (
                (gates[0], lbf_ref, cstf_ref, dbf_scr, dkf_scr, dzf_ref, dlbf_ref),
                (gates[1], lbb_ref, cstb_ref, dbb_scr, dkb_scr, dzb_ref, dlbb_ref)):
            lbv = lb_ref[...]
            hi, lo = _split2(db_scr[...])
            dg = (jnp.dot(cst_ref[...], hi, preferred_element_type=F32)
                  + jnp.dot(cst_ref[...], lo, preferred_element_type=F32))
            dgf = jnp.where(f > F_MIN, dg / f, 0.0)
            dk = dk_scr[...]
            dz_ref[...] = dgf * (1.0 - lbv) * sig * (1.0 - sig) - dk * (1.0 - lbv) * sn * (1.0 - sn)
            dlb_ref[...] += _rowgroups(dgf * (1.0 - sig) - dk * sn)

    def col(c, rev):
        return pl.BlockSpec((HBLK, D_HGRN), lambda bi, j: (bi * nb + (j if rev else (nb - 1 - j)), c))

    def st_spec(rev):
        return pl.BlockSpec((None, D_HGRN, D_HGRN), lambda bi, j: (bi * nb + (j if rev else (nb - 1 - j)), 0, 0))

    sq = (D_HGRN, D_HGRN)
    blk = (HBLK, D_HGRN)
    pairs_shape = (CHUNK, CHUNK * CHUNK)
    return pl.pallas_call(
        body, name=name, grid=(b, nb),
        in_specs=[col(COL_HQ, False), col(COL_FF, False), col(COL_HI, False), col(0, False), st_spec(False),
                  col(COL_HQ, True), col(COL_FB, True), col(COL_HI, True), col(0, True), st_spec(True),
                  _full((1, D_HGRN)), _full((1, D_HGRN)), _full((HBLK, HBLK)), _full((HBLK, HBLK)),
                  _full((HBLK, HBLK)), _full((HBLK, HBLK)), _full(sq), _full(sq),
                  _full(pairs_shape), _full(pairs_shape), _full(pairs_shape)],
        out_specs=[col(0, False)] * 3 + [col(0, True)] * 3 + [_full((8, D_HGRN))] * 2,
        out_shape=[_sds((t, D_HGRN), F32)] * 6 + [_sds((8, D_HGRN), F32)] * 2,
        scratch_shapes=[pltpu.VMEM(sq, F32)] * 2 + [pltpu.VMEM((n_ch,) + sq, F32)] * 2 + [pltpu.VMEM(blk, F32)] * 8,
        compiler_params=_params(("arbitrary", "arbitrary")),
    )(proj, proj, proj, do, st_f, proj, proj, proj, do, st_b, lb_f, lb_b, hc_f["cs"], hc_b["cs"], hc_f["cs_t"],
      hc_b["cs_t"], hc_f["seg"], hc_f["bd"], hc_f["sel"], hc_f["selt"], hc_f["seld"])


def _lower_bounds(logits):
    n = logits.shape[1]

    def body(x_ref, o_ref):
        x = x_ref[...]
        for d in range(2):
            rows = [x[l * 2 + d:l * 2 + d + 1, :] for l in range(DEPTH)]
            mx = functools.reduce(jnp.maximum, rows)
            ex = [jnp.exp(r - mx) for r in rows]
            tot = functools.reduce(lambda a, c: a + c, ex)
            sm = [e / tot for e in ex]
            run = jnp.zeros_like(sm[0])
            for l in range(DEPTH):
                run = run + sm[l]
                o_ref[l * 2 + d:l * 2 + d + 1, :] = run - sm[0]

    return pl.pallas_call(body, name="hgrn_lower_bounds", out_shape=_sds(logits.shape, F32),
                          in_specs=[_full(logits.shape)], out_specs=_full(logits.shape), grid=(1,),
                          compiler_params=_params(("arbitrary",)))(logits)


def _lower_bounds_bwd(logits, dlb):
    def body(x_ref, g_ref, o_ref):
        x = x_ref[...]
        gv = g_ref[...]
        for d in range(2):
            rows = [x[l * 2 + d:l * 2 + d + 1, :] for l in range(DEPTH)]
            gr = [gv[l * 2 + d:l * 2 + d + 1, :] for l in range(DEPTH)]
            mx = functools.reduce(jnp.maximum, rows)
            ex = [jnp.exp(r - mx) for r in rows]
            tot = functools.reduce(lambda a, c: a + c, ex)
            sm = [e / tot for e in ex]
            dsm = []
            for i in range(DEPTH):
                acc = functools.reduce(lambda a, c: a + c, gr[i:])
                if i == 0:
                    acc = acc - functools.reduce(lambda a, c: a + c, gr)
                dsm.append(acc)
            inner = functools.reduce(lambda a, c: a + c, [sm[i] * dsm[i] for i in range(DEPTH)])
            for i in range(DEPTH):
                o_ref[i * 2 + d:i * 2 + d + 1, :] = sm[i] * (dsm[i] - inner)

    return pl.pallas_call(body, name="hgrn_lower_bounds_bwd", out_shape=_sds(logits.shape, F32),
                          in_specs=[_full(logits.shape), _full(logits.shape)], out_specs=_full(logits.shape),
                          grid=(1,), compiler_params=_params(("arbitrary",)))(logits, dlb)


def _conv_rows(s):
    return s + 2 * (CONV_PAD + 1)


def _conv_fwd(name, proj, dw_w, dw_b, ln_w, ln_b, pw_w, pw_b, b, s):
    t = proj.shape[0]
    pad = CONV_PAD + 1
    nt = s // CONV_TILE

    def body(a_ref, g_ref, w_ref, dwb_ref, lnw_ref, lnb_ref, pw_ref, pwb_ref, y_ref, upad, win):
        upad[0:pad, :] = jnp.zeros((pad, D_CONV), F32)
        upad[s + pad:s + 2 * pad, :] = jnp.zeros((pad, D_CONV), F32)

        def fill(i, carry):
            rows = pl.ds(pl.multiple_of(i * CONV_TILE, CONV_TILE), CONV_TILE)
            upad[pl.ds(pl.multiple_of(i * CONV_TILE + pad, pad), CONV_TILE), :] = a_ref[rows, :] * _sigmoid(g_ref[rows, :])
            return carry

        lax.fori_loop(0, nt, fill, 0)

        def tile(i, carry):
            r0 = pl.multiple_of(i * CONV_TILE, CONV_TILE)
            win[...] = upad[pl.ds(r0, CONV_TILE + 2 * pad), :]
            acc = jnp.zeros((CONV_TILE, D_CONV), F32)
            for j in range(CONV_W):
                acc = acc + win[j + 1:j + 1 + CONV_TILE, :] * w_ref[j:j + 1, :]
            c = acc + dwb_ref[...]
            mu = jnp.mean(c, axis=-1, keepdims=True)
            xc = c - mu
            rstd = lax.rsqrt(jnp.mean(xc * xc, axis=-1, keepdims=True) + LN_EPS)
            n = xc * rstd * lnw_ref[...] + lnb_ref[...]
            y_ref[pl.ds(r0, CONV_TILE), :] = (jnp.dot(_silu(n).astype(BF16), pw_ref[...].astype(BF16),
                                                      preferred_element_type=F32) + pwb_ref[...])
            return carry

        lax.fori_loop(0, nt, tile, 0)

    vec = _full((1, D_CONV))
    return pl.pallas_call(
        body, name=name, grid=(b,),
        in_specs=[pl.BlockSpec((s, D_CONV), lambda bi: (bi, COL_CA)), pl.BlockSpec((s, D_CONV), lambda bi: (bi, COL_CB)),
                  _full((CONV_W + 1, D_CONV)), vec, vec, vec, _full((D_CONV, D_CONV)), vec],
        out_specs=pl.BlockSpec((s, D_CONV), lambda bi: (bi, 0)), out_shape=_sds((t, D_CONV), F32),
        scratch_shapes=[pltpu.VMEM((_conv_rows(s), D_CONV), F32), pltpu.VMEM((CONV_TILE + 2 * pad, D_CONV), F32)],
        compiler_params=_params(("parallel",)),
    )(proj, proj, dw_w, dw_b, ln_w, ln_b, pw_w, pw_b)


def _conv_bwd(name, proj, dw_w, dw_b, ln_w, ln_b, pw_w, dy, b, s):
    t = proj.shape[0]
    pad = CONV_PAD + 1
    nt = s // CONV_TILE

    def body(a_ref, g_ref, w_ref, dwb_ref, lnw_ref, lnb_ref, pw_ref, dy_ref, dab_ref, dpw_ref, ddw_ref, dvec_ref,
             upad, dcpad, tap_acc, win, dwin):
        @pl.when(pl.program_id(0) == 0)
        def _():
            dpw_ref[...] = jnp.zeros_like(dpw_ref)
            ddw_ref[...] = jnp.zeros_like(ddw_ref)
            dvec_ref[...] = jnp.zeros_like(dvec_ref)

        zeros = jnp.zeros((pad, D_CONV), F32)
        upad[0:pad, :] = zeros
        upad[s + pad:s + 2 * pad, :] = zeros
        dcpad[0:pad, :] = zeros
        dcpad[s + pad:s + 2 * pad, :] = zeros
        tap_acc[...] = jnp.zeros_like(tap_acc)

        def inner(i):
            return pl.ds(pl.multiple_of(i * CONV_TILE + pad, pad), CONV_TILE)

        def fill(i, carry):
            rows = pl.ds(pl.multiple_of(i * CONV_TILE, CONV_TILE), CONV_TILE)
            upad[inner(i), :] = a_ref[rows, :] * _sigmoid(g_ref[rows, :])
            return carry

        lax.fori_loop(0, nt, fill, 0)

        def tile_a(i, carry):
            r0 = pl.multiple_of(i * CONV_TILE, CONV_TILE)
            win[...] = upad[pl.ds(r0, CONV_TILE + 2 * pad), :]
            acc = jnp.zeros((CONV_TILE, D_CONV), F32)
            for j in range(CONV_W):
                acc = acc + win[j + 1:j + 1 + CONV_TILE, :] * w_ref[j:j + 1, :]
            c = acc + dwb_ref[...]
            mu = jnp.mean(c, axis=-1, keepdims=True)
            xc = c - mu
            rstd = lax.rsqrt(jnp.mean(xc * xc, axis=-1, keepdims=True) + LN_EPS)
            xhat = xc * rstd
            n = xhat * lnw_ref[...] + lnb_ref[...]
            dyt = dy_ref[pl.ds(r0, CONV_TILE), :]
            dyb = dyt.astype(BF16)
            dpw_ref[...] += _dot_tn(_silu(n).astype(BF16), dyb)
            dn = _dot_nt(dyb, pw_ref[...].astype(BF16)) * _dsilu(n)
            dxh = dn * lnw_ref[...]
            dc = rstd * (dxh - jnp.mean(dxh, axis=-1, keepdims=True)
                         - xhat * jnp.mean(dxh * xhat, axis=-1, keepdims=True))
            dcpad[inner(i), :] = dc
            dvec_ref[0:1, :] += jnp.sum(dyt, axis=0, keepdims=True)
            dvec_ref[1:2, :] += jnp.sum(dn * xhat, axis=0, keepdims=True)
            dvec_ref[2:3, :] += jnp.sum(dn, axis=0, keepdims=True)
            dvec_ref[3:4, :] += jnp.sum(dc, axis=0, keepdims=True)
            return carry

        lax.fori_loop(0, nt, tile_a, 0)

        def tile_b(i, carry):
            r0 = pl.multiple_of(i * CONV_TILE, CONV_TILE)
            win[...] = upad[pl.ds(r0, CONV_TILE + 2 * pad), :]
            dwin[...] = dcpad[pl.ds(r0, CONV_TILE + 2 * pad), :]
            dct = dwin[pad:pad + CONV_TILE, :]
            du = jnp.zeros((CONV_TILE, D_CONV), F32)
            for j in range(CONV_W):
                du = du + dwin[2 * pad - 1 - j:2 * pad - 1 - j + CONV_TILE, :] * w_ref[j:j + 1, :]
                tap_acc[8 * j:8 * j + 8, :] += _rowgroups(dct * win[j + 1:j + 1 + CONV_TILE, :])
            rows = pl.ds(r0, CONV_TILE)
            sg = _sigmoid(g_ref[rows, :])
            dab_ref[rows, 0:D_CONV] = (du * sg).astype(BF16)
            dab_ref[rows, D_CONV:2 * D_CONV] = (du * a_ref[rows, :] * sg * (1.0 - sg)).astype(BF16)
            return carry

        lax.fori_loop(0, nt, tile_b, 0)
        for j in range(CONV_W):
            ddw_ref[j:j + 1, :] += jnp.sum(tap_acc[8 * j:8 * j + 8, :], axis=0, keepdims=True)

    vec = _full((1, D_CONV))
    return pl.pallas_call(
        body, name=name, grid=(b,),
        in_specs=[pl.BlockSpec((s, D_CONV), lambda bi: (bi, COL_CA)), pl.BlockSpec((s, D_CONV), lambda bi: (bi, COL_CB)),
                  _full((CONV_W + 1, D_CONV)), vec, vec, vec, _full((D_CONV, D_CONV)),
                  pl.BlockSpec((s, D_CONV), lambda bi: (bi, 0))],
        out_specs=[pl.BlockSpec((s, 2 * D_CONV), lambda bi: (bi, 0)), _full((D_CONV, D_CONV)),
                   _full((CONV_W + 1, D_CONV)), _full((8, D_CONV))],
        out_shape=[_sds((t, 2 * D_CONV), BF16), _sds((D_CONV, D_CONV), F32), _sds((CONV_W + 1, D_CONV), F32),
                   _sds((8, D_CONV), F32)],
        scratch_shapes=[pltpu.VMEM((_conv_rows(s), D_CONV), F32), pltpu.VMEM((_conv_rows(s), D_CONV), F32),
                        pltpu.VMEM((8 * CONV_W, D_CONV), F32), pltpu.VMEM((CONV_TILE + 2 * pad, D_CONV), F32),
                        pltpu.VMEM((CONV_TILE + 2 * pad, D_CONV), F32)],
        compiler_params=_params(("arbitrary",)),
    )(proj, proj, dw_w, dw_b, ln_w, ln_b, pw_w, dy)


def _mix_fwd(name, y_attn, o_fw, o_bw, proj, y_conv, aw, gw, cw, seg):
    t = y_attn.shape[0]
    tm = _row_tile(t)

    def body(ya_ref, of_ref, ob_ref, hg_ref, yc_ref, aw_ref, gw_ref, cw_ref, seg_ref, o_ref):
        ya = ya_ref[...]
        ra = lax.rsqrt(jnp.mean(ya * ya, axis=-1, keepdims=True) + EPS)
        o_ref[:, 0:D_ATTN] = (ya * ra * aw_ref[...]).astype(BF16)
        o = of_ref[...] + ob_ref[...]
        ro = lax.rsqrt(jnp.dot((o * o).astype(BF16), seg_ref[...], preferred_element_type=F32) + EPS)
        o_ref[:, D_ATTN:D_ATTN + D_HGRN] = (o * ro * gw_ref[...] * _silu(hg_ref[...])).astype(BF16)
        yc = yc_ref[...]
        rc = lax.rsqrt(jnp.mean(yc * yc, axis=-1, keepdims=True) + EPS)
        o_ref[:, D_ATTN + D_HGRN:D_MODEL] = (yc * rc * cw_ref[...]).astype(BF16)

    def tile(w, c=0):
        return pl.BlockSpec((tm, w), lambda i: (i, c))

    return pl.pallas_call(
        body, name=name, grid=(t // tm,),
        in_specs=[tile(D_ATTN), tile(D_HGRN), tile(D_HGRN), tile(D_HGRN, COL_HG), tile(D_CONV),
                  _full((1, D_ATTN)), _full((1, D_HGRN)), _full((1, D_CONV)), _full((D_HGRN, D_HGRN))],
        out_specs=tile(D_MODEL), out_shape=_sds((t, D_MODEL), BF16),
        compiler_params=_params(("parallel",)),
    )(y_attn, o_fw, o_bw, proj, y_conv, aw, gw, cw, seg)


def _mix_bwd(name, dmix, y_attn, o_fw, o_bw, proj, y_conv, aw, gw, cw, seg):
    t = y_attn.shape[0]
    tm = _row_tile(t)

    def rms_bwd(x, w, dy):
        r = lax.rsqrt(jnp.mean(x * x, axis=-1, keepdims=True) + EPS)
        gwv = dy * w
        return r * gwv - x * (r * r * r) * jnp.mean(gwv * x, axis=-1, keepdims=True), _rowgroups(dy * x * r)

    def body(dm_ref, ya_ref, of_ref, ob_ref, hg_ref, yc_ref, aw_ref, gw_ref, cw_ref, seg_ref,
             dya_ref, do_ref, dhg_ref, dyc_ref, daw_ref, dgw_ref, dcw_ref):
        @pl.when(pl.program_id(0) == 0)
        def _():
            daw_ref[...] = jnp.zeros_like(daw_ref)
            dgw_ref[...] = jnp.zeros_like(dgw_ref)
            dcw_ref[...] = jnp.zeros_like(dcw_ref)

        dya, daw = rms_bwd(ya_ref[...], aw_ref[...], dm_ref[:, 0:D_ATTN])
        dya_ref[...] = dya
        daw_ref[...] += daw
        dyc, dcw = rms_bwd(yc_ref[...], cw_ref[...], dm_ref[:, D_ATTN + D_HGRN:D_MODEL])
        dyc_ref[...] = dyc
        dcw_ref[...] += dcw
        d2 = dm_ref[:, D_ATTN:D_ATTN + D_HGRN]
        o = of_ref[...] + ob_ref[...]
        hg = hg_ref[...]
        ro = lax.rsqrt(jnp.dot((o * o).astype(BF16), seg_ref[...], preferred_element_type=F32) + EPS)
        dn = d2 * _silu(hg)
        dhg_ref[...] = (d2 * o * ro * gw_ref[...] * _dsilu(hg)).astype(BF16)
        gwv = dn * gw_ref[...]
        do_ref[...] = ro * gwv - o * (ro * ro * ro) * _rdot2(gwv * o, seg_ref[...])
        dgw_ref[...] += _rowgroups(dn * o * ro)

    def tile(w, c=0):
        return pl.BlockSpec((tm, w), lambda i: (i, c))

    return pl.pallas_call(
        body, name=name, grid=(t // tm,),
        in_specs=[tile(D_MODEL), tile(D_ATTN), tile(D_HGRN), tile(D_HGRN), tile(D_HGRN, COL_HG), tile(D_CONV),
                  _full((1, D_ATTN)), _full((1, D_HGRN)), _full((1, D_CONV)), _full((D_HGRN, D_HGRN))],
        out_specs=[tile(D_ATTN), tile(D_HGRN), tile(D_HGRN), tile(D_CONV),
                   _full((8, D_ATTN)), _full((8, D_HGRN)), _full((8, D_CONV))],
        out_shape=[_sds((t, D_ATTN), F32), _sds((t, D_HGRN), F32), _sds((t, D_HGRN), BF16), _sds((t, D_CONV), F32),
                   _sds((8, D_ATTN), F32), _sds((8, D_HGRN), F32), _sds((8, D_CONV), F32)],
        compiler_params=_params(("arbitrary",)),
    )(dmix, y_attn, o_fw, o_bw, proj, y_conv, aw, gw, cw, seg)


def _dproj(name, dp_attn, dq_f, dq_b, dz_fw, dz_bw, dv_f, dv_b, dhg, dp_conv):
    t = dq_f.shape[0]
    tm = _row_tile(t)
    wa, wc = dp_attn.shape[1], dp_conv.shape[1]

    def body(at_ref, qf_ref, qb_ref, zf_ref, zb_ref, vf_ref, vb_ref, hg_ref, cv_ref, o_ref):
        o_ref[:, 0:wa] = at_ref[...]
        cols = (qf_ref[...] + qb_ref[...], zf_ref[...], zb_ref[...], vf_ref[...] + vb_ref[...], hg_ref[...])
        for i, val in enumerate(cols):
            o_ref[:, wa + i * D_HGRN:wa + (i + 1) * D_HGRN] = val.astype(BF16)
        o_ref[:, wa + 5 * D_HGRN:D_IN] = cv_ref[...]

    tile = lambda w: pl.BlockSpec((tm, w), lambda i: (i, 0))
    return pl.pallas_call(
        body, name=name, grid=(t // tm,), in_specs=[tile(wa)] + [tile(D_HGRN)] * 7 + [tile(wc)],
        out_specs=tile(D_IN), out_shape=_sds((t, D_IN), BF16), compiler_params=_params(("parallel",)),
    )(dp_attn, dq_f, dq_b, dz_fw, dz_bw, dv_f, dv_b, dhg, dp_conv)


def _mm_tile(t):
    return min(512, t)


def _resident(shape):
    n = len(shape)
    return pl.BlockSpec(tuple(shape), lambda *_: (0,) * n, pipeline_mode=pl.Buffered(1))


def _w_blk(rows, cols, j_of):
    return pl.BlockSpec((None, rows, cols), lambda *g: (j_of(*g), 0, 0))


def _layer_fwd(l, x, wget, sm, tabs, cst, b, s, deps):
    t = x.shape[0]
    tm = _mm_tile(t)
    nt = t // tm
    pre = "l%d_" % l
    row = lambda w: pl.BlockSpec((tm, w), lambda i, *_: (i, 0))

    def normed(x_ref, nw_ref):
        xv = x_ref[...]
        r = lax.rsqrt(jnp.mean(xv * xv, axis=-1, keepdims=True) + EPS)
        return (xv * r * nw_ref[...]).astype(BF16)

    def in_body(x_ref, nw_ref, w_ref, *rest):
        o_ref, h_ref = rest[-2:]
        hv = normed(x_ref, nw_ref)
        h_ref[...] = hv
        for j in range(N_CHIP):
            o_ref[:, j * IN_BLK:(j + 1) * IN_BLK] = jnp.dot(hv, w_ref[j], preferred_element_type=F32)

    w_in = wget(l, "w_in", x)
    proj, h1 = pl.pallas_call(
        in_body, name=pre + "in_proj", grid=(nt,),
        in_specs=[row(D_MODEL), _full((1, D_MODEL)), _resident(w_in.shape)] + [_full(a.shape) for a in deps],
        out_specs=[row(D_IN), row(D_MODEL)], out_shape=[_sds((t, D_IN), F32), _sds((t, D_MODEL), BF16)],
        compiler_params=_params(("parallel",)),
    )(x, sm["mix_norm_w"][l], w_in, *deps)
    qn, kr, vr = _attn_prep(pre + "attn_prep", proj, s, tabs, sm["q_norm_w"][l], sm["k_norm_w"][l], cst["attn"])
    y_attn = _attn_fwd(pre + "attn", qn, kr, vr, b, s)
    o_fw, o_bw, st_fw, st_bw = _hgrn_fwd2(pre + "hgrn", proj, sm["lb"][l][0], sm["lb"][l][1], b, s, cst["hg_fw"],
                                          cst["hg_bw"])
    y_conv = _conv_fwd(pre + "conv", proj, sm["conv_dw_w"][l], sm["conv_dw_b"][l], sm["conv_ln_w"][l],
                       sm["conv_ln_b"][l], sm["conv_pw_w"][l], sm["conv_pw_b"][l], b, s)
    mixed = _mix_fwd(pre + "mix", y_attn, o_fw, o_bw, proj, y_conv, sm["attn_out_norm_w"][l], sm["gnorm_w"][l],
                     sm["conv_out_norm_w"][l], cst["seg_h"])
    (x1,) = _mm(pre + "out_proj", (nt,),
                [(mixed, row(D_MODEL), wget(l, "w_out", mixed),
                  pl.BlockSpec((N_CHIP, OUT_BLK, D_MODEL), lambda i: (0, 0, 0)), NN)],
                [(x, row(D_MODEL))], [(_sds((t, D_MODEL), F32), row(D_MODEL))],
                lambda tot, xr: (xr + tot,))
    ff3 = pl.BlockSpec((N_CHIP, tm, FF_BLK), lambda i: (0, i, 0))
    ffs = _sds((N_CHIP, t, FF_BLK), BF16)

    def gu_body(x_ref, nw_ref, wg_ref, wu_ref, g_ref, u_ref, a_ref, h_ref):
        hv = normed(x_ref, nw_ref)
        h_ref[...] = hv
        for j in range(N_CHIP):
            gv = jnp.dot(hv, wg_ref[j], preferred_element_type=F32)
            uv = jnp.dot(hv, wu_ref[j], preferred_element_type=F32)
            g_ref[j] = gv.astype(BF16)
            u_ref[j] = uv.astype(BF16)
            a_ref[j] = (_silu(gv) * uv).astype(BF16)

    w_gate, w_up = wget(l, "w_gate", x1), wget(l, "w_up", x1)
    gate, up, act, h2 = pl.pallas_call(
        gu_body, name=pre + "ffn_gate_up", grid=(nt,),
        in_specs=[row(D_MODEL), _full((1, D_MODEL)), _resident(w_gate.shape), _resident(w_up.shape)],
        out_specs=[ff3, ff3, ff3, row(D_MODEL)], out_shape=[ffs, ffs, ffs, _sds((t, D_MODEL), BF16)],
        compiler_params=_params(("parallel",)),
    )(x1, sm["ffn_norm_w"][l], w_gate, w_up)

    def down_body(a_ref, w_ref, x_ref, o_ref):
        tot = x_ref[...]
        for j in range(N_CHIP):
            tot = tot + jnp.dot(a_ref[j], w_ref[j], preferred_element_type=F32)
        o_ref[...] = tot

    w_down = wget(l, "w_down", act)
    x2 = pl.pallas_call(
        down_body, name=pre + "ffn_down", grid=(nt,), in_specs=[ff3, _resident(w_down.shape), row(D_MODEL)],
        out_specs=row(D_MODEL), out_shape=_sds((t, D_MODEL), F32), compiler_params=_params(("parallel",)),
    )(act, w_down, x1)
    saved = dict(x=x, h1=h1, proj=proj, qn=qn, kr=kr, vr=vr, y_attn=y_attn, o_fw=o_fw, o_bw=o_bw, st_fw=st_fw,
                 st_bw=st_bw, y_conv=y_conv, mixed=mixed, x1=x1, h2=h2, gate=gate, up=up, act=act)
    return x2, saved


def _layer_bwd(l, dx2, sv, wget, sm, tabs, cst, b, s, on_grads):
    t = dx2.shape[0]
    tm = _mm_tile(t)
    nt = t // tm
    pre = "l%d_" % l
    tk = min(2048, t)
    nk = t // tk
    row = lambda w: pl.BlockSpec((tm, w), lambda i, *_: (i, 0))
    ff3 = pl.BlockSpec((N_CHIP, tm, FF_BLK), lambda i: (0, i, 0))
    ffs = _sds((N_CHIP, t, FF_BLK), BF16)

    w_down, w_gate, w_up = wget(l, "w_down", dx2), wget(l, "w_gate", dx2), wget(l, "w_up", dx2)

    def ddx_body(dx_ref, w_ref, g_ref, u_ref, dg_ref, du_ref):
        dxb = dx_ref[...].astype(BF16)
        for j in range(N_CHIP):
            da = _dot_nt(dxb, w_ref[j])
            g = g_ref[j].astype(F32)
            dg_ref[j] = (da * u_ref[j].astype(F32) * _dsilu(g)).astype(BF16)
            du_ref[j] = (da * _silu(g)).astype(BF16)

    dgate, dup = pl.pallas_call(
        ddx_body, name=pre + "ffn_down_dx", grid=(nt,), in_specs=[row(D_MODEL), _resident(w_down.shape), ff3, ff3],
        out_specs=[ff3, ff3], out_shape=[ffs, ffs], compiler_params=_params(("parallel",)),
    )(dx2, w_down, sv["gate"], sv["up"])
    colt = lambda w: pl.BlockSpec((tk, w), lambda j, k: (k, 0))
    fft = pl.BlockSpec((None, tk, FF_BLK), lambda j, k: (j, k, 0))
    (g_down,) = _mm(pre + "ffn_down_dw", (N_CHIP, nk), [(sv["act"], fft, dx2, colt(D_MODEL), TN)], [],
                    [(_sds((N_CHIP, FF_BLK, D_MODEL), BF16), pl.BlockSpec((None, FF_BLK, D_MODEL), lambda j, k: (j, 0, 0)))],
                    lambda tot: (tot,), acc=(1, (FF_BLK, D_MODEL)))
    wff = pl.BlockSpec((None, D_MODEL, FF_BLK), lambda j, k: (j, 0, 0))
    (g_gate,) = _mm(pre + "ffn_gate_dw", (N_CHIP, nk), [(sv["h2"], colt(D_MODEL), dgate, fft, TN)], [],
                    [(_sds((N_CHIP, D_MODEL, FF_BLK), BF16), wff)], lambda tot: (tot,), acc=(1, (D_MODEL, FF_BLK)))
    (g_up,) = _mm(pre + "ffn_up_dw", (N_CHIP, nk), [(sv["h2"], colt(D_MODEL), dup, fft, TN)], [],
                  [(_sds((N_CHIP, D_MODEL, FF_BLK), BF16), wff)], lambda tot: (tot,), acc=(1, (D_MODEL, FF_BLK)))

    def norm_bwd_tail(dh, x_ref, nw_ref, dres_ref, dx_ref, dw_ref):
        xv = x_ref[...]
        r = lax.rsqrt(jnp.mean(xv * xv, axis=-1, keepdims=True) + EPS)
        gw = dh * nw_ref[...]
        dx_ref[...] = dres_ref[...] + r * gw - xv * (r * r * r) * jnp.mean(gw * xv, axis=-1, keepdims=True)

        @pl.when(pl.program_id(0) == 0)
        def _():
            dw_ref[...] = jnp.zeros_like(dw_ref)

        dw_ref[...] += _rowgroups(dh * xv * r)

    def dh_body(dg_ref, du_ref, wg_ref, wu_ref, x_ref, nw_ref, dres_ref, *rest):
        tot = None
        for j in range(N_CHIP):
            r = _dot_nt(dg_ref[j], wg_ref[j]) + _dot_nt(du_ref[j], wu_ref[j])
            tot = r if tot is None else tot + r
        norm_bwd_tail(tot, x_ref, nw_ref, dres_ref, *rest[-2:])

    deps = on_grads(l, dict(w_gate=g_gate, w_up=g_up, w_down=g_down))
    dx1, d_ffn_norm = pl.pallas_call(
        dh_body, name=pre + "ffn_dh", grid=(nt,),
        in_specs=[ff3, ff3, _resident(w_gate.shape), _resident(w_up.shape), row(D_MODEL), _full((1, D_MODEL)),
                  row(D_MODEL)] + [_full(a.shape) for a in deps],
        out_specs=[row(D_MODEL), _full((8, D_MODEL))], out_shape=[_sds((t, D_MODEL), F32), _sds((8, D_MODEL), F32)],
        compiler_params=_params(("arbitrary",)),
    )(dgate, dup, w_gate, w_up, sv["x1"], sm["ffn_norm_w"][l], dx2, *deps)

    (dmix,) = _mm(pre + "out_proj_dx", (nt,),
                  [(dx1, row(D_MODEL), wget(l, "w_out", dx2),
                    pl.BlockSpec((N_CHIP, OUT_BLK, D_MODEL), lambda i: (0, 0, 0)), NT)], [],
                  [(_sds((t, D_MODEL), F32), row(D_MODEL))], lambda tot: (tot,))
    (g_out,) = _mm(pre + "out_proj_dw", (N_CHIP, nk),
                   [(sv["mixed"], pl.BlockSpec((tk, OUT_BLK), lambda j, k: (k, j)), dx1, colt(D_MODEL), TN)], [],
                   [(_sds((N_CHIP, OUT_BLK, D_MODEL), BF16), pl.BlockSpec((None, OUT_BLK, D_MODEL), lambda j, k: (j, 0, 0)))],
                   lambda tot: (tot,), acc=(1, (OUT_BLK, D_MODEL)))
    proj = sv["proj"]
    dya, do_h, dhg, dyc, d_aw, d_gw, d_cw = _mix_bwd(
        pre + "mix_bwd", dmix, sv["y_attn"], sv["o_fw"], sv["o_bw"], proj, sv["y_conv"],
        sm["attn_out_norm_w"][l], sm["gnorm_w"][l], sm["conv_out_norm_w"][l], cst["seg_h"])
    dqs, dkr, dvr = _attn_bwd(pre + "attn_bwd", sv["qn"], sv["kr"], sv["vr"], dya, b, s)
    dp_attn, d_qw, d_kw = _attn_prep_bwd(pre + "attn_prep_bwd", proj, s, tabs, sm["q_norm_w"][l], sm["k_norm_w"][l],
                                         cst["attn"], dqs, dkr, dvr)
    dq_f, dv_f, dz_fw, dq_b, dv_b, dz_bw, dlb_fw, dlb_bw = _hgrn_bwd2(
        pre + "hgrn_bwd", proj, sm["lb"][l][0], sm["lb"][l][1], sv["st_fw"], sv["st_bw"], do_h, b, s,
        cst["hg_fw"], cst["hg_bw"])
    dp_conv, d_pw, d_dw, d_cvec = _conv_bwd(pre + "conv_bwd", proj, sm["conv_dw_w"][l], sm["conv_dw_b"][l],
                                            sm["conv_ln_w"][l], sm["conv_ln_b"][l], sm["conv_pw_w"][l], dyc, b, s)
    dproj = _dproj(pre + "dproj", dp_attn, dq_f, dq_b, dz_fw, dz_bw, dv_f, dv_b, dhg, dp_conv)
    g_pw = d_pw.reshape(N_CHIP, D_CONV // N_CHIP, D_CONV).astype(BF16)

    (g_in,) = _mm(pre + "in_proj_dw", (N_CHIP, nk),
                  [(sv["h1"], colt(D_MODEL), dproj, pl.BlockSpec((tk, IN_BLK), lambda j, k: (k, j)), TN)], [],
                  [(_sds((N_CHIP, D_MODEL, IN_BLK), BF16), pl.BlockSpec((None, D_MODEL, IN_BLK), lambda j, k: (j, 0, 0)))],
                  lambda tot: (tot,), acc=(1, (D_MODEL, IN_BLK)))

    def indx_body(dp_ref, w_ref, x_ref, nw_ref, dres_ref, *rest):
        tot = None
        for j in range(N_CHIP):
            r = _dot_nt(dp_ref[:, j * IN_BLK:(j + 1) * IN_BLK], w_ref[j])
            tot = r if tot is None else tot + r
        norm_bwd_tail(tot, x_ref, nw_ref, dres_ref, *rest[-2:])

    w_in = wget(l, "w_in", dx2)
    deps = on_grads(l, dict(w_in=g_in, w_out=g_out, conv_pw_w=g_pw))
    dx, d_mix_norm = pl.pallas_call(
        indx_body, name=pre + "in_proj_dx", grid=(nt,),
        in_specs=[row(D_IN), _resident(w_in.shape), row(D_MODEL), _full((1, D_MODEL)), row(D_MODEL)]
        + [_full(a.shape) for a in deps],
        out_specs=[row(D_MODEL), _full((8, D_MODEL))], out_shape=[_sds((t, D_MODEL), F32), _sds((8, D_MODEL), F32)],
        compiler_params=_params(("arbitrary",)),
    )(dproj, w_in, sv["x"], sm["mix_norm_w"][l], dx1, *deps)
    heads = lambda v, n: v.sum(axis=0).reshape(n, HEAD_DIM).sum(axis=0)
    small = dict(
        mix_norm_w=d_mix_norm.sum(axis=0), q_norm_w=heads(d_qw, D_ATTN // HEAD_DIM), k_norm_w=heads(d_kw, N_KV),
        lb=jnp.stack([dlb_fw.sum(axis=0), dlb_bw.sum(axis=0)]), hgrn_gnorm_w=heads(d_gw, D_HGRN // HEAD_DIM),
        conv_dw_w=d_dw[:CONV_W], conv_dw_b=d_cvec[3], conv_ln_w=d_cvec[1], conv_ln_b=d_cvec[2],
        conv_pw_b=d_cvec[0], attn_out_norm_w=d_aw.sum(axis=0), conv_out_norm_w=d_cw.sum(axis=0),
        ffn_norm_w=d_ffn_norm.sum(axis=0))
    return dx, small


SMALL_ORDER = ("mix_norm_w", "q_norm_w", "k_norm_w", "lb", "hgrn_gnorm_w", "conv_dw_w", "conv_dw_b", "conv_ln_w",
               "conv_ln_b", "conv_pw_b", "attn_out_norm_w", "conv_out_norm_w", "ffn_norm_w")
BIG_ORDER = ("w_in", "w_out", "w_gate", "w_up", "w_down")
SCATTER_ORDER = BIG_ORDER + ("conv_pw_w",)


def _local_step(x, target, wget, sm, deps, on_grads):
    b, s, d = x.shape
    t = b * s
    cos, sin = _rope_tables(s)
    tabs = dict(cq=jnp.tile(cos, (1, D_ATTN // HEAD_DIM)), sq=jnp.tile(sin, (1, D_ATTN // HEAD_DIM)),
                ck=jnp.tile(cos, (1, N_KV)), sk=jnp.tile(sin, (1, N_KV)))
    cst = dict(attn=_attn_consts(), hg_fw=_hgrn_consts(False), hg_bw=_hgrn_consts(True),
               seg_h=_bf(_seg_matrix(D_HGRN, HEAD_DIM, 1.0 / HEAD_DIM)))
    vec = lambda a: a.reshape(DEPTH, 1, -1)
    smk = dict(sm)
    for n in ("mix_norm_w", "conv_dw_b", "conv_ln_w", "conv_ln_b", "conv_pw_b", "attn_out_norm_w", "conv_out_norm_w",
              "ffn_norm_w"):
        smk[n] = vec(sm[n])
    smk["q_norm_w"] = vec(jnp.tile(sm["q_norm_w"], (1, D_ATTN // HEAD_DIM)))
    smk["k_norm_w"] = vec(jnp.tile(sm["k_norm_w"], (1, N_KV)))
    smk["gnorm_w"] = vec(jnp.tile(sm["hgrn_gnorm_w"], (1, D_HGRN // HEAD_DIM)))
    smk["lb"] = sm["lb"].reshape(DEPTH, 2, 1, D_HGRN)
    smk["conv_dw_w"] = jnp.pad(sm["conv_dw_w"], ((0, 0), (0, 1), (0, 0)))

    h = x.reshape(t, d)
    saved = []
    for l in range(DEPTH):
        h, sv = _layer_fwd(l, h, wget, smk, tabs, cst, b, s, deps if l == 0 else ())
        saved.append(sv)
    dy, sq = _loss_kernel(h, target.reshape(t, d))
    sq_sum = jnp.sum(sq)
    dh = dy
    smalls = [None] * DEPTH
    for l in reversed(range(DEPTH)):
        dh, smalls[l] = _layer_bwd(l, dh, saved[l], wget, smk, tabs, cst, b, s, on_grads)
    return sq_sum, dh.reshape(b, s, d), smalls


HBM_SPEC = pl.BlockSpec(memory_space=pltpu.HBM)


def _exchange(name, arrs, mode):
    n = len(arrs)
    if mode == "gather8":
        flips = [(fx, fy, fc) for fx in (0, 1) for fy in (0, 1) for fc in (0, 1)][1:]
    elif mode == "sibling":
        flips = [(0, 0, 1)]
    else:
        flips = [(1, 0, 0), (0, 1, 0), (1, 1, 0)]
    n_f = len(flips)

    def body(*refs):
        ins, outs = refs[:n], refs[n:2 * n]
        send_sems, recv_sems, local_sems = refs[2 * n:]
        x, y, c = lax.axis_index("x"), lax.axis_index("y"), lax.axis_index("c")

        def slot_of(px, py, pc):
            return (2 * px + py) if mode != "gather8" else (4 * px + 2 * py + pc)

        me = slot_of(x, y, c)
        started = []
        for i in range(n):
            if mode != "sibling":
                src = ins[i].at[me] if mode == "scatter4" else ins[i]
                loc = pltpu.make_async_copy(src, outs[i].at[me], local_sems.at[i])
                loc.start()
                started.append(loc)
        sends, recvs = [], []
        for i in range(n):
            for f, (fx, fy, fc) in enumerate(flips):
                peer = (x ^ fx, y ^ fy, c ^ fc)
                ps = slot_of(*peer)
                if mode == "sibling":
                    src, dst, landed = ins[i], outs[i], outs[i]
                elif mode == "scatter4":
                    src, dst, landed = ins[i].at[ps], outs[i].at[me], outs[i].at[ps]
                else:
                    src, dst, landed = ins[i], outs[i].at[me], outs[i].at[ps]
                k = i * n_f + f
                cp = pltpu.make_async_remote_copy(src_ref=src, dst_ref=dst, send_sem=send_sems.at[k],
                                                  recv_sem=recv_sems.at[k], device_id=peer,
                                                  device_id_type=pl.DeviceIdType.MESH)
                cp.start()
                sends.append(cp)
                recvs.append(pltpu.make_async_remote_copy(src_ref=src, dst_ref=landed, send_sem=send_sems.at[k],
                                                          recv_sem=recv_sems.at[k], device_id=peer,
                                                          device_id_type=pl.DeviceIdType.MESH))
        for cp in sends:
            cp.wait_send()
        for cp in recvs:
            cp.wait_recv()
        for loc in started:
            loc.wait()

    def out_sds(a):
        if mode == "gather4":
            return _sds((N_CHIP,) + a.shape, a.dtype)
        if mode == "gather8":
            return _sds((N_DEV,) + a.shape, a.dtype)
        return _sds(a.shape, a.dtype)

    res = pl.pallas_call(
        body, name=name, in_specs=[HBM_SPEC] * n, out_specs=[HBM_SPEC] * n, out_shape=[out_sds(a) for a in arrs],
        scratch_shapes=[pltpu.SemaphoreType.DMA((n * n_f,)), pltpu.SemaphoreType.DMA((n * n_f,)),
                        pltpu.SemaphoreType.DMA((max(n, 1),))],
    )(*arrs)
    return list(res)


SEM_SPEC = pl.BlockSpec(memory_space=pltpu.SEMAPHORE)
SPLIT_EFFECT = pltpu.SideEffectType.DATAFLOW_SIDE_EFFECTING
CHIP_FLIPS = ((1, 0), (0, 1), (1, 1))


def _chip_copies(src_refs, land_refs, send_sems, recv_sems, scatter):
    x, y, c = lax.axis_index("x"), lax.axis_index("y"), lax.axis_index("c")
    me = 2 * x + y
    out = []
    for i, land in enumerate(land_refs):
        for f, (fx, fy) in enumerate(CHIP_FLIPS):
            peer = (x ^ fx, y ^ fy, c)
            ps = 2 * (x ^ fx) + (y ^ fy)
            src = src_refs[i].at[ps] if scatter else land.at[me]
            k = i * len(CHIP_FLIPS) + f
            kw = dict(send_sem=send_sems.at[k], recv_sem=recv_sems.at[k], device_id=peer,
                      device_id_type=pl.DeviceIdType.MESH)
            out.append((pltpu.make_async_remote_copy(src_ref=src, dst_ref=land.at[me], **kw),
                        pltpu.make_async_remote_copy(src_ref=src, dst_ref=land.at[ps], **kw)))
    return out


def _split_start(name, srcs, lands, scatter):
    n = len(lands)
    n_src = len(srcs)
    n_sem = n * len(CHIP_FLIPS)

    def body(*refs):
        src_refs = refs[:n_src]
        land_refs = refs[n_src:n_src + n]
        send_sems, recv_sems = refs[n_src + n], refs[n_src + n + 1]
        token = refs[-1]
        for start, _ in _chip_copies(src_refs, land_refs, send_sems, recv_sems, scatter):
            start.start()
        token[...] = jnp.zeros_like(token)

    arrs = list(srcs) + list(lands)
    res = pl.pallas_call(
        body, name=name,
        out_shape=(pltpu.SemaphoreType.DMA((n_sem,)), pltpu.SemaphoreType.DMA((n_sem,)),
                   *[pltpu.HBM(a.shape, a.dtype) for a in arrs], _sds((8, LANES), F32)),
        in_specs=[HBM_SPEC] * len(arrs),
        out_specs=(SEM_SPEC, SEM_SPEC, *[HBM_SPEC] * len(arrs), pl.BlockSpec(memory_space=pltpu.VMEM)),
        input_output_aliases={i: 2 + i for i in range(len(arrs))},
        compiler_params=pltpu.CompilerParams(has_side_effects=SPLIT_EFFECT),
    )(*[pltpu.with_memory_space_constraint(a, pltpu.HBM) for a in arrs])
    return dict(send=res[0], recv=res[1], srcs=list(res[2:2 + n_src]), lands=list(res[2 + n_src:2 + n_src + n]),
                token=res[-1], scatter=scatter)


def _split_wait(name, started, after):
    srcs, lands, scatter = started["srcs"], started["lands"], started["scatter"]
    n, n_src = len(lands), len(srcs)

    def body(*refs):
        src_refs = refs[:n_src]
        land_refs = refs[n_src:n_src + n]
        send_sems, recv_sems = refs[n_src + n], refs[n_src + n + 1]
        for _, wait in _chip_copies(src_refs, land_refs, send_sems, recv_sems, scatter):
            wait.wait_send()
            wait.wait_recv()

    arrs = list(srcs) + list(lands)
    res = pl.pallas_call(
        body, name=name, out_shape=tuple(pltpu.HBM(a.shape, a.dtype) for a in arrs),
        in_specs=[HBM_SPEC] * len(arrs) + [SEM_SPEC, SEM_SPEC, pl.BlockSpec(memory_space=pl.ANY)],
        out_specs=tuple([HBM_SPEC] * len(arrs)), input_output_aliases={i: i for i in range(len(arrs))},
        compiler_params=pltpu.CompilerParams(has_side_effects=SPLIT_EFFECT),
    )(*arrs, started["send"], started["recv"], after)
    return list(res[n_src:])


def _flat_tile(rows):
    for cand in (512, 256, 128, 64, 32, 16, 8):
        if rows % cand == 0:
            return cand
    return rows


def _cast_slot(name, a, l, chip):
    r, c = a.shape[0] // DEPTH, a.shape[1]
    tr = _flat_tile(r)

    def body(chip_ref, a_ref, o_ref):
        o_ref[...] = a_ref[...].astype(BF16)

    return pl.pallas_call(
        body, name=name, out_shape=_sds((N_CHIP, r, c), BF16),
        grid_spec=pltpu.PrefetchScalarGridSpec(
            num_scalar_prefetch=1, grid=(r // tr,),
            in_specs=[pl.BlockSpec((tr, c), lambda i, ch: (l * (r // tr) + i, 0))],
            out_specs=pl.BlockSpec((None, tr, c), lambda i, ch: (ch[0], i, 0))),
        compiler_params=_params(("parallel",)))(chip, a)


def _own_slot(name, g, chip):
    n, r, c = g.shape
    tr = _flat_tile(r)

    def body(chip_ref, g_ref, o_ref):
        o_ref[...] = g_ref[...]

    spec = pl.BlockSpec((None, tr, c), lambda i, ch: (ch[0], i, 0))
    return pl.pallas_call(
        body, name=name, out_shape=_sds(g.shape, g.dtype),
        grid_spec=pltpu.PrefetchScalarGridSpec(num_scalar_prefetch=1, grid=(r // tr,), in_specs=[spec], out_specs=spec),
        compiler_params=_params(("parallel",)))(chip, g)


def _sum_layers(name, lands):
    n, r, c = lands[0].shape
    tr = _flat_tile(r)
    nl = len(lands)

    def body(*refs):
        o_ref = refs[-1]
        for k in range(nl):
            @pl.when(pl.program_id(0) == k)
            def _():
                tot = refs[k][0].astype(F32)
                for i in range(1, n):
                    tot = tot + refs[k][i].astype(F32)
                o_ref[...] = tot

    return pl.pallas_call(
        body, name=name, grid=(nl, r // tr),
        in_specs=[pl.BlockSpec((n, tr, c), lambda l, i, k=k: (0, jnp.where(l == k, i, 0), 0)) for k in range(nl)],
        out_specs=pl.BlockSpec((tr, c), lambda l, i: (l * (r // tr) + i, 0)), out_shape=_sds((nl * r, c), F32),
        compiler_params=_params(("arbitrary", "arbitrary")))(*lands)


def _sum_slots(name, a, scale=None):
    n, r, c = a.shape
    tr = _flat_tile(r)

    def body(a_ref, o_ref):
        tot = a_ref[0].astype(F32)
        for i in range(1, n):
            tot = tot + a_ref[i].astype(F32)
        o_ref[...] = tot

    return pl.pallas_call(body, name=name, grid=(r // tr,),
                          in_specs=[pl.BlockSpec((n, tr, c), lambda i: (0, i, 0))],
                          out_specs=pl.BlockSpec((tr, c), lambda i: (i, 0)), out_shape=_sds((r, c), F32),
                          compiler_params=_params(("parallel",)))(a)


def _adamw(name, w, ga, gb, m, v):
    r, c = w.shape
    tr = _flat_tile(r)
    c1 = 1.0 - B1 ** STEP
    c2 = 1.0 - B2 ** STEP
    two = gb is not None

    def body(*refs):
        if two:
            w_ref, ga_ref, gb_ref, m_ref, v_ref, g_out, d_out, m_out, v_out = refs
            g = ga_ref[...] + gb_ref[...]
        else:
            w_ref, ga_ref, m_ref, v_ref, g_out, d_out, m_out, v_out = refs
            g = ga_ref[...]
        mn = B1 * m_ref[...] + (1.0 - B1) * g
        vn = B2 * v_ref[...] + (1.0 - B2) * (g * g)
        g_out[...] = g
        m_out[...] = mn
        v_out[...] = vn
        d_out[...] = -LR * ((mn / c1) / (jnp.sqrt(vn / c2) + ADAM_EPS) + WD * w_ref[...])

    spec = pl.BlockSpec((tr, c), lambda i: (i, 0))
    ins = [w, ga, gb, m, v] if two else [w, ga, m, v]
    return pl.pallas_call(body, name=name, grid=(r // tr,), in_specs=[spec] * len(ins), out_specs=[spec] * 4,
                          out_shape=[_sds((r, c), F32)] * 4, compiler_params=_params(("parallel",)))(*ins)


WEIGHTS = ('mix_norm_w', 'w_in', 'q_norm_w', 'k_norm_w', 'hgrn_lb_logits', 'hgrn_gnorm_w', 'conv_dw_w', 'conv_dw_b',
           'conv_ln_w', 'conv_ln_b', 'conv_pw_w', 'conv_pw_b', 'attn_out_norm_w', 'conv_out_norm_w', 'w_out',
           'ffn_norm_w', 'w_gate', 'w_up', 'w_down')
SHARDED_SMALL = {"hgrn_lb_logits": 2, "conv_dw_w": 2, "conv_pw_w": 1}
LANES = 128
PACK_ROWS = 256


def _pack(parts):
    flat = jnp.concatenate([p.reshape(-1) for p in parts])
    n = flat.shape[0]
    rows = -(-n // (PACK_ROWS * LANES)) * PACK_ROWS
    return jnp.pad(flat, (0, rows * LANES - n)).reshape(rows, LANES)


def _unpack(packed, shapes):
    flat = packed.reshape(-1)
    out, off = [], 0
    for shp in shapes:
        n = int(np.prod(shp))
        out.append(flat[off:off + n].reshape(shp))
        off += n
    return out


def kernel(x, mix_norm_w, w_in, q_norm_w, k_norm_w, hgrn_lb_logits, hgrn_gnorm_w, conv_dw_w, conv_dw_b, conv_ln_w, conv_ln_b, conv_pw_w, conv_pw_b, attn_out_norm_w, conv_out_norm_w, w_out, ffn_norm_w, w_gate, w_up, w_down, loss_target, m_mix_norm_w, m_w_in, m_q_norm_w, m_k_norm_w, m_hgrn_lb_logits, m_hgrn_gnorm_w, m_conv_dw_w, m_conv_dw_b, m_conv_ln_w, m_conv_ln_b, m_conv_pw_w, m_conv_pw_b, m_attn_out_norm_w, m_conv_out_norm_w, m_w_out, m_ffn_norm_w, m_w_gate, m_w_up, m_w_down, v_mix_norm_w, v_w_in, v_q_norm_w, v_k_norm_w, v_hgrn_lb_logits, v_hgrn_gnorm_w, v_conv_dw_w, v_conv_dw_b, v_conv_ln_w, v_conv_ln_b, v_conv_pw_w, v_conv_pw_b, v_attn_out_norm_w, v_conv_out_norm_w, v_w_out, v_ffn_norm_w, v_w_gate, v_w_up, v_w_down):
    w = dict(mix_norm_w=mix_norm_w, w_in=w_in, q_norm_w=q_norm_w, k_norm_w=k_norm_w, hgrn_lb_logits=hgrn_lb_logits,
             hgrn_gnorm_w=hgrn_gnorm_w, conv_dw_w=conv_dw_w, conv_dw_b=conv_dw_b, conv_ln_w=conv_ln_w,
             conv_ln_b=conv_ln_b, conv_pw_w=conv_pw_w, conv_pw_b=conv_pw_b, attn_out_norm_w=attn_out_norm_w,
             conv_out_norm_w=conv_out_norm_w, w_out=w_out, ffn_norm_w=ffn_norm_w, w_gate=w_gate, w_up=w_up,
             w_down=w_down)
    m = dict(mix_norm_w=m_mix_norm_w, w_in=m_w_in, q_norm_w=m_q_norm_w, k_norm_w=m_k_norm_w,
             hgrn_lb_logits=m_hgrn_lb_logits, hgrn_gnorm_w=m_hgrn_gnorm_w, conv_dw_w=m_conv_dw_w,
             conv_dw_b=m_conv_dw_b, conv_ln_w=m_conv_ln_w, conv_ln_b=m_conv_ln_b, conv_pw_w=m_conv_pw_w,
             conv_pw_b=m_conv_pw_b, attn_out_norm_w=m_attn_out_norm_w, conv_out_norm_w=m_conv_out_norm_w,
             w_out=m_w_out, ffn_norm_w=m_ffn_norm_w, w_gate=m_w_gate, w_up=m_w_up, w_down=m_w_down)
    v = dict(mix_norm_w=v_mix_norm_w, w_in=v_w_in, q_norm_w=v_q_norm_w, k_norm_w=v_k_norm_w,
             hgrn_lb_logits=v_hgrn_lb_logits, hgrn_gnorm_w=v_hgrn_gnorm_w, conv_dw_w=v_conv_dw_w,
             conv_dw_b=v_conv_dw_b, conv_ln_w=v_conv_ln_w, conv_ln_b=v_conv_ln_b, conv_pw_w=v_conv_pw_w,
             conv_pw_b=v_conv_pw_b, attn_out_norm_w=v_attn_out_norm_w, conv_out_norm_w=v_conv_out_norm_w,
             w_out=v_w_out, ffn_norm_w=v_ffn_norm_w, w_gate=v_w_gate, w_up=v_w_up, w_down=v_w_down)
    chip = 2 * lax.axis_index("x") + lax.axis_index("y")

    chip1 = chip.reshape(1).astype(jnp.int32)

    flat2 = lambda a: a.reshape(-1, a.shape[-1])
    small_pack = _pack([w[n] for n in SHARDED_SMALL])
    gathered = _exchange("gather_small_weights", [small_pack], "gather4")
    groups = [[(0, "w_in")], [(0, n) for n in BIG_ORDER[1:]], [(1, n) for n in BIG_ORDER]]
    group_of = {key: g for g, keys in enumerate(groups) for key in keys}
    starts = []
    for g, keys in enumerate(groups):
        slots = [_cast_slot("cast_%s_l%d" % (n, l), flat2(w[n]), l, chip1) for l, n in keys]
        starts.append(_split_start("gather_start_g%d" % g, [], slots, False))
    got = {}

    def wget(l, name, after):
        if (l, name) not in got:
            g = group_of[(l, name)]
            for key, arr in zip(groups[g], _split_wait("gather_wait_g%d" % g, starts[g], after)):
                got[key] = arr
        return got[(l, name)]

    pending = []

    def on_grads(l, grads):
        names = [n for n in SCATTER_ORDER if n in grads]
        own = [_own_slot("own_%s_l%d" % (n, l), grads[n], chip1) for n in names]
        st = _split_start("scatter_start_l%d_%s" % (l, names[0]), [grads[n] for n in names], own, True)
        pending.append((l, names, st))
        return [st["token"]]

    parts = [_unpack(gathered[-1][j], [w[n].shape for n in SHARDED_SMALL]) for j in range(N_CHIP)]
    full_small = {n: jnp.concatenate([parts[j][i] for j in range(N_CHIP)], axis=ax)
                  for i, (n, ax) in enumerate(SHARDED_SMALL.items())}
    sm = {n: w[n] for n in WEIGHTS if n not in BIG_ORDER and n not in SHARDED_SMALL}
    sm["conv_dw_w"] = full_small["conv_dw_w"]
    sm["conv_pw_w"] = full_small["conv_pw_w"]
    logits = full_small["hgrn_lb_logits"].reshape(DEPTH * 2, D_HGRN)
    sm["lb"] = _lower_bounds(logits).reshape(DEPTH, 2, D_HGRN)

    sq_sum, grad_x, smalls = _local_step(x, loss_target, wget, sm, [st["token"] for st in starts], on_grads)
    loss = lax.psum(0.5 * sq_sum / D_MODEL, ("x", "y", "c"))

    landed = {}
    for l, names, st in pending:
        for n, arr in zip(names, _split_wait("scatter_wait_l%d_%s" % (l, names[0]), st, grad_x)):
            landed[(l, n)] = arr
    sums = [_sum_layers("sum_" + n, [landed[(l, n)] for l in range(DEPTH)]) for n in SCATTER_ORDER]
    sib = _exchange("sibling_grads", sums, "sibling")
    out = {}
    for n, ga, gb in zip(SCATTER_ORDER, sums, sib):
        res = _adamw("adamw_" + n, flat2(w[n]), ga, gb, flat2(m[n]), flat2(v[n]))
        out[n] = [r.reshape(w[n].shape) for r in res]

    small_names = [n for n in WEIGHTS if n not in SCATTER_ORDER]
    g_pack = _pack([jnp.stack([smalls[l][n] for l in range(DEPTH)]) for n in SMALL_ORDER])
    g_all = _exchange("gather_small_grads", [g_pack], "gather8")[0]
    g_tot = _sum_slots("sum_small", g_all)
    shapes = [(DEPTH,) + tuple(smalls[0][n].shape) for n in SMALL_ORDER]
    g_small = dict(zip(SMALL_ORDER, _unpack(g_tot, shapes)))
    lb_shard = lax.dynamic_slice_in_dim(g_small.pop("lb").reshape(DEPTH * 2, D_HGRN), chip * HEAD_DIM, HEAD_DIM, 1)
    g_small["hgrn_lb_logits"] = _lower_bounds_bwd(hgrn_lb_logits.reshape(DEPTH * 2, HEAD_DIM), lb_shard).reshape(
        hgrn_lb_logits.shape)
    g_small["conv_dw_w"] = lax.dynamic_slice_in_dim(g_small["conv_dw_w"], chip * HEAD_DIM, HEAD_DIM, 2)
    res = _adamw("adamw_small", _pack([w[n] for n in small_names]), _pack([g_small[n] for n in small_names]), None,
                 _pack([m[n] for n in small_names]), _pack([v[n] for n in small_names]))
    unpacked = [_unpack(r, [w[n].shape for n in small_names]) for r in res]
    for i, n in enumerate(small_names):
        out[n] = [unpacked[k][i] for k in range(4)]

    return (loss, grad_x, *[out[n][0] for n in WEIGHTS], *[out[n][1] for n in WEIGHTS],
            *[out[n][2] for n in WEIGHTS], *[out[n][3] for n in WEIGHTS])
```

```python
import functools

import numpy as np
import jax
import jax.numpy as jnp
from jax import lax
from jax.experimental import pallas as pl
from jax.experimental.pallas import tpu as pltpu

F32, BF16 = jnp.float32, jnp.bfloat16

D_MODEL = 1024
DEPTH = 2
GRID_W = 64
D_ATTN, D_HGRN, D_CONV = 512, 256, 256
HEAD_DIM = 64
N_KV = 2
KV_LANES = D_ATTN // N_KV
ROPE_THETA = 10000.0
F_MIN = 1e-6
CONV_W = 31
CONV_PAD = 15
D_FF = 2816
D_IN = 2560
N_CHIP = 4
N_DEV = 8
IN_BLK = D_IN // N_CHIP
FF_BLK = D_FF // N_CHIP
OUT_BLK = D_MODEL // N_CHIP
EPS = 1e-6
LN_EPS = 1e-5
LR, B1, B2, ADAM_EPS, WD, STEP = 0.001, 0.9, 0.999, 1e-08, 0.01, 10
CHUNK = 16
HBLK = 256
CONV_TILE = 128
VMEM_LIMIT = 56 * 1024 * 1024

COL_Q, COL_K, COL_V = 0, 4, 5
COL_HQ, COL_FF, COL_FB, COL_HI, COL_HG, COL_CA, COL_CB = 3, 4, 5, 6, 7, 8, 9


def _params(sem=None):
    return pltpu.CompilerParams(dimension_semantics=sem, vmem_limit_bytes=VMEM_LIMIT)


def _sds(shape, dtype):
    return jax.ShapeDtypeStruct(tuple(shape), dtype)


def _full(shape):
    n = len(shape)
    return pl.BlockSpec(tuple(shape), lambda *_: (0,) * n)


def _sigmoid(x):
    return 0.5 * jnp.tanh(0.5 * x) + 0.5


def _gate_sigmoid(x):
    return 1.0 / (1.0 + jnp.exp(-x))


def _silu(x):
    return x * _sigmoid(x)


def _dsilu(x):
    s = _sigmoid(x)
    return s * (1.0 + x * (1.0 - s))


def _rowgroups(v):
    m, c = v.shape
    return v.reshape(m // 8, 8, c).sum(axis=0)


def _split2(x):
    hi = x.astype(BF16)
    lo = (x - hi.astype(F32)).astype(BF16)
    return hi, lo


def _rdot2(x, m):
    hi, lo = _split2(x)
    return (jnp.dot(hi, m, preferred_element_type=F32) + jnp.dot(lo, m, preferred_element_type=F32))


def _ldot3(m, x):
    hi = x.astype(BF16)
    r1 = x - hi.astype(F32)
    mid = r1.astype(BF16)
    lo = (r1 - mid.astype(F32)).astype(BF16)
    return (jnp.dot(m, hi, preferred_element_type=F32) + jnp.dot(m, mid, preferred_element_type=F32)
            + jnp.dot(m, lo, preferred_element_type=F32))


def _dot_nt(a, b):
    return lax.dot_general(a, b, (((1,), (1,)), ((), ())), preferred_element_type=F32)


def _dot_tn(a, b):
    return lax.dot_general(a, b, (((0,), (0,)), ((), ())), preferred_element_type=F32)


def _seg_matrix(n, seg, val):
    i = np.arange(n)
    return ((i[:, None] // seg) == (i[None, :] // seg)).astype(np.float32) * val


def _rot_matrix(n):
    r = np.zeros((n, n), np.float32)
    for i in range(n):
        if (i % 32) < 16:
            r[i + 16, i] = -1.0
        else:
            r[i - 16, i] = 1.0
    return r


def _rep_matrix():
    r = np.zeros((N_KV * HEAD_DIM, D_ATTN), np.float32)
    for kv in range(N_KV):
        for g in range(KV_LANES // HEAD_DIM):
            for d in range(HEAD_DIM):
                r[HEAD_DIM * kv + d, KV_LANES * kv + HEAD_DIM * g + d] = 1.0
    return r


def _cumsum_matrix(rev):
    i = np.arange(HBLK)
    same = (i[:, None] // CHUNK) == (i[None, :] // CHUNK)
    tri = (i[None, :] >= i[:, None]) if rev else (i[None, :] <= i[:, None])
    return (same & tri).astype(np.float32)


def _sel_matrices():
    sel = np.zeros((CHUNK, CHUNK * CHUNK), np.float32)
    selt = np.zeros((CHUNK, CHUNK * CHUNK), np.float32)
    for t in range(CHUNK):
        for s in range(CHUNK):
            sel[t, t * CHUNK + s] = 1.0
            selt[s, t * CHUNK + s] = 1.0
    return sel, selt


def _bf(a):
    return jnp.asarray(a, dtype=BF16)


def _mm(name, grid, pairs, extras, outs, epilogue, acc=None, sem=None):
    n_p, n_e, n_o = len(pairs), len(extras), len(outs)

    def body(*refs):
        ab = refs[:2 * n_p]
        ex = refs[2 * n_p:2 * n_p + n_e]
        out = refs[2 * n_p + n_e:2 * n_p + n_e + n_o]
        scr = refs[2 * n_p + n_e + n_o:]
        tot = None
        for i in range(n_p):
            a = ab[2 * i][...]
            b = ab[2 * i + 1][...]
            if a.ndim == 3:
                a = a.reshape(-1, a.shape[-1])
            if b.ndim == 3:
                b = b.reshape(-1, b.shape[-1])
            r = lax.dot_general(a.astype(BF16), b.astype(BF16), pairs[i][4], preferred_element_type=F32)
            tot = r if tot is None else tot + r

        def finish(total):
            res = epilogue(total, *[e[...] for e in ex])
            for o_ref, val in zip(out, res):
                o_ref[...] = val.astype(o_ref.dtype)

        if acc is None:
            finish(tot)
        else:
            k = pl.program_id(acc[0])

            @pl.when(k == 0)
            def _():
                scr[0][...] = tot

            @pl.when(k > 0)
            def _():
                scr[0][...] += tot

            @pl.when(k == grid[acc[0]] - 1)
            def _():
                finish(scr[0][...])

    args, in_specs = [], []
    for a, a_spec, b, b_spec, _ in pairs:
        args += [a, b]
        in_specs += [a_spec, b_spec]
    for e, e_spec in extras:
        args.append(e)
        in_specs.append(e_spec)
    if sem is None:
        sem = tuple("arbitrary" if (acc is not None and i == acc[0]) else "parallel" for i in range(len(grid)))
    return pl.pallas_call(
        body, name=name, grid=grid, in_specs=in_specs,
        out_specs=[o[1] for o in outs], out_shape=[o[0] for o in outs],
        scratch_shapes=[] if acc is None else [pltpu.VMEM(acc[1], F32)],
        compiler_params=_params(sem),
    )(*args)


NN = (((1,), (0,)), ((), ()))
NT = (((1,), (1,)), ((), ()))
TN = (((0,), (0,)), ((), ()))


def _row_tile(t):
    return min(256, t)


def _rms_fwd(name, x, w, deps=()):
    t, d = x.shape
    tm = _row_tile(t)

    def body(x_ref, w_ref, *rest):
        o_ref = rest[-1]
        xv = x_ref[...]
        r = lax.rsqrt(jnp.mean(xv * xv, axis=-1, keepdims=True) + EPS)
        o_ref[...] = (xv * r * w_ref[...]).astype(BF16)

    return pl.pallas_call(
        body, name=name, grid=(t // tm,),
        in_specs=[pl.BlockSpec((tm, d), lambda i: (i, 0)), _full((1, d))] + [_full(a.shape) for a in deps],
        out_specs=pl.BlockSpec((tm, d), lambda i: (i, 0)), out_shape=_sds((t, d), BF16),
        compiler_params=_params(("parallel",)),
    )(x, w, *deps)


def _rms_bwd(name, x, w, dh, dres, deps=()):
    t, d = x.shape
    tm = _row_tile(t)

    def body(x_ref, w_ref, dh_ref, dres_ref, *rest):
        dx_ref, dw_ref = rest[-2:]
        xv = x_ref[...]
        r = lax.rsqrt(jnp.mean(xv * xv, axis=-1, keepdims=True) + EPS)
        dy = dh_ref[...]
        gw = dy * w_ref[...]
        dx_ref[...] = dres_ref[...] + r * gw - xv * (r * r * r) * jnp.mean(gw * xv, axis=-1, keepdims=True)

        @pl.when(pl.program_id(0) == 0)
        def _():
            dw_ref[...] = jnp.zeros_like(dw_ref)

        dw_ref[...] += _rowgroups(dy * xv * r)

    tile = pl.BlockSpec((tm, d), lambda i: (i, 0))
    return pl.pallas_call(
        body, name=name, grid=(t // tm,),
        in_specs=[tile, _full((1, d)), tile, tile] + [_full(a.shape) for a in deps],
        out_specs=[tile, _full((8, d))], out_shape=[_sds((t, d), F32), _sds((8, d), F32)],
        compiler_params=_params(("arbitrary",)),
    )(x, w, dh, dres, *deps)


def _loss_kernel(y, target):
    t, d = y.shape
    tm = _row_tile(t)

    def body(y_ref, t_ref, dy_ref, acc_ref):
        e = y_ref[...] - t_ref[...]
        dy_ref[...] = e * (1.0 / d)

        @pl.when(pl.program_id(0) == 0)
        def _():
            acc_ref[...] = jnp.zeros_like(acc_ref)

        acc_ref[...] += _rowgroups(e * e)

    tile = pl.BlockSpec((tm, d), lambda i: (i, 0))
    return pl.pallas_call(
        body, name="loss_head", grid=(t // tm,), in_specs=[tile, tile],
        out_specs=[tile, _full((8, d))], out_shape=[_sds((t, d), F32), _sds((8, d), F32)],
        compiler_params=_params(("arbitrary",)),
    )(y, target)


def _rope_tables(s):
    rows = s // GRID_W
    row_id = jnp.repeat(jnp.arange(rows, dtype=F32), GRID_W)
    col_id = jnp.tile(jnp.arange(GRID_W, dtype=F32), rows)
    half = HEAD_DIM // 2
    inv_freq = ROPE_THETA ** (-jnp.arange(0, half, 2, dtype=F32) / half)
    ang_r = row_id[:, None] * inv_freq[None, :]
    ang_c = col_id[:, None] * inv_freq[None, :]
    ang = jnp.concatenate([ang_r, ang_r, ang_c, ang_c], axis=-1)
    return jnp.cos(ang).astype(F32), jnp.sin(ang).astype(F32)


def _attn_consts():
    return dict(
        seg_q=_bf(_seg_matrix(D_ATTN, HEAD_DIM, 1.0 / HEAD_DIM)),
        seg_k=_bf(_seg_matrix(N_KV * HEAD_DIM, HEAD_DIM, 1.0 / HEAD_DIM)),
        rot_q=_bf(_rot_matrix(D_ATTN)), rot_k=_bf(_rot_matrix(N_KV * HEAD_DIM)),
        rep=_bf(_rep_matrix()), rep_t=_bf(_rep_matrix().T))


def _attn_prep(name, proj, s, tabs, qw, kw, ac):
    t = proj.shape[0]
    tm = _row_tile(s)
    nst = s // tm
    kw_ = N_KV * HEAD_DIM

    def body(q_ref, k_ref, v_ref, cq_ref, sq_ref, ck_ref, sk_ref, qw_ref, kw_ref,
             segq_ref, segk_ref, rotq_ref, rotk_ref, rep_ref, qn_ref, kr_ref, vr_ref):
        q = q_ref[...]
        r = lax.rsqrt(jnp.dot((q * q).astype(BF16), segq_ref[...], preferred_element_type=F32) + EPS)
        qn = q * r * qw_ref[...]
        qr = qn * cq_ref[...] + _rdot2(qn, rotq_ref[...]) * sq_ref[...]
        qn_ref[...] = (qr * (HEAD_DIM ** -0.5)).astype(BF16)
        k = k_ref[...]
        rk = lax.rsqrt(jnp.dot((k * k).astype(BF16), segk_ref[...], preferred_element_type=F32) + EPS)
        kn = k * rk * kw_ref[...]
        kr = kn * ck_ref[...] + _rdot2(kn, rotk_ref[...]) * sk_ref[...]
        kr_ref[...] = jnp.dot(kr.astype(BF16), rep_ref[...], preferred_element_type=F32).astype(BF16)
        vr_ref[...] = jnp.dot(v_ref[...].astype(BF16), rep_ref[...], preferred_element_type=F32).astype(BF16)

    wide = pl.BlockSpec((tm, D_ATTN), lambda i: (i, 0))
    tabq = pl.BlockSpec((tm, D_ATTN), lambda i: (i % nst, 0))
    tabk = pl.BlockSpec((tm, kw_), lambda i: (i % nst, 0))
    return pl.pallas_call(
        body, name=name, grid=(t // tm,),
        in_specs=[pl.BlockSpec((tm, D_ATTN), lambda i: (i, COL_Q)), pl.BlockSpec((tm, kw_), lambda i: (i, COL_K)),
                  pl.BlockSpec((tm, kw_), lambda i: (i, COL_V)), tabq, tabq, tabk, tabk,
                  _full((1, D_ATTN)), _full((1, kw_)), _full((D_ATTN, D_ATTN)), _full((kw_, kw_)),
                  _full((D_ATTN, D_ATTN)), _full((kw_, kw_)), _full((kw_, D_ATTN))],
        out_specs=[wide, wide, wide], out_shape=[_sds((t, D_ATTN), BF16)] * 3,
        compiler_params=_params(("parallel",)),
    )(proj, proj, proj, tabs["cq"], tabs["sq"], tabs["ck"], tabs["sk"], qw, kw,
      ac["seg_q"], ac["seg_k"], ac["rot_q"], ac["rot_k"], ac["rep"])


def _attn_prep_bwd(name, proj, s, tabs, qw, kw, ac, dqs, dkr, dvr):
    t = proj.shape[0]
    tm = _row_tile(s)
    nst = s // tm
    kw_ = N_KV * HEAD_DIM
    wout = D_ATTN + 2 * kw_

    def norm_rope_bwd(x, w, cos, sin, seg, rot, d_roped):
        dn = d_roped * cos - _rdot2(d_roped * sin, rot)
        r = lax.rsqrt(jnp.dot((x * x).astype(BF16), seg, preferred_element_type=F32) + EPS)
        gw = dn * w
        dx = r * gw - x * (r * r * r) * _rdot2(gw * x, seg)
        return dx, _rowgroups(dn * x * r)

    def body(q_ref, k_ref, cq_ref, sq_ref, ck_ref, sk_ref, qw_ref, kw_ref, segq_ref, segk_ref, rotq_ref, rotk_ref,
             rept_ref, dqs_ref, dkr_ref, dvr_ref, dp_ref, dqw_ref, dkw_ref):
        dq, dqw = norm_rope_bwd(q_ref[...], qw_ref[...], cq_ref[...], sq_ref[...], segq_ref[...], rotq_ref[...],
                                dqs_ref[...] * (HEAD_DIM ** -0.5))
        dk_roped = _rdot2(dkr_ref[...], rept_ref[...])
        dk, dkw = norm_rope_bwd(k_ref[...], kw_ref[...], ck_ref[...], sk_ref[...], segk_ref[...], rotk_ref[...],
                                dk_roped)
        dv = _rdot2(dvr_ref[...], rept_ref[...])
        dp_ref[:, 0:D_ATTN] = dq.astype(BF16)
        dp_ref[:, D_ATTN:D_ATTN + kw_] = dk.astype(BF16)
        dp_ref[:, D_ATTN + kw_:wout] = dv.astype(BF16)

        @pl.when(pl.program_id(0) == 0)
        def _():
            dqw_ref[...] = jnp.zeros_like(dqw_ref)
            dkw_ref[...] = jnp.zeros_like(dkw_ref)

        dqw_ref[...] += dqw
        dkw_ref[...] += dkw

    wide = pl.BlockSpec((tm, D_ATTN), lambda i: (i, 0))
    tabq = pl.BlockSpec((tm, D_ATTN), lambda i: (i % nst, 0))
    tabk = pl.BlockSpec((tm, kw_), lambda i: (i % nst, 0))
    return pl.pallas_call(
        body, name=name, grid=(t // tm,),
        in_specs=[pl.BlockSpec((tm, D_ATTN), lambda i: (i, COL_Q)), pl.BlockSpec((tm, kw_), lambda i: (i, COL_K)),
                  tabq, tabq, tabk, tabk, _full((1, D_ATTN)), _full((1, kw_)),
                  _full((D_ATTN, D_ATTN)), _full((kw_, kw_)), _full((D_ATTN, D_ATTN)), _full((kw_, kw_)),
                  _full((D_ATTN, kw_)), wide, wide, wide],
        out_specs=[pl.BlockSpec((tm, wout), lambda i: (i, 0)), _full((8, D_ATTN)), _full((8, kw_))],
        out_shape=[_sds((t, wout), BF16), _sds((8, D_ATTN), F32), _sds((8, kw_), F32)],
        compiler_params=_params(("arbitrary",)),
    )(proj, proj, tabs["cq"], tabs["sq"], tabs["ck"], tabs["sk"], qw, kw,
      ac["seg_q"], ac["seg_k"], ac["rot_q"], ac["rot_k"], ac["rep_t"], dqs, dkr, dvr)


def _attn_tile(s):
    return min(256, s)


def _head_masks(shape):
    lane = lax.broadcasted_iota(jnp.int32, shape, 1)
    return [(lane // HEAD_DIM) == g for g in range(KV_LANES // HEAD_DIM)]


def _attn_fwd(name, qn, kr, vr, b, s):
    t = qn.shape[0]
    tq = _attn_tile(s)
    nq = s // tq

    def body(q_ref, k_ref, v_ref, o_ref):
        q = q_ref[...]
        k = k_ref[...]
        v = v_ref[...]
        acc = jnp.zeros((tq, KV_LANES), F32)
        for mask in _head_masks((tq, KV_LANES)):
            sc = _dot_nt(jnp.where(mask, q, jnp.zeros_like(q)), k)
            p = jnp.exp(sc - jnp.max(sc, axis=-1, keepdims=True))
            inv = 1.0 / jnp.sum(p, axis=-1, keepdims=True)
            og = jnp.dot(p.astype(BF16), v, preferred_element_type=F32) * inv
            acc = jnp.where(mask, og, acc)
        o_ref[...] = acc

    return pl.pallas_call(
        body, name=name, grid=(b, N_KV, nq),
        in_specs=[pl.BlockSpec((tq, KV_LANES), lambda bi, kv, i: (bi * nq + i, kv)),
                  pl.BlockSpec((s, KV_LANES), lambda bi, kv, i: (bi, kv)),
                  pl.BlockSpec((s, KV_LANES), lambda bi, kv, i: (bi, kv))],
        out_specs=pl.BlockSpec((tq, KV_LANES), lambda bi, kv, i: (bi * nq + i, kv)),
        out_shape=_sds((t, D_ATTN), F32),
        compiler_params=_params(("parallel", "parallel", "parallel")),
    )(qn, kr, vr)


def _attn_bwd(name, qn, kr, vr, do, b, s):
    t = qn.shape[0]
    tq = _attn_tile(s)
    nq = s // tq

    def body(q_ref, k_ref, v_ref, do_ref, dq_ref, dk_ref, dv_ref):
        @pl.when(pl.program_id(2) == 0)
        def _():
            dk_ref[...] = jnp.zeros_like(dk_ref)
            dv_ref[...] = jnp.zeros_like(dv_ref)

        q = q_ref[...]
        k = k_ref[...]
        v = v_ref[...]
        dout = do_ref[...].astype(BF16)
        masks = _head_masks((tq, KV_LANES))
        q4 = jnp.concatenate([jnp.where(m, q, jnp.zeros_like(q)) for m in masks], axis=0)
        do4 = jnp.concatenate([jnp.where(m, dout, jnp.zeros_like(dout)) for m in masks], axis=0)
        sc = _dot_nt(q4, k)
        p = jnp.exp(sc - jnp.max(sc, axis=-1, keepdims=True))
        p = p * (1.0 / jnp.sum(p, axis=-1, keepdims=True))
        dp = _dot_nt(do4, v)
        ds = (p * (dp - jnp.sum(p * dp, axis=-1, keepdims=True))).astype(BF16)
        dq4 = jnp.dot(ds, k, preferred_element_type=F32)
        dq = jnp.zeros((tq, KV_LANES), F32)
        for g, m in enumerate(masks):
            dq = jnp.where(m, dq4[g * tq:(g + 1) * tq, :], dq)
        dq_ref[...] = dq
        dk_ref[...] += _dot_tn(ds, q4)
        dv_ref[...] += _dot_tn(p.astype(BF16), do4)

    qspec = pl.BlockSpec((tq, KV_LANES), lambda bi, kv, i: (bi * nq + i, kv))
    kspec = pl.BlockSpec((s, KV_LANES), lambda bi, kv, i: (bi, kv))
    return pl.pallas_call(
        body, name=name, grid=(b, N_KV, nq),
        in_specs=[qspec, kspec, kspec, qspec],
        out_specs=[qspec, kspec, kspec], out_shape=[_sds((t, D_ATTN), F32)] * 3,
        compiler_params=_params(("parallel", "parallel", "arbitrary")),
    )(qn, kr, vr, do)


def _hgrn_consts(rev):
    sel, selt = _sel_matrices()
    cs = _cumsum_matrix(rev)
    return dict(cs=_bf(cs), cs_t=_bf(cs.T), seg=_bf(_seg_matrix(D_HGRN, HEAD_DIM, 1.0)),
                bd=jnp.asarray(_seg_matrix(D_HGRN, HEAD_DIM, 1.0), F32),
                sel=_bf(sel), selt=_bf(selt), seld=_bf(sel - selt))


def _gates(z, lb):
    sig = _gate_sigmoid(z)
    f = lb + (1.0 - lb) * sig
    g = jnp.log(jnp.maximum(f, F_MIN))
    sn = _gate_sigmoid(-z)
    return sig, f, g, sn, (1.0 - lb) * sn


def _pair_decay(b, rev):
    row = lax.broadcasted_iota(jnp.int32, (CHUNK, D_HGRN), 0)
    parts = []
    for t in range(CHUNK):
        m = (row >= t) if rev else (row <= t)
        parts.append(jnp.where(m, jnp.exp(jnp.minimum(b[t:t + 1, :] - b, 0.0)), 0.0))
    return jnp.concatenate(parts, axis=0)


def _rows_rep(a):
    return jnp.concatenate([jnp.broadcast_to(a[t:t + 1, :], a.shape) for t in range(CHUNK)], axis=0)


def _tile_rows(a):
    return jnp.concatenate([a] * CHUNK, axis=0)


def _hgrn_specs(b, s, rev):
    nb = s // HBLK

    def blk(j):
        return (nb - 1 - j) if rev else j

    def col(c):
        return pl.BlockSpec((HBLK, D_HGRN), lambda bi, j: (bi * nb + blk(j), c))

    return nb, blk, col


def _hgrn_fwd(name, proj, lb, b, s, rev, hc):
    t = proj.shape[0]
    nb, blk, col = _hgrn_specs(b, s, rev)
    n_ch = HBLK // CHUNK
    last = 0 if rev else CHUNK - 1

    def body(q_ref, z_ref, v_ref, lb_ref, cs_ref, seg_ref, bd_ref, sel_ref, o_ref, st_ref, state, b_scr, k_scr):
        @pl.when(pl.program_id(1) == 0)
        def _():
            state[...] = jnp.zeros_like(state)

        st_ref[...] = state[...]
        _, _, g, _, kk = _gates(z_ref[...], lb_ref[...])
        k_scr[...] = kk
        b_scr[...] = _ldot3(cs_ref[...], g)

        def chunk(i, carry):
            c = (n_ch - 1 - i) if rev else i
            rows = pl.ds(pl.multiple_of(c * CHUNK, CHUNK), CHUNK)
            q = q_ref[rows, :]
            k = k_scr[rows, :]
            v = v_ref[rows, :]
            bb = b_scr[rows, :]
            bl = bb[last:last + 1, :]
            pairs = _pair_decay(bb, rev) * _rows_rep(q) * _tile_rows(k)
            a = jnp.dot(pairs.astype(BF16), seg_ref[...], preferred_element_type=F32)
            o_intra = jnp.dot(sel_ref[...], (a * _tile_rows(v)).astype(BF16), preferred_element_type=F32)
            st = state[...]
            o_inter = _dot_nt((q * jnp.exp(bb)).astype(BF16), st.astype(BF16))
            o_ref[rows, :] = o_intra + o_inter
            ke = k * jnp.exp(bl - bb)
            state[...] = st * jnp.exp(bl) + bd_ref[...] * _dot_tn(v.astype(BF16), ke.astype(BF16))
            return carry

        lax.fori_loop(0, n_ch, chunk, 0)

    sq = (D_HGRN, D_HGRN)
    return pl.pallas_call(
        body, name=name, grid=(b, nb),
        in_specs=[col(COL_HQ), col(COL_FB if rev else COL_FF), col(COL_HI), _full((1, D_HGRN)),
                  _full((HBLK, HBLK)), _full(sq), _full(sq), _full((CHUNK, CHUNK * CHUNK))],
        out_specs=[pl.BlockSpec((HBLK, D_HGRN), lambda bi, j: (bi * nb + blk(j), 0)),
                   pl.BlockSpec((None,) + sq, lambda bi, j: (bi * nb + blk(j), 0, 0))],
        out_shape=[_sds((t, D_HGRN), F32), _sds((b * nb,) + sq, F32)],
        scratch_shapes=[pltpu.VMEM(sq, F32), pltpu.VMEM((HBLK, D_HGRN), F32), pltpu.VMEM((HBLK, D_HGRN), F32)],
        compiler_params=_params(("parallel", "arbitrary")),
    )(proj, proj, proj, lb, hc["cs"], hc["seg"], hc["bd"], hc["sel"])


def _hgrn_bwd(name, proj, lb, st_blk, do, dq_prev, dv_prev, b, s, rev, hc):
    t = proj.shape[0]
    nb = s // HBLK
    n_ch = HBLK // CHUNK
    last = 0 if rev else CHUNK - 1

    def blk(j):
        return j if rev else (nb - 1 - j)

    def col(c):
        return pl.BlockSpec((HBLK, D_HGRN), lambda bi, j: (bi * nb + blk(j), c))

    def body(q_ref, z_ref, v_ref, lb_ref, st_ref, do_ref, dqp_ref, dvp_ref, cs_ref, cst_ref, seg_ref, bd_ref,
             sel_ref, selt_ref, seld_ref, dq_ref, dv_ref, dz_ref, dlb_ref,
             dstate, states, b_scr, k_scr, db_scr, dk_scr):
        first = jnp.logical_and(pl.program_id(0) == 0, pl.program_id(1) == 0)

        @pl.when(first)
        def _():
            dlb_ref[...] = jnp.zeros_like(dlb_ref)

        @pl.when(pl.program_id(1) == 0)
        def _():
            dstate[...] = jnp.zeros_like(dstate)

        lbv = lb_ref[...]
        z = z_ref[...]
        sig, f, g, sn, kk = _gates(z, lbv)
        k_scr[...] = kk
        b_scr[...] = _ldot3(cs_ref[...], g)

        def rows_of(c):
            return pl.ds(pl.multiple_of(c * CHUNK, CHUNK), CHUNK)

        def replay(i, st):
            c = (n_ch - 1 - i) if rev else i
            rows = rows_of(c)
            states[c] = st
            bb = b_scr[rows, :]
            bl = bb[last:last + 1, :]
            ke = k_scr[rows, :] * jnp.exp(bl - bb)
            return st * jnp.exp(bl) + bd_ref[...] * _dot_tn(v_ref[rows, :].astype(BF16), ke.astype(BF16))

        lax.fori_loop(0, n_ch, replay, st_ref[...])
        row = lax.broadcasted_iota(jnp.int32, (CHUNK, D_HGRN), 0)

        def chunk(i, carry):
            c = i if rev else (n_ch - 1 - i)
            rows = rows_of(c)
            q = q_ref[rows, :]
            k = k_scr[rows, :]
            v = v_ref[rows, :]
            bb = b_scr[rows, :]
            dout = do_ref[rows, :]
            bl = bb[last:last + 1, :]
            st_p = states[c]
            dst_n = dstate[...]
            eb = jnp.exp(bb)
            ebl = jnp.exp(bl - bb)
            ebl_last = jnp.exp(bl)
            qe = q * eb
            ke = k * ebl
            dob = dout.astype(BF16)
            dstb = dst_n.astype(BF16)
            dqe = jnp.dot(dob, st_p.astype(BF16), preferred_element_type=F32)
            dke = jnp.dot(v.astype(BF16), dstb, preferred_element_type=F32)
            dv = _dot_nt(ke.astype(BF16), dstb)
            dbl = jnp.sum(dst_n * st_p, axis=0, keepdims=True) * ebl_last + jnp.sum(dke * ke, axis=0, keepdims=True)
            dq = dqe * eb
            dk = dke * ebl
            db = dqe * qe - dke * ke
            dec = _pair_decay(bb, rev)
            q_rep = _rows_rep(q)
            k_til = _tile_rows(k)
            do_rep = _rows_rep(dout)
            pairs = dec * q_rep * k_til
            a = jnp.dot(pairs.astype(BF16), seg_ref[...], preferred_element_type=F32)
            wb = jnp.dot((_tile_rows(v) * do_rep).astype(BF16), seg_ref[...], preferred_element_type=F32)
            gdec = wb * dec
            dq = dq + jnp.dot(sel_ref[...], (gdec * k_til).astype(BF16), preferred_element_type=F32)
            dk = dk + jnp.dot(selt_ref[...], (gdec * q_rep).astype(BF16), preferred_element_type=F32)
            dv = dv + jnp.dot(selt_ref[...], (a * do_rep).astype(BF16), preferred_element_type=F32)
            db = db + jnp.dot(seld_ref[...], (wb * pairs).astype(BF16), preferred_element_type=F32)
            db = db + jnp.where(row == last, dbl, 0.0)
            dq_ref[rows, :] = dq + dqp_ref[rows, :]
            dv_ref[rows, :] = dv + dvp_ref[rows, :]
            dk_scr[rows, :] = dk
            db_scr[rows, :] = db
            dstate[...] = dst_n * ebl_last + bd_ref[...] * _dot_tn(dob, qe.astype(BF16))
            return carry

        lax.fori_loop(0, n_ch, chunk, 0)
        hi, lo = _split2(db_scr[...])
        dg = (jnp.dot(cst_ref[...], hi, preferred_element_type=F32)
              + jnp.dot(cst_ref[...], lo, preferred_element_type=F32))
        dgf = jnp.where(f > F_MIN, dg / f, 0.0)
        dk = dk_scr[...]
        dz_ref[...] = dgf * (1.0 - lbv) * sig * (1.0 - sig) - dk * (1.0 - lbv) * sn * (1.0 - sn)
        dlb_ref[...] += _rowgroups(dgf * (1.0 - sig) - dk * sn)

    sq = (D_HGRN, D_HGRN)
    blk0 = pl.BlockSpec((HBLK, D_HGRN), lambda bi, j: (bi * nb + blk(j), 0))
    pairs_shape = (CHUNK, CHUNK * CHUNK)
    return pl.pallas_call(
        body, name=name, grid=(b, nb),
        in_specs=[col(COL_HQ), col(COL_FB if rev else COL_FF), col(COL_HI), _full((1, D_HGRN)),
                  pl.BlockSpec((None,) + sq, lambda bi, j: (bi * nb + blk(j), 0, 0)), blk0, blk0, blk0,
                  _full((HBLK, HBLK)), _full((HBLK, HBLK)), _full(sq), _full(sq),
                  _full(pairs_shape), _full(pairs_shape), _full(pairs_shape)],
        out_specs=[blk0, blk0, blk0, _full((8, D_HGRN))],
        out_shape=[_sds((t, D_HGRN), F32)] * 3 + [_sds((8, D_HGRN), F32)],
        scratch_shapes=[pltpu.VMEM(sq, F32), pltpu.VMEM((n_ch,) + sq, F32)] + [pltpu.VMEM((HBLK, D_HGRN), F32)] * 4,
        compiler_params=_params(("arbitrary", "arbitrary")),
    )(proj, proj, proj, lb, st_blk, do, dq_prev, dv_prev,
      hc["cs"], hc["cs_t"], hc["seg"], hc["bd"], hc["sel"], hc["selt"], hc["seld"])


def _scan_chunk_fwd(c, rev, q_ref, v_ref, k_scr, b_scr, state, o_ref, seg_ref, bd_ref, sel_ref):
    last = 0 if rev else CHUNK - 1
    rows = pl.ds(pl.multiple_of(c * CHUNK, CHUNK), CHUNK)
    q = q_ref[rows, :]
    k = k_scr[rows, :]
    v = v_ref[rows, :]
    bb = b_scr[rows, :]
    bl = bb[last:last + 1, :]
    pairs = _pair_decay(bb, rev) * _rows_rep(q) * _tile_rows(k)
    a = jnp.dot(pairs.astype(BF16), seg_ref[...], preferred_element_type=F32)
    o_intra = jnp.dot(sel_ref[...], (a * _tile_rows(v)).astype(BF16), preferred_element_type=F32)
    st = state[...]
    o_inter = _dot_nt((q * jnp.exp(bb)).astype(BF16), st.astype(BF16))
    o_ref[rows, :] = o_intra + o_inter
    ke = k * jnp.exp(bl - bb)
    state[...] = st * jnp.exp(bl) + bd_ref[...] * _dot_tn(v.astype(BF16), ke.astype(BF16))


def _hgrn_fwd2(name, proj, lb_f, lb_b, b, s, hc_f, hc_b):
    t = proj.shape[0]
    nb = s // HBLK
    n_ch = HBLK // CHUNK
    n_chain = 2 * b

    def body(qf_ref, zf_ref, vf_ref, qb_ref, zb_ref, vb_ref, lbf_ref, lbb_ref, csf_ref, csb_ref, seg_ref, bd_ref,
             sel_ref, of_ref, ob_ref, stf_ref, stb_ref, state, b_scr, k_scr):
        @pl.when(pl.program_id(0) == 0)
        def _():
            state[...] = jnp.zeros_like(state)

        chains = []
        for bi in range(b):
            chains.append((False, qf_ref.at[bi], zf_ref.at[bi], vf_ref.at[bi], lbf_ref, csf_ref, of_ref.at[bi],
                           stf_ref.at[bi], 2 * bi))
            chains.append((True, qb_ref.at[bi], zb_ref.at[bi], vb_ref.at[bi], lbb_ref, csb_ref, ob_ref.at[bi],
                           stb_ref.at[bi], 2 * bi + 1))
        for rev, q, z, v, lb, cs, o, st, ci in chains:
            st[...] = state[ci]
            _, _, g, _, kk = _gates(z[...], lb[...])
            k_scr[ci] = kk
            b_scr[ci] = _ldot3(cs[...], g)

        def chunk(i, carry):
            for rev, q, z, v, lb, cs, o, st, ci in chains:
                _scan_chunk_fwd((n_ch - 1 - i) if rev else i, rev, q, v, k_scr.at[ci], b_scr.at[ci], state.at[ci], o,
                                seg_ref, bd_ref, sel_ref)
            return carry

        lax.fori_loop(0, n_ch, chunk, 0)

    def col(c, rev):
        return pl.BlockSpec((b, HBLK, D_HGRN), lambda j: (0, (nb - 1 - j) if rev else j, c))

    def st_spec(rev):
        return pl.BlockSpec((b, None, D_HGRN, D_HGRN), lambda j: (0, (nb - 1 - j) if rev else j, 0, 0))

    sq = (D_HGRN, D_HGRN)
    proj3 = proj.reshape(b, s, proj.shape[1])
    o_fw, o_bw, st_fw, st_bw = pl.pallas_call(
        body, name=name, grid=(nb,),
        in_specs=[col(COL_HQ, False), col(COL_FF, False), col(COL_HI, False),
                  col(COL_HQ, True), col(COL_FB, True), col(COL_HI, True),
                  _full((1, D_HGRN)), _full((1, D_HGRN)), _full((HBLK, HBLK)), _full((HBLK, HBLK)),
                  _full(sq), _full(sq), _full((CHUNK, CHUNK * CHUNK))],
        out_specs=[col(0, False), col(0, True), st_spec(False), st_spec(True)],
        out_shape=[_sds((b, s, D_HGRN), F32)] * 2 + [_sds((b, nb) + sq, F32)] * 2,
        scratch_shapes=[pltpu.VMEM((n_chain,) + sq, F32), pltpu.VMEM((n_chain, HBLK, D_HGRN), F32),
                        pltpu.VMEM((n_chain, HBLK, D_HGRN), F32)],
        compiler_params=_params(("arbitrary",)),
    )(proj3, proj3, proj3, proj3, proj3, proj3, lb_f, lb_b, hc_f["cs"], hc_b["cs"], hc_f["seg"], hc_f["bd"],
      hc_f["sel"])
    return o_fw.reshape(t, D_HGRN), o_bw.reshape(t, D_HGRN), st_fw, st_bw


def _scan_replay(c, rev, st, v_ref, k_scr, b_scr, states, bd_ref):
    last = 0 if rev else CHUNK - 1
    rows = pl.ds(pl.multiple_of(c * CHUNK, CHUNK), CHUNK)
    states[c] = st
    bb = b_scr[rows, :]
    bl = bb[last:last + 1, :]
    ke = k_scr[rows, :] * jnp.exp(bl - bb)
    return st * jnp.exp(bl) + bd_ref[...] * _dot_tn(v_ref[rows, :].astype(BF16), ke.astype(BF16))


def _scan_chunk_bwd(c, rev, q_ref, v_ref, do_ref, k_scr, b_scr, states, dstate, dq_ref, dv_ref, dk_scr, db_scr,
                    seg_ref, bd_ref, sel_ref, selt_ref, seld_ref):
    last = 0 if rev else CHUNK - 1
    row = lax.broadcasted_iota(jnp.int32, (CHUNK, D_HGRN), 0)
    rows = pl.ds(pl.multiple_of(c * CHUNK, CHUNK), CHUNK)
    q = q_ref[rows, :]
    k = k_scr[rows, :]
    v = v_ref[rows, :]
    bb = b_scr[rows, :]
    dout = do_ref[rows, :]
    bl = bb[last:last + 1, :]
    st_p = states[c]
    dst_n = dstate[...]
    eb = jnp.exp(bb)
    ebl = jnp.exp(bl - bb)
    ebl_last = jnp.exp(bl)
    qe = q * eb
    ke = k * ebl
    dob = dout.astype(BF16)
    dstb = dst_n.astype(BF16)
    dqe = jnp.dot(dob, st_p.astype(BF16), preferred_element_type=F32)
    dke = jnp.dot(v.astype(BF16), dstb, preferred_element_type=F32)
    dv = _dot_nt(ke.astype(BF16), dstb)
    dbl = jnp.sum(dst_n * st_p, axis=0, keepdims=True) * ebl_last + jnp.sum(dke * ke, axis=0, keepdims=True)
    dq = dqe * eb
    dk = dke * ebl
    db = dqe * qe - dke * ke
    dec = _pair_decay(bb, rev)
    q_rep = _rows_rep(q)
    k_til = _tile_rows(k)
    do_rep = _rows_rep(dout)
    pairs = dec * q_rep * k_til
    a = jnp.dot(pairs.astype(BF16), seg_ref[...], preferred_element_type=F32)
    wb = jnp.dot((_tile_rows(v) * do_rep).astype(BF16), seg_ref[...], preferred_element_type=F32)
    gdec = wb * dec
    dq = dq + jnp.dot(sel_ref[...], (gdec * k_til).astype(BF16), preferred_element_type=F32)
    dk = dk + jnp.dot(selt_ref[...], (gdec * q_rep).astype(BF16), preferred_element_type=F32)
    dv = dv + jnp.dot(selt_ref[...], (a * do_rep).astype(BF16), preferred_element_type=F32)
    db = db + jnp.dot(seld_ref[...], (wb * pairs).astype(BF16), preferred_element_type=F32)
    db = db + jnp.where(row == last, dbl, 0.0)
    dq_ref[rows, :] = dq
    dv_ref[rows, :] = dv
    dk_scr[rows, :] = dk
    db_scr[rows, :] = db
    dstate[...] = dst_n * ebl_last + bd_ref[...] * _dot_tn(dob, qe.astype(BF16))


def _hgrn_bwd2(name, proj, lb_f, lb_b, st_f, st_b, do, b, s, hc_f, hc_b):
    t = proj.shape[0]
    nb = s // HBLK
    n_ch = HBLK // CHUNK

    n_chain = 2 * b

    def body(qf_ref, zf_ref, vf_ref, dof_ref, stf_ref, qb_ref, zb_ref, vb_ref, dob_ref, stb_ref, lbf_ref, lbb_ref,
             csf_ref, csb_ref, cstf_ref, cstb_ref, seg_ref, bd_ref, sel_ref, selt_ref, seld_ref,
             dqf_ref, dvf_ref, dzf_ref, dqb_ref, dvb_ref, dzb_ref, dlbf_ref, dlbb_ref,
             dstate, states, b_scr, k_scr, db_scr, dk_scr):
        @pl.when(pl.program_id(0) == 0)
        def _():
            dlbf_ref[...] = jnp.zeros_like(dlbf_ref)
            dlbb_ref[...] = jnp.zeros_like(dlbb_ref)
            dstate[...] = jnp.zeros_like(dstate)

        chains = []
        for bi in range(b):
            chains.append(dict(rev=False, q=qf_ref.at[bi], z=zf_ref.at[bi], v=vf_ref.at[bi], do=dof_ref.at[bi],
                               st=stf_ref.at[bi], lb=lbf_ref, cs=csf_ref, cst=cstf_ref, dq=dqf_ref.at[bi],
                               dv=dvf_ref.at[bi], dz=dzf_ref.at[bi], dlb=dlbf_ref, ci=2 * bi))
            chains.append(dict(rev=True, q=qb_ref.at[bi], z=zb_ref.at[bi], v=vb_ref.at[bi], do=dob_ref.at[bi],
                               st=stb_ref.at[bi], lb=lbb_ref, cs=csb_ref, cst=cstb_ref, dq=dqb_ref.at[bi],
                               dv=dvb_ref.at[bi], dz=dzb_ref.at[bi], dlb=dlbb_ref, ci=2 * bi + 1))
        for ch in chains:
            sig, f, g, sn, kk = _gates(ch["z"][...], ch["lb"][...])
            k_scr[ch["ci"]] = kk
            b_scr[ch["ci"]] = _ldot3(ch["cs"][...], g)
            ch["gates"] = (sig, f, sn)

        def replay(i, carry):
            return tuple(_scan_replay((n_ch - 1 - i) if ch["rev"] else i, ch["rev"], st, ch["v"],
                                      k_scr.at[ch["ci"]], b_scr.at[ch["ci"]], states.at[ch["ci"]], bd_ref)
                         for ch, st in zip(chains, carry))

        lax.fori_loop(0, n_ch, replay, tuple(ch["st"][...] for ch in chains))

        def chunk(i, carry):
            for ch in chains:
                ci = ch["ci"]
                _scan_chunk_bwd(i if ch["rev"] else (n_ch - 1 - i), ch["rev"], ch["q"], ch["v"], ch["do"],
                                k_scr.at[ci], b_scr.at[ci], states.at[ci], dstate.at[ci], ch["dq"], ch["dv"],
                                dk_scr.at[ci], db_scr.at[ci], seg_ref, bd_ref, sel_ref, selt_ref, seld_ref)
            return carry

        lax.fori_loop(0, n_ch, chunk, 0)
        for ch in chains:
            sig, f, sn = ch["gates"]
            lbv = ch["lb"][...]
            hi, lo = _split2(db_scr[ch["ci"]])
            dg = (jnp.dot(ch["cst"][...], hi, preferred_element_type=F32)
                  + jnp.dot(ch["cst"][...], lo, preferred_element_type=F32))
            dgf = jnp.where(f > F_MIN, dg / f, 0.0)
            dk = dk_scr[ch["ci"]]
            ch["dz"][...] = dgf * (1.0 - lbv) * sig * (1.0 - sig) - dk * (1.0 - lbv) * sn * (1.0 - sn)
            ch["dlb"][...] += _rowgroups(dgf * (1.0 - sig) - dk * sn)

    def col(c, rev):
        return pl.BlockSpec((b, HBLK, D_HGRN), lambda j: (0, j if rev else (nb - 1 - j), c))

    def st_spec(rev):
        return pl.BlockSpec((b, None, D_HGRN, D_HGRN), lambda j: (0, j if rev else (nb - 1 - j), 0, 0))

    sq = (D_HGRN, D_HGRN)
    blk = (n_chain, HBLK, D_HGRN)
    pairs_shape = (CHUNK, CHUNK * CHUNK)
    proj3 = proj.reshape(b, s, proj.shape[1])
    do3 = do.reshape(b, s, D_HGRN)
    res = pl.pallas_call(
        body, name=name, grid=(nb,),
        in_specs=[col(COL_HQ, False), col(COL_FF, False), col(COL_HI, False), col(0, False), st_spec(False),
                  col(COL_HQ, True), col(COL_FB, True), col(COL_HI, True), col(0, True), st_spec(True),
                  _full((1, D_HGRN)), _full((1, D_HGRN)), _full((HBLK, HBLK)), _full((HBLK, HBLK)),
                  _full((HBLK, HBLK)), _full((HBLK, HBLK)), _full(sq), _full(sq),
                  _full(pairs_shape), _full(pairs_shape), _full(pairs_shape)],
        out_specs=[col(0, False)] * 3 + [col(0, True)] * 3 + [_full((8, D_HGRN))] * 2,
        out_shape=[_sds((b, s, D_HGRN), F32)] * 6 + [_sds((8, D_HGRN), F32)] * 2,
        scratch_shapes=[pltpu.VMEM((n_chain,) + sq, F32), pltpu.VMEM((n_chain, n_ch) + sq, F32)]
        + [pltpu.VMEM(blk, F32)] * 4,
        compiler_params=_params(("arbitrary",)),
    )(proj3, proj3, proj3, do3, st_f, proj3, proj3, proj3, do3, st_b, lb_f, lb_b, hc_f["cs"], hc_b["cs"],
      hc_f["cs_t"], hc_b["cs_t"], hc_f["seg"], hc_f["bd"], hc_f["sel"], hc_f["selt"], hc_f["seld"])
    return [r.reshape(t, D_HGRN) for r in res[:6]] + list(res[6:])


def _lower_bounds(logits):
    n = logits.shape[1]

    def body(x_ref, o_ref):
        x = x_ref[...]
        for d in range(2):
            rows = [x[l * 2 + d:l * 2 + d + 1, :] for l in range(DEPTH)]
            mx = functools.reduce(jnp.maximum, rows)
            ex = [jnp.exp(r - mx) for r in rows]
            tot = functools.reduce(lambda a, c: a + c, ex)
            sm = [e / tot for e in ex]
            run = jnp.zeros_like(sm[0])
            for l in range(DEPTH):
                run = run + sm[l]
                o_ref[l * 2 + d:l * 2 + d + 1, :] = run - sm[0]

    return pl.pallas_call(body, name="hgrn_lower_bounds", out_shape=_sds(logits.shape, F32),
                          in_specs=[_full(logits.shape)], out_specs=_full(logits.shape), grid=(1,),
                          compiler_params=_params(("arbitrary",)))(logits)


def _lower_bounds_bwd(logits, dlb):
    def body(x_ref, g_ref, o_ref):
        x = x_ref[...]
        gv = g_ref[...]
        for d in range(2):
            rows = [x[l * 2 + d:l * 2 + d + 1, :] for l in range(DEPTH)]
            gr = [gv[l * 2 + d:l * 2 + d + 1, :] for l in range(DEPTH)]
            mx = functools.reduce(jnp.maximum, rows)
            ex = [jnp.exp(r - mx) for r in rows]
            tot = functools.reduce(lambda a, c: a + c, ex)
            sm = [e / tot for e in ex]
            dsm = []
            for i in range(DEPTH):
                acc = functools.reduce(lambda a, c: a + c, gr[i:])
                if i == 0:
                    acc = acc - functools.reduce(lambda a, c: a + c, gr)
                dsm.append(acc)
            inner = functools.reduce(lambda a, c: a + c, [sm[i] * dsm[i] for i in range(DEPTH)])
            for i in range(DEPTH):
                o_ref[i * 2 + d:i * 2 + d + 1, :] = sm[i] * (dsm[i] - inner)

    return pl.pallas_call(body, name="hgrn_lower_bounds_bwd", out_shape=_sds(logits.shape, F32),
                          in_specs=[_full(logits.shape), _full(logits.shape)], out_specs=_full(logits.shape),
                          grid=(1,), compiler_params=_params(("arbitrary",)))(logits, dlb)


def _conv_rows(s):
    return s + 2 * (CONV_PAD + 1)


def _conv_fwd(name, proj, dw_w, dw_b, ln_w, ln_b, pw_w, pw_b, b, s):
    t = proj.shape[0]
    pad = CONV_PAD + 1
    nt = s // CONV_TILE

    def body(a_ref, g_ref, w_ref, dwb_ref, lnw_ref, lnb_ref, pw_ref, pwb_ref, y_ref, c_ref, upad, win):
        upad[0:pad, :] = jnp.zeros((pad, D_CONV), F32)
        upad[s + pad:s + 2 * pad, :] = jnp.zeros((pad, D_CONV), F32)

        def fill(i, carry):
            rows = pl.ds(pl.multiple_of(i * CONV_TILE, CONV_TILE), CONV_TILE)
            upad[pl.ds(pl.multiple_of(i * CONV_TILE + pad, pad), CONV_TILE), :] = a_ref[rows, :] * _sigmoid(g_ref[rows, :])
            return carry

        lax.fori_loop(0, nt, fill, 0)

        def tile(i, carry):
            r0 = pl.multiple_of(i * CONV_TILE, CONV_TILE)
            win[...] = upad[pl.ds(r0, CONV_TILE + 2 * pad), :]
            acc = jnp.zeros((CONV_TILE, D_CONV), F32)
            for j in range(CONV_W):
                acc = acc + win[j + 1:j + 1 + CONV_TILE, :] * w_ref[j:j + 1, :]
            c = acc + dwb_ref[...]
            c_ref[pl.ds(r0, CONV_TILE), :] = c
            mu = jnp.mean(c, axis=-1, keepdims=True)
            xc = c - mu
            rstd = lax.rsqrt(jnp.mean(xc * xc, axis=-1, keepdims=True) + LN_EPS)
            n = xc * rstd * lnw_ref[...] + lnb_ref[...]
            y_ref[pl.ds(r0, CONV_TILE), :] = (jnp.dot(_silu(n).astype(BF16), pw_ref[...].astype(BF16),
                                                      preferred_element_type=F32) + pwb_ref[...])
            return carry

        lax.fori_loop(0, nt, tile, 0)

    vec = _full((1, D_CONV))
    return pl.pallas_call(
        body, name=name, grid=(b,),
        in_specs=[pl.BlockSpec((s, D_CONV), lambda bi: (bi, COL_CA)), pl.BlockSpec((s, D_CONV), lambda bi: (bi, COL_CB)),
                  _full((CONV_W + 1, D_CONV)), vec, vec, vec, _full((D_CONV, D_CONV)), vec],
        out_specs=[pl.BlockSpec((s, D_CONV), lambda bi: (bi, 0))] * 2, out_shape=[_sds((t, D_CONV), F32)] * 2,
        scratch_shapes=[pltpu.VMEM((_conv_rows(s), D_CONV), F32), pltpu.VMEM((CONV_TILE + 2 * pad, D_CONV), F32)],
        compiler_params=_params(("parallel",)),
    )(proj, proj, dw_w, dw_b, ln_w, ln_b, pw_w, pw_b)


def _conv_bwd(name, proj, conv_out, dw_w, ln_w, ln_b, pw_w, dy, b, s):
    t = proj.shape[0]
    pad = CONV_PAD + 1
    nt = s // CONV_TILE

    def body(a_ref, g_ref, c_ref, w_ref, lnw_ref, lnb_ref, pw_ref, dy_ref, dab_ref, dpw_ref, ddw_ref, dvec_ref,
             upad, dcpad, tap_acc, win, dwin):
        @pl.when(pl.program_id(0) == 0)
        def _():
            dpw_ref[...] = jnp.zeros_like(dpw_ref)
            ddw_ref[...] = jnp.zeros_like(ddw_ref)
            dvec_ref[...] = jnp.zeros_like(dvec_ref)

        zeros = jnp.zeros((pad, D_CONV), F32)
        upad[0:pad, :] = zeros
        upad[s + pad:s + 2 * pad, :] = zeros
        dcpad[0:pad, :] = zeros
        dcpad[s + pad:s + 2 * pad, :] = zeros
        tap_acc[...] = jnp.zeros_like(tap_acc)

        def inner(i):
            return pl.ds(pl.multiple_of(i * CONV_TILE + pad, pad), CONV_TILE)

        def fill(i, carry):
            rows = pl.ds(pl.multiple_of(i * CONV_TILE, CONV_TILE), CONV_TILE)
            upad[inner(i), :] = a_ref[rows, :] * _sigmoid(g_ref[rows, :])
            return carry

        lax.fori_loop(0, nt, fill, 0)

        def tile_a(i, carry):
            r0 = pl.multiple_of(i * CONV_TILE, CONV_TILE)
            c = c_ref[pl.ds(r0, CONV_TILE), :]
            mu = jnp.mean(c, axis=-1, keepdims=True)
            xc = c - mu
            rstd = lax.rsqrt(jnp.mean(xc * xc, axis=-1, keepdims=True) + LN_EPS)
            xhat = xc * rstd
            n = xhat * lnw_ref[...] + lnb_ref[...]
            dyt = dy_ref[pl.ds(r0, CONV_TILE), :]
            dyb = dyt.astype(BF16)
            dpw_ref[...] += _dot_tn(_silu(n).astype(BF16), dyb)
            dn = _dot_nt(dyb, pw_ref[...].astype(BF16)) * _dsilu(n)
            dxh = dn * lnw_ref[...]
            dc = rstd * (dxh - jnp.mean(dxh, axis=-1, keepdims=True)
                         - xhat * jnp.mean(dxh * xhat, axis=-1, keepdims=True))
            dcpad[inner(i), :] = dc
            dvec_ref[0:1, :] += jnp.sum(dyt, axis=0, keepdims=True)
            dvec_ref[1:2, :] += jnp.sum(dn * xhat, axis=0, keepdims=True)
            dvec_ref[2:3, :] += jnp.sum(dn, axis=0, keepdims=True)
            dvec_ref[3:4, :] += jnp.sum(dc, axis=0, keepdims=True)
            return carry

        lax.fori_loop(0, nt, tile_a, 0)

        def tile_b(i, carry):
            r0 = pl.multiple_of(i * CONV_TILE, CONV_TILE)
            win[...] = upad[pl.ds(r0, CONV_TILE + 2 * pad), :]
            dwin[...] = dcpad[pl.ds(r0, CONV_TILE + 2 * pad), :]
            dct = dwin[pad:pad + CONV_TILE, :]
            du = jnp.zeros((CONV_TILE, D_CONV), F32)
            for j in range(CONV_W):
                du = du + dwin[2 * pad - 1 - j:2 * pad - 1 - j + CONV_TILE, :] * w_ref[j:j + 1, :]
                tap_acc[8 * j:8 * j + 8, :] += _rowgroups(dct * win[j + 1:j + 1 + CONV_TILE, :])
            rows = pl.ds(r0, CONV_TILE)
            sg = _sigmoid(g_ref[rows, :])
            dab_ref[rows, 0:D_CONV] = (du * sg).astype(BF16)
            dab_ref[rows, D_CONV:2 * D_CONV] = (du * a_ref[rows, :] * sg * (1.0 - sg)).astype(BF16)
            return carry

        lax.fori_loop(0, nt, tile_b, 0)
        for j in range(CONV_W):
            ddw_ref[j:j + 1, :] += jnp.sum(tap_acc[8 * j:8 * j + 8, :], axis=0, keepdims=True)

    vec = _full((1, D_CONV))
    return pl.pallas_call(
        body, name=name, grid=(b,),
        in_specs=[pl.BlockSpec((s, D_CONV), lambda bi: (bi, COL_CA)), pl.BlockSpec((s, D_CONV), lambda bi: (bi, COL_CB)),
                  pl.BlockSpec((s, D_CONV), lambda bi: (bi, 0)),
                  _full((CONV_W + 1, D_CONV)), vec, vec, _full((D_CONV, D_CONV)),
                  pl.BlockSpec((s, D_CONV), lambda bi: (bi, 0))],
        out_specs=[pl.BlockSpec((s, 2 * D_CONV), lambda bi: (bi, 0)), _full((D_CONV, D_CONV)),
                   _full((CONV_W + 1, D_CONV)), _full((8, D_CONV))],
        out_shape=[_sds((t, 2 * D_CONV), BF16), _sds((D_CONV, D_CONV), F32), _sds((CONV_W + 1, D_CONV), F32),
                   _sds((8, D_CONV), F32)],
        scratch_shapes=[pltpu.VMEM((_conv_rows(s), D_CONV), F32), pltpu.VMEM((_conv_rows(s), D_CONV), F32),
                        pltpu.VMEM((8 * CONV_W, D_CONV), F32), pltpu.VMEM((CONV_TILE + 2 * pad, D_CONV), F32),
                        pltpu.VMEM((CONV_TILE + 2 * pad, D_CONV), F32)],
        compiler_params=_params(("arbitrary",)),
    )(proj, proj, conv_out, dw_w, ln_w, ln_b, pw_w, dy)


def _mix_fwd(name, y_attn, o_fw, o_bw, proj, y_conv, aw, gw, cw, seg):
    t = y_attn.shape[0]
    tm = _row_tile(t)

    def body(ya_ref, of_ref, ob_ref, hg_ref, yc_ref, aw_ref, gw_ref, cw_ref, seg_ref, o_ref):
        ya = ya_ref[...]
        ra = lax.rsqrt(jnp.mean(ya * ya, axis=-1, keepdims=True) + EPS)
        o_ref[:, 0:D_ATTN] = (ya * ra * aw_ref[...]).astype(BF16)
        o = of_ref[...] + ob_ref[...]
        ro = lax.rsqrt(jnp.dot((o * o).astype(BF16), seg_ref[...], preferred_element_type=F32) + EPS)
        o_ref[:, D_ATTN:D_ATTN + D_HGRN] = (o * ro * gw_ref[...] * _silu(hg_ref[...])).astype(BF16)
        yc = yc_ref[...]
        rc = lax.rsqrt(jnp.mean(yc * yc, axis=-1, keepdims=True) + EPS)
        o_ref[:, D_ATTN + D_HGRN:D_MODEL] = (yc * rc * cw_ref[...]).astype(BF16)

    def tile(w, c=0):
        return pl.BlockSpec((tm, w), lambda i: (i, c))

    return pl.pallas_call(
        body, name=name, grid=(t // tm,),
        in_specs=[tile(D_ATTN), tile(D_HGRN), tile(D_HGRN), tile(D_HGRN, COL_HG), tile(D_CONV),
                  _full((1, D_ATTN)), _full((1, D_HGRN)), _full((1, D_CONV)), _full((D_HGRN, D_HGRN))],
        out_specs=tile(D_MODEL), out_shape=_sds((t, D_MODEL), BF16),
        compiler_params=_params(("parallel",)),
    )(y_attn, o_fw, o_bw, proj, y_conv, aw, gw, cw, seg)


def _mix_bwd(name, dmix, y_attn, o_fw, o_bw, proj, y_conv, aw, gw, cw, seg, deps=()):
    t = y_attn.shape[0]
    tm = _row_tile(t)

    def rms_bwd(x, w, dy):
        r = lax.rsqrt(jnp.mean(x * x, axis=-1, keepdims=True) + EPS)
        gwv = dy * w
        return r * gwv - x * (r * r * r) * jnp.mean(gwv * x, axis=-1, keepdims=True), _rowgroups(dy * x * r)

    def body(dm_ref, ya_ref, of_ref, ob_ref, hg_ref, yc_ref, aw_ref, gw_ref, cw_ref, seg_ref, *rest):
        dya_ref, do_ref, dhg_ref, dyc_ref, daw_ref, dgw_ref, dcw_ref = rest[-7:]

        @pl.when(pl.program_id(0) == 0)
        def _():
            daw_ref[...] = jnp.zeros_like(daw_ref)
            dgw_ref[...] = jnp.zeros_like(dgw_ref)
            dcw_ref[...] = jnp.zeros_like(dcw_ref)

        dya, daw = rms_bwd(ya_ref[...], aw_ref[...], dm_ref[:, 0:D_ATTN])
        dya_ref[...] = dya
        daw_ref[...] += daw
        dyc, dcw = rms_bwd(yc_ref[...], cw_ref[...], dm_ref[:, D_ATTN + D_HGRN:D_MODEL])
        dyc_ref[...] = dyc
        dcw_ref[...] += dcw
        d2 = dm_ref[:, D_ATTN:D_ATTN + D_HGRN]
        o = of_ref[...] + ob_ref[...]
        hg = hg_ref[...]
        ro = lax.rsqrt(jnp.dot((o * o).astype(BF16), seg_ref[...], preferred_element_type=F32) + EPS)
        dn = d2 * _silu(hg)
        dhg_ref[...] = (d2 * o * ro * gw_ref[...] * _dsilu(hg)).astype(BF16)
        gwv = dn * gw_ref[...]
        do_ref[...] = ro * gwv - o * (ro * ro * ro) * _rdot2(gwv * o, seg_ref[...])
        dgw_ref[...] += _rowgroups(dn * o * ro)

    def tile(w, c=0):
        return pl.BlockSpec((tm, w), lambda i: (i, c))

    return pl.pallas_call(
        body, name=name, grid=(t // tm,),
        in_specs=[tile(D_MODEL), tile(D_ATTN), tile(D_HGRN), tile(D_HGRN), tile(D_HGRN, COL_HG), tile(D_CONV),
                  _full((1, D_ATTN)), _full((1, D_HGRN)), _full((1, D_CONV)), _full((D_HGRN, D_HGRN))]
        + [_full(a.shape) for a in deps],
        out_specs=[tile(D_ATTN), tile(D_HGRN), tile(D_HGRN), tile(D_CONV),
                   _full((8, D_ATTN)), _full((8, D_HGRN)), _full((8, D_CONV))],
        out_shape=[_sds((t, D_ATTN), F32), _sds((t, D_HGRN), F32), _sds((t, D_HGRN), BF16), _sds((t, D_CONV), F32),
                   _sds((8, D_ATTN), F32), _sds((8, D_HGRN), F32), _sds((8, D_CONV), F32)],
        compiler_params=_params(("arbitrary",)),
    )(dmix, y_attn, o_fw, o_bw, proj, y_conv, aw, gw, cw, seg, *deps)


def _dproj(name, dp_attn, dq_f, dq_b, dz_fw, dz_bw, dv_f, dv_b, dhg, dp_conv):
    t = dq_f.shape[0]
    tm = _row_tile(t)
    wa, wc = dp_attn.shape[1], dp_conv.shape[1]

    def body(at_ref, qf_ref, qb_ref, zf_ref, zb_ref, vf_ref, vb_ref, hg_ref, cv_ref, o_ref):
        o_ref[:, 0:wa] = at_ref[...]
        cols = (qf_ref[...] + qb_ref[...], zf_ref[...], zb_ref[...], vf_ref[...] + vb_ref[...], hg_ref[...])
        for i, val in enumerate(cols):
            o_ref[:, wa + i * D_HGRN:wa + (i + 1) * D_HGRN] = val.astype(BF16)
        o_ref[:, wa + 5 * D_HGRN:D_IN] = cv_ref[...]

    tile = lambda w: pl.BlockSpec((tm, w), lambda i: (i, 0))
    return pl.pallas_call(
        body, name=name, grid=(t // tm,), in_specs=[tile(wa)] + [tile(D_HGRN)] * 7 + [tile(wc)],
        out_specs=tile(D_IN), out_shape=_sds((t, D_IN), BF16), compiler_params=_params(("parallel",)),
    )(dp_attn, dq_f, dq_b, dz_fw, dz_bw, dv_f, dv_b, dhg, dp_conv)


def _mm_tile(t):
    return min(512, t)


def _resident(shape):
    n = len(shape)
    return pl.BlockSpec(tuple(shape), lambda *_: (0,) * n, pipeline_mode=pl.Buffered(1))


def _w_blk(rows, cols, j_of):
    return pl.BlockSpec((None, rows, cols), lambda *g: (j_of(*g), 0, 0))


def _layer_fwd(l, x, wget, sm, tabs, cst, b, s, deps):
    t = x.shape[0]
    tm = _mm_tile(t)
    nt = t // tm
    pre = "l%d_" % l
    row = lambda w: pl.BlockSpec((tm, w), lambda i, *_: (i, 0))

    def normed(x_ref, nw_ref):
        xv = x_ref[...]
        r = lax.rsqrt(jnp.mean(xv * xv, axis=-1, keepdims=True) + EPS)
        return (xv * r * nw_ref[...]).astype(BF16)

    def in_body(x_ref, nw_ref, w_ref, *rest):
        o_ref, h_ref = rest[-2:]
        hv = normed(x_ref, nw_ref)
        h_ref[...] = hv
        for j in range(N_CHIP):
            o_ref[:, j * IN_BLK:(j + 1) * IN_BLK] = jnp.dot(hv, w_ref[j], preferred_element_type=F32)

    w_in = wget(l, "w_in", x)
    proj, h1 = pl.pallas_call(
        in_body, name=pre + "in_proj", grid=(nt,),
        in_specs=[row(D_MODEL), _full((1, D_MODEL)), _resident(w_in.shape)] + [_full(a.shape) for a in deps],
        out_specs=[row(D_IN), row(D_MODEL)], out_shape=[_sds((t, D_IN), F32), _sds((t, D_MODEL), BF16)],
        compiler_params=_params(("parallel",)),
    )(x, sm["mix_norm_w"][l], w_in, *deps)
    qn, kr, vr = _attn_prep(pre + "attn_prep", proj, s, tabs, sm["q_norm_w"][l], sm["k_norm_w"][l], cst["attn"])
    y_attn = _attn_fwd(pre + "attn", qn, kr, vr, b, s)
    o_fw, o_bw, st_fw, st_bw = _hgrn_fwd2(pre + "hgrn", proj, sm["lb"][l][0], sm["lb"][l][1], b, s, cst["hg_fw"],
                                          cst["hg_bw"])
    y_conv, conv_out = _conv_fwd(pre + "conv", proj, sm["conv_dw_w"][l], sm["conv_dw_b"][l], sm["conv_ln_w"][l],
                       sm["conv_ln_b"][l], sm["conv_pw_w"][l], sm["conv_pw_b"][l], b, s)
    mixed = _mix_fwd(pre + "mix", y_attn, o_fw, o_bw, proj, y_conv, sm["attn_out_norm_w"][l], sm["gnorm_w"][l],
                     sm["conv_out_norm_w"][l], cst["seg_h"])
    (x1,) = _mm(pre + "out_proj", (nt,),
                [(mixed, row(D_MODEL), wget(l, "w_out", mixed),
                  pl.BlockSpec((N_CHIP, OUT_BLK, D_MODEL), lambda i: (0, 0, 0)), NN)],
                [(x, row(D_MODEL))], [(_sds((t, D_MODEL), F32), row(D_MODEL))],
                lambda tot, xr: (xr + tot,))
    ff3 = pl.BlockSpec((N_CHIP, tm, FF_BLK), lambda i: (0, i, 0))
    ffs = _sds((N_CHIP, t, FF_BLK), BF16)

    def gu_body(x_ref, nw_ref, wg_ref, wu_ref, g_ref, u_ref, a_ref, h_ref):
        hv = normed(x_ref, nw_ref)
        h_ref[...] = hv
        for j in range(N_CHIP):
            gv = jnp.dot(hv, wg_ref[j], preferred_element_type=F32)
            uv = jnp.dot(hv, wu_ref[j], preferred_element_type=F32)
            g_ref[j] = gv.astype(BF16)
            u_ref[j] = uv.astype(BF16)
            a_ref[j] = (_silu(gv) * uv).astype(BF16)

    w_gate, w_up = wget(l, "w_gate", x1), wget(l, "w_up", x1)
    gate, up, act, h2 = pl.pallas_call(
        gu_body, name=pre + "ffn_gate_up", grid=(nt,),
        in_specs=[row(D_MODEL), _full((1, D_MODEL)), _resident(w_gate.shape), _resident(w_up.shape)],
        out_specs=[ff3, ff3, ff3, row(D_MODEL)], out_shape=[ffs, ffs, ffs, _sds((t, D_MODEL), BF16)],
        compiler_params=_params(("parallel",)),
    )(x1, sm["ffn_norm_w"][l], w_gate, w_up)

    def down_body(a_ref, w_ref, x_ref, o_ref):
        tot = x_ref[...]
        for j in range(N_CHIP):
            tot = tot + jnp.dot(a_ref[j], w_ref[j], preferred_element_type=F32)
        o_ref[...] = tot

    w_down = wget(l, "w_down", act)
    x2 = pl.pallas_call(
        down_body, name=pre + "ffn_down", grid=(nt,), in_specs=[ff3, _resident(w_down.shape), row(D_MODEL)],
        out_specs=row(D_MODEL), out_shape=_sds((t, D_MODEL), F32), compiler_params=_params(("parallel",)),
    )(act, w_down, x1)
    saved = dict(x=x, h1=h1, proj=proj, qn=qn, kr=kr, vr=vr, y_attn=y_attn, o_fw=o_fw, o_bw=o_bw, st_fw=st_fw,
                 st_bw=st_bw, y_conv=y_conv, conv_out=conv_out, mixed=mixed, x1=x1, h2=h2, gate=gate, up=up, act=act)
    return x2, saved


def _layer_bwd(l, dx2, sv, wget, sm, tabs, cst, b, s, on_grads):
    t = dx2.shape[0]
    tm = _mm_tile(t)
    nt = t // tm
    pre = "l%d_" % l
    tk = min(2048, t)
    nk = t // tk
    row = lambda w: pl.BlockSpec((tm, w), lambda i, *_: (i, 0))
    ff3 = pl.BlockSpec((N_CHIP, tm, FF_BLK), lambda i: (0, i, 0))
    ffs = _sds((N_CHIP, t, FF_BLK), BF16)

    w_down, w_gate, w_up = wget(l, "w_down", dx2), wget(l, "w_gate", dx2), wget(l, "w_up", dx2)

    def ddx_body(dx_ref, w_ref, g_ref, u_ref, dg_ref, du_ref):
        dxb = dx_ref[...].astype(BF16)
        for j in range(N_CHIP):
            da = _dot_nt(dxb, w_ref[j])
            g = g_ref[j].astype(F32)
            sg = _sigmoid(g)
            dg_ref[j] = (da * u_ref[j].astype(F32) * (sg * (1.0 + g * (1.0 - sg)))).astype(BF16)
            du_ref[j] = (da * (g * sg)).astype(BF16)

    dgate, dup = pl.pallas_call(
        ddx_body, name=pre + "ffn_down_dx", grid=(nt,), in_specs=[row(D_MODEL), _resident(w_down.shape), ff3, ff3],
        out_specs=[ff3, ff3], out_shape=[ffs, ffs], compiler_params=_params(("parallel",)),
    )(dx2, w_down, sv["gate"], sv["up"])
    colt = lambda w: pl.BlockSpec((tk, w), lambda j, k: (k, 0))
    fft = pl.BlockSpec((None, tk, FF_BLK), lambda j, k: (j, k, 0))
    (g_down,) = _mm(pre + "ffn_down_dw", (N_CHIP, nk), [(sv["act"], fft, dx2, colt(D_MODEL), TN)], [],
                    [(_sds((N_CHIP, FF_BLK, D_MODEL), BF16), pl.BlockSpec((None, FF_BLK, D_MODEL), lambda j, k: (j, 0, 0)))],
                    lambda tot: (tot,), acc=(1, (FF_BLK, D_MODEL)))
    wff = pl.BlockSpec((None, D_MODEL, FF_BLK), lambda j, k: (j, 0, 0))
    (g_gate,) = _mm(pre + "ffn_gate_dw", (N_CHIP, nk), [(sv["h2"], colt(D_MODEL), dgate, fft, TN)], [],
                    [(_sds((N_CHIP, D_MODEL, FF_BLK), BF16), wff)], lambda tot: (tot,), acc=(1, (D_MODEL, FF_BLK)))
    (g_up,) = _mm(pre + "ffn_up_dw", (N_CHIP, nk), [(sv["h2"], colt(D_MODEL), dup, fft, TN)], [],
                  [(_sds((N_CHIP, D_MODEL, FF_BLK), BF16), wff)], lambda tot: (tot,), acc=(1, (D_MODEL, FF_BLK)))

    def norm_bwd_tail(dh, x_ref, nw_ref, dres_ref, dx_ref, dw_ref):
        xv = x_ref[...]
        r = lax.rsqrt(jnp.mean(xv * xv, axis=-1, keepdims=True) + EPS)
        gw = dh * nw_ref[...]
        dx_ref[...] = dres_ref[...] + r * gw - xv * (r * r * r) * jnp.mean(gw * xv, axis=-1, keepdims=True)

        @pl.when(pl.program_id(0) == 0)
        def _():
            dw_ref[...] = jnp.zeros_like(dw_ref)

        dw_ref[...] += _rowgroups(dh * xv * r)

    def dh_body(dg_ref, du_ref, wg_ref, wu_ref, x_ref, nw_ref, dres_ref, *rest):
        tot = None
        for j in range(N_CHIP):
            r = _dot_nt(dg_ref[j], wg_ref[j]) + _dot_nt(du_ref[j], wu_ref[j])
            tot = r if tot is None else tot + r
        norm_bwd_tail(tot, x_ref, nw_ref, dres_ref, *rest[-2:])

    deps = on_grads(l, dict(w_gate=g_gate, w_up=g_up, w_down=g_down))
    dx1, d_ffn_norm = pl.pallas_call(
        dh_body, name=pre + "ffn_dh", grid=(nt,),
        in_specs=[ff3, ff3, _resident(w_gate.shape), _resident(w_up.shape), row(D_MODEL), _full((1, D_MODEL)),
                  row(D_MODEL)] + [_full(a.shape) for a in deps],
        out_specs=[row(D_MODEL), _full((8, D_MODEL))], out_shape=[_sds((t, D_MODEL), F32), _sds((8, D_MODEL), F32)],
        compiler_params=_params(("arbitrary",)),
    )(dgate, dup, w_gate, w_up, sv["x1"], sm["ffn_norm_w"][l], dx2, *deps)

    (dmix,) = _mm(pre + "out_proj_dx", (nt,),
                  [(dx1, row(D_MODEL), wget(l, "w_out", dx2),
                    pl.BlockSpec((N_CHIP, OUT_BLK, D_MODEL), lambda i: (0, 0, 0)), NT)], [],
                  [(_sds((t, D_MODEL), F32), row(D_MODEL))], lambda tot: (tot,))
    (g_out,) = _mm(pre + "out_proj_dw", (N_CHIP, nk),
                   [(sv["mixed"], pl.BlockSpec((tk, OUT_BLK), lambda j, k: (k, j)), dx1, colt(D_MODEL), TN)], [],
                   [(_sds((N_CHIP, OUT_BLK, D_MODEL), BF16), pl.BlockSpec((None, OUT_BLK, D_MODEL), lambda j, k: (j, 0, 0)))],
                   lambda tot: (tot,), acc=(1, (OUT_BLK, D_MODEL)))
    proj = sv["proj"]
    dya, do_h, dhg, dyc, d_aw, d_gw, d_cw = _mix_bwd(
        pre + "mix_bwd", dmix, sv["y_attn"], sv["o_fw"], sv["o_bw"], proj, sv["y_conv"],
        sm["attn_out_norm_w"][l], sm["gnorm_w"][l], sm["conv_out_norm_w"][l], cst["seg_h"],
        on_grads(l, dict(w_out=g_out)))
    dqs, dkr, dvr = _attn_bwd(pre + "attn_bwd", sv["qn"], sv["kr"], sv["vr"], dya, b, s)
    dp_attn, d_qw, d_kw = _attn_prep_bwd(pre + "attn_prep_bwd", proj, s, tabs, sm["q_norm_w"][l], sm["k_norm_w"][l],
                                         cst["attn"], dqs, dkr, dvr)
    dq_f, dv_f, dz_fw, dq_b, dv_b, dz_bw, dlb_fw, dlb_bw = _hgrn_bwd2(
        pre + "hgrn_bwd", proj, sm["lb"][l][0], sm["lb"][l][1], sv["st_fw"], sv["st_bw"], do_h, b, s,
        cst["hg_fw"], cst["hg_bw"])
    dp_conv, d_pw, d_dw, d_cvec = _conv_bwd(pre + "conv_bwd", proj, sv["conv_out"], sm["conv_dw_w"][l],
                                            sm["conv_ln_w"][l], sm["conv_ln_b"][l], sm["conv_pw_w"][l], dyc, b, s)
    dproj = _dproj(pre + "dproj", dp_attn, dq_f, dq_b, dz_fw, dz_bw, dv_f, dv_b, dhg, dp_conv)
    g_pw = d_pw.reshape(N_CHIP, D_CONV // N_CHIP, D_CONV).astype(BF16)

    (g_in,) = _mm(pre + "in_proj_dw", (N_CHIP, nk),
                  [(sv["h1"], colt(D_MODEL), dproj, pl.BlockSpec((tk, IN_BLK), lambda j, k: (k, j)), TN)], [],
                  [(_sds((N_CHIP, D_MODEL, IN_BLK), BF16), pl.BlockSpec((None, D_MODEL, IN_BLK), lambda j, k: (j, 0, 0)))],
                  lambda tot: (tot,), acc=(1, (D_MODEL, IN_BLK)))

    def indx_body(dp_ref, w_ref, x_ref, nw_ref, dres_ref, *rest):
        tot = None
        for j in range(N_CHIP):
            r = _dot_nt(dp_ref[:, j * IN_BLK:(j + 1) * IN_BLK], w_ref[j])
            tot = r if tot is None else tot + r
        norm_bwd_tail(tot, x_ref, nw_ref, dres_ref, *rest[-2:])

    w_in = wget(l, "w_in", dx2)
    deps = on_grads(l, dict(w_in=g_in, conv_pw_w=g_pw))
    dx, d_mix_norm = pl.pallas_call(
        indx_body, name=pre + "in_proj_dx", grid=(nt,),
        in_specs=[row(D_IN), _resident(w_in.shape), row(D_MODEL), _full((1, D_MODEL)), row(D_MODEL)]
        + [_full(a.shape) for a in deps],
        out_specs=[row(D_MODEL), _full((8, D_MODEL))], out_shape=[_sds((t, D_MODEL), F32), _sds((8, D_MODEL), F32)],
        compiler_params=_params(("arbitrary",)),
    )(dproj, w_in, sv["x"], sm["mix_norm_w"][l], dx1, *deps)
    heads = lambda v, n: v.sum(axis=0).reshape(n, HEAD_DIM).sum(axis=0)
    small = dict(
        mix_norm_w=d_mix_norm.sum(axis=0), q_norm_w=heads(d_qw, D_ATTN // HEAD_DIM), k_norm_w=heads(d_kw, N_KV),
        lb=jnp.stack([dlb_fw.sum(axis=0), dlb_bw.sum(axis=0)]), hgrn_gnorm_w=heads(d_gw, D_HGRN // HEAD_DIM),
        conv_dw_w=d_dw[:CONV_W], conv_dw_b=d_cvec[3], conv_ln_w=d_cvec[1], conv_ln_b=d_cvec[2],
        conv_pw_b=d_cvec[0], attn_out_norm_w=d_aw.sum(axis=0), conv_out_norm_w=d_cw.sum(axis=0),
        ffn_norm_w=d_ffn_norm.sum(axis=0))
    return dx, small


SMALL_ORDER = ("mix_norm_w", "q_norm_w", "k_norm_w", "lb", "hgrn_gnorm_w", "conv_dw_w", "conv_dw_b", "conv_ln_w",
               "conv_ln_b", "conv_pw_b", "attn_out_norm_w", "conv_out_norm_w", "ffn_norm_w")
BIG_ORDER = ("w_in", "w_out", "w_gate", "w_up", "w_down")
SCATTER_ORDER = BIG_ORDER + ("conv_pw_w",)


def _local_step(x, target, wget, sm, deps, on_grads):
    b, s, d = x.shape
    t = b * s
    cos, sin = _rope_tables(s)
    tabs = dict(cq=jnp.tile(cos, (1, D_ATTN // HEAD_DIM)), sq=jnp.tile(sin, (1, D_ATTN // HEAD_DIM)),
                ck=jnp.tile(cos, (1, N_KV)), sk=jnp.tile(sin, (1, N_KV)))
    cst = dict(attn=_attn_consts(), hg_fw=_hgrn_consts(False), hg_bw=_hgrn_consts(True),
               seg_h=_bf(_seg_matrix(D_HGRN, HEAD_DIM, 1.0 / HEAD_DIM)))
    vec = lambda a: a.reshape(DEPTH, 1, -1)
    smk = dict(sm)
    for n in ("mix_norm_w", "conv_dw_b", "conv_ln_w", "conv_ln_b", "conv_pw_b", "attn_out_norm_w", "conv_out_norm_w",
              "ffn_norm_w"):
        smk[n] = vec(sm[n])
    smk["q_norm_w"] = vec(jnp.tile(sm["q_norm_w"], (1, D_ATTN // HEAD_DIM)))
    smk["k_norm_w"] = vec(jnp.tile(sm["k_norm_w"], (1, N_KV)))
    smk["gnorm_w"] = vec(jnp.tile(sm["hgrn_gnorm_w"], (1, D_HGRN // HEAD_DIM)))
    smk["lb"] = sm["lb"].reshape(DEPTH, 2, 1, D_HGRN)
    smk["conv_dw_w"] = jnp.pad(sm["conv_dw_w"], ((0, 0), (0, 1), (0, 0)))

    h = x.reshape(t, d)
    saved = []
    for l in range(DEPTH):
        h, sv = _layer_fwd(l, h, wget, smk, tabs, cst, b, s, deps if l == 0 else ())
        saved.append(sv)
    dy, sq = _loss_kernel(h, target.reshape(t, d))
    sq_sum = jnp.sum(sq)
    dh = dy
    smalls = [None] * DEPTH
    for l in reversed(range(DEPTH)):
        dh, smalls[l] = _layer_bwd(l, dh, saved[l], wget, smk, tabs, cst, b, s, on_grads)
    return sq_sum, dh.reshape(b, s, d), smalls


HBM_SPEC = pl.BlockSpec(memory_space=pltpu.HBM)


def _exchange(name, arrs, mode):
    n = len(arrs)
    if mode == "gather8":
        flips = [(fx, fy, fc) for fx in (0, 1) for fy in (0, 1) for fc in (0, 1)][1:]
    elif mode == "sibling":
        flips = [(0, 0, 1)]
    else:
        flips = [(1, 0, 0), (0, 1, 0), (1, 1, 0)]
    n_f = len(flips)

    def body(*refs):
        ins, outs = refs[:n], refs[n:2 * n]
        send_sems, recv_sems, local_sems = refs[2 * n:]
        x, y, c = lax.axis_index("x"), lax.axis_index("y"), lax.axis_index("c")

        def slot_of(px, py, pc):
            return (2 * px + py) if mode != "gather8" else (4 * px + 2 * py + pc)

        me = slot_of(x, y, c)
        started = []
        for i in range(n):
            if mode != "sibling":
                src = ins[i].at[me] if mode == "scatter4" else ins[i]
                loc = pltpu.make_async_copy(src, outs[i].at[me], local_sems.at[i])
                loc.start()
                started.append(loc)
        sends, recvs = [], []
        for i in range(n):
            for f, (fx, fy, fc) in enumerate(flips):
                peer = (x ^ fx, y ^ fy, c ^ fc)
                ps = slot_of(*peer)
                if mode == "sibling":
                    src, dst, landed = ins[i], outs[i], outs[i]
                elif mode == "scatter4":
                    src, dst, landed = ins[i].at[ps], outs[i].at[me], outs[i].at[ps]
                else:
                    src, dst, landed = ins[i], outs[i].at[me], outs[i].at[ps]
                k = i * n_f + f
                cp = pltpu.make_async_remote_copy(src_ref=src, dst_ref=dst, send_sem=send_sems.at[k],
                                                  recv_sem=recv_sems.at[k], device_id=peer,
                                                  device_id_type=pl.DeviceIdType.MESH)
                cp.start()
                sends.append(cp)
                recvs.append(pltpu.make_async_remote_copy(src_ref=src, dst_ref=landed, send_sem=send_sems.at[k],
                                                          recv_sem=recv_sems.at[k], device_id=peer,
                                                          device_id_type=pl.DeviceIdType.MESH))
        for cp in sends:
            cp.wait_send()
        for cp in recvs:
            cp.wait_recv()
        for loc in started:
            loc.wait()

    def out_sds(a):
        if mode == "gather4":
            return _sds((N_CHIP,) + a.shape, a.dtype)
        if mode == "gather8":
            return _sds((N_DEV,) + a.shape, a.dtype)
        return _sds(a.shape, a.dtype)

    res = pl.pallas_call(
        body, name=name, in_specs=[HBM_SPEC] * n, out_specs=[HBM_SPEC] * n, out_shape=[out_sds(a) for a in arrs],
        scratch_shapes=[pltpu.SemaphoreType.DMA((n * n_f,)), pltpu.SemaphoreType.DMA((n * n_f,)),
                        pltpu.SemaphoreType.DMA((max(n, 1),))],
    )(*arrs)
    return list(res)


SEM_SPEC = pl.BlockSpec(memory_space=pltpu.SEMAPHORE)
SPLIT_EFFECT = pltpu.SideEffectType.DATAFLOW_SIDE_EFFECTING
CHIP_FLIPS = ((1, 0), (0, 1), (1, 1))


def _chip_copies(src_refs, land_refs, send_sems, recv_sems, scatter):
    x, y, c = lax.axis_index("x"), lax.axis_index("y"), lax.axis_index("c")
    me = 2 * x + y
    out = []
    for i, land in enumerate(land_refs):
        for f, (fx, fy) in enumerate(CHIP_FLIPS):
            peer = (x ^ fx, y ^ fy, c)
            ps = 2 * (x ^ fx) + (y ^ fy)
            src = src_refs[i].at[ps] if scatter else land.at[me]
            k = i * len(CHIP_FLIPS) + f
            kw = dict(send_sem=send_sems.at[k], recv_sem=recv_sems.at[k], device_id=peer,
                      device_id_type=pl.DeviceIdType.MESH)
            out.append((pltpu.make_async_remote_copy(src_ref=src, dst_ref=land.at[me], **kw),
                        pltpu.make_async_remote_copy(src_ref=src, dst_ref=land.at[ps], **kw)))
    return out


def _split_start(name, srcs, lands, scatter):
    n = len(lands)
    n_src = len(srcs)
    n_sem = n * len(CHIP_FLIPS)

    def body(*refs):
        src_refs = refs[:n_src]
        land_refs = refs[n_src:n_src + n]
        send_sems, recv_sems = refs[n_src + n], refs[n_src + n + 1]
        token = refs[-1]
        for start, _ in _chip_copies(src_refs, land_refs, send_sems, recv_sems, scatter):
            start.start()
        token[...] = jnp.zeros_like(token)

    arrs = list(srcs) + list(lands)
    res = pl.pallas_call(
        body, name=name,
        out_shape=(pltpu.SemaphoreType.DMA((n_sem,)), pltpu.SemaphoreType.DMA((n_sem,)),
                   *[pltpu.HBM(a.shape, a.dtype) for a in arrs], _sds((8, LANES), F32)),
        in_specs=[HBM_SPEC] * len(arrs),
        out_specs=(SEM_SPEC, SEM_SPEC, *[HBM_SPEC] * len(arrs), pl.BlockSpec(memory_space=pltpu.VMEM)),
        input_output_aliases={i: 2 + i for i in range(len(arrs))},
        compiler_params=pltpu.CompilerParams(has_side_effects=SPLIT_EFFECT),
    )(*[pltpu.with_memory_space_constraint(a, pltpu.HBM) for a in arrs])
    return dict(send=res[0], recv=res[1], srcs=list(res[2:2 + n_src]), lands=list(res[2 + n_src:2 + n_src + n]),
                token=res[-1], scatter=scatter)


def _split_wait(name, started, after):
    srcs, lands, scatter = started["srcs"], started["lands"], started["scatter"]
    n, n_src = len(lands), len(srcs)

    def body(*refs):
        src_refs = refs[:n_src]
        land_refs = refs[n_src:n_src + n]
        send_sems, recv_sems = refs[n_src + n], refs[n_src + n + 1]
        for _, wait in _chip_copies(src_refs, land_refs, send_sems, recv_sems, scatter):
            wait.wait_send()
            wait.wait_recv()

    arrs = list(srcs) + list(lands)
    res = pl.pallas_call(
        body, name=name, out_shape=tuple(pltpu.HBM(a.shape, a.dtype) for a in arrs),
        in_specs=[HBM_SPEC] * len(arrs) + [SEM_SPEC, SEM_SPEC, pl.BlockSpec(memory_space=pl.ANY)],
        out_specs=tuple([HBM_SPEC] * len(arrs)), input_output_aliases={i: i for i in range(len(arrs))},
        compiler_params=pltpu.CompilerParams(has_side_effects=SPLIT_EFFECT),
    )(*arrs, started["send"], started["recv"], after)
    return list(res[n_src:])


def _flat_tile(rows):
    for cand in (512, 256, 128, 64, 32, 16, 8):
        if rows % cand == 0:
            return cand
    return rows


def _cast_slot(name, a, l, chip):
    r, c = a.shape[0] // DEPTH, a.shape[1]
    tr = _flat_tile(r)

    def body(chip_ref, a_ref, o_ref):
        o_ref[...] = a_ref[...].astype(BF16)

    return pl.pallas_call(
        body, name=name, out_shape=_sds((N_CHIP, r, c), BF16),
        grid_spec=pltpu.PrefetchScalarGridSpec(
            num_scalar_prefetch=1, grid=(r // tr,),
            in_specs=[pl.BlockSpec((tr, c), lambda i, ch: (l * (r // tr) + i, 0))],
            out_specs=pl.BlockSpec((None, tr, c), lambda i, ch: (ch[0], i, 0))),
        compiler_params=_params(("parallel",)))(chip, a)


def _own_slot(name, g, chip):
    n, r, c = g.shape
    tr = _flat_tile(r)

    def body(chip_ref, g_ref, o_ref):
        o_ref[...] = g_ref[...]

    spec = pl.BlockSpec((None, tr, c), lambda i, ch: (ch[0], i, 0))
    return pl.pallas_call(
        body, name=name, out_shape=_sds(g.shape, g.dtype),
        grid_spec=pltpu.PrefetchScalarGridSpec(num_scalar_prefetch=1, grid=(r // tr,), in_specs=[spec], out_specs=spec),
        compiler_params=_params(("parallel",)))(chip, g)


def _sum_layers(name, lands):
    n, r, c = lands[0].shape
    tr = _flat_tile(r)
    nl = len(lands)

    def body(*refs):
        o_ref = refs[-1]
        for k in range(nl):
            @pl.when(pl.program_id(0) == k)
            def _():
                tot = refs[k][0].astype(F32)
                for i in range(1, n):
                    tot = tot + refs[k][i].astype(F32)
                o_ref[...] = tot

    return pl.pallas_call(
        body, name=name, grid=(nl, r // tr),
        in_specs=[pl.BlockSpec((n, tr, c), lambda l, i, k=k: (0, jnp.where(l == k, i, 0), 0)) for k in range(nl)],
        out_specs=pl.BlockSpec((tr, c), lambda l, i: (l * (r // tr) + i, 0)), out_shape=_sds((nl * r, c), F32),
        compiler_params=_params(("arbitrary", "arbitrary")))(*lands)


def _sum_slots(name, a, scale=None):
    n, r, c = a.shape
    tr = _flat_tile(r)

    def body(a_ref, o_ref):
        tot = a_ref[0].astype(F32)
        for i in range(1, n):
            tot = tot + a_ref[i].astype(F32)
        o_ref[...] = tot

    return pl.pallas_call(body, name=name, grid=(r // tr,),
                          in_specs=[pl.BlockSpec((n, tr, c), lambda i: (0, i, 0))],
                          out_specs=pl.BlockSpec((tr, c), lambda i: (i, 0)), out_shape=_sds((r, c), F32),
                          compiler_params=_params(("parallel",)))(a)


def _adamw(name, w, ga, gb, m, v):
    r, c = w.shape
    tr = _flat_tile(r)
    c1 = 1.0 - B1 ** STEP
    c2 = 1.0 - B2 ** STEP
    two = gb is not None

    def body(*refs):
        if two:
            w_ref, ga_ref, gb_ref, m_ref, v_ref, g_out, d_out, m_out, v_out = refs
            g = ga_ref[...] + gb_ref[...]
        else:
            w_ref, ga_ref, m_ref, v_ref, g_out, d_out, m_out, v_out = refs
            g = ga_ref[...]
        mn = B1 * m_ref[...] + (1.0 - B1) * g
        vn = B2 * v_ref[...] + (1.0 - B2) * (g * g)
        g_out[...] = g
        m_out[...] = mn
        v_out[...] = vn
        d_out[...] = -LR * ((mn / c1) / (jnp.sqrt(vn / c2) + ADAM_EPS) + WD * w_ref[...])

    spec = pl.BlockSpec((tr, c), lambda i: (i, 0))
    ins = [w, ga, gb, m, v] if two else [w, ga, m, v]
    return pl.pallas_call(body, name=name, grid=(r // tr,), in_specs=[spec] * len(ins), out_specs=[spec] * 4,
                          out_shape=[_sds((r, c), F32)] * 4, compiler_params=_params(("parallel",)))(*ins)


WEIGHTS = ('mix_norm_w', 'w_in', 'q_norm_w', 'k_norm_w', 'hgrn_lb_logits', 'hgrn_gnorm_w', 'conv_dw_w', 'conv_dw_b',
           'conv_ln_w', 'conv_ln_b', 'conv_pw_w', 'conv_pw_b', 'attn_out_norm_w', 'conv_out_norm_w', 'w_out',
           'ffn_norm_w', 'w_gate', 'w_up', 'w_down')
SHARDED_SMALL = {"hgrn_lb_logits": 2, "conv_dw_w": 2, "conv_pw_w": 1}
LANES = 128
PACK_ROWS = 256


def _pack(parts):
    flat = jnp.concatenate([p.reshape(-1) for p in parts])
    n = flat.shape[0]
    rows = -(-n // (PACK_ROWS * LANES)) * PACK_ROWS
    return jnp.pad(flat, (0, rows * LANES - n)).reshape(rows, LANES)


def _unpack(packed, shapes):
    flat = packed.reshape(-1)
    out, off = [], 0
    for shp in shapes:
        n = int(np.prod(shp))
        out.append(flat[off:off + n].reshape(shp))
        off += n
    return out


def kernel(x, mix_norm_w, w_in, q_norm_w, k_norm_w, hgrn_lb_logits, hgrn_gnorm_w, conv_dw_w, conv_dw_b, conv_ln_w, conv_ln_b, conv_pw_w, conv_pw_b, attn_out_norm_w, conv_out_norm_w, w_out, ffn_norm_w, w_gate, w_up, w_down, loss_target, m_mix_norm_w, m_w_in, m_q_norm_w, m_k_norm_w, m_hgrn_lb_logits, m_hgrn_gnorm_w, m_conv_dw_w, m_conv_dw_b, m_conv_ln_w, m_conv_ln_b, m_conv_pw_w, m_conv_pw_b, m_attn_out_norm_w, m_conv_out_norm_w, m_w_out, m_ffn_norm_w, m_w_gate, m_w_up, m_w_down, v_mix_norm_w, v_w_in, v_q_norm_w, v_k_norm_w, v_hgrn_lb_logits, v_hgrn_gnorm_w, v_conv_dw_w, v_conv_dw_b, v_conv_ln_w, v_conv_ln_b, v_conv_pw_w, v_conv_pw_b, v_attn_out_norm_w, v_conv_out_norm_w, v_w_out, v_ffn_norm_w, v_w_gate, v_w_up, v_w_down):
    w = dict(mix_norm_w=mix_norm_w, w_in=w_in, q_norm_w=q_norm_w, k_norm_w=k_norm_w, hgrn_lb_logits=hgrn_lb_logits,
             hgrn_gnorm_w=hgrn_gnorm_w, conv_dw_w=conv_dw_w, conv_dw_b=conv_dw_b, conv_ln_w=conv_ln_w,
             conv_ln_b=conv_ln_b, conv_pw_w=conv_pw_w, conv_pw_b=conv_pw_b, attn_out_norm_w=attn_out_norm_w,
             conv_out_norm_w=conv_out_norm_w, w_out=w_out, ffn_norm_w=ffn_norm_w, w_gate=w_gate, w_up=w_up,
             w_down=w_down)
    m = dict(mix_norm_w=m_mix_norm_w, w_in=m_w_in, q_norm_w=m_q_norm_w, k_norm_w=m_k_norm_w,
             hgrn_lb_logits=m_hgrn_lb_logits, hgrn_gnorm_w=m_hgrn_gnorm_w, conv_dw_w=m_conv_dw_w,
             conv_dw_b=m_conv_dw_b, conv_ln_w=m_conv_ln_w, conv_ln_b=m_conv_ln_b, conv_pw_w=m_conv_pw_w,
             conv_pw_b=m_conv_pw_b, attn_out_norm_w=m_attn_out_norm_w, conv_out_norm_w=m_conv_out_norm_w,
             w_out=m_w_out, ffn_norm_w=m_ffn_norm_w, w_gate=m_w_gate, w_up=m_w_up, w_down=m_w_down)
    v = dict(mix_norm_w=v_mix_norm_w, w_in=v_w_in, q_norm_w=v_q_norm_w, k_norm_w=v_k_norm_w,
             hgrn_lb_logits=v_hgrn_lb_logits, hgrn_gnorm_w=v_hgrn_gnorm_w, conv_dw_w=v_conv_dw_w,
             conv_dw_b=v_conv_dw_b, conv_ln_w=v_conv_ln_w, conv_ln_b=v_conv_ln_b, conv_pw_w=v_conv_pw_w,
             conv_pw_b=v_conv_pw_b, attn_out_norm_w=v_attn_out_norm_w, conv_out_norm_w=v_conv_out_norm_w,
             w_out=v_w_out, ffn_norm_w=v_ffn_norm_w, w_gate=v_w_gate, w_up=v_w_up, w_down=v_w_down)
    chip = 2 * lax.axis_index("x") + lax.axis_index("y")

    chip1 = chip.reshape(1).astype(jnp.int32)

    flat2 = lambda a: a.reshape(-1, a.shape[-1])
    small_pack = _pack([w[n] for n in SHARDED_SMALL])
    gathered = _exchange("gather_small_weights", [small_pack], "gather4")
    groups = [[(0, "w_in")], [(0, n) for n in BIG_ORDER[1:]], [(1, n) for n in BIG_ORDER]]
    group_of = {key: g for g, keys in enumerate(groups) for key in keys}
    starts = []
    for g, keys in enumerate(groups):
        slots = [_cast_slot("cast_%s_l%d" % (n, l), flat2(w[n]), l, chip1) for l, n in keys]
        starts.append(_split_start("gather_start_g%d" % g, [], slots, False))
    got = {}

    def wget(l, name, after):
        if (l, name) not in got:
            g = group_of[(l, name)]
            for key, arr in zip(groups[g], _split_wait("gather_wait_g%d" % g, starts[g], after)):
                got[key] = arr
        return got[(l, name)]

    pending = []

    def on_grads(l, grads):
        names = [n for n in SCATTER_ORDER if n in grads]
        own = [_own_slot("own_%s_l%d" % (n, l), grads[n], chip1) for n in names]
        st = _split_start("scatter_start_l%d_%s" % (l, names[0]), [grads[n] for n in names], own, True)
        pending.append((l, names, st))
        return [st["token"]]

    parts = [_unpack(gathered[-1][j], [w[n].shape for n in SHARDED_SMALL]) for j in range(N_CHIP)]
    full_small = {n: jnp.concatenate([parts[j][i] for j in range(N_CHIP)], axis=ax)
                  for i, (n, ax) in enumerate(SHARDED_SMALL.items())}
    sm = {n: w[n] for n in WEIGHTS if n not in BIG_ORDER and n not in SHARDED_SMALL}
    sm["conv_dw_w"] = full_small["conv_dw_w"]
    sm["conv_pw_w"] = full_small["conv_pw_w"]
    logits = full_small["hgrn_lb_logits"].reshape(DEPTH * 2, D_HGRN)
    sm["lb"] = _lower_bounds(logits).reshape(DEPTH, 2, D_HGRN)

    sq_sum, grad_x, smalls = _local_step(x, loss_target, wget, sm, [st["token"] for st in starts], on_grads)
    loss = lax.psum(0.5 * sq_sum / D_MODEL, ("x", "y", "c"))

    landed = {}
    for l, names, st in pending:
        for n, arr in zip(names, _split_wait("scatter_wait_l%d_%s" % (l, names[0]), st, grad_x)):
            landed[(l, n)] = arr
    sums = [_sum_layers("sum_" + n, [landed[(l, n)] for l in range(DEPTH)]) for n in SCATTER_ORDER]
    sib = _exchange("sibling_grads", sums, "sibling")
    out = {}
    for n, ga, gb in zip(SCATTER_ORDER, sums, sib):
        res = _adamw("adamw_" + n, flat2(w[n]), ga, gb, flat2(m[n]), flat2(v[n]))
        out[n] = [r.reshape(w[n].shape) for r in res]

    small_names = [n for n in WEIGHTS if n not in SCATTER_ORDER]
    g_pack = _pack([jnp.stack([smalls[l][n] for l in range(DEPTH)]) for n in SMALL_ORDER])
    g_all = _exchange("gather_small_grads", [g_pack], "gather8")[0]
    g_tot = _sum_slots("sum_small", g_all)
    shapes = [(DEPTH,) + tuple(smalls[0][n].shape) for n in SMALL_ORDER]
    g_small = dict(zip(SMALL_ORDER, _unpack(g_tot, shapes)))
    lb_shard = lax.dynamic_slice_in_dim(g_small.pop("lb").reshape(DEPTH * 2, D_HGRN), chip * HEAD_DIM, HEAD_DIM, 1)
    g_small["hgrn_lb_logits"] = _lower_bounds_bwd(hgrn_lb_logits.reshape(DEPTH * 2, HEAD_DIM), lb_shard).reshape(
        hgrn_lb_logits.shape)
    g_small["conv_dw_w"] = lax.dynamic_slice_in_dim(g_small["conv_dw_w"], chip * HEAD_DIM, HEAD_DIM, 2)
    res = _adamw("adamw_small", _pack([w[n] for n in small_names]), _pack([g_small[n] for n in small_names]), None,
                 _pack([m[n] for n in small_names]), _pack([v[n] for n in small_names]))
    unpacked = [_unpack(r, [w[n].shape for n in small_names]) for r in res]
    for i, n in enumerate(small_names):
        out[n] = [unpacked[k][i] for k in range(4)]

    return (loss, grad_x, *[out[n][0] for n in WEIGHTS], *[out[n][1] for n in WEIGHTS],
            *[out[n][2] for n in WEIGHTS], *[out[n][3] for n in WEIGHTS])
```

```python
import functools

import numpy as np
import jax
import jax.numpy as jnp
from jax import lax
from jax.experimental import pallas as pl
from jax.experimental.pallas import tpu as pltpu

F32, BF16 = jnp.float32, jnp.bfloat16

D_MODEL = 1024
DEPTH = 2
GRID_W = 64
D_ATTN, D_HGRN, D_CONV = 512, 256, 256
HEAD_DIM = 64
N_KV = 2
KV_LANES = D_ATTN // N_KV
ROPE_THETA = 10000.0
F_MIN = 1e-6
CONV_W = 31
CONV_PAD = 15
D_FF = 2816
D_IN = 2560
N_CHIP = 4
N_DEV = 8
IN_BLK = D_IN // N_CHIP
FF_BLK = D_FF // N_CHIP
OUT_BLK = D_MODEL // N_CHIP
EPS = 1e-6
LN_EPS = 1e-5
LR, B1, B2, ADAM_EPS, WD, STEP = 0.001, 0.9, 0.999, 1e-08, 0.01, 10
CHUNK = 16
HBLK = 256
CONV_TILE = 128
BWD_GROUP = 2
VMEM_LIMIT = 56 * 1024 * 1024

COL_Q, COL_K, COL_V = 0, 4, 5
COL_HQ, COL_FF, COL_FB, COL_HI, COL_HG, COL_CA, COL_CB = 3, 4, 5, 6, 7, 8, 9


def _params(sem=None):
    return pltpu.CompilerParams(dimension_semantics=sem, vmem_limit_bytes=VMEM_LIMIT)


def _sds(shape, dtype):
    return jax.ShapeDtypeStruct(tuple(shape), dtype)


def _full(shape):
    n = len(shape)
    return pl.BlockSpec(tuple(shape), lambda *_: (0,) * n)


def _sigmoid(x):
    return 0.5 * jnp.tanh(0.5 * x) + 0.5


def _gate_sigmoid(x):
    return 1.0 / (1.0 + jnp.exp(-x))


def _silu(x):
    return x * _sigmoid(x)


def _dsilu(x):
    s = _sigmoid(x)
    return s * (1.0 + x * (1.0 - s))


def _rowgroups(v):
    m, c = v.shape
    return v.reshape(m // 8, 8, c).sum(axis=0)


def _split2(x):
    hi = x.astype(BF16)
    lo = (x - hi.astype(F32)).astype(BF16)
    return hi, lo


def _rdot2(x, m):
    hi, lo = _split2(x)
    return (jnp.dot(hi, m, preferred_element_type=F32) + jnp.dot(lo, m, preferred_element_type=F32))


def _ldot3(m, x):
    hi = x.astype(BF16)
    r1 = x - hi.astype(F32)
    mid = r1.astype(BF16)
    lo = (r1 - mid.astype(F32)).astype(BF16)
    return (jnp.dot(m, hi, preferred_element_type=F32) + jnp.dot(m, mid, preferred_element_type=F32)
            + jnp.dot(m, lo, preferred_element_type=F32))


def _dot_nt(a, b):
    return lax.dot_general(a, b, (((1,), (1,)), ((), ())), preferred_element_type=F32)


def _dot_tn(a, b):
    return lax.dot_general(a, b, (((0,), (0,)), ((), ())), preferred_element_type=F32)


def _seg_matrix(n, seg, val):
    i = np.arange(n)
    return ((i[:, None] // seg) == (i[None, :] // seg)).astype(np.float32) * val


def _rot_matrix(n):
    r = np.zeros((n, n), np.float32)
    for i in range(n):
        if (i % 32) < 16:
            r[i + 16, i] = -1.0
        else:
            r[i - 16, i] = 1.0
    return r


def _rep_matrix():
    r = np.zeros((N_KV * HEAD_DIM, D_ATTN), np.float32)
    for kv in range(N_KV):
        for g in range(KV_LANES // HEAD_DIM):
            for d in range(HEAD_DIM):
                r[HEAD_DIM * kv + d, KV_LANES * kv + HEAD_DIM * g + d] = 1.0
    return r


def _cumsum_matrix(rev):
    i = np.arange(HBLK)
    same = (i[:, None] // CHUNK) == (i[None, :] // CHUNK)
    tri = (i[None, :] >= i[:, None]) if rev else (i[None, :] <= i[:, None])
    return (same & tri).astype(np.float32)


def _sel_matrices():
    sel = np.zeros((CHUNK, CHUNK * CHUNK), np.float32)
    selt = np.zeros((CHUNK, CHUNK * CHUNK), np.float32)
    for t in range(CHUNK):
        for s in range(CHUNK):
            sel[t, t * CHUNK + s] = 1.0
            selt[s, t * CHUNK + s] = 1.0
    return sel, selt


def _bf(a):
    return jnp.asarray(a, dtype=BF16)


def _mm(name, grid, pairs, extras, outs, epilogue, acc=None, sem=None):
    n_p, n_e, n_o = len(pairs), len(extras), len(outs)

    def body(*refs):
        ab = refs[:2 * n_p]
        ex = refs[2 * n_p:2 * n_p + n_e]
        out = refs[2 * n_p + n_e:2 * n_p + n_e + n_o]
        scr = refs[2 * n_p + n_e + n_o:]
        tot = None
        for i in range(n_p):
            a = ab[2 * i][...]
            b = ab[2 * i + 1][...]
            if a.ndim == 3:
                a = a.reshape(-1, a.shape[-1])
            if b.ndim == 3:
                b = b.reshape(-1, b.shape[-1])
            r = lax.dot_general(a.astype(BF16), b.astype(BF16), pairs[i][4], preferred_element_type=F32)
            tot = r if tot is None else tot + r

        def finish(total):
            res = epilogue(total, *[e[...] for e in ex])
            for o_ref, val in zip(out, res):
                o_ref[...] = val.astype(o_ref.dtype)

        if acc is None:
            finish(tot)
        else:
            k = pl.program_id(acc[0])

            @pl.when(k == 0)
            def _():
                scr[0][...] = tot

            @pl.when(k > 0)
            def _():
                scr[0][...] += tot

            @pl.when(k == grid[acc[0]] - 1)
            def _():
                finish(scr[0][...])

    args, in_specs = [], []
    for a, a_spec, b, b_spec, _ in pairs:
        args += [a, b]
        in_specs += [a_spec, b_spec]
    for e, e_spec in extras:
        args.append(e)
        in_specs.append(e_spec)
    if sem is None:
        sem = tuple("arbitrary" if (acc is not None and i == acc[0]) else "parallel" for i in range(len(grid)))
    return pl.pallas_call(
        body, name=name, grid=grid, in_specs=in_specs,
        out_specs=[o[1] for o in outs], out_shape=[o[0] for o in outs],
        scratch_shapes=[] if acc is None else [pltpu.VMEM(acc[1], F32)],
        compiler_params=_params(sem),
    )(*args)


NN = (((1,), (0,)), ((), ()))
NT = (((1,), (1,)), ((), ()))
TN = (((0,), (0,)), ((), ()))


def _row_tile(t):
    return min(256, t)


def _rms_fwd(name, x, w, deps=()):
    t, d = x.shape
    tm = _row_tile(t)

    def body(x_ref, w_ref, *rest):
        o_ref = rest[-1]
        xv = x_ref[...]
        r = lax.rsqrt(jnp.mean(xv * xv, axis=-1, keepdims=True) + EPS)
        o_ref[...] = (xv * r * w_ref[...]).astype(BF16)

    return pl.pallas_call(
        body, name=name, grid=(t // tm,),
        in_specs=[pl.BlockSpec((tm, d), lambda i: (i, 0)), _full((1, d))] + [_full(a.shape) for a in deps],
        out_specs=pl.BlockSpec((tm, d), lambda i: (i, 0)), out_shape=_sds((t, d), BF16),
        compiler_params=_params(("parallel",)),
    )(x, w, *deps)


def _rms_bwd(name, x, w, dh, dres, deps=()):
    t, d = x.shape
    tm = _row_tile(t)

    def body(x_ref, w_ref, dh_ref, dres_ref, *rest):
        dx_ref, dw_ref = rest[-2:]
        xv = x_ref[...]
        r = lax.rsqrt(jnp.mean(xv * xv, axis=-1, keepdims=True) + EPS)
        dy = dh_ref[...]
        gw = dy * w_ref[...]
        dx_ref[...] = dres_ref[...] + r * gw - xv * (r * r * r) * jnp.mean(gw * xv, axis=-1, keepdims=True)

        @pl.when(pl.program_id(0) == 0)
        def _():
            dw_ref[...] = jnp.zeros_like(dw_ref)

        dw_ref[...] += _rowgroups(dy * xv * r)

    tile = pl.BlockSpec((tm, d), lambda i: (i, 0))
    return pl.pallas_call(
        body, name=name, grid=(t // tm,),
        in_specs=[tile, _full((1, d)), tile, tile] + [_full(a.shape) for a in deps],
        out_specs=[tile, _full((8, d))], out_shape=[_sds((t, d), F32), _sds((8, d), F32)],
        compiler_params=_params(("arbitrary",)),
    )(x, w, dh, dres, *deps)


def _loss_kernel(y, target):
    t, d = y.shape
    tm = _row_tile(t)

    def body(y_ref, t_ref, dy_ref, acc_ref):
        e = y_ref[...] - t_ref[...]
        dy_ref[...] = e * (1.0 / d)

        @pl.when(pl.program_id(0) == 0)
        def _():
            acc_ref[...] = jnp.zeros_like(acc_ref)

        acc_ref[...] += _rowgroups(e * e)

    tile = pl.BlockSpec((tm, d), lambda i: (i, 0))
    return pl.pallas_call(
        body, name="loss_head", grid=(t // tm,), in_specs=[tile, tile],
        out_specs=[tile, _full((8, d))], out_shape=[_sds((t, d), F32), _sds((8, d), F32)],
        compiler_params=_params(("arbitrary",)),
    )(y, target)


def _rope_tables(s):
    rows = s // GRID_W
    row_id = jnp.repeat(jnp.arange(rows, dtype=F32), GRID_W)
    col_id = jnp.tile(jnp.arange(GRID_W, dtype=F32), rows)
    half = HEAD_DIM // 2
    inv_freq = ROPE_THETA ** (-jnp.arange(0, half, 2, dtype=F32) / half)
    ang_r = row_id[:, None] * inv_freq[None, :]
    ang_c = col_id[:, None] * inv_freq[None, :]
    ang = jnp.concatenate([ang_r, ang_r, ang_c, ang_c], axis=-1)
    return jnp.cos(ang).astype(F32), jnp.sin(ang).astype(F32)


def _attn_consts():
    return dict(
        seg_q=_bf(_seg_matrix(D_ATTN, HEAD_DIM, 1.0 / HEAD_DIM)),
        seg_k=_bf(_seg_matrix(N_KV * HEAD_DIM, HEAD_DIM, 1.0 / HEAD_DIM)),
        rot_q=_bf(_rot_matrix(D_ATTN)), rot_k=_bf(_rot_matrix(N_KV * HEAD_DIM)),
        rep=_bf(_rep_matrix()), rep_t=_bf(_rep_matrix().T))


def _attn_prep(name, proj, s, tabs, qw, kw, ac):
    t = proj.shape[0]
    tm = _row_tile(s)
    nst = s // tm
    kw_ = N_KV * HEAD_DIM

    def body(q_ref, k_ref, v_ref, cq_ref, sq_ref, ck_ref, sk_ref, qw_ref, kw_ref,
             segq_ref, segk_ref, rotq_ref, rotk_ref, rep_ref, qn_ref, kr_ref, vr_ref):
        q = q_ref[...]
        r = lax.rsqrt(jnp.dot((q * q).astype(BF16), segq_ref[...], preferred_element_type=F32) + EPS)
        qn = q * r * qw_ref[...]
        qr = qn * cq_ref[...] + _rdot2(qn, rotq_ref[...]) * sq_ref[...]
        qn_ref[...] = (qr * (HEAD_DIM ** -0.5)).astype(BF16)
        k = k_ref[...]
        rk = lax.rsqrt(jnp.dot((k * k).astype(BF16), segk_ref[...], preferred_element_type=F32) + EPS)
        kn = k * rk * kw_ref[...]
        kr = kn * ck_ref[...] + _rdot2(kn, rotk_ref[...]) * sk_ref[...]
        kr_ref[...] = jnp.dot(kr.astype(BF16), rep_ref[...], preferred_element_type=F32).astype(BF16)
        vr_ref[...] = jnp.dot(v_ref[...].astype(BF16), rep_ref[...], preferred_element_type=F32).astype(BF16)

    wide = pl.BlockSpec((tm, D_ATTN), lambda i: (i, 0))
    tabq = pl.BlockSpec((tm, D_ATTN), lambda i: (i % nst, 0))
    tabk = pl.BlockSpec((tm, kw_), lambda i: (i % nst, 0))
    return pl.pallas_call(
        body, name=name, grid=(t // tm,),
        in_specs=[pl.BlockSpec((tm, D_ATTN), lambda i: (i, COL_Q)), pl.BlockSpec((tm, kw_), lambda i: (i, COL_K)),
                  pl.BlockSpec((tm, kw_), lambda i: (i, COL_V)), tabq, tabq, tabk, tabk,
                  _full((1, D_ATTN)), _full((1, kw_)), _full((D_ATTN, D_ATTN)), _full((kw_, kw_)),
                  _full((D_ATTN, D_ATTN)), _full((kw_, kw_)), _full((kw_, D_ATTN))],
        out_specs=[wide, wide, wide], out_shape=[_sds((t, D_ATTN), BF16)] * 3,
        compiler_params=_params(("parallel",)),
    )(proj, proj, proj, tabs["cq"], tabs["sq"], tabs["ck"], tabs["sk"], qw, kw,
      ac["seg_q"], ac["seg_k"], ac["rot_q"], ac["rot_k"], ac["rep"])


def _attn_prep_bwd(name, proj, s, tabs, qw, kw, ac, dqs, dkr, dvr):
    t = proj.shape[0]
    tm = _row_tile(s)
    nst = s // tm
    kw_ = N_KV * HEAD_DIM
    wout = D_ATTN + 2 * kw_

    def norm_rope_bwd(x, w, cos, sin, seg, rot, d_roped):
        dn = d_roped * cos - _rdot2(d_roped * sin, rot)
        r = lax.rsqrt(jnp.dot((x * x).astype(BF16), seg, preferred_element_type=F32) + EPS)
        gw = dn * w
        dx = r * gw - x * (r * r * r) * _rdot2(gw * x, seg)
        return dx, _rowgroups(dn * x * r)

    def body(q_ref, k_ref, cq_ref, sq_ref, ck_ref, sk_ref, qw_ref, kw_ref, segq_ref, segk_ref, rotq_ref, rotk_ref,
             rept_ref, dqs_ref, dkr_ref, dvr_ref, dp_ref, dqw_ref, dkw_ref):
        dq, dqw = norm_rope_bwd(q_ref[...], qw_ref[...], cq_ref[...], sq_ref[...], segq_ref[...], rotq_ref[...],
                                dqs_ref[...] * (HEAD_DIM ** -0.5))
        dk_roped = _rdot2(dkr_ref[...], rept_ref[...])
        dk, dkw = norm_rope_bwd(k_ref[...], kw_ref[...], ck_ref[...], sk_ref[...], segk_ref[...], rotk_ref[...],
                                dk_roped)
        dv = _rdot2(dvr_ref[...], rept_ref[...])
        dp_ref[:, 0:D_ATTN] = dq.astype(BF16)
        dp_ref[:, D_ATTN:D_ATTN + kw_] = dk.astype(BF16)
        dp_ref[:, D_ATTN + kw_:wout] = dv.astype(BF16)

        @pl.when(pl.program_id(0) == 0)
        def _():
            dqw_ref[...] = jnp.zeros_like(dqw_ref)
            dkw_ref[...] = jnp.zeros_like(dkw_ref)

        dqw_ref[...] += dqw
        dkw_ref[...] += dkw

    wide = pl.BlockSpec((tm, D_ATTN), lambda i: (i, 0))
    tabq = pl.BlockSpec((tm, D_ATTN), lambda i: (i % nst, 0))
    tabk = pl.BlockSpec((tm, kw_), lambda i: (i % nst, 0))
    return pl.pallas_call(
        body, name=name, grid=(t // tm,),
        in_specs=[pl.BlockSpec((tm, D_ATTN), lambda i: (i, COL_Q)), pl.BlockSpec((tm, kw_), lambda i: (i, COL_K)),
                  tabq, tabq, tabk, tabk, _full((1, D_ATTN)), _full((1, kw_)),
                  _full((D_ATTN, D_ATTN)), _full((kw_, kw_)), _full((D_ATTN, D_ATTN)), _full((kw_, kw_)),
                  _full((D_ATTN, kw_)), wide, wide, wide],
        out_specs=[pl.BlockSpec((tm, wout), lambda i: (i, 0)), _full((8, D_ATTN)), _full((8, kw_))],
        out_shape=[_sds((t, wout), BF16), _sds((8, D_ATTN), F32), _sds((8, kw_), F32)],
        compiler_params=_params(("arbitrary",)),
    )(proj, proj, tabs["cq"], tabs["sq"], tabs["ck"], tabs["sk"], qw, kw,
      ac["seg_q"], ac["seg_k"], ac["rot_q"], ac["rot_k"], ac["rep_t"], dqs, dkr, dvr)


def _attn_tile(s):
    return min(256, s)


def _head_masks(shape):
    lane = lax.broadcasted_iota(jnp.int32, shape, 1)
    return [(lane // HEAD_DIM) == g for g in range(KV_LANES // HEAD_DIM)]


def _attn_fwd(name, qn, kr, vr, b, s):
    t = qn.shape[0]
    tq = _attn_tile(s)
    nq = s // tq

    def body(q_ref, k_ref, v_ref, o_ref):
        q = q_ref[...]
        k = k_ref[...]
        v = v_ref[...]
        acc = jnp.zeros((tq, KV_LANES), F32)
        for mask in _head_masks((tq, KV_LANES)):
            sc = _dot_nt(jnp.where(mask, q, jnp.zeros_like(q)), k)
            p = jnp.exp(sc - jnp.max(sc, axis=-1, keepdims=True))
            inv = 1.0 / jnp.sum(p, axis=-1, keepdims=True)
            og = jnp.dot(p.astype(BF16), v, preferred_element_type=F32) * inv
            acc = jnp.where(mask, og, acc)
        o_ref[...] = acc

    return pl.pallas_call(
        body, name=name, grid=(b, N_KV, nq),
        in_specs=[pl.BlockSpec((tq, KV_LANES), lambda bi, kv, i: (bi * nq + i, kv)),
                  pl.BlockSpec((s, KV_LANES), lambda bi, kv, i: (bi, kv)),
                  pl.BlockSpec((s, KV_LANES), lambda bi, kv, i: (bi, kv))],
        out_specs=pl.BlockSpec((tq, KV_LANES), lambda bi, kv, i: (bi * nq + i, kv)),
        out_shape=_sds((t, D_ATTN), F32),
        compiler_params=_params(("parallel", "parallel", "parallel")),
    )(qn, kr, vr)


def _attn_bwd(name, qn, kr, vr, do, b, s):
    t = qn.shape[0]
    tq = _attn_tile(s)
    nq = s // tq

    def body(q_ref, k_ref, v_ref, do_ref, dq_ref, dk_ref, dv_ref):
        @pl.when(pl.program_id(2) == 0)
        def _():
            dk_ref[...] = jnp.zeros_like(dk_ref)
            dv_ref[...] = jnp.zeros_like(dv_ref)

        q = q_ref[...]
        k = k_ref[...]
        v = v_ref[...]
        dout = do_ref[...].astype(BF16)
        masks = _head_masks((tq, KV_LANES))
        q4 = jnp.concatenate([jnp.where(m, q, jnp.zeros_like(q)) for m in masks], axis=0)
        do4 = jnp.concatenate([jnp.where(m, dout, jnp.zeros_like(dout)) for m in masks], axis=0)
        sc = _dot_nt(q4, k)
        p = jnp.exp(sc - jnp.max(sc, axis=-1, keepdims=True))
        p = p * (1.0 / jnp.sum(p, axis=-1, keepdims=True))
        dp = _dot_nt(do4, v)
        ds = (p * (dp - jnp.sum(p * dp, axis=-1, keepdims=True))).astype(BF16)
        dq4 = jnp.dot(ds, k, preferred_element_type=F32)
        dq = jnp.zeros((tq, KV_LANES), F32)
        for g, m in enumerate(masks):
            dq = jnp.where(m, dq4[g * tq:(g + 1) * tq, :], dq)
        dq_ref[...] = dq
        dk_ref[...] += _dot_tn(ds, q4)
        dv_ref[...] += _dot_tn(p.astype(BF16), do4)

    qspec = pl.BlockSpec((tq, KV_LANES), lambda bi, kv, i: (bi * nq + i, kv))
    kspec = pl.BlockSpec((s, KV_LANES), lambda bi, kv, i: (bi, kv))
    return pl.pallas_call(
        body, name=name, grid=(b, N_KV, nq),
        in_specs=[qspec, kspec, kspec, qspec],
        out_specs=[qspec, kspec, kspec], out_shape=[_sds((t, D_ATTN), F32)] * 3,
        compiler_params=_params(("parallel", "parallel", "arbitrary")),
    )(qn, kr, vr, do)


def _hgrn_consts(rev):
    sel, selt = _sel_matrices()
    cs = _cumsum_matrix(rev)
    return dict(cs=_bf(cs), cs_t=_bf(cs.T), seg=_bf(_seg_matrix(D_HGRN, HEAD_DIM, 1.0)),
                bd=jnp.asarray(_seg_matrix(D_HGRN, HEAD_DIM, 1.0), F32),
                sel=_bf(sel), selt=_bf(selt), seld=_bf(sel - selt))


def _gates(z, lb):
    sig = _gate_sigmoid(z)
    f = lb + (1.0 - lb) * sig
    g = jnp.log(jnp.maximum(f, F_MIN))
    sn = _gate_sigmoid(-z)
    return sig, f, g, sn, (1.0 - lb) * sn


def _pair_decay(b, rev):
    row = lax.broadcasted_iota(jnp.int32, (CHUNK, D_HGRN), 0)
    parts = []
    for t in range(CHUNK):
        m = (row >= t) if rev else (row <= t)
        parts.append(jnp.where(m, jnp.exp(jnp.minimum(b[t:t + 1, :] - b, 0.0)), 0.0))
    return jnp.concatenate(parts, axis=0)


def _rows_rep(a):
    return jnp.concatenate([jnp.broadcast_to(a[t:t + 1, :], a.shape) for t in range(CHUNK)], axis=0)


def _tile_rows(a):
    return jnp.concatenate([a] * CHUNK, axis=0)


def _hgrn_specs(b, s, rev):
    nb = s // HBLK

    def blk(j):
        return (nb - 1 - j) if rev else j

    def col(c):
        return pl.BlockSpec((HBLK, D_HGRN), lambda bi, j: (bi * nb + blk(j), c))

    return nb, blk, col


def _hgrn_fwd(name, proj, lb, b, s, rev, hc):
    t = proj.shape[0]
    nb, blk, col = _hgrn_specs(b, s, rev)
    n_ch = HBLK // CHUNK
    last = 0 if rev else CHUNK - 1

    def body(q_ref, z_ref, v_ref, lb_ref, cs_ref, seg_ref, bd_ref, sel_ref, o_ref, st_ref, state, b_scr, k_scr):
        @pl.when(pl.program_id(1) == 0)
        def _():
            state[...] = jnp.zeros_like(state)

        st_ref[...] = state[...]
        _, _, g, _, kk = _gates(z_ref[...], lb_ref[...])
        k_scr[...] = kk
        b_scr[...] = _ldot3(cs_ref[...], g)

        def chunk(i, carry):
            c = (n_ch - 1 - i) if rev else i
            rows = pl.ds(pl.multiple_of(c * CHUNK, CHUNK), CHUNK)
            q = q_ref[rows, :]
            k = k_scr[rows, :]
            v = v_ref[rows, :]
            bb = b_scr[rows, :]
            bl = bb[last:last + 1, :]
            pairs = _pair_decay(bb, rev) * _rows_rep(q) * _tile_rows(k)
            a = jnp.dot(pairs.astype(BF16), seg_ref[...], preferred_element_type=F32)
            o_intra = jnp.dot(sel_ref[...], (a * _tile_rows(v)).astype(BF16), preferred_element_type=F32)
            st = state[...]
            o_inter = _dot_nt((q * jnp.exp(bb)).astype(BF16), st.astype(BF16))
            o_ref[rows, :] = o_intra + o_inter
            ke = k * jnp.exp(bl - bb)
            state[...] = st * jnp.exp(bl) + bd_ref[...] * _dot_tn(v.astype(BF16), ke.astype(BF16))
            return carry

        lax.fori_loop(0, n_ch, chunk, 0)

    sq = (D_HGRN, D_HGRN)
    return pl.pallas_call(
        body, name=name, grid=(b, nb),
        in_specs=[col(COL_HQ), col(COL_FB if rev else COL_FF), col(COL_HI), _full((1, D_HGRN)),
                  _full((HBLK, HBLK)), _full(sq), _full(sq), _full((CHUNK, CHUNK * CHUNK))],
        out_specs=[pl.BlockSpec((HBLK, D_HGRN), lambda bi, j: (bi * nb + blk(j), 0)),
                   pl.BlockSpec((None,) + sq, lambda bi, j: (bi * nb + blk(j), 0, 0))],
        out_shape=[_sds((t, D_HGRN), F32), _sds((b * nb,) + sq, F32)],
        scratch_shapes=[pltpu.VMEM(sq, F32), pltpu.VMEM((HBLK, D_HGRN), F32), pltpu.VMEM((HBLK, D_HGRN), F32)],
        compiler_params=_params(("parallel", "arbitrary")),
    )(proj, proj, proj, lb, hc["cs"], hc["seg"], hc["bd"], hc["sel"])


def _hgrn_bwd(name, proj, lb, st_blk, do, dq_prev, dv_prev, b, s, rev, hc):
    t = proj.shape[0]
    nb = s // HBLK
    n_ch = HBLK // CHUNK
    last = 0 if rev else CHUNK - 1

    def blk(j):
        return j if rev else (nb - 1 - j)

    def col(c):
        return pl.BlockSpec((HBLK, D_HGRN), lambda bi, j: (bi * nb + blk(j), c))

    def body(q_ref, z_ref, v_ref, lb_ref, st_ref, do_ref, dqp_ref, dvp_ref, cs_ref, cst_ref, seg_ref, bd_ref,
             sel_ref, selt_ref, seld_ref, dq_ref, dv_ref, dz_ref, dlb_ref,
             dstate, states, b_scr, k_scr, db_scr, dk_scr):
        first = jnp.logical_and(pl.program_id(0) == 0, pl.program_id(1) == 0)

        @pl.when(first)
        def _():
            dlb_ref[...] = jnp.zeros_like(dlb_ref)

        @pl.when(pl.program_id(1) == 0)
        def _():
            dstate[...] = jnp.zeros_like(dstate)

        lbv = lb_ref[...]
        z = z_ref[...]
        sig, f, g, sn, kk = _gates(z, lbv)
        k_scr[...] = kk
        b_scr[...] = _ldot3(cs_ref[...], g)

        def rows_of(c):
            return pl.ds(pl.multiple_of(c * CHUNK, CHUNK), CHUNK)

        def replay(i, st):
            c = (n_ch - 1 - i) if rev else i
            rows = rows_of(c)
            states[c] = st
            bb = b_scr[rows, :]
            bl = bb[last:last + 1, :]
            ke = k_scr[rows, :] * jnp.exp(bl - bb)
            return st * jnp.exp(bl) + bd_ref[...] * _dot_tn(v_ref[rows, :].astype(BF16), ke.astype(BF16))

        lax.fori_loop(0, n_ch, replay, st_ref[...])
        row = lax.broadcasted_iota(jnp.int32, (CHUNK, D_HGRN), 0)

        def chunk(i, carry):
            c = i if rev else (n_ch - 1 - i)
            rows = rows_of(c)
            q = q_ref[rows, :]
            k = k_scr[rows, :]
            v = v_ref[rows, :]
            bb = b_scr[rows, :]
            dout = do_ref[rows, :]
            bl = bb[last:last + 1, :]
            st_p = states[c]
            dst_n = dstate[...]
            eb = jnp.exp(bb)
            ebl = jnp.exp(bl - bb)
            ebl_last = jnp.exp(bl)
            qe = q * eb
            ke = k * ebl
            dob = dout.astype(BF16)
            dstb = dst_n.astype(BF16)
            dqe = jnp.dot(dob, st_p.astype(BF16), preferred_element_type=F32)
            dke = jnp.dot(v.astype(BF16), dstb, preferred_element_type=F32)
            dv = _dot_nt(ke.astype(BF16), dstb)
            dbl = jnp.sum(dst_n * st_p, axis=0, keepdims=True) * ebl_last + jnp.sum(dke * ke, axis=0, keepdims=True)
            dq = dqe * eb
            dk = dke * ebl
            db = dqe * qe - dke * ke
            dec = _pair_decay(bb, rev)
            q_rep = _rows_rep(q)
            k_til = _tile_rows(k)
            do_rep = _rows_rep(dout)
            pairs = dec * q_rep * k_til
            a = jnp.dot(pairs.astype(BF16), seg_ref[...], preferred_element_type=F32)
            wb = jnp.dot((_tile_rows(v) * do_rep).astype(BF16), seg_ref[...], preferred_element_type=F32)
            gdec = wb * dec
            dq = dq + jnp.dot(sel_ref[...], (gdec * k_til).astype(BF16), preferred_element_type=F32)
            dk = dk + jnp.dot(selt_ref[...], (gdec * q_rep).astype(BF16), preferred_element_type=F32)
            dv = dv + jnp.dot(selt_ref[...], (a * do_rep).astype(BF16), preferred_element_type=F32)
            db = db + jnp.dot(seld_ref[...], (wb * pairs).astype(BF16), preferred_element_type=F32)
            db = db + jnp.where(row == last, dbl, 0.0)
            dq_ref[rows, :] = dq + dqp_ref[rows, :]
            dv_ref[rows, :] = dv + dvp_ref[rows, :]
            dk_scr[rows, :] = dk
            db_scr[rows, :] = db
            dstate[...] = dst_n * ebl_last + bd_ref[...] * _dot_tn(dob, qe.astype(BF16))
            return carry

        lax.fori_loop(0, n_ch, chunk, 0)
        hi, lo = _split2(db_scr[...])
        dg = (jnp.dot(cst_ref[...], hi, preferred_element_type=F32)
              + jnp.dot(cst_ref[...], lo, preferred_element_type=F32))
        dgf = jnp.where(f > F_MIN, dg / f, 0.0)
        dk = dk_scr[...]
        dz_ref[...] = dgf * (1.0 - lbv) * sig * (1.0 - sig) - dk * (1.0 - lbv) * sn * (1.0 - sn)
        dlb_ref[...] += _rowgroups(dgf * (1.0 - sig) - dk * sn)

    sq = (D_HGRN, D_HGRN)
    blk0 = pl.BlockSpec((HBLK, D_HGRN), lambda bi, j: (bi * nb + blk(j), 0))
    pairs_shape = (CHUNK, CHUNK * CHUNK)
    return pl.pallas_call(
        body, name=name, grid=(b, nb),
        in_specs=[col(COL_HQ), col(COL_FB if rev else COL_FF), col(COL_HI), _full((1, D_HGRN)),
                  pl.BlockSpec((None,) + sq, lambda bi, j: (bi * nb + blk(j), 0, 0)), blk0, blk0, blk0,
                  _full((HBLK, HBLK)), _full((HBLK, HBLK)), _full(sq), _full(sq),
                  _full(pairs_shape), _full(pairs_shape), _full(pairs_shape)],
        out_specs=[blk0, blk0, blk0, _full((8, D_HGRN))],
        out_shape=[_sds((t, D_HGRN), F32)] * 3 + [_sds((8, D_HGRN), F32)],
        scratch_shapes=[pltpu.VMEM(sq, F32), pltpu.VMEM((n_ch,) + sq, F32)] + [pltpu.VMEM((HBLK, D_HGRN), F32)] * 4,
        compiler_params=_params(("arbitrary", "arbitrary")),
    )(proj, proj, proj, lb, st_blk, do, dq_prev, dv_prev,
      hc["cs"], hc["cs_t"], hc["seg"], hc["bd"], hc["sel"], hc["selt"], hc["seld"])


def _scan_chunk_fwd(c, rev, q_ref, v_ref, k_scr, b_scr, state, o_ref, seg_ref, bd_ref, sel_ref):
    last = 0 if rev else CHUNK - 1
    rows = pl.ds(pl.multiple_of(c * CHUNK, CHUNK), CHUNK)
    q = q_ref[rows, :]
    k = k_scr[rows, :]
    v = v_ref[rows, :]
    bb = b_scr[rows, :]
    bl = bb[last:last + 1, :]
    pairs = _pair_decay(bb, rev) * _rows_rep(q) * _tile_rows(k)
    a = jnp.dot(pairs.astype(BF16), seg_ref[...], preferred_element_type=F32)
    o_intra = jnp.dot(sel_ref[...], (a * _tile_rows(v)).astype(BF16), preferred_element_type=F32)
    st = state[...]
    o_inter = _dot_nt((q * jnp.exp(bb)).astype(BF16), st.astype(BF16))
    o_ref[rows, :] = o_intra + o_inter
    ke = k * jnp.exp(bl - bb)
    state[...] = st * jnp.exp(bl) + bd_ref[...] * _dot_tn(v.astype(BF16), ke.astype(BF16))


def _scan_chunks_fwd(chains, seg_ref, bd_ref, sel_ref):
    work = []
    for c, rev, q_ref, v_ref, k_scr, b_scr, state, o_ref in chains:
        last = 0 if rev else CHUNK - 1
        rows = pl.ds(pl.multiple_of(c * CHUNK, CHUNK), CHUNK)
        q = q_ref[rows, :]
        k = k_scr[rows, :]
        v = v_ref[rows, :]
        bb = b_scr[rows, :]
        bl = bb[last:last + 1, :]
        st = state[...]
        work.append(dict(
            rows=rows, v=v, st=st, state=state, o_ref=o_ref, decay=jnp.exp(bl),
            pairs=(_pair_decay(bb, rev) * _rows_rep(q) * _tile_rows(k)).astype(BF16),
            qe=(q * jnp.exp(bb)).astype(BF16), ke=(k * jnp.exp(bl - bb)).astype(BF16), st_b=st.astype(BF16)))
    for w in work:
        w["a"] = jnp.dot(w["pairs"], seg_ref[...], preferred_element_type=F32)
        w["o_inter"] = _dot_nt(w["qe"], w["st_b"])
        w["upd"] = _dot_tn(w["v"].astype(BF16), w["ke"])
    for w in work:
        w["av"] = (w["a"] * _tile_rows(w["v"])).astype(BF16)
    for w in work:
        w["o_ref"][w["rows"], :] = jnp.dot(sel_ref[...], w["av"], preferred_element_type=F32) + w["o_inter"]
        w["state"][...] = w["st"] * w["decay"] + bd_ref[...] * w["upd"]


def _hgrn_fwd2(name, proj, lb_f, lb_b, b, s, hc_f, hc_b):
    t = proj.shape[0]
    nb = s // HBLK
    n_ch = HBLK // CHUNK
    n_chain = 2 * b

    def body(qf_ref, zf_ref, vf_ref, qb_ref, zb_ref, vb_ref, lbf_ref, lbb_ref, csf_ref, csb_ref, seg_ref, bd_ref,
             sel_ref, of_ref, ob_ref, stf_ref, stb_ref, *scr):
        state, b_scr, k_scr = scr[:n_chain], scr[n_chain:2 * n_chain], scr[2 * n_chain:]

        @pl.when(pl.program_id(0) == 0)
        def _():
            for st0 in state:
                st0[...] = jnp.zeros_like(st0)

        chains = []
        for bi in range(b):
            chains.append((False, qf_ref.at[bi], zf_ref.at[bi], vf_ref.at[bi], lbf_ref, csf_ref, of_ref.at[bi],
                           stf_ref.at[bi], 2 * bi))
            chains.append((True, qb_ref.at[bi], zb_ref.at[bi], vb_ref.at[bi], lbb_ref, csb_ref, ob_ref.at[bi],
                           stb_ref.at[bi], 2 * bi + 1))
        for rev, q, z, v, lb, cs, o, st, ci in chains:
            st[...] = state[ci][...]
            _, _, g, _, kk = _gates(z[...], lb[...])
            k_scr[ci][...] = kk
            b_scr[ci][...] = _ldot3(cs[...], g)

        def chunk(i, carry):
            _scan_chunks_fwd([((n_ch - 1 - i) if rev else i, rev, q, v, k_scr[ci], b_scr[ci], state[ci], o)
                              for rev, q, z, v, lb, cs, o, st, ci in chains], seg_ref, bd_ref, sel_ref)
            return carry

        lax.fori_loop(0, n_ch, chunk, 0)

    def col(c, rev):
        return pl.BlockSpec((b, HBLK, D_HGRN), lambda j: (0, (nb - 1 - j) if rev else j, c))

    def st_spec(rev):
        return pl.BlockSpec((b, None, D_HGRN, D_HGRN), lambda j: (0, (nb - 1 - j) if rev else j, 0, 0))

    sq = (D_HGRN, D_HGRN)
    proj3 = proj.reshape(b, s, proj.shape[1])
    o_fw, o_bw, st_fw, st_bw = pl.pallas_call(
        body, name=name, grid=(nb,),
        in_specs=[col(COL_HQ, False), col(COL_FF, False), col(COL_HI, False),
                  col(COL_HQ, True), col(COL_FB, True), col(COL_HI, True),
                  _full((1, D_HGRN)), _full((1, D_HGRN)), _full((HBLK, HBLK)), _full((HBLK, HBLK)),
                  _full(sq), _full(sq), _full((CHUNK, CHUNK * CHUNK))],
        out_specs=[col(0, False), col(0, True), st_spec(False), st_spec(True)],
        out_shape=[_sds((b, s, D_HGRN), F32)] * 2 + [_sds((b, nb) + sq, F32)] * 2,
        scratch_shapes=[pltpu.VMEM(sq, F32)] * n_chain + [pltpu.VMEM((HBLK, D_HGRN), F32)] * (2 * n_chain),
        compiler_params=_params(("arbitrary",)),
    )(proj3, proj3, proj3, proj3, proj3, proj3, lb_f, lb_b, hc_f["cs"], hc_b["cs"], hc_f["seg"], hc_f["bd"],
      hc_f["sel"])
    return o_fw.reshape(t, D_HGRN), o_bw.reshape(t, D_HGRN), st_fw, st_bw


def _scan_replay(c, rev, st, v_ref, k_scr, b_scr, states, bd_ref):
    last = 0 if rev else CHUNK - 1
    rows = pl.ds(pl.multiple_of(c * CHUNK, CHUNK), CHUNK)
    states[c] = st
    bb = b_scr[rows, :]
    bl = bb[last:last + 1, :]
    ke = k_scr[rows, :] * jnp.exp(bl - bb)
    return st * jnp.exp(bl) + bd_ref[...] * _dot_tn(v_ref[rows, :].astype(BF16), ke.astype(BF16))


def _scan_replays(chains, bd_ref):
    work = []
    for c, rev, st, v_ref, k_scr, b_scr, states in chains:
        last = 0 if rev else CHUNK - 1
        rows = pl.ds(pl.multiple_of(c * CHUNK, CHUNK), CHUNK)
        states[c] = st
        bb = b_scr[rows, :]
        bl = bb[last:last + 1, :]
        work.append((st, jnp.exp(bl), v_ref[rows, :].astype(BF16), (k_scr[rows, :] * jnp.exp(bl - bb)).astype(BF16)))
    upds = [_dot_tn(v, ke) for _, _, v, ke in work]
    return tuple(st * decay + bd_ref[...] * upd for (st, decay, _, _), upd in zip(work, upds))


def _scan_chunks_bwd(chains, seg_ref, bd_ref, sel_ref, selt_ref, seld_ref):
    row = lax.broadcasted_iota(jnp.int32, (CHUNK, D_HGRN), 0)
    work = []
    for c, rev, q_ref, v_ref, do_ref, k_scr, b_scr, states, dstate, dq_ref, dv_ref, dk_scr, db_scr in chains:
        last = 0 if rev else CHUNK - 1
        rows = pl.ds(pl.multiple_of(c * CHUNK, CHUNK), CHUNK)
        q = q_ref[rows, :]
        k = k_scr[rows, :]
        v = v_ref[rows, :]
        bb = b_scr[rows, :]
        dout = do_ref[rows, :]
        bl = bb[last:last + 1, :]
        st_p = states[c]
        dst_n = dstate[...]
        eb = jnp.exp(bb)
        ebl = jnp.exp(bl - bb)
        qe = q * eb
        ke = k * ebl
        dec = _pair_decay(bb, rev)
        q_rep = _rows_rep(q)
        k_til = _tile_rows(k)
        do_rep = _rows_rep(dout)
        pairs = dec * q_rep * k_til
        work.append(dict(
            rows=rows, last=last, eb=eb, ebl=ebl, ebl_last=jnp.exp(bl), qe=qe, ke=ke, dec=dec, q_rep=q_rep, k_til=k_til,
            do_rep=do_rep, pairs=pairs, st_p=st_p, dst_n=dst_n, dstate=dstate, dq_ref=dq_ref, dv_ref=dv_ref,
            dk_scr=dk_scr, db_scr=db_scr, dob=dout.astype(BF16), dstb=dst_n.astype(BF16), vb=v.astype(BF16),
            pairs_b=pairs.astype(BF16), vdo_b=(_tile_rows(v) * do_rep).astype(BF16)))
    for w in work:
        w["dqe"] = jnp.dot(w["dob"], w["st_p"].astype(BF16), preferred_element_type=F32)
        w["dke"] = jnp.dot(w["vb"], w["dstb"], preferred_element_type=F32)
        w["dv"] = _dot_nt(w["ke"].astype(BF16), w["dstb"])
        w["a"] = jnp.dot(w["pairs_b"], seg_ref[...], preferred_element_type=F32)
        w["wb"] = jnp.dot(w["vdo_b"], seg_ref[...], preferred_element_type=F32)
        w["dst_upd"] = _dot_tn(w["dob"], w["qe"].astype(BF16))
    for w in work:
        gdec = w["wb"] * w["dec"]
        w["x_dq"] = (gdec * w["k_til"]).astype(BF16)
        w["x_dk"] = (gdec * w["q_rep"]).astype(BF16)
        w["x_dv"] = (w["a"] * w["do_rep"]).astype(BF16)
        w["x_db"] = (w["wb"] * w["pairs"]).astype(BF16)
    for w in work:
        dke, dqe = w["dke"], w["dqe"]
        dbl = (jnp.sum(w["dst_n"] * w["st_p"], axis=0, keepdims=True) * w["ebl_last"]
               + jnp.sum(dke * w["ke"], axis=0, keepdims=True))
        dq = dqe * w["eb"] + jnp.dot(sel_ref[...], w["x_dq"], preferred_element_type=F32)
        dk = dke * w["ebl"] + jnp.dot(selt_ref[...], w["x_dk"], preferred_element_type=F32)
        dv = w["dv"] + jnp.dot(selt_ref[...], w["x_dv"], preferred_element_type=F32)
        db = (dqe * w["qe"] - dke * w["ke"] + jnp.dot(seld_ref[...], w["x_db"], preferred_element_type=F32)
              + jnp.where(row == w["last"], dbl, 0.0))
        w["dq_ref"][w["rows"], :] = dq
        w["dv_ref"][w["rows"], :] = dv
        w["dk_scr"][w["rows"], :] = dk
        w["db_scr"][w["rows"], :] = db
        w["dstate"][...] = w["dst_n"] * w["ebl_last"] + bd_ref[...] * w["dst_upd"]


def _scan_chunk_bwd(c, rev, q_ref, v_ref, do_ref, k_scr, b_scr, states, dstate, dq_ref, dv_ref, dk_scr, db_scr,
                    seg_ref, bd_ref, sel_ref, selt_ref, seld_ref):
    last = 0 if rev else CHUNK - 1
    row = lax.broadcasted_iota(jnp.int32, (CHUNK, D_HGRN), 0)
    rows = pl.ds(pl.multiple_of(c * CHUNK, CHUNK), CHUNK)
    q = q_ref[rows, :]
    k = k_scr[rows, :]
    v = v_ref[rows, :]
    bb = b_scr[rows, :]
    dout = do_ref[rows, :]
    bl = bb[last:last + 1, :]
    st_p = states[c]
    dst_n = dstate[...]
    eb = jnp.exp(bb)
    ebl = jnp.exp(bl - bb)
    ebl_last = jnp.exp(bl)
    qe = q * eb
    ke = k * ebl
    dob = dout.astype(BF16)
    dstb = dst_n.astype(BF16)
    dqe = jnp.dot(dob, st_p.astype(BF16), preferred_element_type=F32)
    dke = jnp.dot(v.astype(BF16), dstb, preferred_element_type=F32)
    dv = _dot_nt(ke.astype(BF16), dstb)
    dbl = jnp.sum(dst_n * st_p, axis=0, keepdims=True) * ebl_last + jnp.sum(dke * ke, axis=0, keepdims=True)
    dq = dqe * eb
    dk = dke * ebl
    db = dqe * qe - dke * ke
    dec = _pair_decay(bb, rev)
    q_rep = _rows_rep(q)
    k_til = _tile_rows(k)
    do_rep = _rows_rep(dout)
    pairs = dec * q_rep * k_til
    a = jnp.dot(pairs.astype(BF16), seg_ref[...], preferred_element_type=F32)
    wb = jnp.dot((_tile_rows(v) * do_rep).astype(BF16), seg_ref[...], preferred_element_type=F32)
    gdec = wb * dec
    dq = dq + jnp.dot(sel_ref[...], (gdec * k_til).astype(BF16), preferred_element_type=F32)
    dk = dk + jnp.dot(selt_ref[...], (gdec * q_rep).astype(BF16), preferred_element_type=F32)
    dv = dv + jnp.dot(selt_ref[...], (a * do_rep).astype(BF16), preferred_element_type=F32)
    db = db + jnp.dot(seld_ref[...], (wb * pairs).astype(BF16), preferred_element_type=F32)
    db = db + jnp.where(row == last, dbl, 0.0)
    dq_ref[rows, :] = dq
    dv_ref[rows, :] = dv
    dk_scr[rows, :] = dk
    db_scr[rows, :] = db
    dstate[...] = dst_n * ebl_last + bd_ref[...] * _dot_tn(dob, qe.astype(BF16))


def _hgrn_bwd2(name, proj, lb_f, lb_b, st_f, st_b, do, b, s, hc_f, hc_b):
    t = proj.shape[0]
    nb = s // HBLK
    n_ch = HBLK // CHUNK

    n_chain = 2 * b

    def body(qf_ref, zf_ref, vf_ref, dof_ref, stf_ref, qb_ref, zb_ref, vb_ref, dob_ref, stb_ref, lbf_ref, lbb_ref,
             csf_ref, csb_ref, cstf_ref, cstb_ref, seg_ref, bd_ref, sel_ref, selt_ref, seld_ref,
             dqf_ref, dvf_ref, dzf_ref, dqb_ref, dvb_ref, dzb_ref, dlbf_ref, dlbb_ref,
             *scr):
        dstate, states, b_scr, k_scr, db_scr, dk_scr = [scr[i * n_chain:(i + 1) * n_chain] for i in range(6)]

        @pl.when(pl.program_id(0) == 0)
        def _():
            dlbf_ref[...] = jnp.zeros_like(dlbf_ref)
            dlbb_ref[...] = jnp.zeros_like(dlbb_ref)
            for d0 in dstate:
                d0[...] = jnp.zeros_like(d0)

        chains = []
        for bi in range(b):
            chains.append(dict(rev=False, q=qf_ref.at[bi], z=zf_ref.at[bi], v=vf_ref.at[bi], do=dof_ref.at[bi],
                               st=stf_ref.at[bi], lb=lbf_ref, cs=csf_ref, cst=cstf_ref, dq=dqf_ref.at[bi],
                               dv=dvf_ref.at[bi], dz=dzf_ref.at[bi], dlb=dlbf_ref, ci=2 * bi))
            chains.append(dict(rev=True, q=qb_ref.at[bi], z=zb_ref.at[bi], v=vb_ref.at[bi], do=dob_ref.at[bi],
                               st=stb_ref.at[bi], lb=lbb_ref, cs=csb_ref, cst=cstb_ref, dq=dqb_ref.at[bi],
                               dv=dvb_ref.at[bi], dz=dzb_ref.at[bi], dlb=dlbb_ref, ci=2 * bi + 1))
        for ch in chains:
            sig, f, g, sn, kk = _gates(ch["z"][...], ch["lb"][...])
            k_scr[ch["ci"]][...] = kk
            b_scr[ch["ci"]][...] = _ldot3(ch["cs"][...], g)
            ch["gates"] = (sig, f, sn)

        def replay(i, carry):
            return _scan_replays([((n_ch - 1 - i) if ch["rev"] else i, ch["rev"], st, ch["v"], k_scr[ch["ci"]],
                                   b_scr[ch["ci"]], states[ch["ci"]]) for ch, st in zip(chains, carry)], bd_ref)

        lax.fori_loop(0, n_ch, replay, tuple(ch["st"][...] for ch in chains))

        def chunk(i, carry):
            args = [(i if ch["rev"] else (n_ch - 1 - i), ch["rev"], ch["q"], ch["v"], ch["do"],
                     k_scr[ch["ci"]], b_scr[ch["ci"]], states[ch["ci"]], dstate[ch["ci"]], ch["dq"],
                     ch["dv"], dk_scr[ch["ci"]], db_scr[ch["ci"]]) for ch in chains]
            for g0 in range(0, n_chain, BWD_GROUP):
                _scan_chunks_bwd(args[g0:g0 + BWD_GROUP], seg_ref, bd_ref, sel_ref, selt_ref, seld_ref)
            return carry

        lax.fori_loop(0, n_ch, chunk, 0)
        for ch in chains:
            sig, f, sn = ch["gates"]
            lbv = ch["lb"][...]
            hi, lo = _split2(db_scr[ch["ci"]][...])
            dg = (jnp.dot(ch["cst"][...], hi, preferred_element_type=F32)
                  + jnp.dot(ch["cst"][...], lo, preferred_element_type=F32))
            dgf = jnp.where(f > F_MIN, dg / f, 0.0)
            dk = dk_scr[ch["ci"]][...]
            ch["dz"][...] = dgf * (1.0 - lbv) * sig * (1.0 - sig) - dk * (1.0 - lbv) * sn * (1.0 - sn)
            ch["dlb"][...] += _rowgroups(dgf * (1.0 - sig) - dk * sn)

    def col(c, rev):
        return pl.BlockSpec((b, HBLK, D_HGRN), lambda j: (0, j if rev else (nb - 1 - j), c))

    def st_spec(rev):
        return pl.BlockSpec((b, None, D_HGRN, D_HGRN), lambda j: (0, j if rev else (nb - 1 - j), 0, 0))

    sq = (D_HGRN, D_HGRN)
    blk = (HBLK, D_HGRN)
    pairs_shape = (CHUNK, CHUNK * CHUNK)
    proj3 = proj.reshape(b, s, proj.shape[1])
    do3 = do.reshape(b, s, D_HGRN)
    res = pl.pallas_call(
        body, name=name, grid=(nb,),
        in_specs=[col(COL_HQ, False), col(COL_FF, False), col(COL_HI, False), col(0, False), st_spec(False),
                  col(COL_HQ, True), col(COL_FB, True), col(COL_HI, True), col(0, True), st_spec(True),
                  _full((1, D_HGRN)), _full((1, D_HGRN)), _full((HBLK, HBLK)), _full((HBLK, HBLK)),
                  _full((HBLK, HBLK)), _full((HBLK, HBLK)), _full(sq), _full(sq),
                  _full(pairs_shape), _full(pairs_shape), _full(pairs_shape)],
        out_specs=[col(0, False)] * 3 + [col(0, True)] * 3 + [_full((8, D_HGRN))] * 2,
        out_shape=[_sds((b, s, D_HGRN), F32)] * 6 + [_sds((8, D_HGRN), F32)] * 2,
        scratch_shapes=[pltpu.VMEM(sq, F32)] * n_chain + [pltpu.VMEM((n_ch,) + sq, F32)] * n_chain
        + [pltpu.VMEM(blk, F32)] * (4 * n_chain),
        compiler_params=_params(("arbitrary",)),
    )(proj3, proj3, proj3, do3, st_f, proj3, proj3, proj3, do3, st_b, lb_f, lb_b, hc_f["cs"], hc_b["cs"],
      hc_f["cs_t"], hc_b["cs_t"], hc_f["seg"], hc_f["bd"], hc_f["sel"], hc_f["selt"], hc_f["seld"])
    return [r.reshape(t, D_HGRN) for r in res[:6]] + list(res[6:])


def _lower_bounds(logits):
    n = logits.shape[1]

    def body(x_ref, o_ref):
        x = x_ref[...]
        for d in range(2):
            rows = [x[l * 2 + d:l * 2 + d + 1, :] for l in range(DEPTH)]
            mx = functools.reduce(jnp.maximum, rows)
            ex = [jnp.exp(r - mx) for r in rows]
            tot = functools.reduce(lambda a, c: a + c, ex)
            sm = [e / tot for e in ex]
            run = jnp.zeros_like(sm[0])
            for l in range(DEPTH):
                run = run + sm[l]
                o_ref[l * 2 + d:l * 2 + d + 1, :] = run - sm[0]

    return pl.pallas_call(body, name="hgrn_lower_bounds", out_shape=_sds(logits.shape, F32),
                          in_specs=[_full(logits.shape)], out_specs=_full(logits.shape), grid=(1,),
                          compiler_params=_params(("arbitrary",)))(logits)


def _lower_bounds_bwd(logits, dlb):
    def body(x_ref, g_ref, o_ref):
        x = x_ref[...]
        gv = g_ref[...]
        for d in range(2):
            rows = [x[l * 2 + d:l * 2 + d + 1, :] for l in range(DEPTH)]
            gr = [gv[l * 2 + d:l * 2 + d + 1, :] for l in range(DEPTH)]
            mx = functools.reduce(jnp.maximum, rows)
            ex = [jnp.exp(r - mx) for r in rows]
            tot = functools.reduce(lambda a, c: a + c, ex)
            sm = [e / tot for e in ex]
            dsm = []
            for i in range(DEPTH):
                acc = functools.reduce(lambda a, c: a + c, gr[i:])
                if i == 0:
                    acc = acc - functools.reduce(lambda a, c: a + c, gr)
                dsm.append(acc)
            inner = functools.reduce(lambda a, c: a + c, [sm[i] * dsm[i] for i in range(DEPTH)])
            for i in range(DEPTH):
                o_ref[i * 2 + d:i * 2 + d + 1, :] = sm[i] * (dsm[i] - inner)

    return pl.pallas_call(body, name="hgrn_lower_bounds_bwd", out_shape=_sds(logits.shape, F32),
                          in_specs=[_full(logits.shape), _full(logits.shape)], out_specs=_full(logits.shape),
                          grid=(1,), compiler_params=_params(("arbitrary",)))(logits, dlb)


def _conv_rows(s):
    return s + 2 * (CONV_PAD + 1)


def _conv_fwd(name, proj, dw_w, dw_b, ln_w, ln_b, pw_w, pw_b, b, s):
    t = proj.shape[0]
    pad = CONV_PAD + 1
    nt = s // CONV_TILE

    def body(a_ref, g_ref, w_ref, dwb_ref, lnw_ref, lnb_ref, pw_ref, pwb_ref, y_ref, c_ref, upad, win):
        upad[0:pad, :] = jnp.zeros((pad, D_CONV), F32)
        upad[s + pad:s + 2 * pad, :] = jnp.zeros((pad, D_CONV), F32)

        def fill(i, carry):
            rows = pl.ds(pl.multiple_of(i * CONV_TILE, CONV_TILE), CONV_TILE)
            upad[pl.ds(pl.multiple_of(i * CONV_TILE + pad, pad), CONV_TILE), :] = a_ref[rows, :] * _sigmoid(g_ref[rows, :])
            return carry

        lax.fori_loop(0, nt, fill, 0)

        def tile(i, carry):
            r0 = pl.multiple_of(i * CONV_TILE, CONV_TILE)
            win[...] = upad[pl.ds(r0, CONV_TILE + 2 * pad), :]
            acc = jnp.zeros((CONV_TILE, D_CONV), F32)
            for j in range(CONV_W):
                acc = acc + win[j + 1:j + 1 + CONV_TILE, :] * w_ref[j:j + 1, :]
            c = acc + dwb_ref[...]
            c_ref[pl.ds(r0, CONV_TILE), :] = c
            mu = jnp.mean(c, axis=-1, keepdims=True)
            xc = c - mu
            rstd = lax.rsqrt(jnp.mean(xc * xc, axis=-1, keepdims=True) + LN_EPS)
            n = xc * rstd * lnw_ref[...] + lnb_ref[...]
            y_ref[pl.ds(r0, CONV_TILE), :] = (jnp.dot(_silu(n).astype(BF16), pw_ref[...].astype(BF16),
                                                      preferred_element_type=F32) + pwb_ref[...])
            return carry

        lax.fori_loop(0, nt, tile, 0)

    vec = _full((1, D_CONV))
    return pl.pallas_call(
        body, name=name, grid=(b,),
        in_specs=[pl.BlockSpec((s, D_CONV), lambda bi: (bi, COL_CA)), pl.BlockSpec((s, D_CONV), lambda bi: (bi, COL_CB)),
                  _full((CONV_W + 1, D_CONV)), vec, vec, vec, _full((D_CONV, D_CONV)), vec],
        out_specs=[pl.BlockSpec((s, D_CONV), lambda bi: (bi, 0))] * 2, out_shape=[_sds((t, D_CONV), F32)] * 2,
        scratch_shapes=[pltpu.VMEM((_conv_rows(s), D_CONV), F32), pltpu.VMEM((CONV_TILE + 2 * pad, D_CONV), F32)],
        compiler_params=_params(("parallel",)),
    )(proj, proj, dw_w, dw_b, ln_w, ln_b, pw_w, pw_b)


def _conv_bwd(name, proj, conv_out, dw_w, ln_w, ln_b, pw_w, dy, b, s):
    t = proj.shape[0]
    pad = CONV_PAD + 1
    nt = s // CONV_TILE

    def body(a_ref, g_ref, c_ref, w_ref, lnw_ref, lnb_ref, pw_ref, dy_ref, dab_ref, dpw_ref, ddw_ref, dvec_ref,
             upad, dcpad, tap_acc, win, dwin):
        @pl.when(pl.program_id(0) == 0)
        def _():
            dpw_ref[...] = jnp.zeros_like(dpw_ref)
            ddw_ref[...] = jnp.zeros_like(ddw_ref)
            dvec_ref[...] = jnp.zeros_like(dvec_ref)

        zeros = jnp.zeros((pad, D_CONV), F32)
        upad[0:pad, :] = zeros
        upad[s + pad:s + 2 * pad, :] = zeros
        dcpad[0:pad, :] = zeros
        dcpad[s + pad:s + 2 * pad, :] = zeros
        tap_acc[...] = jnp.zeros_like(tap_acc)

        def inner(i):
            return pl.ds(pl.multiple_of(i * CONV_TILE + pad, pad), CONV_TILE)

        def fill(i, carry):
            rows = pl.ds(pl.multiple_of(i * CONV_TILE, CONV_TILE), CONV_TILE)
            upad[inner(i), :] = a_ref[rows, :] * _sigmoid(g_ref[rows, :])
            return carry

        lax.fori_loop(0, nt, fill, 0)

        def tile_a(i, carry):
            r0 = pl.multiple_of(i * CONV_TILE, CONV_TILE)
            c = c_ref[pl.ds(r0, CONV_TILE), :]
            mu = jnp.mean(c, axis=-1, keepdims=True)
            xc = c - mu
            rstd = lax.rsqrt(jnp.mean(xc * xc, axis=-1, keepdims=True) + LN_EPS)
            xhat = xc * rstd
            n = xhat * lnw_ref[...] + lnb_ref[...]
            dyt = dy_ref[pl.ds(r0, CONV_TILE), :]
            dyb = dyt.astype(BF16)
            dpw_ref[...] += _dot_tn(_silu(n).astype(BF16), dyb)
            dn = _dot_nt(dyb, pw_ref[...].astype(BF16)) * _dsilu(n)
            dxh = dn * lnw_ref[...]
            dc = rstd * (dxh - jnp.mean(dxh, axis=-1, keepdims=True)
                         - xhat * jnp.mean(dxh * xhat, axis=-1, keepdims=True))
            dcpad[inner(i), :] = dc
            dvec_ref[0:1, :] += jnp.sum(dyt, axis=0, keepdims=True)
            dvec_ref[1:2, :] += jnp.sum(dn * xhat, axis=0, keepdims=True)
            dvec_ref[2:3, :] += jnp.sum(dn, axis=0, keepdims=True)
            dvec_ref[3:4, :] += jnp.sum(dc, axis=0, keepdims=True)
            return carry

        lax.fori_loop(0, nt, tile_a, 0)

        def tile_b(i, carry):
            r0 = pl.multiple_of(i * CONV_TILE, CONV_TILE)
            win[...] = upad[pl.ds(r0, CONV_TILE + 2 * pad), :]
            dwin[...] = dcpad[pl.ds(r0, CONV_TILE + 2 * pad), :]
            dct = dwin[pad:pad + CONV_TILE, :]
            du = jnp.zeros((CONV_TILE, D_CONV), F32)
            for j in range(CONV_W):
                du = du + dwin[2 * pad - 1 - j:2 * pad - 1 - j + CONV_TILE, :] * w_ref[j:j + 1, :]
                tap_acc[8 * j:8 * j + 8, :] += _rowgroups(dct * win[j + 1:j + 1 + CONV_TILE, :])
            rows = pl.ds(r0, CONV_TILE)
            sg = _sigmoid(g_ref[rows, :])
            dab_ref[rows, 0:D_CONV] = (du * sg).astype(BF16)
            dab_ref[rows, D_CONV:2 * D_CONV] = (du * a_ref[rows, :] * sg * (1.0 - sg)).astype(BF16)
            return carry

        lax.fori_loop(0, nt, tile_b, 0)
        for j in range(CONV_W):
            ddw_ref[j:j + 1, :] += jnp.sum(tap_acc[8 * j:8 * j + 8, :], axis=0, keepdims=True)

    vec = _full((1, D_CONV))
    return pl.pallas_call(
        body, name=name, grid=(b,),
        in_specs=[pl.BlockSpec((s, D_CONV), lambda bi: (bi, COL_CA)), pl.BlockSpec((s, D_CONV), lambda bi: (bi, COL_CB)),
                  pl.BlockSpec((s, D_CONV), lambda bi: (bi, 0)),
                  _full((CONV_W + 1, D_CONV)), vec, vec, _full((D_CONV, D_CONV)),
                  pl.BlockSpec((s, D_CONV), lambda bi: (bi, 0))],
        out_specs=[pl.BlockSpec((s, 2 * D_CONV), lambda bi: (bi, 0)), _full((D_CONV, D_CONV)),
                   _full((CONV_W + 1, D_CONV)), _full((8, D_CONV))],
        out_shape=[_sds((t, 2 * D_CONV), BF16), _sds((D_CONV, D_CONV), F32), _sds((CONV_W + 1, D_CONV), F32),
                   _sds((8, D_CONV), F32)],
        scratch_shapes=[pltpu.VMEM((_conv_rows(s), D_CONV), F32), pltpu.VMEM((_conv_rows(s), D_CONV), F32),
                        pltpu.VMEM((8 * CONV_W, D_CONV), F32), pltpu.VMEM((CONV_TILE + 2 * pad, D_CONV), F32),
                        pltpu.VMEM((CONV_TILE + 2 * pad, D_CONV), F32)],
        compiler_params=_params(("arbitrary",)),
    )(proj, proj, conv_out, dw_w, ln_w, ln_b, pw_w, dy)


def _mix_fwd(name, y_attn, o_fw, o_bw, proj, y_conv, aw, gw, cw, seg):
    t = y_attn.shape[0]
    tm = _row_tile(t)

    def body(ya_ref, of_ref, ob_ref, hg_ref, yc_ref, aw_ref, gw_ref, cw_ref, seg_ref, o_ref):
        ya = ya_ref[...]
        ra = lax.rsqrt(jnp.mean(ya * ya, axis=-1, keepdims=True) + EPS)
        o_ref[:, 0:D_ATTN] = (ya * ra * aw_ref[...]).astype(BF16)
        o = of_ref[...] + ob_ref[...]
        ro = lax.rsqrt(jnp.dot((o * o).astype(BF16), seg_ref[...], preferred_element_type=F32) + EPS)
        o_ref[:, D_ATTN:D_ATTN + D_HGRN] = (o * ro * gw_ref[...] * _silu(hg_ref[...])).astype(BF16)
        yc = yc_ref[...]
        rc = lax.rsqrt(jnp.mean(yc * yc, axis=-1, keepdims=True) + EPS)
        o_ref[:, D_ATTN + D_HGRN:D_MODEL] = (yc * rc * cw_ref[...]).astype(BF16)

    def tile(w, c=0):
        return pl.BlockSpec((tm, w), lambda i: (i, c))

    return pl.pallas_call(
        body, name=name, grid=(t // tm,),
        in_specs=[tile(D_ATTN), tile(D_HGRN), tile(D_HGRN), tile(D_HGRN, COL_HG), tile(D_CONV),
                  _full((1, D_ATTN)), _full((1, D_HGRN)), _full((1, D_CONV)), _full((D_HGRN, D_HGRN))],
        out_specs=tile(D_MODEL), out_shape=_sds((t, D_MODEL), BF16),
        compiler_params=_params(("parallel",)),
    )(y_attn, o_fw, o_bw, proj, y_conv, aw, gw, cw, seg)


def _mix_bwd(name, dmix, y_attn, o_fw, o_bw, proj, y_conv, aw, gw, cw, seg, deps=()):
    t = y_attn.shape[0]
    tm = _row_tile(t)

    def rms_bwd(x, w, dy):
        r = lax.rsqrt(jnp.mean(x * x, axis=-1, keepdims=True) + EPS)
        gwv = dy * w
        return r * gwv - x * (r * r * r) * jnp.mean(gwv * x, axis=-1, keepdims=True), _rowgroups(dy * x * r)

    def body(dm_ref, ya_ref, of_ref, ob_ref, hg_ref, yc_ref, aw_ref, gw_ref, cw_ref, seg_ref, *rest):
        dya_ref, do_ref, dhg_ref, dyc_ref, daw_ref, dgw_ref, dcw_ref = rest[-7:]

        @pl.when(pl.program_id(0) == 0)
        def _():
            daw_ref[...] = jnp.zeros_like(daw_ref)
            dgw_ref[...] = jnp.zeros_like(dgw_ref)
            dcw_ref[...] = jnp.zeros_like(dcw_ref)

        dya, daw = rms_bwd(ya_ref[...], aw_ref[...], dm_ref[:, 0:D_ATTN])
        dya_ref[...] = dya
        daw_ref[...] += daw
        dyc, dcw = rms_bwd(yc_ref[...], cw_ref[...], dm_ref[:, D_ATTN + D_HGRN:D_MODEL])
        dyc_ref[...] = dyc
        dcw_ref[...] += dcw
        d2 = dm_ref[:, D_ATTN:D_ATTN + D_HGRN]
        o = of_ref[...] + ob_ref[...]
        hg = hg_ref[...]
        ro = lax.rsqrt(jnp.dot((o * o).astype(BF16), seg_ref[...], preferred_element_type=F32) + EPS)
        dn = d2 * _silu(hg)
        dhg_ref[...] = (d2 * o * ro * gw_ref[...] * _dsilu(hg)).astype(BF16)
        gwv = dn * gw_ref[...]
        do_ref[...] = ro * gwv - o * (ro * ro * ro) * _rdot2(gwv * o, seg_ref[...])
        dgw_ref[...] += _rowgroups(dn * o * ro)

    def tile(w, c=0):
        return pl.BlockSpec((tm, w), lambda i: (i, c))

    return pl.pallas_call(
        body, name=name, grid=(t // tm,),
        in_specs=[tile(D_MODEL), tile(D_ATTN), tile(D_HGRN), tile(D_HGRN), tile(D_HGRN, COL_HG), tile(D_CONV),
                  _full((1, D_ATTN)), _full((1, D_HGRN)), _full((1, D_CONV)), _full((D_HGRN, D_HGRN))]
        + [_full(a.shape) for a in deps],
        out_specs=[tile(D_ATTN), tile(D_HGRN), tile(D_HGRN), tile(D_CONV),
                   _full((8, D_ATTN)), _full((8, D_HGRN)), _full((8, D_CONV))],
        out_shape=[_sds((t, D_ATTN), F32), _sds((t, D_HGRN), F32), _sds((t, D_HGRN), BF16), _sds((t, D_CONV), F32),
                   _sds((8, D_ATTN), F32), _sds((8, D_HGRN), F32), _sds((8, D_CONV), F32)],
        compiler_params=_params(("arbitrary",)),
    )(dmix, y_attn, o_fw, o_bw, proj, y_conv, aw, gw, cw, seg, *deps)


def _dproj(name, dp_attn, dq_f, dq_b, dz_fw, dz_bw, dv_f, dv_b, dhg, dp_conv):
    t = dq_f.shape[0]
    tm = _row_tile(t)
    wa, wc = dp_attn.shape[1], dp_conv.shape[1]

    def body(at_ref, qf_ref, qb_ref, zf_ref, zb_ref, vf_ref, vb_ref, hg_ref, cv_ref, o_ref):
        o_ref[:, 0:wa] = at_ref[...]
        cols = (qf_ref[...] + qb_ref[...], zf_ref[...], zb_ref[...], vf_ref[...] + vb_ref[...], hg_ref[...])
        for i, val in enumerate(cols):
            o_ref[:, wa + i * D_HGRN:wa + (i + 1) * D_HGRN] = val.astype(BF16)
        o_ref[:, wa + 5 * D_HGRN:D_IN] = cv_ref[...]

    tile = lambda w: pl.BlockSpec((tm, w), lambda i: (i, 0))
    return pl.pallas_call(
        body, name=name, grid=(t // tm,), in_specs=[tile(wa)] + [tile(D_HGRN)] * 7 + [tile(wc)],
        out_specs=tile(D_IN), out_shape=_sds((t, D_IN), BF16), compiler_params=_params(("parallel",)),
    )(dp_attn, dq_f, dq_b, dz_fw, dz_bw, dv_f, dv_b, dhg, dp_conv)


def _mm_tile(t):
    return min(512, t)


def _resident(shape):
    n = len(shape)
    return pl.BlockSpec(tuple(shape), lambda *_: (0,) * n, pipeline_mode=pl.Buffered(1))


def _w_blk(rows, cols, j_of):
    return pl.BlockSpec((None, rows, cols), lambda *g: (j_of(*g), 0, 0))


def _layer_fwd(l, x, wget, sm, tabs, cst, b, s, deps):
    t = x.shape[0]
    tm = _mm_tile(t)
    nt = t // tm
    pre = "l%d_" % l
    row = lambda w: pl.BlockSpec((tm, w), lambda i, *_: (i, 0))

    def normed(x_ref, nw_ref):
        xv = x_ref[...]
        r = lax.rsqrt(jnp.mean(xv * xv, axis=-1, keepdims=True) + EPS)
        return (xv * r * nw_ref[...]).astype(BF16)

    def in_body(x_ref, nw_ref, w_ref, *rest):
        o_ref, h_ref = rest[-2:]
        hv = normed(x_ref, nw_ref)
        h_ref[...] = hv
        for j in range(N_CHIP):
            o_ref[:, j * IN_BLK:(j + 1) * IN_BLK] = jnp.dot(hv, w_ref[j], preferred_element_type=F32)

    w_in = wget(l, "w_in", x)
    proj, h1 = pl.pallas_call(
        in_body, name=pre + "in_proj", grid=(nt,),
        in_specs=[row(D_MODEL), _full((1, D_MODEL)), _resident(w_in.shape)] + [_full(a.shape) for a in deps],
        out_specs=[row(D_IN), row(D_MODEL)], out_shape=[_sds((t, D_IN), F32), _sds((t, D_MODEL), BF16)],
        compiler_params=_params(("parallel",)),
    )(x, sm["mix_norm_w"][l], w_in, *deps)
    qn, kr, vr = _attn_prep(pre + "attn_prep", proj, s, tabs, sm["q_norm_w"][l], sm["k_norm_w"][l], cst["attn"])
    y_attn = _attn_fwd(pre + "attn", qn, kr, vr, b, s)
    o_fw, o_bw, st_fw, st_bw = _hgrn_fwd2(pre + "hgrn", proj, sm["lb"][l][0], sm["lb"][l][1], b, s, cst["hg_fw"],
                                          cst["hg_bw"])
    y_conv, conv_out = _conv_fwd(pre + "conv", proj, sm["conv_dw_w"][l], sm["conv_dw_b"][l], sm["conv_ln_w"][l],
                       sm["conv_ln_b"][l], sm["conv_pw_w"][l], sm["conv_pw_b"][l], b, s)
    mixed = _mix_fwd(pre + "mix", y_attn, o_fw, o_bw, proj, y_conv, sm["attn_out_norm_w"][l], sm["gnorm_w"][l],
                     sm["conv_out_norm_w"][l], cst["seg_h"])
    (x1,) = _mm(pre + "out_proj", (nt,),
                [(mixed, row(D_MODEL), wget(l, "w_out", mixed),
                  pl.BlockSpec((N_CHIP, OUT_BLK, D_MODEL), lambda i: (0, 0, 0)), NN)],
                [(x, row(D_MODEL))], [(_sds((t, D_MODEL), F32), row(D_MODEL))],
                lambda tot, xr: (xr + tot,))
    ff3 = pl.BlockSpec((N_CHIP, tm, FF_BLK), lambda i: (0, i, 0))
    ffs = _sds((N_CHIP, t, FF_BLK), BF16)

    def gu_body(x_ref, nw_ref, wg_ref, wu_ref, g_ref, u_ref, a_ref, h_ref):
        hv = normed(x_ref, nw_ref)
        h_ref[...] = hv
        for j in range(N_CHIP):
            gv = jnp.dot(hv, wg_ref[j], preferred_element_type=F32)
            uv = jnp.dot(hv, wu_ref[j], preferred_element_type=F32)
            g_ref[j] = gv.astype(BF16)
            u_ref[j] = uv.astype(BF16)
            a_ref[j] = (_silu(gv) * uv).astype(BF16)

    w_gate, w_up = wget(l, "w_gate", x1), wget(l, "w_up", x1)
    gate, up, act, h2 = pl.pallas_call(
        gu_body, name=pre + "ffn_gate_up", grid=(nt,),
        in_specs=[row(D_MODEL), _full((1, D_MODEL)), _resident(w_gate.shape), _resident(w_up.shape)],
        out_specs=[ff3, ff3, ff3, row(D_MODEL)], out_shape=[ffs, ffs, ffs, _sds((t, D_MODEL), BF16)],
        compiler_params=_params(("parallel",)),
    )(x1, sm["ffn_norm_w"][l], w_gate, w_up)

    def down_body(a_ref, w_ref, x_ref, o_ref):
        tot = x_ref[...]
        for j in range(N_CHIP):
            tot = tot + jnp.dot(a_ref[j], w_ref[j], preferred_element_type=F32)
        o_ref[...] = tot

    w_down = wget(l, "w_down", act)
    x2 = pl.pallas_call(
        down_body, name=pre + "ffn_down", grid=(nt,), in_specs=[ff3, _resident(w_down.shape), row(D_MODEL)],
        out_specs=row(D_MODEL), out_shape=_sds((t, D_MODEL), F32), compiler_params=_params(("parallel",)),
    )(act, w_down, x1)
    saved = dict(x=x, h1=h1, proj=proj, qn=qn, kr=kr, vr=vr, y_attn=y_attn, o_fw=o_fw, o_bw=o_bw, st_fw=st_fw,
                 st_bw=st_bw, y_conv=y_conv, conv_out=conv_out, mixed=mixed, x1=x1, h2=h2, gate=gate, up=up, act=act)
    return x2, saved


def _layer_bwd(l, dx2, sv, wget, sm, tabs, cst, b, s, on_grads):
    t = dx2.shape[0]
    tm = _mm_tile(t)
    nt = t // tm
    pre = "l%d_" % l
    tk = min(2048, t)
    nk = t // tk
    row = lambda w: pl.BlockSpec((tm, w), lambda i, *_: (i, 0))
    ff3 = pl.BlockSpec((N_CHIP, tm, FF_BLK), lambda i: (0, i, 0))
    ffs = _sds((N_CHIP, t, FF_BLK), BF16)

    w_down, w_gate, w_up = wget(l, "w_down", dx2), wget(l, "w_gate", dx2), wget(l, "w_up", dx2)

    def ddx_body(dx_ref, w_ref, g_ref, u_ref, dg_ref, du_ref):
        dxb = dx_ref[...].astype(BF16)
        for j in range(N_CHIP):
            da = _dot_nt(dxb, w_ref[j])
            g = g_ref[j].astype(F32)
            sg = _sigmoid(g)
            dg_ref[j] = (da * u_ref[j].astype(F32) * (sg * (1.0 + g * (1.0 - sg)))).astype(BF16)
            du_ref[j] = (da * (g * sg)).astype(BF16)

    dgate, dup = pl.pallas_call(
        ddx_body, name=pre + "ffn_down_dx", grid=(nt,), in_specs=[row(D_MODEL), _resident(w_down.shape), ff3, ff3],
        out_specs=[ff3, ff3], out_shape=[ffs, ffs], compiler_params=_params(("parallel",)),
    )(dx2, w_down, sv["gate"], sv["up"])
    colt = lambda w: pl.BlockSpec((tk, w), lambda j, k: (k, 0))
    fft = pl.BlockSpec((None, tk, FF_BLK), lambda j, k: (j, k, 0))
    (g_down,) = _mm(pre + "ffn_down_dw", (N_CHIP, nk), [(sv["act"], fft, dx2, colt(D_MODEL), TN)], [],
                    [(_sds((N_CHIP, FF_BLK, D_MODEL), BF16), pl.BlockSpec((None, FF_BLK, D_MODEL), lambda j, k: (j, 0, 0)))],
                    lambda tot: (tot,), acc=(1, (FF_BLK, D_MODEL)))
    wff = pl.BlockSpec((None, D_MODEL, FF_BLK), lambda j, k: (j, 0, 0))
    (g_gate,) = _mm(pre + "ffn_gate_dw", (N_CHIP, nk), [(sv["h2"], colt(D_MODEL), dgate, fft, TN)], [],
                    [(_sds((N_CHIP, D_MODEL, FF_BLK), BF16), wff)], lambda tot: (tot,), acc=(1, (D_MODEL, FF_BLK)))
    (g_up,) = _mm(pre + "ffn_up_dw", (N_CHIP, nk), [(sv["h2"], colt(D_MODEL), dup, fft, TN)], [],
                  [(_sds((N_CHIP, D_MODEL, FF_BLK), BF16), wff)], lambda tot: (tot,), acc=(1, (D_MODEL, FF_BLK)))

    def norm_bwd_tail(dh, x_ref, nw_ref, dres_ref, dx_ref, dw_ref):
        xv = x_ref[...]
        r = lax.rsqrt(jnp.mean(xv * xv, axis=-1, keepdims=True) + EPS)
        gw = dh * nw_ref[...]
        dx_ref[...] = dres_ref[...] + r * gw - xv * (r * r * r) * jnp.mean(gw * xv, axis=-1, keepdims=True)

        @pl.when(pl.program_id(0) == 0)
        def _():
            dw_ref[...] = jnp.zeros_like(dw_ref)

        dw_ref[...] += _rowgroups(dh * xv * r)

    def dh_body(dg_ref, du_ref, wg_ref, wu_ref, x_ref, nw_ref, dres_ref, *rest):
        tot = None
        for j in range(N_CHIP):
            r = _dot_nt(dg_ref[j], wg_ref[j]) + _dot_nt(du_ref[j], wu_ref[j])
            tot = r if tot is None else tot + r
        norm_bwd_tail(tot, x_ref, nw_ref, dres_ref, *rest[-2:])

    deps = on_grads(l, dict(w_gate=g_gate, w_up=g_up, w_down=g_down))
    dx1, d_ffn_norm = pl.pallas_call(
        dh_body, name=pre + "ffn_dh", grid=(nt,),
        in_specs=[ff3, ff3, _resident(w_gate.shape), _resident(w_up.shape), row(D_MODEL), _full((1, D_MODEL)),
                  row(D_MODEL)] + [_full(a.shape) for a in deps],
        out_specs=[row(D_MODEL), _full((8, D_MODEL))], out_shape=[_sds((t, D_MODEL), F32), _sds((8, D_MODEL), F32)],
        compiler_params=_params(("arbitrary",)),
    )(dgate, dup, w_gate, w_up, sv["x1"], sm["ffn_norm_w"][l], dx2, *deps)

    (dmix,) = _mm(pre + "out_proj_dx", (nt,),
                  [(dx1, row(D_MODEL), wget(l, "w_out", dx2),
                    pl.BlockSpec((N_CHIP, OUT_BLK, D_MODEL), lambda i: (0, 0, 0)), NT)], [],
                  [(_sds((t, D_MODEL), F32), row(D_MODEL))], lambda tot: (tot,))
    (g_out,) = _mm(pre + "out_proj_dw", (N_CHIP, nk),
                   [(sv["mixed"], pl.BlockSpec((tk, OUT_BLK), lambda j, k: (k, j)), dx1, colt(D_MODEL), TN)], [],
                   [(_sds((N_CHIP, OUT_BLK, D_MODEL), BF16), pl.BlockSpec((None, OUT_BLK, D_MODEL), lambda j, k: (j, 0, 0)))],
                   lambda tot: (tot,), acc=(1, (OUT_BLK, D_MODEL)))
    proj = sv["proj"]
    dya, do_h, dhg, dyc, d_aw, d_gw, d_cw = _mix_bwd(
        pre + "mix_bwd", dmix, sv["y_attn"], sv["o_fw"], sv["o_bw"], proj, sv["y_conv"],
        sm["attn_out_norm_w"][l], sm["gnorm_w"][l], sm["conv_out_norm_w"][l], cst["seg_h"],
        on_grads(l, dict(w_out=g_out)))
    dqs, dkr, dvr = _attn_bwd(pre + "attn_bwd", sv["qn"], sv["kr"], sv["vr"], dya, b, s)
    dp_attn, d_qw, d_kw = _attn_prep_bwd(pre + "attn_prep_bwd", proj, s, tabs, sm["q_norm_w"][l], sm["k_norm_w"][l],
                                         cst["attn"], dqs, dkr, dvr)
    dq_f, dv_f, dz_fw, dq_b, dv_b, dz_bw, dlb_fw, dlb_bw = _hgrn_bwd2(
        pre + "hgrn_bwd", proj, sm["lb"][l][0], sm["lb"][l][1], sv["st_fw"], sv["st_bw"], do_h, b, s,
        cst["hg_fw"], cst["hg_bw"])
    dp_conv, d_pw, d_dw, d_cvec = _conv_bwd(pre + "conv_bwd", proj, sv["conv_out"], sm["conv_dw_w"][l],
                                            sm["conv_ln_w"][l], sm["conv_ln_b"][l], sm["conv_pw_w"][l], dyc, b, s)
    dproj = _dproj(pre + "dproj", dp_attn, dq_f, dq_b, dz_fw, dz_bw, dv_f, dv_b, dhg, dp_conv)
    g_pw = d_pw.reshape(N_CHIP, D_CONV // N_CHIP, D_CONV).astype(BF16)

    (g_in,) = _mm(pre + "in_proj_dw", (N_CHIP, nk),
                  [(sv["h1"], colt(D_MODEL), dproj, pl.BlockSpec((tk, IN_BLK), lambda j, k: (k, j)), TN)], [],
                  [(_sds((N_CHIP, D_MODEL, IN_BLK), BF16), pl.BlockSpec((None, D_MODEL, IN_BLK), lambda j, k: (j, 0, 0)))],
                  lambda tot: (tot,), acc=(1, (D_MODEL, IN_BLK)))

    def indx_body(dp_ref, w_ref, x_ref, nw_ref, dres_ref, *rest):
        tot = None
        for j in range(N_CHIP):
            r = _dot_nt(dp_ref[:, j * IN_BLK:(j + 1) * IN_BLK], w_ref[j])
            tot = r if tot is None else tot + r
        norm_bwd_tail(tot, x_ref, nw_ref, dres_ref, *rest[-2:])

    w_in = wget(l, "w_in", dx2)
    deps = on_grads(l, dict(w_in=g_in, conv_pw_w=g_pw))
    dx, d_mix_norm = pl.pallas_call(
        indx_body, name=pre + "in_proj_dx", grid=(nt,),
        in_specs=[row(D_IN), _resident(w_in.shape), row(D_MODEL), _full((1, D_MODEL)), row(D_MODEL)]
        + [_full(a.shape) for a in deps],
        out_specs=[row(D_MODEL), _full((8, D_MODEL))], out_shape=[_sds((t, D_MODEL), F32), _sds((8, D_MODEL), F32)],
        compiler_params=_params(("arbitrary",)),
    )(dproj, w_in, sv["x"], sm["mix_norm_w"][l], dx1, *deps)
    heads = lambda v, n: v.sum(axis=0).reshape(n, HEAD_DIM).sum(axis=0)
    small = dict(
        mix_norm_w=d_mix_norm.sum(axis=0), q_norm_w=heads(d_qw, D_ATTN // HEAD_DIM), k_norm_w=heads(d_kw, N_KV),
        lb=jnp.stack([dlb_fw.sum(axis=0), dlb_bw.sum(axis=0)]), hgrn_gnorm_w=heads(d_gw, D_HGRN // HEAD_DIM),
        conv_dw_w=d_dw[:CONV_W], conv_dw_b=d_cvec[3], conv_ln_w=d_cvec[1], conv_ln_b=d_cvec[2],
        conv_pw_b=d_cvec[0], attn_out_norm_w=d_aw.sum(axis=0), conv_out_norm_w=d_cw.sum(axis=0),
        ffn_norm_w=d_ffn_norm.sum(axis=0))
    return dx, small


SMALL_ORDER = ("mix_norm_w", "q_norm_w", "k_norm_w", "lb", "hgrn_gnorm_w", "conv_dw_w", "conv_dw_b", "conv_ln_w",
               "conv_ln_b", "conv_pw_b", "attn_out_norm_w", "conv_out_norm_w", "ffn_norm_w")
BIG_ORDER = ("w_in", "w_out", "w_gate", "w_up", "w_down")
SCATTER_ORDER = BIG_ORDER + ("conv_pw_w",)


def _local_step(x, target, wget, sm, deps, on_grads):
    b, s, d = x.shape
    t = b * s
    cos, sin = _rope_tables(s)
    tabs = dict(cq=jnp.tile(cos, (1, D_ATTN // HEAD_DIM)), sq=jnp.tile(sin, (1, D_ATTN // HEAD_DIM)),
                ck=jnp.tile(cos, (1, N_KV)), sk=jnp.tile(sin, (1, N_KV)))
    cst = dict(attn=_attn_consts(), hg_fw=_hgrn_consts(False), hg_bw=_hgrn_consts(True),
               seg_h=_bf(_seg_matrix(D_HGRN, HEAD_DIM, 1.0 / HEAD_DIM)))
    vec = lambda a: a.reshape(DEPTH, 1, -1)
    smk = dict(sm)
    for n in ("mix_norm_w", "conv_dw_b", "conv_ln_w", "conv_ln_b", "conv_pw_b", "attn_out_norm_w", "conv_out_norm_w",
              "ffn_norm_w"):
        smk[n] = vec(sm[n])
    smk["q_norm_w"] = vec(jnp.tile(sm["q_norm_w"], (1, D_ATTN // HEAD_DIM)))
    smk["k_norm_w"] = vec(jnp.tile(sm["k_norm_w"], (1, N_KV)))
    smk["gnorm_w"] = vec(jnp.tile(sm["hgrn_gnorm_w"], (1, D_HGRN // HEAD_DIM)))
    smk["lb"] = sm["lb"].reshape(DEPTH, 2, 1, D_HGRN)
    smk["conv_dw_w"] = jnp.pad(sm["conv_dw_w"], ((0, 0), (0, 1), (0, 0)))

    h = x.reshape(t, d)
    saved = []
    for l in range(DEPTH):
        h, sv = _layer_fwd(l, h, wget, smk, tabs, cst, b, s, deps if l == 0 else ())
        saved.append(sv)
    dy, sq = _loss_kernel(h, target.reshape(t, d))
    sq_sum = jnp.sum(sq)
    dh = dy
    smalls = [None] * DEPTH
    for l in reversed(range(DEPTH)):
        dh, smalls[l] = _layer_bwd(l, dh, saved[l], wget, smk, tabs, cst, b, s, on_grads)
    return sq_sum, dh.reshape(b, s, d), smalls


HBM_SPEC = pl.BlockSpec(memory_space=pltpu.HBM)


def _exchange(name, arrs, mode):
    n = len(arrs)
    if mode == "gather8":
        flips = [(fx, fy, fc) for fx in (0, 1) for fy in (0, 1) for fc in (0, 1)][1:]
    elif mode == "sibling":
        flips = [(0, 0, 1)]
    else:
        flips = [(1, 0, 0), (0, 1, 0), (1, 1, 0)]
    n_f = len(flips)

    def body(*refs):
        ins, outs = refs[:n], refs[n:2 * n]
        send_sems, recv_sems, local_sems = refs[2 * n:]
        x, y, c = lax.axis_index("x"), lax.axis_index("y"), lax.axis_index("c")

        def slot_of(px, py, pc):
            return (2 * px + py) if mode != "gather8" else (4 * px + 2 * py + pc)

        me = slot_of(x, y, c)
        started = []
        for i in range(n):
            if mode != "sibling":
                src = ins[i].at[me] if mode == "scatter4" else ins[i]
                loc = pltpu.make_async_copy(src, outs[i].at[me], local_sems.at[i])
                loc.start()
                started.append(loc)
        sends, recvs = [], []
        for i in range(n):
            for f, (fx, fy, fc) in enumerate(flips):
                peer = (x ^ fx, y ^ fy, c ^ fc)
                ps = slot_of(*peer)
                if mode == "sibling":
                    src, dst, landed = ins[i], outs[i], outs[i]
                elif mode == "scatter4":
                    src, dst, landed = ins[i].at[ps], outs[i].at[me], outs[i].at[ps]
                else:
                    src, dst, landed = ins[i], outs[i].at[me], outs[i].at[ps]
                k = i * n_f + f
                cp = pltpu.make_async_remote_copy(src_ref=src, dst_ref=dst, send_sem=send_sems.at[k],
                                                  recv_sem=recv_sems.at[k], device_id=peer,
                                                  device_id_type=pl.DeviceIdType.MESH)
                cp.start()
                sends.append(cp)
                recvs.append(pltpu.make_async_remote_copy(src_ref=src, dst_ref=landed, send_sem=send_sems.at[k],
                                                          recv_sem=recv_sems.at[k], device_id=peer,
                                                          device_id_type=pl.DeviceIdType.MESH))
        for cp in sends:
            cp.wait_send()
        for cp in recvs:
            cp.wait_recv()
        for loc in started:
            loc.wait()

    def out_sds(a):
        if mode == "gather4":
            return _sds((N_CHIP,) + a.shape, a.dtype)
        if mode == "gather8":
            return _sds((N_DEV,) + a.shape, a.dtype)
        return _sds(a.shape, a.dtype)

    res = pl.pallas_call(
        body, name=name, in_specs=[HBM_SPEC] * n, out_specs=[HBM_SPEC] * n, out_shape=[out_sds(a) for a in arrs],
        scratch_shapes=[pltpu.SemaphoreType.DMA((n * n_f,)), pltpu.SemaphoreType.DMA((n * n_f,)),
                        pltpu.SemaphoreType.DMA((max(n, 1),))],
    )(*arrs)
    return list(res)


SEM_SPEC = pl.BlockSpec(memory_space=pltpu.SEMAPHORE)
SPLIT_EFFECT = pltpu.SideEffectType.DATAFLOW_SIDE_EFFECTING
CHIP_FLIPS = ((1, 0), (0, 1), (1, 1))


def _chip_copies(src_refs, land_refs, send_sems, recv_sems, scatter):
    x, y, c = lax.axis_index("x"), lax.axis_index("y"), lax.axis_index("c")
    me = 2 * x + y
    out = []
    for i, land in enumerate(land_refs):
        for f, (fx, fy) in enumerate(CHIP_FLIPS):
            peer = (x ^ fx, y ^ fy, c)
            ps = 2 * (x ^ fx) + (y ^ fy)
            src = src_refs[i].at[ps] if scatter else land.at[me]
            k = i * len(CHIP_FLIPS) + f
            kw = dict(send_sem=send_sems.at[k], recv_sem=recv_sems.at[k], device_id=peer,
                      device_id_type=pl.DeviceIdType.MESH)
            out.append((pltpu.make_async_remote_copy(src_ref=src, dst_ref=land.at[me], **kw),
                        pltpu.make_async_remote_copy(src_ref=src, dst_ref=land.at[ps], **kw)))
    return out


def _split_start(name, srcs, lands, scatter):
    n = len(lands)
    n_src = len(srcs)
    n_sem = n * len(CHIP_FLIPS)

    def body(*refs):
        src_refs = refs[:n_src]
        land_refs = refs[n_src:n_src + n]
        send_sems, recv_sems = refs[n_src + n], refs[n_src + n + 1]
        token = refs[-1]
        for start, _ in _chip_copies(src_refs, land_refs, send_sems, recv_sems, scatter):
            start.start()
        token[...] = jnp.zeros_like(token)

    arrs = list(srcs) + list(lands)
    res = pl.pallas_call(
        body, name=name,
        out_shape=(pltpu.SemaphoreType.DMA((n_sem,)), pltpu.SemaphoreType.DMA((n_sem,)),
                   *[pltpu.HBM(a.shape, a.dtype) for a in arrs], _sds((8, LANES), F32)),
        in_specs=[HBM_SPEC] * len(arrs),
        out_specs=(SEM_SPEC, SEM_SPEC, *[HBM_SPEC] * len(arrs), pl.BlockSpec(memory_space=pltpu.VMEM)),
        input_output_aliases={i: 2 + i for i in range(len(arrs))},
        compiler_params=pltpu.CompilerParams(has_side_effects=SPLIT_EFFECT),
    )(*[pltpu.with_memory_space_constraint(a, pltpu.HBM) for a in arrs])
    return dict(send=res[0], recv=res[1], srcs=list(res[2:2 + n_src]), lands=list(res[2 + n_src:2 + n_src + n]),
                token=res[-1], scatter=scatter)


def _split_wait(name, started, after):
    srcs, lands, scatter = started["srcs"], started["lands"], started["scatter"]
    n, n_src = len(lands), len(srcs)

    def body(*refs):
        src_refs = refs[:n_src]
        land_refs = refs[n_src:n_src + n]
        send_sems, recv_sems = refs[n_src + n], refs[n_src + n + 1]
        for _, wait in _chip_copies(src_refs, land_refs, send_sems, recv_sems, scatter):
            wait.wait_send()
            wait.wait_recv()

    arrs = list(srcs) + list(lands)
    res = pl.pallas_call(
        body, name=name, out_shape=tuple(pltpu.HBM(a.shape, a.dtype) for a in arrs),
        in_specs=[HBM_SPEC] * len(arrs) + [SEM_SPEC, SEM_SPEC, pl.BlockSpec(memory_space=pl.ANY)],
        out_specs=tuple([HBM_SPEC] * len(arrs)), input_output_aliases={i: i for i in range(len(arrs))},
        compiler_params=pltpu.CompilerParams(has_side_effects=SPLIT_EFFECT),
    )(*arrs, started["send"], started["recv"], after)
    return list(res[n_src:])


def _flat_tile(rows):
    for cand in (512, 256, 128, 64, 32, 16, 8):
        if rows % cand == 0:
            return cand
    return rows


def _cast_slot(name, a, l, chip):
    r, c = a.shape[0] // DEPTH, a.shape[1]
    tr = _flat_tile(r)

    def body(chip_ref, a_ref, o_ref):
        o_ref[...] = a_ref[...].astype(BF16)

    return pl.pallas_call(
        body, name=name, out_shape=_sds((N_CHIP, r, c), BF16),
        grid_spec=pltpu.PrefetchScalarGridSpec(
            num_scalar_prefetch=1, grid=(r // tr,),
            in_specs=[pl.BlockSpec((tr, c), lambda i, ch: (l * (r // tr) + i, 0))],
            out_specs=pl.BlockSpec((None, tr, c), lambda i, ch: (ch[0], i, 0))),
        compiler_params=_params(("parallel",)))(chip, a)


def _own_slot(name, g, chip):
    n, r, c = g.shape
    tr = _flat_tile(r)

    def body(chip_ref, g_ref, o_ref):
        o_ref[...] = g_ref[...]

    spec = pl.BlockSpec((None, tr, c), lambda i, ch: (ch[0], i, 0))
    return pl.pallas_call(
        body, name=name, out_shape=_sds(g.shape, g.dtype),
        grid_spec=pltpu.PrefetchScalarGridSpec(num_scalar_prefetch=1, grid=(r // tr,), in_specs=[spec], out_specs=spec),
        compiler_params=_params(("parallel",)))(chip, g)


def _sum_layers(name, lands):
    n, r, c = lands[0].shape
    tr = _flat_tile(r)
    nl = len(lands)

    def body(*refs):
        o_ref = refs[-1]
        for k in range(nl):
            @pl.when(pl.program_id(0) == k)
            def _():
                tot = refs[k][0].astype(F32)
                for i in range(1, n):
                    tot = tot + refs[k][i].astype(F32)
                o_ref[...] = tot

    return pl.pallas_call(
        body, name=name, grid=(nl, r // tr),
        in_specs=[pl.BlockSpec((n, tr, c), lambda l, i, k=k: (0, jnp.where(l == k, i, 0), 0)) for k in range(nl)],
        out_specs=pl.BlockSpec((tr, c), lambda l, i: (l * (r // tr) + i, 0)), out_shape=_sds((nl * r, c), F32),
        compiler_params=_params(("arbitrary", "arbitrary")))(*lands)


def _sum_slots(name, a, scale=None):
    n, r, c = a.shape
    tr = _flat_tile(r)

    def body(a_ref, o_ref):
        tot = a_ref[0].astype(F32)
        for i in range(1, n):
            tot = tot + a_ref[i].astype(F32)
        o_ref[...] = tot

    return pl.pallas_call(body, name=name, grid=(r // tr,),
                          in_specs=[pl.BlockSpec((n, tr, c), lambda i: (0, i, 0))],
                          out_specs=pl.BlockSpec((tr, c), lambda i: (i, 0)), out_shape=_sds((r, c), F32),
                          compiler_params=_params(("parallel",)))(a)


def _adamw(name, w, ga, gb, m, v):
    r, c = w.shape
    tr = _flat_tile(r)
    c1 = 1.0 - B1 ** STEP
    c2 = 1.0 - B2 ** STEP
    two = gb is not None

    def body(*refs):
        if two:
            w_ref, ga_ref, gb_ref, m_ref, v_ref, g_out, d_out, m_out, v_out = refs
            g = ga_ref[...] + gb_ref[...]
        else:
            w_ref, ga_ref, m_ref, v_ref, g_out, d_out, m_out, v_out = refs
            g = ga_ref[...]
        mn = B1 * m_ref[...] + (1.0 - B1) * g
        vn = B2 * v_ref[...] + (1.0 - B2) * (g * g)
        g_out[...] = g
        m_out[...] = mn
        v_out[...] = vn
        d_out[...] = -LR * ((mn / c1) / (jnp.sqrt(vn / c2) + ADAM_EPS) + WD * w_ref[...])

    spec = pl.BlockSpec((tr, c), lambda i: (i, 0))
    ins = [w, ga, gb, m, v] if two else [w, ga, m, v]
    return pl.pallas_call(body, name=name, grid=(r // tr,), in_specs=[spec] * len(ins), out_specs=[spec] * 4,
                          out_shape=[_sds((r, c), F32)] * 4, compiler_params=_params(("parallel",)))(*ins)


WEIGHTS = ('mix_norm_w', 'w_in', 'q_norm_w', 'k_norm_w', 'hgrn_lb_logits', 'hgrn_gnorm_w', 'conv_dw_w', 'conv_dw_b',
           'conv_ln_w', 'conv_ln_b', 'conv_pw_w', 'conv_pw_b', 'attn_out_norm_w', 'conv_out_norm_w', 'w_out',
           'ffn_norm_w', 'w_gate', 'w_up', 'w_down')
SHARDED_SMALL = {"hgrn_lb_logits": 2, "conv_dw_w": 2, "conv_pw_w": 1}
LANES = 128
PACK_ROWS = 256


def _pack(parts):
    flat = jnp.concatenate([p.reshape(-1) for p in parts])
    n = flat.shape[0]
    rows = -(-n // (PACK_ROWS * LANES)) * PACK_ROWS
    return jnp.pad(flat, (0, rows * LANES - n)).reshape(rows, LANES)


def _unpack(packed, shapes):
    flat = packed.reshape(-1)
    out, off = [], 0
    for shp in shapes:
        n = int(np.prod(shp))
        out.append(flat[off:off + n].reshape(shp))
        off += n
    return out


def kernel(x, mix_norm_w, w_in, q_norm_w, k_norm_w, hgrn_lb_logits, hgrn_gnorm_w, conv_dw_w, conv_dw_b, conv_ln_w, conv_ln_b, conv_pw_w, conv_pw_b, attn_out_norm_w, conv_out_norm_w, w_out, ffn_norm_w, w_gate, w_up, w_down, loss_target, m_mix_norm_w, m_w_in, m_q_norm_w, m_k_norm_w, m_hgrn_lb_logits, m_hgrn_gnorm_w, m_conv_dw_w, m_conv_dw_b, m_conv_ln_w, m_conv_ln_b, m_conv_pw_w, m_conv_pw_b, m_attn_out_norm_w, m_conv_out_norm_w, m_w_out, m_ffn_norm_w, m_w_gate, m_w_up, m_w_down, v_mix_norm_w, v_w_in, v_q_norm_w, v_k_norm_w, v_hgrn_lb_logits, v_hgrn_gnorm_w, v_conv_dw_w, v_conv_dw_b, v_conv_ln_w, v_conv_ln_b, v_conv_pw_w, v_conv_pw_b, v_attn_out_norm_w, v_conv_out_norm_w, v_w_out, v_ffn_norm_w, v_w_gate, v_w_up, v_w_down):
    w = dict(mix_norm_w=mix_norm_w, w_in=w_in, q_norm_w=q_norm_w, k_norm_w=k_norm_w, hgrn_lb_logits=hgrn_lb_logits,
             hgrn_gnorm_w=hgrn_gnorm_w, conv_dw_w=conv_dw_w, conv_dw_b=conv_dw_b, conv_ln_w=conv_ln_w,
             conv_ln_b=conv_ln_b, conv_pw_w=conv_pw_w, conv_pw_b=conv_pw_b, attn_out_norm_w=attn_out_norm_w,
             conv_out_norm_w=conv_out_norm_w, w_out=w_out, ffn_norm_w=ffn_norm_w, w_gate=w_gate, w_up=w_up,
             w_down=w_down)
    m = dict(mix_norm_w=m_mix_norm_w, w_in=m_w_in, q_norm_w=m_q_norm_w, k_norm_w=m_k_norm_w,
             hgrn_lb_logits=m_hgrn_lb_logits, hgrn_gnorm_w=m_hgrn_gnorm_w, conv_dw_w=m_conv_dw_w,
             conv_dw_b=m_conv_dw_b, conv_ln_w=m_conv_ln_w, conv_ln_b=m_conv_ln_b, conv_pw_w=m_conv_pw_w,
             conv_pw_b=m_conv_pw_b, attn_out_norm_w=m_attn_out_norm_w, conv_out_norm_w=m_conv_out_norm_w,
             w_out=m_w_out, ffn_norm_w=m_ffn_norm_w, w_gate=m_w_gate, w_up=m_w_up, w_down=m_w_down)
    v = dict(mix_norm_w=v_mix_norm_w, w_in=v_w_in, q_norm_w=v_q_norm_w, k_norm_w=v_k_norm_w,
             hgrn_lb_logits=v_hgrn_lb_logits, hgrn_gnorm_w=v_hgrn_gnorm_w, conv_dw_w=v_conv_dw_w,
             conv_dw_b=v_conv_dw_b, conv_ln_w=v_conv_ln_w, conv_ln_b=v_conv_ln_b, conv_pw_w=v_conv_pw_w,
             conv_pw_b=v_conv_pw_b, attn_out_norm_w=v_attn_out_norm_w, conv_out_norm_w=v_conv_out_norm_w,
             w_out=v_w_out, ffn_norm_w=v_ffn_norm_w, w_gate=v_w_gate, w_up=v_w_up, w_down=v_w_down)
    chip = 2 * lax.axis_index("x") + lax.axis_index("y")

    chip1 = chip.reshape(1).astype(jnp.int32)

    flat2 = lambda a: a.reshape(-1, a.shape[-1])
    small_pack = _pack([w[n] for n in SHARDED_SMALL])
    gathered = _exchange("gather_small_weights", [small_pack], "gather4")
    groups = [[(0, "w_in")], [(0, n) for n in BIG_ORDER[1:]], [(1, n) for n in BIG_ORDER]]
    group_of = {key: g for g, keys in enumerate(groups) for key in keys}
    starts = []
    for g, keys in enumerate(groups):
        slots = [_cast_slot("cast_%s_l%d" % (n, l), flat2(w[n]), l, chip1) for l, n in keys]
        starts.append(_split_start("gather_start_g%d" % g, [], slots, False))
    got = {}

    def wget(l, name, after):
        if (l, name) not in got:
            g = group_of[(l, name)]
            for key, arr in zip(groups[g], _split_wait("gather_wait_g%d" % g, starts[g], after)):
                got[key] = arr
        return got[(l, name)]

    pending = []

    def on_grads(l, grads):
        names = [n for n in SCATTER_ORDER if n in grads]
        own = [_own_slot("own_%s_l%d" % (n, l), grads[n], chip1) for n in names]
        st = _split_start("scatter_start_l%d_%s" % (l, names[0]), [grads[n] for n in names], own, True)
        pending.append((l, names, st))
        return [st["token"]]

    parts = [_unpack(gathered[-1][j], [w[n].shape for n in SHARDED_SMALL]) for j in range(N_CHIP)]
    full_small = {n: jnp.concatenate([parts[j][i] for j in range(N_CHIP)], axis=ax)
                  for i, (n, ax) in enumerate(SHARDED_SMALL.items())}
    sm = {n: w[n] for n in WEIGHTS if n not in BIG_ORDER and n not in SHARDED_SMALL}
    sm["conv_dw_w"] = full_small["conv_dw_w"]
    sm["conv_pw_w"] = full_small["conv_pw_w"]
    logits = full_small["hgrn_lb_logits"].reshape(DEPTH * 2, D_HGRN)
    sm["lb"] = _lower_bounds(logits).reshape(DEPTH, 2, D_HGRN)

    sq_sum, grad_x, smalls = _local_step(x, loss_target, wget, sm, [st["token"] for st in starts], on_grads)
    loss = lax.psum(0.5 * sq_sum / D_MODEL, ("x", "y", "c"))

    landed = {}
    for l, names, st in pending:
        for n, arr in zip(names, _split_wait("scatter_wait_l%d_%s" % (l, names[0]), st, grad_x)):
            landed[(l, n)] = arr
    sums = [_sum_layers("sum_" + n, [landed[(l, n)] for l in range(DEPTH)]) for n in SCATTER_ORDER]
    sib = _exchange("sibling_grads", sums, "sibling")
    out = {}
    for n, ga, gb in zip(SCATTER_ORDER, sums, sib):
        res = _adamw("adamw_" + n, flat2(w[n]), ga, gb, flat2(m[n]), flat2(v[n]))
        out[n] = [r.reshape(w[n].shape) for r in res]

    small_names = [n for n in WEIGHTS if n not in SCATTER_ORDER]
    g_pack = _pack([jnp.stack([smalls[l][n] for l in range(DEPTH)]) for n in SMALL_ORDER])
    g_all = _exchange("gather_small_grads", [g_pack], "gather8")[0]
    g_tot = _sum_slots("sum_small", g_all)
    shapes = [(DEPTH,) + tuple(smalls[0][n].shape) for n in SMALL_ORDER]
    g_small = dict(zip(SMALL_ORDER, _unpack(g_tot, shapes)))
    lb_shard = lax.dynamic_slice_in_dim(g_small.pop("lb").reshape(DEPTH * 2, D_HGRN), chip * HEAD_DIM, HEAD_DIM, 1)
    g_small["hgrn_lb_logits"] = _lower_bounds_bwd(hgrn_lb_logits.reshape(DEPTH * 2, HEAD_DIM), lb_shard).reshape(
        hgrn_lb_logits.shape)
    g_small["conv_dw_w"] = lax.dynamic_slice_in_dim(g_small["conv_dw_w"], chip * HEAD_DIM, HEAD_DIM, 2)
    res = _adamw("adamw_small", _pack([w[n] for n in small_names]), _pack([g_small[n] for n in small_names]), None,
                 _pack([m[n] for n in small_names]), _pack([v[n] for n in small_names]))
    unpacked = [_unpack(r, [w[n].shape for n in small_names]) for r in res]
    for i, n in enumerate(small_names):
        out[n] = [unpacked[k][i] for k in range(4)]

    return (loss, grad_x, *[out[n][0] for n in WEIGHTS], *[out[n][1] for n in WEIGHTS],
            *[out[n][2] for n in WEIGHTS], *[out[n][3] for n in WEIGHTS])
```

```python
import functools

import numpy as np
import jax
import jax.numpy as jnp
from jax import lax
from jax.experimental import pallas as pl
from jax.experimental.pallas import tpu as pltpu

F32, BF16 = jnp.float32, jnp.bfloat16

D_MODEL = 1024
DEPTH = 2
GRID_W = 64
D_ATTN, D_HGRN, D_CONV = 512, 256, 256
HEAD_DIM = 64
N_KV = 2
KV_LANES = D_ATTN // N_KV
ROPE_THETA = 10000.0
F_MIN = 1e-6
CONV_W = 31
CONV_PAD = 15
D_FF = 2816
D_IN = 2560
N_CHIP = 4
N_DEV = 8
IN_BLK = D_IN // N_CHIP
FF_BLK = D_FF // N_CHIP
OUT_BLK = D_MODEL // N_CHIP
EPS = 1e-6
LN_EPS = 1e-5
LR, B1, B2, ADAM_EPS, WD, STEP = 0.001, 0.9, 0.999, 1e-08, 0.01, 10
CHUNK = 16
HBLK = 256
CONV_TILE = 128
BWD_GROUP = 2
VMEM_LIMIT = 56 * 1024 * 1024

COL_Q, COL_K, COL_V = 0, 4, 5
COL_HQ, COL_FF, COL_FB, COL_HI, COL_HG, COL_CA, COL_CB = 3, 4, 5, 6, 7, 8, 9


def _params(sem=None):
    return pltpu.CompilerParams(dimension_semantics=sem, vmem_limit_bytes=VMEM_LIMIT)


def _sds(shape, dtype):
    return jax.ShapeDtypeStruct(tuple(shape), dtype)


def _full(shape):
    n = len(shape)
    return pl.BlockSpec(tuple(shape), lambda *_: (0,) * n)


def _sigmoid(x):
    return 0.5 * jnp.tanh(0.5 * x) + 0.5


def _gate_sigmoid(x):
    return 1.0 / (1.0 + jnp.exp(-x))


def _silu(x):
    return x * _sigmoid(x)


def _dsilu(x):
    s = _sigmoid(x)
    return s * (1.0 + x * (1.0 - s))


def _rowgroups(v):
    m, c = v.shape
    return v.reshape(m // 8, 8, c).sum(axis=0)


def _split2(x):
    hi = x.astype(BF16)
    lo = (x - hi.astype(F32)).astype(BF16)
    return hi, lo


def _rdot2(x, m):
    hi, lo = _split2(x)
    return (jnp.dot(hi, m, preferred_element_type=F32) + jnp.dot(lo, m, preferred_element_type=F32))


def _ldot3(m, x):
    hi = x.astype(BF16)
    r1 = x - hi.astype(F32)
    mid = r1.astype(BF16)
    lo = (r1 - mid.astype(F32)).astype(BF16)
    return (jnp.dot(m, hi, preferred_element_type=F32) + jnp.dot(m, mid, preferred_element_type=F32)
            + jnp.dot(m, lo, preferred_element_type=F32))


def _dot_nt(a, b):
    return lax.dot_general(a, b, (((1,), (1,)), ((), ())), preferred_element_type=F32)


def _dot_tn(a, b):
    return lax.dot_general(a, b, (((0,), (0,)), ((), ())), preferred_element_type=F32)


def _seg_matrix(n, seg, val):
    i = np.arange(n)
    return ((i[:, None] // seg) == (i[None, :] // seg)).astype(np.float32) * val


def _rot_matrix(n):
    r = np.zeros((n, n), np.float32)
    for i in range(n):
        if (i % 32) < 16:
            r[i + 16, i] = -1.0
        else:
            r[i - 16, i] = 1.0
    return r


def _rep_matrix():
    r = np.zeros((N_KV * HEAD_DIM, D_ATTN), np.float32)
    for kv in range(N_KV):
        for g in range(KV_LANES // HEAD_DIM):
            for d in range(HEAD_DIM):
                r[HEAD_DIM * kv + d, KV_LANES * kv + HEAD_DIM * g + d] = 1.0
    return r


def _cumsum_matrix(rev):
    i = np.arange(HBLK)
    same = (i[:, None] // CHUNK) == (i[None, :] // CHUNK)
    tri = (i[None, :] >= i[:, None]) if rev else (i[None, :] <= i[:, None])
    return (same & tri).astype(np.float32)


def _sel_matrices():
    sel = np.zeros((CHUNK, CHUNK * CHUNK), np.float32)
    selt = np.zeros((CHUNK, CHUNK * CHUNK), np.float32)
    for t in range(CHUNK):
        for s in range(CHUNK):
            sel[t, t * CHUNK + s] = 1.0
            selt[s, t * CHUNK + s] = 1.0
    return sel, selt


def _bf(a):
    return jnp.asarray(a, dtype=BF16)


def _mm(name, grid, pairs, extras, outs, epilogue, acc=None, sem=None):
    n_p, n_e, n_o = len(pairs), len(extras), len(outs)

    def body(*refs):
        ab = refs[:2 * n_p]
        ex = refs[2 * n_p:2 * n_p + n_e]
        out = refs[2 * n_p + n_e:2 * n_p + n_e + n_o]
        scr = refs[2 * n_p + n_e + n_o:]
        tot = None
        for i in range(n_p):
            a = ab[2 * i][...]
            b = ab[2 * i + 1][...]
            if a.ndim == 3:
                a = a.reshape(-1, a.shape[-1])
            if b.ndim == 3:
                b = b.reshape(-1, b.shape[-1])
            r = lax.dot_general(a.astype(BF16), b.astype(BF16), pairs[i][4], preferred_element_type=F32)
            tot = r if tot is None else tot + r

        def finish(total):
            res = epilogue(total, *[e[...] for e in ex])
            for o_ref, val in zip(out, res):
                o_ref[...] = val.astype(o_ref.dtype)

        if acc is None:
            finish(tot)
        else:
            k = pl.program_id(acc[0])

            @pl.when(k == 0)
            def _():
                scr[0][...] = tot

            @pl.when(k > 0)
            def _():
                scr[0][...] += tot

            @pl.when(k == grid[acc[0]] - 1)
            def _():
                finish(scr[0][...])

    args, in_specs = [], []
    for a, a_spec, b, b_spec, _ in pairs:
        args += [a, b]
        in_specs += [a_spec, b_spec]
    for e, e_spec in extras:
        args.append(e)
        in_specs.append(e_spec)
    if sem is None:
        sem = tuple("arbitrary" if (acc is not None and i == acc[0]) else "parallel" for i in range(len(grid)))
    return pl.pallas_call(
        body, name=name, grid=grid, in_specs=in_specs,
        out_specs=[o[1] for o in outs], out_shape=[o[0] for o in outs],
        scratch_shapes=[] if acc is None else [pltpu.VMEM(acc[1], F32)],
        compiler_params=_params(sem),
    )(*args)


NN = (((1,), (0,)), ((), ()))
NT = (((1,), (1,)), ((), ()))
TN = (((0,), (0,)), ((), ()))


def _row_tile(t):
    return min(256, t)


def _rms_fwd(name, x, w, deps=()):
    t, d = x.shape
    tm = _row_tile(t)

    def body(x_ref, w_ref, *rest):
        o_ref = rest[-1]
        xv = x_ref[...]
        r = lax.rsqrt(jnp.mean(xv * xv, axis=-1, keepdims=True) + EPS)
        o_ref[...] = (xv * r * w_ref[...]).astype(BF16)

    return pl.pallas_call(
        body, name=name, grid=(t // tm,),
        in_specs=[pl.BlockSpec((tm, d), lambda i: (i, 0)), _full((1, d))] + [_full(a.shape) for a in deps],
        out_specs=pl.BlockSpec((tm, d), lambda i: (i, 0)), out_shape=_sds((t, d), BF16),
        compiler_params=_params(("parallel",)),
    )(x, w, *deps)


def _rms_bwd(name, x, w, dh, dres, deps=()):
    t, d = x.shape
    tm = _row_tile(t)

    def body(x_ref, w_ref, dh_ref, dres_ref, *rest):
        dx_ref, dw_ref = rest[-2:]
        xv = x_ref[...]
        r = lax.rsqrt(jnp.mean(xv * xv, axis=-1, keepdims=True) + EPS)
        dy = dh_ref[...]
        gw = dy * w_ref[...]
        dx_ref[...] = dres_ref[...] + r * gw - xv * (r * r * r) * jnp.mean(gw * xv, axis=-1, keepdims=True)

        @pl.when(pl.program_id(0) == 0)
        def _():
            dw_ref[...] = jnp.zeros_like(dw_ref)

        dw_ref[...] += _rowgroups(dy * xv * r)

    tile = pl.BlockSpec((tm, d), lambda i: (i, 0))
    return pl.pallas_call(
        body, name=name, grid=(t // tm,),
        in_specs=[tile, _full((1, d)), tile, tile] + [_full(a.shape) for a in deps],
        out_specs=[tile, _full((8, d))], out_shape=[_sds((t, d), F32), _sds((8, d), F32)],
        compiler_params=_params(("arbitrary",)),
    )(x, w, dh, dres, *deps)


def _loss_kernel(y, target):
    t, d = y.shape
    tm = _row_tile(t)

    def body(y_ref, t_ref, dy_ref, acc_ref):
        e = y_ref[...] - t_ref[...]
        dy_ref[...] = e * (1.0 / d)

        @pl.when(pl.program_id(0) == 0)
        def _():
            acc_ref[...] = jnp.zeros_like(acc_ref)

        acc_ref[...] += _rowgroups(e * e)

    tile = pl.BlockSpec((tm, d), lambda i: (i, 0))
    return pl.pallas_call(
        body, name="loss_head", grid=(t // tm,), in_specs=[tile, tile],
        out_specs=[tile, _full((8, d))], out_shape=[_sds((t, d), F32), _sds((8, d), F32)],
        compiler_params=_params(("arbitrary",)),
    )(y, target)


def _rope_tables(s):
    rows = s // GRID_W
    row_id = jnp.repeat(jnp.arange(rows, dtype=F32), GRID_W)
    col_id = jnp.tile(jnp.arange(GRID_W, dtype=F32), rows)
    half = HEAD_DIM // 2
    inv_freq = ROPE_THETA ** (-jnp.arange(0, half, 2, dtype=F32) / half)
    ang_r = row_id[:, None] * inv_freq[None, :]
    ang_c = col_id[:, None] * inv_freq[None, :]
    ang = jnp.concatenate([ang_r, ang_r, ang_c, ang_c], axis=-1)
    return jnp.cos(ang).astype(F32), jnp.sin(ang).astype(F32)


def _attn_consts():
    return dict(
        seg_q=_bf(_seg_matrix(D_ATTN, HEAD_DIM, 1.0 / HEAD_DIM)),
        seg_k=_bf(_seg_matrix(N_KV * HEAD_DIM, HEAD_DIM, 1.0 / HEAD_DIM)),
        rot_q=_bf(_rot_matrix(D_ATTN)), rot_k=_bf(_rot_matrix(N_KV * HEAD_DIM)),
        rep=_bf(_rep_matrix()), rep_t=_bf(_rep_matrix().T))


def _attn_prep(name, proj, s, tabs, qw, kw, ac):
    t = proj.shape[0]
    tm = _row_tile(s)
    nst = s // tm
    kw_ = N_KV * HEAD_DIM

    def body(q_ref, k_ref, v_ref, cq_ref, sq_ref, ck_ref, sk_ref, qw_ref, kw_ref,
             segq_ref, segk_ref, rotq_ref, rotk_ref, rep_ref, qn_ref, kr_ref, vr_ref):
        q = q_ref[...]
        r = lax.rsqrt(jnp.dot((q * q).astype(BF16), segq_ref[...], preferred_element_type=F32) + EPS)
        qn = q * r * qw_ref[...]
        qr = qn * cq_ref[...] + _rdot2(qn, rotq_ref[...]) * sq_ref[...]
        qn_ref[...] = (qr * (HEAD_DIM ** -0.5)).astype(BF16)
        k = k_ref[...]
        rk = lax.rsqrt(jnp.dot((k * k).astype(BF16), segk_ref[...], preferred_element_type=F32) + EPS)
        kn = k * rk * kw_ref[...]
        kr = kn * ck_ref[...] + _rdot2(kn, rotk_ref[...]) * sk_ref[...]
        kr_ref[...] = jnp.dot(kr.astype(BF16), rep_ref[...], preferred_element_type=F32).astype(BF16)
        vr_ref[...] = jnp.dot(v_ref[...].astype(BF16), rep_ref[...], preferred_element_type=F32).astype(BF16)

    wide = pl.BlockSpec((tm, D_ATTN), lambda i: (i, 0))
    tabq = pl.BlockSpec((tm, D_ATTN), lambda i: (i % nst, 0))
    tabk = pl.BlockSpec((tm, kw_), lambda i: (i % nst, 0))
    return pl.pallas_call(
        body, name=name, grid=(t // tm,),
        in_specs=[pl.BlockSpec((tm, D_ATTN), lambda i: (i, COL_Q)), pl.BlockSpec((tm, kw_), lambda i: (i, COL_K)),
                  pl.BlockSpec((tm, kw_), lambda i: (i, COL_V)), tabq, tabq, tabk, tabk,
                  _full((1, D_ATTN)), _full((1, kw_)), _full((D_ATTN, D_ATTN)), _full((kw_, kw_)),
                  _full((D_ATTN, D_ATTN)), _full((kw_, kw_)), _full((kw_, D_ATTN))],
        out_specs=[wide, wide, wide], out_shape=[_sds((t, D_ATTN), BF16)] * 3,
        compiler_params=_params(("parallel",)),
    )(proj, proj, proj, tabs["cq"], tabs["sq"], tabs["ck"], tabs["sk"], qw, kw,
      ac["seg_q"], ac["seg_k"], ac["rot_q"], ac["rot_k"], ac["rep"])


def _attn_prep_bwd(name, proj, s, tabs, qw, kw, ac, dqs, dkr, dvr):
    t = proj.shape[0]
    tm = _row_tile(s)
    nst = s // tm
    kw_ = N_KV * HEAD_DIM
    wout = D_ATTN + 2 * kw_

    def norm_rope_bwd(x, w, cos, sin, seg, rot, d_roped):
        dn = d_roped * cos - _rdot2(d_roped * sin, rot)
        r = lax.rsqrt(jnp.dot((x * x).astype(BF16), seg, preferred_element_type=F32) + EPS)
        gw = dn * w
        dx = r * gw - x * (r * r * r) * _rdot2(gw * x, seg)
        return dx, _rowgroups(dn * x * r)

    def body(q_ref, k_ref, cq_ref, sq_ref, ck_ref, sk_ref, qw_ref, kw_ref, segq_ref, segk_ref, rotq_ref, rotk_ref,
             rept_ref, dqs_ref, dkr_ref, dvr_ref, dp_ref, dqw_ref, dkw_ref):
        dq, dqw = norm_rope_bwd(q_ref[...], qw_ref[...], cq_ref[...], sq_ref[...], segq_ref[...], rotq_ref[...],
                                dqs_ref[...] * (HEAD_DIM ** -0.5))
        dk_roped = _rdot2(dkr_ref[...], rept_ref[...])
        dk, dkw = norm_rope_bwd(k_ref[...], kw_ref[...], ck_ref[...], sk_ref[...], segk_ref[...], rotk_ref[...],
                                dk_roped)
        dv = _rdot2(dvr_ref[...], rept_ref[...])
        dp_ref[:, 0:D_ATTN] = dq.astype(BF16)
        dp_ref[:, D_ATTN:D_ATTN + kw_] = dk.astype(BF16)
        dp_ref[:, D_ATTN + kw_:wout] = dv.astype(BF16)

        @pl.when(pl.program_id(0) == 0)
        def _():
            dqw_ref[...] = jnp.zeros_like(dqw_ref)
            dkw_ref[...] = jnp.zeros_like(dkw_ref)

        dqw_ref[...] += dqw
        dkw_ref[...] += dkw

    wide = pl.BlockSpec((tm, D_ATTN), lambda i: (i, 0))
    tabq = pl.BlockSpec((tm, D_ATTN), lambda i: (i % nst, 0))
    tabk = pl.BlockSpec((tm, kw_), lambda i: (i % nst, 0))
    return pl.pallas_call(
        body, name=name, grid=(t // tm,),
        in_specs=[pl.BlockSpec((tm, D_ATTN), lambda i: (i, COL_Q)), pl.BlockSpec((tm, kw_), lambda i: (i, COL_K)),
                  tabq, tabq, tabk, tabk, _full((1, D_ATTN)), _full((1, kw_)),
                  _full((D_ATTN, D_ATTN)), _full((kw_, kw_)), _full((D_ATTN, D_ATTN)), _full((kw_, kw_)),
                  _full((D_ATTN, kw_)), wide, wide, wide],
        out_specs=[pl.BlockSpec((tm, wout), lambda i: (i, 0)), _full((8, D_ATTN)), _full((8, kw_))],
        out_shape=[_sds((t, wout), BF16), _sds((8, D_ATTN), F32), _sds((8, kw_), F32)],
        compiler_params=_params(("arbitrary",)),
    )(proj, proj, tabs["cq"], tabs["sq"], tabs["ck"], tabs["sk"], qw, kw,
      ac["seg_q"], ac["seg_k"], ac["rot_q"], ac["rot_k"], ac["rep_t"], dqs, dkr, dvr)


def _attn_tile(s):
    return min(256, s)


def _head_masks(shape):
    lane = lax.broadcasted_iota(jnp.int32, shape, 1)
    return [(lane // HEAD_DIM) == g for g in range(KV_LANES // HEAD_DIM)]


def _attn_fwd(name, qn, kr, vr, b, s):
    t = qn.shape[0]
    tq = _attn_tile(s)
    nq = s // tq

    def body(q_ref, k_ref, v_ref, o_ref):
        q = q_ref[...]
        k = k_ref[...]
        v = v_ref[...]
        acc = jnp.zeros((tq, KV_LANES), F32)
        for mask in _head_masks((tq, KV_LANES)):
            sc = _dot_nt(jnp.where(mask, q, jnp.zeros_like(q)), k)
            p = jnp.exp(sc - jnp.max(sc, axis=-1, keepdims=True))
            inv = 1.0 / jnp.sum(p, axis=-1, keepdims=True)
            og = jnp.dot(p.astype(BF16), v, preferred_element_type=F32) * inv
            acc = jnp.where(mask, og, acc)
        o_ref[...] = acc

    return pl.pallas_call(
        body, name=name, grid=(b, N_KV, nq),
        in_specs=[pl.BlockSpec((tq, KV_LANES), lambda bi, kv, i: (bi * nq + i, kv)),
                  pl.BlockSpec((s, KV_LANES), lambda bi, kv, i: (bi, kv)),
                  pl.BlockSpec((s, KV_LANES), lambda bi, kv, i: (bi, kv))],
        out_specs=pl.BlockSpec((tq, KV_LANES), lambda bi, kv, i: (bi * nq + i, kv)),
        out_shape=_sds((t, D_ATTN), F32),
        compiler_params=_params(("parallel", "parallel", "parallel")),
    )(qn, kr, vr)


def _attn_bwd(name, qn, kr, vr, do, b, s):
    t = qn.shape[0]
    tq = _attn_tile(s)
    nq = s // tq

    def body(q_ref, k_ref, v_ref, do_ref, dq_ref, dk_ref, dv_ref):
        @pl.when(pl.program_id(2) == 0)
        def _():
            dk_ref[...] = jnp.zeros_like(dk_ref)
            dv_ref[...] = jnp.zeros_like(dv_ref)

        q = q_ref[...]
        k = k_ref[...]
        v = v_ref[...]
        dout = do_ref[...].astype(BF16)
        masks = _head_masks((tq, KV_LANES))
        q4 = jnp.concatenate([jnp.where(m, q, jnp.zeros_like(q)) for m in masks], axis=0)
        do4 = jnp.concatenate([jnp.where(m, dout, jnp.zeros_like(dout)) for m in masks], axis=0)
        sc = _dot_nt(q4, k)
        p = jnp.exp(sc - jnp.max(sc, axis=-1, keepdims=True))
        p = p * (1.0 / jnp.sum(p, axis=-1, keepdims=True))
        dp = _dot_nt(do4, v)
        ds = (p * (dp - jnp.sum(p * dp, axis=-1, keepdims=True))).astype(BF16)
        dq4 = jnp.dot(ds, k, preferred_element_type=F32)
        dq = jnp.zeros((tq, KV_LANES), F32)
        for g, m in enumerate(masks):
            dq = jnp.where(m, dq4[g * tq:(g + 1) * tq, :], dq)
        dq_ref[...] = dq
        dk_ref[...] += _dot_tn(ds, q4)
        dv_ref[...] += _dot_tn(p.astype(BF16), do4)

    qspec = pl.BlockSpec((tq, KV_LANES), lambda bi, kv, i: (bi * nq + i, kv))
    kspec = pl.BlockSpec((s, KV_LANES), lambda bi, kv, i: (bi, kv))
    return pl.pallas_call(
        body, name=name, grid=(b, N_KV, nq),
        in_specs=[qspec, kspec, kspec, qspec],
        out_specs=[qspec, kspec, kspec], out_shape=[_sds((t, D_ATTN), F32)] * 3,
        compiler_params=_params(("parallel", "parallel", "arbitrary")),
    )(qn, kr, vr, do)


def _hgrn_consts(rev):
    sel, selt = _sel_matrices()
    cs = _cumsum_matrix(rev)
    return dict(cs=_bf(cs), cs_t=_bf(cs.T), seg=_bf(_seg_matrix(D_HGRN, HEAD_DIM, 1.0)),
                bd=jnp.asarray(_seg_matrix(D_HGRN, HEAD_DIM, 1.0), F32),
                sel=_bf(sel), selt=_bf(selt), seld=_bf(sel - selt))


def _gates(z, lb):
    sig = _gate_sigmoid(z)
    f = lb + (1.0 - lb) * sig
    g = jnp.log(jnp.maximum(f, F_MIN))
    sn = _gate_sigmoid(-z)
    return sig, f, g, sn, (1.0 - lb) * sn


def _pair_decay(b, rev):
    row = lax.broadcasted_iota(jnp.int32, (CHUNK, D_HGRN), 0)
    parts = []
    for t in range(CHUNK):
        m = (row >= t) if rev else (row <= t)
        parts.append(jnp.where(m, jnp.exp(jnp.minimum(b[t:t + 1, :] - b, 0.0)), 0.0))
    return jnp.concatenate(parts, axis=0)


def _rows_rep(a):
    return jnp.concatenate([jnp.broadcast_to(a[t:t + 1, :], a.shape) for t in range(CHUNK)], axis=0)


def _tile_rows(a):
    return jnp.concatenate([a] * CHUNK, axis=0)


def _hgrn_specs(b, s, rev):
    nb = s // HBLK

    def blk(j):
        return (nb - 1 - j) if rev else j

    def col(c):
        return pl.BlockSpec((HBLK, D_HGRN), lambda bi, j: (bi * nb + blk(j), c))

    return nb, blk, col


def _hgrn_fwd(name, proj, lb, b, s, rev, hc):
    t = proj.shape[0]
    nb, blk, col = _hgrn_specs(b, s, rev)
    n_ch = HBLK // CHUNK
    last = 0 if rev else CHUNK - 1

    def body(q_ref, z_ref, v_ref, lb_ref, cs_ref, seg_ref, bd_ref, sel_ref, o_ref, st_ref, state, b_scr, k_scr):
        @pl.when(pl.program_id(1) == 0)
        def _():
            state[...] = jnp.zeros_like(state)

        st_ref[...] = state[...]
        _, _, g, _, kk = _gates(z_ref[...], lb_ref[...])
        k_scr[...] = kk
        b_scr[...] = _ldot3(cs_ref[...], g)

        def chunk(i, carry):
            c = (n_ch - 1 - i) if rev else i
            rows = pl.ds(pl.multiple_of(c * CHUNK, CHUNK), CHUNK)
            q = q_ref[rows, :]
            k = k_scr[rows, :]
            v = v_ref[rows, :]
            bb = b_scr[rows, :]
            bl = bb[last:last + 1, :]
            pairs = _pair_decay(bb, rev) * _rows_rep(q) * _tile_rows(k)
            a = jnp.dot(pairs.astype(BF16), seg_ref[...], preferred_element_type=F32)
            o_intra = jnp.dot(sel_ref[...], (a * _tile_rows(v)).astype(BF16), preferred_element_type=F32)
            st = state[...]
            o_inter = _dot_nt((q * jnp.exp(bb)).astype(BF16), st.astype(BF16))
            o_ref[rows, :] = o_intra + o_inter
            ke = k * jnp.exp(bl - bb)
            state[...] = st * jnp.exp(bl) + bd_ref[...] * _dot_tn(v.astype(BF16), ke.astype(BF16))
            return carry

        lax.fori_loop(0, n_ch, chunk, 0)

    sq = (D_HGRN, D_HGRN)
    return pl.pallas_call(
        body, name=name, grid=(b, nb),
        in_specs=[col(COL_HQ), col(COL_FB if rev else COL_FF), col(COL_HI), _full((1, D_HGRN)),
                  _full((HBLK, HBLK)), _full(sq), _full(sq), _full((CHUNK, CHUNK * CHUNK))],
        out_specs=[pl.BlockSpec((HBLK, D_HGRN), lambda bi, j: (bi * nb + blk(j), 0)),
                   pl.BlockSpec((None,) + sq, lambda bi, j: (bi * nb + blk(j), 0, 0))],
        out_shape=[_sds((t, D_HGRN), F32), _sds((b * nb,) + sq, F32)],
        scratch_shapes=[pltpu.VMEM(sq, F32), pltpu.VMEM((HBLK, D_HGRN), F32), pltpu.VMEM((HBLK, D_HGRN), F32)],
        compiler_params=_params(("parallel", "arbitrary")),
    )(proj, proj, proj, lb, hc["cs"], hc["seg"], hc["bd"], hc["sel"])


def _hgrn_bwd(name, proj, lb, st_blk, do, dq_prev, dv_prev, b, s, rev, hc):
    t = proj.shape[0]
    nb = s // HBLK
    n_ch = HBLK // CHUNK
    last = 0 if rev else CHUNK - 1

    def blk(j):
        return j if rev else (nb - 1 - j)

    def col(c):
        return pl.BlockSpec((HBLK, D_HGRN), lambda bi, j: (bi * nb + blk(j), c))

    def body(q_ref, z_ref, v_ref, lb_ref, st_ref, do_ref, dqp_ref, dvp_ref, cs_ref, cst_ref, seg_ref, bd_ref,
             sel_ref, selt_ref, seld_ref, dq_ref, dv_ref, dz_ref, dlb_ref,
             dstate, states, b_scr, k_scr, db_scr, dk_scr):
        first = jnp.logical_and(pl.program_id(0) == 0, pl.program_id(1) == 0)

        @pl.when(first)
        def _():
            dlb_ref[...] = jnp.zeros_like(dlb_ref)

        @pl.when(pl.program_id(1) == 0)
        def _():
            dstate[...] = jnp.zeros_like(dstate)

        lbv = lb_ref[...]
        z = z_ref[...]
        sig, f, g, sn, kk = _gates(z, lbv)
        k_scr[...] = kk
        b_scr[...] = _ldot3(cs_ref[...], g)

        def rows_of(c):
            return pl.ds(pl.multiple_of(c * CHUNK, CHUNK), CHUNK)

        def replay(i, st):
            c = (n_ch - 1 - i) if rev else i
            rows = rows_of(c)
            states[c] = st
            bb = b_scr[rows, :]
            bl = bb[last:last + 1, :]
            ke = k_scr[rows, :] * jnp.exp(bl - bb)
            return st * jnp.exp(bl) + bd_ref[...] * _dot_tn(v_ref[rows, :].astype(BF16), ke.astype(BF16))

        lax.fori_loop(0, n_ch, replay, st_ref[...])
        row = lax.broadcasted_iota(jnp.int32, (CHUNK, D_HGRN), 0)

        def chunk(i, carry):
            c = i if rev else (n_ch - 1 - i)
            rows = rows_of(c)
            q = q_ref[rows, :]
            k = k_scr[rows, :]
            v = v_ref[rows, :]
            bb = b_scr[rows, :]
            dout = do_ref[rows, :]
            bl = bb[last:last + 1, :]
            st_p = states[c]
            dst_n = dstate[...]
            eb = jnp.exp(bb)
            ebl = jnp.exp(bl - bb)
            ebl_last = jnp.exp(bl)
            qe = q * eb
            ke = k * ebl
            dob = dout.astype(BF16)
            dstb = dst_n.astype(BF16)
            dqe = jnp.dot(dob, st_p.astype(BF16), preferred_element_type=F32)
            dke = jnp.dot(v.astype(BF16), dstb, preferred_element_type=F32)
            dv = _dot_nt(ke.astype(BF16), dstb)
            dbl = jnp.sum(dst_n * st_p, axis=0, keepdims=True) * ebl_last + jnp.sum(dke * ke, axis=0, keepdims=True)
            dq = dqe * eb
            dk = dke * ebl
            db = dqe * qe - dke * ke
            dec = _pair_decay(bb, rev)
            q_rep = _rows_rep(q)
            k_til = _tile_rows(k)
            do_rep = _rows_rep(dout)
            pairs = dec * q_rep * k_til
            a = jnp.dot(pairs.astype(BF16), seg_ref[...], preferred_element_type=F32)
            wb = jnp.dot((_tile_rows(v) * do_rep).astype(BF16), seg_ref[...], preferred_element_type=F32)
            gdec = wb * dec
            dq = dq + jnp.dot(sel_ref[...], (gdec * k_til).astype(BF16), preferred_element_type=F32)
            dk = dk + jnp.dot(selt_ref[...], (gdec * q_rep).astype(BF16), preferred_element_type=F32)
            dv = dv + jnp.dot(selt_ref[...], (a * do_rep).astype(BF16), preferred_element_type=F32)
            db = db + jnp.dot(seld_ref[...], (wb * pairs).astype(BF16), preferred_element_type=F32)
            db = db + jnp.where(row == last, dbl, 0.0)
            dq_ref[rows, :] = dq + dqp_ref[rows, :]
            dv_ref[rows, :] = dv + dvp_ref[rows, :]
            dk_scr[rows, :] = dk
            db_scr[rows, :] = db
            dstate[...] = dst_n * ebl_last + bd_ref[...] * _dot_tn(dob, qe.astype(BF16))
            return carry

        lax.fori_loop(0, n_ch, chunk, 0)
        hi, lo = _split2(db_scr[...])
        dg = (jnp.dot(cst_ref[...], hi, preferred_element_type=F32)
              + jnp.dot(cst_ref[...], lo, preferred_element_type=F32))
        dgf = jnp.where(f > F_MIN, dg / f, 0.0)
        dk = dk_scr[...]
        dz_ref[...] = dgf * (1.0 - lbv) * sig * (1.0 - sig) - dk * (1.0 - lbv) * sn * (1.0 - sn)
        dlb_ref[...] += _rowgroups(dgf * (1.0 - sig) - dk * sn)

    sq = (D_HGRN, D_HGRN)
    blk0 = pl.BlockSpec((HBLK, D_HGRN), lambda bi, j: (bi * nb + blk(j), 0))
    pairs_shape = (CHUNK, CHUNK * CHUNK)
    return pl.pallas_call(
        body, name=name, grid=(b, nb),
        in_specs=[col(COL_HQ), col(COL_FB if rev else COL_FF), col(COL_HI), _full((1, D_HGRN)),
                  pl.BlockSpec((None,) + sq, lambda bi, j: (bi * nb + blk(j), 0, 0)), blk0, blk0, blk0,
                  _full((HBLK, HBLK)), _full((HBLK, HBLK)), _full(sq), _full(sq),
                  _full(pairs_shape), _full(pairs_shape), _full(pairs_shape)],
        out_specs=[blk0, blk0, blk0, _full((8, D_HGRN))],
        out_shape=[_sds((t, D_HGRN), F32)] * 3 + [_sds((8, D_HGRN), F32)],
        scratch_shapes=[pltpu.VMEM(sq, F32), pltpu.VMEM((n_ch,) + sq, F32)] + [pltpu.VMEM((HBLK, D_HGRN), F32)] * 4,
        compiler_params=_params(("arbitrary", "arbitrary")),
    )(proj, proj, proj, lb, st_blk, do, dq_prev, dv_prev,
      hc["cs"], hc["cs_t"], hc["seg"], hc["bd"], hc["sel"], hc["selt"], hc["seld"])


def _scan_chunk_fwd(c, rev, q_ref, v_ref, k_scr, b_scr, state, o_ref, seg_ref, bd_ref, sel_ref):
    last = 0 if rev else CHUNK - 1
    rows = pl.ds(pl.multiple_of(c * CHUNK, CHUNK), CHUNK)
    q = q_ref[rows, :]
    k = k_scr[rows, :]
    v = v_ref[rows, :]
    bb = b_scr[rows, :]
    bl = bb[last:last + 1, :]
    pairs = _pair_decay(bb, rev) * _rows_rep(q) * _tile_rows(k)
    a = jnp.dot(pairs.astype(BF16), seg_ref[...], preferred_element_type=F32)
    o_intra = jnp.dot(sel_ref[...], (a * _tile_rows(v)).astype(BF16), preferred_element_type=F32)
    st = state[...]
    o_inter = _dot_nt((q * jnp.exp(bb)).astype(BF16), st.astype(BF16))
    o_ref[rows, :] = o_intra + o_inter
    ke = k * jnp.exp(bl - bb)
    state[...] = st * jnp.exp(bl) + bd_ref[...] * _dot_tn(v.astype(BF16), ke.astype(BF16))


def _scan_chunks_fwd(chains, seg_ref, bd_ref, sel_ref):
    work = []
    for c, rev, q_ref, v_ref, k_scr, b_scr, state, o_ref in chains:
        last = 0 if rev else CHUNK - 1
        rows = pl.ds(pl.multiple_of(c * CHUNK, CHUNK), CHUNK)
        q = q_ref[rows, :]
        k = k_scr[rows, :]
        v = v_ref[rows, :]
        bb = b_scr[rows, :]
        bl = bb[last:last + 1, :]
        st = state[...]
        work.append(dict(
            rows=rows, v=v, st=st, state=state, o_ref=o_ref, decay=jnp.exp(bl),
            pairs=(_pair_decay(bb, rev) * _rows_rep(q) * _tile_rows(k)).astype(BF16),
            qe=(q * jnp.exp(bb)).astype(BF16), ke=(k * jnp.exp(bl - bb)).astype(BF16), st_b=st.astype(BF16)))
    for w in work:
        w["a"] = jnp.dot(w["pairs"], seg_ref[...], preferred_element_type=F32)
        w["o_inter"] = _dot_nt(w["qe"], w["st_b"])
        w["upd"] = _dot_tn(w["v"].astype(BF16), w["ke"])
    for w in work:
        w["av"] = (w["a"] * _tile_rows(w["v"])).astype(BF16)
    for w in work:
        w["o_ref"][w["rows"], :] = jnp.dot(sel_ref[...], w["av"], preferred_element_type=F32) + w["o_inter"]
        w["state"][...] = w["st"] * w["decay"] + bd_ref[...] * w["upd"]


def _hgrn_fwd2(name, proj, lb_f, lb_b, b, s, hc_f, hc_b):
    t = proj.shape[0]
    nb = s // HBLK
    n_ch = HBLK // CHUNK
    n_chain = 2 * b

    def body(qf_ref, zf_ref, vf_ref, qb_ref, zb_ref, vb_ref, lbf_ref, lbb_ref, csf_ref, csb_ref, seg_ref, bd_ref,
             sel_ref, of_ref, ob_ref, stf_ref, stb_ref, *scr):
        state, b_scr, k_scr = scr[:n_chain], scr[n_chain:2 * n_chain], scr[2 * n_chain:]

        @pl.when(pl.program_id(0) == 0)
        def _():
            for st0 in state:
                st0[...] = jnp.zeros_like(st0)

        chains = []
        for bi in range(b):
            chains.append((False, qf_ref.at[bi], zf_ref.at[bi], vf_ref.at[bi], lbf_ref, csf_ref, of_ref.at[bi],
                           stf_ref.at[bi], 2 * bi))
            chains.append((True, qb_ref.at[bi], zb_ref.at[bi], vb_ref.at[bi], lbb_ref, csb_ref, ob_ref.at[bi],
                           stb_ref.at[bi], 2 * bi + 1))
        for rev, q, z, v, lb, cs, o, st, ci in chains:
            st[...] = state[ci][...]
            _, _, g, _, kk = _gates(z[...], lb[...])
            k_scr[ci][...] = kk
            b_scr[ci][...] = _ldot3(cs[...], g)

        def chunk(i, carry):
            _scan_chunks_fwd([((n_ch - 1 - i) if rev else i, rev, q, v, k_scr[ci], b_scr[ci], state[ci], o)
                              for rev, q, z, v, lb, cs, o, st, ci in chains], seg_ref, bd_ref, sel_ref)
            return carry

        lax.fori_loop(0, n_ch, chunk, 0)

    def col(c, rev):
        return pl.BlockSpec((b, HBLK, D_HGRN), lambda j: (0, (nb - 1 - j) if rev else j, c))

    def st_spec(rev):
        return pl.BlockSpec((b, None, D_HGRN, D_HGRN), lambda j: (0, (nb - 1 - j) if rev else j, 0, 0))

    sq = (D_HGRN, D_HGRN)
    proj3 = proj.reshape(b, s, proj.shape[1])
    o_fw, o_bw, st_fw, st_bw = pl.pallas_call(
        body, name=name, grid=(nb,),
        in_specs=[col(COL_HQ, False), col(COL_FF, False), col(COL_HI, False),
                  col(COL_HQ, True), col(COL_FB, True), col(COL_HI, True),
                  _full((1, D_HGRN)), _full((1, D_HGRN)), _full((HBLK, HBLK)), _full((HBLK, HBLK)),
                  _full(sq), _full(sq), _full((CHUNK, CHUNK * CHUNK))],
        out_specs=[col(0, False), col(0, True), st_spec(False), st_spec(True)],
        out_shape=[_sds((b, s, D_HGRN), F32)] * 2 + [_sds((b, nb) + sq, F32)] * 2,
        scratch_shapes=[pltpu.VMEM(sq, F32)] * n_chain + [pltpu.VMEM((HBLK, D_HGRN), F32)] * (2 * n_chain),
        compiler_params=_params(("arbitrary",)),
    )(proj3, proj3, proj3, proj3, proj3, proj3, lb_f, lb_b, hc_f["cs"], hc_b["cs"], hc_f["seg"], hc_f["bd"],
      hc_f["sel"])
    return o_fw.reshape(t, D_HGRN), o_bw.reshape(t, D_HGRN), st_fw, st_bw


def _scan_replay(c, rev, st, v_ref, k_scr, b_scr, states, bd_ref):
    last = 0 if rev else CHUNK - 1
    rows = pl.ds(pl.multiple_of(c * CHUNK, CHUNK), CHUNK)
    states[c] = st
    bb = b_scr[rows, :]
    bl = bb[last:last + 1, :]
    ke = k_scr[rows, :] * jnp.exp(bl - bb)
    return st * jnp.exp(bl) + bd_ref[...] * _dot_tn(v_ref[rows, :].astype(BF16), ke.astype(BF16))


def _scan_replays(chains, bd_ref):
    work = []
    for c, rev, st, v_ref, k_scr, b_scr, states in chains:
        last = 0 if rev else CHUNK - 1
        rows = pl.ds(pl.multiple_of(c * CHUNK, CHUNK), CHUNK)
        states[c] = st
        bb = b_scr[rows, :]
        bl = bb[last:last + 1, :]
        work.append((st, jnp.exp(bl), v_ref[rows, :].astype(BF16), (k_scr[rows, :] * jnp.exp(bl - bb)).astype(BF16)))
    upds = [_dot_tn(v, ke) for _, _, v, ke in work]
    return tuple(st * decay + bd_ref[...] * upd for (st, decay, _, _), upd in zip(work, upds))


def _scan_chunks_bwd(chains, seg_ref, bd_ref, sel_ref, selt_ref, seld_ref):
    row = lax.broadcasted_iota(jnp.int32, (CHUNK, D_HGRN), 0)
    work = []
    for c, rev, q_ref, v_ref, do_ref, k_scr, b_scr, states, dstate, dq_ref, dv_ref, dk_scr, db_scr in chains:
        last = 0 if rev else CHUNK - 1
        rows = pl.ds(pl.multiple_of(c * CHUNK, CHUNK), CHUNK)
        q = q_ref[rows, :]
        k = k_scr[rows, :]
        v = v_ref[rows, :]
        bb = b_scr[rows, :]
        dout = do_ref[rows, :]
        bl = bb[last:last + 1, :]
        st_p = states[c]
        dst_n = dstate[...]
        eb = jnp.exp(bb)
        ebl = jnp.exp(bl - bb)
        qe = q * eb
        ke = k * ebl
        dec = _pair_decay(bb, rev)
        q_rep = _rows_rep(q)
        k_til = _tile_rows(k)
        do_rep = _rows_rep(dout)
        pairs = dec * q_rep * k_til
        work.append(dict(
            rows=rows, last=last, eb=eb, ebl=ebl, ebl_last=jnp.exp(bl), qe=qe, ke=ke, dec=dec, q_rep=q_rep, k_til=k_til,
            do_rep=do_rep, pairs=pairs, st_p=st_p, dst_n=dst_n, dstate=dstate, dq_ref=dq_ref, dv_ref=dv_ref,
            dk_scr=dk_scr, db_scr=db_scr, dob=dout.astype(BF16), dstb=dst_n.astype(BF16), vb=v.astype(BF16),
            pairs_b=pairs.astype(BF16), vdo_b=(_tile_rows(v) * do_rep).astype(BF16)))
    for w in work:
        w["dqe"] = jnp.dot(w["dob"], w["st_p"].astype(BF16), preferred_element_type=F32)
        w["dke"] = jnp.dot(w["vb"], w["dstb"], preferred_element_type=F32)
        w["dv"] = _dot_nt(w["ke"].astype(BF16), w["dstb"])
        w["a"] = jnp.dot(w["pairs_b"], seg_ref[...], preferred_element_type=F32)
        w["wb"] = jnp.dot(w["vdo_b"], seg_ref[...], preferred_element_type=F32)
        w["dst_upd"] = _dot_tn(w["dob"], w["qe"].astype(BF16))
    for w in work:
        gdec = w["wb"] * w["dec"]
        w["x_dq"] = (gdec * w["k_til"]).astype(BF16)
        w["x_dk"] = (gdec * w["q_rep"]).astype(BF16)
        w["x_dv"] = (w["a"] * w["do_rep"]).astype(BF16)
        w["x_db"] = (w["wb"] * w["pairs"]).astype(BF16)
    for w in work:
        dke, dqe = w["dke"], w["dqe"]
        dbl = (jnp.sum(w["dst_n"] * w["st_p"], axis=0, keepdims=True) * w["ebl_last"]
               + jnp.sum(dke * w["ke"], axis=0, keepdims=True))
        dq = dqe * w["eb"] + jnp.dot(sel_ref[...], w["x_dq"], preferred_element_type=F32)
        dk = dke * w["ebl"] + jnp.dot(selt_ref[...], w["x_dk"], preferred_element_type=F32)
        dv = w["dv"] + jnp.dot(selt_ref[...], w["x_dv"], preferred_element_type=F32)
        db = (dqe * w["qe"] - dke * w["ke"] + jnp.dot(seld_ref[...], w["x_db"], preferred_element_type=F32)
              + jnp.where(row == w["last"], dbl, 0.0))
        w["dq_ref"][w["rows"], :] = dq
        w["dv_ref"][w["rows"], :] = dv
        w["dk_scr"][w["rows"], :] = dk
        w["db_scr"][w["rows"], :] = db
        w["dstate"][...] = w["dst_n"] * w["ebl_last"] + bd_ref[...] * w["dst_upd"]


def _scan_chunk_bwd(c, rev, q_ref, v_ref, do_ref, k_scr, b_scr, states, dstate, dq_ref, dv_ref, dk_scr, db_scr,
                    seg_ref, bd_ref, sel_ref, selt_ref, seld_ref):
    last = 0 if rev else CHUNK - 1
    row = lax.broadcasted_iota(jnp.int32, (CHUNK, D_HGRN), 0)
    rows = pl.ds(pl.multiple_of(c * CHUNK, CHUNK), CHUNK)
    q = q_ref[rows, :]
    k = k_scr[rows, :]
    v = v_ref[rows, :]
    bb = b_scr[rows, :]
    dout = do_ref[rows, :]
    bl = bb[last:last + 1, :]
    st_p = states[c]
    dst_n = dstate[...]
    eb = jnp.exp(bb)
    ebl = jnp.exp(bl - bb)
    ebl_last = jnp.exp(bl)
    qe = q * eb
    ke = k * ebl
    dob = dout.astype(BF16)
    dstb = dst_n.astype(BF16)
    dqe = jnp.dot(dob, st_p.astype(BF16), preferred_element_type=F32)
    dke = jnp.dot(v.astype(BF16), dstb, preferred_element_type=F32)
    dv = _dot_nt(ke.astype(BF16), dstb)
    dbl = jnp.sum(dst_n * st_p, axis=0, keepdims=True) * ebl_last + jnp.sum(dke * ke, axis=0, keepdims=True)
    dq = dqe * eb
    dk = dke * ebl
    db = dqe * qe - dke * ke
    dec = _pair_decay(bb, rev)
    q_rep = _rows_rep(q)
    k_til = _tile_rows(k)
    do_rep = _rows_rep(dout)
    pairs = dec * q_rep * k_til
    a = jnp.dot(pairs.astype(BF16), seg_ref[...], preferred_element_type=F32)
    wb = jnp.dot((_tile_rows(v) * do_rep).astype(BF16), seg_ref[...], preferred_element_type=F32)
    gdec = wb * dec
    dq = dq + jnp.dot(sel_ref[...], (gdec * k_til).astype(BF16), preferred_element_type=F32)
    dk = dk + jnp.dot(selt_ref[...], (gdec * q_rep).astype(BF16), preferred_element_type=F32)
    dv = dv + jnp.dot(selt_ref[...], (a * do_rep).astype(BF16), preferred_element_type=F32)
    db = db + jnp.dot(seld_ref[...], (wb * pairs).astype(BF16), preferred_element_type=F32)
    db = db + jnp.where(row == last, dbl, 0.0)
    dq_ref[rows, :] = dq
    dv_ref[rows, :] = dv
    dk_scr[rows, :] = dk
    db_scr[rows, :] = db
    dstate[...] = dst_n * ebl_last + bd_ref[...] * _dot_tn(dob, qe.astype(BF16))


def _hgrn_bwd2(name, proj, lb_f, lb_b, st_f, st_b, do, b, s, hc_f, hc_b):
    t = proj.shape[0]
    nb = s // HBLK
    n_ch = HBLK // CHUNK

    n_chain = 2 * b

    def body(qf_ref, zf_ref, vf_ref, dof_ref, stf_ref, qb_ref, zb_ref, vb_ref, dob_ref, stb_ref, lbf_ref, lbb_ref,
             csf_ref, csb_ref, cstf_ref, cstb_ref, seg_ref, bd_ref, sel_ref, selt_ref, seld_ref,
             dqf_ref, dvf_ref, dzf_ref, dqb_ref, dvb_ref, dzb_ref, dlbf_ref, dlbb_ref,
             *scr):
        dstate, states, b_scr, k_scr, db_scr, dk_scr = [scr[i * n_chain:(i + 1) * n_chain] for i in range(6)]

        @pl.when(pl.program_id(0) == 0)
        def _():
            dlbf_ref[...] = jnp.zeros_like(dlbf_ref)
            dlbb_ref[...] = jnp.zeros_like(dlbb_ref)
            for d0 in dstate:
                d0[...] = jnp.zeros_like(d0)

        chains = []
        for bi in range(b):
            chains.append(dict(rev=False, q=qf_ref.at[bi], z=zf_ref.at[bi], v=vf_ref.at[bi], do=dof_ref.at[bi],
                               st=stf_ref.at[bi], lb=lbf_ref, cs=csf_ref, cst=cstf_ref, dq=dqf_ref.at[bi],
                               dv=dvf_ref.at[bi], dz=dzf_ref.at[bi], dlb=dlbf_ref, ci=2 * bi))
            chains.append(dict(rev=True, q=qb_ref.at[bi], z=zb_ref.at[bi], v=vb_ref.at[bi], do=dob_ref.at[bi],
                               st=stb_ref.at[bi], lb=lbb_ref, cs=csb_ref, cst=cstb_ref, dq=dqb_ref.at[bi],
                               dv=dvb_ref.at[bi], dz=dzb_ref.at[bi], dlb=dlbb_ref, ci=2 * bi + 1))
        for ch in chains:
            sig, f, g, sn, kk = _gates(ch["z"][...], ch["lb"][...])
            k_scr[ch["ci"]][...] = kk
            b_scr[ch["ci"]][...] = _ldot3(ch["cs"][...], g)
            ch["gates"] = (sig, f, sn)

        def replay(i, carry):
            return _scan_replays([((n_ch - 1 - i) if ch["rev"] else i, ch["rev"], st, ch["v"], k_scr[ch["ci"]],
                                   b_scr[ch["ci"]], states[ch["ci"]]) for ch, st in zip(chains, carry)], bd_ref)

        lax.fori_loop(0, n_ch, replay, tuple(ch["st"][...] for ch in chains))

        def chunk(i, carry):
            args = [(i if ch["rev"] else (n_ch - 1 - i), ch["rev"], ch["q"], ch["v"], ch["do"],
                     k_scr[ch["ci"]], b_scr[ch["ci"]], states[ch["ci"]], dstate[ch["ci"]], ch["dq"],
                     ch["dv"], dk_scr[ch["ci"]], db_scr[ch["ci"]]) for ch in chains]
            for g0 in range(0, n_chain, BWD_GROUP):
                _scan_chunks_bwd(args[g0:g0 + BWD_GROUP], seg_ref, bd_ref, sel_ref, selt_ref, seld_ref)
            return carry

        lax.fori_loop(0, n_ch, chunk, 0)
        for ch in chains:
            sig, f, sn = ch["gates"]
            lbv = ch["lb"][...]
            hi, lo = _split2(db_scr[ch["ci"]][...])
            dg = (jnp.dot(ch["cst"][...], hi, preferred_element_type=F32)
                  + jnp.dot(ch["cst"][...], lo, preferred_element_type=F32))
            dgf = jnp.where(f > F_MIN, dg / f, 0.0)
            dk = dk_scr[ch["ci"]][...]
            ch["dz"][...] = dgf * (1.0 - lbv) * sig * (1.0 - sig) - dk * (1.0 - lbv) * sn * (1.0 - sn)
            ch["dlb"][...] += _rowgroups(dgf * (1.0 - sig) - dk * sn)

    def col(c, rev):
        return pl.BlockSpec((b, HBLK, D_HGRN), lambda j: (0, j if rev else (nb - 1 - j), c))

    def st_spec(rev):
        return pl.BlockSpec((b, None, D_HGRN, D_HGRN), lambda j: (0, j if rev else (nb - 1 - j), 0, 0))

    sq = (D_HGRN, D_HGRN)
    blk = (HBLK, D_HGRN)
    pairs_shape = (CHUNK, CHUNK * CHUNK)
    proj3 = proj.reshape(b, s, proj.shape[1])
    do3 = do.reshape(b, s, D_HGRN)
    res = pl.pallas_call(
        body, name=name, grid=(nb,),
        in_specs=[col(COL_HQ, False), col(COL_FF, False), col(COL_HI, False), col(0, False), st_spec(False),
                  col(COL_HQ, True), col(COL_FB, True), col(COL_HI, True), col(0, True), st_spec(True),
                  _full((1, D_HGRN)), _full((1, D_HGRN)), _full((HBLK, HBLK)), _full((HBLK, HBLK)),
                  _full((HBLK, HBLK)), _full((HBLK, HBLK)), _full(sq), _full(sq),
                  _full(pairs_shape), _full(pairs_shape), _full(pairs_shape)],
        out_specs=[col(0, False)] * 3 + [col(0, True)] * 3 + [_full((8, D_HGRN))] * 2,
        out_shape=[_sds((b, s, D_HGRN), F32)] * 6 + [_sds((8, D_HGRN), F32)] * 2,
        scratch_shapes=[pltpu.VMEM(sq, F32)] * n_chain + [pltpu.VMEM((n_ch,) + sq, F32)] * n_chain
        + [pltpu.VMEM(blk, F32)] * (4 * n_chain),
        compiler_params=_params(("arbitrary",)),
    )(proj3, proj3, proj3, do3, st_f, proj3, proj3, proj3, do3, st_b, lb_f, lb_b, hc_f["cs"], hc_b["cs"],
      hc_f["cs_t"], hc_b["cs_t"], hc_f["seg"], hc_f["bd"], hc_f["sel"], hc_f["selt"], hc_f["seld"])
    return [r.reshape(t, D_HGRN) for r in res[:6]] + list(res[6:])


def _lower_bounds(logits):
    n = logits.shape[1]

    def body(x_ref, o_ref):
        x = x_ref[...]
        for d in range(2):
            rows = [x[l * 2 + d:l * 2 + d + 1, :] for l in range(DEPTH)]
            mx = functools.reduce(jnp.maximum, rows)
            ex = [jnp.exp(r - mx) for r in rows]
            tot = functools.reduce(lambda a, c: a + c, ex)
            sm = [e / tot for e in ex]
            run = jnp.zeros_like(sm[0])
            for l in range(DEPTH):
                run = run + sm[l]
                o_ref[l * 2 + d:l * 2 + d + 1, :] = run - sm[0]

    return pl.pallas_call(body, name="hgrn_lower_bounds", out_shape=_sds(logits.shape, F32),
                          in_specs=[_full(logits.shape)], out_specs=_full(logits.shape), grid=(1,),
                          compiler_params=_params(("arbitrary",)))(logits)


def _lower_bounds_bwd(logits, dlb):
    def body(x_ref, g_ref, o_ref):
        x = x_ref[...]
        gv = g_ref[...]
        for d in range(2):
            rows = [x[l * 2 + d:l * 2 + d + 1, :] for l in range(DEPTH)]
            gr = [gv[l * 2 + d:l * 2 + d + 1, :] for l in range(DEPTH)]
            mx = functools.reduce(jnp.maximum, rows)
            ex = [jnp.exp(r - mx) for r in rows]
            tot = functools.reduce(lambda a, c: a + c, ex)
            sm = [e / tot for e in ex]
            dsm = []
            for i in range(DEPTH):
                acc = functools.reduce(lambda a, c: a + c, gr[i:])
                if i == 0:
                    acc = acc - functools.reduce(lambda a, c: a + c, gr)
                dsm.append(acc)
            inner = functools.reduce(lambda a, c: a + c, [sm[i] * dsm[i] for i in range(DEPTH)])
            for i in range(DEPTH):
                o_ref[i * 2 + d:i * 2 + d + 1, :] = sm[i] * (dsm[i] - inner)

    return pl.pallas_call(body, name="hgrn_lower_bounds_bwd", out_shape=_sds(logits.shape, F32),
                          in_specs=[_full(logits.shape), _full(logits.shape)], out_specs=_full(logits.shape),
                          grid=(1,), compiler_params=_params(("arbitrary",)))(logits, dlb)


def _conv_rows(s):
    return s + 2 * (CONV_PAD + 1)


def _conv_fwd(name, proj, dw_w, dw_b, ln_w, ln_b, pw_w, pw_b, b, s):
    t = proj.shape[0]
    pad = CONV_PAD + 1
    nt = s // CONV_TILE

    def body(a_ref, g_ref, w_ref, dwb_ref, lnw_ref, lnb_ref, pw_ref, pwb_ref, y_ref, c_ref, upad, win):
        upad[0:pad, :] = jnp.zeros((pad, D_CONV), F32)
        upad[s + pad:s + 2 * pad, :] = jnp.zeros((pad, D_CONV), F32)

        def fill(i, carry):
            rows = pl.ds(pl.multiple_of(i * CONV_TILE, CONV_TILE), CONV_TILE)
            upad[pl.ds(pl.multiple_of(i * CONV_TILE + pad, pad), CONV_TILE), :] = a_ref[rows, :] * _sigmoid(g_ref[rows, :])
            return carry

        lax.fori_loop(0, nt, fill, 0)

        def tile(i, carry):
            r0 = pl.multiple_of(i * CONV_TILE, CONV_TILE)
            win[...] = upad[pl.ds(r0, CONV_TILE + 2 * pad), :]
            acc = jnp.zeros((CONV_TILE, D_CONV), F32)
            for j in range(CONV_W):
                acc = acc + win[j + 1:j + 1 + CONV_TILE, :] * w_ref[j:j + 1, :]
            c = acc + dwb_ref[...]
            c_ref[pl.ds(r0, CONV_TILE), :] = c
            mu = jnp.mean(c, axis=-1, keepdims=True)
            xc = c - mu
            rstd = lax.rsqrt(jnp.mean(xc * xc, axis=-1, keepdims=True) + LN_EPS)
            n = xc * rstd * lnw_ref[...] + lnb_ref[...]
            y_ref[pl.ds(r0, CONV_TILE), :] = (jnp.dot(_silu(n).astype(BF16), pw_ref[...].astype(BF16),
                                                      preferred_element_type=F32) + pwb_ref[...])
            return carry

        lax.fori_loop(0, nt, tile, 0)

    vec = _full((1, D_CONV))
    return pl.pallas_call(
        body, name=name, grid=(b,),
        in_specs=[pl.BlockSpec((s, D_CONV), lambda bi: (bi, COL_CA)), pl.BlockSpec((s, D_CONV), lambda bi: (bi, COL_CB)),
                  _full((CONV_W + 1, D_CONV)), vec, vec, vec, _full((D_CONV, D_CONV)), vec],
        out_specs=[pl.BlockSpec((s, D_CONV), lambda bi: (bi, 0))] * 2, out_shape=[_sds((t, D_CONV), F32)] * 2,
        scratch_shapes=[pltpu.VMEM((_conv_rows(s), D_CONV), F32), pltpu.VMEM((CONV_TILE + 2 * pad, D_CONV), F32)],
        compiler_params=_params(("parallel",)),
    )(proj, proj, dw_w, dw_b, ln_w, ln_b, pw_w, pw_b)


def _conv_bwd(name, proj, conv_out, dw_w, ln_w, ln_b, pw_w, dy, b, s):
    t = proj.shape[0]
    pad = CONV_PAD + 1
    nt = s // CONV_TILE

    def body(a_ref, g_ref, c_ref, w_ref, lnw_ref, lnb_ref, pw_ref, dy_ref, dab_ref, dpw_ref, ddw_ref, dvec_ref,
             upad, dcpad, tap_acc, win, dwin):
        @pl.when(pl.program_id(0) == 0)
        def _():
            dpw_ref[...] = jnp.zeros_like(dpw_ref)
            ddw_ref[...] = jnp.zeros_like(ddw_ref)
            dvec_ref[...] = jnp.zeros_like(dvec_ref)

        zeros = jnp.zeros((pad, D_CONV), F32)
        upad[0:pad, :] = zeros
        upad[s + pad:s + 2 * pad, :] = zeros
        dcpad[0:pad, :] = zeros
        dcpad[s + pad:s + 2 * pad, :] = zeros
        tap_acc[...] = jnp.zeros_like(tap_acc)

        def inner(i):
            return pl.ds(pl.multiple_of(i * CONV_TILE + pad, pad), CONV_TILE)

        def fill(i, carry):
            rows = pl.ds(pl.multiple_of(i * CONV_TILE, CONV_TILE), CONV_TILE)
            upad[inner(i), :] = a_ref[rows, :] * _sigmoid(g_ref[rows, :])
            return carry

        lax.fori_loop(0, nt, fill, 0)

        def tile_a(i, carry):
            r0 = pl.multiple_of(i * CONV_TILE, CONV_TILE)
            c = c_ref[pl.ds(r0, CONV_TILE), :]
            mu = jnp.mean(c, axis=-1, keepdims=True)
            xc = c - mu
            rstd = lax.rsqrt(jnp.mean(xc * xc, axis=-1, keepdims=True) + LN_EPS)
            xhat = xc * rstd
            n = xhat * lnw_ref[...] + lnb_ref[...]
            dyt = dy_ref[pl.ds(r0, CONV_TILE), :]
            dyb = dyt.astype(BF16)
            dpw_ref[...] += _dot_tn(_silu(n).astype(BF16), dyb)
            dn = _dot_nt(dyb, pw_ref[...].astype(BF16)) * _dsilu(n)
            dxh = dn * lnw_ref[...]
            dc = rstd * (dxh - jnp.mean(dxh, axis=-1, keepdims=True)
                         - xhat * jnp.mean(dxh * xhat, axis=-1, keepdims=True))
            dcpad[inner(i), :] = dc
            dvec_ref[0:1, :] += jnp.sum(dyt, axis=0, keepdims=True)
            dvec_ref[1:2, :] += jnp.sum(dn * xhat, axis=0, keepdims=True)
            dvec_ref[2:3, :] += jnp.sum(dn, axis=0, keepdims=True)
            dvec_ref[3:4, :] += jnp.sum(dc, axis=0, keepdims=True)
            return carry

        lax.fori_loop(0, nt, tile_a, 0)

        def tile_b(i, carry):
            r0 = pl.multiple_of(i * CONV_TILE, CONV_TILE)
            win[...] = upad[pl.ds(r0, CONV_TILE + 2 * pad), :]
            dwin[...] = dcpad[pl.ds(r0, CONV_TILE + 2 * pad), :]
            dct = dwin[pad:pad + CONV_TILE, :]
            du = jnp.zeros((CONV_TILE, D_CONV), F32)
            for j in range(CONV_W):
                du = du + dwin[2 * pad - 1 - j:2 * pad - 1 - j + CONV_TILE, :] * w_ref[j:j + 1, :]
                tap_acc[8 * j:8 * j + 8, :] += _rowgroups(dct * win[j + 1:j + 1 + CONV_TILE, :])
            rows = pl.ds(r0, CONV_TILE)
            sg = _sigmoid(g_ref[rows, :])
            dab_ref[rows, 0:D_CONV] = (du * sg).astype(BF16)
            dab_ref[rows, D_CONV:2 * D_CONV] = (du * a_ref[rows, :] * sg * (1.0 - sg)).astype(BF16)
            return carry

        lax.fori_loop(0, nt, tile_b, 0)
        for j in range(CONV_W):
            ddw_ref[j:j + 1, :] += jnp.sum(tap_acc[8 * j:8 * j + 8, :], axis=0, keepdims=True)

    vec = _full((1, D_CONV))
    return pl.pallas_call(
        body, name=name, grid=(b,),
        in_specs=[pl.BlockSpec((s, D_CONV), lambda bi: (bi, COL_CA)), pl.BlockSpec((s, D_CONV), lambda bi: (bi, COL_CB)),
                  pl.BlockSpec((s, D_CONV), lambda bi: (bi, 0)),
                  _full((CONV_W + 1, D_CONV)), vec, vec, _full((D_CONV, D_CONV)),
                  pl.BlockSpec((s, D_CONV), lambda bi: (bi, 0))],
        out_specs=[pl.BlockSpec((s, 2 * D_CONV), lambda bi: (bi, 0)), _full((D_CONV, D_CONV)),
                   _full((CONV_W + 1, D_CONV)), _full((8, D_CONV))],
        out_shape=[_sds((t, 2 * D_CONV), BF16), _sds((D_CONV, D_CONV), F32), _sds((CONV_W + 1, D_CONV), F32),
                   _sds((8, D_CONV), F32)],
        scratch_shapes=[pltpu.VMEM((_conv_rows(s), D_CONV), F32), pltpu.VMEM((_conv_rows(s), D_CONV), F32),
                        pltpu.VMEM((8 * CONV_W, D_CONV), F32), pltpu.VMEM((CONV_TILE + 2 * pad, D_CONV), F32),
                        pltpu.VMEM((CONV_TILE + 2 * pad, D_CONV), F32)],
        compiler_params=_params(("arbitrary",)),
    )(proj, proj, conv_out, dw_w, ln_w, ln_b, pw_w, dy)


def _mix_fwd(name, y_attn, o_fw, o_bw, proj, y_conv, aw, gw, cw, seg):
    t = y_attn.shape[0]
    tm = _row_tile(t)

    def body(ya_ref, of_ref, ob_ref, hg_ref, yc_ref, aw_ref, gw_ref, cw_ref, seg_ref, o_ref):
        ya = ya_ref[...]
        ra = lax.rsqrt(jnp.mean(ya * ya, axis=-1, keepdims=True) + EPS)
        o_ref[:, 0:D_ATTN] = (ya * ra * aw_ref[...]).astype(BF16)
        o = of_ref[...] + ob_ref[...]
        ro = lax.rsqrt(jnp.dot((o * o).astype(BF16), seg_ref[...], preferred_element_type=F32) + EPS)
        o_ref[:, D_ATTN:D_ATTN + D_HGRN] = (o * ro * gw_ref[...] * _silu(hg_ref[...])).astype(BF16)
        yc = yc_ref[...]
        rc = lax.rsqrt(jnp.mean(yc * yc, axis=-1, keepdims=True) + EPS)
        o_ref[:, D_ATTN + D_HGRN:D_MODEL] = (yc * rc * cw_ref[...]).astype(BF16)

    def tile(w, c=0):
        return pl.BlockSpec((tm, w), lambda i: (i, c))

    return pl.pallas_call(
        body, name=name, grid=(t // tm,),
        in_specs=[tile(D_ATTN), tile(D_HGRN), tile(D_HGRN), tile(D_HGRN, COL_HG), tile(D_CONV),
                  _full((1, D_ATTN)), _full((1, D_HGRN)), _full((1, D_CONV)), _full((D_HGRN, D_HGRN))],
        out_specs=tile(D_MODEL), out_shape=_sds((t, D_MODEL), BF16),
        compiler_params=_params(("parallel",)),
    )(y_attn, o_fw, o_bw, proj, y_conv, aw, gw, cw, seg)


def _mix_bwd(name, dmix, y_attn, o_fw, o_bw, proj, y_conv, aw, gw, cw, seg, deps=()):
    t = y_attn.shape[0]
    tm = _row_tile(t)

    def rms_bwd(x, w, dy):
        r = lax.rsqrt(jnp.mean(x * x, axis=-1, keepdims=True) + EPS)
        gwv = dy * w
        return r * gwv - x * (r * r * r) * jnp.mean(gwv * x, axis=-1, keepdims=True), _rowgroups(dy * x * r)

    def body(dm_ref, ya_ref, of_ref, ob_ref, hg_ref, yc_ref, aw_ref, gw_ref, cw_ref, seg_ref, *rest):
        dya_ref, do_ref, dhg_ref, dyc_ref, daw_ref, dgw_ref, dcw_ref = rest[-7:]

        @pl.when(pl.program_id(0) == 0)
        def _():
            daw_ref[...] = jnp.zeros_like(daw_ref)
            dgw_ref[...] = jnp.zeros_like(dgw_ref)
            dcw_ref[...] = jnp.zeros_like(dcw_ref)

        dya, daw = rms_bwd(ya_ref[...], aw_ref[...], dm_ref[:, 0:D_ATTN])
        dya_ref[...] = dya
        daw_ref[...] += daw
        dyc, dcw = rms_bwd(yc_ref[...], cw_ref[...], dm_ref[:, D_ATTN + D_HGRN:D_MODEL])
        dyc_ref[...] = dyc
        dcw_ref[...] += dcw
        d2 = dm_ref[:, D_ATTN:D_ATTN + D_HGRN]
        o = of_ref[...] + ob_ref[...]
        hg = hg_ref[...]
        ro = lax.rsqrt(jnp.dot((o * o).astype(BF16), seg_ref[...], preferred_element_type=F32) + EPS)
        dn = d2 * _silu(hg)
        dhg_ref[...] = (d2 * o * ro * gw_ref[...] * _dsilu(hg)).astype(BF16)
        gwv = dn * gw_ref[...]
        do_ref[...] = ro * gwv - o * (ro * ro * ro) * _rdot2(gwv * o, seg_ref[...])
        dgw_ref[...] += _rowgroups(dn * o * ro)

    def tile(w, c=0):
        return pl.BlockSpec((tm, w), lambda i: (i, c))

    return pl.pallas_call(
        body, name=name, grid=(t // tm,),
        in_specs=[tile(D_MODEL), tile(D_ATTN), tile(D_HGRN), tile(D_HGRN), tile(D_HGRN, COL_HG), tile(D_CONV),
                  _full((1, D_ATTN)), _full((1, D_HGRN)), _full((1, D_CONV)), _full((D_HGRN, D_HGRN))]
        + [_full(a.shape) for a in deps],
        out_specs=[tile(D_ATTN), tile(D_HGRN), tile(D_HGRN), tile(D_CONV),
                   _full((8, D_ATTN)), _full((8, D_HGRN)), _full((8, D_CONV))],
        out_shape=[_sds((t, D_ATTN), F32), _sds((t, D_HGRN), F32), _sds((t, D_HGRN), BF16), _sds((t, D_CONV), F32),
                   _sds((8, D_ATTN), F32), _sds((8, D_HGRN), F32), _sds((8, D_CONV), F32)],
        compiler_params=_params(("arbitrary",)),
    )(dmix, y_attn, o_fw, o_bw, proj, y_conv, aw, gw, cw, seg, *deps)


def _dproj(name, dp_attn, dq_f, dq_b, dz_fw, dz_bw, dv_f, dv_b, dhg, dp_conv):
    t = dq_f.shape[0]
    tm = _row_tile(t)
    wa, wc = dp_attn.shape[1], dp_conv.shape[1]

    def body(at_ref, qf_ref, qb_ref, zf_ref, zb_ref, vf_ref, vb_ref, hg_ref, cv_ref, o_ref):
        o_ref[:, 0:wa] = at_ref[...]
        cols = (qf_ref[...] + qb_ref[...], zf_ref[...], zb_ref[...], vf_ref[...] + vb_ref[...], hg_ref[...])
        for i, val in enumerate(cols):
            o_ref[:, wa + i * D_HGRN:wa + (i + 1) * D_HGRN] = val.astype(BF16)
        o_ref[:, wa + 5 * D_HGRN:D_IN] = cv_ref[...]

    tile = lambda w: pl.BlockSpec((tm, w), lambda i: (i, 0))
    return pl.pallas_call(
        body, name=name, grid=(t // tm,), in_specs=[tile(wa)] + [tile(D_HGRN)] * 7 + [tile(wc)],
        out_specs=tile(D_IN), out_shape=_sds((t, D_IN), BF16), compiler_params=_params(("parallel",)),
    )(dp_attn, dq_f, dq_b, dz_fw, dz_bw, dv_f, dv_b, dhg, dp_conv)


def _mm_tile(t):
    return min(512, t)


def _resident(shape):
    n = len(shape)
    return pl.BlockSpec(tuple(shape), lambda *_: (0,) * n, pipeline_mode=pl.Buffered(1))


def _w_blk(rows, cols, j_of):
    return pl.BlockSpec((None, rows, cols), lambda *g: (j_of(*g), 0, 0))


def _layer_fwd(l, x, wget, sm, tabs, cst, b, s, deps, target=None):
    t = x.shape[0]
    tm = _mm_tile(t)
    nt = t // tm
    pre = "l%d_" % l
    row = lambda w: pl.BlockSpec((tm, w), lambda i, *_: (i, 0))

    def normed(x_ref, nw_ref):
        xv = x_ref[...]
        r = lax.rsqrt(jnp.mean(xv * xv, axis=-1, keepdims=True) + EPS)
        return (xv * r * nw_ref[...]).astype(BF16)

    def in_body(x_ref, nw_ref, w_ref, *rest):
        o_ref, h_ref = rest[-2:]
        hv = normed(x_ref, nw_ref)
        h_ref[...] = hv
        for j in range(N_CHIP):
            o_ref[:, j * IN_BLK:(j + 1) * IN_BLK] = jnp.dot(hv, w_ref[j], preferred_element_type=F32)

    w_in = wget(l, "w_in", x)
    proj, h1 = pl.pallas_call(
        in_body, name=pre + "in_proj", grid=(nt,),
        in_specs=[row(D_MODEL), _full((1, D_MODEL)), _resident(w_in.shape)] + [_full(a.shape) for a in deps],
        out_specs=[row(D_IN), row(D_MODEL)], out_shape=[_sds((t, D_IN), F32), _sds((t, D_MODEL), BF16)],
        compiler_params=_params(("parallel",)),
    )(x, sm["mix_norm_w"][l], w_in, *deps)
    qn, kr, vr = _attn_prep(pre + "attn_prep", proj, s, tabs, sm["q_norm_w"][l], sm["k_norm_w"][l], cst["attn"])
    y_attn = _attn_fwd(pre + "attn", qn, kr, vr, b, s)
    o_fw, o_bw, st_fw, st_bw = _hgrn_fwd2(pre + "hgrn", proj, sm["lb"][l][0], sm["lb"][l][1], b, s, cst["hg_fw"],
                                          cst["hg_bw"])
    y_conv, conv_out = _conv_fwd(pre + "conv", proj, sm["conv_dw_w"][l], sm["conv_dw_b"][l], sm["conv_ln_w"][l],
                       sm["conv_ln_b"][l], sm["conv_pw_w"][l], sm["conv_pw_b"][l], b, s)
    mixed = _mix_fwd(pre + "mix", y_attn, o_fw, o_bw, proj, y_conv, sm["attn_out_norm_w"][l], sm["gnorm_w"][l],
                     sm["conv_out_norm_w"][l], cst["seg_h"])
    (x1,) = _mm(pre + "out_proj", (nt,),
                [(mixed, row(D_MODEL), wget(l, "w_out", mixed),
                  pl.BlockSpec((N_CHIP, OUT_BLK, D_MODEL), lambda i: (0, 0, 0)), NN)],
                [(x, row(D_MODEL))], [(_sds((t, D_MODEL), F32), row(D_MODEL))],
                lambda tot, xr: (xr + tot,))
    ff3 = pl.BlockSpec((N_CHIP, tm, FF_BLK), lambda i: (0, i, 0))
    ffs = _sds((N_CHIP, t, FF_BLK), BF16)

    def gu_body(x_ref, nw_ref, wg_ref, wu_ref, g_ref, u_ref, a_ref, h_ref):
        hv = normed(x_ref, nw_ref)
        h_ref[...] = hv
        for j in range(N_CHIP):
            gv = jnp.dot(hv, wg_ref[j], preferred_element_type=F32)
            uv = jnp.dot(hv, wu_ref[j], preferred_element_type=F32)
            g_ref[j] = gv.astype(BF16)
            u_ref[j] = uv.astype(BF16)
            a_ref[j] = (_silu(gv) * uv).astype(BF16)

    w_gate, w_up = wget(l, "w_gate", x1), wget(l, "w_up", x1)
    gate, up, act, h2 = pl.pallas_call(
        gu_body, name=pre + "ffn_gate_up", grid=(nt,),
        in_specs=[row(D_MODEL), _full((1, D_MODEL)), _resident(w_gate.shape), _resident(w_up.shape)],
        out_specs=[ff3, ff3, ff3, row(D_MODEL)], out_shape=[ffs, ffs, ffs, _sds((t, D_MODEL), BF16)],
        compiler_params=_params(("parallel",)),
    )(x1, sm["ffn_norm_w"][l], w_gate, w_up)

    def down_body(a_ref, w_ref, x_ref, o_ref):
        tot = x_ref[...]
        for j in range(N_CHIP):
            tot = tot + jnp.dot(a_ref[j], w_ref[j], preferred_element_type=F32)
        o_ref[...] = tot

    def down_loss_body(a_ref, w_ref, x_ref, t_ref, dy_ref, acc_ref):
        tot = x_ref[...]
        for j in range(N_CHIP):
            tot = tot + jnp.dot(a_ref[j], w_ref[j], preferred_element_type=F32)
        e = tot - t_ref[...]
        dy_ref[...] = e * (1.0 / D_MODEL)

        @pl.when(pl.program_id(0) == 0)
        def _():
            acc_ref[...] = jnp.zeros_like(acc_ref)

        acc_ref[...] += _rowgroups(e * e)

    w_down = wget(l, "w_down", act)
    if target is None:
        x2 = pl.pallas_call(
            down_body, name=pre + "ffn_down", grid=(nt,), in_specs=[ff3, _resident(w_down.shape), row(D_MODEL)],
            out_specs=row(D_MODEL), out_shape=_sds((t, D_MODEL), F32), compiler_params=_params(("parallel",)),
        )(act, w_down, x1)
    else:
        x2 = pl.pallas_call(
            down_loss_body, name=pre + "ffn_down_loss", grid=(nt,),
            in_specs=[ff3, _resident(w_down.shape), row(D_MODEL), row(D_MODEL)],
            out_specs=[row(D_MODEL), _full((8, D_MODEL))],
            out_shape=[_sds((t, D_MODEL), F32), _sds((8, D_MODEL), F32)], compiler_params=_params(("arbitrary",)),
        )(act, w_down, x1, target)
    saved = dict(x=x, h1=h1, proj=proj, qn=qn, kr=kr, vr=vr, y_attn=y_attn, o_fw=o_fw, o_bw=o_bw, st_fw=st_fw,
                 st_bw=st_bw, y_conv=y_conv, conv_out=conv_out, mixed=mixed, x1=x1, h2=h2, gate=gate, up=up, act=act)
    return x2, saved


def _layer_bwd(l, dx2, sv, wget, sm, tabs, cst, b, s, on_grads):
    t = dx2.shape[0]
    tm = _mm_tile(t)
    nt = t // tm
    pre = "l%d_" % l
    tk = min(2048, t)
    nk = t // tk
    row = lambda w: pl.BlockSpec((tm, w), lambda i, *_: (i, 0))
    ff3 = pl.BlockSpec((N_CHIP, tm, FF_BLK), lambda i: (0, i, 0))
    ffs = _sds((N_CHIP, t, FF_BLK), BF16)

    w_down, w_gate, w_up = wget(l, "w_down", dx2), wget(l, "w_gate", dx2), wget(l, "w_up", dx2)

    def ddx_body(dx_ref, w_ref, g_ref, u_ref, dg_ref, du_ref):
        dxb = dx_ref[...].astype(BF16)
        for j in range(N_CHIP):
            da = _dot_nt(dxb, w_ref[j])
            g = g_ref[j].astype(F32)
            sg = _sigmoid(g)
            dg_ref[j] = (da * u_ref[j].astype(F32) * (sg * (1.0 + g * (1.0 - sg)))).astype(BF16)
            du_ref[j] = (da * (g * sg)).astype(BF16)

    dgate, dup = pl.pallas_call(
        ddx_body, name=pre + "ffn_down_dx", grid=(nt,), in_specs=[row(D_MODEL), _resident(w_down.shape), ff3, ff3],
        out_specs=[ff3, ff3], out_shape=[ffs, ffs], compiler_params=_params(("parallel",)),
    )(dx2, w_down, sv["gate"], sv["up"])
    colt = lambda w: pl.BlockSpec((tk, w), lambda j, k: (k, 0))
    fft = pl.BlockSpec((None, tk, FF_BLK), lambda j, k: (j, k, 0))
    (g_down,) = _mm(pre + "ffn_down_dw", (N_CHIP, nk), [(sv["act"], fft, dx2, colt(D_MODEL), TN)], [],
                    [(_sds((N_CHIP, FF_BLK, D_MODEL), BF16), pl.BlockSpec((None, FF_BLK, D_MODEL), lambda j, k: (j, 0, 0)))],
                    lambda tot: (tot,), acc=(1, (FF_BLK, D_MODEL)))
    wff = pl.BlockSpec((None, D_MODEL, FF_BLK), lambda j, k: (j, 0, 0))
    (g_gate,) = _mm(pre + "ffn_gate_dw", (N_CHIP, nk), [(sv["h2"], colt(D_MODEL), dgate, fft, TN)], [],
                    [(_sds((N_CHIP, D_MODEL, FF_BLK), BF16), wff)], lambda tot: (tot,), acc=(1, (D_MODEL, FF_BLK)))
    (g_up,) = _mm(pre + "ffn_up_dw", (N_CHIP, nk), [(sv["h2"], colt(D_MODEL), dup, fft, TN)], [],
                  [(_sds((N_CHIP, D_MODEL, FF_BLK), BF16), wff)], lambda tot: (tot,), acc=(1, (D_MODEL, FF_BLK)))

    def norm_bwd_tail(dh, x_ref, nw_ref, dres_ref, dx_ref, dw_ref):
        xv = x_ref[...]
        r = lax.rsqrt(jnp.mean(xv * xv, axis=-1, keepdims=True) + EPS)
        gw = dh * nw_ref[...]
        dx_ref[...] = dres_ref[...] + r * gw - xv * (r * r * r) * jnp.mean(gw * xv, axis=-1, keepdims=True)

        @pl.when(pl.program_id(0) == 0)
        def _():
            dw_ref[...] = jnp.zeros_like(dw_ref)

        dw_ref[...] += _rowgroups(dh * xv * r)

    def dh_body(dg_ref, du_ref, wg_ref, wu_ref, x_ref, nw_ref, dres_ref, *rest):
        tot = None
        for j in range(N_CHIP):
            r = _dot_nt(dg_ref[j], wg_ref[j]) + _dot_nt(du_ref[j], wu_ref[j])
            tot = r if tot is None else tot + r
        norm_bwd_tail(tot, x_ref, nw_ref, dres_ref, *rest[-2:])

    deps = on_grads(l, dict(w_gate=g_gate, w_up=g_up, w_down=g_down))
    dx1, d_ffn_norm = pl.pallas_call(
        dh_body, name=pre + "ffn_dh", grid=(nt,),
        in_specs=[ff3, ff3, _resident(w_gate.shape), _resident(w_up.shape), row(D_MODEL), _full((1, D_MODEL)),
                  row(D_MODEL)] + [_full(a.shape) for a in deps],
        out_specs=[row(D_MODEL), _full((8, D_MODEL))], out_shape=[_sds((t, D_MODEL), F32), _sds((8, D_MODEL), F32)],
        compiler_params=_params(("arbitrary",)),
    )(dgate, dup, w_gate, w_up, sv["x1"], sm["ffn_norm_w"][l], dx2, *deps)

    (dmix,) = _mm(pre + "out_proj_dx", (nt,),
                  [(dx1, row(D_MODEL), wget(l, "w_out", dx2),
                    pl.BlockSpec((N_CHIP, OUT_BLK, D_MODEL), lambda i: (0, 0, 0)), NT)], [],
                  [(_sds((t, D_MODEL), F32), row(D_MODEL))], lambda tot: (tot,))
    (g_out,) = _mm(pre + "out_proj_dw", (N_CHIP, nk),
                   [(sv["mixed"], pl.BlockSpec((tk, OUT_BLK), lambda j, k: (k, j)), dx1, colt(D_MODEL), TN)], [],
                   [(_sds((N_CHIP, OUT_BLK, D_MODEL), BF16), pl.BlockSpec((None, OUT_BLK, D_MODEL), lambda j, k: (j, 0, 0)))],
                   lambda tot: (tot,), acc=(1, (OUT_BLK, D_MODEL)))
    proj = sv["proj"]
    dya, do_h, dhg, dyc, d_aw, d_gw, d_cw = _mix_bwd(
        pre + "mix_bwd", dmix, sv["y_attn"], sv["o_fw"], sv["o_bw"], proj, sv["y_conv"],
        sm["attn_out_norm_w"][l], sm["gnorm_w"][l], sm["conv_out_norm_w"][l], cst["seg_h"],
        on_grads(l, dict(w_out=g_out)))
    dqs, dkr, dvr = _attn_bwd(pre + "attn_bwd", sv["qn"], sv["kr"], sv["vr"], dya, b, s)
    dp_attn, d_qw, d_kw = _attn_prep_bwd(pre + "attn_prep_bwd", proj, s, tabs, sm["q_norm_w"][l], sm["k_norm_w"][l],
                                         cst["attn"], dqs, dkr, dvr)
    dq_f, dv_f, dz_fw, dq_b, dv_b, dz_bw, dlb_fw, dlb_bw = _hgrn_bwd2(
        pre + "hgrn_bwd", proj, sm["lb"][l][0], sm["lb"][l][1], sv["st_fw"], sv["st_bw"], do_h, b, s,
        cst["hg_fw"], cst["hg_bw"])
    dp_conv, d_pw, d_dw, d_cvec = _conv_bwd(pre + "conv_bwd", proj, sv["conv_out"], sm["conv_dw_w"][l],
                                            sm["conv_ln_w"][l], sm["conv_ln_b"][l], sm["conv_pw_w"][l], dyc, b, s)
    dproj = _dproj(pre + "dproj", dp_attn, dq_f, dq_b, dz_fw, dz_bw, dv_f, dv_b, dhg, dp_conv)
    g_pw = d_pw.reshape(N_CHIP, D_CONV // N_CHIP, D_CONV).astype(BF16)

    (g_in,) = _mm(pre + "in_proj_dw", (N_CHIP, nk),
                  [(sv["h1"], colt(D_MODEL), dproj, pl.BlockSpec((tk, IN_BLK), lambda j, k: (k, j)), TN)], [],
                  [(_sds((N_CHIP, D_MODEL, IN_BLK), BF16), pl.BlockSpec((None, D_MODEL, IN_BLK), lambda j, k: (j, 0, 0)))],
                  lambda tot: (tot,), acc=(1, (D_MODEL, IN_BLK)))

    def indx_body(dp_ref, w_ref, x_ref, nw_ref, dres_ref, *rest):
        tot = None
        for j in range(N_CHIP):
            r = _dot_nt(dp_ref[:, j * IN_BLK:(j + 1) * IN_BLK], w_ref[j])
            tot = r if tot is None else tot + r
        norm_bwd_tail(tot, x_ref, nw_ref, dres_ref, *rest[-2:])

    w_in = wget(l, "w_in", dx2)
    deps = on_grads(l, dict(w_in=g_in, conv_pw_w=g_pw))
    dx, d_mix_norm = pl.pallas_call(
        indx_body, name=pre + "in_proj_dx", grid=(nt,),
        in_specs=[row(D_IN), _resident(w_in.shape), row(D_MODEL), _full((1, D_MODEL)), row(D_MODEL)]
        + [_full(a.shape) for a in deps],
        out_specs=[row(D_MODEL), _full((8, D_MODEL))], out_shape=[_sds((t, D_MODEL), F32), _sds((8, D_MODEL), F32)],
        compiler_params=_params(("arbitrary",)),
    )(dproj, w_in, sv["x"], sm["mix_norm_w"][l], dx1, *deps)
    heads = lambda v, n: v.sum(axis=0).reshape(n, HEAD_DIM).sum(axis=0)
    small = dict(
        mix_norm_w=d_mix_norm.sum(axis=0), q_norm_w=heads(d_qw, D_ATTN // HEAD_DIM), k_norm_w=heads(d_kw, N_KV),
        lb=jnp.stack([dlb_fw.sum(axis=0), dlb_bw.sum(axis=0)]), hgrn_gnorm_w=heads(d_gw, D_HGRN // HEAD_DIM),
        conv_dw_w=d_dw[:CONV_W], conv_dw_b=d_cvec[3], conv_ln_w=d_cvec[1], conv_ln_b=d_cvec[2],
        conv_pw_b=d_cvec[0], attn_out_norm_w=d_aw.sum(axis=0), conv_out_norm_w=d_cw.sum(axis=0),
        ffn_norm_w=d_ffn_norm.sum(axis=0))
    return dx, small


SMALL_ORDER = ("mix_norm_w", "q_norm_w", "k_norm_w", "lb", "hgrn_gnorm_w", "conv_dw_w", "conv_dw_b", "conv_ln_w",
               "conv_ln_b", "conv_pw_b", "attn_out_norm_w", "conv_out_norm_w", "ffn_norm_w")
BIG_ORDER = ("w_in", "w_out", "w_gate", "w_up", "w_down")
SCATTER_ORDER = BIG_ORDER + ("conv_pw_w",)


def _local_step(x, target, wget, sm, deps, on_grads):
    b, s, d = x.shape
    t = b * s
    cos, sin = _rope_tables(s)
    tabs = dict(cq=jnp.tile(cos, (1, D_ATTN // HEAD_DIM)), sq=jnp.tile(sin, (1, D_ATTN // HEAD_DIM)),
                ck=jnp.tile(cos, (1, N_KV)), sk=jnp.tile(sin, (1, N_KV)))
    cst = dict(attn=_attn_consts(), hg_fw=_hgrn_consts(False), hg_bw=_hgrn_consts(True),
               seg_h=_bf(_seg_matrix(D_HGRN, HEAD_DIM, 1.0 / HEAD_DIM)))
    vec = lambda a: a.reshape(DEPTH, 1, -1)
    smk = dict(sm)
    for n in ("mix_norm_w", "conv_dw_b", "conv_ln_w", "conv_ln_b", "conv_pw_b", "attn_out_norm_w", "conv_out_norm_w",
              "ffn_norm_w"):
        smk[n] = vec(sm[n])
    smk["q_norm_w"] = vec(jnp.tile(sm["q_norm_w"], (1, D_ATTN // HEAD_DIM)))
    smk["k_norm_w"] = vec(jnp.tile(sm["k_norm_w"], (1, N_KV)))
    smk["gnorm_w"] = vec(jnp.tile(sm["hgrn_gnorm_w"], (1, D_HGRN // HEAD_DIM)))
    smk["lb"] = sm["lb"].reshape(DEPTH, 2, 1, D_HGRN)
    smk["conv_dw_w"] = jnp.pad(sm["conv_dw_w"], ((0, 0), (0, 1), (0, 0)))

    h = x.reshape(t, d)
    saved = []
    for l in range(DEPTH):
        h, sv = _layer_fwd(l, h, wget, smk, tabs, cst, b, s, deps if l == 0 else (),
                           target.reshape(t, d) if l == DEPTH - 1 else None)
        saved.append(sv)
    dy, sq = h
    sq_sum = jnp.sum(sq)
    dh = dy
    smalls = [None] * DEPTH
    for l in reversed(range(DEPTH)):
        dh, smalls[l] = _layer_bwd(l, dh, saved[l], wget, smk, tabs, cst, b, s, on_grads)
    return sq_sum, dh.reshape(b, s, d), smalls


HBM_SPEC = pl.BlockSpec(memory_space=pltpu.HBM)


def _exchange(name, arrs, mode):
    n = len(arrs)
    if mode == "gather8":
        flips = [(fx, fy, fc) for fx in (0, 1) for fy in (0, 1) for fc in (0, 1)][1:]
    elif mode == "sibling":
        flips = [(0, 0, 1)]
    else:
        flips = [(1, 0, 0), (0, 1, 0), (1, 1, 0)]
    n_f = len(flips)

    def body(*refs):
        ins, outs = refs[:n], refs[n:2 * n]
        send_sems, recv_sems, local_sems = refs[2 * n:]
        x, y, c = lax.axis_index("x"), lax.axis_index("y"), lax.axis_index("c")

        def slot_of(px, py, pc):
            return (2 * px + py) if mode != "gather8" else (4 * px + 2 * py + pc)

        me = slot_of(x, y, c)
        started = []
        for i in range(n):
            if mode != "sibling":
                src = ins[i].at[me] if mode == "scatter4" else ins[i]
                loc = pltpu.make_async_copy(src, outs[i].at[me], local_sems.at[i])
                loc.start()
                started.append(loc)
        sends, recvs = [], []
        for i in range(n):
            for f, (fx, fy, fc) in enumerate(flips):
                peer = (x ^ fx, y ^ fy, c ^ fc)
                ps = slot_of(*peer)
                if mode == "sibling":
                    src, dst, landed = ins[i], outs[i], outs[i]
                elif mode == "scatter4":
                    src, dst, landed = ins[i].at[ps], outs[i].at[me], outs[i].at[ps]
                else:
                    src, dst, landed = ins[i], outs[i].at[me], outs[i].at[ps]
                k = i * n_f + f
                cp = pltpu.make_async_remote_copy(src_ref=src, dst_ref=dst, send_sem=send_sems.at[k],
                                                  recv_sem=recv_sems.at[k], device_id=peer,
                                                  device_id_type=pl.DeviceIdType.MESH)
                cp.start()
                sends.append(cp)
                recvs.append(pltpu.make_async_remote_copy(src_ref=src, dst_ref=landed, send_sem=send_sems.at[k],
                                                          recv_sem=recv_sems.at[k], device_id=peer,
                                                          device_id_type=pl.DeviceIdType.MESH))
        for cp in sends:
            cp.wait_send()
        for cp in recvs:
            cp.wait_recv()
        for loc in started:
            loc.wait()

    def out_sds(a):
        if mode == "gather4":
            return _sds((N_CHIP,) + a.shape, a.dtype)
        if mode == "gather8":
            return _sds((N_DEV,) + a.shape, a.dtype)
        return _sds(a.shape, a.dtype)

    res = pl.pallas_call(
        body, name=name, in_specs=[HBM_SPEC] * n, out_specs=[HBM_SPEC] * n, out_shape=[out_sds(a) for a in arrs],
        scratch_shapes=[pltpu.SemaphoreType.DMA((n * n_f,)), pltpu.SemaphoreType.DMA((n * n_f,)),
                        pltpu.SemaphoreType.DMA((max(n, 1),))],
    )(*arrs)
    return list(res)


SEM_SPEC = pl.BlockSpec(memory_space=pltpu.SEMAPHORE)
SPLIT_EFFECT = pltpu.SideEffectType.DATAFLOW_SIDE_EFFECTING
CHIP_FLIPS = ((1, 0), (0, 1), (1, 1))


def _chip_copies(src_refs, land_refs, send_sems, recv_sems, scatter):
    x, y, c = lax.axis_index("x"), lax.axis_index("y"), lax.axis_index("c")
    me = 2 * x + y
    out = []
    for i, land in enumerate(land_refs):
        if scatter == "sibling":
            kw = dict(send_sem=send_sems.at[i], recv_sem=recv_sems.at[i], device_id=(x, y, 1 - c),
                      device_id_type=pl.DeviceIdType.MESH)
            cp = pltpu.make_async_remote_copy(src_ref=src_refs[i], dst_ref=land, **kw)
            out.append((cp, cp))
            continue
        for f, (fx, fy) in enumerate(CHIP_FLIPS):
            peer = (x ^ fx, y ^ fy, c)
            ps = 2 * (x ^ fx) + (y ^ fy)
            src = src_refs[i].at[ps] if scatter else land.at[me]
            k = i * len(CHIP_FLIPS) + f
            kw = dict(send_sem=send_sems.at[k], recv_sem=recv_sems.at[k], device_id=peer,
                      device_id_type=pl.DeviceIdType.MESH)
            out.append((pltpu.make_async_remote_copy(src_ref=src, dst_ref=land.at[me], **kw),
                        pltpu.make_async_remote_copy(src_ref=src, dst_ref=land.at[ps], **kw)))
    return out


def _split_start(name, srcs, lands, scatter):
    n = len(lands)
    n_src = len(srcs)
    n_sem = n if scatter == "sibling" else n * len(CHIP_FLIPS)

    def body(*refs):
        src_refs = refs[:n_src]
        land_refs = refs[n_src:n_src + n]
        send_sems, recv_sems = refs[n_src + n], refs[n_src + n + 1]
        token = refs[-1]
        for start, _ in _chip_copies(src_refs, land_refs, send_sems, recv_sems, scatter):
            start.start()
        token[...] = jnp.zeros_like(token)

    arrs = list(srcs) + list(lands)
    res = pl.pallas_call(
        body, name=name,
        out_shape=(pltpu.SemaphoreType.DMA((n_sem,)), pltpu.SemaphoreType.DMA((n_sem,)),
                   *[pltpu.HBM(a.shape, a.dtype) for a in arrs], _sds((8, LANES), F32)),
        in_specs=[HBM_SPEC] * len(arrs),
        out_specs=(SEM_SPEC, SEM_SPEC, *[HBM_SPEC] * len(arrs), pl.BlockSpec(memory_space=pltpu.VMEM)),
        input_output_aliases={i: 2 + i for i in range(len(arrs))},
        compiler_params=pltpu.CompilerParams(has_side_effects=SPLIT_EFFECT),
    )(*[pltpu.with_memory_space_constraint(a, pltpu.HBM) for a in arrs])
    return dict(send=res[0], recv=res[1], srcs=list(res[2:2 + n_src]), lands=list(res[2 + n_src:2 + n_src + n]),
                token=res[-1], scatter=scatter)


def _split_wait(name, started, after, with_srcs=False):
    srcs, lands, scatter = started["srcs"], started["lands"], started["scatter"]
    n, n_src = len(lands), len(srcs)

    def body(*refs):
        src_refs = refs[:n_src]
        land_refs = refs[n_src:n_src + n]
        send_sems, recv_sems = refs[n_src + n], refs[n_src + n + 1]
        for _, wait in _chip_copies(src_refs, land_refs, send_sems, recv_sems, scatter):
            wait.wait_send()
            wait.wait_recv()

    arrs = list(srcs) + list(lands)
    res = pl.pallas_call(
        body, name=name, out_shape=tuple(pltpu.HBM(a.shape, a.dtype) for a in arrs),
        in_specs=[HBM_SPEC] * len(arrs) + [SEM_SPEC, SEM_SPEC, pl.BlockSpec(memory_space=pl.ANY)],
        out_specs=tuple([HBM_SPEC] * len(arrs)), input_output_aliases={i: i for i in range(len(arrs))},
        compiler_params=pltpu.CompilerParams(has_side_effects=SPLIT_EFFECT),
    )(*arrs, started["send"], started["recv"], after)
    return (list(res[:n_src]), list(res[n_src:])) if with_srcs else list(res[n_src:])


def _flat_tile(rows):
    for cand in (512, 256, 128, 64, 32, 16, 8):
        if rows % cand == 0:
            return cand
    return rows


def _cast_slot(name, a, l, chip):
    r, c = a.shape[0] // DEPTH, a.shape[1]
    tr = _flat_tile(r)

    def body(chip_ref, a_ref, o_ref):
        o_ref[...] = a_ref[...].astype(BF16)

    return pl.pallas_call(
        body, name=name, out_shape=_sds((N_CHIP, r, c), BF16),
        grid_spec=pltpu.PrefetchScalarGridSpec(
            num_scalar_prefetch=1, grid=(r // tr,),
            in_specs=[pl.BlockSpec((tr, c), lambda i, ch: (l * (r // tr) + i, 0))],
            out_specs=pl.BlockSpec((None, tr, c), lambda i, ch: (ch[0], i, 0))),
        compiler_params=_params(("parallel",)))(chip, a)


def _own_slot(name, g, chip):
    n, r, c = g.shape
    tr = _flat_tile(r)

    def body(chip_ref, g_ref, o_ref):
        o_ref[...] = g_ref[...]

    spec = pl.BlockSpec((None, tr, c), lambda i, ch: (ch[0], i, 0))
    return pl.pallas_call(
        body, name=name, out_shape=_sds(g.shape, g.dtype),
        grid_spec=pltpu.PrefetchScalarGridSpec(num_scalar_prefetch=1, grid=(r // tr,), in_specs=[spec], out_specs=spec),
        compiler_params=_params(("parallel",)))(chip, g)


def _sum_layers(name, lands):
    n, r, c = lands[0].shape
    tr = _flat_tile(r)
    nl = len(lands)

    def body(*refs):
        o_ref = refs[-1]
        for k in range(nl):
            @pl.when(pl.program_id(0) == k)
            def _():
                tot = refs[k][0].astype(F32)
                for i in range(1, n):
                    tot = tot + refs[k][i].astype(F32)
                o_ref[...] = tot

    return pl.pallas_call(
        body, name=name, grid=(nl, r // tr),
        in_specs=[pl.BlockSpec((n, tr, c), lambda l, i, k=k: (0, jnp.where(l == k, i, 0), 0)) for k in range(nl)],
        out_specs=pl.BlockSpec((tr, c), lambda l, i: (l * (r // tr) + i, 0)), out_shape=_sds((nl * r, c), F32),
        compiler_params=_params(("arbitrary", "arbitrary")))(*lands)


def _sum_slots(name, a, scale=None):
    n, r, c = a.shape
    tr = _flat_tile(r)

    def body(a_ref, o_ref):
        tot = a_ref[0].astype(F32)
        for i in range(1, n):
            tot = tot + a_ref[i].astype(F32)
        o_ref[...] = tot

    return pl.pallas_call(body, name=name, grid=(r // tr,),
                          in_specs=[pl.BlockSpec((n, tr, c), lambda i: (0, i, 0))],
                          out_specs=pl.BlockSpec((tr, c), lambda i: (i, 0)), out_shape=_sds((r, c), F32),
                          compiler_params=_params(("parallel",)))(a)


def _adamw(name, w, ga, gb, m, v):
    r, c = w.shape
    tr = _flat_tile(r)
    c1 = 1.0 - B1 ** STEP
    c2 = 1.0 - B2 ** STEP
    two = gb is not None

    def body(*refs):
        if two:
            w_ref, ga_ref, gb_ref, m_ref, v_ref, g_out, d_out, m_out, v_out = refs
            g = ga_ref[...] + gb_ref[...]
        else:
            w_ref, ga_ref, m_ref, v_ref, g_out, d_out, m_out, v_out = refs
            g = ga_ref[...]
        mn = B1 * m_ref[...] + (1.0 - B1) * g
        vn = B2 * v_ref[...] + (1.0 - B2) * (g * g)
        g_out[...] = g
        m_out[...] = mn
        v_out[...] = vn
        d_out[...] = -LR * ((mn / c1) / (jnp.sqrt(vn / c2) + ADAM_EPS) + WD * w_ref[...])

    spec = pl.BlockSpec((tr, c), lambda i: (i, 0))
    ins = [w, ga, gb, m, v] if two else [w, ga, m, v]
    return pl.pallas_call(body, name=name, grid=(r // tr,), in_specs=[spec] * len(ins), out_specs=[spec] * 4,
                          out_shape=[_sds((r, c), F32)] * 4, compiler_params=_params(("parallel",)))(*ins)


WEIGHTS = ('mix_norm_w', 'w_in', 'q_norm_w', 'k_norm_w', 'hgrn_lb_logits', 'hgrn_gnorm_w', 'conv_dw_w', 'conv_dw_b',
           'conv_ln_w', 'conv_ln_b', 'conv_pw_w', 'conv_pw_b', 'attn_out_norm_w', 'conv_out_norm_w', 'w_out',
           'ffn_norm_w', 'w_gate', 'w_up', 'w_down')
SHARDED_SMALL = {"hgrn_lb_logits": 2, "conv_dw_w": 2, "conv_pw_w": 1}
LANES = 128
PACK_ROWS = 256


def _pack(parts):
    flat = jnp.concatenate([p.reshape(-1) for p in parts])
    n = flat.shape[0]
    rows = -(-n // (PACK_ROWS * LANES)) * PACK_ROWS
    return jnp.pad(flat, (0, rows * LANES - n)).reshape(rows, LANES)


def _unpack(packed, shapes):
    flat = packed.reshape(-1)
    out, off = [], 0
    for shp in shapes:
        n = int(np.prod(shp))
        out.append(flat[off:off + n].reshape(shp))
        off += n
    return out


def kernel(x, mix_norm_w, w_in, q_norm_w, k_norm_w, hgrn_lb_logits, hgrn_gnorm_w, conv_dw_w, conv_dw_b, conv_ln_w, conv_ln_b, conv_pw_w, conv_pw_b, attn_out_norm_w, conv_out_norm_w, w_out, ffn_norm_w, w_gate, w_up, w_down, loss_target, m_mix_norm_w, m_w_in, m_q_norm_w, m_k_norm_w, m_hgrn_lb_logits, m_hgrn_gnorm_w, m_conv_dw_w, m_conv_dw_b, m_conv_ln_w, m_conv_ln_b, m_conv_pw_w, m_conv_pw_b, m_attn_out_norm_w, m_conv_out_norm_w, m_w_out, m_ffn_norm_w, m_w_gate, m_w_up, m_w_down, v_mix_norm_w, v_w_in, v_q_norm_w, v_k_norm_w, v_hgrn_lb_logits, v_hgrn_gnorm_w, v_conv_dw_w, v_conv_dw_b, v_conv_ln_w, v_conv_ln_b, v_conv_pw_w, v_conv_pw_b, v_attn_out_norm_w, v_conv_out_norm_w, v_w_out, v_ffn_norm_w, v_w_gate, v_w_up, v_w_down):
    w = dict(mix_norm_w=mix_norm_w, w_in=w_in, q_norm_w=q_norm_w, k_norm_w=k_norm_w, hgrn_lb_logits=hgrn_lb_logits,
             hgrn_gnorm_w=hgrn_gnorm_w, conv_dw_w=conv_dw_w, conv_dw_b=conv_dw_b, conv_ln_w=conv_ln_w,
             conv_ln_b=conv_ln_b, conv_pw_w=conv_pw_w, conv_pw_b=conv_pw_b, attn_out_norm_w=attn_out_norm_w,
             conv_out_norm_w=conv_out_norm_w, w_out=w_out, ffn_norm_w=ffn_norm_w, w_gate=w_gate, w_up=w_up,
             w_down=w_down)
    m = dict(mix_norm_w=m_mix_norm_w, w_in=m_w_in, q_norm_w=m_q_norm_w, k_norm_w=m_k_norm_w,
             hgrn_lb_logits=m_hgrn_lb_logits, hgrn_gnorm_w=m_hgrn_gnorm_w, conv_dw_w=m_conv_dw_w,
             conv_dw_b=m_conv_dw_b, conv_ln_w=m_conv_ln_w, conv_ln_b=m_conv_ln_b, conv_pw_w=m_conv_pw_w,
             conv_pw_b=m_conv_pw_b, attn_out_norm_w=m_attn_out_norm_w, conv_out_norm_w=m_conv_out_norm_w,
             w_out=m_w_out, ffn_norm_w=m_ffn_norm_w, w_gate=m_w_gate, w_up=m_w_up, w_down=m_w_down)
    v = dict(mix_norm_w=v_mix_norm_w, w_in=v_w_in, q_norm_w=v_q_norm_w, k_norm_w=v_k_norm_w,
             hgrn_lb_logits=v_hgrn_lb_logits, hgrn_gnorm_w=v_hgrn_gnorm_w, conv_dw_w=v_conv_dw_w,
             conv_dw_b=v_conv_dw_b, conv_ln_w=v_conv_ln_w, conv_ln_b=v_conv_ln_b, conv_pw_w=v_conv_pw_w,
             conv_pw_b=v_conv_pw_b, attn_out_norm_w=v_attn_out_norm_w, conv_out_norm_w=v_conv_out_norm_w,
             w_out=v_w_out, ffn_norm_w=v_ffn_norm_w, w_gate=v_w_gate, w_up=v_w_up, w_down=v_w_down)
    chip = 2 * lax.axis_index("x") + lax.axis_index("y")

    chip1 = chip.reshape(1).astype(jnp.int32)

    flat2 = lambda a: a.reshape(-1, a.shape[-1])
    groups = [[(l, "w_in")] if first else [(l, n) for n in BIG_ORDER[1:]] for l in range(DEPTH) for first in (1, 0)]
    group_of = {key: g for g, keys in enumerate(groups) for key in keys}
    starts = []
    for g, keys in enumerate(groups):
        slots = [_cast_slot("cast_%s_l%d" % (n, l), flat2(w[n]), l, chip1) for l, n in keys]
        starts.append(_split_start("gather_start_g%d" % g, [], slots, False))
        if g == 0:
            small_pack = _pack([w[n] for n in SHARDED_SMALL]) + starts[0]["token"][0, 0]
            gathered = _exchange("gather_small_weights", [small_pack], "gather4")
    got = {}

    def wget(l, name, after):
        if (l, name) not in got:
            g = group_of[(l, name)]
            for key, arr in zip(groups[g], _split_wait("gather_wait_g%d" % g, starts[g], after)):
                got[key] = arr
        return got[(l, name)]

    pending = []

    def on_grads(l, grads):
        names = [n for n in SCATTER_ORDER if n in grads]
        own = [_own_slot("own_%s_l%d" % (n, l), grads[n], chip1) for n in names]
        st = _split_start("scatter_start_l%d_%s" % (l, names[0]), [grads[n] for n in names], own, True)
        pending.append((l, names, st))
        return [st["token"]]

    parts = [_unpack(gathered[-1][j], [w[n].shape for n in SHARDED_SMALL]) for j in range(N_CHIP)]
    full_small = {n: jnp.concatenate([parts[j][i] for j in range(N_CHIP)], axis=ax)
                  for i, (n, ax) in enumerate(SHARDED_SMALL.items())}
    sm = {n: w[n] for n in WEIGHTS if n not in BIG_ORDER and n not in SHARDED_SMALL}
    sm["conv_dw_w"] = full_small["conv_dw_w"]
    sm["conv_pw_w"] = full_small["conv_pw_w"]
    logits = full_small["hgrn_lb_logits"].reshape(DEPTH * 2, D_HGRN)
    sm["lb"] = _lower_bounds(logits).reshape(DEPTH, 2, D_HGRN)

    sq_sum, grad_x, smalls = _local_step(x, loss_target, wget, sm, [st["token"] for st in starts], on_grads)
    loss = lax.psum(0.5 * sq_sum / D_MODEL, ("x", "y", "c"))

    landed = {}
    for l, names, st in pending:
        for n, arr in zip(names, _split_wait("scatter_wait_l%d_%s" % (l, names[0]), st, grad_x)):
            landed[(l, n)] = arr
    sums = [_sum_layers("sum_" + n, [landed[(l, n)] for l in range(DEPTH)]) for n in SCATTER_ORDER]
    sib_start = _split_start("sibling_start", sums, [lax.empty(a.shape, a.dtype) for a in sums], "sibling")
    out = {}

    small_names = [n for n in WEIGHTS if n not in SCATTER_ORDER]
    g_pack = _pack([jnp.stack([smalls[l][n] for l in range(DEPTH)]) for n in SMALL_ORDER])
    g_all = _exchange("gather_small_grads", [g_pack], "gather8")[0]
    g_tot = _sum_slots("sum_small", g_all)
    shapes = [(DEPTH,) + tuple(smalls[0][n].shape) for n in SMALL_ORDER]
    g_small = dict(zip(SMALL_ORDER, _unpack(g_tot, shapes)))
    lb_shard = lax.dynamic_slice_in_dim(g_small.pop("lb").reshape(DEPTH * 2, D_HGRN), chip * HEAD_DIM, HEAD_DIM, 1)
    g_small["hgrn_lb_logits"] = _lower_bounds_bwd(hgrn_lb_logits.reshape(DEPTH * 2, HEAD_DIM), lb_shard).reshape(
        hgrn_lb_logits.shape)
    g_small["conv_dw_w"] = lax.dynamic_slice_in_dim(g_small["conv_dw_w"], chip * HEAD_DIM, HEAD_DIM, 2)
    res = _adamw("adamw_small", _pack([w[n] for n in small_names]), _pack([g_small[n] for n in small_names]), None,
                 _pack([m[n] for n in small_names]), _pack([v[n] for n in small_names]))
    unpacked = [_unpack(r, [w[n].shape for n in small_names]) for r in res]
    for i, n in enumerate(small_names):
        out[n] = [unpacked[k][i] for k in range(4)]
    own, sib = _split_wait("sibling_wait", sib_start, res[0], with_srcs=True)
    for n, ga, gb in zip(SCATTER_ORDER, own, sib):
        big = _adamw("adamw_" + n, flat2(w[n]), ga, gb, flat2(m[n]), flat2(v[n]))
        out[n] = [r.reshape(w[n].shape) for r in big]

    return (loss, grad_x, *[out[n][0] for n in WEIGHTS], *[out[n][1] for n in WEIGHTS],
            *[out[n][2] for n in WEIGHTS], *[out[n][3] for n in WEIGHTS])
```

```python
import functools

import numpy as np
import jax
import jax.numpy as jnp
from jax import lax
from jax.experimental import pallas as pl
from jax.experimental.pallas import tpu as pltpu

F32, BF16 = jnp.float32, jnp.bfloat16

D_MODEL = 1024
DEPTH = 2
GRID_W = 64
D_ATTN, D_HGRN, D_CONV = 512, 256, 256
HEAD_DIM = 64
N_KV = 2
KV_LANES = D_ATTN // N_KV
ROPE_THETA = 10000.0
F_MIN = 1e-6
CONV_W = 31
CONV_PAD = 15
D_FF = 2816
D_IN = 2560
N_CHIP = 4
N_DEV = 8
IN_BLK = D_IN // N_CHIP
FF_BLK = D_FF // N_CHIP
OUT_BLK = D_MODEL // N_CHIP
EPS = 1e-6
LN_EPS = 1e-5
LR, B1, B2, ADAM_EPS, WD, STEP = 0.001, 0.9, 0.999, 1e-08, 0.01, 10
CHUNK = 16
HBLK = 256
CONV_TILE = 128
BWD_GROUP = 2
VMEM_LIMIT = 56 * 1024 * 1024

COL_Q, COL_K, COL_V = 0, 4, 5
COL_HQ, COL_FF, COL_FB, COL_HI, COL_HG, COL_CA, COL_CB = 3, 4, 5, 6, 7, 8, 9


def _params(sem=None):
    return pltpu.CompilerParams(dimension_semantics=sem, vmem_limit_bytes=VMEM_LIMIT)


def _sds(shape, dtype):
    return jax.ShapeDtypeStruct(tuple(shape), dtype)


def _full(shape):
    n = len(shape)
    return pl.BlockSpec(tuple(shape), lambda *_: (0,) * n)


def _sigmoid(x):
    return 0.5 * jnp.tanh(0.5 * x) + 0.5


def _gate_sigmoid(x):
    return 1.0 / (1.0 + jnp.exp(-x))


def _silu(x):
    return x * _sigmoid(x)


def _dsilu(x):
    s = _sigmoid(x)
    return s * (1.0 + x * (1.0 - s))


def _rowgroups(v):
    m, c = v.shape
    return v.reshape(m // 8, 8, c).sum(axis=0)


def _split2(x):
    hi = x.astype(BF16)
    lo = (x - hi.astype(F32)).astype(BF16)
    return hi, lo


def _rdot2(x, m):
    hi, lo = _split2(x)
    return (jnp.dot(hi, m, preferred_element_type=F32) + jnp.dot(lo, m, preferred_element_type=F32))


def _ldot3(m, x):
    hi = x.astype(BF16)
    r1 = x - hi.astype(F32)
    mid = r1.astype(BF16)
    lo = (r1 - mid.astype(F32)).astype(BF16)
    return (jnp.dot(m, hi, preferred_element_type=F32) + jnp.dot(m, mid, preferred_element_type=F32)
            + jnp.dot(m, lo, preferred_element_type=F32))


def _dot_nt(a, b):
    return lax.dot_general(a, b, (((1,), (1,)), ((), ())), preferred_element_type=F32)


def _dot_tn(a, b):
    return lax.dot_general(a, b, (((0,), (0,)), ((), ())), preferred_element_type=F32)


def _seg_matrix(n, seg, val):
    i = np.arange(n)
    return ((i[:, None] // seg) == (i[None, :] // seg)).astype(np.float32) * val


def _rot_matrix(n):
    r = np.zeros((n, n), np.float32)
    for i in range(n):
        if (i % 32) < 16:
            r[i + 16, i] = -1.0
        else:
            r[i - 16, i] = 1.0
    return r


def _rep_matrix():
    r = np.zeros((N_KV * HEAD_DIM, D_ATTN), np.float32)
    for kv in range(N_KV):
        for g in range(KV_LANES // HEAD_DIM):
            for d in range(HEAD_DIM):
                r[HEAD_DIM * kv + d, KV_LANES * kv + HEAD_DIM * g + d] = 1.0
    return r


def _cumsum_matrix(rev):
    i = np.arange(HBLK)
    same = (i[:, None] // CHUNK) == (i[None, :] // CHUNK)
    tri = (i[None, :] >= i[:, None]) if rev else (i[None, :] <= i[:, None])
    return (same & tri).astype(np.float32)


def _sel_matrices():
    sel = np.zeros((CHUNK, CHUNK * CHUNK), np.float32)
    selt = np.zeros((CHUNK, CHUNK * CHUNK), np.float32)
    for t in range(CHUNK):
        for s in range(CHUNK):
            sel[t, t * CHUNK + s] = 1.0
            selt[s, t * CHUNK + s] = 1.0
    return sel, selt


def _bf(a):
    return jnp.asarray(a, dtype=BF16)


def _mm(name, grid, pairs, extras, outs, epilogue, acc=None, sem=None):
    n_p, n_e, n_o = len(pairs), len(extras), len(outs)

    def body(*refs):
        ab = refs[:2 * n_p]
        ex = refs[2 * n_p:2 * n_p + n_e]
        out = refs[2 * n_p + n_e:2 * n_p + n_e + n_o]
        scr = refs[2 * n_p + n_e + n_o:]
        tot = None
        for i in range(n_p):
            a = ab[2 * i][...]
            b = ab[2 * i + 1][...]
            if a.ndim == 3:
                a = a.reshape(-1, a.shape[-1])
            if b.ndim == 3:
                b = b.reshape(-1, b.shape[-1])
            r = lax.dot_general(a.astype(BF16), b.astype(BF16), pairs[i][4], preferred_element_type=F32)
            tot = r if tot is None else tot + r

        def finish(total):
            res = epilogue(total, *[e[...] for e in ex])
            for o_ref, val in zip(out, res):
                o_ref[...] = val.astype(o_ref.dtype)

        if acc is None:
            finish(tot)
        else:
            k = pl.program_id(acc[0])

            @pl.when(k == 0)
            def _():
                scr[0][...] = tot

            @pl.when(k > 0)
            def _():
                scr[0][...] += tot

            @pl.when(k == grid[acc[0]] - 1)
            def _():
                finish(scr[0][...])

    args, in_specs = [], []
    for a, a_spec, b, b_spec, _ in pairs:
        args += [a, b]
        in_specs += [a_spec, b_spec]
    for e, e_spec in extras:
        args.append(e)
        in_specs.append(e_spec)
    if sem is None:
        sem = tuple("arbitrary" if (acc is not None and i == acc[0]) else "parallel" for i in range(len(grid)))
    return pl.pallas_call(
        body, name=name, grid=grid, in_specs=in_specs,
        out_specs=[o[1] for o in outs], out_shape=[o[0] for o in outs],
        scratch_shapes=[] if acc is None else [pltpu.VMEM(acc[1], F32)],
        compiler_params=_params(sem),
    )(*args)


NN = (((1,), (0,)), ((), ()))
NT = (((1,), (1,)), ((), ()))
TN = (((0,), (0,)), ((), ()))


def _row_tile(t):
    return min(256, t)


def _rms_fwd(name, x, w, deps=()):
    t, d = x.shape
    tm = _row_tile(t)

    def body(x_ref, w_ref, *rest):
        o_ref = rest[-1]
        xv = x_ref[...]
        r = lax.rsqrt(jnp.mean(xv * xv, axis=-1, keepdims=True) + EPS)
        o_ref[...] = (xv * r * w_ref[...]).astype(BF16)

    return pl.pallas_call(
        body, name=name, grid=(t // tm,),
        in_specs=[pl.BlockSpec((tm, d), lambda i: (i, 0)), _full((1, d))] + [_full(a.shape) for a in deps],
        out_specs=pl.BlockSpec((tm, d), lambda i: (i, 0)), out_shape=_sds((t, d), BF16),
        compiler_params=_params(("parallel",)),
    )(x, w, *deps)


def _rms_bwd(name, x, w, dh, dres, deps=()):
    t, d = x.shape
    tm = _row_tile(t)

    def body(x_ref, w_ref, dh_ref, dres_ref, *rest):
        dx_ref, dw_ref = rest[-2:]
        xv = x_ref[...]
        r = lax.rsqrt(jnp.mean(xv * xv, axis=-1, keepdims=True) + EPS)
        dy = dh_ref[...]
        gw = dy * w_ref[...]
        dx_ref[...] = dres_ref[...] + r * gw - xv * (r * r * r) * jnp.mean(gw * xv, axis=-1, keepdims=True)

        @pl.when(pl.program_id(0) == 0)
        def _():
            dw_ref[...] = jnp.zeros_like(dw_ref)

        dw_ref[...] += _rowgroups(dy * xv * r)

    tile = pl.BlockSpec((tm, d), lambda i: (i, 0))
    return pl.pallas_call(
        body, name=name, grid=(t // tm,),
        in_specs=[tile, _full((1, d)), tile, tile] + [_full(a.shape) for a in deps],
        out_specs=[tile, _full((8, d))], out_shape=[_sds((t, d), F32), _sds((8, d), F32)],
        compiler_params=_params(("arbitrary",)),
    )(x, w, dh, dres, *deps)


def _loss_kernel(y, target):
    t, d = y.shape
    tm = _row_tile(t)

    def body(y_ref, t_ref, dy_ref, acc_ref):
        e = y_ref[...] - t_ref[...]
        dy_ref[...] = e * (1.0 / d)

        @pl.when(pl.program_id(0) == 0)
        def _():
            acc_ref[...] = jnp.zeros_like(acc_ref)

        acc_ref[...] += _rowgroups(e * e)

    tile = pl.BlockSpec((tm, d), lambda i: (i, 0))
    return pl.pallas_call(
        body, name="loss_head", grid=(t // tm,), in_specs=[tile, tile],
        out_specs=[tile, _full((8, d))], out_shape=[_sds((t, d), F32), _sds((8, d), F32)],
        compiler_params=_params(("arbitrary",)),
    )(y, target)


def _rope_tables(s):
    rows = s // GRID_W
    row_id = jnp.repeat(jnp.arange(rows, dtype=F32), GRID_W)
    col_id = jnp.tile(jnp.arange(GRID_W, dtype=F32), rows)
    half = HEAD_DIM // 2
    inv_freq = ROPE_THETA ** (-jnp.arange(0, half, 2, dtype=F32) / half)
    ang_r = row_id[:, None] * inv_freq[None, :]
    ang_c = col_id[:, None] * inv_freq[None, :]
    ang = jnp.concatenate([ang_r, ang_r, ang_c, ang_c], axis=-1)
    return jnp.cos(ang).astype(F32), jnp.sin(ang).astype(F32)


def _attn_consts():
    return dict(
        seg_q=_bf(_seg_matrix(D_ATTN, HEAD_DIM, 1.0 / HEAD_DIM)),
        seg_k=_bf(_seg_matrix(N_KV * HEAD_DIM, HEAD_DIM, 1.0 / HEAD_DIM)),
        rot_q=_bf(_rot_matrix(D_ATTN)), rot_k=_bf(_rot_matrix(N_KV * HEAD_DIM)),
        rep=_bf(_rep_matrix()), rep_t=_bf(_rep_matrix().T))


def _attn_prep(name, proj, s, tabs, qw, kw, ac):
    t = proj.shape[0]
    tm = _row_tile(s)
    nst = s // tm
    kw_ = N_KV * HEAD_DIM

    def body(q_ref, k_ref, v_ref, cq_ref, sq_ref, ck_ref, sk_ref, qw_ref, kw_ref,
             segq_ref, segk_ref, rotq_ref, rotk_ref, rep_ref, qn_ref, kr_ref, vr_ref):
        q = q_ref[...]
        r = lax.rsqrt(jnp.dot((q * q).astype(BF16), segq_ref[...], preferred_element_type=F32) + EPS)
        qn = q * r * qw_ref[...]
        qr = qn * cq_ref[...] + _rdot2(qn, rotq_ref[...]) * sq_ref[...]
        qn_ref[...] = (qr * (HEAD_DIM ** -0.5)).astype(BF16)
        k = k_ref[...]
        rk = lax.rsqrt(jnp.dot((k * k).astype(BF16), segk_ref[...], preferred_element_type=F32) + EPS)
        kn = k * rk * kw_ref[...]
        kr = kn * ck_ref[...] + _rdot2(kn, rotk_ref[...]) * sk_ref[...]
        kr_ref[...] = jnp.dot(kr.astype(BF16), rep_ref[...], preferred_element_type=F32).astype(BF16)
        vr_ref[...] = jnp.dot(v_ref[...].astype(BF16), rep_ref[...], preferred_element_type=F32).astype(BF16)

    wide = pl.BlockSpec((tm, D_ATTN), lambda i: (i, 0))
    tabq = pl.BlockSpec((tm, D_ATTN), lambda i: (i % nst, 0))
    tabk = pl.BlockSpec((tm, kw_), lambda i: (i % nst, 0))
    return pl.pallas_call(
        body, name=name, grid=(t // tm,),
        in_specs=[pl.BlockSpec((tm, D_ATTN), lambda i: (i, COL_Q)), pl.BlockSpec((tm, kw_), lambda i: (i, COL_K)),
                  pl.BlockSpec((tm, kw_), lambda i: (i, COL_V)), tabq, tabq, tabk, tabk,
                  _full((1, D_ATTN)), _full((1, kw_)), _full((D_ATTN, D_ATTN)), _full((kw_, kw_)),
                  _full((D_ATTN, D_ATTN)), _full((kw_, kw_)), _full((kw_, D_ATTN))],
        out_specs=[wide, wide, wide], out_shape=[_sds((t, D_ATTN), BF16)] * 3,
        compiler_params=_params(("parallel",)),
    )(proj, proj, proj, tabs["cq"], tabs["sq"], tabs["ck"], tabs["sk"], qw, kw,
      ac["seg_q"], ac["seg_k"], ac["rot_q"], ac["rot_k"], ac["rep"])


def _attn_prep_bwd(name, proj, s, tabs, qw, kw, ac, dqs, dkr, dvr):
    t = proj.shape[0]
    tm = _row_tile(s)
    nst = s // tm
    kw_ = N_KV * HEAD_DIM
    wout = D_ATTN + 2 * kw_

    def norm_rope_bwd(x, w, cos, sin, seg, rot, d_roped):
        dn = d_roped * cos - _rdot2(d_roped * sin, rot)
        r = lax.rsqrt(jnp.dot((x * x).astype(BF16), seg, preferred_element_type=F32) + EPS)
        gw = dn * w
        dx = r * gw - x * (r * r * r) * _rdot2(gw * x, seg)
        return dx, _rowgroups(dn * x * r)

    def body(q_ref, k_ref, cq_ref, sq_ref, ck_ref, sk_ref, qw_ref, kw_ref, segq_ref, segk_ref, rotq_ref, rotk_ref,
             rept_ref, dqs_ref, dkr_ref, dvr_ref, dp_ref, dqw_ref, dkw_ref):
        dq, dqw = norm_rope_bwd(q_ref[...], qw_ref[...], cq_ref[...], sq_ref[...], segq_ref[...], rotq_ref[...],
                                dqs_ref[...] * (HEAD_DIM ** -0.5))
        dk_roped = _rdot2(dkr_ref[...], rept_ref[...])
        dk, dkw = norm_rope_bwd(k_ref[...], kw_ref[...], ck_ref[...], sk_ref[...], segk_ref[...], rotk_ref[...],
                                dk_roped)
        dv = _rdot2(dvr_ref[...], rept_ref[...])
        dp_ref[:, 0:D_ATTN] = dq.astype(BF16)
        dp_ref[:, D_ATTN:D_ATTN + kw_] = dk.astype(BF16)
        dp_ref[:, D_ATTN + kw_:wout] = dv.astype(BF16)

        @pl.when(pl.program_id(0) == 0)
        def _():
            dqw_ref[...] = jnp.zeros_like(dqw_ref)
            dkw_ref[...] = jnp.zeros_like(dkw_ref)

        dqw_ref[...] += dqw
        dkw_ref[...] += dkw

    wide = pl.BlockSpec((tm, D_ATTN), lambda i: (i, 0))
    tabq = pl.BlockSpec((tm, D_ATTN), lambda i: (i % nst, 0))
    tabk = pl.BlockSpec((tm, kw_), lambda i: (i % nst, 0))
    return pl.pallas_call(
        body, name=name, grid=(t // tm,),
        in_specs=[pl.BlockSpec((tm, D_ATTN), lambda i: (i, COL_Q)), pl.BlockSpec((tm, kw_), lambda i: (i, COL_K)),
                  tabq, tabq, tabk, tabk, _full((1, D_ATTN)), _full((1, kw_)),
                  _full((D_ATTN, D_ATTN)), _full((kw_, kw_)), _full((D_ATTN, D_ATTN)), _full((kw_, kw_)),
                  _full((D_ATTN, kw_)), wide, wide, wide],
        out_specs=[pl.BlockSpec((tm, wout), lambda i: (i, 0)), _full((8, D_ATTN)), _full((8, kw_))],
        out_shape=[_sds((t, wout), BF16), _sds((8, D_ATTN), F32), _sds((8, kw_), F32)],
        compiler_params=_params(("arbitrary",)),
    )(proj, proj, tabs["cq"], tabs["sq"], tabs["ck"], tabs["sk"], qw, kw,
      ac["seg_q"], ac["seg_k"], ac["rot_q"], ac["rot_k"], ac["rep_t"], dqs, dkr, dvr)


def _attn_tile(s):
    return min(256, s)


def _head_masks(shape):
    lane = lax.broadcasted_iota(jnp.int32, shape, 1)
    return [(lane // HEAD_DIM) == g for g in range(KV_LANES // HEAD_DIM)]


def _attn_fwd(name, qn, kr, vr, b, s):
    t = qn.shape[0]
    tq = _attn_tile(s)
    nq = s // tq

    def body(q_ref, k_ref, v_ref, o_ref):
        q = q_ref[...]
        k = k_ref[...]
        v = v_ref[...]
        acc = jnp.zeros((tq, KV_LANES), F32)
        for mask in _head_masks((tq, KV_LANES)):
            sc = _dot_nt(jnp.where(mask, q, jnp.zeros_like(q)), k)
            p = jnp.exp(sc - jnp.max(sc, axis=-1, keepdims=True))
            inv = 1.0 / jnp.sum(p, axis=-1, keepdims=True)
            og = jnp.dot(p.astype(BF16), v, preferred_element_type=F32) * inv
            acc = jnp.where(mask, og, acc)
        o_ref[...] = acc

    return pl.pallas_call(
        body, name=name, grid=(b, N_KV, nq),
        in_specs=[pl.BlockSpec((tq, KV_LANES), lambda bi, kv, i: (bi * nq + i, kv)),
                  pl.BlockSpec((s, KV_LANES), lambda bi, kv, i: (bi, kv)),
                  pl.BlockSpec((s, KV_LANES), lambda bi, kv, i: (bi, kv))],
        out_specs=pl.BlockSpec((tq, KV_LANES), lambda bi, kv, i: (bi * nq + i, kv)),
        out_shape=_sds((t, D_ATTN), F32),
        compiler_params=_params(("parallel", "parallel", "parallel")),
    )(qn, kr, vr)


def _attn_bwd(name, qn, kr, vr, do, b, s):
    t = qn.shape[0]
    tq = _attn_tile(s)
    nq = s // tq

    def body(q_ref, k_ref, v_ref, do_ref, dq_ref, dk_ref, dv_ref):
        @pl.when(pl.program_id(2) == 0)
        def _():
            dk_ref[...] = jnp.zeros_like(dk_ref)
            dv_ref[...] = jnp.zeros_like(dv_ref)

        q = q_ref[...]
        k = k_ref[...]
        v = v_ref[...]
        dout = do_ref[...].astype(BF16)
        masks = _head_masks((tq, KV_LANES))
        q4 = jnp.concatenate([jnp.where(m, q, jnp.zeros_like(q)) for m in masks], axis=0)
        do4 = jnp.concatenate([jnp.where(m, dout, jnp.zeros_like(dout)) for m in masks], axis=0)
        sc = _dot_nt(q4, k)
        p = jnp.exp(sc - jnp.max(sc, axis=-1, keepdims=True))
        p = p * (1.0 / jnp.sum(p, axis=-1, keepdims=True))
        dp = _dot_nt(do4, v)
        ds = (p * (dp - jnp.sum(p * dp, axis=-1, keepdims=True))).astype(BF16)
        dq4 = jnp.dot(ds, k, preferred_element_type=F32)
        dq = jnp.zeros((tq, KV_LANES), F32)
        for g, m in enumerate(masks):
            dq = jnp.where(m, dq4[g * tq:(g + 1) * tq, :], dq)
        dq_ref[...] = dq
        dk_ref[...] += _dot_tn(ds, q4)
        dv_ref[...] += _dot_tn(p.astype(BF16), do4)

    qspec = pl.BlockSpec((tq, KV_LANES), lambda bi, kv, i: (bi * nq + i, kv))
    kspec = pl.BlockSpec((s, KV_LANES), lambda bi, kv, i: (bi, kv))
    return pl.pallas_call(
        body, name=name, grid=(b, N_KV, nq),
        in_specs=[qspec, kspec, kspec, qspec],
        out_specs=[qspec, kspec, kspec], out_shape=[_sds((t, D_ATTN), F32)] * 3,
        compiler_params=_params(("parallel", "parallel", "arbitrary")),
    )(qn, kr, vr, do)


def _hgrn_consts(rev):
    sel, selt = _sel_matrices()
    cs = _cumsum_matrix(rev)
    return dict(cs=_bf(cs), cs_t=_bf(cs.T), seg=_bf(_seg_matrix(D_HGRN, HEAD_DIM, 1.0)),
                bd=jnp.asarray(_seg_matrix(D_HGRN, HEAD_DIM, 1.0), F32),
                sel=_bf(sel), selt=_bf(selt), seld=_bf(sel - selt))


def _gates(z, lb):
    sig = _gate_sigmoid(z)
    f = lb + (1.0 - lb) * sig
    g = jnp.log(jnp.maximum(f, F_MIN))
    sn = _gate_sigmoid(-z)
    return sig, f, g, sn, (1.0 - lb) * sn


def _pair_decay(b, rev):
    row = lax.broadcasted_iota(jnp.int32, (CHUNK, D_HGRN), 0)
    parts = []
    for t in range(CHUNK):
        m = (row >= t) if rev else (row <= t)
        parts.append(jnp.where(m, jnp.exp(jnp.minimum(b[t:t + 1, :] - b, 0.0)), 0.0))
    return jnp.concatenate(parts, axis=0)


def _rows_rep(a):
    return jnp.concatenate([jnp.broadcast_to(a[t:t + 1, :], a.shape) for t in range(CHUNK)], axis=0)


def _tile_rows(a):
    return jnp.concatenate([a] * CHUNK, axis=0)


def _hgrn_specs(b, s, rev):
    nb = s // HBLK

    def blk(j):
        return (nb - 1 - j) if rev else j

    def col(c):
        return pl.BlockSpec((HBLK, D_HGRN), lambda bi, j: (bi * nb + blk(j), c))

    return nb, blk, col


def _hgrn_fwd(name, proj, lb, b, s, rev, hc):
    t = proj.shape[0]
    nb, blk, col = _hgrn_specs(b, s, rev)
    n_ch = HBLK // CHUNK
    last = 0 if rev else CHUNK - 1

    def body(q_ref, z_ref, v_ref, lb_ref, cs_ref, seg_ref, bd_ref, sel_ref, o_ref, st_ref, state, b_scr, k_scr):
        @pl.when(pl.program_id(1) == 0)
        def _():
            state[...] = jnp.zeros_like(state)

        st_ref[...] = state[...]
        _, _, g, _, kk = _gates(z_ref[...], lb_ref[...])
        k_scr[...] = kk
        b_scr[...] = _ldot3(cs_ref[...], g)

        def chunk(i, carry):
            c = (n_ch - 1 - i) if rev else i
            rows = pl.ds(pl.multiple_of(c * CHUNK, CHUNK), CHUNK)
            q = q_ref[rows, :]
            k = k_scr[rows, :]
            v = v_ref[rows, :]
            bb = b_scr[rows, :]
            bl = bb[last:last + 1, :]
            pairs = _pair_decay(bb, rev) * _rows_rep(q) * _tile_rows(k)
            a = jnp.dot(pairs.astype(BF16), seg_ref[...], preferred_element_type=F32)
            o_intra = jnp.dot(sel_ref[...], (a * _tile_rows(v)).astype(BF16), preferred_element_type=F32)
            st = state[...]
            o_inter = _dot_nt((q * jnp.exp(bb)).astype(BF16), st.astype(BF16))
            o_ref[rows, :] = o_intra + o_inter
            ke = k * jnp.exp(bl - bb)
            state[...] = st * jnp.exp(bl) + bd_ref[...] * _dot_tn(v.astype(BF16), ke.astype(BF16))
            return carry

        lax.fori_loop(0, n_ch, chunk, 0)

    sq = (D_HGRN, D_HGRN)
    return pl.pallas_call(
        body, name=name, grid=(b, nb),
        in_specs=[col(COL_HQ), col(COL_FB if rev else COL_FF), col(COL_HI), _full((1, D_HGRN)),
                  _full((HBLK, HBLK)), _full(sq), _full(sq), _full((CHUNK, CHUNK * CHUNK))],
        out_specs=[pl.BlockSpec((HBLK, D_HGRN), lambda bi, j: (bi * nb + blk(j), 0)),
                   pl.BlockSpec((None,) + sq, lambda bi, j: (bi * nb + blk(j), 0, 0))],
        out_shape=[_sds((t, D_HGRN), F32), _sds((b * nb,) + sq, F32)],
        scratch_shapes=[pltpu.VMEM(sq, F32), pltpu.VMEM((HBLK, D_HGRN), F32), pltpu.VMEM((HBLK, D_HGRN), F32)],
        compiler_params=_params(("parallel", "arbitrary")),
    )(proj, proj, proj, lb, hc["cs"], hc["seg"], hc["bd"], hc["sel"])


def _hgrn_bwd(name, proj, lb, st_blk, do, dq_prev, dv_prev, b, s, rev, hc):
    t = proj.shape[0]
    nb = s // HBLK
    n_ch = HBLK // CHUNK
    last = 0 if rev else CHUNK - 1

    def blk(j):
        return j if rev else (nb - 1 - j)

    def col(c):
        return pl.BlockSpec((HBLK, D_HGRN), lambda bi, j: (bi * nb + blk(j), c))

    def body(q_ref, z_ref, v_ref, lb_ref, st_ref, do_ref, dqp_ref, dvp_ref, cs_ref, cst_ref, seg_ref, bd_ref,
             sel_ref, selt_ref, seld_ref, dq_ref, dv_ref, dz_ref, dlb_ref,
             dstate, states, b_scr, k_scr, db_scr, dk_scr):
        first = jnp.logical_and(pl.program_id(0) == 0, pl.program_id(1) == 0)

        @pl.when(first)
        def _():
            dlb_ref[...] = jnp.zeros_like(dlb_ref)

        @pl.when(pl.program_id(1) == 0)
        def _():
            dstate[...] = jnp.zeros_like(dstate)

        lbv = lb_ref[...]
        z = z_ref[...]
        sig, f, g, sn, kk = _gates(z, lbv)
        k_scr[...] = kk
        b_scr[...] = _ldot3(cs_ref[...], g)

        def rows_of(c):
            return pl.ds(pl.multiple_of(c * CHUNK, CHUNK), CHUNK)

        def replay(i, st):
            c = (n_ch - 1 - i) if rev else i
            rows = rows_of(c)
            states[c] = st
            bb = b_scr[rows, :]
            bl = bb[last:last + 1, :]
            ke = k_scr[rows, :] * jnp.exp(bl - bb)
            return st * jnp.exp(bl) + bd_ref[...] * _dot_tn(v_ref[rows, :].astype(BF16), ke.astype(BF16))

        lax.fori_loop(0, n_ch, replay, st_ref[...])
        row = lax.broadcasted_iota(jnp.int32, (CHUNK, D_HGRN), 0)

        def chunk(i, carry):
            c = i if rev else (n_ch - 1 - i)
            rows = rows_of(c)
            q = q_ref[rows, :]
            k = k_scr[rows, :]
            v = v_ref[rows, :]
            bb = b_scr[rows, :]
            dout = do_ref[rows, :]
            bl = bb[last:last + 1, :]
            st_p = states[c]
            dst_n = dstate[...]
            eb = jnp.exp(bb)
            ebl = jnp.exp(bl - bb)
            ebl_last = jnp.exp(bl)
            qe = q * eb
            ke = k * ebl
            dob = dout.astype(BF16)
            dstb = dst_n.astype(BF16)
            dqe = jnp.dot(dob, st_p.astype(BF16), preferred_element_type=F32)
            dke = jnp.dot(v.astype(BF16), dstb, preferred_element_type=F32)
            dv = _dot_nt(ke.astype(BF16), dstb)
            dbl = jnp.sum(dst_n * st_p, axis=0, keepdims=True) * ebl_last + jnp.sum(dke * ke, axis=0, keepdims=True)
            dq = dqe * eb
            dk = dke * ebl
            db = dqe * qe - dke * ke
            dec = _pair_decay(bb, rev)
            q_rep = _rows_rep(q)
            k_til = _tile_rows(k)
            do_rep = _rows_rep(dout)
            pairs = dec * q_rep * k_til
            a = jnp.dot(pairs.astype(BF16), seg_ref[...], preferred_element_type=F32)
            wb = jnp.dot((_tile_rows(v) * do_rep).astype(BF16), seg_ref[...], preferred_element_type=F32)
            gdec = wb * dec
            dq = dq + jnp.dot(sel_ref[...], (gdec * k_til).astype(BF16), preferred_element_type=F32)
            dk = dk + jnp.dot(selt_ref[...], (gdec * q_rep).astype(BF16), preferred_element_type=F32)
            dv = dv + jnp.dot(selt_ref[...], (a * do_rep).astype(BF16), preferred_element_type=F32)
            db = db + jnp.dot(seld_ref[...], (wb * pairs).astype(BF16), preferred_element_type=F32)
            db = db + jnp.where(row == last, dbl, 0.0)
            dq_ref[rows, :] = dq + dqp_ref[rows, :]
            dv_ref[rows, :] = dv + dvp_ref[rows, :]
            dk_scr[rows, :] = dk
            db_scr[rows, :] = db
            dstate[...] = dst_n * ebl_last + bd_ref[...] * _dot_tn(dob, qe.astype(BF16))
            return carry

        lax.fori_loop(0, n_ch, chunk, 0)
        hi, lo = _split2(db_scr[...])
        dg = (jnp.dot(cst_ref[...], hi, preferred_element_type=F32)
              + jnp.dot(cst_ref[...], lo, preferred_element_type=F32))
        dgf = jnp.where(f > F_MIN, dg / f, 0.0)
        dk = dk_scr[...]
        dz_ref[...] = dgf * (1.0 - lbv) * sig * (1.0 - sig) - dk * (1.0 - lbv) * sn * (1.0 - sn)
        dlb_ref[...] += _rowgroups(dgf * (1.0 - sig) - dk * sn)

    sq = (D_HGRN, D_HGRN)
    blk0 = pl.BlockSpec((HBLK, D_HGRN), lambda bi, j: (bi * nb + blk(j), 0))
    pairs_shape = (CHUNK, CHUNK * CHUNK)
    return pl.pallas_call(
        body, name=name, grid=(b, nb),
        in_specs=[col(COL_HQ), col(COL_FB if rev else COL_FF), col(COL_HI), _full((1, D_HGRN)),
                  pl.BlockSpec((None,) + sq, lambda bi, j: (bi * nb + blk(j), 0, 0)), blk0, blk0, blk0,
                  _full((HBLK, HBLK)), _full((HBLK, HBLK)), _full(sq), _full(sq),
                  _full(pairs_shape), _full(pairs_shape), _full(pairs_shape)],
        out_specs=[blk0, blk0, blk0, _full((8, D_HGRN))],
        out_shape=[_sds((t, D_HGRN), F32)] * 3 + [_sds((8, D_HGRN), F32)],
        scratch_shapes=[pltpu.VMEM(sq, F32), pltpu.VMEM((n_ch,) + sq, F32)] + [pltpu.VMEM((HBLK, D_HGRN), F32)] * 4,
        compiler_params=_params(("arbitrary", "arbitrary")),
    )(proj, proj, proj, lb, st_blk, do, dq_prev, dv_prev,
      hc["cs"], hc["cs_t"], hc["seg"], hc["bd"], hc["sel"], hc["selt"], hc["seld"])


def _scan_chunk_fwd(c, rev, q_ref, v_ref, k_scr, b_scr, state, o_ref, seg_ref, bd_ref, sel_ref):
    last = 0 if rev else CHUNK - 1
    rows = pl.ds(pl.multiple_of(c * CHUNK, CHUNK), CHUNK)
    q = q_ref[rows, :]
    k = k_scr[rows, :]
    v = v_ref[rows, :]
    bb = b_scr[rows, :]
    bl = bb[last:last + 1, :]
    pairs = _pair_decay(bb, rev) * _rows_rep(q) * _tile_rows(k)
    a = jnp.dot(pairs.astype(BF16), seg_ref[...], preferred_element_type=F32)
    o_intra = jnp.dot(sel_ref[...], (a * _tile_rows(v)).astype(BF16), preferred_element_type=F32)
    st = state[...]
    o_inter = _dot_nt((q * jnp.exp(bb)).astype(BF16), st.astype(BF16))
    o_ref[rows, :] = o_intra + o_inter
    ke = k * jnp.exp(bl - bb)
    state[...] = st * jnp.exp(bl) + bd_ref[...] * _dot_tn(v.astype(BF16), ke.astype(BF16))


def _scan_chunks_fwd(chains, seg_ref, bd_ref, sel_ref):
    work = []
    for c, rev, q_ref, v_ref, k_scr, b_scr, state, o_ref in chains:
        last = 0 if rev else CHUNK - 1
        rows = pl.ds(pl.multiple_of(c * CHUNK, CHUNK), CHUNK)
        q = q_ref[rows, :]
        k = k_scr[rows, :]
        v = v_ref[rows, :]
        bb = b_scr[rows, :]
        bl = bb[last:last + 1, :]
        st = state[...]
        work.append(dict(
            rows=rows, v=v, st=st, state=state, o_ref=o_ref, decay=jnp.exp(bl),
            pairs=(_pair_decay(bb, rev) * _rows_rep(q) * _tile_rows(k)).astype(BF16),
            qe=(q * jnp.exp(bb)).astype(BF16), ke=(k * jnp.exp(bl - bb)).astype(BF16), st_b=st.astype(BF16)))
    for w in work:
        w["a"] = jnp.dot(w["pairs"], seg_ref[...], preferred_element_type=F32)
        w["o_inter"] = _dot_nt(w["qe"], w["st_b"])
        w["upd"] = _dot_tn(w["v"].astype(BF16), w["ke"])
    for w in work:
        w["av"] = (w["a"] * _tile_rows(w["v"])).astype(BF16)
    for w in work:
        w["o_ref"][w["rows"], :] = jnp.dot(sel_ref[...], w["av"], preferred_element_type=F32) + w["o_inter"]
        w["state"][...] = w["st"] * w["decay"] + bd_ref[...] * w["upd"]


def _hgrn_fwd2(name, proj, lb_f, lb_b, b, s, hc_f, hc_b):
    t = proj.shape[0]
    nb = s // HBLK
    n_ch = HBLK // CHUNK
    n_chain = 2 * b

    def body(qf_ref, zf_ref, vf_ref, qb_ref, zb_ref, vb_ref, lbf_ref, lbb_ref, csf_ref, csb_ref, seg_ref, bd_ref,
             sel_ref, of_ref, ob_ref, stf_ref, stb_ref, *scr):
        state, b_scr, k_scr = scr[:n_chain], scr[n_chain:2 * n_chain], scr[2 * n_chain:]

        @pl.when(pl.program_id(0) == 0)
        def _():
            for st0 in state:
                st0[...] = jnp.zeros_like(st0)

        chains = []
        for bi in range(b):
            chains.append((False, qf_ref.at[bi], zf_ref.at[bi], vf_ref.at[bi], lbf_ref, csf_ref, of_ref.at[bi],
                           stf_ref.at[bi], 2 * bi))
            chains.append((True, qb_ref.at[bi], zb_ref.at[bi], vb_ref.at[bi], lbb_ref, csb_ref, ob_ref.at[bi],
                           stb_ref.at[bi], 2 * bi + 1))
        for rev, q, z, v, lb, cs, o, st, ci in chains:
            st[...] = state[ci][...]
            _, _, g, _, kk = _gates(z[...], lb[...])
            k_scr[ci][...] = kk
            b_scr[ci][...] = _ldot3(cs[...], g)

        def chunk(i, carry):
            _scan_chunks_fwd([((n_ch - 1 - i) if rev else i, rev, q, v, k_scr[ci], b_scr[ci], state[ci], o)
                              for rev, q, z, v, lb, cs, o, st, ci in chains], seg_ref, bd_ref, sel_ref)
            return carry

        lax.fori_loop(0, n_ch, chunk, 0)

    def col(c, rev):
        return pl.BlockSpec((b, HBLK, D_HGRN), lambda j: (0, (nb - 1 - j) if rev else j, c))

    def st_spec(rev):
        return pl.BlockSpec((b, None, D_HGRN, D_HGRN), lambda j: (0, (nb - 1 - j) if rev else j, 0, 0))

    sq = (D_HGRN, D_HGRN)
    proj3 = proj.reshape(b, s, proj.shape[1])
    o_fw, o_bw, st_fw, st_bw = pl.pallas_call(
        body, name=name, grid=(nb,),
        in_specs=[col(COL_HQ, False), col(COL_FF, False), col(COL_HI, False),
                  col(COL_HQ, True), col(COL_FB, True), col(COL_HI, True),
                  _full((1, D_HGRN)), _full((1, D_HGRN)), _full((HBLK, HBLK)), _full((HBLK, HBLK)),
                  _full(sq), _full(sq), _full((CHUNK, CHUNK * CHUNK))],
        out_specs=[col(0, False), col(0, True), st_spec(False), st_spec(True)],
        out_shape=[_sds((b, s, D_HGRN), F32)] * 2 + [_sds((b, nb) + sq, F32)] * 2,
        scratch_shapes=[pltpu.VMEM(sq, F32)] * n_chain + [pltpu.VMEM((HBLK, D_HGRN), F32)] * (2 * n_chain),
        compiler_params=_params(("arbitrary",)),
    )(proj3, proj3, proj3, proj3, proj3, proj3, lb_f, lb_b, hc_f["cs"], hc_b["cs"], hc_f["seg"], hc_f["bd"],
      hc_f["sel"])
    return o_fw.reshape(t, D_HGRN), o_bw.reshape(t, D_HGRN), st_fw, st_bw


def _scan_replay(c, rev, st, v_ref, k_scr, b_scr, states, bd_ref):
    last = 0 if rev else CHUNK - 1
    rows = pl.ds(pl.multiple_of(c * CHUNK, CHUNK), CHUNK)
    states[c] = st
    bb = b_scr[rows, :]
    bl = bb[last:last + 1, :]
    ke = k_scr[rows, :] * jnp.exp(bl - bb)
    return st * jnp.exp(bl) + bd_ref[...] * _dot_tn(v_ref[rows, :].astype(BF16), ke.astype(BF16))


def _scan_replays(chains, bd_ref):
    work = []
    for c, rev, st, v_ref, k_scr, b_scr, states in chains:
        last = 0 if rev else CHUNK - 1
        rows = pl.ds(pl.multiple_of(c * CHUNK, CHUNK), CHUNK)
        states[c] = st
        bb = b_scr[rows, :]
        bl = bb[last:last + 1, :]
        work.append((st, jnp.exp(bl), v_ref[rows, :].astype(BF16), (k_scr[rows, :] * jnp.exp(bl - bb)).astype(BF16)))
    upds = [_dot_tn(v, ke) for _, _, v, ke in work]
    return tuple(st * decay + bd_ref[...] * upd for (st, decay, _, _), upd in zip(work, upds))


def _scan_chunks_bwd(chains, seg_ref, bd_ref, sel_ref, selt_ref, seld_ref):
    row = lax.broadcasted_iota(jnp.int32, (CHUNK, D_HGRN), 0)
    work = []
    for c, rev, q_ref, v_ref, do_ref, k_scr, b_scr, states, dstate, dq_ref, dv_ref, dk_scr, db_scr in chains:
        last = 0 if rev else CHUNK - 1
        rows = pl.ds(pl.multiple_of(c * CHUNK, CHUNK), CHUNK)
        q = q_ref[rows, :]
        k = k_scr[rows, :]
        v = v_ref[rows, :]
        bb = b_scr[rows, :]
        dout = do_ref[rows, :]
        bl = bb[last:last + 1, :]
        st_p = states[c]
        dst_n = dstate[...]
        eb = jnp.exp(bb)
        ebl = jnp.exp(bl - bb)
        qe = q * eb
        ke = k * ebl
        dec = _pair_decay(bb, rev)
        q_rep = _rows_rep(q)
        k_til = _tile_rows(k)
        do_rep = _rows_rep(dout)
        pairs = dec * q_rep * k_til
        work.append(dict(
            rows=rows, last=last, eb=eb, ebl=ebl, ebl_last=jnp.exp(bl), qe=qe, ke=ke, dec=dec, q_rep=q_rep, k_til=k_til,
            do_rep=do_rep, pairs=pairs, st_p=st_p, dst_n=dst_n, dstate=dstate, dq_ref=dq_ref, dv_ref=dv_ref,
            dk_scr=dk_scr, db_scr=db_scr, dob=dout.astype(BF16), dstb=dst_n.astype(BF16), vb=v.astype(BF16),
            pairs_b=pairs.astype(BF16), vdo_b=(_tile_rows(v) * do_rep).astype(BF16)))
    for w in work:
        w["dqe"] = jnp.dot(w["dob"], w["st_p"].astype(BF16), preferred_element_type=F32)
        w["dke"] = jnp.dot(w["vb"], w["dstb"], preferred_element_type=F32)
        w["dv"] = _dot_nt(w["ke"].astype(BF16), w["dstb"])
        w["a"] = jnp.dot(w["pairs_b"], seg_ref[...], preferred_element_type=F32)
        w["wb"] = jnp.dot(w["vdo_b"], seg_ref[...], preferred_element_type=F32)
        w["dst_upd"] = _dot_tn(w["dob"], w["qe"].astype(BF16))
    for w in work:
        gdec = w["wb"] * w["dec"]
        w["x_dq"] = (gdec * w["k_til"]).astype(BF16)
        w["x_dk"] = (gdec * w["q_rep"]).astype(BF16)
        w["x_dv"] = (w["a"] * w["do_rep"]).astype(BF16)
        w["x_db"] = (w["wb"] * w["pairs"]).astype(BF16)
    for w in work:
        dke, dqe = w["dke"], w["dqe"]
        dbl = (jnp.sum(w["dst_n"] * w["st_p"], axis=0, keepdims=True) * w["ebl_last"]
               + jnp.sum(dke * w["ke"], axis=0, keepdims=True))
        dq = dqe * w["eb"] + jnp.dot(sel_ref[...], w["x_dq"], preferred_element_type=F32)
        dk = dke * w["ebl"] + jnp.dot(selt_ref[...], w["x_dk"], preferred_element_type=F32)
        dv = w["dv"] + jnp.dot(selt_ref[...], w["x_dv"], preferred_element_type=F32)
        db = (dqe * w["qe"] - dke * w["ke"] + jnp.dot(seld_ref[...], w["x_db"], preferred_element_type=F32)
              + jnp.where(row == w["last"], dbl, 0.0))
        w["dq_ref"][w["rows"], :] = dq
        w["dv_ref"][w["rows"], :] = dv
        w["dk_scr"][w["rows"], :] = dk
        w["db_scr"][w["rows"], :] = db
        w["dstate"][...] = w["dst_n"] * w["ebl_last"] + bd_ref[...] * w["dst_upd"]


def _scan_chunk_bwd(c, rev, q_ref, v_ref, do_ref, k_scr, b_scr, states, dstate, dq_ref, dv_ref, dk_scr, db_scr,
                    seg_ref, bd_ref, sel_ref, selt_ref, seld_ref):
    last = 0 if rev else CHUNK - 1
    row = lax.broadcasted_iota(jnp.int32, (CHUNK, D_HGRN), 0)
    rows = pl.ds(pl.multiple_of(c * CHUNK, CHUNK), CHUNK)
    q = q_ref[rows, :]
    k = k_scr[rows, :]
    v = v_ref[rows, :]
    bb = b_scr[rows, :]
    dout = do_ref[rows, :]
    bl = bb[last:last + 1, :]
    st_p = states[c]
    dst_n = dstate[...]
    eb = jnp.exp(bb)
    ebl = jnp.exp(bl - bb)
    ebl_last = jnp.exp(bl)
    qe = q * eb
    ke = k * ebl
    dob = dout.astype(BF16)
    dstb = dst_n.astype(BF16)
    dqe = jnp.dot(dob, st_p.astype(BF16), preferred_element_type=F32)
    dke = jnp.dot(v.astype(BF16), dstb, preferred_element_type=F32)
    dv = _dot_nt(ke.astype(BF16), dstb)
    dbl = jnp.sum(dst_n * st_p, axis=0, keepdims=True) * ebl_last + jnp.sum(dke * ke, axis=0, keepdims=True)
    dq = dqe * eb
    dk = dke * ebl
    db = dqe * qe - dke * ke
    dec = _pair_decay(bb, rev)
    q_rep = _rows_rep(q)
    k_til = _tile_rows(k)
    do_rep = _rows_rep(dout)
    pairs = dec * q_rep * k_til
    a = jnp.dot(pairs.astype(BF16), seg_ref[...], preferred_element_type=F32)
    wb = jnp.dot((_tile_rows(v) * do_rep).astype(BF16), seg_ref[...], preferred_element_type=F32)
    gdec = wb * dec
    dq = dq + jnp.dot(sel_ref[...], (gdec * k_til).astype(BF16), preferred_element_type=F32)
    dk = dk + jnp.dot(selt_ref[...], (gdec * q_rep).astype(BF16), preferred_element_type=F32)
    dv = dv + jnp.dot(selt_ref[...], (a * do_rep).astype(BF16), preferred_element_type=F32)
    db = db + jnp.dot(seld_ref[...], (wb * pairs).astype(BF16), preferred_element_type=F32)
    db = db + jnp.where(row == last, dbl, 0.0)
    dq_ref[rows, :] = dq
    dv_ref[rows, :] = dv
    dk_scr[rows, :] = dk
    db_scr[rows, :] = db
    dstate[...] = dst_n * ebl_last + bd_ref[...] * _dot_tn(dob, qe.astype(BF16))


def _hgrn_bwd2(name, proj, lb_f, lb_b, st_f, st_b, do, b, s, hc_f, hc_b):
    t = proj.shape[0]
    nb = s // HBLK
    n_ch = HBLK // CHUNK

    n_chain = 2 * b

    def body(qf_ref, zf_ref, vf_ref, dof_ref, stf_ref, qb_ref, zb_ref, vb_ref, dob_ref, stb_ref, lbf_ref, lbb_ref,
             csf_ref, csb_ref, cstf_ref, cstb_ref, seg_ref, bd_ref, sel_ref, selt_ref, seld_ref,
             dqf_ref, dvf_ref, dzf_ref, dqb_ref, dvb_ref, dzb_ref, dlbf_ref, dlbb_ref,
             *scr):
        dstate, states, b_scr, k_scr, db_scr, dk_scr = [scr[i * n_chain:(i + 1) * n_chain] for i in range(6)]

        @pl.when(pl.program_id(0) == 0)
        def _():
            dlbf_ref[...] = jnp.zeros_like(dlbf_ref)
            dlbb_ref[...] = jnp.zeros_like(dlbb_ref)
            for d0 in dstate:
                d0[...] = jnp.zeros_like(d0)

        chains = []
        for bi in range(b):
            chains.append(dict(rev=False, q=qf_ref.at[bi], z=zf_ref.at[bi], v=vf_ref.at[bi], do=dof_ref.at[bi],
                               st=stf_ref.at[bi], lb=lbf_ref, cs=csf_ref, cst=cstf_ref, dq=dqf_ref.at[bi],
                               dv=dvf_ref.at[bi], dz=dzf_ref.at[bi], dlb=dlbf_ref, ci=2 * bi))
            chains.append(dict(rev=True, q=qb_ref.at[bi], z=zb_ref.at[bi], v=vb_ref.at[bi], do=dob_ref.at[bi],
                               st=stb_ref.at[bi], lb=lbb_ref, cs=csb_ref, cst=cstb_ref, dq=dqb_ref.at[bi],
                               dv=dvb_ref.at[bi], dz=dzb_ref.at[bi], dlb=dlbb_ref, ci=2 * bi + 1))
        for ch in chains:
            sig, f, g, sn, kk = _gates(ch["z"][...], ch["lb"][...])
            k_scr[ch["ci"]][...] = kk
            b_scr[ch["ci"]][...] = _ldot3(ch["cs"][...], g)
            ch["gates"] = (sig, f, sn)

        def replay(i, carry):
            return _scan_replays([((n_ch - 1 - i) if ch["rev"] else i, ch["rev"], st, ch["v"], k_scr[ch["ci"]],
                                   b_scr[ch["ci"]], states[ch["ci"]]) for ch, st in zip(chains, carry)], bd_ref)

        lax.fori_loop(0, n_ch, replay, tuple(ch["st"][...] for ch in chains))

        def chunk(i, carry):
            args = [(i if ch["rev"] else (n_ch - 1 - i), ch["rev"], ch["q"], ch["v"], ch["do"],
                     k_scr[ch["ci"]], b_scr[ch["ci"]], states[ch["ci"]], dstate[ch["ci"]], ch["dq"],
                     ch["dv"], dk_scr[ch["ci"]], db_scr[ch["ci"]]) for ch in chains]
            for g0 in range(0, n_chain, BWD_GROUP):
                _scan_chunks_bwd(args[g0:g0 + BWD_GROUP], seg_ref, bd_ref, sel_ref, selt_ref, seld_ref)
            return carry

        lax.fori_loop(0, n_ch, chunk, 0)
        for ch in chains:
            sig, f, sn = ch["gates"]
            lbv = ch["lb"][...]
            hi, lo = _split2(db_scr[ch["ci"]][...])
            dg = (jnp.dot(ch["cst"][...], hi, preferred_element_type=F32)
                  + jnp.dot(ch["cst"][...], lo, preferred_element_type=F32))
            dgf = jnp.where(f > F_MIN, dg / f, 0.0)
            dk = dk_scr[ch["ci"]][...]
            ch["dz"][...] = dgf * (1.0 - lbv) * sig * (1.0 - sig) - dk * (1.0 - lbv) * sn * (1.0 - sn)
            ch["dlb"][...] += _rowgroups(dgf * (1.0 - sig) - dk * sn)

    def col(c, rev):
        return pl.BlockSpec((b, HBLK, D_HGRN), lambda j: (0, j if rev else (nb - 1 - j), c))

    def st_spec(rev):
        return pl.BlockSpec((b, None, D_HGRN, D_HGRN), lambda j: (0, j if rev else (nb - 1 - j), 0, 0))

    sq = (D_HGRN, D_HGRN)
    blk = (HBLK, D_HGRN)
    pairs_shape = (CHUNK, CHUNK * CHUNK)
    proj3 = proj.reshape(b, s, proj.shape[1])
    do3 = do.reshape(b, s, D_HGRN)
    res = pl.pallas_call(
        body, name=name, grid=(nb,),
        in_specs=[col(COL_HQ, False), col(COL_FF, False), col(COL_HI, False), col(0, False), st_spec(False),
                  col(COL_HQ, True), col(COL_FB, True), col(COL_HI, True), col(0, True), st_spec(True),
                  _full((1, D_HGRN)), _full((1, D_HGRN)), _full((HBLK, HBLK)), _full((HBLK, HBLK)),
                  _full((HBLK, HBLK)), _full((HBLK, HBLK)), _full(sq), _full(sq),
                  _full(pairs_shape), _full(pairs_shape), _full(pairs_shape)],
        out_specs=[col(0, False)] * 3 + [col(0, True)] * 3 + [_full((8, D_HGRN))] * 2,
        out_shape=[_sds((b, s, D_HGRN), F32)] * 6 + [_sds((8, D_HGRN), F32)] * 2,
        scratch_shapes=[pltpu.VMEM(sq, F32)] * n_chain + [pltpu.VMEM((n_ch,) + sq, F32)] * n_chain
        + [pltpu.VMEM(blk, F32)] * (4 * n_chain),
        compiler_params=_params(("arbitrary",)),
    )(proj3, proj3, proj3, do3, st_f, proj3, proj3, proj3, do3, st_b, lb_f, lb_b, hc_f["cs"], hc_b["cs"],
      hc_f["cs_t"], hc_b["cs_t"], hc_f["seg"], hc_f["bd"], hc_f["sel"], hc_f["selt"], hc_f["seld"])
    return [r.reshape(t, D_HGRN) for r in res[:6]] + list(res[6:])


def _lower_bounds(logits):
    n = logits.shape[1]

    def body(x_ref, o_ref):
        x = x_ref[...]
        for d in range(2):
            rows = [x[l * 2 + d:l * 2 + d + 1, :] for l in range(DEPTH)]
            mx = functools.reduce(jnp.maximum, rows)
            ex = [jnp.exp(r - mx) for r in rows]
            tot = functools.reduce(lambda a, c: a + c, ex)
            sm = [e / tot for e in ex]
            run = jnp.zeros_like(sm[0])
            for l in range(DEPTH):
                run = run + sm[l]
                o_ref[l * 2 + d:l * 2 + d + 1, :] = run - sm[0]

    return pl.pallas_call(body, name="hgrn_lower_bounds", out_shape=_sds(logits.shape, F32),
                          in_specs=[_full(logits.shape)], out_specs=_full(logits.shape), grid=(1,),
                          compiler_params=_params(("arbitrary",)))(logits)


def _lower_bounds_bwd(logits, dlb):
    def body(x_ref, g_ref, o_ref):
        x = x_ref[...]
        gv = g_ref[...]
        for d in range(2):
            rows = [x[l * 2 + d:l * 2 + d + 1, :] for l in range(DEPTH)]
            gr = [gv[l * 2 + d:l * 2 + d + 1, :] for l in range(DEPTH)]
            mx = functools.reduce(jnp.maximum, rows)
            ex = [jnp.exp(r - mx) for r in rows]
            tot = functools.reduce(lambda a, c: a + c, ex)
            sm = [e / tot for e in ex]
            dsm = []
            for i in range(DEPTH):
                acc = functools.reduce(lambda a, c: a + c, gr[i:])
                if i == 0:
                    acc = acc - functools.reduce(lambda a, c: a + c, gr)
                dsm.append(acc)
            inner = functools.reduce(lambda a, c: a + c, [sm[i] * dsm[i] for i in range(DEPTH)])
            for i in range(DEPTH):
                o_ref[i * 2 + d:i * 2 + d + 1, :] = sm[i] * (dsm[i] - inner)

    return pl.pallas_call(body, name="hgrn_lower_bounds_bwd", out_shape=_sds(logits.shape, F32),
                          in_specs=[_full(logits.shape), _full(logits.shape)], out_specs=_full(logits.shape),
                          grid=(1,), compiler_params=_params(("arbitrary",)))(logits, dlb)


def _conv_rows(s):
    return s + 2 * (CONV_PAD + 1)


def _conv_fwd(name, proj, dw_w, dw_b, ln_w, ln_b, pw_w, pw_b, b, s):
    t = proj.shape[0]
    pad = CONV_PAD + 1
    nt = s // CONV_TILE

    def body(a_ref, g_ref, w_ref, dwb_ref, lnw_ref, lnb_ref, pw_ref, pwb_ref, y_ref, c_ref, upad, win):
        upad[0:pad, :] = jnp.zeros((pad, D_CONV), F32)
        upad[s + pad:s + 2 * pad, :] = jnp.zeros((pad, D_CONV), F32)

        def fill(i, carry):
            rows = pl.ds(pl.multiple_of(i * CONV_TILE, CONV_TILE), CONV_TILE)
            upad[pl.ds(pl.multiple_of(i * CONV_TILE + pad, pad), CONV_TILE), :] = a_ref[rows, :] * _sigmoid(g_ref[rows, :])
            return carry

        lax.fori_loop(0, nt, fill, 0)

        def tile(i, carry):
            r0 = pl.multiple_of(i * CONV_TILE, CONV_TILE)
            win[...] = upad[pl.ds(r0, CONV_TILE + 2 * pad), :]
            acc = jnp.zeros((CONV_TILE, D_CONV), F32)
            for j in range(CONV_W):
                acc = acc + win[j + 1:j + 1 + CONV_TILE, :] * w_ref[j:j + 1, :]
            c = acc + dwb_ref[...]
            c_ref[pl.ds(r0, CONV_TILE), :] = c
            mu = jnp.mean(c, axis=-1, keepdims=True)
            xc = c - mu
            rstd = lax.rsqrt(jnp.mean(xc * xc, axis=-1, keepdims=True) + LN_EPS)
            n = xc * rstd * lnw_ref[...] + lnb_ref[...]
            y_ref[pl.ds(r0, CONV_TILE), :] = (jnp.dot(_silu(n).astype(BF16), pw_ref[...].astype(BF16),
                                                      preferred_element_type=F32) + pwb_ref[...])
            return carry

        lax.fori_loop(0, nt, tile, 0)

    vec = _full((1, D_CONV))
    return pl.pallas_call(
        body, name=name, grid=(b,),
        in_specs=[pl.BlockSpec((s, D_CONV), lambda bi: (bi, COL_CA)), pl.BlockSpec((s, D_CONV), lambda bi: (bi, COL_CB)),
                  _full((CONV_W + 1, D_CONV)), vec, vec, vec, _full((D_CONV, D_CONV)), vec],
        out_specs=[pl.BlockSpec((s, D_CONV), lambda bi: (bi, 0))] * 2, out_shape=[_sds((t, D_CONV), F32)] * 2,
        scratch_shapes=[pltpu.VMEM((_conv_rows(s), D_CONV), F32), pltpu.VMEM((CONV_TILE + 2 * pad, D_CONV), F32)],
        compiler_params=_params(("parallel",)),
    )(proj, proj, dw_w, dw_b, ln_w, ln_b, pw_w, pw_b)


def _conv_bwd(name, proj, conv_out, dw_w, ln_w, ln_b, pw_w, dy, b, s):
    t = proj.shape[0]
    pad = CONV_PAD + 1
    nt = s // CONV_TILE

    def body(a_ref, g_ref, c_ref, w_ref, lnw_ref, lnb_ref, pw_ref, dy_ref, dab_ref, dpw_ref, ddw_ref, dvec_ref,
             upad, dcpad, tap_acc, win, dwin):
        @pl.when(pl.program_id(0) == 0)
        def _():
            dpw_ref[...] = jnp.zeros_like(dpw_ref)
            ddw_ref[...] = jnp.zeros_like(ddw_ref)
            dvec_ref[...] = jnp.zeros_like(dvec_ref)

        zeros = jnp.zeros((pad, D_CONV), F32)
        upad[0:pad, :] = zeros
        upad[s + pad:s + 2 * pad, :] = zeros
        dcpad[0:pad, :] = zeros
        dcpad[s + pad:s + 2 * pad, :] = zeros
        tap_acc[...] = jnp.zeros_like(tap_acc)

        def inner(i):
            return pl.ds(pl.multiple_of(i * CONV_TILE + pad, pad), CONV_TILE)

        def fill(i, carry):
            rows = pl.ds(pl.multiple_of(i * CONV_TILE, CONV_TILE), CONV_TILE)
            upad[inner(i), :] = a_ref[rows, :] * _sigmoid(g_ref[rows, :])
            return carry

        lax.fori_loop(0, nt, fill, 0)

        def tile_a(i, carry):
            r0 = pl.multiple_of(i * CONV_TILE, CONV_TILE)
            c = c_ref[pl.ds(r0, CONV_TILE), :]
            mu = jnp.mean(c, axis=-1, keepdims=True)
            xc = c - mu
            rstd = lax.rsqrt(jnp.mean(xc * xc, axis=-1, keepdims=True) + LN_EPS)
            xhat = xc * rstd
            n = xhat * lnw_ref[...] + lnb_ref[...]
            dyt = dy_ref[pl.ds(r0, CONV_TILE), :]
            dyb = dyt.astype(BF16)
            dpw_ref[...] += _dot_tn(_silu(n).astype(BF16), dyb)
            dn = _dot_nt(dyb, pw_ref[...].astype(BF16)) * _dsilu(n)
            dxh = dn * lnw_ref[...]
            dc = rstd * (dxh - jnp.mean(dxh, axis=-1, keepdims=True)
                         - xhat * jnp.mean(dxh * xhat, axis=-1, keepdims=True))
            dcpad[inner(i), :] = dc
            dvec_ref[0:1, :] += jnp.sum(dyt, axis=0, keepdims=True)
            dvec_ref[1:2, :] += jnp.sum(dn * xhat, axis=0, keepdims=True)
            dvec_ref[2:3, :] += jnp.sum(dn, axis=0, keepdims=True)
            dvec_ref[3:4, :] += jnp.sum(dc, axis=0, keepdims=True)
            return carry

        lax.fori_loop(0, nt, tile_a, 0)

        def tile_b(i, carry):
            r0 = pl.multiple_of(i * CONV_TILE, CONV_TILE)
            win[...] = upad[pl.ds(r0, CONV_TILE + 2 * pad), :]
            dwin[...] = dcpad[pl.ds(r0, CONV_TILE + 2 * pad), :]
            dct = dwin[pad:pad + CONV_TILE, :]
            du = jnp.zeros((CONV_TILE, D_CONV), F32)
            for j in range(CONV_W):
                du = du + dwin[2 * pad - 1 - j:2 * pad - 1 - j + CONV_TILE, :] * w_ref[j:j + 1, :]
                tap_acc[8 * j:8 * j + 8, :] += _rowgroups(dct * win[j + 1:j + 1 + CONV_TILE, :])
            rows = pl.ds(r0, CONV_TILE)
            sg = _sigmoid(g_ref[rows, :])
            dab_ref[rows, 0:D_CONV] = (du * sg).astype(BF16)
            dab_ref[rows, D_CONV:2 * D_CONV] = (du * a_ref[rows, :] * sg * (1.0 - sg)).astype(BF16)
            return carry

        lax.fori_loop(0, nt, tile_b, 0)
        for j in range(CONV_W):
            ddw_ref[j:j + 1, :] += jnp.sum(tap_acc[8 * j:8 * j + 8, :], axis=0, keepdims=True)

    vec = _full((1, D_CONV))
    return pl.pallas_call(
        body, name=name, grid=(b,),
        in_specs=[pl.BlockSpec((s, D_CONV), lambda bi: (bi, COL_CA)), pl.BlockSpec((s, D_CONV), lambda bi: (bi, COL_CB)),
                  pl.BlockSpec((s, D_CONV), lambda bi: (bi, 0)),
                  _full((CONV_W + 1, D_CONV)), vec, vec, _full((D_CONV, D_CONV)),
                  pl.BlockSpec((s, D_CONV), lambda bi: (bi, 0))],
        out_specs=[pl.BlockSpec((s, 2 * D_CONV), lambda bi: (bi, 0)), _full((D_CONV, D_CONV)),
                   _full((CONV_W + 1, D_CONV)), _full((8, D_CONV))],
        out_shape=[_sds((t, 2 * D_CONV), BF16), _sds((D_CONV, D_CONV), F32), _sds((CONV_W + 1, D_CONV), F32),
                   _sds((8, D_CONV), F32)],
        scratch_shapes=[pltpu.VMEM((_conv_rows(s), D_CONV), F32), pltpu.VMEM((_conv_rows(s), D_CONV), F32),
                        pltpu.VMEM((8 * CONV_W, D_CONV), F32), pltpu.VMEM((CONV_TILE + 2 * pad, D_CONV), F32),
                        pltpu.VMEM((CONV_TILE + 2 * pad, D_CONV), F32)],
        compiler_params=_params(("arbitrary",)),
    )(proj, proj, conv_out, dw_w, ln_w, ln_b, pw_w, dy)


def _mix_fwd(name, y_attn, o_fw, o_bw, proj, y_conv, aw, gw, cw, seg):
    t = y_attn.shape[0]
    tm = _row_tile(t)

    def body(ya_ref, of_ref, ob_ref, hg_ref, yc_ref, aw_ref, gw_ref, cw_ref, seg_ref, o_ref):
        ya = ya_ref[...]
        ra = lax.rsqrt(jnp.mean(ya * ya, axis=-1, keepdims=True) + EPS)
        o_ref[:, 0:D_ATTN] = (ya * ra * aw_ref[...]).astype(BF16)
        o = of_ref[...] + ob_ref[...]
        ro = lax.rsqrt(jnp.dot((o * o).astype(BF16), seg_ref[...], preferred_element_type=F32) + EPS)
        o_ref[:, D_ATTN:D_ATTN + D_HGRN] = (o * ro * gw_ref[...] * _silu(hg_ref[...])).astype(BF16)
        yc = yc_ref[...]
        rc = lax.rsqrt(jnp.mean(yc * yc, axis=-1, keepdims=True) + EPS)
        o_ref[:, D_ATTN + D_HGRN:D_MODEL] = (yc * rc * cw_ref[...]).astype(BF16)

    def tile(w, c=0):
        return pl.BlockSpec((tm, w), lambda i: (i, c))

    return pl.pallas_call(
        body, name=name, grid=(t // tm,),
        in_specs=[tile(D_ATTN), tile(D_HGRN), tile(D_HGRN), tile(D_HGRN, COL_HG), tile(D_CONV),
                  _full((1, D_ATTN)), _full((1, D_HGRN)), _full((1, D_CONV)), _full((D_HGRN, D_HGRN))],
        out_specs=tile(D_MODEL), out_shape=_sds((t, D_MODEL), BF16),
        compiler_params=_params(("parallel",)),
    )(y_attn, o_fw, o_bw, proj, y_conv, aw, gw, cw, seg)


def _mix_bwd(name, dmix, y_attn, o_fw, o_bw, proj, y_conv, aw, gw, cw, seg, deps=()):
    t = y_attn.shape[0]
    tm = _row_tile(t)

    def rms_bwd(x, w, dy):
        r = lax.rsqrt(jnp.mean(x * x, axis=-1, keepdims=True) + EPS)
        gwv = dy * w
        return r * gwv - x * (r * r * r) * jnp.mean(gwv * x, axis=-1, keepdims=True), _rowgroups(dy * x * r)

    def body(dm_ref, ya_ref, of_ref, ob_ref, hg_ref, yc_ref, aw_ref, gw_ref, cw_ref, seg_ref, *rest):
        dya_ref, do_ref, dhg_ref, dyc_ref, daw_ref, dgw_ref, dcw_ref = rest[-7:]

        @pl.when(pl.program_id(0) == 0)
        def _():
            daw_ref[...] = jnp.zeros_like(daw_ref)
            dgw_ref[...] = jnp.zeros_like(dgw_ref)
            dcw_ref[...] = jnp.zeros_like(dcw_ref)

        dya, daw = rms_bwd(ya_ref[...], aw_ref[...], dm_ref[:, 0:D_ATTN])
        dya_ref[...] = dya
        daw_ref[...] += daw
        dyc, dcw = rms_bwd(yc_ref[...], cw_ref[...], dm_ref[:, D_ATTN + D_HGRN:D_MODEL])
        dyc_ref[...] = dyc
        dcw_ref[...] += dcw
        d2 = dm_ref[:, D_ATTN:D_ATTN + D_HGRN]
        o = of_ref[...] + ob_ref[...]
        hg = hg_ref[...]
        ro = lax.rsqrt(jnp.dot((o * o).astype(BF16), seg_ref[...], preferred_element_type=F32) + EPS)
        dn = d2 * _silu(hg)
        dhg_ref[...] = (d2 * o * ro * gw_ref[...] * _dsilu(hg)).astype(BF16)
        gwv = dn * gw_ref[...]
        do_ref[...] = ro * gwv - o * (ro * ro * ro) * _rdot2(gwv * o, seg_ref[...])
        dgw_ref[...] += _rowgroups(dn * o * ro)

    def tile(w, c=0):
        return pl.BlockSpec((tm, w), lambda i: (i, c))

    return pl.pallas_call(
        body, name=name, grid=(t // tm,),
        in_specs=[tile(D_MODEL), tile(D_ATTN), tile(D_HGRN), tile(D_HGRN), tile(D_HGRN, COL_HG), tile(D_CONV),
                  _full((1, D_ATTN)), _full((1, D_HGRN)), _full((1, D_CONV)), _full((D_HGRN, D_HGRN))]
        + [_full(a.shape) for a in deps],
        out_specs=[tile(D_ATTN), tile(D_HGRN), tile(D_HGRN), tile(D_CONV),
                   _full((8, D_ATTN)), _full((8, D_HGRN)), _full((8, D_CONV))],
        out_shape=[_sds((t, D_ATTN), F32), _sds((t, D_HGRN), F32), _sds((t, D_HGRN), BF16), _sds((t, D_CONV), F32),
                   _sds((8, D_ATTN), F32), _sds((8, D_HGRN), F32), _sds((8, D_CONV), F32)],
        compiler_params=_params(("arbitrary",)),
    )(dmix, y_attn, o_fw, o_bw, proj, y_conv, aw, gw, cw, seg, *deps)


def _dproj(name, dp_attn, dq_f, dq_b, dz_fw, dz_bw, dv_f, dv_b, dhg, dp_conv):
    t = dq_f.shape[0]
    tm = _row_tile(t)
    wa, wc = dp_attn.shape[1], dp_conv.shape[1]

    def body(at_ref, qf_ref, qb_ref, zf_ref, zb_ref, vf_ref, vb_ref, hg_ref, cv_ref, o_ref):
        o_ref[:, 0:wa] = at_ref[...]
        cols = (qf_ref[...] + qb_ref[...], zf_ref[...], zb_ref[...], vf_ref[...] + vb_ref[...], hg_ref[...])
        for i, val in enumerate(cols):
            o_ref[:, wa + i * D_HGRN:wa + (i + 1) * D_HGRN] = val.astype(BF16)
        o_ref[:, wa + 5 * D_HGRN:D_IN] = cv_ref[...]

    tile = lambda w: pl.BlockSpec((tm, w), lambda i: (i, 0))
    return pl.pallas_call(
        body, name=name, grid=(t // tm,), in_specs=[tile(wa)] + [tile(D_HGRN)] * 7 + [tile(wc)],
        out_specs=tile(D_IN), out_shape=_sds((t, D_IN), BF16), compiler_params=_params(("parallel",)),
    )(dp_attn, dq_f, dq_b, dz_fw, dz_bw, dv_f, dv_b, dhg, dp_conv)


def _mm_tile(t):
    return min(512, t)


def _resident(shape):
    n = len(shape)
    return pl.BlockSpec(tuple(shape), lambda *_: (0,) * n, pipeline_mode=pl.Buffered(1))


def _w_blk(rows, cols, j_of):
    return pl.BlockSpec((None, rows, cols), lambda *g: (j_of(*g), 0, 0))


def _layer_fwd(l, x, wget, sm, tabs, cst, b, s, deps, target=None):
    t = x.shape[0]
    tm = _mm_tile(t)
    nt = t // tm
    pre = "l%d_" % l
    row = lambda w: pl.BlockSpec((tm, w), lambda i, *_: (i, 0))

    def normed(x_ref, nw_ref):
        xv = x_ref[...]
        r = lax.rsqrt(jnp.mean(xv * xv, axis=-1, keepdims=True) + EPS)
        return (xv * r * nw_ref[...]).astype(BF16)

    def in_body(x_ref, nw_ref, w_ref, *rest):
        o_ref, h_ref = rest[-2:]
        hv = normed(x_ref, nw_ref)
        h_ref[...] = hv
        for j in range(N_CHIP):
            o_ref[:, j * IN_BLK:(j + 1) * IN_BLK] = jnp.dot(hv, w_ref[j], preferred_element_type=F32)

    w_in = wget(l, "w_in", x)
    proj, h1 = pl.pallas_call(
        in_body, name=pre + "in_proj", grid=(nt,),
        in_specs=[row(D_MODEL), _full((1, D_MODEL)), _resident(w_in.shape)] + [_full(a.shape) for a in deps],
        out_specs=[row(D_IN), row(D_MODEL)], out_shape=[_sds((t, D_IN), F32), _sds((t, D_MODEL), BF16)],
        compiler_params=_params(("parallel",)),
    )(x, sm["mix_norm_w"][l], w_in, *deps)
    qn, kr, vr = _attn_prep(pre + "attn_prep", proj, s, tabs, sm["q_norm_w"][l], sm["k_norm_w"][l], cst["attn"])
    y_attn = _attn_fwd(pre + "attn", qn, kr, vr, b, s)
    o_fw, o_bw, st_fw, st_bw = _hgrn_fwd2(pre + "hgrn", proj, sm["lb"][l][0], sm["lb"][l][1], b, s, cst["hg_fw"],
                                          cst["hg_bw"])
    y_conv, conv_out = _conv_fwd(pre + "conv", proj, sm["conv_dw_w"][l], sm["conv_dw_b"][l], sm["conv_ln_w"][l],
                       sm["conv_ln_b"][l], sm["conv_pw_w"][l], sm["conv_pw_b"][l], b, s)
    mixed = _mix_fwd(pre + "mix", y_attn, o_fw, o_bw, proj, y_conv, sm["attn_out_norm_w"][l], sm["gnorm_w"][l],
                     sm["conv_out_norm_w"][l], cst["seg_h"])
    (x1,) = _mm(pre + "out_proj", (nt,),
                [(mixed, row(D_MODEL), wget(l, "w_out", mixed),
                  pl.BlockSpec((N_CHIP, OUT_BLK, D_MODEL), lambda i: (0, 0, 0)), NN)],
                [(x, row(D_MODEL))], [(_sds((t, D_MODEL), F32), row(D_MODEL))],
                lambda tot, xr: (xr + tot,))
    ff3 = pl.BlockSpec((N_CHIP, tm, FF_BLK), lambda i: (0, i, 0))
    ffs = _sds((N_CHIP, t, FF_BLK), BF16)

    def gu_body(x_ref, nw_ref, wg_ref, wu_ref, g_ref, u_ref, a_ref, h_ref):
        hv = normed(x_ref, nw_ref)
        h_ref[...] = hv
        for j in range(N_CHIP):
            gv = jnp.dot(hv, wg_ref[j], preferred_element_type=F32)
            uv = jnp.dot(hv, wu_ref[j], preferred_element_type=F32)
            g_ref[j] = gv.astype(BF16)
            u_ref[j] = uv.astype(BF16)
            a_ref[j] = (_silu(gv) * uv).astype(BF16)

    w_gate, w_up = wget(l, "w_gate", x1), wget(l, "w_up", x1)
    gate, up, act, h2 = pl.pallas_call(
        gu_body, name=pre + "ffn_gate_up", grid=(nt,),
        in_specs=[row(D_MODEL), _full((1, D_MODEL)), _resident(w_gate.shape), _resident(w_up.shape)],
        out_specs=[ff3, ff3, ff3, row(D_MODEL)], out_shape=[ffs, ffs, ffs, _sds((t, D_MODEL), BF16)],
        compiler_params=_params(("parallel",)),
    )(x1, sm["ffn_norm_w"][l], w_gate, w_up)

    def down_body(a_ref, w_ref, x_ref, o_ref):
        tot = x_ref[...]
        for j in range(N_CHIP):
            tot = tot + jnp.dot(a_ref[j], w_ref[j], preferred_element_type=F32)
        o_ref[...] = tot

    def down_loss_body(a_ref, w_ref, x_ref, t_ref, dy_ref, acc_ref):
        tot = x_ref[...]
        for j in range(N_CHIP):
            tot = tot + jnp.dot(a_ref[j], w_ref[j], preferred_element_type=F32)
        e = tot - t_ref[...]
        dy_ref[...] = e * (1.0 / D_MODEL)

        @pl.when(pl.program_id(0) == 0)
        def _():
            acc_ref[...] = jnp.zeros_like(acc_ref)

        acc_ref[...] += _rowgroups(e * e)

    w_down = wget(l, "w_down", act)
    if target is None:
        x2 = pl.pallas_call(
            down_body, name=pre + "ffn_down", grid=(nt,), in_specs=[ff3, _resident(w_down.shape), row(D_MODEL)],
            out_specs=row(D_MODEL), out_shape=_sds((t, D_MODEL), F32), compiler_params=_params(("parallel",)),
        )(act, w_down, x1)
    else:
        x2 = pl.pallas_call(
            down_loss_body, name=pre + "ffn_down_loss", grid=(nt,),
            in_specs=[ff3, _resident(w_down.shape), row(D_MODEL), row(D_MODEL)],
            out_specs=[row(D_MODEL), _full((8, D_MODEL))],
            out_shape=[_sds((t, D_MODEL), F32), _sds((8, D_MODEL), F32)], compiler_params=_params(("arbitrary",)),
        )(act, w_down, x1, target)
    saved = dict(x=x, h1=h1, proj=proj, qn=qn, kr=kr, vr=vr, y_attn=y_attn, o_fw=o_fw, o_bw=o_bw, st_fw=st_fw,
                 st_bw=st_bw, y_conv=y_conv, conv_out=conv_out, mixed=mixed, x1=x1, h2=h2, gate=gate, up=up, act=act)
    return x2, saved


def _layer_bwd(l, dx2, sv, wget, sm, tabs, cst, b, s, on_grads):
    t = dx2.shape[0]
    tm = _mm_tile(t)
    nt = t // tm
    pre = "l%d_" % l
    tk = min(2048, t)
    nk = t // tk
    row = lambda w: pl.BlockSpec((tm, w), lambda i, *_: (i, 0))
    ff3 = pl.BlockSpec((N_CHIP, tm, FF_BLK), lambda i: (0, i, 0))
    ffs = _sds((N_CHIP, t, FF_BLK), BF16)

    w_down, w_gate, w_up = wget(l, "w_down", dx2), wget(l, "w_gate", dx2), wget(l, "w_up", dx2)

    def ddx_body(dx_ref, w_ref, g_ref, u_ref, dg_ref, du_ref):
        dxb = dx_ref[...].astype(BF16)
        for j in range(N_CHIP):
            da = _dot_nt(dxb, w_ref[j])
            g = g_ref[j].astype(F32)
            sg = _sigmoid(g)
            dg_ref[j] = (da * u_ref[j].astype(F32) * (sg * (1.0 + g * (1.0 - sg)))).astype(BF16)
            du_ref[j] = (da * (g * sg)).astype(BF16)

    dgate, dup = pl.pallas_call(
        ddx_body, name=pre + "ffn_down_dx", grid=(nt,), in_specs=[row(D_MODEL), _resident(w_down.shape), ff3, ff3],
        out_specs=[ff3, ff3], out_shape=[ffs, ffs], compiler_params=_params(("parallel",)),
    )(dx2, w_down, sv["gate"], sv["up"])
    colt = lambda w: pl.BlockSpec((tk, w), lambda j, k: (k, 0))
    fft = pl.BlockSpec((None, tk, FF_BLK), lambda j, k: (j, k, 0))
    (g_down,) = _mm(pre + "ffn_down_dw", (N_CHIP, nk), [(sv["act"], fft, dx2, colt(D_MODEL), TN)], [],
                    [(_sds((N_CHIP, FF_BLK, D_MODEL), BF16), pl.BlockSpec((None, FF_BLK, D_MODEL), lambda j, k: (j, 0, 0)))],
                    lambda tot: (tot,), acc=(1, (FF_BLK, D_MODEL)))
    wff = pl.BlockSpec((None, D_MODEL, FF_BLK), lambda j, k: (j, 0, 0))
    (g_gate,) = _mm(pre + "ffn_gate_dw", (N_CHIP, nk), [(sv["h2"], colt(D_MODEL), dgate, fft, TN)], [],
                    [(_sds((N_CHIP, D_MODEL, FF_BLK), BF16), wff)], lambda tot: (tot,), acc=(1, (D_MODEL, FF_BLK)))
    (g_up,) = _mm(pre + "ffn_up_dw", (N_CHIP, nk), [(sv["h2"], colt(D_MODEL), dup, fft, TN)], [],
                  [(_sds((N_CHIP, D_MODEL, FF_BLK), BF16), wff)], lambda tot: (tot,), acc=(1, (D_MODEL, FF_BLK)))

    def norm_bwd_tail(dh, x_ref, nw_ref, dres_ref, dx_ref, dw_ref):
        xv = x_ref[...]
        r = lax.rsqrt(jnp.mean(xv * xv, axis=-1, keepdims=True) + EPS)
        gw = dh * nw_ref[...]
        dx_ref[...] = dres_ref[...] + r * gw - xv * (r * r * r) * jnp.mean(gw * xv, axis=-1, keepdims=True)

        @pl.when(pl.program_id(0) == 0)
        def _():
            dw_ref[...] = jnp.zeros_like(dw_ref)

        dw_ref[...] += _rowgroups(dh * xv * r)

    def dh_body(dg_ref, du_ref, wg_ref, wu_ref, x_ref, nw_ref, dres_ref, *rest):
        tot = None
        for j in range(N_CHIP):
            r = _dot_nt(dg_ref[j], wg_ref[j]) + _dot_nt(du_ref[j], wu_ref[j])
            tot = r if tot is None else tot + r
        norm_bwd_tail(tot, x_ref, nw_ref, dres_ref, *rest[-2:])

    deps = on_grads(l, dict(w_gate=g_gate, w_up=g_up, w_down=g_down))
    dx1, d_ffn_norm = pl.pallas_call(
        dh_body, name=pre + "ffn_dh", grid=(nt,),
        in_specs=[ff3, ff3, _resident(w_gate.shape), _resident(w_up.shape), row(D_MODEL), _full((1, D_MODEL)),
                  row(D_MODEL)] + [_full(a.shape) for a in deps],
        out_specs=[row(D_MODEL), _full((8, D_MODEL))], out_shape=[_sds((t, D_MODEL), F32), _sds((8, D_MODEL), F32)],
        compiler_params=_params(("arbitrary",)),
    )(dgate, dup, w_gate, w_up, sv["x1"], sm["ffn_norm_w"][l], dx2, *deps)

    (dmix,) = _mm(pre + "out_proj_dx", (nt,),
                  [(dx1, row(D_MODEL), wget(l, "w_out", dx2),
                    pl.BlockSpec((N_CHIP, OUT_BLK, D_MODEL), lambda i: (0, 0, 0)), NT)], [],
                  [(_sds((t, D_MODEL), F32), row(D_MODEL))], lambda tot: (tot,))
    (g_out,) = _mm(pre + "out_proj_dw", (N_CHIP, nk),
                   [(sv["mixed"], pl.BlockSpec((tk, OUT_BLK), lambda j, k: (k, j)), dx1, colt(D_MODEL), TN)], [],
                   [(_sds((N_CHIP, OUT_BLK, D_MODEL), BF16), pl.BlockSpec((None, OUT_BLK, D_MODEL), lambda j, k: (j, 0, 0)))],
                   lambda tot: (tot,), acc=(1, (OUT_BLK, D_MODEL)))
    proj = sv["proj"]
    dya, do_h, dhg, dyc, d_aw, d_gw, d_cw = _mix_bwd(
        pre + "mix_bwd", dmix, sv["y_attn"], sv["o_fw"], sv["o_bw"], proj, sv["y_conv"],
        sm["attn_out_norm_w"][l], sm["gnorm_w"][l], sm["conv_out_norm_w"][l], cst["seg_h"],
        on_grads(l, dict(w_out=g_out)))
    dqs, dkr, dvr = _attn_bwd(pre + "attn_bwd", sv["qn"], sv["kr"], sv["vr"], dya, b, s)
    dp_attn, d_qw, d_kw = _attn_prep_bwd(pre + "attn_prep_bwd", proj, s, tabs, sm["q_norm_w"][l], sm["k_norm_w"][l],
                                         cst["attn"], dqs, dkr, dvr)
    dq_f, dv_f, dz_fw, dq_b, dv_b, dz_bw, dlb_fw, dlb_bw = _hgrn_bwd2(
        pre + "hgrn_bwd", proj, sm["lb"][l][0], sm["lb"][l][1], sv["st_fw"], sv["st_bw"], do_h, b, s,
        cst["hg_fw"], cst["hg_bw"])
    dp_conv, d_pw, d_dw, d_cvec = _conv_bwd(pre + "conv_bwd", proj, sv["conv_out"], sm["conv_dw_w"][l],
                                            sm["conv_ln_w"][l], sm["conv_ln_b"][l], sm["conv_pw_w"][l], dyc, b, s)
    dproj = _dproj(pre + "dproj", dp_attn, dq_f, dq_b, dz_fw, dz_bw, dv_f, dv_b, dhg, dp_conv)
    g_pw = d_pw.reshape(N_CHIP, D_CONV // N_CHIP, D_CONV).astype(BF16)

    (g_in,) = _mm(pre + "in_proj_dw", (N_CHIP, nk),
                  [(sv["h1"], colt(D_MODEL), dproj, pl.BlockSpec((tk, IN_BLK), lambda j, k: (k, j)), TN)], [],
                  [(_sds((N_CHIP, D_MODEL, IN_BLK), BF16), pl.BlockSpec((None, D_MODEL, IN_BLK), lambda j, k: (j, 0, 0)))],
                  lambda tot: (tot,), acc=(1, (D_MODEL, IN_BLK)))

    def indx_body(dp_ref, w_ref, x_ref, nw_ref, dres_ref, *rest):
        tot = None
        for j in range(N_CHIP):
            r = _dot_nt(dp_ref[:, j * IN_BLK:(j + 1) * IN_BLK], w_ref[j])
            tot = r if tot is None else tot + r
        norm_bwd_tail(tot, x_ref, nw_ref, dres_ref, *rest[-2:])

    w_in = wget(l, "w_in", dx2)
    deps = on_grads(l, dict(w_in=g_in, conv_pw_w=g_pw))
    dx, d_mix_norm = pl.pallas_call(
        indx_body, name=pre + "in_proj_dx", grid=(nt,),
        in_specs=[row(D_IN), _resident(w_in.shape), row(D_MODEL), _full((1, D_MODEL)), row(D_MODEL)]
        + [_full(a.shape) for a in deps],
        out_specs=[row(D_MODEL), _full((8, D_MODEL))], out_shape=[_sds((t, D_MODEL), F32), _sds((8, D_MODEL), F32)],
        compiler_params=_params(("arbitrary",)),
    )(dproj, w_in, sv["x"], sm["mix_norm_w"][l], dx1, *deps)
    heads = lambda v, n: v.sum(axis=0).reshape(n, HEAD_DIM).sum(axis=0)
    small = dict(
        mix_norm_w=d_mix_norm.sum(axis=0), q_norm_w=heads(d_qw, D_ATTN // HEAD_DIM), k_norm_w=heads(d_kw, N_KV),
        lb=jnp.stack([dlb_fw.sum(axis=0), dlb_bw.sum(axis=0)]), hgrn_gnorm_w=heads(d_gw, D_HGRN // HEAD_DIM),
        conv_dw_w=d_dw[:CONV_W], conv_dw_b=d_cvec[3], conv_ln_w=d_cvec[1], conv_ln_b=d_cvec[2],
        conv_pw_b=d_cvec[0], attn_out_norm_w=d_aw.sum(axis=0), conv_out_norm_w=d_cw.sum(axis=0),
        ffn_norm_w=d_ffn_norm.sum(axis=0))
    return dx, small


SMALL_ORDER = ("mix_norm_w", "q_norm_w", "k_norm_w", "lb", "hgrn_gnorm_w", "conv_dw_w", "conv_dw_b", "conv_ln_w",
               "conv_ln_b", "conv_pw_b", "attn_out_norm_w", "conv_out_norm_w", "ffn_norm_w")
BIG_ORDER = ("w_in", "w_out", "w_gate", "w_up", "w_down")
SCATTER_ORDER = BIG_ORDER + ("conv_pw_w",)


def _local_step(x, target, wget, sm, deps, on_grads):
    b, s, d = x.shape
    t = b * s
    cos, sin = _rope_tables(s)
    tabs = dict(cq=jnp.tile(cos, (1, D_ATTN // HEAD_DIM)), sq=jnp.tile(sin, (1, D_ATTN // HEAD_DIM)),
                ck=jnp.tile(cos, (1, N_KV)), sk=jnp.tile(sin, (1, N_KV)))
    cst = dict(attn=_attn_consts(), hg_fw=_hgrn_consts(False), hg_bw=_hgrn_consts(True),
               seg_h=_bf(_seg_matrix(D_HGRN, HEAD_DIM, 1.0 / HEAD_DIM)))
    vec = lambda a: a.reshape(DEPTH, 1, -1)
    smk = dict(sm)
    for n in ("mix_norm_w", "conv_dw_b", "conv_ln_w", "conv_ln_b", "conv_pw_b", "attn_out_norm_w", "conv_out_norm_w",
              "ffn_norm_w"):
        smk[n] = vec(sm[n])
    smk["q_norm_w"] = vec(jnp.tile(sm["q_norm_w"], (1, D_ATTN // HEAD_DIM)))
    smk["k_norm_w"] = vec(jnp.tile(sm["k_norm_w"], (1, N_KV)))
    smk["gnorm_w"] = vec(jnp.tile(sm["hgrn_gnorm_w"], (1, D_HGRN // HEAD_DIM)))
    smk["lb"] = sm["lb"].reshape(DEPTH, 2, 1, D_HGRN)
    smk["conv_dw_w"] = jnp.pad(sm["conv_dw_w"], ((0, 0), (0, 1), (0, 0)))

    h = x.reshape(t, d)
    saved = []
    for l in range(DEPTH):
        h, sv = _layer_fwd(l, h, wget, smk, tabs, cst, b, s, deps if l == 0 else (),
                           target.reshape(t, d) if l == DEPTH - 1 else None)
        saved.append(sv)
    dy, sq = h
    sq_sum = jnp.sum(sq)
    dh = dy
    smalls = [None] * DEPTH
    for l in reversed(range(DEPTH)):
        dh, smalls[l] = _layer_bwd(l, dh, saved[l], wget, smk, tabs, cst, b, s, on_grads)
    return sq_sum, dh.reshape(b, s, d), smalls


HBM_SPEC = pl.BlockSpec(memory_space=pltpu.HBM)


def _exchange(name, arrs, mode):
    n = len(arrs)
    if mode == "gather8":
        flips = [(fx, fy, fc) for fx in (0, 1) for fy in (0, 1) for fc in (0, 1)][1:]
    elif mode == "sibling":
        flips = [(0, 0, 1)]
    else:
        flips = [(1, 0, 0), (0, 1, 0), (1, 1, 0)]
    n_f = len(flips)

    def body(*refs):
        ins, outs = refs[:n], refs[n:2 * n]
        send_sems, recv_sems, local_sems = refs[2 * n:]
        x, y, c = lax.axis_index("x"), lax.axis_index("y"), lax.axis_index("c")

        def slot_of(px, py, pc):
            return (2 * px + py) if mode != "gather8" else (4 * px + 2 * py + pc)

        me = slot_of(x, y, c)
        started = []
        for i in range(n):
            if mode != "sibling":
                src = ins[i].at[me] if mode == "scatter4" else ins[i]
                loc = pltpu.make_async_copy(src, outs[i].at[me], local_sems.at[i])
                loc.start()
                started.append(loc)
        sends, recvs = [], []
        for i in range(n):
            for f, (fx, fy, fc) in enumerate(flips):
                peer = (x ^ fx, y ^ fy, c ^ fc)
                ps = slot_of(*peer)
                if mode == "sibling":
                    src, dst, landed = ins[i], outs[i], outs[i]
                elif mode == "scatter4":
                    src, dst, landed = ins[i].at[ps], outs[i].at[me], outs[i].at[ps]
                else:
                    src, dst, landed = ins[i], outs[i].at[me], outs[i].at[ps]
                k = i * n_f + f
                cp = pltpu.make_async_remote_copy(src_ref=src, dst_ref=dst, send_sem=send_sems.at[k],
                                                  recv_sem=recv_sems.at[k], device_id=peer,
                                                  device_id_type=pl.DeviceIdType.MESH)
                cp.start()
                sends.append(cp)
                recvs.append(pltpu.make_async_remote_copy(src_ref=src, dst_ref=landed, send_sem=send_sems.at[k],
                                                          recv_sem=recv_sems.at[k], device_id=peer,
                                                          device_id_type=pl.DeviceIdType.MESH))
        for cp in sends:
            cp.wait_send()
        for cp in recvs:
            cp.wait_recv()
        for loc in started:
            loc.wait()

    def out_sds(a):
        if mode == "gather4":
            return _sds((N_CHIP,) + a.shape, a.dtype)
        if mode == "gather8":
            return _sds((N_DEV,) + a.shape, a.dtype)
        return _sds(a.shape, a.dtype)

    res = pl.pallas_call(
        body, name=name, in_specs=[HBM_SPEC] * n, out_specs=[HBM_SPEC] * n, out_shape=[out_sds(a) for a in arrs],
        scratch_shapes=[pltpu.SemaphoreType.DMA((n * n_f,)), pltpu.SemaphoreType.DMA((n * n_f,)),
                        pltpu.SemaphoreType.DMA((max(n, 1),))],
    )(*arrs)
    return list(res)


SEM_SPEC = pl.BlockSpec(memory_space=pltpu.SEMAPHORE)
SPLIT_EFFECT = pltpu.SideEffectType.DATAFLOW_SIDE_EFFECTING
CHIP_FLIPS = ((1, 0), (0, 1), (1, 1))


def _chip_copies(src_refs, land_refs, send_sems, recv_sems, scatter):
    x, y, c = lax.axis_index("x"), lax.axis_index("y"), lax.axis_index("c")
    me = 2 * x + y
    out = []
    for i, land in enumerate(land_refs):
        if scatter == "sibling":
            kw = dict(send_sem=send_sems.at[i], recv_sem=recv_sems.at[i], device_id=(x, y, 1 - c),
                      device_id_type=pl.DeviceIdType.MESH)
            cp = pltpu.make_async_remote_copy(src_ref=src_refs[i], dst_ref=land, **kw)
            out.append((cp, cp))
            continue
        for f, (fx, fy) in enumerate(CHIP_FLIPS):
            peer = (x ^ fx, y ^ fy, c)
            ps = 2 * (x ^ fx) + (y ^ fy)
            src = src_refs[i].at[ps] if scatter else land.at[me]
            k = i * len(CHIP_FLIPS) + f
            kw = dict(send_sem=send_sems.at[k], recv_sem=recv_sems.at[k], device_id=peer,
                      device_id_type=pl.DeviceIdType.MESH)
            out.append((pltpu.make_async_remote_copy(src_ref=src, dst_ref=land.at[me], **kw),
                        pltpu.make_async_remote_copy(src_ref=src, dst_ref=land.at[ps], **kw)))
    return out


def _split_start(name, srcs, lands, scatter):
    n = len(lands)
    n_src = len(srcs)
    n_sem = n if scatter == "sibling" else n * len(CHIP_FLIPS)

    def body(*refs):
        src_refs = refs[:n_src]
        land_refs = refs[n_src:n_src + n]
        send_sems, recv_sems = refs[n_src + n], refs[n_src + n + 1]
        token = refs[-1]
        for start, _ in _chip_copies(src_refs, land_refs, send_sems, recv_sems, scatter):
            start.start()
        token[...] = jnp.zeros_like(token)

    arrs = list(srcs) + list(lands)
    res = pl.pallas_call(
        body, name=name,
        out_shape=(pltpu.SemaphoreType.DMA((n_sem,)), pltpu.SemaphoreType.DMA((n_sem,)),
                   *[pltpu.HBM(a.shape, a.dtype) for a in arrs], _sds((8, LANES), F32)),
        in_specs=[HBM_SPEC] * len(arrs),
        out_specs=(SEM_SPEC, SEM_SPEC, *[HBM_SPEC] * len(arrs), pl.BlockSpec(memory_space=pltpu.VMEM)),
        input_output_aliases={i: 2 + i for i in range(len(arrs))},
        compiler_params=pltpu.CompilerParams(has_side_effects=SPLIT_EFFECT),
    )(*[pltpu.with_memory_space_constraint(a, pltpu.HBM) for a in arrs])
    return dict(send=res[0], recv=res[1], srcs=list(res[2:2 + n_src]), lands=list(res[2 + n_src:2 + n_src + n]),
                token=res[-1], scatter=scatter)


def _split_wait(name, started, after, with_srcs=False):
    srcs, lands, scatter = started["srcs"], started["lands"], started["scatter"]
    n, n_src = len(lands), len(srcs)

    def body(*refs):
        src_refs = refs[:n_src]
        land_refs = refs[n_src:n_src + n]
        send_sems, recv_sems = refs[n_src + n], refs[n_src + n + 1]
        for _, wait in _chip_copies(src_refs, land_refs, send_sems, recv_sems, scatter):
            wait.wait_send()
            wait.wait_recv()

    arrs = list(srcs) + list(lands)
    res = pl.pallas_call(
        body, name=name, out_shape=tuple(pltpu.HBM(a.shape, a.dtype) for a in arrs),
        in_specs=[HBM_SPEC] * len(arrs) + [SEM_SPEC, SEM_SPEC, pl.BlockSpec(memory_space=pl.ANY)],
        out_specs=tuple([HBM_SPEC] * len(arrs)), input_output_aliases={i: i for i in range(len(arrs))},
        compiler_params=pltpu.CompilerParams(has_side_effects=SPLIT_EFFECT),
    )(*arrs, started["send"], started["recv"], after)
    return (list(res[:n_src]), list(res[n_src:])) if with_srcs else list(res[n_src:])


def _flat_tile(rows):
    for cand in (512, 256, 128, 64, 32, 16, 8):
        if rows % cand == 0:
            return cand
    return rows


def _cast_slot(name, a, l, chip, layers=DEPTH, dtype=BF16):
    r, c = a.shape[0] // layers, a.shape[1]
    tr = _flat_tile(r)

    def body(chip_ref, a_ref, o_ref):
        o_ref[...] = a_ref[...].astype(dtype)

    return pl.pallas_call(
        body, name=name, out_shape=_sds((N_CHIP, r, c), dtype),
        grid_spec=pltpu.PrefetchScalarGridSpec(
            num_scalar_prefetch=1, grid=(r // tr,),
            in_specs=[pl.BlockSpec((tr, c), lambda i, ch: (l * (r // tr) + i, 0))],
            out_specs=pl.BlockSpec((None, tr, c), lambda i, ch: (ch[0], i, 0))),
        compiler_params=_params(("parallel",)))(chip, a)


def _own_slot(name, g, chip):
    n, r, c = g.shape
    tr = _flat_tile(r)

    def body(chip_ref, g_ref, o_ref):
        o_ref[...] = g_ref[...]

    spec = pl.BlockSpec((None, tr, c), lambda i, ch: (ch[0], i, 0))
    return pl.pallas_call(
        body, name=name, out_shape=_sds(g.shape, g.dtype),
        grid_spec=pltpu.PrefetchScalarGridSpec(num_scalar_prefetch=1, grid=(r // tr,), in_specs=[spec], out_specs=spec),
        compiler_params=_params(("parallel",)))(chip, g)


def _sum_layers(name, lands):
    n, r, c = lands[0].shape
    tr = _flat_tile(r)
    nl = len(lands)

    def body(*refs):
        o_ref = refs[-1]
        for k in range(nl):
            @pl.when(pl.program_id(0) == k)
            def _():
                tot = refs[k][0].astype(F32)
                for i in range(1, n):
                    tot = tot + refs[k][i].astype(F32)
                o_ref[...] = tot

    return pl.pallas_call(
        body, name=name, grid=(nl, r // tr),
        in_specs=[pl.BlockSpec((n, tr, c), lambda l, i, k=k: (0, jnp.where(l == k, i, 0), 0)) for k in range(nl)],
        out_specs=pl.BlockSpec((tr, c), lambda l, i: (l * (r // tr) + i, 0)), out_shape=_sds((nl * r, c), F32),
        compiler_params=_params(("arbitrary", "arbitrary")))(*lands)


def _sum_slots(name, a, scale=None):
    n, r, c = a.shape
    tr = _flat_tile(r)

    def body(a_ref, o_ref):
        tot = a_ref[0].astype(F32)
        for i in range(1, n):
            tot = tot + a_ref[i].astype(F32)
        o_ref[...] = tot

    return pl.pallas_call(body, name=name, grid=(r // tr,),
                          in_specs=[pl.BlockSpec((n, tr, c), lambda i: (0, i, 0))],
                          out_specs=pl.BlockSpec((tr, c), lambda i: (i, 0)), out_shape=_sds((r, c), F32),
                          compiler_params=_params(("parallel",)))(a)


def _adamw(name, w, ga, gb, m, v):
    r, c = w.shape
    tr = _flat_tile(r)
    c1 = 1.0 - B1 ** STEP
    c2 = 1.0 - B2 ** STEP
    two = gb is not None

    def body(*refs):
        if two:
            w_ref, ga_ref, gb_ref, m_ref, v_ref, g_out, d_out, m_out, v_out = refs
            g = ga_ref[...] + gb_ref[...]
        else:
            w_ref, ga_ref, m_ref, v_ref, g_out, d_out, m_out, v_out = refs
            g = ga_ref[...]
        mn = B1 * m_ref[...] + (1.0 - B1) * g
        vn = B2 * v_ref[...] + (1.0 - B2) * (g * g)
        g_out[...] = g
        m_out[...] = mn
        v_out[...] = vn
        d_out[...] = -LR * ((mn / c1) / (jnp.sqrt(vn / c2) + ADAM_EPS) + WD * w_ref[...])

    spec = pl.BlockSpec((tr, c), lambda i: (i, 0))
    ins = [w, ga, gb, m, v] if two else [w, ga, m, v]
    return pl.pallas_call(body, name=name, grid=(r // tr,), in_specs=[spec] * len(ins), out_specs=[spec] * 4,
                          out_shape=[_sds((r, c), F32)] * 4, compiler_params=_params(("parallel",)))(*ins)


WEIGHTS = ('mix_norm_w', 'w_in', 'q_norm_w', 'k_norm_w', 'hgrn_lb_logits', 'hgrn_gnorm_w', 'conv_dw_w', 'conv_dw_b',
           'conv_ln_w', 'conv_ln_b', 'conv_pw_w', 'conv_pw_b', 'attn_out_norm_w', 'conv_out_norm_w', 'w_out',
           'ffn_norm_w', 'w_gate', 'w_up', 'w_down')
SHARDED_SMALL = {"hgrn_lb_logits": 2, "conv_dw_w": 2, "conv_pw_w": 1}
LANES = 128
PACK_ROWS = 256


def _pack(parts):
    flat = jnp.concatenate([p.reshape(-1) for p in parts])
    n = flat.shape[0]
    rows = -(-n // (PACK_ROWS * LANES)) * PACK_ROWS
    return jnp.pad(flat, (0, rows * LANES - n)).reshape(rows, LANES)


def _unpack(packed, shapes):
    flat = packed.reshape(-1)
    out, off = [], 0
    for shp in shapes:
        n = int(np.prod(shp))
        out.append(flat[off:off + n].reshape(shp))
        off += n
    return out


def kernel(x, mix_norm_w, w_in, q_norm_w, k_norm_w, hgrn_lb_logits, hgrn_gnorm_w, conv_dw_w, conv_dw_b, conv_ln_w, conv_ln_b, conv_pw_w, conv_pw_b, attn_out_norm_w, conv_out_norm_w, w_out, ffn_norm_w, w_gate, w_up, w_down, loss_target, m_mix_norm_w, m_w_in, m_q_norm_w, m_k_norm_w, m_hgrn_lb_logits, m_hgrn_gnorm_w, m_conv_dw_w, m_conv_dw_b, m_conv_ln_w, m_conv_ln_b, m_conv_pw_w, m_conv_pw_b, m_attn_out_norm_w, m_conv_out_norm_w, m_w_out, m_ffn_norm_w, m_w_gate, m_w_up, m_w_down, v_mix_norm_w, v_w_in, v_q_norm_w, v_k_norm_w, v_hgrn_lb_logits, v_hgrn_gnorm_w, v_conv_dw_w, v_conv_dw_b, v_conv_ln_w, v_conv_ln_b, v_conv_pw_w, v_conv_pw_b, v_attn_out_norm_w, v_conv_out_norm_w, v_w_out, v_ffn_norm_w, v_w_gate, v_w_up, v_w_down):
    w = dict(mix_norm_w=mix_norm_w, w_in=w_in, q_norm_w=q_norm_w, k_norm_w=k_norm_w, hgrn_lb_logits=hgrn_lb_logits,
             hgrn_gnorm_w=hgrn_gnorm_w, conv_dw_w=conv_dw_w, conv_dw_b=conv_dw_b, conv_ln_w=conv_ln_w,
             conv_ln_b=conv_ln_b, conv_pw_w=conv_pw_w, conv_pw_b=conv_pw_b, attn_out_norm_w=attn_out_norm_w,
             conv_out_norm_w=conv_out_norm_w, w_out=w_out, ffn_norm_w=ffn_norm_w, w_gate=w_gate, w_up=w_up,
             w_down=w_down)
    m = dict(mix_norm_w=m_mix_norm_w, w_in=m_w_in, q_norm_w=m_q_norm_w, k_norm_w=m_k_norm_w,
             hgrn_lb_logits=m_hgrn_lb_logits, hgrn_gnorm_w=m_hgrn_gnorm_w, conv_dw_w=m_conv_dw_w,
             conv_dw_b=m_conv_dw_b, conv_ln_w=m_conv_ln_w, conv_ln_b=m_conv_ln_b, conv_pw_w=m_conv_pw_w,
             conv_pw_b=m_conv_pw_b, attn_out_norm_w=m_attn_out_norm_w, conv_out_norm_w=m_conv_out_norm_w,
             w_out=m_w_out, ffn_norm_w=m_ffn_norm_w, w_gate=m_w_gate, w_up=m_w_up, w_down=m_w_down)
    v = dict(mix_norm_w=v_mix_norm_w, w_in=v_w_in, q_norm_w=v_q_norm_w, k_norm_w=v_k_norm_w,
             hgrn_lb_logits=v_hgrn_lb_logits, hgrn_gnorm_w=v_hgrn_gnorm_w, conv_dw_w=v_conv_dw_w,
             conv_dw_b=v_conv_dw_b, conv_ln_w=v_conv_ln_w, conv_ln_b=v_conv_ln_b, conv_pw_w=v_conv_pw_w,
             conv_pw_b=v_conv_pw_b, attn_out_norm_w=v_attn_out_norm_w, conv_out_norm_w=v_conv_out_norm_w,
             w_out=v_w_out, ffn_norm_w=v_ffn_norm_w, w_gate=v_w_gate, w_up=v_w_up, w_down=v_w_down)
    chip = 2 * lax.axis_index("x") + lax.axis_index("y")

    chip1 = chip.reshape(1).astype(jnp.int32)

    flat2 = lambda a: a.reshape(-1, a.shape[-1])
    groups = [[(l, "w_in")] if first else [(l, n) for n in BIG_ORDER[1:]] for l in range(DEPTH) for first in (1, 0)]
    group_of = {key: g for g, keys in enumerate(groups) for key in keys}
    groups[0].append((0, "small"))
    starts = []
    for g, keys in enumerate(groups):
        slots = [_cast_slot("cast_small", _pack([w[k] for k in SHARDED_SMALL]), 0, chip1, 1, F32) if n == "small"
                 else _cast_slot("cast_%s_l%d" % (n, l), flat2(w[n]), l, chip1) for l, n in keys]
        starts.append(_split_start("gather_start_g%d" % g, [], slots, False))
    got = {}

    def wget(l, name, after):
        if (l, name) not in got:
            g = group_of[(l, name)]
            for key, arr in zip(groups[g], _split_wait("gather_wait_g%d" % g, starts[g], after)):
                got[key] = arr
        return got[(l, name)]

    pending = []

    def on_grads(l, grads):
        names = [n for n in SCATTER_ORDER if n in grads]
        own = [_own_slot("own_%s_l%d" % (n, l), grads[n], chip1) for n in names]
        st = _split_start("scatter_start_l%d_%s" % (l, names[0]), [grads[n] for n in names], own, True)
        pending.append((l, names, st))
        return [st["token"]]

    wget(0, "w_in", x)
    gathered_small = got[(0, "small")]
    parts = [_unpack(gathered_small[j], [w[n].shape for n in SHARDED_SMALL]) for j in range(N_CHIP)]
    full_small = {n: jnp.concatenate([parts[j][i] for j in range(N_CHIP)], axis=ax)
                  for i, (n, ax) in enumerate(SHARDED_SMALL.items())}
    sm = {n: w[n] for n in WEIGHTS if n not in BIG_ORDER and n not in SHARDED_SMALL}
    sm["conv_dw_w"] = full_small["conv_dw_w"]
    sm["conv_pw_w"] = full_small["conv_pw_w"]
    logits = full_small["hgrn_lb_logits"].reshape(DEPTH * 2, D_HGRN)
    sm["lb"] = _lower_bounds(logits).reshape(DEPTH, 2, D_HGRN)

    sq_sum, grad_x, smalls = _local_step(x, loss_target, wget, sm, [st["token"] for st in starts], on_grads)
    loss = lax.psum(0.5 * sq_sum / D_MODEL, ("x", "y", "c"))

    landed = {}
    for l, names, st in pending:
        for n, arr in zip(names, _split_wait("scatter_wait_l%d_%s" % (l, names[0]), st, grad_x)):
            landed[(l, n)] = arr
    sums = [_sum_layers("sum_" + n, [landed[(l, n)] for l in range(DEPTH)]) for n in SCATTER_ORDER]
    sib_start = _split_start("sibling_start", sums, [lax.empty(a.shape, a.dtype) for a in sums], "sibling")
    out = {}

    small_names = [n for n in WEIGHTS if n not in SCATTER_ORDER]
    g_pack = _pack([jnp.stack([smalls[l][n] for l in range(DEPTH)]) for n in SMALL_ORDER]) + sib_start["token"][0, 0]
    g_all = _exchange("gather_small_grads", [g_pack], "gather8")[0]
    g_tot = _sum_slots("sum_small", g_all)
    shapes = [(DEPTH,) + tuple(smalls[0][n].shape) for n in SMALL_ORDER]
    g_small = dict(zip(SMALL_ORDER, _unpack(g_tot, shapes)))
    lb_shard = lax.dynamic_slice_in_dim(g_small.pop("lb").reshape(DEPTH * 2, D_HGRN), chip * HEAD_DIM, HEAD_DIM, 1)
    g_small["hgrn_lb_logits"] = _lower_bounds_bwd(hgrn_lb_logits.reshape(DEPTH * 2, HEAD_DIM), lb_shard).reshape(
        hgrn_lb_logits.shape)
    g_small["conv_dw_w"] = lax.dynamic_slice_in_dim(g_small["conv_dw_w"], chip * HEAD_DIM, HEAD_DIM, 2)
    res = _adamw("adamw_small", _pack([w[n] for n in small_names]), _pack([g_small[n] for n in small_names]), None,
                 _pack([m[n] for n in small_names]), _pack([v[n] for n in small_names]))
    unpacked = [_unpack(r, [w[n].shape for n in small_names]) for r in res]
    for i, n in enumerate(small_names):
        out[n] = [unpacked[k][i] for k in range(4)]
    own, sib = _split_wait("sibling_wait", sib_start, res[0], with_srcs=True)
    for n, ga, gb in zip(SCATTER_ORDER, own, sib):
        big = _adamw("adamw_" + n, flat2(w[n]), ga, gb, flat2(m[n]), flat2(v[n]))
        out[n] = [r.reshape(w[n].shape) for r in big]

    return (loss, grad_x, *[out[n][0] for n in WEIGHTS], *[out[n][1] for n in WEIGHTS],
            *[out[n][2] for n in WEIGHTS], *[out[n][3] for n in WEIGHTS])
```

```python
import functools

import numpy as np
import jax
import jax.numpy as jnp
from jax import lax
from jax.experimental import pallas as pl
from jax.experimental.pallas import tpu as pltpu

F32, BF16 = jnp.float32, jnp.bfloat16

D_MODEL = 1024
DEPTH = 2
GRID_W = 64
D_ATTN, D_HGRN, D_CONV = 512, 256, 256
HEAD_DIM = 64
N_KV = 2
KV_LANES = D_ATTN // N_KV
ROPE_THETA = 10000.0
F_MIN = 1e-6
CONV_W = 31
CONV_PAD = 15
D_FF = 2816
D_IN = 2560
N_CHIP = 4
N_DEV = 8
IN_BLK = D_IN // N_CHIP
FF_BLK = D_FF // N_CHIP
OUT_BLK = D_MODEL // N_CHIP
EPS = 1e-6
LN_EPS = 1e-5
LR, B1, B2, ADAM_EPS, WD, STEP = 0.001, 0.9, 0.999, 1e-08, 0.01, 10
CHUNK = 16
HBLK = 256
CONV_TILE = 128
ATTN_FWD_ROWS = 512
ATTN_BWD_ROWS = 256
BWD_GROUP = 2
VMEM_LIMIT = 56 * 1024 * 1024

COL_Q, COL_K, COL_V = 0, 4, 5
COL_HQ, COL_FF, COL_FB, COL_HI, COL_HG, COL_CA, COL_CB = 3, 4, 5, 6, 7, 8, 9


def _params(sem=None):
    return pltpu.CompilerParams(dimension_semantics=sem, vmem_limit_bytes=VMEM_LIMIT)


def _sds(shape, dtype):
    return jax.ShapeDtypeStruct(tuple(shape), dtype)


def _full(shape):
    n = len(shape)
    return pl.BlockSpec(tuple(shape), lambda *_: (0,) * n)


def _sigmoid(x):
    return 0.5 * jnp.tanh(0.5 * x) + 0.5


def _gate_sigmoid(x):
    return 1.0 / (1.0 + jnp.exp(-x))


def _silu(x):
    return x * _sigmoid(x)


def _dsilu(x):
    s = _sigmoid(x)
    return s * (1.0 + x * (1.0 - s))


def _rowgroups(v):
    m, c = v.shape
    return v.reshape(m // 8, 8, c).sum(axis=0)


def _split2(x):
    hi = x.astype(BF16)
    lo = (x - hi.astype(F32)).astype(BF16)
    return hi, lo


def _rdot2(x, m):
    hi, lo = _split2(x)
    return (jnp.dot(hi, m, preferred_element_type=F32) + jnp.dot(lo, m, preferred_element_type=F32))


def _ldot3(m, x):
    hi = x.astype(BF16)
    r1 = x - hi.astype(F32)
    mid = r1.astype(BF16)
    lo = (r1 - mid.astype(F32)).astype(BF16)
    return (jnp.dot(m, hi, preferred_element_type=F32) + jnp.dot(m, mid, preferred_element_type=F32)
            + jnp.dot(m, lo, preferred_element_type=F32))


def _dot_nt(a, b):
    return lax.dot_general(a, b, (((1,), (1,)), ((), ())), preferred_element_type=F32)


def _dot_tn(a, b):
    return lax.dot_general(a, b, (((0,), (0,)), ((), ())), preferred_element_type=F32)


def _seg_matrix(n, seg, val):
    i = np.arange(n)
    return ((i[:, None] // seg) == (i[None, :] // seg)).astype(np.float32) * val


def _rot_matrix(n):
    r = np.zeros((n, n), np.float32)
    for i in range(n):
        if (i % 32) < 16:
            r[i + 16, i] = -1.0
        else:
            r[i - 16, i] = 1.0
    return r


def _rep_matrix():
    r = np.zeros((N_KV * HEAD_DIM, D_ATTN), np.float32)
    for kv in range(N_KV):
        for g in range(KV_LANES // HEAD_DIM):
            for d in range(HEAD_DIM):
                r[HEAD_DIM * kv + d, KV_LANES * kv + HEAD_DIM * g + d] = 1.0
    return r


def _cumsum_matrix(rev):
    i = np.arange(HBLK)
    same = (i[:, None] // CHUNK) == (i[None, :] // CHUNK)
    tri = (i[None, :] >= i[:, None]) if rev else (i[None, :] <= i[:, None])
    return (same & tri).astype(np.float32)


def _sel_matrices():
    sel = np.zeros((CHUNK, CHUNK * CHUNK), np.float32)
    selt = np.zeros((CHUNK, CHUNK * CHUNK), np.float32)
    for t in range(CHUNK):
        for s in range(CHUNK):
            sel[t, t * CHUNK + s] = 1.0
            selt[s, t * CHUNK + s] = 1.0
    return sel, selt


def _bf(a):
    return jnp.asarray(a, dtype=BF16)


def _mm(name, grid, pairs, extras, outs, epilogue, acc=None, sem=None):
    n_p, n_e, n_o = len(pairs), len(extras), len(outs)

    def body(*refs):
        ab = refs[:2 * n_p]
        ex = refs[2 * n_p:2 * n_p + n_e]
        out = refs[2 * n_p + n_e:2 * n_p + n_e + n_o]
        scr = refs[2 * n_p + n_e + n_o:]
        tot = None
        for i in range(n_p):
            a = ab[2 * i][...]
            b = ab[2 * i + 1][...]
            if a.ndim == 3:
                a = a.reshape(-1, a.shape[-1])
            if b.ndim == 3:
                b = b.reshape(-1, b.shape[-1])
            r = lax.dot_general(a.astype(BF16), b.astype(BF16), pairs[i][4], preferred_element_type=F32)
            tot = r if tot is None else tot + r

        def finish(total):
            res = epilogue(total, *[e[...] for e in ex])
            for o_ref, val in zip(out, res):
                o_ref[...] = val.astype(o_ref.dtype)

        if acc is None:
            finish(tot)
        else:
            k = pl.program_id(acc[0])

            @pl.when(k == 0)
            def _():
                scr[0][...] = tot

            @pl.when(k > 0)
            def _():
                scr[0][...] += tot

            @pl.when(k == grid[acc[0]] - 1)
            def _():
                finish(scr[0][...])

    args, in_specs = [], []
    for a, a_spec, b, b_spec, _ in pairs:
        args += [a, b]
        in_specs += [a_spec, b_spec]
    for e, e_spec in extras:
        args.append(e)
        in_specs.append(e_spec)
    if sem is None:
        sem = tuple("arbitrary" if (acc is not None and i == acc[0]) else "parallel" for i in range(len(grid)))
    return pl.pallas_call(
        body, name=name, grid=grid, in_specs=in_specs,
        out_specs=[o[1] for o in outs], out_shape=[o[0] for o in outs],
        scratch_shapes=[] if acc is None else [pltpu.VMEM(acc[1], F32)],
        compiler_params=_params(sem),
    )(*args)


NN = (((1,), (0,)), ((), ()))
NT = (((1,), (1,)), ((), ()))
TN = (((0,), (0,)), ((), ()))


def _row_tile(t):
    return min(256, t)


def _rms_fwd(name, x, w, deps=()):
    t, d = x.shape
    tm = _row_tile(t)

    def body(x_ref, w_ref, *rest):
        o_ref = rest[-1]
        xv = x_ref[...]
        r = lax.rsqrt(jnp.mean(xv * xv, axis=-1, keepdims=True) + EPS)
        o_ref[...] = (xv * r * w_ref[...]).astype(BF16)

    return pl.pallas_call(
        body, name=name, grid=(t // tm,),
        in_specs=[pl.BlockSpec((tm, d), lambda i: (i, 0)), _full((1, d))] + [_full(a.shape) for a in deps],
        out_specs=pl.BlockSpec((tm, d), lambda i: (i, 0)), out_shape=_sds((t, d), BF16),
        compiler_params=_params(("parallel",)),
    )(x, w, *deps)


def _rms_bwd(name, x, w, dh, dres, deps=()):
    t, d = x.shape
    tm = _row_tile(t)

    def body(x_ref, w_ref, dh_ref, dres_ref, *rest):
        dx_ref, dw_ref = rest[-2:]
        xv = x_ref[...]
        r = lax.rsqrt(jnp.mean(xv * xv, axis=-1, keepdims=True) + EPS)
        dy = dh_ref[...]
        gw = dy * w_ref[...]
        dx_ref[...] = dres_ref[...] + r * gw - xv * (r * r * r) * jnp.mean(gw * xv, axis=-1, keepdims=True)

        @pl.when(pl.program_id(0) == 0)
        def _():
            dw_ref[...] = jnp.zeros_like(dw_ref)

        dw_ref[...] += _rowgroups(dy * xv * r)

    tile = pl.BlockSpec((tm, d), lambda i: (i, 0))
    return pl.pallas_call(
        body, name=name, grid=(t // tm,),
        in_specs=[tile, _full((1, d)), tile, tile] + [_full(a.shape) for a in deps],
        out_specs=[tile, _full((8, d))], out_shape=[_sds((t, d), F32), _sds((8, d), F32)],
        compiler_params=_params(("arbitrary",)),
    )(x, w, dh, dres, *deps)


def _loss_kernel(y, target):
    t, d = y.shape
    tm = _row_tile(t)

    def body(y_ref, t_ref, dy_ref, acc_ref):
        e = y_ref[...] - t_ref[...]
        dy_ref[...] = e * (1.0 / d)

        @pl.when(pl.program_id(0) == 0)
        def _():
            acc_ref[...] = jnp.zeros_like(acc_ref)

        acc_ref[...] += _rowgroups(e * e)

    tile = pl.BlockSpec((tm, d), lambda i: (i, 0))
    return pl.pallas_call(
        body, name="loss_head", grid=(t // tm,), in_specs=[tile, tile],
        out_specs=[tile, _full((8, d))], out_shape=[_sds((t, d), F32), _sds((8, d), F32)],
        compiler_params=_params(("arbitrary",)),
    )(y, target)


def _rope_tables(s):
    rows = s // GRID_W
    row_id = jnp.repeat(jnp.arange(rows, dtype=F32), GRID_W)
    col_id = jnp.tile(jnp.arange(GRID_W, dtype=F32), rows)
    half = HEAD_DIM // 2
    inv_freq = ROPE_THETA ** (-jnp.arange(0, half, 2, dtype=F32) / half)
    ang_r = row_id[:, None] * inv_freq[None, :]
    ang_c = col_id[:, None] * inv_freq[None, :]
    ang = jnp.concatenate([ang_r, ang_r, ang_c, ang_c], axis=-1)
    return jnp.cos(ang).astype(F32), jnp.sin(ang).astype(F32)


def _attn_consts():
    return dict(
        seg_q=_bf(_seg_matrix(D_ATTN, HEAD_DIM, 1.0 / HEAD_DIM)),
        seg_k=_bf(_seg_matrix(N_KV * HEAD_DIM, HEAD_DIM, 1.0 / HEAD_DIM)),
        rot_q=_bf(_rot_matrix(D_ATTN)), rot_k=_bf(_rot_matrix(N_KV * HEAD_DIM)),
        rep=_bf(_rep_matrix()), rep_t=_bf(_rep_matrix().T))


def _attn_prep(name, proj, s, tabs, qw, kw, ac):
    t = proj.shape[0]
    tm = _row_tile(s)
    nst = s // tm
    kw_ = N_KV * HEAD_DIM

    def body(q_ref, k_ref, v_ref, cq_ref, sq_ref, ck_ref, sk_ref, qw_ref, kw_ref,
             segq_ref, segk_ref, rotq_ref, rotk_ref, rep_ref, qn_ref, kr_ref, vr_ref):
        q = q_ref[...]
        r = lax.rsqrt(jnp.dot((q * q).astype(BF16), segq_ref[...], preferred_element_type=F32) + EPS)
        qn = q * r * qw_ref[...]
        qr = qn * cq_ref[...] + _rdot2(qn, rotq_ref[...]) * sq_ref[...]
        qn_ref[...] = (qr * (HEAD_DIM ** -0.5)).astype(BF16)
        k = k_ref[...]
        rk = lax.rsqrt(jnp.dot((k * k).astype(BF16), segk_ref[...], preferred_element_type=F32) + EPS)
        kn = k * rk * kw_ref[...]
        kr = kn * ck_ref[...] + _rdot2(kn, rotk_ref[...]) * sk_ref[...]
        kr_ref[...] = jnp.dot(kr.astype(BF16), rep_ref[...], preferred_element_type=F32).astype(BF16)
        vr_ref[...] = jnp.dot(v_ref[...].astype(BF16), rep_ref[...], preferred_element_type=F32).astype(BF16)

    wide = pl.BlockSpec((tm, D_ATTN), lambda i: (i, 0))
    tabq = pl.BlockSpec((tm, D_ATTN), lambda i: (i % nst, 0))
    tabk = pl.BlockSpec((tm, kw_), lambda i: (i % nst, 0))
    return pl.pallas_call(
        body, name=name, grid=(t // tm,),
        in_specs=[pl.BlockSpec((tm, D_ATTN), lambda i: (i, COL_Q)), pl.BlockSpec((tm, kw_), lambda i: (i, COL_K)),
                  pl.BlockSpec((tm, kw_), lambda i: (i, COL_V)), tabq, tabq, tabk, tabk,
                  _full((1, D_ATTN)), _full((1, kw_)), _full((D_ATTN, D_ATTN)), _full((kw_, kw_)),
                  _full((D_ATTN, D_ATTN)), _full((kw_, kw_)), _full((kw_, D_ATTN))],
        out_specs=[wide, wide, wide], out_shape=[_sds((t, D_ATTN), BF16)] * 3,
        compiler_params=_params(("parallel",)),
    )(proj, proj, proj, tabs["cq"], tabs["sq"], tabs["ck"], tabs["sk"], qw, kw,
      ac["seg_q"], ac["seg_k"], ac["rot_q"], ac["rot_k"], ac["rep"])


def _attn_prep_bwd(name, proj, s, tabs, qw, kw, ac, dqs, dkr, dvr):
    t = proj.shape[0]
    tm = _row_tile(s)
    nst = s // tm
    kw_ = N_KV * HEAD_DIM
    wout = D_ATTN + 2 * kw_

    def norm_rope_bwd(x, w, cos, sin, seg, rot, d_roped):
        dn = d_roped * cos - _rdot2(d_roped * sin, rot)
        r = lax.rsqrt(jnp.dot((x * x).astype(BF16), seg, preferred_element_type=F32) + EPS)
        gw = dn * w
        dx = r * gw - x * (r * r * r) * _rdot2(gw * x, seg)
        return dx, _rowgroups(dn * x * r)

    def body(q_ref, k_ref, cq_ref, sq_ref, ck_ref, sk_ref, qw_ref, kw_ref, segq_ref, segk_ref, rotq_ref, rotk_ref,
             rept_ref, dqs_ref, dkr_ref, dvr_ref, dp_ref, dqw_ref, dkw_ref):
        dq, dqw = norm_rope_bwd(q_ref[...], qw_ref[...], cq_ref[...], sq_ref[...], segq_ref[...], rotq_ref[...],
                                dqs_ref[...] * (HEAD_DIM ** -0.5))
        dk_roped = _rdot2(dkr_ref[...], rept_ref[...])
        dk, dkw = norm_rope_bwd(k_ref[...], kw_ref[...], ck_ref[...], sk_ref[...], segk_ref[...], rotk_ref[...],
                                dk_roped)
        dv = _rdot2(dvr_ref[...], rept_ref[...])
        dp_ref[:, 0:D_ATTN] = dq.astype(BF16)
        dp_ref[:, D_ATTN:D_ATTN + kw_] = dk.astype(BF16)
        dp_ref[:, D_ATTN + kw_:wout] = dv.astype(BF16)

        @pl.when(pl.program_id(0) == 0)
        def _():
            dqw_ref[...] = jnp.zeros_like(dqw_ref)
            dkw_ref[...] = jnp.zeros_like(dkw_ref)

        dqw_ref[...] += dqw
        dkw_ref[...] += dkw

    wide = pl.BlockSpec((tm, D_ATTN), lambda i: (i, 0))
    tabq = pl.BlockSpec((tm, D_ATTN), lambda i: (i % nst, 0))
    tabk = pl.BlockSpec((tm, kw_), lambda i: (i % nst, 0))
    return pl.pallas_call(
        body, name=name, grid=(t // tm,),
        in_specs=[pl.BlockSpec((tm, D_ATTN), lambda i: (i, COL_Q)), pl.BlockSpec((tm, kw_), lambda i: (i, COL_K)),
                  tabq, tabq, tabk, tabk, _full((1, D_ATTN)), _full((1, kw_)),
                  _full((D_ATTN, D_ATTN)), _full((kw_, kw_)), _full((D_ATTN, D_ATTN)), _full((kw_, kw_)),
                  _full((D_ATTN, kw_)), wide, wide, wide],
        out_specs=[pl.BlockSpec((tm, wout), lambda i: (i, 0)), _full((8, D_ATTN)), _full((8, kw_))],
        out_shape=[_sds((t, wout), BF16), _sds((8, D_ATTN), F32), _sds((8, kw_), F32)],
        compiler_params=_params(("arbitrary",)),
    )(proj, proj, tabs["cq"], tabs["sq"], tabs["ck"], tabs["sk"], qw, kw,
      ac["seg_q"], ac["seg_k"], ac["rot_q"], ac["rot_k"], ac["rep_t"], dqs, dkr, dvr)


def _attn_tile(s, rows=256):
    return min(rows, s)


def _head_masks(shape):
    lane = lax.broadcasted_iota(jnp.int32, shape, 1)
    return [(lane // HEAD_DIM) == g for g in range(KV_LANES // HEAD_DIM)]


def _attn_fwd(name, qn, kr, vr, b, s):
    t = qn.shape[0]
    tq = _attn_tile(s, ATTN_FWD_ROWS)
    nq = s // tq

    def body(q_ref, k_ref, v_ref, o_ref):
        q = q_ref[...]
        k = k_ref[...]
        v = v_ref[...]
        acc = jnp.zeros((tq, KV_LANES), F32)
        for mask in _head_masks((tq, KV_LANES)):
            sc = _dot_nt(jnp.where(mask, q, jnp.zeros_like(q)), k)
            p = jnp.exp(sc - jnp.max(sc, axis=-1, keepdims=True))
            inv = 1.0 / jnp.sum(p, axis=-1, keepdims=True)
            og = jnp.dot(p.astype(BF16), v, preferred_element_type=F32) * inv
            acc = jnp.where(mask, og, acc)
        o_ref[...] = acc

    return pl.pallas_call(
        body, name=name, grid=(b, N_KV, nq),
        in_specs=[pl.BlockSpec((tq, KV_LANES), lambda bi, kv, i: (bi * nq + i, kv)),
                  pl.BlockSpec((s, KV_LANES), lambda bi, kv, i: (bi, kv)),
                  pl.BlockSpec((s, KV_LANES), lambda bi, kv, i: (bi, kv))],
        out_specs=pl.BlockSpec((tq, KV_LANES), lambda bi, kv, i: (bi * nq + i, kv)),
        out_shape=_sds((t, D_ATTN), F32),
        compiler_params=_params(("parallel", "parallel", "parallel")),
    )(qn, kr, vr)


def _attn_bwd(name, qn, kr, vr, do, b, s):
    t = qn.shape[0]
    tq = _attn_tile(s, ATTN_BWD_ROWS)
    nq = s // tq

    def body(q_ref, k_ref, v_ref, do_ref, dq_ref, dk_ref, dv_ref):
        @pl.when(pl.program_id(2) == 0)
        def _():
            dk_ref[...] = jnp.zeros_like(dk_ref)
            dv_ref[...] = jnp.zeros_like(dv_ref)

        q = q_ref[...]
        k = k_ref[...]
        v = v_ref[...]
        dout = do_ref[...].astype(BF16)
        masks = _head_masks((tq, KV_LANES))
        q4 = jnp.concatenate([jnp.where(m, q, jnp.zeros_like(q)) for m in masks], axis=0)
        do4 = jnp.concatenate([jnp.where(m, dout, jnp.zeros_like(dout)) for m in masks], axis=0)
        sc = _dot_nt(q4, k)
        p = jnp.exp(sc - jnp.max(sc, axis=-1, keepdims=True))
        p = p * (1.0 / jnp.sum(p, axis=-1, keepdims=True))
        dp = _dot_nt(do4, v)
        ds = (p * (dp - jnp.sum(p * dp, axis=-1, keepdims=True))).astype(BF16)
        dq4 = jnp.dot(ds, k, preferred_element_type=F32)
        dq = jnp.zeros((tq, KV_LANES), F32)
        for g, m in enumerate(masks):
            dq = jnp.where(m, dq4[g * tq:(g + 1) * tq, :], dq)
        dq_ref[...] = dq
        dk_ref[...] += _dot_tn(ds, q4)
        dv_ref[...] += _dot_tn(p.astype(BF16), do4)

    qspec = pl.BlockSpec((tq, KV_LANES), lambda bi, kv, i: (bi * nq + i, kv))
    kspec = pl.BlockSpec((s, KV_LANES), lambda bi, kv, i: (bi, kv))
    return pl.pallas_call(
        body, name=name, grid=(b, N_KV, nq),
        in_specs=[qspec, kspec, kspec, qspec],
        out_specs=[qspec, kspec, kspec], out_shape=[_sds((t, D_ATTN), F32)] * 3,
        compiler_params=_params(("parallel", "parallel", "arbitrary")),
    )(qn, kr, vr, do)


def _hgrn_consts(rev):
    sel, selt = _sel_matrices()
    cs = _cumsum_matrix(rev)
    return dict(cs=_bf(cs), cs_t=_bf(cs.T), seg=_bf(_seg_matrix(D_HGRN, HEAD_DIM, 1.0)),
                bd=jnp.asarray(_seg_matrix(D_HGRN, HEAD_DIM, 1.0), F32),
                sel=_bf(sel), selt=_bf(selt), seld=_bf(sel - selt))


def _gates(z, lb):
    sig = _gate_sigmoid(z)
    f = lb + (1.0 - lb) * sig
    g = jnp.log(jnp.maximum(f, F_MIN))
    sn = _gate_sigmoid(-z)
    return sig, f, g, sn, (1.0 - lb) * sn


def _pair_decay(b, rev):
    row = lax.broadcasted_iota(jnp.int32, (CHUNK, D_HGRN), 0)
    parts = []
    for t in range(CHUNK):
        m = (row >= t) if rev else (row <= t)
        parts.append(jnp.where(m, jnp.exp(jnp.minimum(b[t:t + 1, :] - b, 0.0)), 0.0))
    return jnp.concatenate(parts, axis=0)


def _rows_rep(a):
    return jnp.concatenate([jnp.broadcast_to(a[t:t + 1, :], a.shape) for t in range(CHUNK)], axis=0)


def _tile_rows(a):
    return jnp.concatenate([a] * CHUNK, axis=0)


def _hgrn_specs(b, s, rev):
    nb = s // HBLK

    def blk(j):
        return (nb - 1 - j) if rev else j

    def col(c):
        return pl.BlockSpec((HBLK, D_HGRN), lambda bi, j: (bi * nb + blk(j), c))

    return nb, blk, col


def _hgrn_fwd(name, proj, lb, b, s, rev, hc):
    t = proj.shape[0]
    nb, blk, col = _hgrn_specs(b, s, rev)
    n_ch = HBLK // CHUNK
    last = 0 if rev else CHUNK - 1

    def body(q_ref, z_ref, v_ref, lb_ref, cs_ref, seg_ref, bd_ref, sel_ref, o_ref, st_ref, state, b_scr, k_scr):
        @pl.when(pl.program_id(1) == 0)
        def _():
            state[...] = jnp.zeros_like(state)

        st_ref[...] = state[...]
        _, _, g, _, kk = _gates(z_ref[...], lb_ref[...])
        k_scr[...] = kk
        b_scr[...] = _ldot3(cs_ref[...], g)

        def chunk(i, carry):
            c = (n_ch - 1 - i) if rev else i
            rows = pl.ds(pl.multiple_of(c * CHUNK, CHUNK), CHUNK)
            q = q_ref[rows, :]
            k = k_scr[rows, :]
            v = v_ref[rows, :]
            bb = b_scr[rows, :]
            bl = bb[last:last + 1, :]
            pairs = _pair_decay(bb, rev) * _rows_rep(q) * _tile_rows(k)
            a = jnp.dot(pairs.astype(BF16), seg_ref[...], preferred_element_type=F32)
            o_intra = jnp.dot(sel_ref[...], (a * _tile_rows(v)).astype(BF16), preferred_element_type=F32)
            st = state[...]
            o_inter = _dot_nt((q * jnp.exp(bb)).astype(BF16), st.astype(BF16))
            o_ref[rows, :] = o_intra + o_inter
            ke = k * jnp.exp(bl - bb)
            state[...] = st * jnp.exp(bl) + bd_ref[...] * _dot_tn(v.astype(BF16), ke.astype(BF16))
            return carry

        lax.fori_loop(0, n_ch, chunk, 0)

    sq = (D_HGRN, D_HGRN)
    return pl.pallas_call(
        body, name=name, grid=(b, nb),
        in_specs=[col(COL_HQ), col(COL_FB if rev else COL_FF), col(COL_HI), _full((1, D_HGRN)),
                  _full((HBLK, HBLK)), _full(sq), _full(sq), _full((CHUNK, CHUNK * CHUNK))],
        out_specs=[pl.BlockSpec((HBLK, D_HGRN), lambda bi, j: (bi * nb + blk(j), 0)),
                   pl.BlockSpec((None,) + sq, lambda bi, j: (bi * nb + blk(j), 0, 0))],
        out_shape=[_sds((t, D_HGRN), F32), _sds((b * nb,) + sq, F32)],
        scratch_shapes=[pltpu.VMEM(sq, F32), pltpu.VMEM((HBLK, D_HGRN), F32), pltpu.VMEM((HBLK, D_HGRN), F32)],
        compiler_params=_params(("parallel", "arbitrary")),
    )(proj, proj, proj, lb, hc["cs"], hc["seg"], hc["bd"], hc["sel"])


def _hgrn_bwd(name, proj, lb, st_blk, do, dq_prev, dv_prev, b, s, rev, hc):
    t = proj.shape[0]
    nb = s // HBLK
    n_ch = HBLK // CHUNK
    last = 0 if rev else CHUNK - 1

    def blk(j):
        return j if rev else (nb - 1 - j)

    def col(c):
        return pl.BlockSpec((HBLK, D_HGRN), lambda bi, j: (bi * nb + blk(j), c))

    def body(q_ref, z_ref, v_ref, lb_ref, st_ref, do_ref, dqp_ref, dvp_ref, cs_ref, cst_ref, seg_ref, bd_ref,
             sel_ref, selt_ref, seld_ref, dq_ref, dv_ref, dz_ref, dlb_ref,
             dstate, states, b_scr, k_scr, db_scr, dk_scr):
        first = jnp.logical_and(pl.program_id(0) == 0, pl.program_id(1) == 0)

        @pl.when(first)
        def _():
            dlb_ref[...] = jnp.zeros_like(dlb_ref)

        @pl.when(pl.program_id(1) == 0)
        def _():
            dstate[...] = jnp.zeros_like(dstate)

        lbv = lb_ref[...]
        z = z_ref[...]
        sig, f, g, sn, kk = _gates(z, lbv)
        k_scr[...] = kk
        b_scr[...] = _ldot3(cs_ref[...], g)

        def rows_of(c):
            return pl.ds(pl.multiple_of(c * CHUNK, CHUNK), CHUNK)

        def replay(i, st):
            c = (n_ch - 1 - i) if rev else i
            rows = rows_of(c)
            states[c] = st
            bb = b_scr[rows, :]
            bl = bb[last:last + 1, :]
            ke = k_scr[rows, :] * jnp.exp(bl - bb)
            return st * jnp.exp(bl) + bd_ref[...] * _dot_tn(v_ref[rows, :].astype(BF16), ke.astype(BF16))

        lax.fori_loop(0, n_ch, replay, st_ref[...])
        row = lax.broadcasted_iota(jnp.int32, (CHUNK, D_HGRN), 0)

        def chunk(i, carry):
            c = i if rev else (n_ch - 1 - i)
            rows = rows_of(c)
            q = q_ref[rows, :]
            k = k_scr[rows, :]
            v = v_ref[rows, :]
            bb = b_scr[rows, :]
            dout = do_ref[rows, :]
            bl = bb[last:last + 1, :]
            st_p = states[c]
            dst_n = dstate[...]
            eb = jnp.exp(bb)
            ebl = jnp.exp(bl - bb)
            ebl_last = jnp.exp(bl)
            qe = q * eb
            ke = k * ebl
            dob = dout.astype(BF16)
            dstb = dst_n.astype(BF16)
            dqe = jnp.dot(dob, st_p.astype(BF16), preferred_element_type=F32)
            dke = jnp.dot(v.astype(BF16), dstb, preferred_element_type=F32)
            dv = _dot_nt(ke.astype(BF16), dstb)
            dbl = jnp.sum(dst_n * st_p, axis=0, keepdims=True) * ebl_last + jnp.sum(dke * ke, axis=0, keepdims=True)
            dq = dqe * eb
            dk = dke * ebl
            db = dqe * qe - dke * ke
            dec = _pair_decay(bb, rev)
            q_rep = _rows_rep(q)
            k_til = _tile_rows(k)
            do_rep = _rows_rep(dout)
            pairs = dec * q_rep * k_til
            a = jnp.dot(pairs.astype(BF16), seg_ref[...], preferred_element_type=F32)
            wb = jnp.dot((_tile_rows(v) * do_rep).astype(BF16), seg_ref[...], preferred_element_type=F32)
            gdec = wb * dec
            dq = dq + jnp.dot(sel_ref[...], (gdec * k_til).astype(BF16), preferred_element_type=F32)
            dk = dk + jnp.dot(selt_ref[...], (gdec * q_rep).astype(BF16), preferred_element_type=F32)
            dv = dv + jnp.dot(selt_ref[...], (a * do_rep).astype(BF16), preferred_element_type=F32)
            db = db + jnp.dot(seld_ref[...], (wb * pairs).astype(BF16), preferred_element_type=F32)
            db = db + jnp.where(row == last, dbl, 0.0)
            dq_ref[rows, :] = dq + dqp_ref[rows, :]
            dv_ref[rows, :] = dv + dvp_ref[rows, :]
            dk_scr[rows, :] = dk
            db_scr[rows, :] = db
            dstate[...] = dst_n * ebl_last + bd_ref[...] * _dot_tn(dob, qe.astype(BF16))
            return carry

        lax.fori_loop(0, n_ch, chunk, 0)
        hi, lo = _split2(db_scr[...])
        dg = (jnp.dot(cst_ref[...], hi, preferred_element_type=F32)
              + jnp.dot(cst_ref[...], lo, preferred_element_type=F32))
        dgf = jnp.where(f > F_MIN, dg / f, 0.0)
        dk = dk_scr[...]
        dz_ref[...] = dgf * (1.0 - lbv) * sig * (1.0 - sig) - dk * (1.0 - lbv) * sn * (1.0 - sn)
        dlb_ref[...] += _rowgroups(dgf * (1.0 - sig) - dk * sn)

    sq = (D_HGRN, D_HGRN)
    blk0 = pl.BlockSpec((HBLK, D_HGRN), lambda bi, j: (bi * nb + blk(j), 0))
    pairs_shape = (CHUNK, CHUNK * CHUNK)
    return pl.pallas_call(
        body, name=name, grid=(b, nb),
        in_specs=[col(COL_HQ), col(COL_FB if rev else COL_FF), col(COL_HI), _full((1, D_HGRN)),
                  pl.BlockSpec((None,) + sq, lambda bi, j: (bi * nb + blk(j), 0, 0)), blk0, blk0, blk0,
                  _full((HBLK, HBLK)), _full((HBLK, HBLK)), _full(sq), _full(sq),
                  _full(pairs_shape), _full(pairs_shape), _full(pairs_shape)],
        out_specs=[blk0, blk0, blk0, _full((8, D_HGRN))],
        out_shape=[_sds((t, D_HGRN), F32)] * 3 + [_sds((8, D_HGRN), F32)],
        scratch_shapes=[pltpu.VMEM(sq, F32), pltpu.VMEM((n_ch,) + sq, F32)] + [pltpu.VMEM((HBLK, D_HGRN), F32)] * 4,
        compiler_params=_params(("arbitrary", "arbitrary")),
    )(proj, proj, proj, lb, st_blk, do, dq_prev, dv_prev,
      hc["cs"], hc["cs_t"], hc["seg"], hc["bd"], hc["sel"], hc["selt"], hc["seld"])


def _scan_chunk_fwd(c, rev, q_ref, v_ref, k_scr, b_scr, state, o_ref, seg_ref, bd_ref, sel_ref):
    last = 0 if rev else CHUNK - 1
    rows = pl.ds(pl.multiple_of(c * CHUNK, CHUNK), CHUNK)
    q = q_ref[rows, :]
    k = k_scr[rows, :]
    v = v_ref[rows, :]
    bb = b_scr[rows, :]
    bl = bb[last:last + 1, :]
    pairs = _pair_decay(bb, rev) * _rows_rep(q) * _tile_rows(k)
    a = jnp.dot(pairs.astype(BF16), seg_ref[...], preferred_element_type=F32)
    o_intra = jnp.dot(sel_ref[...], (a * _tile_rows(v)).astype(BF16), preferred_element_type=F32)
    st = state[...]
    o_inter = _dot_nt((q * jnp.exp(bb)).astype(BF16), st.astype(BF16))
    o_ref[rows, :] = o_intra + o_inter
    ke = k * jnp.exp(bl - bb)
    state[...] = st * jnp.exp(bl) + bd_ref[...] * _dot_tn(v.astype(BF16), ke.astype(BF16))


def _scan_chunks_fwd(chains, seg_ref, bd_ref, sel_ref):
    work = []
    for c, rev, q_ref, v_ref, k_scr, b_scr, state, o_ref in chains:
        last = 0 if rev else CHUNK - 1
        rows = pl.ds(pl.multiple_of(c * CHUNK, CHUNK), CHUNK)
        q = q_ref[rows, :]
        k = k_scr[rows, :]
        v = v_ref[rows, :]
        bb = b_scr[rows, :]
        bl = bb[last:last + 1, :]
        st = state[...]
        work.append(dict(
            rows=rows, v=v, st=st, state=state, o_ref=o_ref, decay=jnp.exp(bl),
            pairs=(_pair_decay(bb, rev) * _rows_rep(q) * _tile_rows(k)).astype(BF16),
            qe=(q * jnp.exp(bb)).astype(BF16), ke=(k * jnp.exp(bl - bb)).astype(BF16), st_b=st.astype(BF16)))
    for w in work:
        w["a"] = jnp.dot(w["pairs"], seg_ref[...], preferred_element_type=F32)
        w["o_inter"] = _dot_nt(w["qe"], w["st_b"])
        w["upd"] = _dot_tn(w["v"].astype(BF16), w["ke"])
    for w in work:
        w["av"] = (w["a"] * _tile_rows(w["v"])).astype(BF16)
    for w in work:
        w["o_ref"][w["rows"], :] = jnp.dot(sel_ref[...], w["av"], preferred_element_type=F32) + w["o_inter"]
        w["state"][...] = w["st"] * w["decay"] + bd_ref[...] * w["upd"]


def _hgrn_fwd2(name, proj, lb_f, lb_b, b, s, hc_f, hc_b):
    t = proj.shape[0]
    nb = s // HBLK
    n_ch = HBLK // CHUNK
    n_chain = 2 * b

    def body(qf_ref, zf_ref, vf_ref, qb_ref, zb_ref, vb_ref, lbf_ref, lbb_ref, csf_ref, csb_ref, seg_ref, bd_ref,
             sel_ref, of_ref, ob_ref, stf_ref, stb_ref, *scr):
        state, b_scr, k_scr = scr[:n_chain], scr[n_chain:2 * n_chain], scr[2 * n_chain:]

        @pl.when(pl.program_id(0) == 0)
        def _():
            for st0 in state:
                st0[...] = jnp.zeros_like(st0)

        chains = []
        for bi in range(b):
            chains.append((False, qf_ref.at[bi], zf_ref.at[bi], vf_ref.at[bi], lbf_ref, csf_ref, of_ref.at[bi],
                           stf_ref.at[bi], 2 * bi))
            chains.append((True, qb_ref.at[bi], zb_ref.at[bi], vb_ref.at[bi], lbb_ref, csb_ref, ob_ref.at[bi],
                           stb_ref.at[bi], 2 * bi + 1))
        for rev, q, z, v, lb, cs, o, st, ci in chains:
            st[...] = state[ci][...]
            _, _, g, _, kk = _gates(z[...], lb[...])
            k_scr[ci][...] = kk
            b_scr[ci][...] = _ldot3(cs[...], g)

        def chunk(i, carry):
            _scan_chunks_fwd([((n_ch - 1 - i) if rev else i, rev, q, v, k_scr[ci], b_scr[ci], state[ci], o)
                              for rev, q, z, v, lb, cs, o, st, ci in chains], seg_ref, bd_ref, sel_ref)
            return carry

        lax.fori_loop(0, n_ch, chunk, 0)

    def col(c, rev):
        return pl.BlockSpec((b, HBLK, D_HGRN), lambda j: (0, (nb - 1 - j) if rev else j, c))

    def st_spec(rev):
        return pl.BlockSpec((b, None, D_HGRN, D_HGRN), lambda j: (0, (nb - 1 - j) if rev else j, 0, 0))

    sq = (D_HGRN, D_HGRN)
    proj3 = proj.reshape(b, s, proj.shape[1])
    o_fw, o_bw, st_fw, st_bw = pl.pallas_call(
        body, name=name, grid=(nb,),
        in_specs=[col(COL_HQ, False), col(COL_FF, False), col(COL_HI, False),
                  col(COL_HQ, True), col(COL_FB, True), col(COL_HI, True),
                  _full((1, D_HGRN)), _full((1, D_HGRN)), _full((HBLK, HBLK)), _full((HBLK, HBLK)),
                  _full(sq), _full(sq), _full((CHUNK, CHUNK * CHUNK))],
        out_specs=[col(0, False), col(0, True), st_spec(False), st_spec(True)],
        out_shape=[_sds((b, s, D_HGRN), F32)] * 2 + [_sds((b, nb) + sq, F32)] * 2,
        scratch_shapes=[pltpu.VMEM(sq, F32)] * n_chain + [pltpu.VMEM((HBLK, D_HGRN), F32)] * (2 * n_chain),
        compiler_params=_params(("arbitrary",)),
    )(proj3, proj3, proj3, proj3, proj3, proj3, lb_f, lb_b, hc_f["cs"], hc_b["cs"], hc_f["seg"], hc_f["bd"],
      hc_f["sel"])
    return o_fw.reshape(t, D_HGRN), o_bw.reshape(t, D_HGRN), st_fw, st_bw


def _scan_replay(c, rev, st, v_ref, k_scr, b_scr, states, bd_ref):
    last = 0 if rev else CHUNK - 1
    rows = pl.ds(pl.multiple_of(c * CHUNK, CHUNK), CHUNK)
    states[c] = st
    bb = b_scr[rows, :]
    bl = bb[last:last + 1, :]
    ke = k_scr[rows, :] * jnp.exp(bl - bb)
    return st * jnp.exp(bl) + bd_ref[...] * _dot_tn(v_ref[rows, :].astype(BF16), ke.astype(BF16))


def _scan_replays(chains, bd_ref):
    work = []
    for c, rev, st, v_ref, k_scr, b_scr, states in chains:
        last = 0 if rev else CHUNK - 1
        rows = pl.ds(pl.multiple_of(c * CHUNK, CHUNK), CHUNK)
        states[c] = st
        bb = b_scr[rows, :]
        bl = bb[last:last + 1, :]
        work.append((st, jnp.exp(bl), v_ref[rows, :].astype(BF16), (k_scr[rows, :] * jnp.exp(bl - bb)).astype(BF16)))
    upds = [_dot_tn(v, ke) for _, _, v, ke in work]
    return tuple(st * decay + bd_ref[...] * upd for (st, decay, _, _), upd in zip(work, upds))


def _scan_chunks_bwd(chains, seg_ref, bd_ref, sel_ref, selt_ref, seld_ref):
    row = lax.broadcasted_iota(jnp.int32, (CHUNK, D_HGRN), 0)
    work = []
    for c, rev, q_ref, v_ref, do_ref, k_scr, b_scr, states, dstate, dq_ref, dv_ref, dk_scr, db_scr in chains:
        last = 0 if rev else CHUNK - 1
        rows = pl.ds(pl.multiple_of(c * CHUNK, CHUNK), CHUNK)
        q = q_ref[rows, :]
        k = k_scr[rows, :]
        v = v_ref[rows, :]
        bb = b_scr[rows, :]
        dout = do_ref[rows, :]
        bl = bb[last:last + 1, :]
        st_p = states[c]
        dst_n = dstate[...]
        eb = jnp.exp(bb)
        ebl = jnp.exp(bl - bb)
        qe = q * eb
        ke = k * ebl
        dec = _pair_decay(bb, rev)
        q_rep = _rows_rep(q)
        k_til = _tile_rows(k)
        do_rep = _rows_rep(dout)
        pairs = dec * q_rep * k_til
        work.append(dict(
            rows=rows, last=last, eb=eb, ebl=ebl, ebl_last=jnp.exp(bl), qe=qe, ke=ke, dec=dec, q_rep=q_rep, k_til=k_til,
            do_rep=do_rep, pairs=pairs, st_p=st_p, dst_n=dst_n, dstate=dstate, dq_ref=dq_ref, dv_ref=dv_ref,
            dk_scr=dk_scr, db_scr=db_scr, dob=dout.astype(BF16), dstb=dst_n.astype(BF16), vb=v.astype(BF16),
            pairs_b=pairs.astype(BF16), vdo_b=(_tile_rows(v) * do_rep).astype(BF16)))
    for w in work:
        w["dqe"] = jnp.dot(w["dob"], w["st_p"].astype(BF16), preferred_element_type=F32)
        w["dke"] = jnp.dot(w["vb"], w["dstb"], preferred_element_type=F32)
        w["dv"] = _dot_nt(w["ke"].astype(BF16), w["dstb"])
        w["a"] = jnp.dot(w["pairs_b"], seg_ref[...], preferred_element_type=F32)
        w["wb"] = jnp.dot(w["vdo_b"], seg_ref[...], preferred_element_type=F32)
        w["dst_upd"] = _dot_tn(w["dob"], w["qe"].astype(BF16))
    for w in work:
        gdec = w["wb"] * w["dec"]
        w["x_dq"] = (gdec * w["k_til"]).astype(BF16)
        w["x_dk"] = (gdec * w["q_rep"]).astype(BF16)
        w["x_dv"] = (w["a"] * w["do_rep"]).astype(BF16)
        w["x_db"] = (w["wb"] * w["pairs"]).astype(BF16)
    for w in work:
        dke, dqe = w["dke"], w["dqe"]
        dbl = (jnp.sum(w["dst_n"] * w["st_p"], axis=0, keepdims=True) * w["ebl_last"]
               + jnp.sum(dke * w["ke"], axis=0, keepdims=True))
        dq = dqe * w["eb"] + jnp.dot(sel_ref[...], w["x_dq"], preferred_element_type=F32)
        dk = dke * w["ebl"] + jnp.dot(selt_ref[...], w["x_dk"], preferred_element_type=F32)
        dv = w["dv"] + jnp.dot(selt_ref[...], w["x_dv"], preferred_element_type=F32)
        db = (dqe * w["qe"] - dke * w["ke"] + jnp.dot(seld_ref[...], w["x_db"], preferred_element_type=F32)
              + jnp.where(row == w["last"], dbl, 0.0))
        w["dq_ref"][w["rows"], :] = dq
        w["dv_ref"][w["rows"], :] = dv
        w["dk_scr"][w["rows"], :] = dk
        w["db_scr"][w["rows"], :] = db
        w["dstate"][...] = w["dst_n"] * w["ebl_last"] + bd_ref[...] * w["dst_upd"]


def _scan_chunk_bwd(c, rev, q_ref, v_ref, do_ref, k_scr, b_scr, states, dstate, dq_ref, dv_ref, dk_scr, db_scr,
                    seg_ref, bd_ref, sel_ref, selt_ref, seld_ref):
    last = 0 if rev else CHUNK - 1
    row = lax.broadcasted_iota(jnp.int32, (CHUNK, D_HGRN), 0)
    rows = pl.ds(pl.multiple_of(c * CHUNK, CHUNK), CHUNK)
    q = q_ref[rows, :]
    k = k_scr[rows, :]
    v = v_ref[rows, :]
    bb = b_scr[rows, :]
    dout = do_ref[rows, :]
    bl = bb[last:last + 1, :]
    st_p = states[c]
    dst_n = dstate[...]
    eb = jnp.exp(bb)
    ebl = jnp.exp(bl - bb)
    ebl_last = jnp.exp(bl)
    qe = q * eb
    ke = k * ebl
    dob = dout.astype(BF16)
    dstb = dst_n.astype(BF16)
    dqe = jnp.dot(dob, st_p.astype(BF16), preferred_element_type=F32)
    dke = jnp.dot(v.astype(BF16), dstb, preferred_element_type=F32)
    dv = _dot_nt(ke.astype(BF16), dstb)
    dbl = jnp.sum(dst_n * st_p, axis=0, keepdims=True) * ebl_last + jnp.sum(dke * ke, axis=0, keepdims=True)
    dq = dqe * eb
    dk = dke * ebl
    db = dqe * qe - dke * ke
    dec = _pair_decay(bb, rev)
    q_rep = _rows_rep(q)
    k_til = _tile_rows(k)
    do_rep = _rows_rep(dout)
    pairs = dec * q_rep * k_til
    a = jnp.dot(pairs.astype(BF16), seg_ref[...], preferred_element_type=F32)
    wb = jnp.dot((_tile_rows(v) * do_rep).astype(BF16), seg_ref[...], preferred_element_type=F32)
    gdec = wb * dec
    dq = dq + jnp.dot(sel_ref[...], (gdec * k_til).astype(BF16), preferred_element_type=F32)
    dk = dk + jnp.dot(selt_ref[...], (gdec * q_rep).astype(BF16), preferred_element_type=F32)
    dv = dv + jnp.dot(selt_ref[...], (a * do_rep).astype(BF16), preferred_element_type=F32)
    db = db + jnp.dot(seld_ref[...], (wb * pairs).astype(BF16), preferred_element_type=F32)
    db = db + jnp.where(row == last, dbl, 0.0)
    dq_ref[rows, :] = dq
    dv_ref[rows, :] = dv
    dk_scr[rows, :] = dk
    db_scr[rows, :] = db
    dstate[...] = dst_n * ebl_last + bd_ref[...] * _dot_tn(dob, qe.astype(BF16))


def _hgrn_bwd2(name, proj, lb_f, lb_b, st_f, st_b, do, b, s, hc_f, hc_b):
    t = proj.shape[0]
    nb = s // HBLK
    n_ch = HBLK // CHUNK

    n_chain = 2 * b

    def body(qf_ref, zf_ref, vf_ref, dof_ref, stf_ref, qb_ref, zb_ref, vb_ref, dob_ref, stb_ref, lbf_ref, lbb_ref,
             csf_ref, csb_ref, cstf_ref, cstb_ref, seg_ref, bd_ref, sel_ref, selt_ref, seld_ref,
             dqf_ref, dvf_ref, dzf_ref, dqb_ref, dvb_ref, dzb_ref, dlbf_ref, dlbb_ref,
             *scr):
        dstate, states, b_scr, k_scr, db_scr, dk_scr = [scr[i * n_chain:(i + 1) * n_chain] for i in range(6)]

        @pl.when(pl.program_id(0) == 0)
        def _():
            dlbf_ref[...] = jnp.zeros_like(dlbf_ref)
            dlbb_ref[...] = jnp.zeros_like(dlbb_ref)
            for d0 in dstate:
                d0[...] = jnp.zeros_like(d0)

        chains = []
        for bi in range(b):
            chains.append(dict(rev=False, q=qf_ref.at[bi], z=zf_ref.at[bi], v=vf_ref.at[bi], do=dof_ref.at[bi],
                               st=stf_ref.at[bi], lb=lbf_ref, cs=csf_ref, cst=cstf_ref, dq=dqf_ref.at[bi],
                               dv=dvf_ref.at[bi], dz=dzf_ref.at[bi], dlb=dlbf_ref, ci=2 * bi))
            chains.append(dict(rev=True, q=qb_ref.at[bi], z=zb_ref.at[bi], v=vb_ref.at[bi], do=dob_ref.at[bi],
                               st=stb_ref.at[bi], lb=lbb_ref, cs=csb_ref, cst=cstb_ref, dq=dqb_ref.at[bi],
                               dv=dvb_ref.at[bi], dz=dzb_ref.at[bi], dlb=dlbb_ref, ci=2 * bi + 1))
        for ch in chains:
            sig, f, g, sn, kk = _gates(ch["z"][...], ch["lb"][...])
            k_scr[ch["ci"]][...] = kk
            b_scr[ch["ci"]][...] = _ldot3(ch["cs"][...], g)
            ch["gates"] = (sig, f, sn)

        def replay(i, carry):
            return _scan_replays([((n_ch - 1 - i) if ch["rev"] else i, ch["rev"], st, ch["v"], k_scr[ch["ci"]],
                                   b_scr[ch["ci"]], states[ch["ci"]]) for ch, st in zip(chains, carry)], bd_ref)

        lax.fori_loop(0, n_ch, replay, tuple(ch["st"][...] for ch in chains))

        def chunk(i, carry):
            args = [(i if ch["rev"] else (n_ch - 1 - i), ch["rev"], ch["q"], ch["v"], ch["do"],
                     k_scr[ch["ci"]], b_scr[ch["ci"]], states[ch["ci"]], dstate[ch["ci"]], ch["dq"],
                     ch["dv"], dk_scr[ch["ci"]], db_scr[ch["ci"]]) for ch in chains]
            for g0 in range(0, n_chain, BWD_GROUP):
                _scan_chunks_bwd(args[g0:g0 + BWD_GROUP], seg_ref, bd_ref, sel_ref, selt_ref, seld_ref)
            return carry

        lax.fori_loop(0, n_ch, chunk, 0)
        for ch in chains:
            sig, f, sn = ch["gates"]
            lbv = ch["lb"][...]
            hi, lo = _split2(db_scr[ch["ci"]][...])
            dg = (jnp.dot(ch["cst"][...], hi, preferred_element_type=F32)
                  + jnp.dot(ch["cst"][...], lo, preferred_element_type=F32))
            dgf = jnp.where(f > F_MIN, dg / f, 0.0)
            dk = dk_scr[ch["ci"]][...]
            ch["dz"][...] = dgf * (1.0 - lbv) * sig * (1.0 - sig) - dk * (1.0 - lbv) * sn * (1.0 - sn)
            ch["dlb"][...] += _rowgroups(dgf * (1.0 - sig) - dk * sn)

    def col(c, rev):
        return pl.BlockSpec((b, HBLK, D_HGRN), lambda j: (0, j if rev else (nb - 1 - j), c))

    def st_spec(rev):
        return pl.BlockSpec((b, None, D_HGRN, D_HGRN), lambda j: (0, j if rev else (nb - 1 - j), 0, 0))

    sq = (D_HGRN, D_HGRN)
    blk = (HBLK, D_HGRN)
    pairs_shape = (CHUNK, CHUNK * CHUNK)
    proj3 = proj.reshape(b, s, proj.shape[1])
    do3 = do.reshape(b, s, D_HGRN)
    res = pl.pallas_call(
        body, name=name, grid=(nb,),
        in_specs=[col(COL_HQ, False), col(COL_FF, False), col(COL_HI, False), col(0, False), st_spec(False),
                  col(COL_HQ, True), col(COL_FB, True), col(COL_HI, True), col(0, True), st_spec(True),
                  _full((1, D_HGRN)), _full((1, D_HGRN)), _full((HBLK, HBLK)), _full((HBLK, HBLK)),
                  _full((HBLK, HBLK)), _full((HBLK, HBLK)), _full(sq), _full(sq),
                  _full(pairs_shape), _full(pairs_shape), _full(pairs_shape)],
        out_specs=[col(0, False)] * 3 + [col(0, True)] * 3 + [_full((8, D_HGRN))] * 2,
        out_shape=[_sds((b, s, D_HGRN), F32)] * 6 + [_sds((8, D_HGRN), F32)] * 2,
        scratch_shapes=[pltpu.VMEM(sq, F32)] * n_chain + [pltpu.VMEM((n_ch,) + sq, F32)] * n_chain
        + [pltpu.VMEM(blk, F32)] * (4 * n_chain),
        compiler_params=_params(("arbitrary",)),
    )(proj3, proj3, proj3, do3, st_f, proj3, proj3, proj3, do3, st_b, lb_f, lb_b, hc_f["cs"], hc_b["cs"],
      hc_f["cs_t"], hc_b["cs_t"], hc_f["seg"], hc_f["bd"], hc_f["sel"], hc_f["selt"], hc_f["seld"])
    return [r.reshape(t, D_HGRN) for r in res[:6]] + list(res[6:])


def _lower_bounds(logits):
    n = logits.shape[1]

    def body(x_ref, o_ref):
        x = x_ref[...]
        for d in range(2):
            rows = [x[l * 2 + d:l * 2 + d + 1, :] for l in range(DEPTH)]
            mx = functools.reduce(jnp.maximum, rows)
            ex = [jnp.exp(r - mx) for r in rows]
            tot = functools.reduce(lambda a, c: a + c, ex)
            sm = [e / tot for e in ex]
            run = jnp.zeros_like(sm[0])
            for l in range(DEPTH):
                run = run + sm[l]
                o_ref[l * 2 + d:l * 2 + d + 1, :] = run - sm[0]

    return pl.pallas_call(body, name="hgrn_lower_bounds", out_shape=_sds(logits.shape, F32),
                          in_specs=[_full(logits.shape)], out_specs=_full(logits.shape), grid=(1,),
                          compiler_params=_params(("arbitrary",)))(logits)


def _lower_bounds_bwd(logits, dlb):
    def body(x_ref, g_ref, o_ref):
        x = x_ref[...]
        gv = g_ref[...]
        for d in range(2):
            rows = [x[l * 2 + d:l * 2 + d + 1, :] for l in range(DEPTH)]
            gr = [gv[l * 2 + d:l * 2 + d + 1, :] for l in range(DEPTH)]
            mx = functools.reduce(jnp.maximum, rows)
            ex = [jnp.exp(r - mx) for r in rows]
            tot = functools.reduce(lambda a, c: a + c, ex)
            sm = [e / tot for e in ex]
            dsm = []
            for i in range(DEPTH):
                acc = functools.reduce(lambda a, c: a + c, gr[i:])
                if i == 0:
                    acc = acc - functools.reduce(lambda a, c: a + c, gr)
                dsm.append(acc)
            inner = functools.reduce(lambda a, c: a + c, [sm[i] * dsm[i] for i in range(DEPTH)])
            for i in range(DEPTH):
                o_ref[i * 2 + d:i * 2 + d + 1, :] = sm[i] * (dsm[i] - inner)

    return pl.pallas_call(body, name="hgrn_lower_bounds_bwd", out_shape=_sds(logits.shape, F32),
                          in_specs=[_full(logits.shape), _full(logits.shape)], out_specs=_full(logits.shape),
                          grid=(1,), compiler_params=_params(("arbitrary",)))(logits, dlb)


def _conv_rows(s):
    return s + 2 * (CONV_PAD + 1)


def _conv_fwd(name, proj, dw_w, dw_b, ln_w, ln_b, pw_w, pw_b, b, s):
    t = proj.shape[0]
    pad = CONV_PAD + 1
    nt = s // CONV_TILE

    def body(a_ref, g_ref, w_ref, dwb_ref, lnw_ref, lnb_ref, pw_ref, pwb_ref, y_ref, c_ref, upad, win):
        upad[0:pad, :] = jnp.zeros((pad, D_CONV), F32)
        upad[s + pad:s + 2 * pad, :] = jnp.zeros((pad, D_CONV), F32)

        def fill(i, carry):
            rows = pl.ds(pl.multiple_of(i * CONV_TILE, CONV_TILE), CONV_TILE)
            upad[pl.ds(pl.multiple_of(i * CONV_TILE + pad, pad), CONV_TILE), :] = a_ref[rows, :] * _sigmoid(g_ref[rows, :])
            return carry

        lax.fori_loop(0, nt, fill, 0)

        def tile(i, carry):
            r0 = pl.multiple_of(i * CONV_TILE, CONV_TILE)
            win[...] = upad[pl.ds(r0, CONV_TILE + 2 * pad), :]
            acc = jnp.zeros((CONV_TILE, D_CONV), F32)
            for j in range(CONV_W):
                acc = acc + win[j + 1:j + 1 + CONV_TILE, :] * w_ref[j:j + 1, :]
            c = acc + dwb_ref[...]
            c_ref[pl.ds(r0, CONV_TILE), :] = c
            mu = jnp.mean(c, axis=-1, keepdims=True)
            xc = c - mu
            rstd = lax.rsqrt(jnp.mean(xc * xc, axis=-1, keepdims=True) + LN_EPS)
            n = xc * rstd * lnw_ref[...] + lnb_ref[...]
            y_ref[pl.ds(r0, CONV_TILE), :] = (jnp.dot(_silu(n).astype(BF16), pw_ref[...].astype(BF16),
                                                      preferred_element_type=F32) + pwb_ref[...])
            return carry

        lax.fori_loop(0, nt, tile, 0)

    vec = _full((1, D_CONV))
    return pl.pallas_call(
        body, name=name, grid=(b,),
        in_specs=[pl.BlockSpec((s, D_CONV), lambda bi: (bi, COL_CA)), pl.BlockSpec((s, D_CONV), lambda bi: (bi, COL_CB)),
                  _full((CONV_W + 1, D_CONV)), vec, vec, vec, _full((D_CONV, D_CONV)), vec],
        out_specs=[pl.BlockSpec((s, D_CONV), lambda bi: (bi, 0))] * 2, out_shape=[_sds((t, D_CONV), F32)] * 2,
        scratch_shapes=[pltpu.VMEM((_conv_rows(s), D_CONV), F32), pltpu.VMEM((CONV_TILE + 2 * pad, D_CONV), F32)],
        compiler_params=_params(("parallel",)),
    )(proj, proj, dw_w, dw_b, ln_w, ln_b, pw_w, pw_b)


def _conv_bwd(name, proj, conv_out, dw_w, ln_w, ln_b, pw_w, dy, b, s):
    t = proj.shape[0]
    pad = CONV_PAD + 1
    nt = s // CONV_TILE

    def body(a_ref, g_ref, c_ref, w_ref, lnw_ref, lnb_ref, pw_ref, dy_ref, dab_ref, dpw_ref, ddw_ref, dvec_ref,
             upad, dcpad, tap_acc, win, dwin):
        @pl.when(pl.program_id(0) == 0)
        def _():
            dpw_ref[...] = jnp.zeros_like(dpw_ref)
            ddw_ref[...] = jnp.zeros_like(ddw_ref)
            dvec_ref[...] = jnp.zeros_like(dvec_ref)

        zeros = jnp.zeros((pad, D_CONV), F32)
        upad[0:pad, :] = zeros
        upad[s + pad:s + 2 * pad, :] = zeros
        dcpad[0:pad, :] = zeros
        dcpad[s + pad:s + 2 * pad, :] = zeros
        tap_acc[...] = jnp.zeros_like(tap_acc)

        def inner(i):
            return pl.ds(pl.multiple_of(i * CONV_TILE + pad, pad), CONV_TILE)

        def fill(i, carry):
            rows = pl.ds(pl.multiple_of(i * CONV_TILE, CONV_TILE), CONV_TILE)
            upad[inner(i), :] = a_ref[rows, :] * _sigmoid(g_ref[rows, :])
            return carry

        lax.fori_loop(0, nt, fill, 0)

        def tile_a(i, carry):
            r0 = pl.multiple_of(i * CONV_TILE, CONV_TILE)
            c = c_ref[pl.ds(r0, CONV_TILE), :]
            mu = jnp.mean(c, axis=-1, keepdims=True)
            xc = c - mu
            rstd = lax.rsqrt(jnp.mean(xc * xc, axis=-1, keepdims=True) + LN_EPS)
            xhat = xc * rstd
            n = xhat * lnw_ref[...] + lnb_ref[...]
            dyt = dy_ref[pl.ds(r0, CONV_TILE), :]
            dyb = dyt.astype(BF16)
            dpw_ref[...] += _dot_tn(_silu(n).astype(BF16), dyb)
            dn = _dot_nt(dyb, pw_ref[...].astype(BF16)) * _dsilu(n)
            dxh = dn * lnw_ref[...]
            dc = rstd * (dxh - jnp.mean(dxh, axis=-1, keepdims=True)
                         - xhat * jnp.mean(dxh * xhat, axis=-1, keepdims=True))
            dcpad[inner(i), :] = dc
            dvec_ref[0:1, :] += jnp.sum(dyt, axis=0, keepdims=True)
            dvec_ref[1:2, :] += jnp.sum(dn * xhat, axis=0, keepdims=True)
            dvec_ref[2:3, :] += jnp.sum(dn, axis=0, keepdims=True)
            dvec_ref[3:4, :] += jnp.sum(dc, axis=0, keepdims=True)
            return carry

        lax.fori_loop(0, nt, tile_a, 0)

        def tile_b(i, carry):
            r0 = pl.multiple_of(i * CONV_TILE, CONV_TILE)
            dwin[...] = dcpad[pl.ds(r0, CONV_TILE + 2 * pad), :]
            ut = upad[pl.ds(pl.multiple_of(r0 + pad, pad), CONV_TILE), :]
            du = jnp.zeros((CONV_TILE, D_CONV), F32)
            for j in range(CONV_W):
                shifted = dwin[2 * pad - 1 - j:2 * pad - 1 - j + CONV_TILE, :]
                du = du + shifted * w_ref[j:j + 1, :]
                tap_acc[8 * j:8 * j + 8, :] += _rowgroups(shifted * ut)
            rows = pl.ds(r0, CONV_TILE)
            sg = _sigmoid(g_ref[rows, :])
            dab_ref[rows, 0:D_CONV] = (du * sg).astype(BF16)
            dab_ref[rows, D_CONV:2 * D_CONV] = (du * a_ref[rows, :] * sg * (1.0 - sg)).astype(BF16)
            return carry

        lax.fori_loop(0, nt, tile_b, 0)
        for j in range(CONV_W):
            ddw_ref[j:j + 1, :] += jnp.sum(tap_acc[8 * j:8 * j + 8, :], axis=0, keepdims=True)

    vec = _full((1, D_CONV))
    return pl.pallas_call(
        body, name=name, grid=(b,),
        in_specs=[pl.BlockSpec((s, D_CONV), lambda bi: (bi, COL_CA)), pl.BlockSpec((s, D_CONV), lambda bi: (bi, COL_CB)),
                  pl.BlockSpec((s, D_CONV), lambda bi: (bi, 0)),
                  _full((CONV_W + 1, D_CONV)), vec, vec, _full((D_CONV, D_CONV)),
                  pl.BlockSpec((s, D_CONV), lambda bi: (bi, 0))],
        out_specs=[pl.BlockSpec((s, 2 * D_CONV), lambda bi: (bi, 0)), _full((D_CONV, D_CONV)),
                   _full((CONV_W + 1, D_CONV)), _full((8, D_CONV))],
        out_shape=[_sds((t, 2 * D_CONV), BF16), _sds((D_CONV, D_CONV), F32), _sds((CONV_W + 1, D_CONV), F32),
                   _sds((8, D_CONV), F32)],
        scratch_shapes=[pltpu.VMEM((_conv_rows(s), D_CONV), F32), pltpu.VMEM((_conv_rows(s), D_CONV), F32),
                        pltpu.VMEM((8 * CONV_W, D_CONV), F32), pltpu.VMEM((CONV_TILE + 2 * pad, D_CONV), F32),
                        pltpu.VMEM((CONV_TILE + 2 * pad, D_CONV), F32)],
        compiler_params=_params(("arbitrary",)),
    )(proj, proj, conv_out, dw_w, ln_w, ln_b, pw_w, dy)


def _mix_fwd(name, y_attn, o_fw, o_bw, proj, y_conv, aw, gw, cw, seg):
    t = y_attn.shape[0]
    tm = _row_tile(t)

    def body(ya_ref, of_ref, ob_ref, hg_ref, yc_ref, aw_ref, gw_ref, cw_ref, seg_ref, o_ref):
        ya = ya_ref[...]
        ra = lax.rsqrt(jnp.mean(ya * ya, axis=-1, keepdims=True) + EPS)
        o_ref[:, 0:D_ATTN] = (ya * ra * aw_ref[...]).astype(BF16)
        o = of_ref[...] + ob_ref[...]
        ro = lax.rsqrt(jnp.dot((o * o).astype(BF16), seg_ref[...], preferred_element_type=F32) + EPS)
        o_ref[:, D_ATTN:D_ATTN + D_HGRN] = (o * ro * gw_ref[...] * _silu(hg_ref[...])).astype(BF16)
        yc = yc_ref[...]
        rc = lax.rsqrt(jnp.mean(yc * yc, axis=-1, keepdims=True) + EPS)
        o_ref[:, D_ATTN + D_HGRN:D_MODEL] = (yc * rc * cw_ref[...]).astype(BF16)

    def tile(w, c=0):
        return pl.BlockSpec((tm, w), lambda i: (i, c))

    return pl.pallas_call(
        body, name=name, grid=(t // tm,),
        in_specs=[tile(D_ATTN), tile(D_HGRN), tile(D_HGRN), tile(D_HGRN, COL_HG), tile(D_CONV),
                  _full((1, D_ATTN)), _full((1, D_HGRN)), _full((1, D_CONV)), _full((D_HGRN, D_HGRN))],
        out_specs=tile(D_MODEL), out_shape=_sds((t, D_MODEL), BF16),
        compiler_params=_params(("parallel",)),
    )(y_attn, o_fw, o_bw, proj, y_conv, aw, gw, cw, seg)


def _mix_bwd(name, dmix, y_attn, o_fw, o_bw, proj, y_conv, aw, gw, cw, seg, deps=()):
    t = y_attn.shape[0]
    tm = _row_tile(t)

    def rms_bwd(x, w, dy):
        r = lax.rsqrt(jnp.mean(x * x, axis=-1, keepdims=True) + EPS)
        gwv = dy * w
        return r * gwv - x * (r * r * r) * jnp.mean(gwv * x, axis=-1, keepdims=True), _rowgroups(dy * x * r)

    def body(dm_ref, ya_ref, of_ref, ob_ref, hg_ref, yc_ref, aw_ref, gw_ref, cw_ref, seg_ref, *rest):
        dya_ref, do_ref, dhg_ref, dyc_ref, daw_ref, dgw_ref, dcw_ref = rest[-7:]

        @pl.when(pl.program_id(0) == 0)
        def _():
            daw_ref[...] = jnp.zeros_like(daw_ref)
            dgw_ref[...] = jnp.zeros_like(dgw_ref)
            dcw_ref[...] = jnp.zeros_like(dcw_ref)

        dya, daw = rms_bwd(ya_ref[...], aw_ref[...], dm_ref[:, 0:D_ATTN])
        dya_ref[...] = dya
        daw_ref[...] += daw
        dyc, dcw = rms_bwd(yc_ref[...], cw_ref[...], dm_ref[:, D_ATTN + D_HGRN:D_MODEL])
        dyc_ref[...] = dyc
        dcw_ref[...] += dcw
        d2 = dm_ref[:, D_ATTN:D_ATTN + D_HGRN]
        o = of_ref[...] + ob_ref[...]
        hg = hg_ref[...]
        ro = lax.rsqrt(jnp.dot((o * o).astype(BF16), seg_ref[...], preferred_element_type=F32) + EPS)
        dn = d2 * _silu(hg)
        dhg_ref[...] = (d2 * o * ro * gw_ref[...] * _dsilu(hg)).astype(BF16)
        gwv = dn * gw_ref[...]
        do_ref[...] = ro * gwv - o * (ro * ro * ro) * _rdot2(gwv * o, seg_ref[...])
        dgw_ref[...] += _rowgroups(dn * o * ro)

    def tile(w, c=0):
        return pl.BlockSpec((tm, w), lambda i: (i, c))

    return pl.pallas_call(
        body, name=name, grid=(t // tm,),
        in_specs=[tile(D_MODEL), tile(D_ATTN), tile(D_HGRN), tile(D_HGRN), tile(D_HGRN, COL_HG), tile(D_CONV),
                  _full((1, D_ATTN)), _full((1, D_HGRN)), _full((1, D_CONV)), _full((D_HGRN, D_HGRN))]
        + [_full(a.shape) for a in deps],
        out_specs=[tile(D_ATTN), tile(D_HGRN), tile(D_HGRN), tile(D_CONV),
                   _full((8, D_ATTN)), _full((8, D_HGRN)), _full((8, D_CONV))],
        out_shape=[_sds((t, D_ATTN), F32), _sds((t, D_HGRN), F32), _sds((t, D_HGRN), BF16), _sds((t, D_CONV), F32),
                   _sds((8, D_ATTN), F32), _sds((8, D_HGRN), F32), _sds((8, D_CONV), F32)],
        compiler_params=_params(("arbitrary",)),
    )(dmix, y_attn, o_fw, o_bw, proj, y_conv, aw, gw, cw, seg, *deps)


def _dproj(name, dp_attn, dq_f, dq_b, dz_fw, dz_bw, dv_f, dv_b, dhg, dp_conv):
    t = dq_f.shape[0]
    tm = _row_tile(t)
    wa, wc = dp_attn.shape[1], dp_conv.shape[1]

    def body(at_ref, qf_ref, qb_ref, zf_ref, zb_ref, vf_ref, vb_ref, hg_ref, cv_ref, o_ref):
        o_ref[:, 0:wa] = at_ref[...]
        cols = (qf_ref[...] + qb_ref[...], zf_ref[...], zb_ref[...], vf_ref[...] + vb_ref[...], hg_ref[...])
        for i, val in enumerate(cols):
            o_ref[:, wa + i * D_HGRN:wa + (i + 1) * D_HGRN] = val.astype(BF16)
        o_ref[:, wa + 5 * D_HGRN:D_IN] = cv_ref[...]

    tile = lambda w: pl.BlockSpec((tm, w), lambda i: (i, 0))
    return pl.pallas_call(
        body, name=name, grid=(t // tm,), in_specs=[tile(wa)] + [tile(D_HGRN)] * 7 + [tile(wc)],
        out_specs=tile(D_IN), out_shape=_sds((t, D_IN), BF16), compiler_params=_params(("parallel",)),
    )(dp_attn, dq_f, dq_b, dz_fw, dz_bw, dv_f, dv_b, dhg, dp_conv)


def _mm_tile(t):
    return min(512, t)


def _resident(shape):
    n = len(shape)
    return pl.BlockSpec(tuple(shape), lambda *_: (0,) * n, pipeline_mode=pl.Buffered(1))


def _w_blk(rows, cols, j_of):
    return pl.BlockSpec((None, rows, cols), lambda *g: (j_of(*g), 0, 0))


def _layer_fwd(l, x, wget, sm, tabs, cst, b, s, deps, target=None):
    t = x.shape[0]
    tm = _mm_tile(t)
    nt = t // tm
    pre = "l%d_" % l
    row = lambda w: pl.BlockSpec((tm, w), lambda i, *_: (i, 0))

    def normed(x_ref, nw_ref):
        xv = x_ref[...]
        r = lax.rsqrt(jnp.mean(xv * xv, axis=-1, keepdims=True) + EPS)
        return (xv * r * nw_ref[...]).astype(BF16)

    def in_body(x_ref, nw_ref, w_ref, *rest):
        o_ref, h_ref = rest[-2:]
        hv = normed(x_ref, nw_ref)
        h_ref[...] = hv
        for j in range(N_CHIP):
            o_ref[:, j * IN_BLK:(j + 1) * IN_BLK] = jnp.dot(hv, w_ref[j], preferred_element_type=F32)

    w_in = wget(l, "w_in", x)
    proj, h1 = pl.pallas_call(
        in_body, name=pre + "in_proj", grid=(nt,),
        in_specs=[row(D_MODEL), _full((1, D_MODEL)), _resident(w_in.shape)] + [_full(a.shape) for a in deps],
        out_specs=[row(D_IN), row(D_MODEL)], out_shape=[_sds((t, D_IN), F32), _sds((t, D_MODEL), BF16)],
        compiler_params=_params(("parallel",)),
    )(x, sm["mix_norm_w"][l], w_in, *deps)
    qn, kr, vr = _attn_prep(pre + "attn_prep", proj, s, tabs, sm["q_norm_w"][l], sm["k_norm_w"][l], cst["attn"])
    y_attn = _attn_fwd(pre + "attn", qn, kr, vr, b, s)
    o_fw, o_bw, st_fw, st_bw = _hgrn_fwd2(pre + "hgrn", proj, sm["lb"][l][0], sm["lb"][l][1], b, s, cst["hg_fw"],
                                          cst["hg_bw"])
    y_conv, conv_out = _conv_fwd(pre + "conv", proj, sm["conv_dw_w"][l], sm["conv_dw_b"][l], sm["conv_ln_w"][l],
                       sm["conv_ln_b"][l], sm["conv_pw_w"][l], sm["conv_pw_b"][l], b, s)
    mixed = _mix_fwd(pre + "mix", y_attn, o_fw, o_bw, proj, y_conv, sm["attn_out_norm_w"][l], sm["gnorm_w"][l],
                     sm["conv_out_norm_w"][l], cst["seg_h"])
    (x1,) = _mm(pre + "out_proj", (nt,),
                [(mixed, row(D_MODEL), wget(l, "w_out", mixed),
                  pl.BlockSpec((N_CHIP, OUT_BLK, D_MODEL), lambda i: (0, 0, 0)), NN)],
                [(x, row(D_MODEL))], [(_sds((t, D_MODEL), F32), row(D_MODEL))],
                lambda tot, xr: (xr + tot,))
    ff3 = pl.BlockSpec((N_CHIP, tm, FF_BLK), lambda i: (0, i, 0))
    ffs = _sds((N_CHIP, t, FF_BLK), BF16)

    def gu_body(x_ref, nw_ref, wg_ref, wu_ref, g_ref, u_ref, a_ref, h_ref):
        hv = normed(x_ref, nw_ref)
        h_ref[...] = hv
        for j in range(N_CHIP):
            gv = jnp.dot(hv, wg_ref[j], preferred_element_type=F32)
            uv = jnp.dot(hv, wu_ref[j], preferred_element_type=F32)
            g_ref[j] = gv.astype(BF16)
            u_ref[j] = uv.astype(BF16)
            a_ref[j] = (_silu(gv) * uv).astype(BF16)

    w_gate, w_up = wget(l, "w_gate", x1), wget(l, "w_up", x1)
    gate, up, act, h2 = pl.pallas_call(
        gu_body, name=pre + "ffn_gate_up", grid=(nt,),
        in_specs=[row(D_MODEL), _full((1, D_MODEL)), _resident(w_gate.shape), _resident(w_up.shape)],
        out_specs=[ff3, ff3, ff3, row(D_MODEL)], out_shape=[ffs, ffs, ffs, _sds((t, D_MODEL), BF16)],
        compiler_params=_params(("parallel",)),
    )(x1, sm["ffn_norm_w"][l], w_gate, w_up)

    def down_body(a_ref, w_ref, x_ref, o_ref):
        tot = x_ref[...]
        for j in range(N_CHIP):
            tot = tot + jnp.dot(a_ref[j], w_ref[j], preferred_element_type=F32)
        o_ref[...] = tot

    def down_loss_body(a_ref, w_ref, x_ref, t_ref, dy_ref, acc_ref):
        tot = x_ref[...]
        for j in range(N_CHIP):
            tot = tot + jnp.dot(a_ref[j], w_ref[j], preferred_element_type=F32)
        e = tot - t_ref[...]
        dy_ref[...] = e * (1.0 / D_MODEL)

        @pl.when(pl.program_id(0) == 0)
        def _():
            acc_ref[...] = jnp.zeros_like(acc_ref)

        acc_ref[...] += _rowgroups(e * e)

    w_down = wget(l, "w_down", act)
    if target is None:
        x2 = pl.pallas_call(
            down_body, name=pre + "ffn_down", grid=(nt,), in_specs=[ff3, _resident(w_down.shape), row(D_MODEL)],
            out_specs=row(D_MODEL), out_shape=_sds((t, D_MODEL), F32), compiler_params=_params(("parallel",)),
        )(act, w_down, x1)
    else:
        x2 = pl.pallas_call(
            down_loss_body, name=pre + "ffn_down_loss", grid=(nt,),
            in_specs=[ff3, _resident(w_down.shape), row(D_MODEL), row(D_MODEL)],
            out_specs=[row(D_MODEL), _full((8, D_MODEL))],
            out_shape=[_sds((t, D_MODEL), F32), _sds((8, D_MODEL), F32)], compiler_params=_params(("arbitrary",)),
        )(act, w_down, x1, target)
    saved = dict(x=x, h1=h1, proj=proj, qn=qn, kr=kr, vr=vr, y_attn=y_attn, o_fw=o_fw, o_bw=o_bw, st_fw=st_fw,
                 st_bw=st_bw, y_conv=y_conv, conv_out=conv_out, mixed=mixed, x1=x1, h2=h2, gate=gate, up=up, act=act)
    return x2, saved


def _layer_bwd(l, dx2, sv, wget, sm, tabs, cst, b, s, on_grads):
    t = dx2.shape[0]
    tm = _mm_tile(t)
    nt = t // tm
    pre = "l%d_" % l
    tk = min(2048, t)
    nk = t // tk
    row = lambda w: pl.BlockSpec((tm, w), lambda i, *_: (i, 0))
    ff3 = pl.BlockSpec((N_CHIP, tm, FF_BLK), lambda i: (0, i, 0))
    ffs = _sds((N_CHIP, t, FF_BLK), BF16)

    w_down, w_gate, w_up = wget(l, "w_down", dx2), wget(l, "w_gate", dx2), wget(l, "w_up", dx2)

    def ddx_body(dx_ref, w_ref, g_ref, u_ref, dg_ref, du_ref):
        dxb = dx_ref[...].astype(BF16)
        for j in range(N_CHIP):
            da = _dot_nt(dxb, w_ref[j])
            g = g_ref[j].astype(F32)
            sg = _sigmoid(g)
            dg_ref[j] = (da * u_ref[j].astype(F32) * (sg * (1.0 + g * (1.0 - sg)))).astype(BF16)
            du_ref[j] = (da * (g * sg)).astype(BF16)

    dgate, dup = pl.pallas_call(
        ddx_body, name=pre + "ffn_down_dx", grid=(nt,), in_specs=[row(D_MODEL), _resident(w_down.shape), ff3, ff3],
        out_specs=[ff3, ff3], out_shape=[ffs, ffs], compiler_params=_params(("parallel",)),
    )(dx2, w_down, sv["gate"], sv["up"])
    colt = lambda w: pl.BlockSpec((tk, w), lambda j, k: (k, 0))
    fft = pl.BlockSpec((None, tk, FF_BLK), lambda j, k: (j, k, 0))
    (g_down,) = _mm(pre + "ffn_down_dw", (N_CHIP, nk), [(sv["act"], fft, dx2, colt(D_MODEL), TN)], [],
                    [(_sds((N_CHIP, FF_BLK, D_MODEL), BF16), pl.BlockSpec((None, FF_BLK, D_MODEL), lambda j, k: (j, 0, 0)))],
                    lambda tot: (tot,), acc=(1, (FF_BLK, D_MODEL)))
    wff = pl.BlockSpec((None, D_MODEL, FF_BLK), lambda j, k: (j, 0, 0))
    (g_gate,) = _mm(pre + "ffn_gate_dw", (N_CHIP, nk), [(sv["h2"], colt(D_MODEL), dgate, fft, TN)], [],
                    [(_sds((N_CHIP, D_MODEL, FF_BLK), BF16), wff)], lambda tot: (tot,), acc=(1, (D_MODEL, FF_BLK)))
    (g_up,) = _mm(pre + "ffn_up_dw", (N_CHIP, nk), [(sv["h2"], colt(D_MODEL), dup, fft, TN)], [],
                  [(_sds((N_CHIP, D_MODEL, FF_BLK), BF16), wff)], lambda tot: (tot,), acc=(1, (D_MODEL, FF_BLK)))

    def norm_bwd_tail(dh, x_ref, nw_ref, dres_ref, dx_ref, dw_ref):
        xv = x_ref[...]
        r = lax.rsqrt(jnp.mean(xv * xv, axis=-1, keepdims=True) + EPS)
        gw = dh * nw_ref[...]
        dx_ref[...] = dres_ref[...] + r * gw - xv * (r * r * r) * jnp.mean(gw * xv, axis=-1, keepdims=True)

        @pl.when(pl.program_id(0) == 0)
        def _():
            dw_ref[...] = jnp.zeros_like(dw_ref)

        dw_ref[...] += _rowgroups(dh * xv * r)

    def dh_body(dg_ref, du_ref, wg_ref, wu_ref, x_ref, nw_ref, dres_ref, *rest):
        tot = None
        for j in range(N_CHIP):
            r = _dot_nt(dg_ref[j], wg_ref[j]) + _dot_nt(du_ref[j], wu_ref[j])
            tot = r if tot is None else tot + r
        norm_bwd_tail(tot, x_ref, nw_ref, dres_ref, *rest[-2:])

    deps = on_grads(l, dict(w_gate=g_gate, w_up=g_up, w_down=g_down))
    dx1, d_ffn_norm = pl.pallas_call(
        dh_body, name=pre + "ffn_dh", grid=(nt,),
        in_specs=[ff3, ff3, _resident(w_gate.shape), _resident(w_up.shape), row(D_MODEL), _full((1, D_MODEL)),
                  row(D_MODEL)] + [_full(a.shape) for a in deps],
        out_specs=[row(D_MODEL), _full((8, D_MODEL))], out_shape=[_sds((t, D_MODEL), F32), _sds((8, D_MODEL), F32)],
        compiler_params=_params(("arbitrary",)),
    )(dgate, dup, w_gate, w_up, sv["x1"], sm["ffn_norm_w"][l], dx2, *deps)

    (dmix,) = _mm(pre + "out_proj_dx", (nt,),
                  [(dx1, row(D_MODEL), wget(l, "w_out", dx2),
                    pl.BlockSpec((N_CHIP, OUT_BLK, D_MODEL), lambda i: (0, 0, 0)), NT)], [],
                  [(_sds((t, D_MODEL), F32), row(D_MODEL))], lambda tot: (tot,))
    (g_out,) = _mm(pre + "out_proj_dw", (N_CHIP, nk),
                   [(sv["mixed"], pl.BlockSpec((tk, OUT_BLK), lambda j, k: (k, j)), dx1, colt(D_MODEL), TN)], [],
                   [(_sds((N_CHIP, OUT_BLK, D_MODEL), BF16), pl.BlockSpec((None, OUT_BLK, D_MODEL), lambda j, k: (j, 0, 0)))],
                   lambda tot: (tot,), acc=(1, (OUT_BLK, D_MODEL)))
    proj = sv["proj"]
    dya, do_h, dhg, dyc, d_aw, d_gw, d_cw = _mix_bwd(
        pre + "mix_bwd", dmix, sv["y_attn"], sv["o_fw"], sv["o_bw"], proj, sv["y_conv"],
        sm["attn_out_norm_w"][l], sm["gnorm_w"][l], sm["conv_out_norm_w"][l], cst["seg_h"],
        on_grads(l, dict(w_out=g_out)))
    dqs, dkr, dvr = _attn_bwd(pre + "attn_bwd", sv["qn"], sv["kr"], sv["vr"], dya, b, s)
    dp_attn, d_qw, d_kw = _attn_prep_bwd(pre + "attn_prep_bwd", proj, s, tabs, sm["q_norm_w"][l], sm["k_norm_w"][l],
                                         cst["attn"], dqs, dkr, dvr)
    dq_f, dv_f, dz_fw, dq_b, dv_b, dz_bw, dlb_fw, dlb_bw = _hgrn_bwd2(
        pre + "hgrn_bwd", proj, sm["lb"][l][0], sm["lb"][l][1], sv["st_fw"], sv["st_bw"], do_h, b, s,
        cst["hg_fw"], cst["hg_bw"])
    dp_conv, d_pw, d_dw, d_cvec = _conv_bwd(pre + "conv_bwd", proj, sv["conv_out"], sm["conv_dw_w"][l],
                                            sm["conv_ln_w"][l], sm["conv_ln_b"][l], sm["conv_pw_w"][l], dyc, b, s)
    dproj = _dproj(pre + "dproj", dp_attn, dq_f, dq_b, dz_fw, dz_bw, dv_f, dv_b, dhg, dp_conv)
    g_pw = d_pw.reshape(N_CHIP, D_CONV // N_CHIP, D_CONV).astype(BF16)

    (g_in,) = _mm(pre + "in_proj_dw", (N_CHIP, nk),
                  [(sv["h1"], colt(D_MODEL), dproj, pl.BlockSpec((tk, IN_BLK), lambda j, k: (k, j)), TN)], [],
                  [(_sds((N_CHIP, D_MODEL, IN_BLK), BF16), pl.BlockSpec((None, D_MODEL, IN_BLK), lambda j, k: (j, 0, 0)))],
                  lambda tot: (tot,), acc=(1, (D_MODEL, IN_BLK)))

    def indx_body(dp_ref, w_ref, x_ref, nw_ref, dres_ref, *rest):
        tot = None
        for j in range(N_CHIP):
            r = _dot_nt(dp_ref[:, j * IN_BLK:(j + 1) * IN_BLK], w_ref[j])
            tot = r if tot is None else tot + r
        norm_bwd_tail(tot, x_ref, nw_ref, dres_ref, *rest[-2:])

    w_in = wget(l, "w_in", dx2)
    deps = on_grads(l, dict(w_in=g_in, conv_pw_w=g_pw))
    dx, d_mix_norm = pl.pallas_call(
        indx_body, name=pre + "in_proj_dx", grid=(nt,),
        in_specs=[row(D_IN), _resident(w_in.shape), row(D_MODEL), _full((1, D_MODEL)), row(D_MODEL)]
        + [_full(a.shape) for a in deps],
        out_specs=[row(D_MODEL), _full((8, D_MODEL))], out_shape=[_sds((t, D_MODEL), F32), _sds((8, D_MODEL), F32)],
        compiler_params=_params(("arbitrary",)),
    )(dproj, w_in, sv["x"], sm["mix_norm_w"][l], dx1, *deps)
    heads = lambda v, n: v.sum(axis=0).reshape(n, HEAD_DIM).sum(axis=0)
    small = dict(
        mix_norm_w=d_mix_norm.sum(axis=0), q_norm_w=heads(d_qw, D_ATTN // HEAD_DIM), k_norm_w=heads(d_kw, N_KV),
        lb=jnp.stack([dlb_fw.sum(axis=0), dlb_bw.sum(axis=0)]), hgrn_gnorm_w=heads(d_gw, D_HGRN // HEAD_DIM),
        conv_dw_w=d_dw[:CONV_W], conv_dw_b=d_cvec[3], conv_ln_w=d_cvec[1], conv_ln_b=d_cvec[2],
        conv_pw_b=d_cvec[0], attn_out_norm_w=d_aw.sum(axis=0), conv_out_norm_w=d_cw.sum(axis=0),
        ffn_norm_w=d_ffn_norm.sum(axis=0))
    return dx, small


SMALL_ORDER = ("mix_norm_w", "q_norm_w", "k_norm_w", "lb", "hgrn_gnorm_w", "conv_dw_w", "conv_dw_b", "conv_ln_w",
               "conv_ln_b", "conv_pw_b", "attn_out_norm_w", "conv_out_norm_w", "ffn_norm_w")
BIG_ORDER = ("w_in", "w_out", "w_gate", "w_up", "w_down")
SCATTER_ORDER = BIG_ORDER + ("conv_pw_w",)


def _local_step(x, target, wget, sm, deps, on_grads):
    b, s, d = x.shape
    t = b * s
    cos, sin = _rope_tables(s)
    tabs = dict(cq=jnp.tile(cos, (1, D_ATTN // HEAD_DIM)), sq=jnp.tile(sin, (1, D_ATTN // HEAD_DIM)),
                ck=jnp.tile(cos, (1, N_KV)), sk=jnp.tile(sin, (1, N_KV)))
    cst = dict(attn=_attn_consts(), hg_fw=_hgrn_consts(False), hg_bw=_hgrn_consts(True),
               seg_h=_bf(_seg_matrix(D_HGRN, HEAD_DIM, 1.0 / HEAD_DIM)))
    vec = lambda a: a.reshape(DEPTH, 1, -1)
    smk = dict(sm)
    for n in ("mix_norm_w", "conv_dw_b", "conv_ln_w", "conv_ln_b", "conv_pw_b", "attn_out_norm_w", "conv_out_norm_w",
              "ffn_norm_w"):
        smk[n] = vec(sm[n])
    smk["q_norm_w"] = vec(jnp.tile(sm["q_norm_w"], (1, D_ATTN // HEAD_DIM)))
    smk["k_norm_w"] = vec(jnp.tile(sm["k_norm_w"], (1, N_KV)))
    smk["gnorm_w"] = vec(jnp.tile(sm["hgrn_gnorm_w"], (1, D_HGRN // HEAD_DIM)))
    smk["lb"] = sm["lb"].reshape(DEPTH, 2, 1, D_HGRN)
    smk["conv_dw_w"] = jnp.pad(sm["conv_dw_w"], ((0, 0), (0, 1), (0, 0)))

    h = x.reshape(t, d)
    saved = []
    for l in range(DEPTH):
        h, sv = _layer_fwd(l, h, wget, smk, tabs, cst, b, s, deps if l == 0 else (),
                           target.reshape(t, d) if l == DEPTH - 1 else None)
        saved.append(sv)
    dy, sq = h
    sq_sum = jnp.sum(sq)
    dh = dy
    smalls = [None] * DEPTH
    for l in reversed(range(DEPTH)):
        dh, smalls[l] = _layer_bwd(l, dh, saved[l], wget, smk, tabs, cst, b, s, on_grads)
    return sq_sum, dh.reshape(b, s, d), smalls


HBM_SPEC = pl.BlockSpec(memory_space=pltpu.HBM)


def _exchange(name, arrs, mode):
    n = len(arrs)
    if mode == "gather8":
        flips = [(fx, fy, fc) for fx in (0, 1) for fy in (0, 1) for fc in (0, 1)][1:]
    elif mode == "sibling":
        flips = [(0, 0, 1)]
    else:
        flips = [(1, 0, 0), (0, 1, 0), (1, 1, 0)]
    n_f = len(flips)

    def body(*refs):
        ins, outs = refs[:n], refs[n:2 * n]
        send_sems, recv_sems, local_sems = refs[2 * n:]
        x, y, c = lax.axis_index("x"), lax.axis_index("y"), lax.axis_index("c")

        def slot_of(px, py, pc):
            return (2 * px + py) if mode != "gather8" else (4 * px + 2 * py + pc)

        me = slot_of(x, y, c)
        started = []
        for i in range(n):
            if mode != "sibling":
                src = ins[i].at[me] if mode == "scatter4" else ins[i]
                loc = pltpu.make_async_copy(src, outs[i].at[me], local_sems.at[i])
                loc.start()
                started.append(loc)
        sends, recvs = [], []
        for i in range(n):
            for f, (fx, fy, fc) in enumerate(flips):
                peer = (x ^ fx, y ^ fy, c ^ fc)
                ps = slot_of(*peer)
                if mode == "sibling":
                    src, dst, landed = ins[i], outs[i], outs[i]
                elif mode == "scatter4":
                    src, dst, landed = ins[i].at[ps], outs[i].at[me], outs[i].at[ps]
                else:
                    src, dst, landed = ins[i], outs[i].at[me], outs[i].at[ps]
                k = i * n_f + f
                cp = pltpu.make_async_remote_copy(src_ref=src, dst_ref=dst, send_sem=send_sems.at[k],
                                                  recv_sem=recv_sems.at[k], device_id=peer,
                                                  device_id_type=pl.DeviceIdType.MESH)
                cp.start()
                sends.append(cp)
                recvs.append(pltpu.make_async_remote_copy(src_ref=src, dst_ref=landed, send_sem=send_sems.at[k],
                                                          recv_sem=recv_sems.at[k], device_id=peer,
                                                          device_id_type=pl.DeviceIdType.MESH))
        for cp in sends:
            cp.wait_send()
        for cp in recvs:
            cp.wait_recv()
        for loc in started:
            loc.wait()

    def out_sds(a):
        if mode == "gather4":
            return _sds((N_CHIP,) + a.shape, a.dtype)
        if mode == "gather8":
            return _sds((N_DEV,) + a.shape, a.dtype)
        return _sds(a.shape, a.dtype)

    res = pl.pallas_call(
        body, name=name, in_specs=[HBM_SPEC] * n, out_specs=[HBM_SPEC] * n, out_shape=[out_sds(a) for a in arrs],
        scratch_shapes=[pltpu.SemaphoreType.DMA((n * n_f,)), pltpu.SemaphoreType.DMA((n * n_f,)),
                        pltpu.SemaphoreType.DMA((max(n, 1),))],
    )(*arrs)
    return list(res)


SEM_SPEC = pl.BlockSpec(memory_space=pltpu.SEMAPHORE)
SPLIT_EFFECT = pltpu.SideEffectType.DATAFLOW_SIDE_EFFECTING
CHIP_FLIPS = ((1, 0), (0, 1), (1, 1))


def _chip_copies(src_refs, land_refs, send_sems, recv_sems, scatter):
    x, y, c = lax.axis_index("x"), lax.axis_index("y"), lax.axis_index("c")
    me = 2 * x + y
    out = []
    for i, land in enumerate(land_refs):
        if scatter == "sibling":
            kw = dict(send_sem=send_sems.at[i], recv_sem=recv_sems.at[i], device_id=(x, y, 1 - c),
                      device_id_type=pl.DeviceIdType.MESH)
            cp = pltpu.make_async_remote_copy(src_ref=src_refs[i], dst_ref=land, **kw)
            out.append((cp, cp))
            continue
        for f, (fx, fy) in enumerate(CHIP_FLIPS):
            peer = (x ^ fx, y ^ fy, c)
            ps = 2 * (x ^ fx) + (y ^ fy)
            src = src_refs[i].at[ps] if scatter else land.at[me]
            k = i * len(CHIP_FLIPS) + f
            kw = dict(send_sem=send_sems.at[k], recv_sem=recv_sems.at[k], device_id=peer,
                      device_id_type=pl.DeviceIdType.MESH)
            out.append((pltpu.make_async_remote_copy(src_ref=src, dst_ref=land.at[me], **kw),
                        pltpu.make_async_remote_copy(src_ref=src, dst_ref=land.at[ps], **kw)))
    return out


def _split_start(name, srcs, lands, scatter):
    n = len(lands)
    n_src = len(srcs)
    n_sem = n if scatter == "sibling" else n * len(CHIP_FLIPS)

    def body(*refs):
        src_refs = refs[:n_src]
        land_refs = refs[n_src:n_src + n]
        send_sems, recv_sems = refs[n_src + n], refs[n_src + n + 1]
        token = refs[-1]
        for start, _ in _chip_copies(src_refs, land_refs, send_sems, recv_sems, scatter):
            start.start()
        token[...] = jnp.zeros_like(token)

    arrs = list(srcs) + list(lands)
    res = pl.pallas_call(
        body, name=name,
        out_shape=(pltpu.SemaphoreType.DMA((n_sem,)), pltpu.SemaphoreType.DMA((n_sem,)),
                   *[pltpu.HBM(a.shape, a.dtype) for a in arrs], _sds((8, LANES), F32)),
        in_specs=[HBM_SPEC] * len(arrs),
        out_specs=(SEM_SPEC, SEM_SPEC, *[HBM_SPEC] * len(arrs), pl.BlockSpec(memory_space=pltpu.VMEM)),
        input_output_aliases={i: 2 + i for i in range(len(arrs))},
        compiler_params=pltpu.CompilerParams(has_side_effects=SPLIT_EFFECT),
    )(*[pltpu.with_memory_space_constraint(a, pltpu.HBM) for a in arrs])
    return dict(send=res[0], recv=res[1], srcs=list(res[2:2 + n_src]), lands=list(res[2 + n_src:2 + n_src + n]),
                token=res[-1], scatter=scatter)


def _split_wait(name, started, after, with_srcs=False):
    srcs, lands, scatter = started["srcs"], started["lands"], started["scatter"]
    n, n_src = len(lands), len(srcs)

    def body(*refs):
        src_refs = refs[:n_src]
        land_refs = refs[n_src:n_src + n]
        send_sems, recv_sems = refs[n_src + n], refs[n_src + n + 1]
        for _, wait in _chip_copies(src_refs, land_refs, send_sems, recv_sems, scatter):
            wait.wait_send()
            wait.wait_recv()

    arrs = list(srcs) + list(lands)
    res = pl.pallas_call(
        body, name=name, out_shape=tuple(pltpu.HBM(a.shape, a.dtype) for a in arrs),
        in_specs=[HBM_SPEC] * len(arrs) + [SEM_SPEC, SEM_SPEC, pl.BlockSpec(memory_space=pl.ANY)],
        out_specs=tuple([HBM_SPEC] * len(arrs)), input_output_aliases={i: i for i in range(len(arrs))},
        compiler_params=pltpu.CompilerParams(has_side_effects=SPLIT_EFFECT),
    )(*arrs, started["send"], started["recv"], after)
    return (list(res[:n_src]), list(res[n_src:])) if with_srcs else list(res[n_src:])


def _flat_tile(rows):
    for cand in (512, 256, 128, 64, 32, 16, 8):
        if rows % cand == 0:
            return cand
    return rows


def _cast_slot(name, a, l, chip, layers=DEPTH, dtype=BF16):
    r, c = a.shape[0] // layers, a.shape[1]
    tr = _flat_tile(r)

    def body(chip_ref, a_ref, o_ref):
        o_ref[...] = a_ref[...].astype(dtype)

    return pl.pallas_call(
        body, name=name, out_shape=_sds((N_CHIP, r, c), dtype),
        grid_spec=pltpu.PrefetchScalarGridSpec(
            num_scalar_prefetch=1, grid=(r // tr,),
            in_specs=[pl.BlockSpec((tr, c), lambda i, ch: (l * (r // tr) + i, 0))],
            out_specs=pl.BlockSpec((None, tr, c), lambda i, ch: (ch[0], i, 0))),
        compiler_params=_params(("parallel",)))(chip, a)


def _own_slot(name, g, chip):
    n, r, c = g.shape
    tr = _flat_tile(r)

    def body(chip_ref, g_ref, o_ref):
        o_ref[...] = g_ref[...]

    spec = pl.BlockSpec((None, tr, c), lambda i, ch: (ch[0], i, 0))
    return pl.pallas_call(
        body, name=name, out_shape=_sds(g.shape, g.dtype),
        grid_spec=pltpu.PrefetchScalarGridSpec(num_scalar_prefetch=1, grid=(r // tr,), in_specs=[spec], out_specs=spec),
        compiler_params=_params(("parallel",)))(chip, g)


def _sum_layers(name, lands):
    n, r, c = lands[0].shape
    tr = _flat_tile(r)
    nl = len(lands)

    def body(*refs):
        o_ref = refs[-1]
        for k in range(nl):
            @pl.when(pl.program_id(0) == k)
            def _():
                tot = refs[k][0].astype(F32)
                for i in range(1, n):
                    tot = tot + refs[k][i].astype(F32)
                o_ref[...] = tot

    return pl.pallas_call(
        body, name=name, grid=(nl, r // tr),
        in_specs=[pl.BlockSpec((n, tr, c), lambda l, i, k=k: (0, jnp.where(l == k, i, 0), 0)) for k in range(nl)],
        out_specs=pl.BlockSpec((tr, c), lambda l, i: (l * (r // tr) + i, 0)), out_shape=_sds((nl * r, c), F32),
        compiler_params=_params(("arbitrary", "arbitrary")))(*lands)


def _sum_slots(name, a, scale=None):
    n, r, c = a.shape
    tr = _flat_tile(r)

    def body(a_ref, o_ref):
        tot = a_ref[0].astype(F32)
        for i in range(1, n):
            tot = tot + a_ref[i].astype(F32)
        o_ref[...] = tot

    return pl.pallas_call(body, name=name, grid=(r // tr,),
                          in_specs=[pl.BlockSpec((n, tr, c), lambda i: (0, i, 0))],
                          out_specs=pl.BlockSpec((tr, c), lambda i: (i, 0)), out_shape=_sds((r, c), F32),
                          compiler_params=_params(("parallel",)))(a)


def _adamw(name, w, ga, gb, m, v):
    r, c = w.shape
    tr = _flat_tile(r)
    c1 = 1.0 - B1 ** STEP
    c2 = 1.0 - B2 ** STEP
    two = gb is not None

    def body(*refs):
        if two:
            w_ref, ga_ref, gb_ref, m_ref, v_ref, g_out, d_out, m_out, v_out = refs
            g = ga_ref[...] + gb_ref[...]
        else:
            w_ref, ga_ref, m_ref, v_ref, g_out, d_out, m_out, v_out = refs
            g = ga_ref[...]
        mn = B1 * m_ref[...] + (1.0 - B1) * g
        vn = B2 * v_ref[...] + (1.0 - B2) * (g * g)
        g_out[...] = g
        m_out[...] = mn
        v_out[...] = vn
        d_out[...] = -LR * ((mn / c1) / (jnp.sqrt(vn / c2) + ADAM_EPS) + WD * w_ref[...])

    spec = pl.BlockSpec((tr, c), lambda i: (i, 0))
    ins = [w, ga, gb, m, v] if two else [w, ga, m, v]
    return pl.pallas_call(body, name=name, grid=(r // tr,), in_specs=[spec] * len(ins), out_specs=[spec] * 4,
                          out_shape=[_sds((r, c), F32)] * 4, compiler_params=_params(("parallel",)))(*ins)


WEIGHTS = ('mix_norm_w', 'w_in', 'q_norm_w', 'k_norm_w', 'hgrn_lb_logits', 'hgrn_gnorm_w', 'conv_dw_w', 'conv_dw_b',
           'conv_ln_w', 'conv_ln_b', 'conv_pw_w', 'conv_pw_b', 'attn_out_norm_w', 'conv_out_norm_w', 'w_out',
           'ffn_norm_w', 'w_gate', 'w_up', 'w_down')
SHARDED_SMALL = {"hgrn_lb_logits": 2, "conv_dw_w": 2, "conv_pw_w": 1}
LANES = 128
PACK_ROWS = 256


def _pack(parts):
    flat = jnp.concatenate([p.reshape(-1) for p in parts])
    n = flat.shape[0]
    rows = -(-n // (PACK_ROWS * LANES)) * PACK_ROWS
    return jnp.pad(flat, (0, rows * LANES - n)).reshape(rows, LANES)


def _unpack(packed, shapes):
    flat = packed.reshape(-1)
    out, off = [], 0
    for shp in shapes:
        n = int(np.prod(shp))
        out.append(flat[off:off + n].reshape(shp))
        off += n
    return out


def kernel(x, mix_norm_w, w_in, q_norm_w, k_norm_w, hgrn_lb_logits, hgrn_gnorm_w, conv_dw_w, conv_dw_b, conv_ln_w, conv_ln_b, conv_pw_w, conv_pw_b, attn_out_norm_w, conv_out_norm_w, w_out, ffn_norm_w, w_gate, w_up, w_down, loss_target, m_mix_norm_w, m_w_in, m_q_norm_w, m_k_norm_w, m_hgrn_lb_logits, m_hgrn_gnorm_w, m_conv_dw_w, m_conv_dw_b, m_conv_ln_w, m_conv_ln_b, m_conv_pw_w, m_conv_pw_b, m_attn_out_norm_w, m_conv_out_norm_w, m_w_out, m_ffn_norm_w, m_w_gate, m_w_up, m_w_down, v_mix_norm_w, v_w_in, v_q_norm_w, v_k_norm_w, v_hgrn_lb_logits, v_hgrn_gnorm_w, v_conv_dw_w, v_conv_dw_b, v_conv_ln_w, v_conv_ln_b, v_conv_pw_w, v_conv_pw_b, v_attn_out_norm_w, v_conv_out_norm_w, v_w_out, v_ffn_norm_w, v_w_gate, v_w_up, v_w_down):
    w = dict(mix_norm_w=mix_norm_w, w_in=w_in, q_norm_w=q_norm_w, k_norm_w=k_norm_w, hgrn_lb_logits=hgrn_lb_logits,
             hgrn_gnorm_w=hgrn_gnorm_w, conv_dw_w=conv_dw_w, conv_dw_b=conv_dw_b, conv_ln_w=conv_ln_w,
             conv_ln_b=conv_ln_b, conv_pw_w=conv_pw_w, conv_pw_b=conv_pw_b, attn_out_norm_w=attn_out_norm_w,
             conv_out_norm_w=conv_out_norm_w, w_out=w_out, ffn_norm_w=ffn_norm_w, w_gate=w_gate, w_up=w_up,
             w_down=w_down)
    m = dict(mix_norm_w=m_mix_norm_w, w_in=m_w_in, q_norm_w=m_q_norm_w, k_norm_w=m_k_norm_w,
             hgrn_lb_logits=m_hgrn_lb_logits, hgrn_gnorm_w=m_hgrn_gnorm_w, conv_dw_w=m_conv_dw_w,
             conv_dw_b=m_conv_dw_b, conv_ln_w=m_conv_ln_w, conv_ln_b=m_conv_ln_b, conv_pw_w=m_conv_pw_w,
             conv_pw_b=m_conv_pw_b, attn_out_norm_w=m_attn_out_norm_w, conv_out_norm_w=m_conv_out_norm_w,
             w_out=m_w_out, ffn_norm_w=m_ffn_norm_w, w_gate=m_w_gate, w_up=m_w_up, w_down=m_w_down)
    v = dict(mix_norm_w=v_mix_norm_w, w_in=v_w_in, q_norm_w=v_q_norm_w, k_norm_w=v_k_norm_w,
             hgrn_lb_logits=v_hgrn_lb_logits, hgrn_gnorm_w=v_hgrn_gnorm_w, conv_dw_w=v_conv_dw_w,
             conv_dw_b=v_conv_dw_b, conv_ln_w=v_conv_ln_w, conv_ln_b=v_conv_ln_b, conv_pw_w=v_conv_pw_w,
             conv_pw_b=v_conv_pw_b, attn_out_norm_w=v_attn_out_norm_w, conv_out_norm_w=v_conv_out_norm_w,
             w_out=v_w_out, ffn_norm_w=v_ffn_norm_w, w_gate=v_w_gate, w_up=v_w_up, w_down=v_w_down)
    chip = 2 * lax.axis_index("x") + lax.axis_index("y")

    chip1 = chip.reshape(1).astype(jnp.int32)

    flat2 = lambda a: a.reshape(-1, a.shape[-1])
    groups = [[(l, "w_in")] if first else [(l, n) for n in BIG_ORDER[1:]] for l in range(DEPTH) for first in (1, 0)]
    group_of = {key: g for g, keys in enumerate(groups) for key in keys}
    groups[0].append((0, "small"))
    starts = []
    for g, keys in enumerate(groups):
        slots = [_cast_slot("cast_small", _pack([w[k] for k in SHARDED_SMALL]), 0, chip1, 1, F32) if n == "small"
                 else _cast_slot("cast_%s_l%d" % (n, l), flat2(w[n]), l, chip1) for l, n in keys]
        starts.append(_split_start("gather_start_g%d" % g, [], slots, False))
    got = {}

    def wget(l, name, after):
        if (l, name) not in got:
            g = group_of[(l, name)]
            for key, arr in zip(groups[g], _split_wait("gather_wait_g%d" % g, starts[g], after)):
                got[key] = arr
        return got[(l, name)]

    pending = []

    def on_grads(l, grads):
        names = [n for n in SCATTER_ORDER if n in grads]
        own = [_own_slot("own_%s_l%d" % (n, l), grads[n], chip1) for n in names]
        st = _split_start("scatter_start_l%d_%s" % (l, names[0]), [grads[n] for n in names], own, True)
        pending.append((l, names, st))
        return [st["token"]]

    wget(0, "w_in", starts[-1]["token"])
    gathered_small = got[(0, "small")]
    parts = [_unpack(gathered_small[j], [w[n].shape for n in SHARDED_SMALL]) for j in range(N_CHIP)]
    full_small = {n: jnp.concatenate([parts[j][i] for j in range(N_CHIP)], axis=ax)
                  for i, (n, ax) in enumerate(SHARDED_SMALL.items())}
    sm = {n: w[n] for n in WEIGHTS if n not in BIG_ORDER and n not in SHARDED_SMALL}
    sm["conv_dw_w"] = full_small["conv_dw_w"]
    sm["conv_pw_w"] = full_small["conv_pw_w"]
    logits = full_small["hgrn_lb_logits"].reshape(DEPTH * 2, D_HGRN)
    sm["lb"] = _lower_bounds(logits).reshape(DEPTH, 2, D_HGRN)

    sq_sum, grad_x, smalls = _local_step(x, loss_target, wget, sm, [st["token"] for st in starts], on_grads)
    loss = lax.psum(0.5 * sq_sum / D_MODEL, ("x", "y", "c"))

    landed = {}
    for l, names, st in pending:
        for n, arr in zip(names, _split_wait("scatter_wait_l%d_%s" % (l, names[0]), st, grad_x)):
            landed[(l, n)] = arr
    sums = [_sum_layers("sum_" + n, [landed[(l, n)] for l in range(DEPTH)]) for n in SCATTER_ORDER]
    sib_start = _split_start("sibling_start", sums, [lax.empty(a.shape, a.dtype) for a in sums], "sibling")
    out = {}

    small_names = [n for n in WEIGHTS if n not in SCATTER_ORDER]
    g_pack = _pack([jnp.stack([smalls[l][n] for l in range(DEPTH)]) for n in SMALL_ORDER]) + sib_start["token"][0, 0]
    g_all = _exchange("gather_small_grads", [g_pack], "gather8")[0]
    g_tot = _sum_slots("sum_small", g_all)
    shapes = [(DEPTH,) + tuple(smalls[0][n].shape) for n in SMALL_ORDER]
    g_small = dict(zip(SMALL_ORDER, _unpack(g_tot, shapes)))
    lb_shard = lax.dynamic_slice_in_dim(g_small.pop("lb").reshape(DEPTH * 2, D_HGRN), chip * HEAD_DIM, HEAD_DIM, 1)
    g_small["hgrn_lb_logits"] = _lower_bounds_bwd(hgrn_lb_logits.reshape(DEPTH * 2, HEAD_DIM), lb_shard).reshape(
        hgrn_lb_logits.shape)
    g_small["conv_dw_w"] = lax.dynamic_slice_in_dim(g_small["conv_dw_w"], chip * HEAD_DIM, HEAD_DIM, 2)
    res = _adamw("adamw_small", _pack([w[n] for n in small_names]), _pack([g_small[n] for n in small_names]), None,
                 _pack([m[n] for n in small_names]), _pack([v[n] for n in small_names]))
    unpacked = [_unpack(r, [w[n].shape for n in small_names]) for r in res]
    for i, n in enumerate(small_names):
        out[n] = [unpacked[k][i] for k in range(4)]
    own, sib = _split_wait("sibling_wait", sib_start, res[0], with_srcs=True)
    for n, ga, gb in zip(SCATTER_ORDER, own, sib):
        big = _adamw("adamw_" + n, flat2(w[n]), ga, gb, flat2(m[n]), flat2(v[n]))
        out[n] = [r.reshape(w[n].shape) for r in big]

    return (loss, grad_x, *[out[n][0] for n in WEIGHTS], *[out[n][1] for n in WEIGHTS],
            *[out[n][2] for n in WEIGHTS], *[out[n][3] for n in WEIGHTS])
```

```python
import functools

import numpy as np
import jax
import jax.numpy as jnp
from jax import lax
from jax.experimental import pallas as pl
from jax.experimental.pallas import tpu as pltpu

F32, BF16 = jnp.float32, jnp.bfloat16

D_MODEL = 1024
DEPTH = 2
GRID_W = 64
D_ATTN, D_HGRN, D_CONV = 512, 256, 256
HEAD_DIM = 64
N_KV = 2
KV_LANES = D_ATTN // N_KV
ROPE_THETA = 10000.0
F_MIN = 1e-6
CONV_W = 31
CONV_PAD = 15
D_FF = 2816
D_IN = 2560
N_CHIP = 4
N_DEV = 8
IN_BLK = D_IN // N_CHIP
FF_BLK = D_FF // N_CHIP
OUT_BLK = D_MODEL // N_CHIP
EPS = 1e-6
LN_EPS = 1e-5
LR, B1, B2, ADAM_EPS, WD, STEP = 0.001, 0.9, 0.999, 1e-08, 0.01, 10
CHUNK = 16
HBLK = 256
CONV_TILE = 128
ATTN_FWD_ROWS = 512
ATTN_BWD_ROWS = 256
BWD_GROUP = 2
VMEM_LIMIT = 56 * 1024 * 1024

COL_Q, COL_K, COL_V = 0, 4, 5
COL_HQ, COL_FF, COL_FB, COL_HI, COL_HG, COL_CA, COL_CB = 3, 4, 5, 6, 7, 8, 9


def _params(sem=None):
    return pltpu.CompilerParams(dimension_semantics=sem, vmem_limit_bytes=VMEM_LIMIT)


def _sds(shape, dtype):
    return jax.ShapeDtypeStruct(tuple(shape), dtype)


def _full(shape):
    n = len(shape)
    return pl.BlockSpec(tuple(shape), lambda *_: (0,) * n)


def _sigmoid(x):
    return 0.5 * jnp.tanh(0.5 * x) + 0.5


def _gate_sigmoid(x):
    return 1.0 / (1.0 + jnp.exp(-x))


def _silu(x):
    return x * _sigmoid(x)


def _dsilu(x):
    s = _sigmoid(x)
    return s * (1.0 + x * (1.0 - s))


def _rowgroups(v):
    m, c = v.shape
    return v.reshape(m // 8, 8, c).sum(axis=0)


def _split2(x):
    hi = x.astype(BF16)
    lo = (x - hi.astype(F32)).astype(BF16)
    return hi, lo


def _rdot2(x, m):
    hi, lo = _split2(x)
    return (jnp.dot(hi, m, preferred_element_type=F32) + jnp.dot(lo, m, preferred_element_type=F32))


def _ldot3(m, x):
    hi = x.astype(BF16)
    r1 = x - hi.astype(F32)
    mid = r1.astype(BF16)
    lo = (r1 - mid.astype(F32)).astype(BF16)
    return (jnp.dot(m, hi, preferred_element_type=F32) + jnp.dot(m, mid, preferred_element_type=F32)
            + jnp.dot(m, lo, preferred_element_type=F32))


def _dot_nt(a, b):
    return lax.dot_general(a, b, (((1,), (1,)), ((), ())), preferred_element_type=F32)


def _dot_tn(a, b):
    return lax.dot_general(a, b, (((0,), (0,)), ((), ())), preferred_element_type=F32)


def _seg_matrix(n, seg, val):
    i = np.arange(n)
    return ((i[:, None] // seg) == (i[None, :] // seg)).astype(np.float32) * val


def _rot_matrix(n):
    r = np.zeros((n, n), np.float32)
    for i in range(n):
        if (i % 32) < 16:
            r[i + 16, i] = -1.0
        else:
            r[i - 16, i] = 1.0
    return r


def _rep_matrix():
    r = np.zeros((N_KV * HEAD_DIM, D_ATTN), np.float32)
    for kv in range(N_KV):
        for g in range(KV_LANES // HEAD_DIM):
            for d in range(HEAD_DIM):
                r[HEAD_DIM * kv + d, KV_LANES * kv + HEAD_DIM * g + d] = 1.0
    return r


def _cumsum_matrix(rev):
    i = np.arange(HBLK)
    same = (i[:, None] // CHUNK) == (i[None, :] // CHUNK)
    tri = (i[None, :] >= i[:, None]) if rev else (i[None, :] <= i[:, None])
    return (same & tri).astype(np.float32)


def _sel_matrices():
    sel = np.zeros((CHUNK, CHUNK * CHUNK), np.float32)
    selt = np.zeros((CHUNK, CHUNK * CHUNK), np.float32)
    for t in range(CHUNK):
        for s in range(CHUNK):
            sel[t, t * CHUNK + s] = 1.0
            selt[s, t * CHUNK + s] = 1.0
    return sel, selt


def _bf(a):
    return jnp.asarray(a, dtype=BF16)


def _mm(name, grid, pairs, extras, outs, epilogue, acc=None, sem=None):
    n_p, n_e, n_o = len(pairs), len(extras), len(outs)

    def body(*refs):
        ab = refs[:2 * n_p]
        ex = refs[2 * n_p:2 * n_p + n_e]
        out = refs[2 * n_p + n_e:2 * n_p + n_e + n_o]
        scr = refs[2 * n_p + n_e + n_o:]
        tot = None
        for i in range(n_p):
            a = ab[2 * i][...]
            b = ab[2 * i + 1][...]
            if a.ndim == 3:
                a = a.reshape(-1, a.shape[-1])
            if b.ndim == 3:
                b = b.reshape(-1, b.shape[-1])
            r = lax.dot_general(a.astype(BF16), b.astype(BF16), pairs[i][4], preferred_element_type=F32)
            tot = r if tot is None else tot + r

        def finish(total):
            res = epilogue(total, *[e[...] for e in ex])
            for o_ref, val in zip(out, res):
                o_ref[...] = val.astype(o_ref.dtype)

        if acc is None:
            finish(tot)
        else:
            k = pl.program_id(acc[0])

            @pl.when(k == 0)
            def _():
                scr[0][...] = tot

            @pl.when(k > 0)
            def _():
                scr[0][...] += tot

            @pl.when(k == grid[acc[0]] - 1)
            def _():
                finish(scr[0][...])

    args, in_specs = [], []
    for a, a_spec, b, b_spec, _ in pairs:
        args += [a, b]
        in_specs += [a_spec, b_spec]
    for e, e_spec in extras:
        args.append(e)
        in_specs.append(e_spec)
    if sem is None:
        sem = tuple("arbitrary" if (acc is not None and i == acc[0]) else "parallel" for i in range(len(grid)))
    return pl.pallas_call(
        body, name=name, grid=grid, in_specs=in_specs,
        out_specs=[o[1] for o in outs], out_shape=[o[0] for o in outs],
        scratch_shapes=[] if acc is None else [pltpu.VMEM(acc[1], F32)],
        compiler_params=_params(sem),
    )(*args)


NN = (((1,), (0,)), ((), ()))
NT = (((1,), (1,)), ((), ()))
TN = (((0,), (0,)), ((), ()))


def _row_tile(t):
    return min(256, t)


def _rms_fwd(name, x, w, deps=()):
    t, d = x.shape
    tm = _row_tile(t)

    def body(x_ref, w_ref, *rest):
        o_ref = rest[-1]
        xv = x_ref[...]
        r = lax.rsqrt(jnp.mean(xv * xv, axis=-1, keepdims=True) + EPS)
        o_ref[...] = (xv * r * w_ref[...]).astype(BF16)

    return pl.pallas_call(
        body, name=name, grid=(t // tm,),
        in_specs=[pl.BlockSpec((tm, d), lambda i: (i, 0)), _full((1, d))] + [_full(a.shape) for a in deps],
        out_specs=pl.BlockSpec((tm, d), lambda i: (i, 0)), out_shape=_sds((t, d), BF16),
        compiler_params=_params(("parallel",)),
    )(x, w, *deps)


def _rms_bwd(name, x, w, dh, dres, deps=()):
    t, d = x.shape
    tm = _row_tile(t)

    def body(x_ref, w_ref, dh_ref, dres_ref, *rest):
        dx_ref, dw_ref = rest[-2:]
        xv = x_ref[...]
        r = lax.rsqrt(jnp.mean(xv * xv, axis=-1, keepdims=True) + EPS)
        dy = dh_ref[...]
        gw = dy * w_ref[...]
        dx_ref[...] = dres_ref[...] + r * gw - xv * (r * r * r) * jnp.mean(gw * xv, axis=-1, keepdims=True)

        @pl.when(pl.program_id(0) == 0)
        def _():
            dw_ref[...] = jnp.zeros_like(dw_ref)

        dw_ref[...] += _rowgroups(dy * xv * r)

    tile = pl.BlockSpec((tm, d), lambda i: (i, 0))
    return pl.pallas_call(
        body, name=name, grid=(t // tm,),
        in_specs=[tile, _full((1, d)), tile, tile] + [_full(a.shape) for a in deps],
        out_specs=[tile, _full((8, d))], out_shape=[_sds((t, d), F32), _sds((8, d), F32)],
        compiler_params=_params(("arbitrary",)),
    )(x, w, dh, dres, *deps)


def _loss_kernel(y, target):
    t, d = y.shape
    tm = _row_tile(t)

    def body(y_ref, t_ref, dy_ref, acc_ref):
        e = y_ref[...] - t_ref[...]
        dy_ref[...] = e * (1.0 / d)

        @pl.when(pl.program_id(0) == 0)
        def _():
            acc_ref[...] = jnp.zeros_like(acc_ref)

        acc_ref[...] += _rowgroups(e * e)

    tile = pl.BlockSpec((tm, d), lambda i: (i, 0))
    return pl.pallas_call(
        body, name="loss_head", grid=(t // tm,), in_specs=[tile, tile],
        out_specs=[tile, _full((8, d))], out_shape=[_sds((t, d), F32), _sds((8, d), F32)],
        compiler_params=_params(("arbitrary",)),
    )(y, target)


def _rope_tables(s):
    rows = s // GRID_W
    row_id = jnp.repeat(jnp.arange(rows, dtype=F32), GRID_W)
    col_id = jnp.tile(jnp.arange(GRID_W, dtype=F32), rows)
    half = HEAD_DIM // 2
    inv_freq = ROPE_THETA ** (-jnp.arange(0, half, 2, dtype=F32) / half)
    ang_r = row_id[:, None] * inv_freq[None, :]
    ang_c = col_id[:, None] * inv_freq[None, :]
    ang = jnp.concatenate([ang_r, ang_r, ang_c, ang_c], axis=-1)
    return jnp.cos(ang).astype(F32), jnp.sin(ang).astype(F32)


def _attn_consts():
    return dict(
        seg_q=_bf(_seg_matrix(D_ATTN, HEAD_DIM, 1.0 / HEAD_DIM)),
        seg_k=_bf(_seg_matrix(N_KV * HEAD_DIM, HEAD_DIM, 1.0 / HEAD_DIM)),
        rot_q=_bf(_rot_matrix(D_ATTN)), rot_k=_bf(_rot_matrix(N_KV * HEAD_DIM)),
        rep=_bf(_rep_matrix()), rep_t=_bf(_rep_matrix().T))


def _attn_prep(name, proj, s, tabs, qw, kw, ac):
    t = proj.shape[0]
    tm = _row_tile(s)
    nst = s // tm
    kw_ = N_KV * HEAD_DIM

    def body(q_ref, k_ref, v_ref, cq_ref, sq_ref, ck_ref, sk_ref, qw_ref, kw_ref,
             segq_ref, segk_ref, rotq_ref, rotk_ref, rep_ref, qn_ref, kr_ref, vr_ref):
        q = q_ref[...]
        r = lax.rsqrt(jnp.dot((q * q).astype(BF16), segq_ref[...], preferred_element_type=F32) + EPS)
        qn = q * r * qw_ref[...]
        qr = qn * cq_ref[...] + _rdot2(qn, rotq_ref[...]) * sq_ref[...]
        qn_ref[...] = (qr * (HEAD_DIM ** -0.5)).astype(BF16)
        k = k_ref[...]
        rk = lax.rsqrt(jnp.dot((k * k).astype(BF16), segk_ref[...], preferred_element_type=F32) + EPS)
        kn = k * rk * kw_ref[...]
        kr = kn * ck_ref[...] + _rdot2(kn, rotk_ref[...]) * sk_ref[...]
        kr_ref[...] = jnp.dot(kr.astype(BF16), rep_ref[...], preferred_element_type=F32).astype(BF16)
        vr_ref[...] = jnp.dot(v_ref[...].astype(BF16), rep_ref[...], preferred_element_type=F32).astype(BF16)

    wide = pl.BlockSpec((tm, D_ATTN), lambda i: (i, 0))
    tabq = pl.BlockSpec((tm, D_ATTN), lambda i: (i % nst, 0))
    tabk = pl.BlockSpec((tm, kw_), lambda i: (i % nst, 0))
    return pl.pallas_call(
        body, name=name, grid=(t // tm,),
        in_specs=[pl.BlockSpec((tm, D_ATTN), lambda i: (i, COL_Q)), pl.BlockSpec((tm, kw_), lambda i: (i, COL_K)),
                  pl.BlockSpec((tm, kw_), lambda i: (i, COL_V)), tabq, tabq, tabk, tabk,
                  _full((1, D_ATTN)), _full((1, kw_)), _full((D_ATTN, D_ATTN)), _full((kw_, kw_)),
                  _full((D_ATTN, D_ATTN)), _full((kw_, kw_)), _full((kw_, D_ATTN))],
        out_specs=[wide, wide, wide], out_shape=[_sds((t, D_ATTN), BF16)] * 3,
        compiler_params=_params(("parallel",)),
    )(proj, proj, proj, tabs["cq"], tabs["sq"], tabs["ck"], tabs["sk"], qw, kw,
      ac["seg_q"], ac["seg_k"], ac["rot_q"], ac["rot_k"], ac["rep"])


def _attn_prep_bwd(name, proj, s, tabs, qw, kw, ac, dqs, dkr, dvr):
    t = proj.shape[0]
    tm = _row_tile(s)
    nst = s // tm
    kw_ = N_KV * HEAD_DIM
    wout = D_ATTN + 2 * kw_

    def norm_rope_bwd(x, w, cos, sin, seg, rot, d_roped):
        dn = d_roped * cos - _rdot2(d_roped * sin, rot)
        r = lax.rsqrt(jnp.dot((x * x).astype(BF16), seg, preferred_element_type=F32) + EPS)
        gw = dn * w
        dx = r * gw - x * (r * r * r) * _rdot2(gw * x, seg)
        return dx, _rowgroups(dn * x * r)

    def body(q_ref, k_ref, cq_ref, sq_ref, ck_ref, sk_ref, qw_ref, kw_ref, segq_ref, segk_ref, rotq_ref, rotk_ref,
             rept_ref, dqs_ref, dkr_ref, dvr_ref, dp_ref, dqw_ref, dkw_ref):
        dq, dqw = norm_rope_bwd(q_ref[...], qw_ref[...], cq_ref[...], sq_ref[...], segq_ref[...], rotq_ref[...],
                                dqs_ref[...] * (HEAD_DIM ** -0.5))
        dk_roped = _rdot2(dkr_ref[...], rept_ref[...])
        dk, dkw = norm_rope_bwd(k_ref[...], kw_ref[...], ck_ref[...], sk_ref[...], segk_ref[...], rotk_ref[...],
                                dk_roped)
        dv = _rdot2(dvr_ref[...], rept_ref[...])
        dp_ref[:, 0:D_ATTN] = dq.astype(BF16)
        dp_ref[:, D_ATTN:D_ATTN + kw_] = dk.astype(BF16)
        dp_ref[:, D_ATTN + kw_:wout] = dv.astype(BF16)

        @pl.when(pl.program_id(0) == 0)
        def _():
            dqw_ref[...] = jnp.zeros_like(dqw_ref)
            dkw_ref[...] = jnp.zeros_like(dkw_ref)

        dqw_ref[...] += dqw
        dkw_ref[...] += dkw

    wide = pl.BlockSpec((tm, D_ATTN), lambda i: (i, 0))
    tabq = pl.BlockSpec((tm, D_ATTN), lambda i: (i % nst, 0))
    tabk = pl.BlockSpec((tm, kw_), lambda i: (i % nst, 0))
    return pl.pallas_call(
        body, name=name, grid=(t // tm,),
        in_specs=[pl.BlockSpec((tm, D_ATTN), lambda i: (i, COL_Q)), pl.BlockSpec((tm, kw_), lambda i: (i, COL_K)),
                  tabq, tabq, tabk, tabk, _full((1, D_ATTN)), _full((1, kw_)),
                  _full((D_ATTN, D_ATTN)), _full((kw_, kw_)), _full((D_ATTN, D_ATTN)), _full((kw_, kw_)),
                  _full((D_ATTN, kw_)), wide, wide, wide],
        out_specs=[pl.BlockSpec((tm, wout), lambda i: (i, 0)), _full((8, D_ATTN)), _full((8, kw_))],
        out_shape=[_sds((t, wout), BF16), _sds((8, D_ATTN), F32), _sds((8, kw_), F32)],
        compiler_params=_params(("arbitrary",)),
    )(proj, proj, tabs["cq"], tabs["sq"], tabs["ck"], tabs["sk"], qw, kw,
      ac["seg_q"], ac["seg_k"], ac["rot_q"], ac["rot_k"], ac["rep_t"], dqs, dkr, dvr)


def _attn_tile(s, rows=256):
    return min(rows, s)


def _head_masks(shape):
    lane = lax.broadcasted_iota(jnp.int32, shape, 1)
    return [(lane // HEAD_DIM) == g for g in range(KV_LANES // HEAD_DIM)]


def _attn_fwd(name, qn, kr, vr, b, s):
    t = qn.shape[0]
    tq = _attn_tile(s, ATTN_FWD_ROWS)
    nq = s // tq

    def body(q_ref, k_ref, v_ref, o_ref):
        q = q_ref[...]
        k = k_ref[...]
        v = v_ref[...]
        acc = jnp.zeros((tq, KV_LANES), F32)
        for mask in _head_masks((tq, KV_LANES)):
            sc = _dot_nt(jnp.where(mask, q, jnp.zeros_like(q)), k)
            p = jnp.exp(sc - jnp.max(sc, axis=-1, keepdims=True))
            inv = 1.0 / jnp.sum(p, axis=-1, keepdims=True)
            og = jnp.dot(p.astype(BF16), v, preferred_element_type=F32) * inv
            acc = jnp.where(mask, og, acc)
        o_ref[...] = acc

    return pl.pallas_call(
        body, name=name, grid=(b, N_KV, nq),
        in_specs=[pl.BlockSpec((tq, KV_LANES), lambda bi, kv, i: (bi * nq + i, kv)),
                  pl.BlockSpec((s, KV_LANES), lambda bi, kv, i: (bi, kv)),
                  pl.BlockSpec((s, KV_LANES), lambda bi, kv, i: (bi, kv))],
        out_specs=pl.BlockSpec((tq, KV_LANES), lambda bi, kv, i: (bi * nq + i, kv)),
        out_shape=_sds((t, D_ATTN), F32),
        compiler_params=_params(("parallel", "parallel", "parallel")),
    )(qn, kr, vr)


def _attn_bwd(name, qn, kr, vr, do, b, s):
    t = qn.shape[0]
    tq = _attn_tile(s, ATTN_BWD_ROWS)
    nq = s // tq

    def body(q_ref, k_ref, v_ref, do_ref, dq_ref, dk_ref, dv_ref):
        @pl.when(pl.program_id(2) == 0)
        def _():
            dk_ref[...] = jnp.zeros_like(dk_ref)
            dv_ref[...] = jnp.zeros_like(dv_ref)

        q = q_ref[...]
        k = k_ref[...]
        v = v_ref[...]
        dout = do_ref[...].astype(BF16)
        masks = _head_masks((tq, KV_LANES))
        q4 = jnp.concatenate([jnp.where(m, q, jnp.zeros_like(q)) for m in masks], axis=0)
        do4 = jnp.concatenate([jnp.where(m, dout, jnp.zeros_like(dout)) for m in masks], axis=0)
        sc = _dot_nt(q4, k)
        p = jnp.exp(sc - jnp.max(sc, axis=-1, keepdims=True))
        p = p * (1.0 / jnp.sum(p, axis=-1, keepdims=True))
        dp = _dot_nt(do4, v)
        ds = (p * (dp - jnp.sum(p * dp, axis=-1, keepdims=True))).astype(BF16)
        dq4 = jnp.dot(ds, k, preferred_element_type=F32)
        dq = jnp.zeros((tq, KV_LANES), F32)
        for g, m in enumerate(masks):
            dq = jnp.where(m, dq4[g * tq:(g + 1) * tq, :], dq)
        dq_ref[...] = dq
        dk_ref[...] += _dot_tn(ds, q4)
        dv_ref[...] += _dot_tn(p.astype(BF16), do4)

    qspec = pl.BlockSpec((tq, KV_LANES), lambda bi, kv, i: (bi * nq + i, kv))
    kspec = pl.BlockSpec((s, KV_LANES), lambda bi, kv, i: (bi, kv))
    return pl.pallas_call(
        body, name=name, grid=(b, N_KV, nq),
        in_specs=[qspec, kspec, kspec, qspec],
        out_specs=[qspec, kspec, kspec], out_shape=[_sds((t, D_ATTN), F32)] * 3,
        compiler_params=_params(("parallel", "parallel", "arbitrary")),
    )(qn, kr, vr, do)


def _hgrn_consts(rev):
    sel, selt = _sel_matrices()
    cs = _cumsum_matrix(rev)
    return dict(cs=_bf(cs), cs_t=_bf(cs.T), seg=_bf(_seg_matrix(D_HGRN, HEAD_DIM, 1.0)),
                bd=jnp.asarray(_seg_matrix(D_HGRN, HEAD_DIM, 1.0), F32),
                sel=_bf(sel), selt=_bf(selt), seld=_bf(sel - selt))


def _gates(z, lb):
    sig = _gate_sigmoid(z)
    f = lb + (1.0 - lb) * sig
    g = jnp.log(jnp.maximum(f, F_MIN))
    sn = _gate_sigmoid(-z)
    return sig, f, g, sn, (1.0 - lb) * sn


def _pair_decay(b, rev):
    row = lax.broadcasted_iota(jnp.int32, (CHUNK, D_HGRN), 0)
    parts = []
    for t in range(CHUNK):
        m = (row >= t) if rev else (row <= t)
        parts.append(jnp.where(m, jnp.exp(jnp.minimum(b[t:t + 1, :] - b, 0.0)), 0.0))
    return jnp.concatenate(parts, axis=0)


def _rows_rep(a):
    return jnp.concatenate([jnp.broadcast_to(a[t:t + 1, :], a.shape) for t in range(CHUNK)], axis=0)


def _tile_rows(a):
    return jnp.concatenate([a] * CHUNK, axis=0)


def _hgrn_specs(b, s, rev):
    nb = s // HBLK

    def blk(j):
        return (nb - 1 - j) if rev else j

    def col(c):
        return pl.BlockSpec((HBLK, D_HGRN), lambda bi, j: (bi * nb + blk(j), c))

    return nb, blk, col


def _hgrn_fwd(name, proj, lb, b, s, rev, hc):
    t = proj.shape[0]
    nb, blk, col = _hgrn_specs(b, s, rev)
    n_ch = HBLK // CHUNK
    last = 0 if rev else CHUNK - 1

    def body(q_ref, z_ref, v_ref, lb_ref, cs_ref, seg_ref, bd_ref, sel_ref, o_ref, st_ref, state, b_scr, k_scr):
        @pl.when(pl.program_id(1) == 0)
        def _():
            state[...] = jnp.zeros_like(state)

        st_ref[...] = state[...]
        _, _, g, _, kk = _gates(z_ref[...], lb_ref[...])
        k_scr[...] = kk
        b_scr[...] = _ldot3(cs_ref[...], g)

        def chunk(i, carry):
            c = (n_ch - 1 - i) if rev else i
            rows = pl.ds(pl.multiple_of(c * CHUNK, CHUNK), CHUNK)
            q = q_ref[rows, :]
            k = k_scr[rows, :]
            v = v_ref[rows, :]
            bb = b_scr[rows, :]
            bl = bb[last:last + 1, :]
            pairs = _pair_decay(bb, rev) * _rows_rep(q) * _tile_rows(k)
            a = jnp.dot(pairs.astype(BF16), seg_ref[...], preferred_element_type=F32)
            o_intra = jnp.dot(sel_ref[...], (a * _tile_rows(v)).astype(BF16), preferred_element_type=F32)
            st = state[...]
            o_inter = _dot_nt((q * jnp.exp(bb)).astype(BF16), st.astype(BF16))
            o_ref[rows, :] = o_intra + o_inter
            ke = k * jnp.exp(bl - bb)
            state[...] = st * jnp.exp(bl) + bd_ref[...] * _dot_tn(v.astype(BF16), ke.astype(BF16))
            return carry

        lax.fori_loop(0, n_ch, chunk, 0)

    sq = (D_HGRN, D_HGRN)
    return pl.pallas_call(
        body, name=name, grid=(b, nb),
        in_specs=[col(COL_HQ), col(COL_FB if rev else COL_FF), col(COL_HI), _full((1, D_HGRN)),
                  _full((HBLK, HBLK)), _full(sq), _full(sq), _full((CHUNK, CHUNK * CHUNK))],
        out_specs=[pl.BlockSpec((HBLK, D_HGRN), lambda bi, j: (bi * nb + blk(j), 0)),
                   pl.BlockSpec((None,) + sq, lambda bi, j: (bi * nb + blk(j), 0, 0))],
        out_shape=[_sds((t, D_HGRN), F32), _sds((b * nb,) + sq, F32)],
        scratch_shapes=[pltpu.VMEM(sq, F32), pltpu.VMEM((HBLK, D_HGRN), F32), pltpu.VMEM((HBLK, D_HGRN), F32)],
        compiler_params=_params(("parallel", "arbitrary")),
    )(proj, proj, proj, lb, hc["cs"], hc["seg"], hc["bd"], hc["sel"])


def _hgrn_bwd(name, proj, lb, st_blk, do, dq_prev, dv_prev, b, s, rev, hc):
    t = proj.shape[0]
    nb = s // HBLK
    n_ch = HBLK // CHUNK
    last = 0 if rev else CHUNK - 1

    def blk(j):
        return j if rev else (nb - 1 - j)

    def col(c):
        return pl.BlockSpec((HBLK, D_HGRN), lambda bi, j: (bi * nb + blk(j), c))

    def body(q_ref, z_ref, v_ref, lb_ref, st_ref, do_ref, dqp_ref, dvp_ref, cs_ref, cst_ref, seg_ref, bd_ref,
             sel_ref, selt_ref, seld_ref, dq_ref, dv_ref, dz_ref, dlb_ref,
             dstate, states, b_scr, k_scr, db_scr, dk_scr):
        first = jnp.logical_and(pl.program_id(0) == 0, pl.program_id(1) == 0)

        @pl.when(first)
        def _():
            dlb_ref[...] = jnp.zeros_like(dlb_ref)

        @pl.when(pl.program_id(1) == 0)
        def _():
            dstate[...] = jnp.zeros_like(dstate)

        lbv = lb_ref[...]
        z = z_ref[...]
        sig, f, g, sn, kk = _gates(z, lbv)
        k_scr[...] = kk
        b_scr[...] = _ldot3(cs_ref[...], g)

        def rows_of(c):
            return pl.ds(pl.multiple_of(c * CHUNK, CHUNK), CHUNK)

        def replay(i, st):
            c = (n_ch - 1 - i) if rev else i
            rows = rows_of(c)
            states[c] = st
            bb = b_scr[rows, :]
            bl = bb[last:last + 1, :]
            ke = k_scr[rows, :] * jnp.exp(bl - bb)
            return st * jnp.exp(bl) + bd_ref[...] * _dot_tn(v_ref[rows, :].astype(BF16), ke.astype(BF16))

        lax.fori_loop(0, n_ch, replay, st_ref[...])
        row = lax.broadcasted_iota(jnp.int32, (CHUNK, D_HGRN), 0)

        def chunk(i, carry):
            c = i if rev else (n_ch - 1 - i)
            rows = rows_of(c)
            q = q_ref[rows, :]
            k = k_scr[rows, :]
            v = v_ref[rows, :]
            bb = b_scr[rows, :]
            dout = do_ref[rows, :]
            bl = bb[last:last + 1, :]
            st_p = states[c]
            dst_n = dstate[...]
            eb = jnp.exp(bb)
            ebl = jnp.exp(bl - bb)
            ebl_last = jnp.exp(bl)
            qe = q * eb
            ke = k * ebl
            dob = dout.astype(BF16)
            dstb = dst_n.astype(BF16)
            dqe = jnp.dot(dob, st_p.astype(BF16), preferred_element_type=F32)
            dke = jnp.dot(v.astype(BF16), dstb, preferred_element_type=F32)
            dv = _dot_nt(ke.astype(BF16), dstb)
            dbl = jnp.sum(dst_n * st_p, axis=0, keepdims=True) * ebl_last + jnp.sum(dke * ke, axis=0, keepdims=True)
            dq = dqe * eb
            dk = dke * ebl
            db = dqe * qe - dke * ke
            dec = _pair_decay(bb, rev)
            q_rep = _rows_rep(q)
            k_til = _tile_rows(k)
            do_rep = _rows_rep(dout)
            pairs = dec * q_rep * k_til
            a = jnp.dot(pairs.astype(BF16), seg_ref[...], preferred_element_type=F32)
            wb = jnp.dot((_tile_rows(v) * do_rep).astype(BF16), seg_ref[...], preferred_element_type=F32)
            gdec = wb * dec
            dq = dq + jnp.dot(sel_ref[...], (gdec * k_til).astype(BF16), preferred_element_type=F32)
            dk = dk + jnp.dot(selt_ref[...], (gdec * q_rep).astype(BF16), preferred_element_type=F32)
            dv = dv + jnp.dot(selt_ref[...], (a * do_rep).astype(BF16), preferred_element_type=F32)
            db = db + jnp.dot(seld_ref[...], (wb * pairs).astype(BF16), preferred_element_type=F32)
            db = db + jnp.where(row == last, dbl, 0.0)
            dq_ref[rows, :] = dq + dqp_ref[rows, :]
            dv_ref[rows, :] = dv + dvp_ref[rows, :]
            dk_scr[rows, :] = dk
            db_scr[rows, :] = db
            dstate[...] = dst_n * ebl_last + bd_ref[...] * _dot_tn(dob, qe.astype(BF16))
            return carry

        lax.fori_loop(0, n_ch, chunk, 0)
        hi, lo = _split2(db_scr[...])
        dg = (jnp.dot(cst_ref[...], hi, preferred_element_type=F32)
              + jnp.dot(cst_ref[...], lo, preferred_element_type=F32))
        dgf = jnp.where(f > F_MIN, dg / f, 0.0)
        dk = dk_scr[...]
        dz_ref[...] = dgf * (1.0 - lbv) * sig * (1.0 - sig) - dk * (1.0 - lbv) * sn * (1.0 - sn)
        dlb_ref[...] += _rowgroups(dgf * (1.0 - sig) - dk * sn)

    sq = (D_HGRN, D_HGRN)
    blk0 = pl.BlockSpec((HBLK, D_HGRN), lambda bi, j: (bi * nb + blk(j), 0))
    pairs_shape = (CHUNK, CHUNK * CHUNK)
    return pl.pallas_call(
        body, name=name, grid=(b, nb),
        in_specs=[col(COL_HQ), col(COL_FB if rev else COL_FF), col(COL_HI), _full((1, D_HGRN)),
                  pl.BlockSpec((None,) + sq, lambda bi, j: (bi * nb + blk(j), 0, 0)), blk0, blk0, blk0,
                  _full((HBLK, HBLK)), _full((HBLK, HBLK)), _full(sq), _full(sq),
                  _full(pairs_shape), _full(pairs_shape), _full(pairs_shape)],
        out_specs=[blk0, blk0, blk0, _full((8, D_HGRN))],
        out_shape=[_sds((t, D_HGRN), F32)] * 3 + [_sds((8, D_HGRN), F32)],
        scratch_shapes=[pltpu.VMEM(sq, F32), pltpu.VMEM((n_ch,) + sq, F32)] + [pltpu.VMEM((HBLK, D_HGRN), F32)] * 4,
        compiler_params=_params(("arbitrary", "arbitrary")),
    )(proj, proj, proj, lb, st_blk, do, dq_prev, dv_prev,
      hc["cs"], hc["cs_t"], hc["seg"], hc["bd"], hc["sel"], hc["selt"], hc["seld"])


def _scan_chunk_fwd(c, rev, q_ref, v_ref, k_scr, b_scr, state, o_ref, seg_ref, bd_ref, sel_ref):
    last = 0 if rev else CHUNK - 1
    rows = pl.ds(pl.multiple_of(c * CHUNK, CHUNK), CHUNK)
    q = q_ref[rows, :]
    k = k_scr[rows, :]
    v = v_ref[rows, :]
    bb = b_scr[rows, :]
    bl = bb[last:last + 1, :]
    pairs = _pair_decay(bb, rev) * _rows_rep(q) * _tile_rows(k)
    a = jnp.dot(pairs.astype(BF16), seg_ref[...], preferred_element_type=F32)
    o_intra = jnp.dot(sel_ref[...], (a * _tile_rows(v)).astype(BF16), preferred_element_type=F32)
    st = state[...]
    o_inter = _dot_nt((q * jnp.exp(bb)).astype(BF16), st.astype(BF16))
    o_ref[rows, :] = o_intra + o_inter
    ke = k * jnp.exp(bl - bb)
    state[...] = st * jnp.exp(bl) + bd_ref[...] * _dot_tn(v.astype(BF16), ke.astype(BF16))


def _scan_chunks_fwd(chains, seg_ref, bd_ref, sel_ref):
    work = []
    for c, rev, q_ref, v_ref, k_scr, b_scr, state, o_ref in chains:
        last = 0 if rev else CHUNK - 1
        rows = pl.ds(pl.multiple_of(c * CHUNK, CHUNK), CHUNK)
        q = q_ref[rows, :]
        k = k_scr[rows, :]
        v = v_ref[rows, :]
        bb = b_scr[rows, :]
        bl = bb[last:last + 1, :]
        st = state[...]
        work.append(dict(
            rows=rows, v=v, st=st, state=state, o_ref=o_ref, decay=jnp.exp(bl),
            pairs=(_pair_decay(bb, rev) * _rows_rep(q) * _tile_rows(k)).astype(BF16),
            qe=(q * jnp.exp(bb)).astype(BF16), ke=(k * jnp.exp(bl - bb)).astype(BF16), st_b=st.astype(BF16)))
    for w in work:
        w["a"] = jnp.dot(w["pairs"], seg_ref[...], preferred_element_type=F32)
        w["o_inter"] = _dot_nt(w["qe"], w["st_b"])
        w["upd"] = _dot_tn(w["v"].astype(BF16), w["ke"])
    for w in work:
        w["av"] = (w["a"] * _tile_rows(w["v"])).astype(BF16)
    for w in work:
        w["o_ref"][w["rows"], :] = jnp.dot(sel_ref[...], w["av"], preferred_element_type=F32) + w["o_inter"]
        w["state"][...] = w["st"] * w["decay"] + bd_ref[...] * w["upd"]


def _hgrn_fwd2(name, proj, lb_f, lb_b, b, s, hc_f, hc_b):
    t = proj.shape[0]
    nb = s // HBLK
    n_ch = HBLK // CHUNK
    n_chain = 2 * b

    def body(qf_ref, zf_ref, vf_ref, qb_ref, zb_ref, vb_ref, lbf_ref, lbb_ref, csf_ref, csb_ref, seg_ref, bd_ref,
             sel_ref, of_ref, ob_ref, stf_ref, stb_ref, *scr):
        state, b_scr, k_scr = scr[:n_chain], scr[n_chain:2 * n_chain], scr[2 * n_chain:]

        @pl.when(pl.program_id(0) == 0)
        def _():
            for st0 in state:
                st0[...] = jnp.zeros_like(st0)

        chains = []
        for bi in range(b):
            chains.append((False, qf_ref.at[bi], zf_ref.at[bi], vf_ref.at[bi], lbf_ref, csf_ref, of_ref.at[bi],
                           stf_ref.at[bi], 2 * bi))
            chains.append((True, qb_ref.at[bi], zb_ref.at[bi], vb_ref.at[bi], lbb_ref, csb_ref, ob_ref.at[bi],
                           stb_ref.at[bi], 2 * bi + 1))
        for rev, q, z, v, lb, cs, o, st, ci in chains:
            st[...] = state[ci][...]
            _, _, g, _, kk = _gates(z[...], lb[...])
            k_scr[ci][...] = kk
            b_scr[ci][...] = _ldot3(cs[...], g)

        def chunk(i, carry):
            _scan_chunks_fwd([((n_ch - 1 - i) if rev else i, rev, q, v, k_scr[ci], b_scr[ci], state[ci], o)
                              for rev, q, z, v, lb, cs, o, st, ci in chains], seg_ref, bd_ref, sel_ref)
            return carry

        lax.fori_loop(0, n_ch, chunk, 0)

    def col(c, rev):
        return pl.BlockSpec((b, HBLK, D_HGRN), lambda j: (0, (nb - 1 - j) if rev else j, c))

    def st_spec(rev):
        return pl.BlockSpec((b, None, D_HGRN, D_HGRN), lambda j: (0, (nb - 1 - j) if rev else j, 0, 0))

    sq = (D_HGRN, D_HGRN)
    proj3 = proj.reshape(b, s, proj.shape[1])
    o_fw, o_bw, st_fw, st_bw = pl.pallas_call(
        body, name=name, grid=(nb,),
        in_specs=[col(COL_HQ, False), col(COL_FF, False), col(COL_HI, False),
                  col(COL_HQ, True), col(COL_FB, True), col(COL_HI, True),
                  _full((1, D_HGRN)), _full((1, D_HGRN)), _full((HBLK, HBLK)), _full((HBLK, HBLK)),
                  _full(sq), _full(sq), _full((CHUNK, CHUNK * CHUNK))],
        out_specs=[col(0, False), col(0, True), st_spec(False), st_spec(True)],
        out_shape=[_sds((b, s, D_HGRN), F32)] * 2 + [_sds((b, nb) + sq, F32)] * 2,
        scratch_shapes=[pltpu.VMEM(sq, F32)] * n_chain + [pltpu.VMEM((HBLK, D_HGRN), F32)] * (2 * n_chain),
        compiler_params=_params(("arbitrary",)),
    )(proj3, proj3, proj3, proj3, proj3, proj3, lb_f, lb_b, hc_f["cs"], hc_b["cs"], hc_f["seg"], hc_f["bd"],
      hc_f["sel"])
    return o_fw.reshape(t, D_HGRN), o_bw.reshape(t, D_HGRN), st_fw, st_bw


def _scan_replay(c, rev, st, v_ref, k_scr, b_scr, states, bd_ref):
    last = 0 if rev else CHUNK - 1
    rows = pl.ds(pl.multiple_of(c * CHUNK, CHUNK), CHUNK)
    states[c] = st
    bb = b_scr[rows, :]
    bl = bb[last:last + 1, :]
    ke = k_scr[rows, :] * jnp.exp(bl - bb)
    return st * jnp.exp(bl) + bd_ref[...] * _dot_tn(v_ref[rows, :].astype(BF16), ke.astype(BF16))


def _scan_replays(chains, bd_ref):
    work = []
    for c, rev, st, v_ref, k_scr, b_scr, states in chains:
        last = 0 if rev else CHUNK - 1
        rows = pl.ds(pl.multiple_of(c * CHUNK, CHUNK), CHUNK)
        states[c] = st
        bb = b_scr[rows, :]
        bl = bb[last:last + 1, :]
        work.append((st, jnp.exp(bl), v_ref[rows, :].astype(BF16), (k_scr[rows, :] * jnp.exp(bl - bb)).astype(BF16)))
    upds = [_dot_tn(v, ke) for _, _, v, ke in work]
    return tuple(st * decay + bd_ref[...] * upd for (st, decay, _, _), upd in zip(work, upds))


def _scan_chunks_bwd(chains, seg_ref, bd_ref, sel_ref, selt_ref, seld_ref):
    row = lax.broadcasted_iota(jnp.int32, (CHUNK, D_HGRN), 0)
    work = []
    for c, rev, q_ref, v_ref, do_ref, k_scr, b_scr, states, dstate, dq_ref, dv_ref, dk_scr, db_scr in chains:
        last = 0 if rev else CHUNK - 1
        rows = pl.ds(pl.multiple_of(c * CHUNK, CHUNK), CHUNK)
        q = q_ref[rows, :]
        k = k_scr[rows, :]
        v = v_ref[rows, :]
        bb = b_scr[rows, :]
        dout = do_ref[rows, :]
        bl = bb[last:last + 1, :]
        st_p = states[c]
        dst_n = dstate[...]
        eb = jnp.exp(bb)
        ebl = jnp.exp(bl - bb)
        qe = q * eb
        ke = k * ebl
        dec = _pair_decay(bb, rev)
        q_rep = _rows_rep(q)
        k_til = _tile_rows(k)
        do_rep = _rows_rep(dout)
        pairs = dec * q_rep * k_til
        work.append(dict(
            rows=rows, last=last, eb=eb, ebl=ebl, ebl_last=jnp.exp(bl), qe=qe, ke=ke, dec=dec, q_rep=q_rep, k_til=k_til,
            do_rep=do_rep, pairs=pairs, st_p=st_p, dst_n=dst_n, dstate=dstate, dq_ref=dq_ref, dv_ref=dv_ref,
            dk_scr=dk_scr, db_scr=db_scr, dob=dout.astype(BF16), dstb=dst_n.astype(BF16), vb=v.astype(BF16),
            pairs_b=pairs.astype(BF16), vdo_b=(_tile_rows(v) * do_rep).astype(BF16)))
    for w in work:
        w["dqe"] = jnp.dot(w["dob"], w["st_p"].astype(BF16), preferred_element_type=F32)
        w["dke"] = jnp.dot(w["vb"], w["dstb"], preferred_element_type=F32)
        w["dv"] = _dot_nt(w["ke"].astype(BF16), w["dstb"])
        w["a"] = jnp.dot(w["pairs_b"], seg_ref[...], preferred_element_type=F32)
        w["wb"] = jnp.dot(w["vdo_b"], seg_ref[...], preferred_element_type=F32)
        w["dst_upd"] = _dot_tn(w["dob"], w["qe"].astype(BF16))
    for w in work:
        gdec = w["wb"] * w["dec"]
        w["x_dq"] = (gdec * w["k_til"]).astype(BF16)
        w["x_dk"] = (gdec * w["q_rep"]).astype(BF16)
        w["x_dv"] = (w["a"] * w["do_rep"]).astype(BF16)
        w["x_db"] = (w["wb"] * w["pairs"]).astype(BF16)
    for w in work:
        dke, dqe = w["dke"], w["dqe"]
        dbl = (jnp.sum(w["dst_n"] * w["st_p"], axis=0, keepdims=True) * w["ebl_last"]
               + jnp.sum(dke * w["ke"], axis=0, keepdims=True))
        dq = dqe * w["eb"] + jnp.dot(sel_ref[...], w["x_dq"], preferred_element_type=F32)
        dk = dke * w["ebl"] + jnp.dot(selt_ref[...], w["x_dk"], preferred_element_type=F32)
        dv = w["dv"] + jnp.dot(selt_ref[...], w["x_dv"], preferred_element_type=F32)
        db = (dqe * w["qe"] - dke * w["ke"] + jnp.dot(seld_ref[...], w["x_db"], preferred_element_type=F32)
              + jnp.where(row == w["last"], dbl, 0.0))
        w["dq_ref"][w["rows"], :] = dq
        w["dv_ref"][w["rows"], :] = dv
        w["dk_scr"][w["rows"], :] = dk
        w["db_scr"][w["rows"], :] = db
        w["dstate"][...] = w["dst_n"] * w["ebl_last"] + bd_ref[...] * w["dst_upd"]


def _scan_chunk_bwd(c, rev, q_ref, v_ref, do_ref, k_scr, b_scr, states, dstate, dq_ref, dv_ref, dk_scr, db_scr,
                    seg_ref, bd_ref, sel_ref, selt_ref, seld_ref):
    last = 0 if rev else CHUNK - 1
    row = lax.broadcasted_iota(jnp.int32, (CHUNK, D_HGRN), 0)
    rows = pl.ds(pl.multiple_of(c * CHUNK, CHUNK), CHUNK)
    q = q_ref[rows, :]
    k = k_scr[rows, :]
    v = v_ref[rows, :]
    bb = b_scr[rows, :]
    dout = do_ref[rows, :]
    bl = bb[last:last + 1, :]
    st_p = states[c]
    dst_n = dstate[...]
    eb = jnp.exp(bb)
    ebl = jnp.exp(bl - bb)
    ebl_last = jnp.exp(bl)
    qe = q * eb
    ke = k * ebl
    dob = dout.astype(BF16)
    dstb = dst_n.astype(BF16)
    dqe = jnp.dot(dob, st_p.astype(BF16), preferred_element_type=F32)
    dke = jnp.dot(v.astype(BF16), dstb, preferred_element_type=F32)
    dv = _dot_nt(ke.astype(BF16), dstb)
    dbl = jnp.sum(dst_n * st_p, axis=0, keepdims=True) * ebl_last + jnp.sum(dke * ke, axis=0, keepdims=True)
    dq = dqe * eb
    dk = dke * ebl
    db = dqe * qe - dke * ke
    dec = _pair_decay(bb, rev)
    q_rep = _rows_rep(q)
    k_til = _tile_rows(k)
    do_rep = _rows_rep(dout)
    pairs = dec * q_rep * k_til
    a = jnp.dot(pairs.astype(BF16), seg_ref[...], preferred_element_type=F32)
    wb = jnp.dot((_tile_rows(v) * do_rep).astype(BF16), seg_ref[...], preferred_element_type=F32)
    gdec = wb * dec
    dq = dq + jnp.dot(sel_ref[...], (gdec * k_til).astype(BF16), preferred_element_type=F32)
    dk = dk + jnp.dot(selt_ref[...], (gdec * q_rep).astype(BF16), preferred_element_type=F32)
    dv = dv + jnp.dot(selt_ref[...], (a * do_rep).astype(BF16), preferred_element_type=F32)
    db = db + jnp.dot(seld_ref[...], (wb * pairs).astype(BF16), preferred_element_type=F32)
    db = db + jnp.where(row == last, dbl, 0.0)
    dq_ref[rows, :] = dq
    dv_ref[rows, :] = dv
    dk_scr[rows, :] = dk
    db_scr[rows, :] = db
    dstate[...] = dst_n * ebl_last + bd_ref[...] * _dot_tn(dob, qe.astype(BF16))


def _hgrn_bwd2(name, proj, lb_f, lb_b, st_f, st_b, do, b, s, hc_f, hc_b):
    t = proj.shape[0]
    nb = s // HBLK
    n_ch = HBLK // CHUNK

    n_chain = 2 * b

    def body(qf_ref, zf_ref, vf_ref, dof_ref, stf_ref, qb_ref, zb_ref, vb_ref, dob_ref, stb_ref, lbf_ref, lbb_ref,
             csf_ref, csb_ref, cstf_ref, cstb_ref, seg_ref, bd_ref, sel_ref, selt_ref, seld_ref,
             dqf_ref, dvf_ref, dzf_ref, dqb_ref, dvb_ref, dzb_ref, dlbf_ref, dlbb_ref,
             *scr):
        dstate, states, b_scr, k_scr, db_scr, dk_scr = [scr[i * n_chain:(i + 1) * n_chain] for i in range(6)]

        @pl.when(pl.program_id(0) == 0)
        def _():
            dlbf_ref[...] = jnp.zeros_like(dlbf_ref)
            dlbb_ref[...] = jnp.zeros_like(dlbb_ref)
            for d0 in dstate:
                d0[...] = jnp.zeros_like(d0)

        chains = []
        for bi in range(b):
            chains.append(dict(rev=False, q=qf_ref.at[bi], z=zf_ref.at[bi], v=vf_ref.at[bi], do=dof_ref.at[bi],
                               st=stf_ref.at[bi], lb=lbf_ref, cs=csf_ref, cst=cstf_ref, dq=dqf_ref.at[bi],
                               dv=dvf_ref.at[bi], dz=dzf_ref.at[bi], dlb=dlbf_ref, ci=2 * bi))
            chains.append(dict(rev=True, q=qb_ref.at[bi], z=zb_ref.at[bi], v=vb_ref.at[bi], do=dob_ref.at[bi],
                               st=stb_ref.at[bi], lb=lbb_ref, cs=csb_ref, cst=cstb_ref, dq=dqb_ref.at[bi],
                               dv=dvb_ref.at[bi], dz=dzb_ref.at[bi], dlb=dlbb_ref, ci=2 * bi + 1))
        for ch in chains:
            sig, f, g, sn, kk = _gates(ch["z"][...], ch["lb"][...])
            k_scr[ch["ci"]][...] = kk
            b_scr[ch["ci"]][...] = _ldot3(ch["cs"][...], g)
            ch["gates"] = (sig, f, sn)

        def replay(i, carry):
            return _scan_replays([((n_ch - 1 - i) if ch["rev"] else i, ch["rev"], st, ch["v"], k_scr[ch["ci"]],
                                   b_scr[ch["ci"]], states[ch["ci"]]) for ch, st in zip(chains, carry)], bd_ref)

        lax.fori_loop(0, n_ch, replay, tuple(ch["st"][...] for ch in chains))

        def chunk(i, carry):
            args = [(i if ch["rev"] else (n_ch - 1 - i), ch["rev"], ch["q"], ch["v"], ch["do"],
                     k_scr[ch["ci"]], b_scr[ch["ci"]], states[ch["ci"]], dstate[ch["ci"]], ch["dq"],
                     ch["dv"], dk_scr[ch["ci"]], db_scr[ch["ci"]]) for ch in chains]
            for g0 in range(0, n_chain, BWD_GROUP):
                _scan_chunks_bwd(args[g0:g0 + BWD_GROUP], seg_ref, bd_ref, sel_ref, selt_ref, seld_ref)
            return carry

        lax.fori_loop(0, n_ch, chunk, 0)
        for ch in chains:
            sig, f, sn = ch["gates"]
            lbv = ch["lb"][...]
            hi, lo = _split2(db_scr[ch["ci"]][...])
            dg = (jnp.dot(ch["cst"][...], hi, preferred_element_type=F32)
                  + jnp.dot(ch["cst"][...], lo, preferred_element_type=F32))
            dgf = jnp.where(f > F_MIN, dg / f, 0.0)
            dk = dk_scr[ch["ci"]][...]
            ch["dz"][...] = dgf * (1.0 - lbv) * sig * (1.0 - sig) - dk * (1.0 - lbv) * sn * (1.0 - sn)
            ch["dlb"][...] += _rowgroups(dgf * (1.0 - sig) - dk * sn)

    def col(c, rev):
        return pl.BlockSpec((b, HBLK, D_HGRN), lambda j: (0, j if rev else (nb - 1 - j), c))

    def st_spec(rev):
        return pl.BlockSpec((b, None, D_HGRN, D_HGRN), lambda j: (0, j if rev else (nb - 1 - j), 0, 0))

    sq = (D_HGRN, D_HGRN)
    blk = (HBLK, D_HGRN)
    pairs_shape = (CHUNK, CHUNK * CHUNK)
    proj3 = proj.reshape(b, s, proj.shape[1])
    do3 = do.reshape(b, s, D_HGRN)
    res = pl.pallas_call(
        body, name=name, grid=(nb,),
        in_specs=[col(COL_HQ, False), col(COL_FF, False), col(COL_HI, False), col(0, False), st_spec(False),
                  col(COL_HQ, True), col(COL_FB, True), col(COL_HI, True), col(0, True), st_spec(True),
                  _full((1, D_HGRN)), _full((1, D_HGRN)), _full((HBLK, HBLK)), _full((HBLK, HBLK)),
                  _full((HBLK, HBLK)), _full((HBLK, HBLK)), _full(sq), _full(sq),
                  _full(pairs_shape), _full(pairs_shape), _full(pairs_shape)],
        out_specs=[col(0, False)] * 3 + [col(0, True)] * 3 + [_full((8, D_HGRN))] * 2,
        out_shape=[_sds((b, s, D_HGRN), F32)] * 6 + [_sds((8, D_HGRN), F32)] * 2,
        scratch_shapes=[pltpu.VMEM(sq, F32)] * n_chain + [pltpu.VMEM((n_ch,) + sq, F32)] * n_chain
        + [pltpu.VMEM(blk, F32)] * (4 * n_chain),
        compiler_params=_params(("arbitrary",)),
    )(proj3, proj3, proj3, do3, st_f, proj3, proj3, proj3, do3, st_b, lb_f, lb_b, hc_f["cs"], hc_b["cs"],
      hc_f["cs_t"], hc_b["cs_t"], hc_f["seg"], hc_f["bd"], hc_f["sel"], hc_f["selt"], hc_f["seld"])
    return [r.reshape(t, D_HGRN) for r in res[:6]] + list(res[6:])


def _lower_bounds(logits):
    n = logits.shape[1]

    def body(x_ref, o_ref):
        x = x_ref[...]
        for d in range(2):
            rows = [x[l * 2 + d:l * 2 + d + 1, :] for l in range(DEPTH)]
            mx = functools.reduce(jnp.maximum, rows)
            ex = [jnp.exp(r - mx) for r in rows]
            tot = functools.reduce(lambda a, c: a + c, ex)
            sm = [e / tot for e in ex]
            run = jnp.zeros_like(sm[0])
            for l in range(DEPTH):
                run = run + sm[l]
                o_ref[l * 2 + d:l * 2 + d + 1, :] = run - sm[0]

    return pl.pallas_call(body, name="hgrn_lower_bounds", out_shape=_sds(logits.shape, F32),
                          in_specs=[_full(logits.shape)], out_specs=_full(logits.shape), grid=(1,),
                          compiler_params=_params(("arbitrary",)))(logits)


def _lower_bounds_bwd(logits, dlb):
    def body(x_ref, g_ref, o_ref):
        x = x_ref[...]
        gv = g_ref[...]
        for d in range(2):
            rows = [x[l * 2 + d:l * 2 + d + 1, :] for l in range(DEPTH)]
            gr = [gv[l * 2 + d:l * 2 + d + 1, :] for l in range(DEPTH)]
            mx = functools.reduce(jnp.maximum, rows)
            ex = [jnp.exp(r - mx) for r in rows]
            tot = functools.reduce(lambda a, c: a + c, ex)
            sm = [e / tot for e in ex]
            dsm = []
            for i in range(DEPTH):
                acc = functools.reduce(lambda a, c: a + c, gr[i:])
                if i == 0:
                    acc = acc - functools.reduce(lambda a, c: a + c, gr)
                dsm.append(acc)
            inner = functools.reduce(lambda a, c: a + c, [sm[i] * dsm[i] for i in range(DEPTH)])
            for i in range(DEPTH):
                o_ref[i * 2 + d:i * 2 + d + 1, :] = sm[i] * (dsm[i] - inner)

    return pl.pallas_call(body, name="hgrn_lower_bounds_bwd", out_shape=_sds(logits.shape, F32),
                          in_specs=[_full(logits.shape), _full(logits.shape)], out_specs=_full(logits.shape),
                          grid=(1,), compiler_params=_params(("arbitrary",)))(logits, dlb)


def _conv_rows(s):
    return s + 2 * (CONV_PAD + 1)


def _conv_fwd(name, proj, dw_w, dw_b, ln_w, ln_b, pw_w, pw_b, b, s):
    t = proj.shape[0]
    pad = CONV_PAD + 1
    nt = s // CONV_TILE

    def body(a_ref, g_ref, w_ref, dwb_ref, lnw_ref, lnb_ref, pw_ref, pwb_ref, y_ref, c_ref, upad, win):
        upad[0:pad, :] = jnp.zeros((pad, D_CONV), F32)
        upad[s + pad:s + 2 * pad, :] = jnp.zeros((pad, D_CONV), F32)

        def fill(i, carry):
            rows = pl.ds(pl.multiple_of(i * CONV_TILE, CONV_TILE), CONV_TILE)
            upad[pl.ds(pl.multiple_of(i * CONV_TILE + pad, pad), CONV_TILE), :] = a_ref[rows, :] * _sigmoid(g_ref[rows, :])
            return carry

        lax.fori_loop(0, nt, fill, 0)

        def tile(i, carry):
            r0 = pl.multiple_of(i * CONV_TILE, CONV_TILE)
            win[...] = upad[pl.ds(r0, CONV_TILE + 2 * pad), :]
            acc = jnp.zeros((CONV_TILE, D_CONV), F32)
            for j in range(CONV_W):
                acc = acc + win[j + 1:j + 1 + CONV_TILE, :] * w_ref[j:j + 1, :]
            c = acc + dwb_ref[...]
            c_ref[pl.ds(r0, CONV_TILE), :] = c
            mu = jnp.mean(c, axis=-1, keepdims=True)
            xc = c - mu
            rstd = lax.rsqrt(jnp.mean(xc * xc, axis=-1, keepdims=True) + LN_EPS)
            n = xc * rstd * lnw_ref[...] + lnb_ref[...]
            y_ref[pl.ds(r0, CONV_TILE), :] = (jnp.dot(_silu(n).astype(BF16), pw_ref[...].astype(BF16),
                                                      preferred_element_type=F32) + pwb_ref[...])
            return carry

        lax.fori_loop(0, nt, tile, 0)

    vec = _full((1, D_CONV))
    return pl.pallas_call(
        body, name=name, grid=(b,),
        in_specs=[pl.BlockSpec((s, D_CONV), lambda bi: (bi, COL_CA)), pl.BlockSpec((s, D_CONV), lambda bi: (bi, COL_CB)),
                  _full((CONV_W + 1, D_CONV)), vec, vec, vec, _full((D_CONV, D_CONV)), vec],
        out_specs=[pl.BlockSpec((s, D_CONV), lambda bi: (bi, 0))] * 2, out_shape=[_sds((t, D_CONV), F32)] * 2,
        scratch_shapes=[pltpu.VMEM((_conv_rows(s), D_CONV), F32), pltpu.VMEM((CONV_TILE + 2 * pad, D_CONV), F32)],
        compiler_params=_params(("parallel",)),
    )(proj, proj, dw_w, dw_b, ln_w, ln_b, pw_w, pw_b)


def _conv_bwd(name, proj, conv_out, dw_w, ln_w, ln_b, pw_w, dy, b, s):
    t = proj.shape[0]
    pad = CONV_PAD + 1
    nt = s // CONV_TILE

    def body(a_ref, g_ref, c_ref, w_ref, lnw_ref, lnb_ref, pw_ref, dy_ref, dab_ref, dpw_ref, ddw_ref, dvec_ref,
             upad, dcpad, tap_acc, win, dwin):
        @pl.when(pl.program_id(0) == 0)
        def _():
            dpw_ref[...] = jnp.zeros_like(dpw_ref)
            ddw_ref[...] = jnp.zeros_like(ddw_ref)
            dvec_ref[...] = jnp.zeros_like(dvec_ref)

        zeros = jnp.zeros((pad, D_CONV), F32)
        upad[0:pad, :] = zeros
        upad[s + pad:s + 2 * pad, :] = zeros
        dcpad[0:pad, :] = zeros
        dcpad[s + pad:s + 2 * pad, :] = zeros
        tap_acc[...] = jnp.zeros_like(tap_acc)

        def inner(i):
            return pl.ds(pl.multiple_of(i * CONV_TILE + pad, pad), CONV_TILE)

        def fill(i, carry):
            rows = pl.ds(pl.multiple_of(i * CONV_TILE, CONV_TILE), CONV_TILE)
            upad[inner(i), :] = a_ref[rows, :] * _sigmoid(g_ref[rows, :])
            return carry

        lax.fori_loop(0, nt, fill, 0)

        def tile_a(i, carry):
            r0 = pl.multiple_of(i * CONV_TILE, CONV_TILE)
            c = c_ref[pl.ds(r0, CONV_TILE), :]
            mu = jnp.mean(c, axis=-1, keepdims=True)
            xc = c - mu
            rstd = lax.rsqrt(jnp.mean(xc * xc, axis=-1, keepdims=True) + LN_EPS)
            xhat = xc * rstd
            n = xhat * lnw_ref[...] + lnb_ref[...]
            dyt = dy_ref[pl.ds(r0, CONV_TILE), :]
            dyb = dyt.astype(BF16)
            dpw_ref[...] += _dot_tn(_silu(n).astype(BF16), dyb)
            dn = _dot_nt(dyb, pw_ref[...].astype(BF16)) * _dsilu(n)
            dxh = dn * lnw_ref[...]
            dc = rstd * (dxh - jnp.mean(dxh, axis=-1, keepdims=True)
                         - xhat * jnp.mean(dxh * xhat, axis=-1, keepdims=True))
            dcpad[inner(i), :] = dc
            dvec_ref[0:1, :] += jnp.sum(dyt, axis=0, keepdims=True)
            dvec_ref[1:2, :] += jnp.sum(dn * xhat, axis=0, keepdims=True)
            dvec_ref[2:3, :] += jnp.sum(dn, axis=0, keepdims=True)
            dvec_ref[3:4, :] += jnp.sum(dc, axis=0, keepdims=True)
            return carry

        lax.fori_loop(0, nt, tile_a, 0)

        def tile_b(i, carry):
            r0 = pl.multiple_of(i * CONV_TILE, CONV_TILE)
            win[...] = upad[pl.ds(r0, CONV_TILE + 2 * pad), :]
            dwin[...] = dcpad[pl.ds(r0, CONV_TILE + 2 * pad), :]
            dct = dwin[pad:pad + CONV_TILE, :]
            du = jnp.zeros((CONV_TILE, D_CONV), F32)
            for j in range(CONV_W):
                du = du + dwin[2 * pad - 1 - j:2 * pad - 1 - j + CONV_TILE, :] * w_ref[j:j + 1, :]
                tap_acc[8 * j:8 * j + 8, :] += _rowgroups(dct * win[j + 1:j + 1 + CONV_TILE, :])
            rows = pl.ds(r0, CONV_TILE)
            sg = _sigmoid(g_ref[rows, :])
            dab_ref[rows, 0:D_CONV] = (du * sg).astype(BF16)
            dab_ref[rows, D_CONV:2 * D_CONV] = (du * a_ref[rows, :] * sg * (1.0 - sg)).astype(BF16)
            return carry

        lax.fori_loop(0, nt, tile_b, 0)
        for j in range(CONV_W):
            ddw_ref[j:j + 1, :] += jnp.sum(tap_acc[8 * j:8 * j + 8, :], axis=0, keepdims=True)

    vec = _full((1, D_CONV))
    return pl.pallas_call(
        body, name=name, grid=(b,),
        in_specs=[pl.BlockSpec((s, D_CONV), lambda bi: (bi, COL_CA)), pl.BlockSpec((s, D_CONV), lambda bi: (bi, COL_CB)),
                  pl.BlockSpec((s, D_CONV), lambda bi: (bi, 0)),
                  _full((CONV_W + 1, D_CONV)), vec, vec, _full((D_CONV, D_CONV)),
                  pl.BlockSpec((s, D_CONV), lambda bi: (bi, 0))],
        out_specs=[pl.BlockSpec((s, 2 * D_CONV), lambda bi: (bi, 0)), _full((D_CONV, D_CONV)),
                   _full((CONV_W + 1, D_CONV)), _full((8, D_CONV))],
        out_shape=[_sds((t, 2 * D_CONV), BF16), _sds((D_CONV, D_CONV), F32), _sds((CONV_W + 1, D_CONV), F32),
                   _sds((8, D_CONV), F32)],
        scratch_shapes=[pltpu.VMEM((_conv_rows(s), D_CONV), F32), pltpu.VMEM((_conv_rows(s), D_CONV), F32),
                        pltpu.VMEM((8 * CONV_W, D_CONV), F32), pltpu.VMEM((CONV_TILE + 2 * pad, D_CONV), F32),
                        pltpu.VMEM((CONV_TILE + 2 * pad, D_CONV), F32)],
        compiler_params=_params(("arbitrary",)),
    )(proj, proj, conv_out, dw_w, ln_w, ln_b, pw_w, dy)


def _mix_fwd(name, y_attn, o_fw, o_bw, proj, y_conv, aw, gw, cw, seg):
    t = y_attn.shape[0]
    tm = _row_tile(t)

    def body(ya_ref, of_ref, ob_ref, hg_ref, yc_ref, aw_ref, gw_ref, cw_ref, seg_ref, o_ref):
        ya = ya_ref[...]
        ra = lax.rsqrt(jnp.mean(ya * ya, axis=-1, keepdims=True) + EPS)
        o_ref[:, 0:D_ATTN] = (ya * ra * aw_ref[...]).astype(BF16)
        o = of_ref[...] + ob_ref[...]
        ro = lax.rsqrt(jnp.dot((o * o).astype(BF16), seg_ref[...], preferred_element_type=F32) + EPS)
        o_ref[:, D_ATTN:D_ATTN + D_HGRN] = (o * ro * gw_ref[...] * _silu(hg_ref[...])).astype(BF16)
        yc = yc_ref[...]
        rc = lax.rsqrt(jnp.mean(yc * yc, axis=-1, keepdims=True) + EPS)
        o_ref[:, D_ATTN + D_HGRN:D_MODEL] = (yc * rc * cw_ref[...]).astype(BF16)

    def tile(w, c=0):
        return pl.BlockSpec((tm, w), lambda i: (i, c))

    return pl.pallas_call(
        body, name=name, grid=(t // tm,),
        in_specs=[tile(D_ATTN), tile(D_HGRN), tile(D_HGRN), tile(D_HGRN, COL_HG), tile(D_CONV),
                  _full((1, D_ATTN)), _full((1, D_HGRN)), _full((1, D_CONV)), _full((D_HGRN, D_HGRN))],
        out_specs=tile(D_MODEL), out_shape=_sds((t, D_MODEL), BF16),
        compiler_params=_params(("parallel",)),
    )(y_attn, o_fw, o_bw, proj, y_conv, aw, gw, cw, seg)


def _mix_bwd(name, dmix, y_attn, o_fw, o_bw, proj, y_conv, aw, gw, cw, seg, deps=()):
    t = y_attn.shape[0]
    tm = _row_tile(t)

    def rms_bwd(x, w, dy):
        r = lax.rsqrt(jnp.mean(x * x, axis=-1, keepdims=True) + EPS)
        gwv = dy * w
        return r * gwv - x * (r * r * r) * jnp.mean(gwv * x, axis=-1, keepdims=True), _rowgroups(dy * x * r)

    def body(dm_ref, ya_ref, of_ref, ob_ref, hg_ref, yc_ref, aw_ref, gw_ref, cw_ref, seg_ref, *rest):
        dya_ref, do_ref, dhg_ref, dyc_ref, daw_ref, dgw_ref, dcw_ref = rest[-7:]

        @pl.when(pl.program_id(0) == 0)
        def _():
            daw_ref[...] = jnp.zeros_like(daw_ref)
            dgw_ref[...] = jnp.zeros_like(dgw_ref)
            dcw_ref[...] = jnp.zeros_like(dcw_ref)

        dya, daw = rms_bwd(ya_ref[...], aw_ref[...], dm_ref[:, 0:D_ATTN])
        dya_ref[...] = dya
        daw_ref[...] += daw
        dyc, dcw = rms_bwd(yc_ref[...], cw_ref[...], dm_ref[:, D_ATTN + D_HGRN:D_MODEL])
        dyc_ref[...] = dyc
        dcw_ref[...] += dcw
        d2 = dm_ref[:, D_ATTN:D_ATTN + D_HGRN]
        o = of_ref[...] + ob_ref[...]
        hg = hg_ref[...]
        ro = lax.rsqrt(jnp.dot((o * o).astype(BF16), seg_ref[...], preferred_element_type=F32) + EPS)
        dn = d2 * _silu(hg)
        dhg_ref[...] = (d2 * o * ro * gw_ref[...] * _dsilu(hg)).astype(BF16)
        gwv = dn * gw_ref[...]
        do_ref[...] = ro * gwv - o * (ro * ro * ro) * _rdot2(gwv * o, seg_ref[...])
        dgw_ref[...] += _rowgroups(dn * o * ro)

    def tile(w, c=0):
        return pl.BlockSpec((tm, w), lambda i: (i, c))

    return pl.pallas_call(
        body, name=name, grid=(t // tm,),
        in_specs=[tile(D_MODEL), tile(D_ATTN), tile(D_HGRN), tile(D_HGRN), tile(D_HGRN, COL_HG), tile(D_CONV),
                  _full((1, D_ATTN)), _full((1, D_HGRN)), _full((1, D_CONV)), _full((D_HGRN, D_HGRN))]
        + [_full(a.shape) for a in deps],
        out_specs=[tile(D_ATTN), tile(D_HGRN), tile(D_HGRN), tile(D_CONV),
                   _full((8, D_ATTN)), _full((8, D_HGRN)), _full((8, D_CONV))],
        out_shape=[_sds((t, D_ATTN), F32), _sds((t, D_HGRN), F32), _sds((t, D_HGRN), BF16), _sds((t, D_CONV), F32),
                   _sds((8, D_ATTN), F32), _sds((8, D_HGRN), F32), _sds((8, D_CONV), F32)],
        compiler_params=_params(("arbitrary",)),
    )(dmix, y_attn, o_fw, o_bw, proj, y_conv, aw, gw, cw, seg, *deps)


def _dproj(name, dp_attn, dq_f, dq_b, dz_fw, dz_bw, dv_f, dv_b, dhg, dp_conv):
    t = dq_f.shape[0]
    tm = _row_tile(t)
    wa, wc = dp_attn.shape[1], dp_conv.shape[1]

    def body(at_ref, qf_ref, qb_ref, zf_ref, zb_ref, vf_ref, vb_ref, hg_ref, cv_ref, o_ref):
        o_ref[:, 0:wa] = at_ref[...]
        cols = (qf_ref[...] + qb_ref[...], zf_ref[...], zb_ref[...], vf_ref[...] + vb_ref[...], hg_ref[...])
        for i, val in enumerate(cols):
            o_ref[:, wa + i * D_HGRN:wa + (i + 1) * D_HGRN] = val.astype(BF16)
        o_ref[:, wa + 5 * D_HGRN:D_IN] = cv_ref[...]

    tile = lambda w: pl.BlockSpec((tm, w), lambda i: (i, 0))
    return pl.pallas_call(
        body, name=name, grid=(t // tm,), in_specs=[tile(wa)] + [tile(D_HGRN)] * 7 + [tile(wc)],
        out_specs=tile(D_IN), out_shape=_sds((t, D_IN), BF16), compiler_params=_params(("parallel",)),
    )(dp_attn, dq_f, dq_b, dz_fw, dz_bw, dv_f, dv_b, dhg, dp_conv)


def _mm_tile(t):
    return min(512, t)


def _resident(shape):
    n = len(shape)
    return pl.BlockSpec(tuple(shape), lambda *_: (0,) * n, pipeline_mode=pl.Buffered(1))


def _w_blk(rows, cols, j_of):
    return pl.BlockSpec((None, rows, cols), lambda *g: (j_of(*g), 0, 0))


def _layer_fwd(l, x, wget, sm, tabs, cst, b, s, deps, target=None):
    t = x.shape[0]
    tm = _mm_tile(t)
    nt = t // tm
    pre = "l%d_" % l
    row = lambda w: pl.BlockSpec((tm, w), lambda i, *_: (i, 0))

    def normed(x_ref, nw_ref):
        xv = x_ref[...]
        r = lax.rsqrt(jnp.mean(xv * xv, axis=-1, keepdims=True) + EPS)
        return (xv * r * nw_ref[...]).astype(BF16)

    def in_body(x_ref, nw_ref, w_ref, *rest):
        o_ref, h_ref = rest[-2:]
        hv = normed(x_ref, nw_ref)
        h_ref[...] = hv
        for j in range(N_CHIP):
            o_ref[:, j * IN_BLK:(j + 1) * IN_BLK] = jnp.dot(hv, w_ref[j], preferred_element_type=F32)

    w_in = wget(l, "w_in", x)
    proj, h1 = pl.pallas_call(
        in_body, name=pre + "in_proj", grid=(nt,),
        in_specs=[row(D_MODEL), _full((1, D_MODEL)), _resident(w_in.shape)] + [_full(a.shape) for a in deps],
        out_specs=[row(D_IN), row(D_MODEL)], out_shape=[_sds((t, D_IN), F32), _sds((t, D_MODEL), BF16)],
        compiler_params=_params(("parallel",)),
    )(x, sm["mix_norm_w"][l], w_in, *deps)
    qn, kr, vr = _attn_prep(pre + "attn_prep", proj, s, tabs, sm["q_norm_w"][l], sm["k_norm_w"][l], cst["attn"])
    y_attn = _attn_fwd(pre + "attn", qn, kr, vr, b, s)
    o_fw, o_bw, st_fw, st_bw = _hgrn_fwd2(pre + "hgrn", proj, sm["lb"][l][0], sm["lb"][l][1], b, s, cst["hg_fw"],
                                          cst["hg_bw"])
    y_conv, conv_out = _conv_fwd(pre + "conv", proj, sm["conv_dw_w"][l], sm["conv_dw_b"][l], sm["conv_ln_w"][l],
                       sm["conv_ln_b"][l], sm["conv_pw_w"][l], sm["conv_pw_b"][l], b, s)
    mixed = _mix_fwd(pre + "mix", y_attn, o_fw, o_bw, proj, y_conv, sm["attn_out_norm_w"][l], sm["gnorm_w"][l],
                     sm["conv_out_norm_w"][l], cst["seg_h"])
    (x1,) = _mm(pre + "out_proj", (nt,),
                [(mixed, row(D_MODEL), wget(l, "w_out", mixed),
                  pl.BlockSpec((N_CHIP, OUT_BLK, D_MODEL), lambda i: (0, 0, 0)), NN)],
                [(x, row(D_MODEL))], [(_sds((t, D_MODEL), F32), row(D_MODEL))],
                lambda tot, xr: (xr + tot,))
    ff3 = pl.BlockSpec((N_CHIP, tm, FF_BLK), lambda i: (0, i, 0))
    ffs = _sds((N_CHIP, t, FF_BLK), BF16)

    def gu_body(x_ref, nw_ref, wg_ref, wu_ref, g_ref, u_ref, a_ref, h_ref):
        hv = normed(x_ref, nw_ref)
        h_ref[...] = hv
        for j in range(N_CHIP):
            gv = jnp.dot(hv, wg_ref[j], preferred_element_type=F32)
            uv = jnp.dot(hv, wu_ref[j], preferred_element_type=F32)
            g_ref[j] = gv.astype(BF16)
            u_ref[j] = uv.astype(BF16)
            a_ref[j] = (_silu(gv) * uv).astype(BF16)

    w_gate, w_up = wget(l, "w_gate", x1), wget(l, "w_up", x1)
    gate, up, act, h2 = pl.pallas_call(
        gu_body, name=pre + "ffn_gate_up", grid=(nt,),
        in_specs=[row(D_MODEL), _full((1, D_MODEL)), _resident(w_gate.shape), _resident(w_up.shape)],
        out_specs=[ff3, ff3, ff3, row(D_MODEL)], out_shape=[ffs, ffs, ffs, _sds((t, D_MODEL), BF16)],
        compiler_params=_params(("parallel",)),
    )(x1, sm["ffn_norm_w"][l], w_gate, w_up)

    def down_body(a_ref, w_ref, x_ref, o_ref):
        tot = x_ref[...]
        for j in range(N_CHIP):
            tot = tot + jnp.dot(a_ref[j], w_ref[j], preferred_element_type=F32)
        o_ref[...] = tot

    def down_loss_body(a_ref, w_ref, x_ref, t_ref, dy_ref, acc_ref):
        tot = x_ref[...]
        for j in range(N_CHIP):
            tot = tot + jnp.dot(a_ref[j], w_ref[j], preferred_element_type=F32)
        e = tot - t_ref[...]
        dy_ref[...] = e * (1.0 / D_MODEL)

        @pl.when(pl.program_id(0) == 0)
        def _():
            acc_ref[...] = jnp.zeros_like(acc_ref)

        acc_ref[...] += _rowgroups(e * e)

    w_down = wget(l, "w_down", act)
    if target is None:
        x2 = pl.pallas_call(
            down_body, name=pre + "ffn_down", grid=(nt,), in_specs=[ff3, _resident(w_down.shape), row(D_MODEL)],
            out_specs=row(D_MODEL), out_shape=_sds((t, D_MODEL), F32), compiler_params=_params(("parallel",)),
        )(act, w_down, x1)
    else:
        x2 = pl.pallas_call(
            down_loss_body, name=pre + "ffn_down_loss", grid=(nt,),
            in_specs=[ff3, _resident(w_down.shape), row(D_MODEL), row(D_MODEL)],
            out_specs=[row(D_MODEL), _full((8, D_MODEL))],
            out_shape=[_sds((t, D_MODEL), F32), _sds((8, D_MODEL), F32)], compiler_params=_params(("arbitrary",)),
        )(act, w_down, x1, target)
    saved = dict(x=x, h1=h1, proj=proj, qn=qn, kr=kr, vr=vr, y_attn=y_attn, o_fw=o_fw, o_bw=o_bw, st_fw=st_fw,
                 st_bw=st_bw, y_conv=y_conv, conv_out=conv_out, mixed=mixed, x1=x1, h2=h2, gate=gate, up=up, act=act)
    return x2, saved


def _layer_bwd(l, dx2, sv, wget, sm, tabs, cst, b, s, on_grads):
    t = dx2.shape[0]
    tm = _mm_tile(t)
    nt = t // tm
    pre = "l%d_" % l
    tk = min(2048, t)
    nk = t // tk
    row = lambda w: pl.BlockSpec((tm, w), lambda i, *_: (i, 0))
    ff3 = pl.BlockSpec((N_CHIP, tm, FF_BLK), lambda i: (0, i, 0))
    ffs = _sds((N_CHIP, t, FF_BLK), BF16)

    w_down, w_gate, w_up = wget(l, "w_down", dx2), wget(l, "w_gate", dx2), wget(l, "w_up", dx2)

    def ddx_body(dx_ref, w_ref, g_ref, u_ref, dg_ref, du_ref):
        dxb = dx_ref[...].astype(BF16)
        for j in range(N_CHIP):
            da = _dot_nt(dxb, w_ref[j])
            g = g_ref[j].astype(F32)
            sg = _sigmoid(g)
            dg_ref[j] = (da * u_ref[j].astype(F32) * (sg * (1.0 + g * (1.0 - sg)))).astype(BF16)
            du_ref[j] = (da * (g * sg)).astype(BF16)

    dgate, dup = pl.pallas_call(
        ddx_body, name=pre + "ffn_down_dx", grid=(nt,), in_specs=[row(D_MODEL), _resident(w_down.shape), ff3, ff3],
        out_specs=[ff3, ff3], out_shape=[ffs, ffs], compiler_params=_params(("parallel",)),
    )(dx2, w_down, sv["gate"], sv["up"])
    colt = lambda w: pl.BlockSpec((tk, w), lambda j, k: (k, 0))
    fft = pl.BlockSpec((None, tk, FF_BLK), lambda j, k: (j, k, 0))
    (g_down,) = _mm(pre + "ffn_down_dw", (N_CHIP, nk), [(sv["act"], fft, dx2, colt(D_MODEL), TN)], [],
                    [(_sds((N_CHIP, FF_BLK, D_MODEL), BF16), pl.BlockSpec((None, FF_BLK, D_MODEL), lambda j, k: (j, 0, 0)))],
                    lambda tot: (tot,), acc=(1, (FF_BLK, D_MODEL)))
    wff = pl.BlockSpec((None, D_MODEL, FF_BLK), lambda j, k: (j, 0, 0))
    (g_gate,) = _mm(pre + "ffn_gate_dw", (N_CHIP, nk), [(sv["h2"], colt(D_MODEL), dgate, fft, TN)], [],
                    [(_sds((N_CHIP, D_MODEL, FF_BLK), BF16), wff)], lambda tot: (tot,), acc=(1, (D_MODEL, FF_BLK)))
    (g_up,) = _mm(pre + "ffn_up_dw", (N_CHIP, nk), [(sv["h2"], colt(D_MODEL), dup, fft, TN)], [],
                  [(_sds((N_CHIP, D_MODEL, FF_BLK), BF16), wff)], lambda tot: (tot,), acc=(1, (D_MODEL, FF_BLK)))

    def norm_bwd_tail(dh, x_ref, nw_ref, dres_ref, dx_ref, dw_ref):
        xv = x_ref[...]
        r = lax.rsqrt(jnp.mean(xv * xv, axis=-1, keepdims=True) + EPS)
        gw = dh * nw_ref[...]
        dx_ref[...] = dres_ref[...] + r * gw - xv * (r * r * r) * jnp.mean(gw * xv, axis=-1, keepdims=True)

        @pl.when(pl.program_id(0) == 0)
        def _():
            dw_ref[...] = jnp.zeros_like(dw_ref)

        dw_ref[...] += _rowgroups(dh * xv * r)

    def dh_body(dg_ref, du_ref, wg_ref, wu_ref, x_ref, nw_ref, dres_ref, *rest):
        tot = None
        for j in range(N_CHIP):
            r = _dot_nt(dg_ref[j], wg_ref[j]) + _dot_nt(du_ref[j], wu_ref[j])
            tot = r if tot is None else tot + r
        norm_bwd_tail(tot, x_ref, nw_ref, dres_ref, *rest[-2:])

    deps = on_grads(l, dict(w_gate=g_gate, w_up=g_up, w_down=g_down))
    dx1, d_ffn_norm = pl.pallas_call(
        dh_body, name=pre + "ffn_dh", grid=(nt,),
        in_specs=[ff3, ff3, _resident(w_gate.shape), _resident(w_up.shape), row(D_MODEL), _full((1, D_MODEL)),
                  row(D_MODEL)] + [_full(a.shape) for a in deps],
        out_specs=[row(D_MODEL), _full((8, D_MODEL))], out_shape=[_sds((t, D_MODEL), F32), _sds((8, D_MODEL), F32)],
        compiler_params=_params(("arbitrary",)),
    )(dgate, dup, w_gate, w_up, sv["x1"], sm["ffn_norm_w"][l], dx2, *deps)

    (dmix,) = _mm(pre + "out_proj_dx", (nt,),
                  [(dx1, row(D_MODEL), wget(l, "w_out", dx2),
                    pl.BlockSpec((N_CHIP, OUT_BLK, D_MODEL), lambda i: (0, 0, 0)), NT)], [],
                  [(_sds((t, D_MODEL), F32), row(D_MODEL))], lambda tot: (tot,))
    (g_out,) = _mm(pre + "out_proj_dw", (N_CHIP, nk),
                   [(sv["mixed"], pl.BlockSpec((tk, OUT_BLK), lambda j, k: (k, j)), dx1, colt(D_MODEL), TN)], [],
                   [(_sds((N_CHIP, OUT_BLK, D_MODEL), BF16), pl.BlockSpec((None, OUT_BLK, D_MODEL), lambda j, k: (j, 0, 0)))],
                   lambda tot: (tot,), acc=(1, (OUT_BLK, D_MODEL)))
    proj = sv["proj"]
    dya, do_h, dhg, dyc, d_aw, d_gw, d_cw = _mix_bwd(
        pre + "mix_bwd", dmix, sv["y_attn"], sv["o_fw"], sv["o_bw"], proj, sv["y_conv"],
        sm["attn_out_norm_w"][l], sm["gnorm_w"][l], sm["conv_out_norm_w"][l], cst["seg_h"],
        on_grads(l, dict(w_out=g_out)))
    dqs, dkr, dvr = _attn_bwd(pre + "attn_bwd", sv["qn"], sv["kr"], sv["vr"], dya, b, s)
    dp_attn, d_qw, d_kw = _attn_prep_bwd(pre + "attn_prep_bwd", proj, s, tabs, sm["q_norm_w"][l], sm["k_norm_w"][l],
                                         cst["attn"], dqs, dkr, dvr)
    dq_f, dv_f, dz_fw, dq_b, dv_b, dz_bw, dlb_fw, dlb_bw = _hgrn_bwd2(
        pre + "hgrn_bwd", proj, sm["lb"][l][0], sm["lb"][l][1], sv["st_fw"], sv["st_bw"], do_h, b, s,
        cst["hg_fw"], cst["hg_bw"])
    dp_conv, d_pw, d_dw, d_cvec = _conv_bwd(pre + "conv_bwd", proj, sv["conv_out"], sm["conv_dw_w"][l],
                                            sm["conv_ln_w"][l], sm["conv_ln_b"][l], sm["conv_pw_w"][l], dyc, b, s)
    dproj = _dproj(pre + "dproj", dp_attn, dq_f, dq_b, dz_fw, dz_bw, dv_f, dv_b, dhg, dp_conv)
    g_pw = d_pw.reshape(N_CHIP, D_CONV // N_CHIP, D_CONV).astype(BF16)

    (g_in,) = _mm(pre + "in_proj_dw", (N_CHIP, nk),
                  [(sv["h1"], colt(D_MODEL), dproj, pl.BlockSpec((tk, IN_BLK), lambda j, k: (k, j)), TN)], [],
                  [(_sds((N_CHIP, D_MODEL, IN_BLK), BF16), pl.BlockSpec((None, D_MODEL, IN_BLK), lambda j, k: (j, 0, 0)))],
                  lambda tot: (tot,), acc=(1, (D_MODEL, IN_BLK)))

    def indx_body(dp_ref, w_ref, x_ref, nw_ref, dres_ref, *rest):
        tot = None
        for j in range(N_CHIP):
            r = _dot_nt(dp_ref[:, j * IN_BLK:(j + 1) * IN_BLK], w_ref[j])
            tot = r if tot is None else tot + r
        norm_bwd_tail(tot, x_ref, nw_ref, dres_ref, *rest[-2:])

    w_in = wget(l, "w_in", dx2)
    deps = on_grads(l, dict(w_in=g_in, conv_pw_w=g_pw))
    dx, d_mix_norm = pl.pallas_call(
        indx_body, name=pre + "in_proj_dx", grid=(nt,),
        in_specs=[row(D_IN), _resident(w_in.shape), row(D_MODEL), _full((1, D_MODEL)), row(D_MODEL)]
        + [_full(a.shape) for a in deps],
        out_specs=[row(D_MODEL), _full((8, D_MODEL))], out_shape=[_sds((t, D_MODEL), F32), _sds((8, D_MODEL), F32)],
        compiler_params=_params(("arbitrary",)),
    )(dproj, w_in, sv["x"], sm["mix_norm_w"][l], dx1, *deps)
    heads = lambda v, n: v.sum(axis=0).reshape(n, HEAD_DIM).sum(axis=0)
    small = dict(
        mix_norm_w=d_mix_norm.sum(axis=0), q_norm_w=heads(d_qw, D_ATTN // HEAD_DIM), k_norm_w=heads(d_kw, N_KV),
        lb=jnp.stack([dlb_fw.sum(axis=0), dlb_bw.sum(axis=0)]), hgrn_gnorm_w=heads(d_gw, D_HGRN // HEAD_DIM),
        conv_dw_w=d_dw[:CONV_W], conv_dw_b=d_cvec[3], conv_ln_w=d_cvec[1], conv_ln_b=d_cvec[2],
        conv_pw_b=d_cvec[0], attn_out_norm_w=d_aw.sum(axis=0), conv_out_norm_w=d_cw.sum(axis=0),
        ffn_norm_w=d_ffn_norm.sum(axis=0))
    return dx, small


SMALL_ORDER = ("mix_norm_w", "q_norm_w", "k_norm_w", "lb", "hgrn_gnorm_w", "conv_dw_w", "conv_dw_b", "conv_ln_w",
               "conv_ln_b", "conv_pw_b", "attn_out_norm_w", "conv_out_norm_w", "ffn_norm_w")
BIG_ORDER = ("w_in", "w_out", "w_gate", "w_up", "w_down")
SCATTER_ORDER = BIG_ORDER + ("conv_pw_w",)


def _local_step(x, target, wget, sm, deps, on_grads):
    b, s, d = x.shape
    t = b * s
    cos, sin = _rope_tables(s)
    tabs = dict(cq=jnp.tile(cos, (1, D_ATTN // HEAD_DIM)), sq=jnp.tile(sin, (1, D_ATTN // HEAD_DIM)),
                ck=jnp.tile(cos, (1, N_KV)), sk=jnp.tile(sin, (1, N_KV)))
    cst = dict(attn=_attn_consts(), hg_fw=_hgrn_consts(False), hg_bw=_hgrn_consts(True),
               seg_h=_bf(_seg_matrix(D_HGRN, HEAD_DIM, 1.0 / HEAD_DIM)))
    vec = lambda a: a.reshape(DEPTH, 1, -1)
    smk = dict(sm)
    for n in ("mix_norm_w", "conv_dw_b", "conv_ln_w", "conv_ln_b", "conv_pw_b", "attn_out_norm_w", "conv_out_norm_w",
              "ffn_norm_w"):
        smk[n] = vec(sm[n])
    smk["q_norm_w"] = vec(jnp.tile(sm["q_norm_w"], (1, D_ATTN // HEAD_DIM)))
    smk["k_norm_w"] = vec(jnp.tile(sm["k_norm_w"], (1, N_KV)))
    smk["gnorm_w"] = vec(jnp.tile(sm["hgrn_gnorm_w"], (1, D_HGRN // HEAD_DIM)))
    smk["lb"] = sm["lb"].reshape(DEPTH, 2, 1, D_HGRN)
    smk["conv_dw_w"] = jnp.pad(sm["conv_dw_w"], ((0, 0), (0, 1), (0, 0)))

    h = x.reshape(t, d)
    saved = []
    for l in range(DEPTH):
        h, sv = _layer_fwd(l, h, wget, smk, tabs, cst, b, s, deps if l == 0 else (),
                           target.reshape(t, d) if l == DEPTH - 1 else None)
        saved.append(sv)
    dy, sq = h
    sq_sum = jnp.sum(sq)
    dh = dy
    smalls = [None] * DEPTH
    for l in reversed(range(DEPTH)):
        dh, smalls[l] = _layer_bwd(l, dh, saved[l], wget, smk, tabs, cst, b, s, on_grads)
    return sq_sum, dh.reshape(b, s, d), smalls


HBM_SPEC = pl.BlockSpec(memory_space=pltpu.HBM)


def _exchange(name, arrs, mode):
    n = len(arrs)
    if mode == "gather8":
        flips = [(fx, fy, fc) for fx in (0, 1) for fy in (0, 1) for fc in (0, 1)][1:]
    elif mode == "sibling":
        flips = [(0, 0, 1)]
    else:
        flips = [(1, 0, 0), (0, 1, 0), (1, 1, 0)]
    n_f = len(flips)

    def body(*refs):
        ins, outs = refs[:n], refs[n:2 * n]
        send_sems, recv_sems, local_sems = refs[2 * n:]
        x, y, c = lax.axis_index("x"), lax.axis_index("y"), lax.axis_index("c")

        def slot_of(px, py, pc):
            return (2 * px + py) if mode != "gather8" else (4 * px + 2 * py + pc)

        me = slot_of(x, y, c)
        started = []
        for i in range(n):
            if mode != "sibling":
                src = ins[i].at[me] if mode == "scatter4" else ins[i]
                loc = pltpu.make_async_copy(src, outs[i].at[me], local_sems.at[i])
                loc.start()
                started.append(loc)
        sends, recvs = [], []
        for i in range(n):
            for f, (fx, fy, fc) in enumerate(flips):
                peer = (x ^ fx, y ^ fy, c ^ fc)
                ps = slot_of(*peer)
                if mode == "sibling":
                    src, dst, landed = ins[i], outs[i], outs[i]
                elif mode == "scatter4":
                    src, dst, landed = ins[i].at[ps], outs[i].at[me], outs[i].at[ps]
                else:
                    src, dst, landed = ins[i], outs[i].at[me], outs[i].at[ps]
                k = i * n_f + f
                cp = pltpu.make_async_remote_copy(src_ref=src, dst_ref=dst, send_sem=send_sems.at[k],
                                                  recv_sem=recv_sems.at[k], device_id=peer,
                                                  device_id_type=pl.DeviceIdType.MESH)
                cp.start()
                sends.append(cp)
                recvs.append(pltpu.make_async_remote_copy(src_ref=src, dst_ref=landed, send_sem=send_sems.at[k],
                                                          recv_sem=recv_sems.at[k], device_id=peer,
                                                          device_id_type=pl.DeviceIdType.MESH))
        for cp in sends:
            cp.wait_send()
        for cp in recvs:
            cp.wait_recv()
        for loc in started:
            loc.wait()

    def out_sds(a):
        if mode == "gather4":
            return _sds((N_CHIP,) + a.shape, a.dtype)
        if mode == "gather8":
            return _sds((N_DEV,) + a.shape, a.dtype)
        return _sds(a.shape, a.dtype)

    res = pl.pallas_call(
        body, name=name, in_specs=[HBM_SPEC] * n, out_specs=[HBM_SPEC] * n, out_shape=[out_sds(a) for a in arrs],
        scratch_shapes=[pltpu.SemaphoreType.DMA((n * n_f,)), pltpu.SemaphoreType.DMA((n * n_f,)),
                        pltpu.SemaphoreType.DMA((max(n, 1),))],
    )(*arrs)
    return list(res)


SEM_SPEC = pl.BlockSpec(memory_space=pltpu.SEMAPHORE)
SPLIT_EFFECT = pltpu.SideEffectType.DATAFLOW_SIDE_EFFECTING
CHIP_FLIPS = ((1, 0), (0, 1), (1, 1))


def _chip_copies(src_refs, land_refs, send_sems, recv_sems, scatter):
    x, y, c = lax.axis_index("x"), lax.axis_index("y"), lax.axis_index("c")
    me = 2 * x + y
    out = []
    for i, land in enumerate(land_refs):
        if scatter == "sibling":
            kw = dict(send_sem=send_sems.at[i], recv_sem=recv_sems.at[i], device_id=(x, y, 1 - c),
                      device_id_type=pl.DeviceIdType.MESH)
            cp = pltpu.make_async_remote_copy(src_ref=src_refs[i], dst_ref=land, **kw)
            out.append((cp, cp))
            continue
        for f, (fx, fy) in enumerate(CHIP_FLIPS):
            peer = (x ^ fx, y ^ fy, c)
            ps = 2 * (x ^ fx) + (y ^ fy)
            src = src_refs[i].at[ps] if scatter else land.at[me]
            k = i * len(CHIP_FLIPS) + f
            kw = dict(send_sem=send_sems.at[k], recv_sem=recv_sems.at[k], device_id=peer,
                      device_id_type=pl.DeviceIdType.MESH)
            out.append((pltpu.make_async_remote_copy(src_ref=src, dst_ref=land.at[me], **kw),
                        pltpu.make_async_remote_copy(src_ref=src, dst_ref=land.at[ps], **kw)))
    return out


def _split_start(name, srcs, lands, scatter):
    n = len(lands)
    n_src = len(srcs)
    n_sem = n if scatter == "sibling" else n * len(CHIP_FLIPS)

    def body(*refs):
        src_refs = refs[:n_src]
        land_refs = refs[n_src:n_src + n]
        send_sems, recv_sems = refs[n_src + n], refs[n_src + n + 1]
        token = refs[-1]
        for start, _ in _chip_copies(src_refs, land_refs, send_sems, recv_sems, scatter):
            start.start()
        token[...] = jnp.zeros_like(token)

    arrs = list(srcs) + list(lands)
    res = pl.pallas_call(
        body, name=name,
        out_shape=(pltpu.SemaphoreType.DMA((n_sem,)), pltpu.SemaphoreType.DMA((n_sem,)),
                   *[pltpu.HBM(a.shape, a.dtype) for a in arrs], _sds((8, LANES), F32)),
        in_specs=[HBM_SPEC] * len(arrs),
        out_specs=(SEM_SPEC, SEM_SPEC, *[HBM_SPEC] * len(arrs), pl.BlockSpec(memory_space=pltpu.VMEM)),
        input_output_aliases={i: 2 + i for i in range(len(arrs))},
        compiler_params=pltpu.CompilerParams(has_side_effects=SPLIT_EFFECT),
    )(*[pltpu.with_memory_space_constraint(a, pltpu.HBM) for a in arrs])
    return dict(send=res[0], recv=res[1], srcs=list(res[2:2 + n_src]), lands=list(res[2 + n_src:2 + n_src + n]),
                token=res[-1], scatter=scatter)


def _split_wait(name, started, after, with_srcs=False):
    srcs, lands, scatter = started["srcs"], started["lands"], started["scatter"]
    n, n_src = len(lands), len(srcs)

    def body(*refs):
        src_refs = refs[:n_src]
        land_refs = refs[n_src:n_src + n]
        send_sems, recv_sems = refs[n_src + n], refs[n_src + n + 1]
        for _, wait in _chip_copies(src_refs, land_refs, send_sems, recv_sems, scatter):
            wait.wait_send()
            wait.wait_recv()

    arrs = list(srcs) + list(lands)
    res = pl.pallas_call(
        body, name=name, out_shape=tuple(pltpu.HBM(a.shape, a.dtype) for a in arrs),
        in_specs=[HBM_SPEC] * len(arrs) + [SEM_SPEC, SEM_SPEC, pl.BlockSpec(memory_space=pl.ANY)],
        out_specs=tuple([HBM_SPEC] * len(arrs)), input_output_aliases={i: i for i in range(len(arrs))},
        compiler_params=pltpu.CompilerParams(has_side_effects=SPLIT_EFFECT),
    )(*arrs, started["send"], started["recv"], after)
    return (list(res[:n_src]), list(res[n_src:])) if with_srcs else list(res[n_src:])


def _flat_tile(rows):
    for cand in (512, 256, 128, 64, 32, 16, 8):
        if rows % cand == 0:
            return cand
    return rows


def _cast_slot(name, a, l, chip, layers=DEPTH, dtype=BF16, deps=()):
    r, c = a.shape[0] // layers, a.shape[1]
    tr = _flat_tile(r)

    def body(chip_ref, a_ref, *rest):
        rest[-1][...] = a_ref[...].astype(dtype)

    return pl.pallas_call(
        body, name=name, out_shape=_sds((N_CHIP, r, c), dtype),
        grid_spec=pltpu.PrefetchScalarGridSpec(
            num_scalar_prefetch=1, grid=(r // tr,),
            in_specs=[pl.BlockSpec((tr, c), lambda i, ch: (l * (r // tr) + i, 0))]
            + [pl.BlockSpec(d.shape, lambda i, ch: (0, 0)) for d in deps],
            out_specs=pl.BlockSpec((None, tr, c), lambda i, ch: (ch[0], i, 0))),
        compiler_params=_params(("parallel",)))(chip, a, *deps)


def _own_slot(name, g, chip):
    n, r, c = g.shape
    tr = _flat_tile(r)

    def body(chip_ref, g_ref, o_ref):
        o_ref[...] = g_ref[...]

    spec = pl.BlockSpec((None, tr, c), lambda i, ch: (ch[0], i, 0))
    return pl.pallas_call(
        body, name=name, out_shape=_sds(g.shape, g.dtype),
        grid_spec=pltpu.PrefetchScalarGridSpec(num_scalar_prefetch=1, grid=(r // tr,), in_specs=[spec], out_specs=spec),
        compiler_params=_params(("parallel",)))(chip, g)


def _sum_layers(name, lands):
    n, r, c = lands[0].shape
    tr = _flat_tile(r)
    nl = len(lands)

    def body(*refs):
        o_ref = refs[-1]
        for k in range(nl):
            @pl.when(pl.program_id(0) == k)
            def _():
                tot = refs[k][0].astype(F32)
                for i in range(1, n):
                    tot = tot + refs[k][i].astype(F32)
                o_ref[...] = tot

    return pl.pallas_call(
        body, name=name, grid=(nl, r // tr),
        in_specs=[pl.BlockSpec((n, tr, c), lambda l, i, k=k: (0, jnp.where(l == k, i, 0), 0)) for k in range(nl)],
        out_specs=pl.BlockSpec((tr, c), lambda l, i: (l * (r // tr) + i, 0)), out_shape=_sds((nl * r, c), F32),
        compiler_params=_params(("arbitrary", "arbitrary")))(*lands)


def _sum_slots(name, a, scale=None):
    n, r, c = a.shape
    tr = _flat_tile(r)

    def body(a_ref, o_ref):
        tot = a_ref[0].astype(F32)
        for i in range(1, n):
            tot = tot + a_ref[i].astype(F32)
        o_ref[...] = tot

    return pl.pallas_call(body, name=name, grid=(r // tr,),
                          in_specs=[pl.BlockSpec((n, tr, c), lambda i: (0, i, 0))],
                          out_specs=pl.BlockSpec((tr, c), lambda i: (i, 0)), out_shape=_sds((r, c), F32),
                          compiler_params=_params(("parallel",)))(a)


def _adamw(name, w, ga, gb, m, v):
    r, c = w.shape
    tr = _flat_tile(r)
    c1 = 1.0 - B1 ** STEP
    c2 = 1.0 - B2 ** STEP
    two = gb is not None

    def body(*refs):
        if two:
            w_ref, ga_ref, gb_ref, m_ref, v_ref, g_out, d_out, m_out, v_out = refs
            g = ga_ref[...] + gb_ref[...]
        else:
            w_ref, ga_ref, m_ref, v_ref, g_out, d_out, m_out, v_out = refs
            g = ga_ref[...]
        mn = B1 * m_ref[...] + (1.0 - B1) * g
        vn = B2 * v_ref[...] + (1.0 - B2) * (g * g)
        g_out[...] = g
        m_out[...] = mn
        v_out[...] = vn
        d_out[...] = -LR * ((mn / c1) / (jnp.sqrt(vn / c2) + ADAM_EPS) + WD * w_ref[...])

    spec = pl.BlockSpec((tr, c), lambda i: (i, 0))
    ins = [w, ga, gb, m, v] if two else [w, ga, m, v]
    return pl.pallas_call(body, name=name, grid=(r // tr,), in_specs=[spec] * len(ins), out_specs=[spec] * 4,
                          out_shape=[_sds((r, c), F32)] * 4, compiler_params=_params(("parallel",)))(*ins)


WEIGHTS = ('mix_norm_w', 'w_in', 'q_norm_w', 'k_norm_w', 'hgrn_lb_logits', 'hgrn_gnorm_w', 'conv_dw_w', 'conv_dw_b',
           'conv_ln_w', 'conv_ln_b', 'conv_pw_w', 'conv_pw_b', 'attn_out_norm_w', 'conv_out_norm_w', 'w_out',
           'ffn_norm_w', 'w_gate', 'w_up', 'w_down')
SHARDED_SMALL = {"hgrn_lb_logits": 2, "conv_dw_w": 2, "conv_pw_w": 1}
LANES = 128
PACK_ROWS = 256


def _pack(parts):
    flat = jnp.concatenate([p.reshape(-1) for p in parts])
    n = flat.shape[0]
    rows = -(-n // (PACK_ROWS * LANES)) * PACK_ROWS
    return jnp.pad(flat, (0, rows * LANES - n)).reshape(rows, LANES)


def _unpack(packed, shapes):
    flat = packed.reshape(-1)
    out, off = [], 0
    for shp in shapes:
        n = int(np.prod(shp))
        out.append(flat[off:off + n].reshape(shp))
        off += n
    return out


def kernel(x, mix_norm_w, w_in, q_norm_w, k_norm_w, hgrn_lb_logits, hgrn_gnorm_w, conv_dw_w, conv_dw_b, conv_ln_w, conv_ln_b, conv_pw_w, conv_pw_b, attn_out_norm_w, conv_out_norm_w, w_out, ffn_norm_w, w_gate, w_up, w_down, loss_target, m_mix_norm_w, m_w_in, m_q_norm_w, m_k_norm_w, m_hgrn_lb_logits, m_hgrn_gnorm_w, m_conv_dw_w, m_conv_dw_b, m_conv_ln_w, m_conv_ln_b, m_conv_pw_w, m_conv_pw_b, m_attn_out_norm_w, m_conv_out_norm_w, m_w_out, m_ffn_norm_w, m_w_gate, m_w_up, m_w_down, v_mix_norm_w, v_w_in, v_q_norm_w, v_k_norm_w, v_hgrn_lb_logits, v_hgrn_gnorm_w, v_conv_dw_w, v_conv_dw_b, v_conv_ln_w, v_conv_ln_b, v_conv_pw_w, v_conv_pw_b, v_attn_out_norm_w, v_conv_out_norm_w, v_w_out, v_ffn_norm_w, v_w_gate, v_w_up, v_w_down):
    w = dict(mix_norm_w=mix_norm_w, w_in=w_in, q_norm_w=q_norm_w, k_norm_w=k_norm_w, hgrn_lb_logits=hgrn_lb_logits,
             hgrn_gnorm_w=hgrn_gnorm_w, conv_dw_w=conv_dw_w, conv_dw_b=conv_dw_b, conv_ln_w=conv_ln_w,
             conv_ln_b=conv_ln_b, conv_pw_w=conv_pw_w, conv_pw_b=conv_pw_b, attn_out_norm_w=attn_out_norm_w,
             conv_out_norm_w=conv_out_norm_w, w_out=w_out, ffn_norm_w=ffn_norm_w, w_gate=w_gate, w_up=w_up,
             w_down=w_down)
    m = dict(mix_norm_w=m_mix_norm_w, w_in=m_w_in, q_norm_w=m_q_norm_w, k_norm_w=m_k_norm_w,
             hgrn_lb_logits=m_hgrn_lb_logits, hgrn_gnorm_w=m_hgrn_gnorm_w, conv_dw_w=m_conv_dw_w,
             conv_dw_b=m_conv_dw_b, conv_ln_w=m_conv_ln_w, conv_ln_b=m_conv_ln_b, conv_pw_w=m_conv_pw_w,
             conv_pw_b=m_conv_pw_b, attn_out_norm_w=m_attn_out_norm_w, conv_out_norm_w=m_conv_out_norm_w,
             w_out=m_w_out, ffn_norm_w=m_ffn_norm_w, w_gate=m_w_gate, w_up=m_w_up, w_down=m_w_down)
    v = dict(mix_norm_w=v_mix_norm_w, w_in=v_w_in, q_norm_w=v_q_norm_w, k_norm_w=v_k_norm_w,
             hgrn_lb_logits=v_hgrn_lb_logits, hgrn_gnorm_w=v_hgrn_gnorm_w, conv_dw_w=v_conv_dw_w,
             conv_dw_b=v_conv_dw_b, conv_ln_w=v_conv_ln_w, conv_ln_b=v_conv_ln_b, conv_pw_w=v_conv_pw_w,
             conv_pw_b=v_conv_pw_b, attn_out_norm_w=v_attn_out_norm_w, conv_out_norm_w=v_conv_out_norm_w,
             w_out=v_w_out, ffn_norm_w=v_ffn_norm_w, w_gate=v_w_gate, w_up=v_w_up, w_down=v_w_down)
    chip = 2 * lax.axis_index("x") + lax.axis_index("y")

    chip1 = chip.reshape(1).astype(jnp.int32)

    flat2 = lambda a: a.reshape(-1, a.shape[-1])
    groups = [[(l, "w_in")] if first else [(l, n) for n in BIG_ORDER[1:]] for l in range(DEPTH) for first in (1, 0)]
    group_of = {key: g for g, keys in enumerate(groups) for key in keys}
    groups[0].append((0, "small"))
    starts = []
    for g, keys in enumerate(groups):
        after_prev = [starts[-1]["token"]] if starts else []
        slots = [_cast_slot("cast_small", _pack([w[k] for k in SHARDED_SMALL]), 0, chip1, 1, F32, after_prev)
                 if n == "small" else _cast_slot("cast_%s_l%d" % (n, l), flat2(w[n]), l, chip1, deps=after_prev)
                 for l, n in keys]
        starts.append(_split_start("gather_start_g%d" % g, [], slots, False))
    got = {}

    def wget(l, name, after):
        if (l, name) not in got:
            g = group_of[(l, name)]
            for key, arr in zip(groups[g], _split_wait("gather_wait_g%d" % g, starts[g], after)):
                got[key] = arr
        return got[(l, name)]

    pending = []

    def on_grads(l, grads):
        names = [n for n in SCATTER_ORDER if n in grads]
        own = [_own_slot("own_%s_l%d" % (n, l), grads[n], chip1) for n in names]
        st = _split_start("scatter_start_l%d_%s" % (l, names[0]), [grads[n] for n in names], own, True)
        pending.append((l, names, st))
        return [st["token"]]

    wget(0, "w_in", starts[-1]["token"])
    gathered_small = got[(0, "small")]
    parts = [_unpack(gathered_small[j], [w[n].shape for n in SHARDED_SMALL]) for j in range(N_CHIP)]
    full_small = {n: jnp.concatenate([parts[j][i] for j in range(N_CHIP)], axis=ax)
                  for i, (n, ax) in enumerate(SHARDED_SMALL.items())}
    sm = {n: w[n] for n in WEIGHTS if n not in BIG_ORDER and n not in SHARDED_SMALL}
    sm["conv_dw_w"] = full_small["conv_dw_w"]
    sm["conv_pw_w"] = full_small["conv_pw_w"]
    logits = full_small["hgrn_lb_logits"].reshape(DEPTH * 2, D_HGRN)
    sm["lb"] = _lower_bounds(logits).reshape(DEPTH, 2, D_HGRN)

    sq_sum, grad_x, smalls = _local_step(x, loss_target, wget, sm, [st["token"] for st in starts], on_grads)
    loss = lax.psum(0.5 * sq_sum / D_MODEL, ("x", "y", "c"))

    landed = {}
    for l, names, st in pending:
        for n, arr in zip(names, _split_wait("scatter_wait_l%d_%s" % (l, names[0]), st, grad_x)):
            landed[(l, n)] = arr
    sums = [_sum_layers("sum_" + n, [landed[(l, n)] for l in range(DEPTH)]) for n in SCATTER_ORDER]
    sib_start = _split_start("sibling_start", sums, [lax.empty(a.shape, a.dtype) for a in sums], "sibling")
    out = {}

    small_names = [n for n in WEIGHTS if n not in SCATTER_ORDER]
    g_pack = _pack([jnp.stack([smalls[l][n] for l in range(DEPTH)]) for n in SMALL_ORDER]) + sib_start["token"][0, 0]
    g_all = _exchange("gather_small_grads", [g_pack], "gather8")[0]
    g_tot = _sum_slots("sum_small", g_all)
    shapes = [(DEPTH,) + tuple(smalls[0][n].shape) for n in SMALL_ORDER]
    g_small = dict(zip(SMALL_ORDER, _unpack(g_tot, shapes)))
    lb_shard = lax.dynamic_slice_in_dim(g_small.pop("lb").reshape(DEPTH * 2, D_HGRN), chip * HEAD_DIM, HEAD_DIM, 1)
    g_small["hgrn_lb_logits"] = _lower_bounds_bwd(hgrn_lb_logits.reshape(DEPTH * 2, HEAD_DIM), lb_shard).reshape(
        hgrn_lb_logits.shape)
    g_small["conv_dw_w"] = lax.dynamic_slice_in_dim(g_small["conv_dw_w"], chip * HEAD_DIM, HEAD_DIM, 2)
    res = _adamw("adamw_small", _pack([w[n] for n in small_names]), _pack([g_small[n] for n in small_names]), None,
                 _pack([m[n] for n in small_names]), _pack([v[n] for n in small_names]))
    unpacked = [_unpack(r, [w[n].shape for n in small_names]) for r in res]
    for i, n in enumerate(small_names):
        out[n] = [unpacked[k][i] for k in range(4)]
    own, sib = _split_wait("sibling_wait", sib_start, res[0], with_srcs=True)
    for n, ga, gb in zip(SCATTER_ORDER, own, sib):
        big = _adamw("adamw_" + n, flat2(w[n]), ga, gb, flat2(m[n]), flat2(v[n]))
        out[n] = [r.reshape(w[n].shape) for r in big]

    return (loss, grad_x, *[out[n][0] for n in WEIGHTS], *[out[n][1] for n in WEIGHTS],
            *[out[n][2] for n in WEIGHTS], *[out[n][3] for n in WEIGHTS])
```

```python
import functools

import numpy as np
import jax
import jax.numpy as jnp
from jax import lax
from jax.experimental import pallas as pl
from jax.experimental.pallas import tpu as pltpu

F32, BF16 = jnp.float32, jnp.bfloat16

D_MODEL = 1024
DEPTH = 2
GRID_W = 64
D_ATTN, D_HGRN, D_CONV = 512, 256, 256
HEAD_DIM = 64
N_KV = 2
KV_LANES = D_ATTN // N_KV
ROPE_THETA = 10000.0
F_MIN = 1e-6
CONV_W = 31
CONV_PAD = 15
D_FF = 2816
D_IN = 2560
N_CHIP = 4
N_DEV = 8
IN_BLK = D_IN // N_CHIP
FF_BLK = D_FF // N_CHIP
OUT_BLK = D_MODEL // N_CHIP
EPS = 1e-6
LN_EPS = 1e-5
LR, B1, B2, ADAM_EPS, WD, STEP = 0.001, 0.9, 0.999, 1e-08, 0.01, 10
CHUNK = 16
HBLK = 256
CONV_TILE = 128
ATTN_FWD_ROWS = 512
ATTN_BWD_ROWS = 256
ATTN_BWD_STACK = 4
BWD_GROUP = 2
VMEM_LIMIT = 56 * 1024 * 1024

COL_Q, COL_K, COL_V = 0, 4, 5
COL_HQ, COL_FF, COL_FB, COL_HI, COL_HG, COL_CA, COL_CB = 3, 4, 5, 6, 7, 8, 9


def _params(sem=None):
    return pltpu.CompilerParams(dimension_semantics=sem, vmem_limit_bytes=VMEM_LIMIT)


def _sds(shape, dtype):
    return jax.ShapeDtypeStruct(tuple(shape), dtype)


def _full(shape):
    n = len(shape)
    return pl.BlockSpec(tuple(shape), lambda *_: (0,) * n)


def _sigmoid(x):
    return 0.5 * jnp.tanh(0.5 * x) + 0.5


def _gate_sigmoid(x):
    return 1.0 / (1.0 + jnp.exp(-x))


def _silu(x):
    return x * _sigmoid(x)


def _dsilu(x):
    s = _sigmoid(x)
    return s * (1.0 + x * (1.0 - s))


def _rowgroups(v):
    m, c = v.shape
    return v.reshape(m // 8, 8, c).sum(axis=0)


def _split2(x):
    hi = x.astype(BF16)
    lo = (x - hi.astype(F32)).astype(BF16)
    return hi, lo


def _rdot2(x, m):
    hi, lo = _split2(x)
    return (jnp.dot(hi, m, preferred_element_type=F32) + jnp.dot(lo, m, preferred_element_type=F32))


def _ldot3(m, x):
    hi = x.astype(BF16)
    r1 = x - hi.astype(F32)
    mid = r1.astype(BF16)
    lo = (r1 - mid.astype(F32)).astype(BF16)
    return (jnp.dot(m, hi, preferred_element_type=F32) + jnp.dot(m, mid, preferred_element_type=F32)
            + jnp.dot(m, lo, preferred_element_type=F32))


def _dot_nt(a, b):
    return lax.dot_general(a, b, (((1,), (1,)), ((), ())), preferred_element_type=F32)


def _dot_tn(a, b):
    return lax.dot_general(a, b, (((0,), (0,)), ((), ())), preferred_element_type=F32)


def _seg_matrix(n, seg, val):
    i = np.arange(n)
    return ((i[:, None] // seg) == (i[None, :] // seg)).astype(np.float32) * val


def _rot_matrix(n):
    r = np.zeros((n, n), np.float32)
    for i in range(n):
        if (i % 32) < 16:
            r[i + 16, i] = -1.0
        else:
            r[i - 16, i] = 1.0
    return r


def _rep_matrix():
    r = np.zeros((N_KV * HEAD_DIM, D_ATTN), np.float32)
    for kv in range(N_KV):
        for g in range(KV_LANES // HEAD_DIM):
            for d in range(HEAD_DIM):
                r[HEAD_DIM * kv + d, KV_LANES * kv + HEAD_DIM * g + d] = 1.0
    return r


def _cumsum_matrix(rev):
    i = np.arange(HBLK)
    same = (i[:, None] // CHUNK) == (i[None, :] // CHUNK)
    tri = (i[None, :] >= i[:, None]) if rev else (i[None, :] <= i[:, None])
    return (same & tri).astype(np.float32)


def _sel_matrices():
    sel = np.zeros((CHUNK, CHUNK * CHUNK), np.float32)
    selt = np.zeros((CHUNK, CHUNK * CHUNK), np.float32)
    for t in range(CHUNK):
        for s in range(CHUNK):
            sel[t, t * CHUNK + s] = 1.0
            selt[s, t * CHUNK + s] = 1.0
    return sel, selt


def _bf(a):
    return jnp.asarray(a, dtype=BF16)


def _mm(name, grid, pairs, extras, outs, epilogue, acc=None, sem=None):
    n_p, n_e, n_o = len(pairs), len(extras), len(outs)

    def body(*refs):
        ab = refs[:2 * n_p]
        ex = refs[2 * n_p:2 * n_p + n_e]
        out = refs[2 * n_p + n_e:2 * n_p + n_e + n_o]
        scr = refs[2 * n_p + n_e + n_o:]
        tot = None
        for i in range(n_p):
            a = ab[2 * i][...]
            b = ab[2 * i + 1][...]
            if a.ndim == 3:
                a = a.reshape(-1, a.shape[-1])
            if b.ndim == 3:
                b = b.reshape(-1, b.shape[-1])
            r = lax.dot_general(a.astype(BF16), b.astype(BF16), pairs[i][4], preferred_element_type=F32)
            tot = r if tot is None else tot + r

        def finish(total):
            res = epilogue(total, *[e[...] for e in ex])
            for o_ref, val in zip(out, res):
                o_ref[...] = val.astype(o_ref.dtype)

        if acc is None:
            finish(tot)
        else:
            k = pl.program_id(acc[0])

            @pl.when(k == 0)
            def _():
                scr[0][...] = tot

            @pl.when(k > 0)
            def _():
                scr[0][...] += tot

            @pl.when(k == grid[acc[0]] - 1)
            def _():
                finish(scr[0][...])

    args, in_specs = [], []
    for a, a_spec, b, b_spec, _ in pairs:
        args += [a, b]
        in_specs += [a_spec, b_spec]
    for e, e_spec in extras:
        args.append(e)
        in_specs.append(e_spec)
    if sem is None:
        sem = tuple("arbitrary" if (acc is not None and i == acc[0]) else "parallel" for i in range(len(grid)))
    return pl.pallas_call(
        body, name=name, grid=grid, in_specs=in_specs,
        out_specs=[o[1] for o in outs], out_shape=[o[0] for o in outs],
        scratch_shapes=[] if acc is None else [pltpu.VMEM(acc[1], F32)],
        compiler_params=_params(sem),
    )(*args)


NN = (((1,), (0,)), ((), ()))
NT = (((1,), (1,)), ((), ()))
TN = (((0,), (0,)), ((), ()))


def _row_tile(t):
    return min(256, t)


def _rms_fwd(name, x, w, deps=()):
    t, d = x.shape
    tm = _row_tile(t)

    def body(x_ref, w_ref, *rest):
        o_ref = rest[-1]
        xv = x_ref[...]
        r = lax.rsqrt(jnp.mean(xv * xv, axis=-1, keepdims=True) + EPS)
        o_ref[...] = (xv * r * w_ref[...]).astype(BF16)

    return pl.pallas_call(
        body, name=name, grid=(t // tm,),
        in_specs=[pl.BlockSpec((tm, d), lambda i: (i, 0)), _full((1, d))] + [_full(a.shape) for a in deps],
        out_specs=pl.BlockSpec((tm, d), lambda i: (i, 0)), out_shape=_sds((t, d), BF16),
        compiler_params=_params(("parallel",)),
    )(x, w, *deps)


def _rms_bwd(name, x, w, dh, dres, deps=()):
    t, d = x.shape
    tm = _row_tile(t)

    def body(x_ref, w_ref, dh_ref, dres_ref, *rest):
        dx_ref, dw_ref = rest[-2:]
        xv = x_ref[...]
        r = lax.rsqrt(jnp.mean(xv * xv, axis=-1, keepdims=True) + EPS)
        dy = dh_ref[...]
        gw = dy * w_ref[...]
        dx_ref[...] = dres_ref[...] + r * gw - xv * (r * r * r) * jnp.mean(gw * xv, axis=-1, keepdims=True)

        @pl.when(pl.program_id(0) == 0)
        def _():
            dw_ref[...] = jnp.zeros_like(dw_ref)

        dw_ref[...] += _rowgroups(dy * xv * r)

    tile = pl.BlockSpec((tm, d), lambda i: (i, 0))
    return pl.pallas_call(
        body, name=name, grid=(t // tm,),
        in_specs=[tile, _full((1, d)), tile, tile] + [_full(a.shape) for a in deps],
        out_specs=[tile, _full((8, d))], out_shape=[_sds((t, d), F32), _sds((8, d), F32)],
        compiler_params=_params(("arbitrary",)),
    )(x, w, dh, dres, *deps)


def _loss_kernel(y, target):
    t, d = y.shape
    tm = _row_tile(t)

    def body(y_ref, t_ref, dy_ref, acc_ref):
        e = y_ref[...] - t_ref[...]
        dy_ref[...] = e * (1.0 / d)

        @pl.when(pl.program_id(0) == 0)
        def _():
            acc_ref[...] = jnp.zeros_like(acc_ref)

        acc_ref[...] += _rowgroups(e * e)

    tile = pl.BlockSpec((tm, d), lambda i: (i, 0))
    return pl.pallas_call(
        body, name="loss_head", grid=(t // tm,), in_specs=[tile, tile],
        out_specs=[tile, _full((8, d))], out_shape=[_sds((t, d), F32), _sds((8, d), F32)],
        compiler_params=_params(("arbitrary",)),
    )(y, target)


def _rope_tables(s):
    rows = s // GRID_W
    row_id = jnp.repeat(jnp.arange(rows, dtype=F32), GRID_W)
    col_id = jnp.tile(jnp.arange(GRID_W, dtype=F32), rows)
    half = HEAD_DIM // 2
    inv_freq = ROPE_THETA ** (-jnp.arange(0, half, 2, dtype=F32) / half)
    ang_r = row_id[:, None] * inv_freq[None, :]
    ang_c = col_id[:, None] * inv_freq[None, :]
    ang = jnp.concatenate([ang_r, ang_r, ang_c, ang_c], axis=-1)
    return jnp.cos(ang).astype(F32), jnp.sin(ang).astype(F32)


def _attn_consts():
    return dict(
        seg_q=_bf(_seg_matrix(D_ATTN, HEAD_DIM, 1.0 / HEAD_DIM)),
        seg_k=_bf(_seg_matrix(N_KV * HEAD_DIM, HEAD_DIM, 1.0 / HEAD_DIM)),
        rot_q=_bf(_rot_matrix(D_ATTN)), rot_k=_bf(_rot_matrix(N_KV * HEAD_DIM)),
        rep=_bf(_rep_matrix()), rep_t=_bf(_rep_matrix().T))


def _attn_prep(name, proj, s, tabs, qw, kw, ac):
    t = proj.shape[0]
    tm = _row_tile(s)
    nst = s // tm
    kw_ = N_KV * HEAD_DIM

    def body(q_ref, k_ref, v_ref, cq_ref, sq_ref, ck_ref, sk_ref, qw_ref, kw_ref,
             segq_ref, segk_ref, rotq_ref, rotk_ref, rep_ref, qn_ref, kr_ref, vr_ref):
        q = q_ref[...]
        r = lax.rsqrt(jnp.dot((q * q).astype(BF16), segq_ref[...], preferred_element_type=F32) + EPS)
        qn = q * r * qw_ref[...]
        qr = qn * cq_ref[...] + _rdot2(qn, rotq_ref[...]) * sq_ref[...]
        qn_ref[...] = (qr * (HEAD_DIM ** -0.5)).astype(BF16)
        k = k_ref[...]
        rk = lax.rsqrt(jnp.dot((k * k).astype(BF16), segk_ref[...], preferred_element_type=F32) + EPS)
        kn = k * rk * kw_ref[...]
        kr = kn * ck_ref[...] + _rdot2(kn, rotk_ref[...]) * sk_ref[...]
        kr_ref[...] = jnp.dot(kr.astype(BF16), rep_ref[...], preferred_element_type=F32).astype(BF16)
        vr_ref[...] = jnp.dot(v_ref[...].astype(BF16), rep_ref[...], preferred_element_type=F32).astype(BF16)

    wide = pl.BlockSpec((tm, D_ATTN), lambda i: (i, 0))
    tabq = pl.BlockSpec((tm, D_ATTN), lambda i: (i % nst, 0))
    tabk = pl.BlockSpec((tm, kw_), lambda i: (i % nst, 0))
    return pl.pallas_call(
        body, name=name, grid=(t // tm,),
        in_specs=[pl.BlockSpec((tm, D_ATTN), lambda i: (i, COL_Q)), pl.BlockSpec((tm, kw_), lambda i: (i, COL_K)),
                  pl.BlockSpec((tm, kw_), lambda i: (i, COL_V)), tabq, tabq, tabk, tabk,
                  _full((1, D_ATTN)), _full((1, kw_)), _full((D_ATTN, D_ATTN)), _full((kw_, kw_)),
                  _full((D_ATTN, D_ATTN)), _full((kw_, kw_)), _full((kw_, D_ATTN))],
        out_specs=[wide, wide, wide], out_shape=[_sds((t, D_ATTN), BF16)] * 3,
        compiler_params=_params(("parallel",)),
    )(proj, proj, proj, tabs["cq"], tabs["sq"], tabs["ck"], tabs["sk"], qw, kw,
      ac["seg_q"], ac["seg_k"], ac["rot_q"], ac["rot_k"], ac["rep"])


def _attn_prep_bwd(name, proj, s, tabs, qw, kw, ac, dqs, dkr, dvr):
    t = proj.shape[0]
    tm = _row_tile(s)
    nst = s // tm
    kw_ = N_KV * HEAD_DIM
    wout = D_ATTN + 2 * kw_

    def norm_rope_bwd(x, w, cos, sin, seg, rot, d_roped):
        dn = d_roped * cos - _rdot2(d_roped * sin, rot)
        r = lax.rsqrt(jnp.dot((x * x).astype(BF16), seg, preferred_element_type=F32) + EPS)
        gw = dn * w
        dx = r * gw - x * (r * r * r) * _rdot2(gw * x, seg)
        return dx, _rowgroups(dn * x * r)

    def body(q_ref, k_ref, cq_ref, sq_ref, ck_ref, sk_ref, qw_ref, kw_ref, segq_ref, segk_ref, rotq_ref, rotk_ref,
             rept_ref, dqs_ref, dkr_ref, dvr_ref, dp_ref, dqw_ref, dkw_ref):
        dq, dqw = norm_rope_bwd(q_ref[...], qw_ref[...], cq_ref[...], sq_ref[...], segq_ref[...], rotq_ref[...],
                                dqs_ref[...] * (HEAD_DIM ** -0.5))
        dk_roped = jnp.dot(dkr_ref[...].astype(BF16), rept_ref[...], preferred_element_type=F32)
        dk, dkw = norm_rope_bwd(k_ref[...], kw_ref[...], ck_ref[...], sk_ref[...], segk_ref[...], rotk_ref[...],
                                dk_roped)
        dv = jnp.dot(dvr_ref[...].astype(BF16), rept_ref[...], preferred_element_type=F32)
        dp_ref[:, 0:D_ATTN] = dq.astype(BF16)
        dp_ref[:, D_ATTN:D_ATTN + kw_] = dk.astype(BF16)
        dp_ref[:, D_ATTN + kw_:wout] = dv.astype(BF16)

        @pl.when(pl.program_id(0) == 0)
        def _():
            dqw_ref[...] = jnp.zeros_like(dqw_ref)
            dkw_ref[...] = jnp.zeros_like(dkw_ref)

        dqw_ref[...] += dqw
        dkw_ref[...] += dkw

    wide = pl.BlockSpec((tm, D_ATTN), lambda i: (i, 0))
    tabq = pl.BlockSpec((tm, D_ATTN), lambda i: (i % nst, 0))
    tabk = pl.BlockSpec((tm, kw_), lambda i: (i % nst, 0))
    return pl.pallas_call(
        body, name=name, grid=(t // tm,),
        in_specs=[pl.BlockSpec((tm, D_ATTN), lambda i: (i, COL_Q)), pl.BlockSpec((tm, kw_), lambda i: (i, COL_K)),
                  tabq, tabq, tabk, tabk, _full((1, D_ATTN)), _full((1, kw_)),
                  _full((D_ATTN, D_ATTN)), _full((kw_, kw_)), _full((D_ATTN, D_ATTN)), _full((kw_, kw_)),
                  _full((D_ATTN, kw_)), wide, wide, wide],
        out_specs=[pl.BlockSpec((tm, wout), lambda i: (i, 0)), _full((8, D_ATTN)), _full((8, kw_))],
        out_shape=[_sds((t, wout), BF16), _sds((8, D_ATTN), F32), _sds((8, kw_), F32)],
        compiler_params=_params(("arbitrary",)),
    )(proj, proj, tabs["cq"], tabs["sq"], tabs["ck"], tabs["sk"], qw, kw,
      ac["seg_q"], ac["seg_k"], ac["rot_q"], ac["rot_k"], ac["rep_t"], dqs, dkr, dvr)


def _attn_tile(s, rows=256):
    return min(rows, s)


def _head_masks(shape):
    lane = lax.broadcasted_iota(jnp.int32, shape, 1)
    return [(lane // HEAD_DIM) == g for g in range(KV_LANES // HEAD_DIM)]


def _attn_fwd(name, qn, kr, vr, b, s):
    t = qn.shape[0]
    tq = _attn_tile(s, ATTN_FWD_ROWS)
    nq = s // tq

    def body(q_ref, k_ref, v_ref, o_ref):
        q = q_ref[...]
        k = k_ref[...]
        v = v_ref[...]
        acc = jnp.zeros((tq, KV_LANES), F32)
        for mask in _head_masks((tq, KV_LANES)):
            sc = _dot_nt(jnp.where(mask, q, jnp.zeros_like(q)), k)
            p = jnp.exp(sc - jnp.max(sc, axis=-1, keepdims=True))
            inv = 1.0 / jnp.sum(p, axis=-1, keepdims=True)
            og = jnp.dot(p.astype(BF16), v, preferred_element_type=F32) * inv
            acc = jnp.where(mask, og, acc)
        o_ref[...] = acc

    return pl.pallas_call(
        body, name=name, grid=(b, N_KV, nq),
        in_specs=[pl.BlockSpec((tq, KV_LANES), lambda bi, kv, i: (bi * nq + i, kv)),
                  pl.BlockSpec((s, KV_LANES), lambda bi, kv, i: (bi, kv)),
                  pl.BlockSpec((s, KV_LANES), lambda bi, kv, i: (bi, kv))],
        out_specs=pl.BlockSpec((tq, KV_LANES), lambda bi, kv, i: (bi * nq + i, kv)),
        out_shape=_sds((t, D_ATTN), F32),
        compiler_params=_params(("parallel", "parallel", "parallel")),
    )(qn, kr, vr)


def _attn_bwd(name, qn, kr, vr, do, b, s):
    t = qn.shape[0]
    tq = _attn_tile(s, ATTN_BWD_ROWS)
    nq = s // tq

    def body(q_ref, k_ref, v_ref, do_ref, dq_ref, dk_ref, dv_ref):
        @pl.when(pl.program_id(2) == 0)
        def _():
            dk_ref[...] = jnp.zeros_like(dk_ref)
            dv_ref[...] = jnp.zeros_like(dv_ref)

        q = q_ref[...]
        k = k_ref[...]
        v = v_ref[...]
        dout = do_ref[...].astype(BF16)
        all_masks = _head_masks((tq, KV_LANES))
        dq = jnp.zeros((tq, KV_LANES), F32)
        for g0 in range(0, len(all_masks), ATTN_BWD_STACK):
            masks = all_masks[g0:g0 + ATTN_BWD_STACK]
            q4 = jnp.concatenate([jnp.where(m, q, jnp.zeros_like(q)) for m in masks], axis=0)
            do4 = jnp.concatenate([jnp.where(m, dout, jnp.zeros_like(dout)) for m in masks], axis=0)
            sc = _dot_nt(q4, k)
            p = jnp.exp(sc - jnp.max(sc, axis=-1, keepdims=True))
            p = p * (1.0 / jnp.sum(p, axis=-1, keepdims=True))
            dp = _dot_nt(do4, v)
            ds = (p * (dp - jnp.sum(p * dp, axis=-1, keepdims=True))).astype(BF16)
            dq4 = jnp.dot(ds, k, preferred_element_type=F32)
            for g, m in enumerate(masks):
                dq = jnp.where(m, dq4[g * tq:(g + 1) * tq, :], dq)
            dk_ref[...] += _dot_tn(ds, q4)
            dv_ref[...] += _dot_tn(p.astype(BF16), do4)
        dq_ref[...] = dq

    qspec = pl.BlockSpec((tq, KV_LANES), lambda bi, kv, i: (bi * nq + i, kv))
    kspec = pl.BlockSpec((s, KV_LANES), lambda bi, kv, i: (bi, kv))
    return pl.pallas_call(
        body, name=name, grid=(b, N_KV, nq),
        in_specs=[qspec, kspec, kspec, qspec],
        out_specs=[qspec, kspec, kspec], out_shape=[_sds((t, D_ATTN), F32)] * 3,
        compiler_params=_params(("parallel", "parallel", "arbitrary")),
    )(qn, kr, vr, do)


def _hgrn_consts(rev):
    sel, selt = _sel_matrices()
    cs = _cumsum_matrix(rev)
    return dict(cs=_bf(cs), cs_t=_bf(cs.T), seg=_bf(_seg_matrix(D_HGRN, HEAD_DIM, 1.0)),
                bd=jnp.asarray(_seg_matrix(D_HGRN, HEAD_DIM, 1.0), F32),
                sel=_bf(sel), selt=_bf(selt), seld=_bf(sel - selt))


def _gates(z, lb):
    sig = _gate_sigmoid(z)
    f = lb + (1.0 - lb) * sig
    g = jnp.log(jnp.maximum(f, F_MIN))
    sn = _gate_sigmoid(-z)
    return sig, f, g, sn, (1.0 - lb) * sn


def _pair_decay(b, rev):
    row = lax.broadcasted_iota(jnp.int32, (CHUNK, D_HGRN), 0)
    parts = []
    for t in range(CHUNK):
        m = (row >= t) if rev else (row <= t)
        parts.append(jnp.where(m, jnp.exp(jnp.minimum(b[t:t + 1, :] - b, 0.0)), 0.0))
    return jnp.concatenate(parts, axis=0)


def _rows_rep(a):
    return jnp.concatenate([jnp.broadcast_to(a[t:t + 1, :], a.shape) for t in range(CHUNK)], axis=0)


def _tile_rows(a):
    return jnp.concatenate([a] * CHUNK, axis=0)


def _hgrn_specs(b, s, rev):
    nb = s // HBLK

    def blk(j):
        return (nb - 1 - j) if rev else j

    def col(c):
        return pl.BlockSpec((HBLK, D_HGRN), lambda bi, j: (bi * nb + blk(j), c))

    return nb, blk, col


def _hgrn_fwd(name, proj, lb, b, s, rev, hc):
    t = proj.shape[0]
    nb, blk, col = _hgrn_specs(b, s, rev)
    n_ch = HBLK // CHUNK
    last = 0 if rev else CHUNK - 1

    def body(q_ref, z_ref, v_ref, lb_ref, cs_ref, seg_ref, bd_ref, sel_ref, o_ref, st_ref, state, b_scr, k_scr):
        @pl.when(pl.program_id(1) == 0)
        def _():
            state[...] = jnp.zeros_like(state)

        st_ref[...] = state[...]
        _, _, g, _, kk = _gates(z_ref[...], lb_ref[...])
        k_scr[...] = kk
        b_scr[...] = _ldot3(cs_ref[...], g)

        def chunk(i, carry):
            c = (n_ch - 1 - i) if rev else i
            rows = pl.ds(pl.multiple_of(c * CHUNK, CHUNK), CHUNK)
            q = q_ref[rows, :]
            k = k_scr[rows, :]
            v = v_ref[rows, :]
            bb = b_scr[rows, :]
            bl = bb[last:last + 1, :]
            pairs = _pair_decay(bb, rev) * _rows_rep(q) * _tile_rows(k)
            a = jnp.dot(pairs.astype(BF16), seg_ref[...], preferred_element_type=F32)
            o_intra = jnp.dot(sel_ref[...], (a * _tile_rows(v)).astype(BF16), preferred_element_type=F32)
            st = state[...]
            o_inter = _dot_nt((q * jnp.exp(bb)).astype(BF16), st.astype(BF16))
            o_ref[rows, :] = o_intra + o_inter
            ke = k * jnp.exp(bl - bb)
            state[...] = st * jnp.exp(bl) + bd_ref[...] * _dot_tn(v.astype(BF16), ke.astype(BF16))
            return carry

        lax.fori_loop(0, n_ch, chunk, 0)

    sq = (D_HGRN, D_HGRN)
    return pl.pallas_call(
        body, name=name, grid=(b, nb),
        in_specs=[col(COL_HQ), col(COL_FB if rev else COL_FF), col(COL_HI), _full((1, D_HGRN)),
                  _full((HBLK, HBLK)), _full(sq), _full(sq), _full((CHUNK, CHUNK * CHUNK))],
        out_specs=[pl.BlockSpec((HBLK, D_HGRN), lambda bi, j: (bi * nb + blk(j), 0)),
                   pl.BlockSpec((None,) + sq, lambda bi, j: (bi * nb + blk(j), 0, 0))],
        out_shape=[_sds((t, D_HGRN), F32), _sds((b * nb,) + sq, F32)],
        scratch_shapes=[pltpu.VMEM(sq, F32), pltpu.VMEM((HBLK, D_HGRN), F32), pltpu.VMEM((HBLK, D_HGRN), F32)],
        compiler_params=_params(("parallel", "arbitrary")),
    )(proj, proj, proj, lb, hc["cs"], hc["seg"], hc["bd"], hc["sel"])


def _hgrn_bwd(name, proj, lb, st_blk, do, dq_prev, dv_prev, b, s, rev, hc):
    t = proj.shape[0]
    nb = s // HBLK
    n_ch = HBLK // CHUNK
    last = 0 if rev else CHUNK - 1

    def blk(j):
        return j if rev else (nb - 1 - j)

    def col(c):
        return pl.BlockSpec((HBLK, D_HGRN), lambda bi, j: (bi * nb + blk(j), c))

    def body(q_ref, z_ref, v_ref, lb_ref, st_ref, do_ref, dqp_ref, dvp_ref, cs_ref, cst_ref, seg_ref, bd_ref,
             sel_ref, selt_ref, seld_ref, dq_ref, dv_ref, dz_ref, dlb_ref,
             dstate, states, b_scr, k_scr, db_scr, dk_scr):
        first = jnp.logical_and(pl.program_id(0) == 0, pl.program_id(1) == 0)

        @pl.when(first)
        def _():
            dlb_ref[...] = jnp.zeros_like(dlb_ref)

        @pl.when(pl.program_id(1) == 0)
        def _():
            dstate[...] = jnp.zeros_like(dstate)

        lbv = lb_ref[...]
        z = z_ref[...]
        sig, f, g, sn, kk = _gates(z, lbv)
        k_scr[...] = kk
        b_scr[...] = _ldot3(cs_ref[...], g)

        def rows_of(c):
            return pl.ds(pl.multiple_of(c * CHUNK, CHUNK), CHUNK)

        def replay(i, st):
            c = (n_ch - 1 - i) if rev else i
            rows = rows_of(c)
            states[c] = st
            bb = b_scr[rows, :]
            bl = bb[last:last + 1, :]
            ke = k_scr[rows, :] * jnp.exp(bl - bb)
            return st * jnp.exp(bl) + bd_ref[...] * _dot_tn(v_ref[rows, :].astype(BF16), ke.astype(BF16))

        lax.fori_loop(0, n_ch, replay, st_ref[...])
        row = lax.broadcasted_iota(jnp.int32, (CHUNK, D_HGRN), 0)

        def chunk(i, carry):
            c = i if rev else (n_ch - 1 - i)
            rows = rows_of(c)
            q = q_ref[rows, :]
            k = k_scr[rows, :]
            v = v_ref[rows, :]
            bb = b_scr[rows, :]
            dout = do_ref[rows, :]
            bl = bb[last:last + 1, :]
            st_p = states[c]
            dst_n = dstate[...]
            eb = jnp.exp(bb)
            ebl = jnp.exp(bl - bb)
            ebl_last = jnp.exp(bl)
            qe = q * eb
            ke = k * ebl
            dob = dout.astype(BF16)
            dstb = dst_n.astype(BF16)
            dqe = jnp.dot(dob, st_p.astype(BF16), preferred_element_type=F32)
            dke = jnp.dot(v.astype(BF16), dstb, preferred_element_type=F32)
            dv = _dot_nt(ke.astype(BF16), dstb)
            dbl = jnp.sum(dst_n * st_p, axis=0, keepdims=True) * ebl_last + jnp.sum(dke * ke, axis=0, keepdims=True)
            dq = dqe * eb
            dk = dke * ebl
            db = dqe * qe - dke * ke
            dec = _pair_decay(bb, rev)
            q_rep = _rows_rep(q)
            k_til = _tile_rows(k)
            do_rep = _rows_rep(dout)
            pairs = dec * q_rep * k_til
            a = jnp.dot(pairs.astype(BF16), seg_ref[...], preferred_element_type=F32)
            wb = jnp.dot((_tile_rows(v) * do_rep).astype(BF16), seg_ref[...], preferred_element_type=F32)
            gdec = wb * dec
            dq = dq + jnp.dot(sel_ref[...], (gdec * k_til).astype(BF16), preferred_element_type=F32)
            dk = dk + jnp.dot(selt_ref[...], (gdec * q_rep).astype(BF16), preferred_element_type=F32)
            dv = dv + jnp.dot(selt_ref[...], (a * do_rep).astype(BF16), preferred_element_type=F32)
            db = db + jnp.dot(seld_ref[...], (wb * pairs).astype(BF16), preferred_element_type=F32)
            db = db + jnp.where(row == last, dbl, 0.0)
            dq_ref[rows, :] = dq + dqp_ref[rows, :]
            dv_ref[rows, :] = dv + dvp_ref[rows, :]
            dk_scr[rows, :] = dk
            db_scr[rows, :] = db
            dstate[...] = dst_n * ebl_last + bd_ref[...] * _dot_tn(dob, qe.astype(BF16))
            return carry

        lax.fori_loop(0, n_ch, chunk, 0)
        hi, lo = _split2(db_scr[...])
        dg = (jnp.dot(cst_ref[...], hi, preferred_element_type=F32)
              + jnp.dot(cst_ref[...], lo, preferred_element_type=F32))
        dgf = jnp.where(f > F_MIN, dg / f, 0.0)
        dk = dk_scr[...]
        dz_ref[...] = dgf * (1.0 - lbv) * sig * (1.0 - sig) - dk * (1.0 - lbv) * sn * (1.0 - sn)
        dlb_ref[...] += _rowgroups(dgf * (1.0 - sig) - dk * sn)

    sq = (D_HGRN, D_HGRN)
    blk0 = pl.BlockSpec((HBLK, D_HGRN), lambda bi, j: (bi * nb + blk(j), 0))
    pairs_shape = (CHUNK, CHUNK * CHUNK)
    return pl.pallas_call(
        body, name=name, grid=(b, nb),
        in_specs=[col(COL_HQ), col(COL_FB if rev else COL_FF), col(COL_HI), _full((1, D_HGRN)),
                  pl.BlockSpec((None,) + sq, lambda bi, j: (bi * nb + blk(j), 0, 0)), blk0, blk0, blk0,
                  _full((HBLK, HBLK)), _full((HBLK, HBLK)), _full(sq), _full(sq),
                  _full(pairs_shape), _full(pairs_shape), _full(pairs_shape)],
        out_specs=[blk0, blk0, blk0, _full((8, D_HGRN))],
        out_shape=[_sds((t, D_HGRN), F32)] * 3 + [_sds((8, D_HGRN), F32)],
        scratch_shapes=[pltpu.VMEM(sq, F32), pltpu.VMEM((n_ch,) + sq, F32)] + [pltpu.VMEM((HBLK, D_HGRN), F32)] * 4,
        compiler_params=_params(("arbitrary", "arbitrary")),
    )(proj, proj, proj, lb, st_blk, do, dq_prev, dv_prev,
      hc["cs"], hc["cs_t"], hc["seg"], hc["bd"], hc["sel"], hc["selt"], hc["seld"])


def _scan_chunk_fwd(c, rev, q_ref, v_ref, k_scr, b_scr, state, o_ref, seg_ref, bd_ref, sel_ref):
    last = 0 if rev else CHUNK - 1
    rows = pl.ds(pl.multiple_of(c * CHUNK, CHUNK), CHUNK)
    q = q_ref[rows, :]
    k = k_scr[rows, :]
    v = v_ref[rows, :]
    bb = b_scr[rows, :]
    bl = bb[last:last + 1, :]
    pairs = _pair_decay(bb, rev) * _rows_rep(q) * _tile_rows(k)
    a = jnp.dot(pairs.astype(BF16), seg_ref[...], preferred_element_type=F32)
    o_intra = jnp.dot(sel_ref[...], (a * _tile_rows(v)).astype(BF16), preferred_element_type=F32)
    st = state[...]
    o_inter = _dot_nt((q * jnp.exp(bb)).astype(BF16), st.astype(BF16))
    o_ref[rows, :] = o_intra + o_inter
    ke = k * jnp.exp(bl - bb)
    state[...] = st * jnp.exp(bl) + bd_ref[...] * _dot_tn(v.astype(BF16), ke.astype(BF16))


def _scan_chunks_fwd(chains, seg_ref, bd_ref, sel_ref):
    work = []
    for c, rev, q_ref, v_ref, k_scr, b_scr, state, o_ref in chains:
        last = 0 if rev else CHUNK - 1
        rows = pl.ds(pl.multiple_of(c * CHUNK, CHUNK), CHUNK)
        q = q_ref[rows, :]
        k = k_scr[rows, :]
        v = v_ref[rows, :]
        bb = b_scr[rows, :]
        bl = bb[last:last + 1, :]
        st = state[...]
        work.append(dict(
            rows=rows, v=v, st=st, state=state, o_ref=o_ref, decay=jnp.exp(bl),
            pairs=(_pair_decay(bb, rev) * _rows_rep(q) * _tile_rows(k)).astype(BF16),
            qe=(q * jnp.exp(bb)).astype(BF16), ke=(k * jnp.exp(bl - bb)).astype(BF16), st_b=st.astype(BF16)))
    for w in work:
        w["a"] = jnp.dot(w["pairs"], seg_ref[...], preferred_element_type=F32)
        w["o_inter"] = _dot_nt(w["qe"], w["st_b"])
        w["upd"] = _dot_tn(w["v"].astype(BF16), w["ke"])
    for w in work:
        w["av"] = (w["a"] * _tile_rows(w["v"])).astype(BF16)
    for w in work:
        w["o_ref"][w["rows"], :] = jnp.dot(sel_ref[...], w["av"], preferred_element_type=F32) + w["o_inter"]
        w["state"][...] = w["st"] * w["decay"] + bd_ref[...] * w["upd"]


def _hgrn_fwd2(name, proj, lb_f, lb_b, b, s, hc_f, hc_b):
    t = proj.shape[0]
    nb = s // HBLK
    n_ch = HBLK // CHUNK
    n_chain = 2 * b

    def body(qf_ref, zf_ref, vf_ref, qb_ref, zb_ref, vb_ref, lbf_ref, lbb_ref, csf_ref, csb_ref, seg_ref, bd_ref,
             sel_ref, of_ref, ob_ref, stf_ref, stb_ref, *scr):
        state, b_scr, k_scr = scr[:n_chain], scr[n_chain:2 * n_chain], scr[2 * n_chain:]

        @pl.when(pl.program_id(0) == 0)
        def _():
            for st0 in state:
                st0[...] = jnp.zeros_like(st0)

        chains = []
        for bi in range(b):
            chains.append((False, qf_ref.at[bi], zf_ref.at[bi], vf_ref.at[bi], lbf_ref, csf_ref, of_ref.at[bi],
                           stf_ref.at[bi], 2 * bi))
            chains.append((True, qb_ref.at[bi], zb_ref.at[bi], vb_ref.at[bi], lbb_ref, csb_ref, ob_ref.at[bi],
                           stb_ref.at[bi], 2 * bi + 1))
        for rev, q, z, v, lb, cs, o, st, ci in chains:
            st[...] = state[ci][...]
            _, _, g, _, kk = _gates(z[...], lb[...])
            k_scr[ci][...] = kk
            b_scr[ci][...] = _ldot3(cs[...], g)

        def chunk(i, carry):
            _scan_chunks_fwd([((n_ch - 1 - i) if rev else i, rev, q, v, k_scr[ci], b_scr[ci], state[ci], o)
                              for rev, q, z, v, lb, cs, o, st, ci in chains], seg_ref, bd_ref, sel_ref)
            return carry

        lax.fori_loop(0, n_ch, chunk, 0)

    def col(c, rev):
        return pl.BlockSpec((b, HBLK, D_HGRN), lambda j: (0, (nb - 1 - j) if rev else j, c))

    def st_spec(rev):
        return pl.BlockSpec((b, None, D_HGRN, D_HGRN), lambda j: (0, (nb - 1 - j) if rev else j, 0, 0))

    sq = (D_HGRN, D_HGRN)
    proj3 = proj.reshape(b, s, proj.shape[1])
    o_fw, o_bw, st_fw, st_bw = pl.pallas_call(
        body, name=name, grid=(nb,),
        in_specs=[col(COL_HQ, False), col(COL_FF, False), col(COL_HI, False),
                  col(COL_HQ, True), col(COL_FB, True), col(COL_HI, True),
                  _full((1, D_HGRN)), _full((1, D_HGRN)), _full((HBLK, HBLK)), _full((HBLK, HBLK)),
                  _full(sq), _full(sq), _full((CHUNK, CHUNK * CHUNK))],
        out_specs=[col(0, False), col(0, True), st_spec(False), st_spec(True)],
        out_shape=[_sds((b, s, D_HGRN), F32)] * 2 + [_sds((b, nb) + sq, F32)] * 2,
        scratch_shapes=[pltpu.VMEM(sq, F32)] * n_chain + [pltpu.VMEM((HBLK, D_HGRN), F32)] * (2 * n_chain),
        compiler_params=_params(("arbitrary",)),
    )(proj3, proj3, proj3, proj3, proj3, proj3, lb_f, lb_b, hc_f["cs"], hc_b["cs"], hc_f["seg"], hc_f["bd"],
      hc_f["sel"])
    return o_fw.reshape(t, D_HGRN), o_bw.reshape(t, D_HGRN), st_fw, st_bw


def _scan_replay(c, rev, st, v_ref, k_scr, b_scr, states, bd_ref):
    last = 0 if rev else CHUNK - 1
    rows = pl.ds(pl.multiple_of(c * CHUNK, CHUNK), CHUNK)
    states[c] = st
    bb = b_scr[rows, :]
    bl = bb[last:last + 1, :]
    ke = k_scr[rows, :] * jnp.exp(bl - bb)
    return st * jnp.exp(bl) + bd_ref[...] * _dot_tn(v_ref[rows, :].astype(BF16), ke.astype(BF16))


def _scan_replays(chains, bd_ref):
    work = []
    for c, rev, st, v_ref, k_scr, b_scr, states in chains:
        last = 0 if rev else CHUNK - 1
        rows = pl.ds(pl.multiple_of(c * CHUNK, CHUNK), CHUNK)
        states[c] = st
        bb = b_scr[rows, :]
        bl = bb[last:last + 1, :]
        work.append((st, jnp.exp(bl), v_ref[rows, :].astype(BF16), (k_scr[rows, :] * jnp.exp(bl - bb)).astype(BF16)))
    upds = [_dot_tn(v, ke) for _, _, v, ke in work]
    return tuple(st * decay + bd_ref[...] * upd for (st, decay, _, _), upd in zip(work, upds))


def _scan_chunks_bwd(chains, seg_ref, bd_ref, sel_ref, selt_ref, seld_ref):
    row = lax.broadcasted_iota(jnp.int32, (CHUNK, D_HGRN), 0)
    work = []
    for c, rev, q_ref, v_ref, do_ref, k_scr, b_scr, states, dstate, dq_ref, dv_ref, dk_scr, db_scr in chains:
        last = 0 if rev else CHUNK - 1
        rows = pl.ds(pl.multiple_of(c * CHUNK, CHUNK), CHUNK)
        q = q_ref[rows, :]
        k = k_scr[rows, :]
        v = v_ref[rows, :]
        bb = b_scr[rows, :]
        dout = do_ref[rows, :]
        bl = bb[last:last + 1, :]
        st_p = states[c]
        dst_n = dstate[...]
        eb = jnp.exp(bb)
        ebl = jnp.exp(bl - bb)
        qe = q * eb
        ke = k * ebl
        dec = _pair_decay(bb, rev)
        q_rep = _rows_rep(q)
        k_til = _tile_rows(k)
        do_rep = _rows_rep(dout)
        pairs = dec * q_rep * k_til
        work.append(dict(
            rows=rows, last=last, eb=eb, ebl=ebl, ebl_last=jnp.exp(bl), qe=qe, ke=ke, dec=dec, q_rep=q_rep, k_til=k_til,
            do_rep=do_rep, pairs=pairs, st_p=st_p, dst_n=dst_n, dstate=dstate, dq_ref=dq_ref, dv_ref=dv_ref,
            dk_scr=dk_scr, db_scr=db_scr, dob=dout.astype(BF16), dstb=dst_n.astype(BF16), vb=v.astype(BF16),
            pairs_b=pairs.astype(BF16), vdo_b=(_tile_rows(v) * do_rep).astype(BF16)))
    for w in work:
        w["dqe"] = jnp.dot(w["dob"], w["st_p"].astype(BF16), preferred_element_type=F32)
        w["dke"] = jnp.dot(w["vb"], w["dstb"], preferred_element_type=F32)
        w["dv"] = _dot_nt(w["ke"].astype(BF16), w["dstb"])
        w["a"] = jnp.dot(w["pairs_b"], seg_ref[...], preferred_element_type=F32)
        w["wb"] = jnp.dot(w["vdo_b"], seg_ref[...], preferred_element_type=F32)
        w["dst_upd"] = _dot_tn(w["dob"], w["qe"].astype(BF16))
    for w in work:
        gdec = w["wb"] * w["dec"]
        w["x_dq"] = (gdec * w["k_til"]).astype(BF16)
        w["x_dk"] = (gdec * w["q_rep"]).astype(BF16)
        w["x_dv"] = (w["a"] * w["do_rep"]).astype(BF16)
        w["x_db"] = (w["wb"] * w["pairs"]).astype(BF16)
    for w in work:
        dke, dqe = w["dke"], w["dqe"]
        dbl = (jnp.sum(w["dst_n"] * w["st_p"], axis=0, keepdims=True) * w["ebl_last"]
               + jnp.sum(dke * w["ke"], axis=0, keepdims=True))
        dq = dqe * w["eb"] + jnp.dot(sel_ref[...], w["x_dq"], preferred_element_type=F32)
        dk = dke * w["ebl"] + jnp.dot(selt_ref[...], w["x_dk"], preferred_element_type=F32)
        dv = w["dv"] + jnp.dot(selt_ref[...], w["x_dv"], preferred_element_type=F32)
        db = (dqe * w["qe"] - dke * w["ke"] + jnp.dot(seld_ref[...], w["x_db"], preferred_element_type=F32)
              + jnp.where(row == w["last"], dbl, 0.0))
        w["dq_ref"][w["rows"], :] = dq
        w["dv_ref"][w["rows"], :] = dv
        w["dk_scr"][w["rows"], :] = dk
        w["db_scr"][w["rows"], :] = db
        w["dstate"][...] = w["dst_n"] * w["ebl_last"] + bd_ref[...] * w["dst_upd"]


def _scan_chunk_bwd(c, rev, q_ref, v_ref, do_ref, k_scr, b_scr, states, dstate, dq_ref, dv_ref, dk_scr, db_scr,
                    seg_ref, bd_ref, sel_ref, selt_ref, seld_ref):
    last = 0 if rev else CHUNK - 1
    row = lax.broadcasted_iota(jnp.int32, (CHUNK, D_HGRN), 0)
    rows = pl.ds(pl.multiple_of(c * CHUNK, CHUNK), CHUNK)
    q = q_ref[rows, :]
    k = k_scr[rows, :]
    v = v_ref[rows, :]
    bb = b_scr[rows, :]
    dout = do_ref[rows, :]
    bl = bb[last:last + 1, :]
    st_p = states[c]
    dst_n = dstate[...]
    eb = jnp.exp(bb)
    ebl = jnp.exp(bl - bb)
    ebl_last = jnp.exp(bl)
    qe = q * eb
    ke = k * ebl
    dob = dout.astype(BF16)
    dstb = dst_n.astype(BF16)
    dqe = jnp.dot(dob, st_p.astype(BF16), preferred_element_type=F32)
    dke = jnp.dot(v.astype(BF16), dstb, preferred_element_type=F32)
    dv = _dot_nt(ke.astype(BF16), dstb)
    dbl = jnp.sum(dst_n * st_p, axis=0, keepdims=True) * ebl_last + jnp.sum(dke * ke, axis=0, keepdims=True)
    dq = dqe * eb
    dk = dke * ebl
    db = dqe * qe - dke * ke
    dec = _pair_decay(bb, rev)
    q_rep = _rows_rep(q)
    k_til = _tile_rows(k)
    do_rep = _rows_rep(dout)
    pairs = dec * q_rep * k_til
    a = jnp.dot(pairs.astype(BF16), seg_ref[...], preferred_element_type=F32)
    wb = jnp.dot((_tile_rows(v) * do_rep).astype(BF16), seg_ref[...], preferred_element_type=F32)
    gdec = wb * dec
    dq = dq + jnp.dot(sel_ref[...], (gdec * k_til).astype(BF16), preferred_element_type=F32)
    dk = dk + jnp.dot(selt_ref[...], (gdec * q_rep).astype(BF16), preferred_element_type=F32)
    dv = dv + jnp.dot(selt_ref[...], (a * do_rep).astype(BF16), preferred_element_type=F32)
    db = db + jnp.dot(seld_ref[...], (wb * pairs).astype(BF16), preferred_element_type=F32)
    db = db + jnp.where(row == last, dbl, 0.0)
    dq_ref[rows, :] = dq
    dv_ref[rows, :] = dv
    dk_scr[rows, :] = dk
    db_scr[rows, :] = db
    dstate[...] = dst_n * ebl_last + bd_ref[...] * _dot_tn(dob, qe.astype(BF16))


def _hgrn_bwd2(name, proj, lb_f, lb_b, st_f, st_b, do, b, s, hc_f, hc_b):
    t = proj.shape[0]
    nb = s // HBLK
    n_ch = HBLK // CHUNK

    n_chain = 2 * b

    def body(qf_ref, zf_ref, vf_ref, dof_ref, stf_ref, qb_ref, zb_ref, vb_ref, dob_ref, stb_ref, lbf_ref, lbb_ref,
             csf_ref, csb_ref, cstf_ref, cstb_ref, seg_ref, bd_ref, sel_ref, selt_ref, seld_ref,
             dqf_ref, dvf_ref, dzf_ref, dqb_ref, dvb_ref, dzb_ref, dlbf_ref, dlbb_ref,
             *scr):
        dstate, states, b_scr, k_scr, db_scr, dk_scr = [scr[i * n_chain:(i + 1) * n_chain] for i in range(6)]

        @pl.when(pl.program_id(0) == 0)
        def _():
            dlbf_ref[...] = jnp.zeros_like(dlbf_ref)
            dlbb_ref[...] = jnp.zeros_like(dlbb_ref)
            for d0 in dstate:
                d0[...] = jnp.zeros_like(d0)

        chains = []
        for bi in range(b):
            chains.append(dict(rev=False, q=qf_ref.at[bi], z=zf_ref.at[bi], v=vf_ref.at[bi], do=dof_ref.at[bi],
                               st=stf_ref.at[bi], lb=lbf_ref, cs=csf_ref, cst=cstf_ref, dq=dqf_ref.at[bi],
                               dv=dvf_ref.at[bi], dz=dzf_ref.at[bi], dlb=dlbf_ref, ci=2 * bi))
            chains.append(dict(rev=True, q=qb_ref.at[bi], z=zb_ref.at[bi], v=vb_ref.at[bi], do=dob_ref.at[bi],
                               st=stb_ref.at[bi], lb=lbb_ref, cs=csb_ref, cst=cstb_ref, dq=dqb_ref.at[bi],
                               dv=dvb_ref.at[bi], dz=dzb_ref.at[bi], dlb=dlbb_ref, ci=2 * bi + 1))
        for ch in chains:
            sig, f, g, sn, kk = _gates(ch["z"][...], ch["lb"][...])
            k_scr[ch["ci"]][...] = kk
            b_scr[ch["ci"]][...] = _ldot3(ch["cs"][...], g)
            ch["gates"] = (sig, f, sn)

        def replay(i, carry):
            return _scan_replays([((n_ch - 1 - i) if ch["rev"] else i, ch["rev"], st, ch["v"], k_scr[ch["ci"]],
                                   b_scr[ch["ci"]], states[ch["ci"]]) for ch, st in zip(chains, carry)], bd_ref)

        lax.fori_loop(0, n_ch, replay, tuple(ch["st"][...] for ch in chains))

        def chunk(i, carry):
            args = [(i if ch["rev"] else (n_ch - 1 - i), ch["rev"], ch["q"], ch["v"], ch["do"],
                     k_scr[ch["ci"]], b_scr[ch["ci"]], states[ch["ci"]], dstate[ch["ci"]], ch["dq"],
                     ch["dv"], dk_scr[ch["ci"]], db_scr[ch["ci"]]) for ch in chains]
            for g0 in range(0, n_chain, BWD_GROUP):
                _scan_chunks_bwd(args[g0:g0 + BWD_GROUP], seg_ref, bd_ref, sel_ref, selt_ref, seld_ref)
            return carry

        lax.fori_loop(0, n_ch, chunk, 0)
        for ch in chains:
            sig, f, sn = ch["gates"]
            lbv = ch["lb"][...]
            hi, lo = _split2(db_scr[ch["ci"]][...])
            dg = (jnp.dot(ch["cst"][...], hi, preferred_element_type=F32)
                  + jnp.dot(ch["cst"][...], lo, preferred_element_type=F32))
            dgf = jnp.where(f > F_MIN, dg / f, 0.0)
            dk = dk_scr[ch["ci"]][...]
            ch["dz"][...] = dgf * (1.0 - lbv) * sig * (1.0 - sig) - dk * (1.0 - lbv) * sn * (1.0 - sn)
            ch["dlb"][...] += _rowgroups(dgf * (1.0 - sig) - dk * sn)

    def col(c, rev):
        return pl.BlockSpec((b, HBLK, D_HGRN), lambda j: (0, j if rev else (nb - 1 - j), c))

    def st_spec(rev):
        return pl.BlockSpec((b, None, D_HGRN, D_HGRN), lambda j: (0, j if rev else (nb - 1 - j), 0, 0))

    sq = (D_HGRN, D_HGRN)
    blk = (HBLK, D_HGRN)
    pairs_shape = (CHUNK, CHUNK * CHUNK)
    proj3 = proj.reshape(b, s, proj.shape[1])
    do3 = do.reshape(b, s, D_HGRN)
    res = pl.pallas_call(
        body, name=name, grid=(nb,),
        in_specs=[col(COL_HQ, False), col(COL_FF, False), col(COL_HI, False), col(0, False), st_spec(False),
                  col(COL_HQ, True), col(COL_FB, True), col(COL_HI, True), col(0, True), st_spec(True),
                  _full((1, D_HGRN)), _full((1, D_HGRN)), _full((HBLK, HBLK)), _full((HBLK, HBLK)),
                  _full((HBLK, HBLK)), _full((HBLK, HBLK)), _full(sq), _full(sq),
                  _full(pairs_shape), _full(pairs_shape), _full(pairs_shape)],
        out_specs=[col(0, False)] * 3 + [col(0, True)] * 3 + [_full((8, D_HGRN))] * 2,
        out_shape=[_sds((b, s, D_HGRN), F32)] * 6 + [_sds((8, D_HGRN), F32)] * 2,
        scratch_shapes=[pltpu.VMEM(sq, F32)] * n_chain + [pltpu.VMEM((n_ch,) + sq, F32)] * n_chain
        + [pltpu.VMEM(blk, F32)] * (4 * n_chain),
        compiler_params=_params(("arbitrary",)),
    )(proj3, proj3, proj3, do3, st_f, proj3, proj3, proj3, do3, st_b, lb_f, lb_b, hc_f["cs"], hc_b["cs"],
      hc_f["cs_t"], hc_b["cs_t"], hc_f["seg"], hc_f["bd"], hc_f["sel"], hc_f["selt"], hc_f["seld"])
    return [r.reshape(t, D_HGRN) for r in res[:6]] + list(res[6:])


def _lower_bounds(logits):
    n = logits.shape[1]

    def body(x_ref, o_ref):
        x = x_ref[...]
        for d in range(2):
            rows = [x[l * 2 + d:l * 2 + d + 1, :] for l in range(DEPTH)]
            mx = functools.reduce(jnp.maximum, rows)
            ex = [jnp.exp(r - mx) for r in rows]
            tot = functools.reduce(lambda a, c: a + c, ex)
            sm = [e / tot for e in ex]
            run = jnp.zeros_like(sm[0])
            for l in range(DEPTH):
                run = run + sm[l]
                o_ref[l * 2 + d:l * 2 + d + 1, :] = run - sm[0]

    return pl.pallas_call(body, name="hgrn_lower_bounds", out_shape=_sds(logits.shape, F32),
                          in_specs=[_full(logits.shape)], out_specs=_full(logits.shape), grid=(1,),
                          compiler_params=_params(("arbitrary",)))(logits)


def _lower_bounds_bwd(logits, dlb):
    def body(x_ref, g_ref, o_ref):
        x = x_ref[...]
        gv = g_ref[...]
        for d in range(2):
            rows = [x[l * 2 + d:l * 2 + d + 1, :] for l in range(DEPTH)]
            gr = [gv[l * 2 + d:l * 2 + d + 1, :] for l in range(DEPTH)]
            mx = functools.reduce(jnp.maximum, rows)
            ex = [jnp.exp(r - mx) for r in rows]
            tot = functools.reduce(lambda a, c: a + c, ex)
            sm = [e / tot for e in ex]
            dsm = []
            for i in range(DEPTH):
                acc = functools.reduce(lambda a, c: a + c, gr[i:])
                if i == 0:
                    acc = acc - functools.reduce(lambda a, c: a + c, gr)
                dsm.append(acc)
            inner = functools.reduce(lambda a, c: a + c, [sm[i] * dsm[i] for i in range(DEPTH)])
            for i in range(DEPTH):
                o_ref[i * 2 + d:i * 2 + d + 1, :] = sm[i] * (dsm[i] - inner)

    return pl.pallas_call(body, name="hgrn_lower_bounds_bwd", out_shape=_sds(logits.shape, F32),
                          in_specs=[_full(logits.shape), _full(logits.shape)], out_specs=_full(logits.shape),
                          grid=(1,), compiler_params=_params(("arbitrary",)))(logits, dlb)


def _conv_rows(s):
    return s + 2 * (CONV_PAD + 1)


def _conv_fwd(name, proj, dw_w, dw_b, ln_w, ln_b, pw_w, pw_b, b, s):
    t = proj.shape[0]
    pad = CONV_PAD + 1
    nt = s // CONV_TILE

    def body(a_ref, g_ref, w_ref, dwb_ref, lnw_ref, lnb_ref, pw_ref, pwb_ref, y_ref, c_ref, upad, win):
        upad[0:pad, :] = jnp.zeros((pad, D_CONV), F32)
        upad[s + pad:s + 2 * pad, :] = jnp.zeros((pad, D_CONV), F32)

        def fill(i, carry):
            rows = pl.ds(pl.multiple_of(i * CONV_TILE, CONV_TILE), CONV_TILE)
            upad[pl.ds(pl.multiple_of(i * CONV_TILE + pad, pad), CONV_TILE), :] = a_ref[rows, :] * _sigmoid(g_ref[rows, :])
            return carry

        lax.fori_loop(0, nt, fill, 0)

        def tile(i, carry):
            r0 = pl.multiple_of(i * CONV_TILE, CONV_TILE)
            win[...] = upad[pl.ds(r0, CONV_TILE + 2 * pad), :]
            acc = jnp.zeros((CONV_TILE, D_CONV), F32)
            for j in range(CONV_W):
                acc = acc + win[j + 1:j + 1 + CONV_TILE, :] * w_ref[j:j + 1, :]
            c = acc + dwb_ref[...]
            c_ref[pl.ds(r0, CONV_TILE), :] = c
            mu = jnp.mean(c, axis=-1, keepdims=True)
            xc = c - mu
            rstd = lax.rsqrt(jnp.mean(xc * xc, axis=-1, keepdims=True) + LN_EPS)
            n = xc * rstd * lnw_ref[...] + lnb_ref[...]
            y_ref[pl.ds(r0, CONV_TILE), :] = (jnp.dot(_silu(n).astype(BF16), pw_ref[...].astype(BF16),
                                                      preferred_element_type=F32) + pwb_ref[...])
            return carry

        lax.fori_loop(0, nt, tile, 0)

    vec = _full((1, D_CONV))
    return pl.pallas_call(
        body, name=name, grid=(b,),
        in_specs=[pl.BlockSpec((s, D_CONV), lambda bi: (bi, COL_CA)), pl.BlockSpec((s, D_CONV), lambda bi: (bi, COL_CB)),
                  _full((CONV_W + 1, D_CONV)), vec, vec, vec, _full((D_CONV, D_CONV)), vec],
        out_specs=[pl.BlockSpec((s, D_CONV), lambda bi: (bi, 0))] * 2, out_shape=[_sds((t, D_CONV), F32)] * 2,
        scratch_shapes=[pltpu.VMEM((_conv_rows(s), D_CONV), F32), pltpu.VMEM((CONV_TILE + 2 * pad, D_CONV), F32)],
        compiler_params=_params(("parallel",)),
    )(proj, proj, dw_w, dw_b, ln_w, ln_b, pw_w, pw_b)


def _conv_bwd(name, proj, conv_out, dw_w, ln_w, ln_b, pw_w, dy, b, s):
    t = proj.shape[0]
    pad = CONV_PAD + 1
    nt = s // CONV_TILE

    def body(a_ref, g_ref, c_ref, w_ref, lnw_ref, lnb_ref, pw_ref, dy_ref, dab_ref, dpw_ref, ddw_ref, dvec_ref,
             upad, dcpad, tap_acc, win, dwin):
        @pl.when(pl.program_id(0) == 0)
        def _():
            dpw_ref[...] = jnp.zeros_like(dpw_ref)
            ddw_ref[...] = jnp.zeros_like(ddw_ref)
            dvec_ref[...] = jnp.zeros_like(dvec_ref)

        zeros = jnp.zeros((pad, D_CONV), F32)
        upad[0:pad, :] = zeros
        upad[s + pad:s + 2 * pad, :] = zeros
        dcpad[0:pad, :] = zeros
        dcpad[s + pad:s + 2 * pad, :] = zeros
        tap_acc[...] = jnp.zeros_like(tap_acc)

        def inner(i):
            return pl.ds(pl.multiple_of(i * CONV_TILE + pad, pad), CONV_TILE)

        def fill(i, carry):
            rows = pl.ds(pl.multiple_of(i * CONV_TILE, CONV_TILE), CONV_TILE)
            upad[inner(i), :] = a_ref[rows, :] * _sigmoid(g_ref[rows, :])
            return carry

        lax.fori_loop(0, nt, fill, 0)

        def tile_a(i, carry):
            r0 = pl.multiple_of(i * CONV_TILE, CONV_TILE)
            c = c_ref[pl.ds(r0, CONV_TILE), :]
            mu = jnp.mean(c, axis=-1, keepdims=True)
            xc = c - mu
            rstd = lax.rsqrt(jnp.mean(xc * xc, axis=-1, keepdims=True) + LN_EPS)
            xhat = xc * rstd
            n = xhat * lnw_ref[...] + lnb_ref[...]
            dyt = dy_ref[pl.ds(r0, CONV_TILE), :]
            dyb = dyt.astype(BF16)
            dpw_ref[...] += _dot_tn(_silu(n).astype(BF16), dyb)
            dn = _dot_nt(dyb, pw_ref[...].astype(BF16)) * _dsilu(n)
            dxh = dn * lnw_ref[...]
            dc = rstd * (dxh - jnp.mean(dxh, axis=-1, keepdims=True)
                         - xhat * jnp.mean(dxh * xhat, axis=-1, keepdims=True))
            dcpad[inner(i), :] = dc
            dvec_ref[0:1, :] += jnp.sum(dyt, axis=0, keepdims=True)
            dvec_ref[1:2, :] += jnp.sum(dn * xhat, axis=0, keepdims=True)
            dvec_ref[2:3, :] += jnp.sum(dn, axis=0, keepdims=True)
            dvec_ref[3:4, :] += jnp.sum(dc, axis=0, keepdims=True)
            return carry

        lax.fori_loop(0, nt, tile_a, 0)

        def tile_b(i, carry):
            r0 = pl.multiple_of(i * CONV_TILE, CONV_TILE)
            win[...] = upad[pl.ds(r0, CONV_TILE + 2 * pad), :]
            dwin[...] = dcpad[pl.ds(r0, CONV_TILE + 2 * pad), :]
            dct = dwin[pad:pad + CONV_TILE, :]
            du = jnp.zeros((CONV_TILE, D_CONV), F32)
            for j in range(CONV_W):
                du = du + dwin[2 * pad - 1 - j:2 * pad - 1 - j + CONV_TILE, :] * w_ref[j:j + 1, :]
                tap_acc[8 * j:8 * j + 8, :] += _rowgroups(dct * win[j + 1:j + 1 + CONV_TILE, :])
            rows = pl.ds(r0, CONV_TILE)
            sg = _sigmoid(g_ref[rows, :])
            dab_ref[rows, 0:D_CONV] = (du * sg).astype(BF16)
            dab_ref[rows, D_CONV:2 * D_CONV] = (du * a_ref[rows, :] * sg * (1.0 - sg)).astype(BF16)
            return carry

        lax.fori_loop(0, nt, tile_b, 0)
        for j in range(CONV_W):
            ddw_ref[j:j + 1, :] += jnp.sum(tap_acc[8 * j:8 * j + 8, :], axis=0, keepdims=True)

    vec = _full((1, D_CONV))
    return pl.pallas_call(
        body, name=name, grid=(b,),
        in_specs=[pl.BlockSpec((s, D_CONV), lambda bi: (bi, COL_CA)), pl.BlockSpec((s, D_CONV), lambda bi: (bi, COL_CB)),
                  pl.BlockSpec((s, D_CONV), lambda bi: (bi, 0)),
                  _full((CONV_W + 1, D_CONV)), vec, vec, _full((D_CONV, D_CONV)),
                  pl.BlockSpec((s, D_CONV), lambda bi: (bi, 0))],
        out_specs=[pl.BlockSpec((s, 2 * D_CONV), lambda bi: (bi, 0)), _full((D_CONV, D_CONV)),
                   _full((CONV_W + 1, D_CONV)), _full((8, D_CONV))],
        out_shape=[_sds((t, 2 * D_CONV), BF16), _sds((D_CONV, D_CONV), F32), _sds((CONV_W + 1, D_CONV), F32),
                   _sds((8, D_CONV), F32)],
        scratch_shapes=[pltpu.VMEM((_conv_rows(s), D_CONV), F32), pltpu.VMEM((_conv_rows(s), D_CONV), F32),
                        pltpu.VMEM((8 * CONV_W, D_CONV), F32), pltpu.VMEM((CONV_TILE + 2 * pad, D_CONV), F32),
                        pltpu.VMEM((CONV_TILE + 2 * pad, D_CONV), F32)],
        compiler_params=_params(("arbitrary",)),
    )(proj, proj, conv_out, dw_w, ln_w, ln_b, pw_w, dy)


def _mix_fwd(name, y_attn, o_fw, o_bw, proj, y_conv, aw, gw, cw, seg):
    t = y_attn.shape[0]
    tm = _row_tile(t)

    def body(ya_ref, of_ref, ob_ref, hg_ref, yc_ref, aw_ref, gw_ref, cw_ref, seg_ref, o_ref):
        ya = ya_ref[...]
        ra = lax.rsqrt(jnp.mean(ya * ya, axis=-1, keepdims=True) + EPS)
        o_ref[:, 0:D_ATTN] = (ya * ra * aw_ref[...]).astype(BF16)
        o = of_ref[...] + ob_ref[...]
        ro = lax.rsqrt(jnp.dot((o * o).astype(BF16), seg_ref[...], preferred_element_type=F32) + EPS)
        o_ref[:, D_ATTN:D_ATTN + D_HGRN] = (o * ro * gw_ref[...] * _silu(hg_ref[...])).astype(BF16)
        yc = yc_ref[...]
        rc = lax.rsqrt(jnp.mean(yc * yc, axis=-1, keepdims=True) + EPS)
        o_ref[:, D_ATTN + D_HGRN:D_MODEL] = (yc * rc * cw_ref[...]).astype(BF16)

    def tile(w, c=0):
        return pl.BlockSpec((tm, w), lambda i: (i, c))

    return pl.pallas_call(
        body, name=name, grid=(t // tm,),
        in_specs=[tile(D_ATTN), tile(D_HGRN), tile(D_HGRN), tile(D_HGRN, COL_HG), tile(D_CONV),
                  _full((1, D_ATTN)), _full((1, D_HGRN)), _full((1, D_CONV)), _full((D_HGRN, D_HGRN))],
        out_specs=tile(D_MODEL), out_shape=_sds((t, D_MODEL), BF16),
        compiler_params=_params(("parallel",)),
    )(y_attn, o_fw, o_bw, proj, y_conv, aw, gw, cw, seg)


def _mix_bwd(name, dmix, y_attn, o_fw, o_bw, proj, y_conv, aw, gw, cw, seg, deps=()):
    t = y_attn.shape[0]
    tm = _row_tile(t)

    def rms_bwd(x, w, dy):
        r = lax.rsqrt(jnp.mean(x * x, axis=-1, keepdims=True) + EPS)
        gwv = dy * w
        return r * gwv - x * (r * r * r) * jnp.mean(gwv * x, axis=-1, keepdims=True), _rowgroups(dy * x * r)

    def body(dm_ref, ya_ref, of_ref, ob_ref, hg_ref, yc_ref, aw_ref, gw_ref, cw_ref, seg_ref, *rest):
        dya_ref, do_ref, dhg_ref, dyc_ref, daw_ref, dgw_ref, dcw_ref = rest[-7:]

        @pl.when(pl.program_id(0) == 0)
        def _():
            daw_ref[...] = jnp.zeros_like(daw_ref)
            dgw_ref[...] = jnp.zeros_like(dgw_ref)
            dcw_ref[...] = jnp.zeros_like(dcw_ref)

        dya, daw = rms_bwd(ya_ref[...], aw_ref[...], dm_ref[:, 0:D_ATTN])
        dya_ref[...] = dya
        daw_ref[...] += daw
        dyc, dcw = rms_bwd(yc_ref[...], cw_ref[...], dm_ref[:, D_ATTN + D_HGRN:D_MODEL])
        dyc_ref[...] = dyc
        dcw_ref[...] += dcw
        d2 = dm_ref[:, D_ATTN:D_ATTN + D_HGRN]
        o = of_ref[...] + ob_ref[...]
        hg = hg_ref[...]
        ro = lax.rsqrt(jnp.dot((o * o).astype(BF16), seg_ref[...], preferred_element_type=F32) + EPS)
        dn = d2 * _silu(hg)
        dhg_ref[...] = (d2 * o * ro * gw_ref[...] * _dsilu(hg)).astype(BF16)
        gwv = dn * gw_ref[...]
        do_ref[...] = ro * gwv - o * (ro * ro * ro) * _rdot2(gwv * o, seg_ref[...])
        dgw_ref[...] += _rowgroups(dn * o * ro)

    def tile(w, c=0):
        return pl.BlockSpec((tm, w), lambda i: (i, c))

    return pl.pallas_call(
        body, name=name, grid=(t // tm,),
        in_specs=[tile(D_MODEL), tile(D_ATTN), tile(D_HGRN), tile(D_HGRN), tile(D_HGRN, COL_HG), tile(D_CONV),
                  _full((1, D_ATTN)), _full((1, D_HGRN)), _full((1, D_CONV)), _full((D_HGRN, D_HGRN))]
        + [_full(a.shape) for a in deps],
        out_specs=[tile(D_ATTN), tile(D_HGRN), tile(D_HGRN), tile(D_CONV),
                   _full((8, D_ATTN)), _full((8, D_HGRN)), _full((8, D_CONV))],
        out_shape=[_sds((t, D_ATTN), F32), _sds((t, D_HGRN), F32), _sds((t, D_HGRN), BF16), _sds((t, D_CONV), F32),
                   _sds((8, D_ATTN), F32), _sds((8, D_HGRN), F32), _sds((8, D_CONV), F32)],
        compiler_params=_params(("arbitrary",)),
    )(dmix, y_attn, o_fw, o_bw, proj, y_conv, aw, gw, cw, seg, *deps)


def _dproj(name, dp_attn, dq_f, dq_b, dz_fw, dz_bw, dv_f, dv_b, dhg, dp_conv):
    t = dq_f.shape[0]
    tm = _row_tile(t)
    wa, wc = dp_attn.shape[1], dp_conv.shape[1]

    def body(at_ref, qf_ref, qb_ref, zf_ref, zb_ref, vf_ref, vb_ref, hg_ref, cv_ref, o_ref):
        o_ref[:, 0:wa] = at_ref[...]
        cols = (qf_ref[...] + qb_ref[...], zf_ref[...], zb_ref[...], vf_ref[...] + vb_ref[...], hg_ref[...])
        for i, val in enumerate(cols):
            o_ref[:, wa + i * D_HGRN:wa + (i + 1) * D_HGRN] = val.astype(BF16)
        o_ref[:, wa + 5 * D_HGRN:D_IN] = cv_ref[...]

    tile = lambda w: pl.BlockSpec((tm, w), lambda i: (i, 0))
    return pl.pallas_call(
        body, name=name, grid=(t // tm,), in_specs=[tile(wa)] + [tile(D_HGRN)] * 7 + [tile(wc)],
        out_specs=tile(D_IN), out_shape=_sds((t, D_IN), BF16), compiler_params=_params(("parallel",)),
    )(dp_attn, dq_f, dq_b, dz_fw, dz_bw, dv_f, dv_b, dhg, dp_conv)


def _mm_tile(t):
    return min(512, t)


def _resident(shape):
    n = len(shape)
    return pl.BlockSpec(tuple(shape), lambda *_: (0,) * n, pipeline_mode=pl.Buffered(1))


def _w_blk(rows, cols, j_of):
    return pl.BlockSpec((None, rows, cols), lambda *g: (j_of(*g), 0, 0))


def _layer_fwd(l, x, wget, sm, tabs, cst, b, s, deps, target=None):
    t = x.shape[0]
    tm = _mm_tile(t)
    nt = t // tm
    pre = "l%d_" % l
    row = lambda w: pl.BlockSpec((tm, w), lambda i, *_: (i, 0))

    def normed(x_ref, nw_ref):
        xv = x_ref[...]
        r = lax.rsqrt(jnp.mean(xv * xv, axis=-1, keepdims=True) + EPS)
        return (xv * r * nw_ref[...]).astype(BF16)

    def in_body(x_ref, nw_ref, w_ref, *rest):
        o_ref, h_ref = rest[-2:]
        hv = normed(x_ref, nw_ref)
        h_ref[...] = hv
        for j in range(N_CHIP):
            o_ref[:, j * IN_BLK:(j + 1) * IN_BLK] = jnp.dot(hv, w_ref[j], preferred_element_type=F32)

    w_in = wget(l, "w_in", x)
    proj, h1 = pl.pallas_call(
        in_body, name=pre + "in_proj", grid=(nt,),
        in_specs=[row(D_MODEL), _full((1, D_MODEL)), _resident(w_in.shape)] + [_full(a.shape) for a in deps],
        out_specs=[row(D_IN), row(D_MODEL)], out_shape=[_sds((t, D_IN), F32), _sds((t, D_MODEL), BF16)],
        compiler_params=_params(("parallel",)),
    )(x, sm["mix_norm_w"][l], w_in, *deps)
    qn, kr, vr = _attn_prep(pre + "attn_prep", proj, s, tabs, sm["q_norm_w"][l], sm["k_norm_w"][l], cst["attn"])
    y_attn = _attn_fwd(pre + "attn", qn, kr, vr, b, s)
    o_fw, o_bw, st_fw, st_bw = _hgrn_fwd2(pre + "hgrn", proj, sm["lb"][l][0], sm["lb"][l][1], b, s, cst["hg_fw"],
                                          cst["hg_bw"])
    y_conv, conv_out = _conv_fwd(pre + "conv", proj, sm["conv_dw_w"][l], sm["conv_dw_b"][l], sm["conv_ln_w"][l],
                       sm["conv_ln_b"][l], sm["conv_pw_w"][l], sm["conv_pw_b"][l], b, s)
    mixed = _mix_fwd(pre + "mix", y_attn, o_fw, o_bw, proj, y_conv, sm["attn_out_norm_w"][l], sm["gnorm_w"][l],
                     sm["conv_out_norm_w"][l], cst["seg_h"])
    (x1,) = _mm(pre + "out_proj", (nt,),
                [(mixed, row(D_MODEL), wget(l, "w_out", mixed),
                  pl.BlockSpec((N_CHIP, OUT_BLK, D_MODEL), lambda i: (0, 0, 0)), NN)],
                [(x, row(D_MODEL))], [(_sds((t, D_MODEL), F32), row(D_MODEL))],
                lambda tot, xr: (xr + tot,))
    ff3 = pl.BlockSpec((N_CHIP, tm, FF_BLK), lambda i: (0, i, 0))
    ffs = _sds((N_CHIP, t, FF_BLK), BF16)

    def gu_body(x_ref, nw_ref, wg_ref, wu_ref, g_ref, u_ref, a_ref, h_ref):
        hv = normed(x_ref, nw_ref)
        h_ref[...] = hv
        for j in range(N_CHIP):
            gv = jnp.dot(hv, wg_ref[j], preferred_element_type=F32)
            uv = jnp.dot(hv, wu_ref[j], preferred_element_type=F32)
            g_ref[j] = gv.astype(BF16)
            u_ref[j] = uv.astype(BF16)
            a_ref[j] = (_silu(gv) * uv).astype(BF16)

    w_gate, w_up = wget(l, "w_gate", x1), wget(l, "w_up", x1)
    gate, up, act, h2 = pl.pallas_call(
        gu_body, name=pre + "ffn_gate_up", grid=(nt,),
        in_specs=[row(D_MODEL), _full((1, D_MODEL)), _resident(w_gate.shape), _resident(w_up.shape)],
        out_specs=[ff3, ff3, ff3, row(D_MODEL)], out_shape=[ffs, ffs, ffs, _sds((t, D_MODEL), BF16)],
        compiler_params=_params(("parallel",)),
    )(x1, sm["ffn_norm_w"][l], w_gate, w_up)

    def down_body(a_ref, w_ref, x_ref, o_ref):
        tot = x_ref[...]
        for j in range(N_CHIP):
            tot = tot + jnp.dot(a_ref[j], w_ref[j], preferred_element_type=F32)
        o_ref[...] = tot

    def down_loss_body(a_ref, w_ref, x_ref, t_ref, dy_ref, acc_ref):
        tot = x_ref[...]
        for j in range(N_CHIP):
            tot = tot + jnp.dot(a_ref[j], w_ref[j], preferred_element_type=F32)
        e = tot - t_ref[...]
        dy_ref[...] = e * (1.0 / D_MODEL)

        @pl.when(pl.program_id(0) == 0)
        def _():
            acc_ref[...] = jnp.zeros_like(acc_ref)

        acc_ref[...] += _rowgroups(e * e)

    w_down = wget(l, "w_down", act)
    if target is None:
        x2 = pl.pallas_call(
            down_body, name=pre + "ffn_down", grid=(nt,), in_specs=[ff3, _resident(w_down.shape), row(D_MODEL)],
            out_specs=row(D_MODEL), out_shape=_sds((t, D_MODEL), F32), compiler_params=_params(("parallel",)),
        )(act, w_down, x1)
    else:
        x2 = pl.pallas_call(
            down_loss_body, name=pre + "ffn_down_loss", grid=(nt,),
            in_specs=[ff3, _resident(w_down.shape), row(D_MODEL), row(D_MODEL)],
            out_specs=[row(D_MODEL), _full((8, D_MODEL))],
            out_shape=[_sds((t, D_MODEL), F32), _sds((8, D_MODEL), F32)], compiler_params=_params(("arbitrary",)),
        )(act, w_down, x1, target)
    saved = dict(x=x, h1=h1, proj=proj, qn=qn, kr=kr, vr=vr, y_attn=y_attn, o_fw=o_fw, o_bw=o_bw, st_fw=st_fw,
                 st_bw=st_bw, y_conv=y_conv, conv_out=conv_out, mixed=mixed, x1=x1, h2=h2, gate=gate, up=up, act=act)
    return x2, saved


def _layer_bwd(l, dx2, sv, wget, sm, tabs, cst, b, s, on_grads):
    t = dx2.shape[0]
    tm = _mm_tile(t)
    nt = t // tm
    pre = "l%d_" % l
    tk = min(2048, t)
    nk = t // tk
    row = lambda w: pl.BlockSpec((tm, w), lambda i, *_: (i, 0))
    ff3 = pl.BlockSpec((N_CHIP, tm, FF_BLK), lambda i: (0, i, 0))
    ffs = _sds((N_CHIP, t, FF_BLK), BF16)

    w_down, w_gate, w_up = wget(l, "w_down", dx2), wget(l, "w_gate", dx2), wget(l, "w_up", dx2)

    def ddx_body(dx_ref, w_ref, g_ref, u_ref, dg_ref, du_ref):
        dxb = dx_ref[...].astype(BF16)
        for j in range(N_CHIP):
            da = _dot_nt(dxb, w_ref[j])
            g = g_ref[j].astype(F32)
            sg = _sigmoid(g)
            dg_ref[j] = (da * u_ref[j].astype(F32) * (sg * (1.0 + g * (1.0 - sg)))).astype(BF16)
            du_ref[j] = (da * (g * sg)).astype(BF16)

    dgate, dup = pl.pallas_call(
        ddx_body, name=pre + "ffn_down_dx", grid=(nt,), in_specs=[row(D_MODEL), _resident(w_down.shape), ff3, ff3],
        out_specs=[ff3, ff3], out_shape=[ffs, ffs], compiler_params=_params(("parallel",)),
    )(dx2, w_down, sv["gate"], sv["up"])
    colt = lambda w: pl.BlockSpec((tk, w), lambda j, k: (k, 0))
    fft = pl.BlockSpec((None, tk, FF_BLK), lambda j, k: (j, k, 0))
    (g_down,) = _mm(pre + "ffn_down_dw", (N_CHIP, nk), [(sv["act"], fft, dx2, colt(D_MODEL), TN)], [],
                    [(_sds((N_CHIP, FF_BLK, D_MODEL), BF16), pl.BlockSpec((None, FF_BLK, D_MODEL), lambda j, k: (j, 0, 0)))],
                    lambda tot: (tot,), acc=(1, (FF_BLK, D_MODEL)))
    wff = pl.BlockSpec((None, D_MODEL, FF_BLK), lambda j, k: (j, 0, 0))
    (g_gate,) = _mm(pre + "ffn_gate_dw", (N_CHIP, nk), [(sv["h2"], colt(D_MODEL), dgate, fft, TN)], [],
                    [(_sds((N_CHIP, D_MODEL, FF_BLK), BF16), wff)], lambda tot: (tot,), acc=(1, (D_MODEL, FF_BLK)))
    (g_up,) = _mm(pre + "ffn_up_dw", (N_CHIP, nk), [(sv["h2"], colt(D_MODEL), dup, fft, TN)], [],
                  [(_sds((N_CHIP, D_MODEL, FF_BLK), BF16), wff)], lambda tot: (tot,), acc=(1, (D_MODEL, FF_BLK)))

    def norm_bwd_tail(dh, x_ref, nw_ref, dres_ref, dx_ref, dw_ref):
        xv = x_ref[...]
        r = lax.rsqrt(jnp.mean(xv * xv, axis=-1, keepdims=True) + EPS)
        gw = dh * nw_ref[...]
        dx_ref[...] = dres_ref[...] + r * gw - xv * (r * r * r) * jnp.mean(gw * xv, axis=-1, keepdims=True)

        @pl.when(pl.program_id(0) == 0)
        def _():
            dw_ref[...] = jnp.zeros_like(dw_ref)

        dw_ref[...] += _rowgroups(dh * xv * r)

    def dh_body(dg_ref, du_ref, wg_ref, wu_ref, x_ref, nw_ref, dres_ref, *rest):
        tot = None
        for j in range(N_CHIP):
            r = _dot_nt(dg_ref[j], wg_ref[j]) + _dot_nt(du_ref[j], wu_ref[j])
            tot = r if tot is None else tot + r
        norm_bwd_tail(tot, x_ref, nw_ref, dres_ref, *rest[-2:])

    deps = on_grads(l, dict(w_gate=g_gate, w_up=g_up, w_down=g_down))
    dx1, d_ffn_norm = pl.pallas_call(
        dh_body, name=pre + "ffn_dh", grid=(nt,),
        in_specs=[ff3, ff3, _resident(w_gate.shape), _resident(w_up.shape), row(D_MODEL), _full((1, D_MODEL)),
                  row(D_MODEL)] + [_full(a.shape) for a in deps],
        out_specs=[row(D_MODEL), _full((8, D_MODEL))], out_shape=[_sds((t, D_MODEL), F32), _sds((8, D_MODEL), F32)],
        compiler_params=_params(("arbitrary",)),
    )(dgate, dup, w_gate, w_up, sv["x1"], sm["ffn_norm_w"][l], dx2, *deps)

    (dmix,) = _mm(pre + "out_proj_dx", (nt,),
                  [(dx1, row(D_MODEL), wget(l, "w_out", dx2),
                    pl.BlockSpec((N_CHIP, OUT_BLK, D_MODEL), lambda i: (0, 0, 0)), NT)], [],
                  [(_sds((t, D_MODEL), F32), row(D_MODEL))], lambda tot: (tot,))
    (g_out,) = _mm(pre + "out_proj_dw", (N_CHIP, nk),
                   [(sv["mixed"], pl.BlockSpec((tk, OUT_BLK), lambda j, k: (k, j)), dx1, colt(D_MODEL), TN)], [],
                   [(_sds((N_CHIP, OUT_BLK, D_MODEL), BF16), pl.BlockSpec((None, OUT_BLK, D_MODEL), lambda j, k: (j, 0, 0)))],
                   lambda tot: (tot,), acc=(1, (OUT_BLK, D_MODEL)))
    proj = sv["proj"]
    dya, do_h, dhg, dyc, d_aw, d_gw, d_cw = _mix_bwd(
        pre + "mix_bwd", dmix, sv["y_attn"], sv["o_fw"], sv["o_bw"], proj, sv["y_conv"],
        sm["attn_out_norm_w"][l], sm["gnorm_w"][l], sm["conv_out_norm_w"][l], cst["seg_h"],
        on_grads(l, dict(w_out=g_out)))
    dqs, dkr, dvr = _attn_bwd(pre + "attn_bwd", sv["qn"], sv["kr"], sv["vr"], dya, b, s)
    dp_attn, d_qw, d_kw = _attn_prep_bwd(pre + "attn_prep_bwd", proj, s, tabs, sm["q_norm_w"][l], sm["k_norm_w"][l],
                                         cst["attn"], dqs, dkr, dvr)
    dq_f, dv_f, dz_fw, dq_b, dv_b, dz_bw, dlb_fw, dlb_bw = _hgrn_bwd2(
        pre + "hgrn_bwd", proj, sm["lb"][l][0], sm["lb"][l][1], sv["st_fw"], sv["st_bw"], do_h, b, s,
        cst["hg_fw"], cst["hg_bw"])
    dp_conv, d_pw, d_dw, d_cvec = _conv_bwd(pre + "conv_bwd", proj, sv["conv_out"], sm["conv_dw_w"][l],
                                            sm["conv_ln_w"][l], sm["conv_ln_b"][l], sm["conv_pw_w"][l], dyc, b, s)
    dproj = _dproj(pre + "dproj", dp_attn, dq_f, dq_b, dz_fw, dz_bw, dv_f, dv_b, dhg, dp_conv)
    g_pw = d_pw.reshape(N_CHIP, D_CONV // N_CHIP, D_CONV).astype(BF16)

    (g_in,) = _mm(pre + "in_proj_dw", (N_CHIP, nk),
                  [(sv["h1"], colt(D_MODEL), dproj, pl.BlockSpec((tk, IN_BLK), lambda j, k: (k, j)), TN)], [],
                  [(_sds((N_CHIP, D_MODEL, IN_BLK), BF16), pl.BlockSpec((None, D_MODEL, IN_BLK), lambda j, k: (j, 0, 0)))],
                  lambda tot: (tot,), acc=(1, (D_MODEL, IN_BLK)))

    def indx_body(dp_ref, w_ref, x_ref, nw_ref, dres_ref, *rest):
        tot = None
        for j in range(N_CHIP):
            r = _dot_nt(dp_ref[:, j * IN_BLK:(j + 1) * IN_BLK], w_ref[j])
            tot = r if tot is None else tot + r
        norm_bwd_tail(tot, x_ref, nw_ref, dres_ref, *rest[-2:])

    w_in = wget(l, "w_in", dx2)
    deps = on_grads(l, dict(w_in=g_in, conv_pw_w=g_pw))
    dx, d_mix_norm = pl.pallas_call(
        indx_body, name=pre + "in_proj_dx", grid=(nt,),
        in_specs=[row(D_IN), _resident(w_in.shape), row(D_MODEL), _full((1, D_MODEL)), row(D_MODEL)]
        + [_full(a.shape) for a in deps],
        out_specs=[row(D_MODEL), _full((8, D_MODEL))], out_shape=[_sds((t, D_MODEL), F32), _sds((8, D_MODEL), F32)],
        compiler_params=_params(("arbitrary",)),
    )(dproj, w_in, sv["x"], sm["mix_norm_w"][l], dx1, *deps)
    heads = lambda v, n: v.sum(axis=0).reshape(n, HEAD_DIM).sum(axis=0)
    small = dict(
        mix_norm_w=d_mix_norm.sum(axis=0), q_norm_w=heads(d_qw, D_ATTN // HEAD_DIM), k_norm_w=heads(d_kw, N_KV),
        lb=jnp.stack([dlb_fw.sum(axis=0), dlb_bw.sum(axis=0)]), hgrn_gnorm_w=heads(d_gw, D_HGRN // HEAD_DIM),
        conv_dw_w=d_dw[:CONV_W], conv_dw_b=d_cvec[3], conv_ln_w=d_cvec[1], conv_ln_b=d_cvec[2],
        conv_pw_b=d_cvec[0], attn_out_norm_w=d_aw.sum(axis=0), conv_out_norm_w=d_cw.sum(axis=0),
        ffn_norm_w=d_ffn_norm.sum(axis=0))
    return dx, small


SMALL_ORDER = ("mix_norm_w", "q_norm_w", "k_norm_w", "lb", "hgrn_gnorm_w", "conv_dw_w", "conv_dw_b", "conv_ln_w",
               "conv_ln_b", "conv_pw_b", "attn_out_norm_w", "conv_out_norm_w", "ffn_norm_w")
BIG_ORDER = ("w_in", "w_out", "w_gate", "w_up", "w_down")
SCATTER_ORDER = BIG_ORDER + ("conv_pw_w",)


def _local_step(x, target, wget, sm, deps, on_grads):
    b, s, d = x.shape
    t = b * s
    cos, sin = _rope_tables(s)
    tabs = dict(cq=jnp.tile(cos, (1, D_ATTN // HEAD_DIM)), sq=jnp.tile(sin, (1, D_ATTN // HEAD_DIM)),
                ck=jnp.tile(cos, (1, N_KV)), sk=jnp.tile(sin, (1, N_KV)))
    cst = dict(attn=_attn_consts(), hg_fw=_hgrn_consts(False), hg_bw=_hgrn_consts(True),
               seg_h=_bf(_seg_matrix(D_HGRN, HEAD_DIM, 1.0 / HEAD_DIM)))
    vec = lambda a: a.reshape(DEPTH, 1, -1)
    smk = dict(sm)
    for n in ("mix_norm_w", "conv_dw_b", "conv_ln_w", "conv_ln_b", "conv_pw_b", "attn_out_norm_w", "conv_out_norm_w",
              "ffn_norm_w"):
        smk[n] = vec(sm[n])
    smk["q_norm_w"] = vec(jnp.tile(sm["q_norm_w"], (1, D_ATTN // HEAD_DIM)))
    smk["k_norm_w"] = vec(jnp.tile(sm["k_norm_w"], (1, N_KV)))
    smk["gnorm_w"] = vec(jnp.tile(sm["hgrn_gnorm_w"], (1, D_HGRN // HEAD_DIM)))
    smk["lb"] = sm["lb"].reshape(DEPTH, 2, 1, D_HGRN)
    smk["conv_dw_w"] = jnp.pad(sm["conv_dw_w"], ((0, 0), (0, 1), (0, 0)))

    h = x.reshape(t, d)
    saved = []
    for l in range(DEPTH):
        h, sv = _layer_fwd(l, h, wget, smk, tabs, cst, b, s, deps if l == 0 else (),
                           target.reshape(t, d) if l == DEPTH - 1 else None)
        saved.append(sv)
    dy, sq = h
    sq_sum = jnp.sum(sq)
    dh = dy
    smalls = [None] * DEPTH
    for l in reversed(range(DEPTH)):
        dh, smalls[l] = _layer_bwd(l, dh, saved[l], wget, smk, tabs, cst, b, s, on_grads)
    return sq_sum, dh.reshape(b, s, d), smalls


HBM_SPEC = pl.BlockSpec(memory_space=pltpu.HBM)


def _exchange(name, arrs, mode):
    n = len(arrs)
    if mode == "gather8":
        flips = [(fx, fy, fc) for fx in (0, 1) for fy in (0, 1) for fc in (0, 1)][1:]
    elif mode == "sibling":
        flips = [(0, 0, 1)]
    else:
        flips = [(1, 0, 0), (0, 1, 0), (1, 1, 0)]
    n_f = len(flips)

    def body(*refs):
        ins, outs = refs[:n], refs[n:2 * n]
        send_sems, recv_sems, local_sems = refs[2 * n:]
        x, y, c = lax.axis_index("x"), lax.axis_index("y"), lax.axis_index("c")

        def slot_of(px, py, pc):
            return (2 * px + py) if mode != "gather8" else (4 * px + 2 * py + pc)

        me = slot_of(x, y, c)
        started = []
        for i in range(n):
            if mode != "sibling":
                src = ins[i].at[me] if mode == "scatter4" else ins[i]
                loc = pltpu.make_async_copy(src, outs[i].at[me], local_sems.at[i])
                loc.start()
                started.append(loc)
        sends, recvs = [], []
        for i in range(n):
            for f, (fx, fy, fc) in enumerate(flips):
                peer = (x ^ fx, y ^ fy, c ^ fc)
                ps = slot_of(*peer)
                if mode == "sibling":
                    src, dst, landed = ins[i], outs[i], outs[i]
                elif mode == "scatter4":
                    src, dst, landed = ins[i].at[ps], outs[i].at[me], outs[i].at[ps]
                else:
                    src, dst, landed = ins[i], outs[i].at[me], outs[i].at[ps]
                k = i * n_f + f
                cp = pltpu.make_async_remote_copy(src_ref=src, dst_ref=dst, send_sem=send_sems.at[k],
                                                  recv_sem=recv_sems.at[k], device_id=peer,
                                                  device_id_type=pl.DeviceIdType.MESH)
                cp.start()
                sends.append(cp)
                recvs.append(pltpu.make_async_remote_copy(src_ref=src, dst_ref=landed, send_sem=send_sems.at[k],
                                                          recv_sem=recv_sems.at[k], device_id=peer,
                                                          device_id_type=pl.DeviceIdType.MESH))
        for cp in sends:
            cp.wait_send()
        for cp in recvs:
            cp.wait_recv()
        for loc in started:
            loc.wait()

    def out_sds(a):
        if mode == "gather4":
            return _sds((N_CHIP,) + a.shape, a.dtype)
        if mode == "gather8":
            return _sds((N_DEV,) + a.shape, a.dtype)
        return _sds(a.shape, a.dtype)

    res = pl.pallas_call(
        body, name=name, in_specs=[HBM_SPEC] * n, out_specs=[HBM_SPEC] * n, out_shape=[out_sds(a) for a in arrs],
        scratch_shapes=[pltpu.SemaphoreType.DMA((n * n_f,)), pltpu.SemaphoreType.DMA((n * n_f,)),
                        pltpu.SemaphoreType.DMA((max(n, 1),))],
    )(*arrs)
    return list(res)


SEM_SPEC = pl.BlockSpec(memory_space=pltpu.SEMAPHORE)
SPLIT_EFFECT = pltpu.SideEffectType.DATAFLOW_SIDE_EFFECTING
CHIP_FLIPS = ((1, 0), (0, 1), (1, 1))


def _chip_copies(src_refs, land_refs, send_sems, recv_sems, scatter):
    x, y, c = lax.axis_index("x"), lax.axis_index("y"), lax.axis_index("c")
    me = 2 * x + y
    out = []
    for i, land in enumerate(land_refs):
        if scatter == "sibling":
            kw = dict(send_sem=send_sems.at[i], recv_sem=recv_sems.at[i], device_id=(x, y, 1 - c),
                      device_id_type=pl.DeviceIdType.MESH)
            cp = pltpu.make_async_remote_copy(src_ref=src_refs[i], dst_ref=land, **kw)
            out.append((cp, cp))
            continue
        if scatter == "all":
            dev = 4 * x + 2 * y + c
            for f in range(1, N_DEV):
                fx, fy, fc = (f >> 2) & 1, (f >> 1) & 1, f & 1
                peer = (x ^ fx, y ^ fy, c ^ fc)
                ps = 4 * (x ^ fx) + 2 * (y ^ fy) + (c ^ fc)
                k = i * (N_DEV - 1) + f - 1
                kw = dict(send_sem=send_sems.at[k], recv_sem=recv_sems.at[k], device_id=peer,
                          device_id_type=pl.DeviceIdType.MESH)
                out.append((pltpu.make_async_remote_copy(src_ref=land.at[dev], dst_ref=land.at[dev], **kw),
                            pltpu.make_async_remote_copy(src_ref=land.at[dev], dst_ref=land.at[ps], **kw)))
            continue
        for f, (fx, fy) in enumerate(CHIP_FLIPS):
            peer = (x ^ fx, y ^ fy, c)
            ps = 2 * (x ^ fx) + (y ^ fy)
            src = src_refs[i].at[ps] if scatter else land.at[me]
            k = i * len(CHIP_FLIPS) + f
            kw = dict(send_sem=send_sems.at[k], recv_sem=recv_sems.at[k], device_id=peer,
                      device_id_type=pl.DeviceIdType.MESH)
            out.append((pltpu.make_async_remote_copy(src_ref=src, dst_ref=land.at[me], **kw),
                        pltpu.make_async_remote_copy(src_ref=src, dst_ref=land.at[ps], **kw)))
    return out


def _split_start(name, srcs, lands, scatter):
    n = len(lands)
    n_src = len(srcs)
    n_sem = n * {"sibling": 1, "all": N_DEV - 1}.get(scatter, len(CHIP_FLIPS))

    def body(*refs):
        src_refs = refs[:n_src]
        land_refs = refs[n_src:n_src + n]
        send_sems, recv_sems = refs[n_src + n], refs[n_src + n + 1]
        token = refs[-1]
        for start, _ in _chip_copies(src_refs, land_refs, send_sems, recv_sems, scatter):
            start.start()
        token[...] = jnp.zeros_like(token)

    arrs = list(srcs) + list(lands)
    res = pl.pallas_call(
        body, name=name,
        out_shape=(pltpu.SemaphoreType.DMA((n_sem,)), pltpu.SemaphoreType.DMA((n_sem,)),
                   *[pltpu.HBM(a.shape, a.dtype) for a in arrs], _sds((8, LANES), F32)),
        in_specs=[HBM_SPEC] * len(arrs),
        out_specs=(SEM_SPEC, SEM_SPEC, *[HBM_SPEC] * len(arrs), pl.BlockSpec(memory_space=pltpu.VMEM)),
        input_output_aliases={i: 2 + i for i in range(len(arrs))},
        compiler_params=pltpu.CompilerParams(has_side_effects=SPLIT_EFFECT),
    )(*[pltpu.with_memory_space_constraint(a, pltpu.HBM) for a in arrs])
    return dict(send=res[0], recv=res[1], srcs=list(res[2:2 + n_src]), lands=list(res[2 + n_src:2 + n_src + n]),
                token=res[-1], scatter=scatter)


def _split_wait(name, started, after, with_srcs=False):
    srcs, lands, scatter = started["srcs"], started["lands"], started["scatter"]
    n, n_src = len(lands), len(srcs)

    def body(*refs):
        src_refs = refs[:n_src]
        land_refs = refs[n_src:n_src + n]
        send_sems, recv_sems = refs[n_src + n], refs[n_src + n + 1]
        for _, wait in _chip_copies(src_refs, land_refs, send_sems, recv_sems, scatter):
            wait.wait_send()
            wait.wait_recv()

    arrs = list(srcs) + list(lands)
    res = pl.pallas_call(
        body, name=name, out_shape=tuple(pltpu.HBM(a.shape, a.dtype) for a in arrs),
        in_specs=[HBM_SPEC] * len(arrs) + [SEM_SPEC, SEM_SPEC, pl.BlockSpec(memory_space=pl.ANY)],
        out_specs=tuple([HBM_SPEC] * len(arrs)), input_output_aliases={i: i for i in range(len(arrs))},
        compiler_params=pltpu.CompilerParams(has_side_effects=SPLIT_EFFECT),
    )(*arrs, started["send"], started["recv"], after)
    return (list(res[:n_src]), list(res[n_src:])) if with_srcs else list(res[n_src:])


def _flat_tile(rows):
    for cand in (512, 256, 128, 64, 32, 16, 8):
        if rows % cand == 0:
            return cand
    return rows


def _cast_slot(name, a, l, chip, layers=DEPTH, dtype=BF16, deps=(), slots=N_CHIP):
    r, c = a.shape[0] // layers, a.shape[1]
    tr = _flat_tile(r)

    def body(chip_ref, a_ref, *rest):
        rest[-1][...] = a_ref[...].astype(dtype)

    return pl.pallas_call(
        body, name=name, out_shape=_sds((slots, r, c), dtype),
        grid_spec=pltpu.PrefetchScalarGridSpec(
            num_scalar_prefetch=1, grid=(r // tr,),
            in_specs=[pl.BlockSpec((tr, c), lambda i, ch: (l * (r // tr) + i, 0))]
            + [pl.BlockSpec(d.shape, lambda i, ch: (0, 0)) for d in deps],
            out_specs=pl.BlockSpec((None, tr, c), lambda i, ch: (ch[0], i, 0))),
        compiler_params=_params(("parallel",)))(chip, a, *deps)


def _own_slot(name, g, chip):
    n, r, c = g.shape
    tr = _flat_tile(r)

    def body(chip_ref, g_ref, o_ref):
        o_ref[...] = g_ref[...]

    spec = pl.BlockSpec((None, tr, c), lambda i, ch: (ch[0], i, 0))
    return pl.pallas_call(
        body, name=name, out_shape=_sds(g.shape, g.dtype),
        grid_spec=pltpu.PrefetchScalarGridSpec(num_scalar_prefetch=1, grid=(r // tr,), in_specs=[spec], out_specs=spec),
        compiler_params=_params(("parallel",)))(chip, g)


def _sum_layers(name, lands):
    n, r, c = lands[0].shape
    tr = _flat_tile(r)
    nl = len(lands)

    def body(*refs):
        o_ref = refs[-1]
        for k in range(nl):
            @pl.when(pl.program_id(0) == k)
            def _():
                tot = refs[k][0].astype(F32)
                for i in range(1, n):
                    tot = tot + refs[k][i].astype(F32)
                o_ref[...] = tot

    return pl.pallas_call(
        body, name=name, grid=(nl, r // tr),
        in_specs=[pl.BlockSpec((n, tr, c), lambda l, i, k=k: (0, jnp.where(l == k, i, 0), 0)) for k in range(nl)],
        out_specs=pl.BlockSpec((tr, c), lambda l, i: (l * (r // tr) + i, 0)), out_shape=_sds((nl * r, c), F32),
        compiler_params=_params(("arbitrary", "arbitrary")))(*lands)


def _sum_slots(name, a, scale=None):
    n, r, c = a.shape
    tr = _flat_tile(r)

    def body(a_ref, o_ref):
        tot = a_ref[0].astype(F32)
        for i in range(1, n):
            tot = tot + a_ref[i].astype(F32)
        o_ref[...] = tot

    return pl.pallas_call(body, name=name, grid=(r // tr,),
                          in_specs=[pl.BlockSpec((n, tr, c), lambda i: (0, i, 0))],
                          out_specs=pl.BlockSpec((tr, c), lambda i: (i, 0)), out_shape=_sds((r, c), F32),
                          compiler_params=_params(("parallel",)))(a)


def _adamw(name, w, ga, gb, m, v):
    r, c = w.shape
    tr = _flat_tile(r)
    c1 = 1.0 - B1 ** STEP
    c2 = 1.0 - B2 ** STEP
    two = gb is not None

    def body(*refs):
        if two:
            w_ref, ga_ref, gb_ref, m_ref, v_ref, g_out, d_out, m_out, v_out = refs
            g = ga_ref[...] + gb_ref[...]
        else:
            w_ref, ga_ref, m_ref, v_ref, g_out, d_out, m_out, v_out = refs
            g = ga_ref[...]
        mn = B1 * m_ref[...] + (1.0 - B1) * g
        vn = B2 * v_ref[...] + (1.0 - B2) * (g * g)
        g_out[...] = g
        m_out[...] = mn
        v_out[...] = vn
        d_out[...] = -LR * ((mn / c1) / (jnp.sqrt(vn / c2) + ADAM_EPS) + WD * w_ref[...])

    spec = pl.BlockSpec((tr, c), lambda i: (i, 0))
    ins = [w, ga, gb, m, v] if two else [w, ga, m, v]
    return pl.pallas_call(body, name=name, grid=(r // tr,), in_specs=[spec] * len(ins), out_specs=[spec] * 4,
                          out_shape=[_sds((r, c), F32)] * 4, compiler_params=_params(("parallel",)))(*ins)


WEIGHTS = ('mix_norm_w', 'w_in', 'q_norm_w', 'k_norm_w', 'hgrn_lb_logits', 'hgrn_gnorm_w', 'conv_dw_w', 'conv_dw_b',
           'conv_ln_w', 'conv_ln_b', 'conv_pw_w', 'conv_pw_b', 'attn_out_norm_w', 'conv_out_norm_w', 'w_out',
           'ffn_norm_w', 'w_gate', 'w_up', 'w_down')
SHARDED_SMALL = {"hgrn_lb_logits": 2, "conv_dw_w": 2, "conv_pw_w": 1}
LANES = 128
PACK_ROWS = 256


def _pack(parts):
    flat = jnp.concatenate([p.reshape(-1) for p in parts])
    n = flat.shape[0]
    rows = -(-n // (PACK_ROWS * LANES)) * PACK_ROWS
    return jnp.pad(flat, (0, rows * LANES - n)).reshape(rows, LANES)


def _unpack(packed, shapes):
    flat = packed.reshape(-1)
    out, off = [], 0
    for shp in shapes:
        n = int(np.prod(shp))
        out.append(flat[off:off + n].reshape(shp))
        off += n
    return out


def kernel(x, mix_norm_w, w_in, q_norm_w, k_norm_w, hgrn_lb_logits, hgrn_gnorm_w, conv_dw_w, conv_dw_b, conv_ln_w, conv_ln_b, conv_pw_w, conv_pw_b, attn_out_norm_w, conv_out_norm_w, w_out, ffn_norm_w, w_gate, w_up, w_down, loss_target, m_mix_norm_w, m_w_in, m_q_norm_w, m_k_norm_w, m_hgrn_lb_logits, m_hgrn_gnorm_w, m_conv_dw_w, m_conv_dw_b, m_conv_ln_w, m_conv_ln_b, m_conv_pw_w, m_conv_pw_b, m_attn_out_norm_w, m_conv_out_norm_w, m_w_out, m_ffn_norm_w, m_w_gate, m_w_up, m_w_down, v_mix_norm_w, v_w_in, v_q_norm_w, v_k_norm_w, v_hgrn_lb_logits, v_hgrn_gnorm_w, v_conv_dw_w, v_conv_dw_b, v_conv_ln_w, v_conv_ln_b, v_conv_pw_w, v_conv_pw_b, v_attn_out_norm_w, v_conv_out_norm_w, v_w_out, v_ffn_norm_w, v_w_gate, v_w_up, v_w_down):
    w = dict(mix_norm_w=mix_norm_w, w_in=w_in, q_norm_w=q_norm_w, k_norm_w=k_norm_w, hgrn_lb_logits=hgrn_lb_logits,
             hgrn_gnorm_w=hgrn_gnorm_w, conv_dw_w=conv_dw_w, conv_dw_b=conv_dw_b, conv_ln_w=conv_ln_w,
             conv_ln_b=conv_ln_b, conv_pw_w=conv_pw_w, conv_pw_b=conv_pw_b, attn_out_norm_w=attn_out_norm_w,
             conv_out_norm_w=conv_out_norm_w, w_out=w_out, ffn_norm_w=ffn_norm_w, w_gate=w_gate, w_up=w_up,
             w_down=w_down)
    m = dict(mix_norm_w=m_mix_norm_w, w_in=m_w_in, q_norm_w=m_q_norm_w, k_norm_w=m_k_norm_w,
             hgrn_lb_logits=m_hgrn_lb_logits, hgrn_gnorm_w=m_hgrn_gnorm_w, conv_dw_w=m_conv_dw_w,
             conv_dw_b=m_conv_dw_b, conv_ln_w=m_conv_ln_w, conv_ln_b=m_conv_ln_b, conv_pw_w=m_conv_pw_w,
             conv_pw_b=m_conv_pw_b, attn_out_norm_w=m_attn_out_norm_w, conv_out_norm_w=m_conv_out_norm_w,
             w_out=m_w_out, ffn_norm_w=m_ffn_norm_w, w_gate=m_w_gate, w_up=m_w_up, w_down=m_w_down)
    v = dict(mix_norm_w=v_mix_norm_w, w_in=v_w_in, q_norm_w=v_q_norm_w, k_norm_w=v_k_norm_w,
             hgrn_lb_logits=v_hgrn_lb_logits, hgrn_gnorm_w=v_hgrn_gnorm_w, conv_dw_w=v_conv_dw_w,
             conv_dw_b=v_conv_dw_b, conv_ln_w=v_conv_ln_w, conv_ln_b=v_conv_ln_b, conv_pw_w=v_conv_pw_w,
             conv_pw_b=v_conv_pw_b, attn_out_norm_w=v_attn_out_norm_w, conv_out_norm_w=v_conv_out_norm_w,
             w_out=v_w_out, ffn_norm_w=v_ffn_norm_w, w_gate=v_w_gate, w_up=v_w_up, w_down=v_w_down)
    chip = 2 * lax.axis_index("x") + lax.axis_index("y")

    chip1 = chip.reshape(1).astype(jnp.int32)

    flat2 = lambda a: a.reshape(-1, a.shape[-1])
    groups = [[(l, "w_in")] if first else [(l, n) for n in BIG_ORDER[1:]] for l in range(DEPTH) for first in (1, 0)]
    group_of = {key: g for g, keys in enumerate(groups) for key in keys}
    groups[0].append((0, "small"))
    starts = []
    for g, keys in enumerate(groups):
        after_prev = [starts[-1]["token"]] if starts else []
        slots = [_cast_slot("cast_small", _pack([w[k] for k in SHARDED_SMALL]), 0, chip1, 1, F32, after_prev)
                 if n == "small" else _cast_slot("cast_%s_l%d" % (n, l), flat2(w[n]), l, chip1, deps=after_prev)
                 for l, n in keys]
        starts.append(_split_start("gather_start_g%d" % g, [], slots, False))
    got = {}

    def wget(l, name, after):
        if (l, name) not in got:
            g = group_of[(l, name)]
            for key, arr in zip(groups[g], _split_wait("gather_wait_g%d" % g, starts[g], after)):
                got[key] = arr
        return got[(l, name)]

    pending = []

    def on_grads(l, grads):
        names = [n for n in SCATTER_ORDER if n in grads]
        own = [_own_slot("own_%s_l%d" % (n, l), grads[n], chip1) for n in names]
        st = _split_start("scatter_start_l%d_%s" % (l, names[0]), [grads[n] for n in names], own, True)
        pending.append((l, names, st))
        return [st["token"]]

    wget(0, "w_in", starts[-1]["token"])
    gathered_small = got[(0, "small")]
    parts = [_unpack(gathered_small[j], [w[n].shape for n in SHARDED_SMALL]) for j in range(N_CHIP)]
    full_small = {n: jnp.concatenate([parts[j][i] for j in range(N_CHIP)], axis=ax)
                  for i, (n, ax) in enumerate(SHARDED_SMALL.items())}
    sm = {n: w[n] for n in WEIGHTS if n not in BIG_ORDER and n not in SHARDED_SMALL}
    sm["conv_dw_w"] = full_small["conv_dw_w"]
    sm["conv_pw_w"] = full_small["conv_pw_w"]
    logits = full_small["hgrn_lb_logits"].reshape(DEPTH * 2, D_HGRN)
    sm["lb"] = _lower_bounds(logits).reshape(DEPTH, 2, D_HGRN)

    sq_sum, grad_x, smalls = _local_step(x, loss_target, wget, sm, [st["token"] for st in starts], on_grads)
    loss = lax.psum(0.5 * sq_sum / D_MODEL, ("x", "y", "c"))

    dev1 = (4 * lax.axis_index("x") + 2 * lax.axis_index("y") + lax.axis_index("c")).reshape(1).astype(jnp.int32)
    g_pack = _pack([jnp.stack([smalls[l][n] for l in range(DEPTH)]) for n in SMALL_ORDER])
    small_start = _split_start("small_grads_start", [],
                               [_cast_slot("small_grads_slot", g_pack, 0, dev1, 1, F32, slots=N_DEV)], "all")
    landed = {}
    for l, names, st in pending:
        for n, arr in zip(names, _split_wait("scatter_wait_l%d_%s" % (l, names[0]), st, small_start["token"])):
            landed[(l, n)] = arr
    sums = [_sum_layers("sum_" + n, [landed[(l, n)] for l in range(DEPTH)]) for n in SCATTER_ORDER]
    sib_start = _split_start("sibling_start", sums, [lax.empty(a.shape, a.dtype) for a in sums], "sibling")
    out = {}

    small_names = [n for n in WEIGHTS if n not in SCATTER_ORDER]
    g_all = _split_wait("small_grads_wait", small_start, sib_start["token"])[0]
    g_tot = _sum_slots("sum_small", g_all)
    shapes = [(DEPTH,) + tuple(smalls[0][n].shape) for n in SMALL_ORDER]
    g_small = dict(zip(SMALL_ORDER, _unpack(g_tot, shapes)))
    lb_shard = lax.dynamic_slice_in_dim(g_small.pop("lb").reshape(DEPTH * 2, D_HGRN), chip * HEAD_DIM, HEAD_DIM, 1)
    g_small["hgrn_lb_logits"] = _lower_bounds_bwd(hgrn_lb_logits.reshape(DEPTH * 2, HEAD_DIM), lb_shard).reshape(
        hgrn_lb_logits.shape)
    g_small["conv_dw_w"] = lax.dynamic_slice_in_dim(g_small["conv_dw_w"], chip * HEAD_DIM, HEAD_DIM, 2)
    res = _adamw("adamw_small", _pack([w[n] for n in small_names]), _pack([g_small[n] for n in small_names]), None,
                 _pack([m[n] for n in small_names]), _pack([v[n] for n in small_names]))
    unpacked = [_unpack(r, [w[n].shape for n in small_names]) for r in res]
    for i, n in enumerate(small_names):
        out[n] = [unpacked[k][i] for k in range(4)]
    own, sib = _split_wait("sibling_wait", sib_start, res[0], with_srcs=True)
    for n, ga, gb in zip(SCATTER_ORDER, own, sib):
        big = _adamw("adamw_" + n, flat2(w[n]), ga, gb, flat2(m[n]), flat2(v[n]))
        out[n] = [r.reshape(w[n].shape) for r in big]

    return (loss, grad_x, *[out[n][0] for n in WEIGHTS], *[out[n][1] for n in WEIGHTS],
            *[out[n][2] for n in WEIGHTS], *[out[n][3] for n in WEIGHTS])
```

```python
import functools

import numpy as np
import jax
import jax.numpy as jnp
from jax import lax
from jax.experimental import pallas as pl
from jax.experimental.pallas import tpu as pltpu

F32, BF16 = jnp.float32, jnp.bfloat16

D_MODEL = 1024
DEPTH = 2
GRID_W = 64
D_ATTN, D_HGRN, D_CONV = 512, 256, 256
HEAD_DIM = 64
N_KV = 2
KV_LANES = D_ATTN // N_KV
ROPE_THETA = 10000.0
F_MIN = 1e-6
CONV_W = 31
CONV_PAD = 15
D_FF = 2816
D_IN = 2560
N_CHIP = 4
N_DEV = 8
IN_BLK = D_IN // N_CHIP
FF_BLK = D_FF // N_CHIP
OUT_BLK = D_MODEL // N_CHIP
EPS = 1e-6
LN_EPS = 1e-5
LR, B1, B2, ADAM_EPS, WD, STEP = 0.001, 0.9, 0.999, 1e-08, 0.01, 10
CHUNK = 16
HBLK = 256
CONV_TILE = 128
ATTN_FWD_ROWS = 512
ATTN_BWD_ROWS = 256
ATTN_BWD_STACK = 4
BWD_GROUP = 2
VMEM_LIMIT = 56 * 1024 * 1024

COL_Q, COL_K, COL_V = 0, 4, 5
COL_HQ, COL_FF, COL_FB, COL_HI, COL_HG, COL_CA, COL_CB = 3, 4, 5, 6, 7, 8, 9


def _params(sem=None):
    return pltpu.CompilerParams(dimension_semantics=sem, vmem_limit_bytes=VMEM_LIMIT)


def _sds(shape, dtype):
    return jax.ShapeDtypeStruct(tuple(shape), dtype)


def _full(shape):
    n = len(shape)
    return pl.BlockSpec(tuple(shape), lambda *_: (0,) * n)


def _sigmoid(x):
    return 0.5 * jnp.tanh(0.5 * x) + 0.5


def _gate_sigmoid(x):
    return 1.0 / (1.0 + jnp.exp(-x))


def _silu(x):
    return x * _sigmoid(x)


def _dsilu(x):
    s = _sigmoid(x)
    return s * (1.0 + x * (1.0 - s))


def _rowgroups(v):
    m, c = v.shape
    return v.reshape(m // 8, 8, c).sum(axis=0)


def _split2(x):
    hi = x.astype(BF16)
    lo = (x - hi.astype(F32)).astype(BF16)
    return hi, lo


def _rdot2(x, m):
    hi, lo = _split2(x)
    return (jnp.dot(hi, m, preferred_element_type=F32) + jnp.dot(lo, m, preferred_element_type=F32))


def _ldot3(m, x):
    hi = x.astype(BF16)
    r1 = x - hi.astype(F32)
    mid = r1.astype(BF16)
    lo = (r1 - mid.astype(F32)).astype(BF16)
    return (jnp.dot(m, hi, preferred_element_type=F32) + jnp.dot(m, mid, preferred_element_type=F32)
            + jnp.dot(m, lo, preferred_element_type=F32))


def _dot_nt(a, b):
    return lax.dot_general(a, b, (((1,), (1,)), ((), ())), preferred_element_type=F32)


def _dot_tn(a, b):
    return lax.dot_general(a, b, (((0,), (0,)), ((), ())), preferred_element_type=F32)


def _seg_matrix(n, seg, val):
    i = np.arange(n)
    return ((i[:, None] // seg) == (i[None, :] // seg)).astype(np.float32) * val


def _rot_matrix(n):
    r = np.zeros((n, n), np.float32)
    for i in range(n):
        if (i % 32) < 16:
            r[i + 16, i] = -1.0
        else:
            r[i - 16, i] = 1.0
    return r


def _rep_matrix():
    r = np.zeros((N_KV * HEAD_DIM, D_ATTN), np.float32)
    for kv in range(N_KV):
        for g in range(KV_LANES // HEAD_DIM):
            for d in range(HEAD_DIM):
                r[HEAD_DIM * kv + d, KV_LANES * kv + HEAD_DIM * g + d] = 1.0
    return r


def _cumsum_matrix(rev):
    i = np.arange(HBLK)
    same = (i[:, None] // CHUNK) == (i[None, :] // CHUNK)
    tri = (i[None, :] >= i[:, None]) if rev else (i[None, :] <= i[:, None])
    return (same & tri).astype(np.float32)


def _sel_matrices():
    sel = np.zeros((CHUNK, CHUNK * CHUNK), np.float32)
    selt = np.zeros((CHUNK, CHUNK * CHUNK), np.float32)
    for t in range(CHUNK):
        for s in range(CHUNK):
            sel[t, t * CHUNK + s] = 1.0
            selt[s, t * CHUNK + s] = 1.0
    return sel, selt


def _bf(a):
    return jnp.asarray(a, dtype=BF16)


def _mm(name, grid, pairs, extras, outs, epilogue, acc=None, sem=None):
    n_p, n_e, n_o = len(pairs), len(extras), len(outs)

    def body(*refs):
        ab = refs[:2 * n_p]
        ex = refs[2 * n_p:2 * n_p + n_e]
        out = refs[2 * n_p + n_e:2 * n_p + n_e + n_o]
        scr = refs[2 * n_p + n_e + n_o:]
        tot = None
        for i in range(n_p):
            a = ab[2 * i][...]
            b = ab[2 * i + 1][...]
            if a.ndim == 3:
                a = a.reshape(-1, a.shape[-1])
            if b.ndim == 3:
                b = b.reshape(-1, b.shape[-1])
            r = lax.dot_general(a.astype(BF16), b.astype(BF16), pairs[i][4], preferred_element_type=F32)
            tot = r if tot is None else tot + r

        def finish(total):
            res = epilogue(total, *[e[...] for e in ex])
            for o_ref, val in zip(out, res):
                o_ref[...] = val.astype(o_ref.dtype)

        if acc is None:
            finish(tot)
        else:
            k = pl.program_id(acc[0])

            @pl.when(k == 0)
            def _():
                scr[0][...] = tot

            @pl.when(k > 0)
            def _():
                scr[0][...] += tot

            @pl.when(k == grid[acc[0]] - 1)
            def _():
                finish(scr[0][...])

    args, in_specs = [], []
    for a, a_spec, b, b_spec, _ in pairs:
        args += [a, b]
        in_specs += [a_spec, b_spec]
    for e, e_spec in extras:
        args.append(e)
        in_specs.append(e_spec)
    if sem is None:
        sem = tuple("arbitrary" if (acc is not None and i == acc[0]) else "parallel" for i in range(len(grid)))
    return pl.pallas_call(
        body, name=name, grid=grid, in_specs=in_specs,
        out_specs=[o[1] for o in outs], out_shape=[o[0] for o in outs],
        scratch_shapes=[] if acc is None else [pltpu.VMEM(acc[1], F32)],
        compiler_params=_params(sem),
    )(*args)


NN = (((1,), (0,)), ((), ()))
NT = (((1,), (1,)), ((), ()))
TN = (((0,), (0,)), ((), ()))


def _row_tile(t):
    return min(256, t)


def _rms_fwd(name, x, w, deps=()):
    t, d = x.shape
    tm = _row_tile(t)

    def body(x_ref, w_ref, *rest):
        o_ref = rest[-1]
        xv = x_ref[...]
        r = lax.rsqrt(jnp.mean(xv * xv, axis=-1, keepdims=True) + EPS)
        o_ref[...] = (xv * r * w_ref[...]).astype(BF16)

    return pl.pallas_call(
        body, name=name, grid=(t // tm,),
        in_specs=[pl.BlockSpec((tm, d), lambda i: (i, 0)), _full((1, d))] + [_full(a.shape) for a in deps],
        out_specs=pl.BlockSpec((tm, d), lambda i: (i, 0)), out_shape=_sds((t, d), BF16),
        compiler_params=_params(("parallel",)),
    )(x, w, *deps)


def _rms_bwd(name, x, w, dh, dres, deps=()):
    t, d = x.shape
    tm = _row_tile(t)

    def body(x_ref, w_ref, dh_ref, dres_ref, *rest):
        dx_ref, dw_ref = rest[-2:]
        xv = x_ref[...]
        r = lax.rsqrt(jnp.mean(xv * xv, axis=-1, keepdims=True) + EPS)
        dy = dh_ref[...]
        gw = dy * w_ref[...]
        dx_ref[...] = dres_ref[...] + r * gw - xv * (r * r * r) * jnp.mean(gw * xv, axis=-1, keepdims=True)

        @pl.when(pl.program_id(0) == 0)
        def _():
            dw_ref[...] = jnp.zeros_like(dw_ref)

        dw_ref[...] += _rowgroups(dy * xv * r)

    tile = pl.BlockSpec((tm, d), lambda i: (i, 0))
    return pl.pallas_call(
        body, name=name, grid=(t // tm,),
        in_specs=[tile, _full((1, d)), tile, tile] + [_full(a.shape) for a in deps],
        out_specs=[tile, _full((8, d))], out_shape=[_sds((t, d), F32), _sds((8, d), F32)],
        compiler_params=_params(("arbitrary",)),
    )(x, w, dh, dres, *deps)


def _loss_kernel(y, target):
    t, d = y.shape
    tm = _row_tile(t)

    def body(y_ref, t_ref, dy_ref, acc_ref):
        e = y_ref[...] - t_ref[...]
        dy_ref[...] = e * (1.0 / d)

        @pl.when(pl.program_id(0) == 0)
        def _():
            acc_ref[...] = jnp.zeros_like(acc_ref)

        acc_ref[...] += _rowgroups(e * e)

    tile = pl.BlockSpec((tm, d), lambda i: (i, 0))
    return pl.pallas_call(
        body, name="loss_head", grid=(t // tm,), in_specs=[tile, tile],
        out_specs=[tile, _full((8, d))], out_shape=[_sds((t, d), F32), _sds((8, d), F32)],
        compiler_params=_params(("arbitrary",)),
    )(y, target)


def _rope_tables(s):
    rows = s // GRID_W
    row_id = jnp.repeat(jnp.arange(rows, dtype=F32), GRID_W)
    col_id = jnp.tile(jnp.arange(GRID_W, dtype=F32), rows)
    half = HEAD_DIM // 2
    inv_freq = ROPE_THETA ** (-jnp.arange(0, half, 2, dtype=F32) / half)
    ang_r = row_id[:, None] * inv_freq[None, :]
    ang_c = col_id[:, None] * inv_freq[None, :]
    ang = jnp.concatenate([ang_r, ang_r, ang_c, ang_c], axis=-1)
    return jnp.cos(ang).astype(F32), jnp.sin(ang).astype(F32)


def _attn_consts():
    return dict(
        seg_q=_bf(_seg_matrix(D_ATTN, HEAD_DIM, 1.0 / HEAD_DIM)),
        seg_k=_bf(_seg_matrix(N_KV * HEAD_DIM, HEAD_DIM, 1.0 / HEAD_DIM)),
        rot_q=_bf(_rot_matrix(D_ATTN)), rot_k=_bf(_rot_matrix(N_KV * HEAD_DIM)),
        rep=_bf(_rep_matrix()), rep_t=_bf(_rep_matrix().T))


def _attn_prep(name, proj, s, tabs, qw, kw, ac):
    t = proj.shape[0]
    tm = _row_tile(s)
    nst = s // tm
    kw_ = N_KV * HEAD_DIM

    def body(q_ref, k_ref, v_ref, cq_ref, sq_ref, ck_ref, sk_ref, qw_ref, kw_ref,
             segq_ref, segk_ref, rotq_ref, rotk_ref, rep_ref, qn_ref, kr_ref, vr_ref):
        q = q_ref[...]
        r = lax.rsqrt(jnp.dot((q * q).astype(BF16), segq_ref[...], preferred_element_type=F32) + EPS)
        qn = q * r * qw_ref[...]
        qr = qn * cq_ref[...] + _rdot2(qn, rotq_ref[...]) * sq_ref[...]
        qn_ref[...] = (qr * (HEAD_DIM ** -0.5)).astype(BF16)
        k = k_ref[...]
        rk = lax.rsqrt(jnp.dot((k * k).astype(BF16), segk_ref[...], preferred_element_type=F32) + EPS)
        kn = k * rk * kw_ref[...]
        kr = kn * ck_ref[...] + _rdot2(kn, rotk_ref[...]) * sk_ref[...]
        kr_ref[...] = jnp.dot(kr.astype(BF16), rep_ref[...], preferred_element_type=F32).astype(BF16)
        vr_ref[...] = jnp.dot(v_ref[...].astype(BF16), rep_ref[...], preferred_element_type=F32).astype(BF16)

    wide = pl.BlockSpec((tm, D_ATTN), lambda i: (i, 0))
    tabq = pl.BlockSpec((tm, D_ATTN), lambda i: (i % nst, 0))
    tabk = pl.BlockSpec((tm, kw_), lambda i: (i % nst, 0))
    return pl.pallas_call(
        body, name=name, grid=(t // tm,),
        in_specs=[pl.BlockSpec((tm, D_ATTN), lambda i: (i, COL_Q)), pl.BlockSpec((tm, kw_), lambda i: (i, COL_K)),
                  pl.BlockSpec((tm, kw_), lambda i: (i, COL_V)), tabq, tabq, tabk, tabk,
                  _full((1, D_ATTN)), _full((1, kw_)), _full((D_ATTN, D_ATTN)), _full((kw_, kw_)),
                  _full((D_ATTN, D_ATTN)), _full((kw_, kw_)), _full((kw_, D_ATTN))],
        out_specs=[wide, wide, wide], out_shape=[_sds((t, D_ATTN), BF16)] * 3,
        compiler_params=_params(("parallel",)),
    )(proj, proj, proj, tabs["cq"], tabs["sq"], tabs["ck"], tabs["sk"], qw, kw,
      ac["seg_q"], ac["seg_k"], ac["rot_q"], ac["rot_k"], ac["rep"])


def _attn_prep_bwd(name, proj, s, tabs, qw, kw, ac, dqs, dkr, dvr):
    t = proj.shape[0]
    tm = _row_tile(s)
    nst = s // tm
    kw_ = N_KV * HEAD_DIM
    wout = D_ATTN + 2 * kw_

    def norm_rope_bwd(x, w, cos, sin, seg, rot, d_roped):
        dn = d_roped * cos - _rdot2(d_roped * sin, rot)
        r = lax.rsqrt(jnp.dot((x * x).astype(BF16), seg, preferred_element_type=F32) + EPS)
        gw = dn * w
        dx = r * gw - x * (r * r * r) * _rdot2(gw * x, seg)
        return dx, _rowgroups(dn * x * r)

    def body(q_ref, k_ref, cq_ref, sq_ref, ck_ref, sk_ref, qw_ref, kw_ref, segq_ref, segk_ref, rotq_ref, rotk_ref,
             rept_ref, dqs_ref, dkr_ref, dvr_ref, dp_ref, dqw_ref, dkw_ref):
        dq, dqw = norm_rope_bwd(q_ref[...], qw_ref[...], cq_ref[...], sq_ref[...], segq_ref[...], rotq_ref[...],
                                dqs_ref[...] * (HEAD_DIM ** -0.5))
        dk_roped = jnp.dot(dkr_ref[...].astype(BF16), rept_ref[...], preferred_element_type=F32)
        dk, dkw = norm_rope_bwd(k_ref[...], kw_ref[...], ck_ref[...], sk_ref[...], segk_ref[...], rotk_ref[...],
                                dk_roped)
        dv = jnp.dot(dvr_ref[...].astype(BF16), rept_ref[...], preferred_element_type=F32)
        dp_ref[:, 0:D_ATTN] = dq.astype(BF16)
        dp_ref[:, D_ATTN:D_ATTN + kw_] = dk.astype(BF16)
        dp_ref[:, D_ATTN + kw_:wout] = dv.astype(BF16)

        @pl.when(pl.program_id(0) == 0)
        def _():
            dqw_ref[...] = jnp.zeros_like(dqw_ref)
            dkw_ref[...] = jnp.zeros_like(dkw_ref)

        dqw_ref[...] += dqw
        dkw_ref[...] += dkw

    wide = pl.BlockSpec((tm, D_ATTN), lambda i: (i, 0))
    tabq = pl.BlockSpec((tm, D_ATTN), lambda i: (i % nst, 0))
    tabk = pl.BlockSpec((tm, kw_), lambda i: (i % nst, 0))
    return pl.pallas_call(
        body, name=name, grid=(t // tm,),
        in_specs=[pl.BlockSpec((tm, D_ATTN), lambda i: (i, COL_Q)), pl.BlockSpec((tm, kw_), lambda i: (i, COL_K)),
                  tabq, tabq, tabk, tabk, _full((1, D_ATTN)), _full((1, kw_)),
                  _full((D_ATTN, D_ATTN)), _full((kw_, kw_)), _full((D_ATTN, D_ATTN)), _full((kw_, kw_)),
                  _full((D_ATTN, kw_)), wide, wide, wide],
        out_specs=[pl.BlockSpec((tm, wout), lambda i: (i, 0)), _full((8, D_ATTN)), _full((8, kw_))],
        out_shape=[_sds((t, wout), BF16), _sds((8, D_ATTN), F32), _sds((8, kw_), F32)],
        compiler_params=_params(("arbitrary",)),
    )(proj, proj, tabs["cq"], tabs["sq"], tabs["ck"], tabs["sk"], qw, kw,
      ac["seg_q"], ac["seg_k"], ac["rot_q"], ac["rot_k"], ac["rep_t"], dqs, dkr, dvr)


def _attn_tile(s, rows=256):
    return min(rows, s)


def _head_masks(shape):
    lane = lax.broadcasted_iota(jnp.int32, shape, 1)
    return [(lane // HEAD_DIM) == g for g in range(KV_LANES // HEAD_DIM)]


def _attn_fwd(name, qn, kr, vr, b, s):
    t = qn.shape[0]
    tq = _attn_tile(s, ATTN_FWD_ROWS)
    nq = s // tq

    def body(q_ref, k_ref, v_ref, o_ref):
        q = q_ref[...]
        k = k_ref[...]
        v = v_ref[...]
        acc = jnp.zeros((tq, KV_LANES), F32)
        for mask in _head_masks((tq, KV_LANES)):
            sc = _dot_nt(jnp.where(mask, q, jnp.zeros_like(q)), k)
            p = jnp.exp(sc - jnp.max(sc, axis=-1, keepdims=True))
            inv = 1.0 / jnp.sum(p, axis=-1, keepdims=True)
            og = jnp.dot(p.astype(BF16), v, preferred_element_type=F32) * inv
            acc = jnp.where(mask, og, acc)
        o_ref[...] = acc

    return pl.pallas_call(
        body, name=name, grid=(b, N_KV, nq),
        in_specs=[pl.BlockSpec((tq, KV_LANES), lambda bi, kv, i: (bi * nq + i, kv)),
                  pl.BlockSpec((s, KV_LANES), lambda bi, kv, i: (bi, kv)),
                  pl.BlockSpec((s, KV_LANES), lambda bi, kv, i: (bi, kv))],
        out_specs=pl.BlockSpec((tq, KV_LANES), lambda bi, kv, i: (bi * nq + i, kv)),
        out_shape=_sds((t, D_ATTN), F32),
        compiler_params=_params(("parallel", "parallel", "parallel")),
    )(qn, kr, vr)


def _attn_bwd(name, qn, kr, vr, do, b, s):
    t = qn.shape[0]
    tq = _attn_tile(s, ATTN_BWD_ROWS)
    nq = s // tq

    def body(q_ref, k_ref, v_ref, do_ref, dq_ref, dk_ref, dv_ref):
        @pl.when(pl.program_id(2) == 0)
        def _():
            dk_ref[...] = jnp.zeros_like(dk_ref)
            dv_ref[...] = jnp.zeros_like(dv_ref)

        q = q_ref[...]
        k = k_ref[...]
        v = v_ref[...]
        dout = do_ref[...].astype(BF16)
        all_masks = _head_masks((tq, KV_LANES))
        dq = jnp.zeros((tq, KV_LANES), F32)
        for g0 in range(0, len(all_masks), ATTN_BWD_STACK):
            masks = all_masks[g0:g0 + ATTN_BWD_STACK]
            q4 = jnp.concatenate([jnp.where(m, q, jnp.zeros_like(q)) for m in masks], axis=0)
            do4 = jnp.concatenate([jnp.where(m, dout, jnp.zeros_like(dout)) for m in masks], axis=0)
            sc = _dot_nt(q4, k)
            p = jnp.exp(sc - jnp.max(sc, axis=-1, keepdims=True))
            p = p * (1.0 / jnp.sum(p, axis=-1, keepdims=True))
            dp = _dot_nt(do4, v)
            ds = (p * (dp - jnp.sum(p * dp, axis=-1, keepdims=True))).astype(BF16)
            dq4 = jnp.dot(ds, k, preferred_element_type=F32)
            for g, m in enumerate(masks):
                dq = jnp.where(m, dq4[g * tq:(g + 1) * tq, :], dq)
            dk_ref[...] += _dot_tn(ds, q4)
            dv_ref[...] += _dot_tn(p.astype(BF16), do4)
        dq_ref[...] = dq

    qspec = pl.BlockSpec((tq, KV_LANES), lambda bi, kv, i: (bi * nq + i, kv))
    kspec = pl.BlockSpec((s, KV_LANES), lambda bi, kv, i: (bi, kv))
    return pl.pallas_call(
        body, name=name, grid=(b, N_KV, nq),
        in_specs=[qspec, kspec, kspec, qspec],
        out_specs=[qspec, kspec, kspec], out_shape=[_sds((t, D_ATTN), F32)] * 3,
        compiler_params=_params(("parallel", "parallel", "arbitrary")),
    )(qn, kr, vr, do)


def _hgrn_consts(rev):
    sel, selt = _sel_matrices()
    cs = _cumsum_matrix(rev)
    return dict(cs=_bf(cs), cs_t=_bf(cs.T), seg=_bf(_seg_matrix(D_HGRN, HEAD_DIM, 1.0)),
                bd=jnp.asarray(_seg_matrix(D_HGRN, HEAD_DIM, 1.0), F32),
                sel=_bf(sel), selt=_bf(selt), seld=_bf(sel - selt))


def _gates(z, lb):
    sig = _gate_sigmoid(z)
    f = lb + (1.0 - lb) * sig
    g = jnp.log(jnp.maximum(f, F_MIN))
    sn = _gate_sigmoid(-z)
    return sig, f, g, sn, (1.0 - lb) * sn


def _pair_decay(b, rev):
    row = lax.broadcasted_iota(jnp.int32, (CHUNK, D_HGRN), 0)
    parts = []
    for t in range(CHUNK):
        m = (row >= t) if rev else (row <= t)
        parts.append(jnp.where(m, jnp.exp(jnp.minimum(b[t:t + 1, :] - b, 0.0)), 0.0))
    return jnp.concatenate(parts, axis=0)


def _rows_rep(a):
    return jnp.concatenate([jnp.broadcast_to(a[t:t + 1, :], a.shape) for t in range(CHUNK)], axis=0)


def _tile_rows(a):
    return jnp.concatenate([a] * CHUNK, axis=0)


def _hgrn_specs(b, s, rev):
    nb = s // HBLK

    def blk(j):
        return (nb - 1 - j) if rev else j

    def col(c):
        return pl.BlockSpec((HBLK, D_HGRN), lambda bi, j: (bi * nb + blk(j), c))

    return nb, blk, col


def _hgrn_fwd(name, proj, lb, b, s, rev, hc):
    t = proj.shape[0]
    nb, blk, col = _hgrn_specs(b, s, rev)
    n_ch = HBLK // CHUNK
    last = 0 if rev else CHUNK - 1

    def body(q_ref, z_ref, v_ref, lb_ref, cs_ref, seg_ref, bd_ref, sel_ref, o_ref, st_ref, state, b_scr, k_scr):
        @pl.when(pl.program_id(1) == 0)
        def _():
            state[...] = jnp.zeros_like(state)

        st_ref[...] = state[...]
        _, _, g, _, kk = _gates(z_ref[...], lb_ref[...])
        k_scr[...] = kk
        b_scr[...] = _ldot3(cs_ref[...], g)

        def chunk(i, carry):
            c = (n_ch - 1 - i) if rev else i
            rows = pl.ds(pl.multiple_of(c * CHUNK, CHUNK), CHUNK)
            q = q_ref[rows, :]
            k = k_scr[rows, :]
            v = v_ref[rows, :]
            bb = b_scr[rows, :]
            bl = bb[last:last + 1, :]
            pairs = _pair_decay(bb, rev) * _rows_rep(q) * _tile_rows(k)
            a = jnp.dot(pairs.astype(BF16), seg_ref[...], preferred_element_type=F32)
            o_intra = jnp.dot(sel_ref[...], (a * _tile_rows(v)).astype(BF16), preferred_element_type=F32)
            st = state[...]
            o_inter = _dot_nt((q * jnp.exp(bb)).astype(BF16), st.astype(BF16))
            o_ref[rows, :] = o_intra + o_inter
            ke = k * jnp.exp(bl - bb)
            state[...] = st * jnp.exp(bl) + bd_ref[...] * _dot_tn(v.astype(BF16), ke.astype(BF16))
            return carry

        lax.fori_loop(0, n_ch, chunk, 0)

    sq = (D_HGRN, D_HGRN)
    return pl.pallas_call(
        body, name=name, grid=(b, nb),
        in_specs=[col(COL_HQ), col(COL_FB if rev else COL_FF), col(COL_HI), _full((1, D_HGRN)),
                  _full((HBLK, HBLK)), _full(sq), _full(sq), _full((CHUNK, CHUNK * CHUNK))],
        out_specs=[pl.BlockSpec((HBLK, D_HGRN), lambda bi, j: (bi * nb + blk(j), 0)),
                   pl.BlockSpec((None,) + sq, lambda bi, j: (bi * nb + blk(j), 0, 0))],
        out_shape=[_sds((t, D_HGRN), F32), _sds((b * nb,) + sq, F32)],
        scratch_shapes=[pltpu.VMEM(sq, F32), pltpu.VMEM((HBLK, D_HGRN), F32), pltpu.VMEM((HBLK, D_HGRN), F32)],
        compiler_params=_params(("parallel", "arbitrary")),
    )(proj, proj, proj, lb, hc["cs"], hc["seg"], hc["bd"], hc["sel"])


def _hgrn_bwd(name, proj, lb, st_blk, do, dq_prev, dv_prev, b, s, rev, hc):
    t = proj.shape[0]
    nb = s // HBLK
    n_ch = HBLK // CHUNK
    last = 0 if rev else CHUNK - 1

    def blk(j):
        return j if rev else (nb - 1 - j)

    def col(c):
        return pl.BlockSpec((HBLK, D_HGRN), lambda bi, j: (bi * nb + blk(j), c))

    def body(q_ref, z_ref, v_ref, lb_ref, st_ref, do_ref, dqp_ref, dvp_ref, cs_ref, cst_ref, seg_ref, bd_ref,
             sel_ref, selt_ref, seld_ref, dq_ref, dv_ref, dz_ref, dlb_ref,
             dstate, states, b_scr, k_scr, db_scr, dk_scr):
        first = jnp.logical_and(pl.program_id(0) == 0, pl.program_id(1) == 0)

        @pl.when(first)
        def _():
            dlb_ref[...] = jnp.zeros_like(dlb_ref)

        @pl.when(pl.program_id(1) == 0)
        def _():
            dstate[...] = jnp.zeros_like(dstate)

        lbv = lb_ref[...]
        z = z_ref[...]
        sig, f, g, sn, kk = _gates(z, lbv)
        k_scr[...] = kk
        b_scr[...] = _ldot3(cs_ref[...], g)

        def rows_of(c):
            return pl.ds(pl.multiple_of(c * CHUNK, CHUNK), CHUNK)

        def replay(i, st):
            c = (n_ch - 1 - i) if rev else i
            rows = rows_of(c)
            states[c] = st
            bb = b_scr[rows, :]
            bl = bb[last:last + 1, :]
            ke = k_scr[rows, :] * jnp.exp(bl - bb)
            return st * jnp.exp(bl) + bd_ref[...] * _dot_tn(v_ref[rows, :].astype(BF16), ke.astype(BF16))

        lax.fori_loop(0, n_ch, replay, st_ref[...])
        row = lax.broadcasted_iota(jnp.int32, (CHUNK, D_HGRN), 0)

        def chunk(i, carry):
            c = i if rev else (n_ch - 1 - i)
            rows = rows_of(c)
            q = q_ref[rows, :]
            k = k_scr[rows, :]
            v = v_ref[rows, :]
            bb = b_scr[rows, :]
            dout = do_ref[rows, :]
            bl = bb[last:last + 1, :]
            st_p = states[c]
            dst_n = dstate[...]
            eb = jnp.exp(bb)
            ebl = jnp.exp(bl - bb)
            ebl_last = jnp.exp(bl)
            qe = q * eb
            ke = k * ebl
            dob = dout.astype(BF16)
            dstb = dst_n.astype(BF16)
            dqe = jnp.dot(dob, st_p.astype(BF16), preferred_element_type=F32)
            dke = jnp.dot(v.astype(BF16), dstb, preferred_element_type=F32)
            dv = _dot_nt(ke.astype(BF16), dstb)
            dbl = jnp.sum(dst_n * st_p, axis=0, keepdims=True) * ebl_last + jnp.sum(dke * ke, axis=0, keepdims=True)
            dq = dqe * eb
            dk = dke * ebl
            db = dqe * qe - dke * ke
            dec = _pair_decay(bb, rev)
            q_rep = _rows_rep(q)
            k_til = _tile_rows(k)
            do_rep = _rows_rep(dout)
            pairs = dec * q_rep * k_til
            a = jnp.dot(pairs.astype(BF16), seg_ref[...], preferred_element_type=F32)
            wb = jnp.dot((_tile_rows(v) * do_rep).astype(BF16), seg_ref[...], preferred_element_type=F32)
            gdec = wb * dec
            dq = dq + jnp.dot(sel_ref[...], (gdec * k_til).astype(BF16), preferred_element_type=F32)
            dk = dk + jnp.dot(selt_ref[...], (gdec * q_rep).astype(BF16), preferred_element_type=F32)
            dv = dv + jnp.dot(selt_ref[...], (a * do_rep).astype(BF16), preferred_element_type=F32)
            db = db + jnp.dot(seld_ref[...], (wb * pairs).astype(BF16), preferred_element_type=F32)
            db = db + jnp.where(row == last, dbl, 0.0)
            dq_ref[rows, :] = dq + dqp_ref[rows, :]
            dv_ref[rows, :] = dv + dvp_ref[rows, :]
            dk_scr[rows, :] = dk
            db_scr[rows, :] = db
            dstate[...] = dst_n * ebl_last + bd_ref[...] * _dot_tn(dob, qe.astype(BF16))
            return carry

        lax.fori_loop(0, n_ch, chunk, 0)
        hi, lo = _split2(db_scr[...])
        dg = (jnp.dot(cst_ref[...], hi, preferred_element_type=F32)
              + jnp.dot(cst_ref[...], lo, preferred_element_type=F32))
        dgf = jnp.where(f > F_MIN, dg / f, 0.0)
        dk = dk_scr[...]
        dz_ref[...] = dgf * (1.0 - lbv) * sig * (1.0 - sig) - dk * (1.0 - lbv) * sn * (1.0 - sn)
        dlb_ref[...] += _rowgroups(dgf * (1.0 - sig) - dk * sn)

    sq = (D_HGRN, D_HGRN)
    blk0 = pl.BlockSpec((HBLK, D_HGRN), lambda bi, j: (bi * nb + blk(j), 0))
    pairs_shape = (CHUNK, CHUNK * CHUNK)
    return pl.pallas_call(
        body, name=name, grid=(b, nb),
        in_specs=[col(COL_HQ), col(COL_FB if rev else COL_FF), col(COL_HI), _full((1, D_HGRN)),
                  pl.BlockSpec((None,) + sq, lambda bi, j: (bi * nb + blk(j), 0, 0)), blk0, blk0, blk0,
                  _full((HBLK, HBLK)), _full((HBLK, HBLK)), _full(sq), _full(sq),
                  _full(pairs_shape), _full(pairs_shape), _full(pairs_shape)],
        out_specs=[blk0, blk0, blk0, _full((8, D_HGRN))],
        out_shape=[_sds((t, D_HGRN), F32)] * 3 + [_sds((8, D_HGRN), F32)],
        scratch_shapes=[pltpu.VMEM(sq, F32), pltpu.VMEM((n_ch,) + sq, F32)] + [pltpu.VMEM((HBLK, D_HGRN), F32)] * 4,
        compiler_params=_params(("arbitrary", "arbitrary")),
    )(proj, proj, proj, lb, st_blk, do, dq_prev, dv_prev,
      hc["cs"], hc["cs_t"], hc["seg"], hc["bd"], hc["sel"], hc["selt"], hc["seld"])


def _scan_chunk_fwd(c, rev, q_ref, v_ref, k_scr, b_scr, state, o_ref, seg_ref, bd_ref, sel_ref):
    last = 0 if rev else CHUNK - 1
    rows = pl.ds(pl.multiple_of(c * CHUNK, CHUNK), CHUNK)
    q = q_ref[rows, :]
    k = k_scr[rows, :]
    v = v_ref[rows, :]
    bb = b_scr[rows, :]
    bl = bb[last:last + 1, :]
    pairs = _pair_decay(bb, rev) * _rows_rep(q) * _tile_rows(k)
    a = jnp.dot(pairs.astype(BF16), seg_ref[...], preferred_element_type=F32)
    o_intra = jnp.dot(sel_ref[...], (a * _tile_rows(v)).astype(BF16), preferred_element_type=F32)
    st = state[...]
    o_inter = _dot_nt((q * jnp.exp(bb)).astype(BF16), st.astype(BF16))
    o_ref[rows, :] = o_intra + o_inter
    ke = k * jnp.exp(bl - bb)
    state[...] = st * jnp.exp(bl) + bd_ref[...] * _dot_tn(v.astype(BF16), ke.astype(BF16))


def _scan_chunks_fwd(chains, seg_ref, bd_ref, sel_ref):
    work = []
    for c, rev, q_ref, v_ref, k_scr, b_scr, state, o_ref in chains:
        last = 0 if rev else CHUNK - 1
        rows = pl.ds(pl.multiple_of(c * CHUNK, CHUNK), CHUNK)
        q = q_ref[rows, :]
        k = k_scr[rows, :]
        v = v_ref[rows, :]
        bb = b_scr[rows, :]
        bl = bb[last:last + 1, :]
        st = state[...]
        work.append(dict(
            rows=rows, v=v, st=st, state=state, o_ref=o_ref, decay=jnp.exp(bl),
            pairs=(_pair_decay(bb, rev) * _rows_rep(q) * _tile_rows(k)).astype(BF16),
            qe=(q * jnp.exp(bb)).astype(BF16), ke=(k * jnp.exp(bl - bb)).astype(BF16), st_b=st.astype(BF16)))
    for w in work:
        w["a"] = jnp.dot(w["pairs"], seg_ref[...], preferred_element_type=F32)
        w["o_inter"] = _dot_nt(w["qe"], w["st_b"])
        w["upd"] = _dot_tn(w["v"].astype(BF16), w["ke"])
    for w in work:
        w["av"] = (w["a"] * _tile_rows(w["v"])).astype(BF16)
    for w in work:
        w["o_ref"][w["rows"], :] = jnp.dot(sel_ref[...], w["av"], preferred_element_type=F32) + w["o_inter"]
        w["state"][...] = w["st"] * w["decay"] + bd_ref[...] * w["upd"]


def _hgrn_fwd2(name, proj, lb_f, lb_b, b, s, hc_f, hc_b):
    t = proj.shape[0]
    nb = s // HBLK
    n_ch = HBLK // CHUNK
    n_chain = 2 * b

    def body(qf_ref, zf_ref, vf_ref, qb_ref, zb_ref, vb_ref, lbf_ref, lbb_ref, csf_ref, csb_ref, seg_ref, bd_ref,
             sel_ref, of_ref, ob_ref, stf_ref, stb_ref, *scr):
        state, b_scr, k_scr = scr[:n_chain], scr[n_chain:2 * n_chain], scr[2 * n_chain:]

        @pl.when(pl.program_id(0) == 0)
        def _():
            for st0 in state:
                st0[...] = jnp.zeros_like(st0)

        chains = []
        for bi in range(b):
            chains.append((False, qf_ref.at[bi], zf_ref.at[bi], vf_ref.at[bi], lbf_ref, csf_ref, of_ref.at[bi],
                           stf_ref.at[bi], 2 * bi))
            chains.append((True, qb_ref.at[bi], zb_ref.at[bi], vb_ref.at[bi], lbb_ref, csb_ref, ob_ref.at[bi],
                           stb_ref.at[bi], 2 * bi + 1))
        for rev, q, z, v, lb, cs, o, st, ci in chains:
            st[...] = state[ci][...]
            _, _, g, _, kk = _gates(z[...], lb[...])
            k_scr[ci][...] = kk
            b_scr[ci][...] = _ldot3(cs[...], g)

        def chunk(i, carry):
            _scan_chunks_fwd([((n_ch - 1 - i) if rev else i, rev, q, v, k_scr[ci], b_scr[ci], state[ci], o)
                              for rev, q, z, v, lb, cs, o, st, ci in chains], seg_ref, bd_ref, sel_ref)
            return carry

        lax.fori_loop(0, n_ch, chunk, 0)

    def col(c, rev):
        return pl.BlockSpec((b, HBLK, D_HGRN), lambda j: (0, (nb - 1 - j) if rev else j, c))

    def st_spec(rev):
        return pl.BlockSpec((b, None, D_HGRN, D_HGRN), lambda j: (0, (nb - 1 - j) if rev else j, 0, 0))

    sq = (D_HGRN, D_HGRN)
    proj3 = proj.reshape(b, s, proj.shape[1])
    o_fw, o_bw, st_fw, st_bw = pl.pallas_call(
        body, name=name, grid=(nb,),
        in_specs=[col(COL_HQ, False), col(COL_FF, False), col(COL_HI, False),
                  col(COL_HQ, True), col(COL_FB, True), col(COL_HI, True),
                  _full((1, D_HGRN)), _full((1, D_HGRN)), _full((HBLK, HBLK)), _full((HBLK, HBLK)),
                  _full(sq), _full(sq), _full((CHUNK, CHUNK * CHUNK))],
        out_specs=[col(0, False), col(0, True), st_spec(False), st_spec(True)],
        out_shape=[_sds((b, s, D_HGRN), F32)] * 2 + [_sds((b, nb) + sq, F32)] * 2,
        scratch_shapes=[pltpu.VMEM(sq, F32)] * n_chain + [pltpu.VMEM((HBLK, D_HGRN), F32)] * (2 * n_chain),
        compiler_params=_params(("arbitrary",)),
    )(proj3, proj3, proj3, proj3, proj3, proj3, lb_f, lb_b, hc_f["cs"], hc_b["cs"], hc_f["seg"], hc_f["bd"],
      hc_f["sel"])
    return o_fw.reshape(t, D_HGRN), o_bw.reshape(t, D_HGRN), st_fw, st_bw


def _scan_replay(c, rev, st, v_ref, k_scr, b_scr, states, bd_ref):
    last = 0 if rev else CHUNK - 1
    rows = pl.ds(pl.multiple_of(c * CHUNK, CHUNK), CHUNK)
    states[c] = st
    bb = b_scr[rows, :]
    bl = bb[last:last + 1, :]
    ke = k_scr[rows, :] * jnp.exp(bl - bb)
    return st * jnp.exp(bl) + bd_ref[...] * _dot_tn(v_ref[rows, :].astype(BF16), ke.astype(BF16))


def _scan_replays(chains, bd_ref):
    work = []
    for c, p, rev, v_ref, k_scr, b_scr, states in chains:
        last = 0 if rev else CHUNK - 1
        rows = pl.ds(pl.multiple_of(c * CHUNK, CHUNK), CHUNK)
        bb = b_scr[rows, :]
        bl = bb[last:last + 1, :]
        work.append((states, p, jnp.exp(bl), v_ref[rows, :].astype(BF16),
                     (k_scr[rows, :] * jnp.exp(bl - bb)).astype(BF16)))
    upds = [_dot_tn(v, ke) for _, _, _, v, ke in work]
    for (states, p, decay, _, _), upd in zip(work, upds):
        states[p + 1] = states[p] * decay + bd_ref[...] * upd


def _scan_chunks_bwd(chains, seg_ref, bd_ref, sel_ref, selt_ref, seld_ref):
    row = lax.broadcasted_iota(jnp.int32, (CHUNK, D_HGRN), 0)
    work = []
    for c, p, rev, q_ref, v_ref, do_ref, k_scr, b_scr, states, dstate, dq_ref, dv_ref, dk_scr, db_scr in chains:
        last = 0 if rev else CHUNK - 1
        rows = pl.ds(pl.multiple_of(c * CHUNK, CHUNK), CHUNK)
        q = q_ref[rows, :]
        k = k_scr[rows, :]
        v = v_ref[rows, :]
        bb = b_scr[rows, :]
        dout = do_ref[rows, :]
        bl = bb[last:last + 1, :]
        st_p = states[p]
        dst_n = dstate[...]
        eb = jnp.exp(bb)
        ebl = jnp.exp(bl - bb)
        qe = q * eb
        ke = k * ebl
        dec = _pair_decay(bb, rev)
        q_rep = _rows_rep(q)
        k_til = _tile_rows(k)
        do_rep = _rows_rep(dout)
        pairs = dec * q_rep * k_til
        work.append(dict(
            rows=rows, last=last, eb=eb, ebl=ebl, ebl_last=jnp.exp(bl), qe=qe, ke=ke, dec=dec, q_rep=q_rep, k_til=k_til,
            do_rep=do_rep, pairs=pairs, st_p=st_p, dst_n=dst_n, dstate=dstate, dq_ref=dq_ref, dv_ref=dv_ref,
            dk_scr=dk_scr, db_scr=db_scr, dob=dout.astype(BF16), dstb=dst_n.astype(BF16), vb=v.astype(BF16),
            pairs_b=pairs.astype(BF16), vdo_b=(_tile_rows(v) * do_rep).astype(BF16)))
    for w in work:
        w["dqe"] = jnp.dot(w["dob"], w["st_p"].astype(BF16), preferred_element_type=F32)
        w["dke"] = jnp.dot(w["vb"], w["dstb"], preferred_element_type=F32)
        w["dv"] = _dot_nt(w["ke"].astype(BF16), w["dstb"])
        w["a"] = jnp.dot(w["pairs_b"], seg_ref[...], preferred_element_type=F32)
        w["wb"] = jnp.dot(w["vdo_b"], seg_ref[...], preferred_element_type=F32)
        w["dst_upd"] = _dot_tn(w["dob"], w["qe"].astype(BF16))
    for w in work:
        gdec = w["wb"] * w["dec"]
        w["x_dq"] = (gdec * w["k_til"]).astype(BF16)
        w["x_dk"] = (gdec * w["q_rep"]).astype(BF16)
        w["x_dv"] = (w["a"] * w["do_rep"]).astype(BF16)
        w["x_db"] = (w["wb"] * w["pairs"]).astype(BF16)
    for w in work:
        dke, dqe = w["dke"], w["dqe"]
        dbl = (jnp.sum(w["dst_n"] * w["st_p"], axis=0, keepdims=True) * w["ebl_last"]
               + jnp.sum(dke * w["ke"], axis=0, keepdims=True))
        dq = dqe * w["eb"] + jnp.dot(sel_ref[...], w["x_dq"], preferred_element_type=F32)
        dk = dke * w["ebl"] + jnp.dot(selt_ref[...], w["x_dk"], preferred_element_type=F32)
        dv = w["dv"] + jnp.dot(selt_ref[...], w["x_dv"], preferred_element_type=F32)
        db = (dqe * w["qe"] - dke * w["ke"] + jnp.dot(seld_ref[...], w["x_db"], preferred_element_type=F32)
              + jnp.where(row == w["last"], dbl, 0.0))
        w["dq_ref"][w["rows"], :] = dq
        w["dv_ref"][w["rows"], :] = dv
        w["dk_scr"][w["rows"], :] = dk
        w["db_scr"][w["rows"], :] = db
        w["dstate"][...] = w["dst_n"] * w["ebl_last"] + bd_ref[...] * w["dst_upd"]


def _scan_chunk_bwd(c, rev, q_ref, v_ref, do_ref, k_scr, b_scr, states, dstate, dq_ref, dv_ref, dk_scr, db_scr,
                    seg_ref, bd_ref, sel_ref, selt_ref, seld_ref):
    last = 0 if rev else CHUNK - 1
    row = lax.broadcasted_iota(jnp.int32, (CHUNK, D_HGRN), 0)
    rows = pl.ds(pl.multiple_of(c * CHUNK, CHUNK), CHUNK)
    q = q_ref[rows, :]
    k = k_scr[rows, :]
    v = v_ref[rows, :]
    bb = b_scr[rows, :]
    dout = do_ref[rows, :]
    bl = bb[last:last + 1, :]
    st_p = states[c]
    dst_n = dstate[...]
    eb = jnp.exp(bb)
    ebl = jnp.exp(bl - bb)
    ebl_last = jnp.exp(bl)
    qe = q * eb
    ke = k * ebl
    dob = dout.astype(BF16)
    dstb = dst_n.astype(BF16)
    dqe = jnp.dot(dob, st_p.astype(BF16), preferred_element_type=F32)
    dke = jnp.dot(v.astype(BF16), dstb, preferred_element_type=F32)
    dv = _dot_nt(ke.astype(BF16), dstb)
    dbl = jnp.sum(dst_n * st_p, axis=0, keepdims=True) * ebl_last + jnp.sum(dke * ke, axis=0, keepdims=True)
    dq = dqe * eb
    dk = dke * ebl
    db = dqe * qe - dke * ke
    dec = _pair_decay(bb, rev)
    q_rep = _rows_rep(q)
    k_til = _tile_rows(k)
    do_rep = _rows_rep(dout)
    pairs = dec * q_rep * k_til
    a = jnp.dot(pairs.astype(BF16), seg_ref[...], preferred_element_type=F32)
    wb = jnp.dot((_tile_rows(v) * do_rep).astype(BF16), seg_ref[...], preferred_element_type=F32)
    gdec = wb * dec
    dq = dq + jnp.dot(sel_ref[...], (gdec * k_til).astype(BF16), preferred_element_type=F32)
    dk = dk + jnp.dot(selt_ref[...], (gdec * q_rep).astype(BF16), preferred_element_type=F32)
    dv = dv + jnp.dot(selt_ref[...], (a * do_rep).astype(BF16), preferred_element_type=F32)
    db = db + jnp.dot(seld_ref[...], (wb * pairs).astype(BF16), preferred_element_type=F32)
    db = db + jnp.where(row == last, dbl, 0.0)
    dq_ref[rows, :] = dq
    dv_ref[rows, :] = dv
    dk_scr[rows, :] = dk
    db_scr[rows, :] = db
    dstate[...] = dst_n * ebl_last + bd_ref[...] * _dot_tn(dob, qe.astype(BF16))


def _hgrn_bwd2(name, proj, lb_f, lb_b, st_f, st_b, do, b, s, hc_f, hc_b):
    t = proj.shape[0]
    nb = s // HBLK
    n_ch = HBLK // CHUNK

    n_chain = 2 * b

    def body(qf_ref, zf_ref, vf_ref, dof_ref, stf_ref, qb_ref, zb_ref, vb_ref, dob_ref, stb_ref, lbf_ref, lbb_ref,
             csf_ref, csb_ref, cstf_ref, cstb_ref, seg_ref, bd_ref, sel_ref, selt_ref, seld_ref,
             dqf_ref, dvf_ref, dzf_ref, dqb_ref, dvb_ref, dzb_ref, dlbf_ref, dlbb_ref,
             *scr):
        dstate, states, b_scr, k_scr, db_scr, dk_scr = [scr[i * n_chain:(i + 1) * n_chain] for i in range(6)]

        @pl.when(pl.program_id(0) == 0)
        def _():
            dlbf_ref[...] = jnp.zeros_like(dlbf_ref)
            dlbb_ref[...] = jnp.zeros_like(dlbb_ref)
            for d0 in dstate:
                d0[...] = jnp.zeros_like(d0)

        chains = []
        for bi in range(b):
            chains.append(dict(rev=False, q=qf_ref.at[bi], z=zf_ref.at[bi], v=vf_ref.at[bi], do=dof_ref.at[bi],
                               st=stf_ref.at[bi], lb=lbf_ref, cs=csf_ref, cst=cstf_ref, dq=dqf_ref.at[bi],
                               dv=dvf_ref.at[bi], dz=dzf_ref.at[bi], dlb=dlbf_ref, ci=2 * bi))
            chains.append(dict(rev=True, q=qb_ref.at[bi], z=zb_ref.at[bi], v=vb_ref.at[bi], do=dob_ref.at[bi],
                               st=stb_ref.at[bi], lb=lbb_ref, cs=csb_ref, cst=cstb_ref, dq=dqb_ref.at[bi],
                               dv=dvb_ref.at[bi], dz=dzb_ref.at[bi], dlb=dlbb_ref, ci=2 * bi + 1))
        for ch in chains:
            sig, f, g, sn, kk = _gates(ch["z"][...], ch["lb"][...])
            k_scr[ch["ci"]][...] = kk
            b_scr[ch["ci"]][...] = _ldot3(ch["cs"][...], g)
            ch["gates"] = (sig, f, sn)

        for ch in chains:
            states[ch["ci"]][0] = ch["st"][...]

        def replay(i, carry):
            _scan_replays([((n_ch - 1 - i) if ch["rev"] else i, i, ch["rev"], ch["v"], k_scr[ch["ci"]],
                            b_scr[ch["ci"]], states[ch["ci"]]) for ch in chains], bd_ref)
            return carry

        lax.fori_loop(0, n_ch - 1, replay, 0)

        def chunk(i, carry):
            args = [(i if ch["rev"] else (n_ch - 1 - i), n_ch - 1 - i, ch["rev"], ch["q"], ch["v"], ch["do"],
                     k_scr[ch["ci"]], b_scr[ch["ci"]], states[ch["ci"]], dstate[ch["ci"]], ch["dq"],
                     ch["dv"], dk_scr[ch["ci"]], db_scr[ch["ci"]]) for ch in chains]
            for g0 in range(0, n_chain, BWD_GROUP):
                _scan_chunks_bwd(args[g0:g0 + BWD_GROUP], seg_ref, bd_ref, sel_ref, selt_ref, seld_ref)
            return carry

        lax.fori_loop(0, n_ch, chunk, 0)
        for ch in chains:
            sig, f, sn = ch["gates"]
            lbv = ch["lb"][...]
            hi, lo = _split2(db_scr[ch["ci"]][...])
            dg = (jnp.dot(ch["cst"][...], hi, preferred_element_type=F32)
                  + jnp.dot(ch["cst"][...], lo, preferred_element_type=F32))
            dgf = jnp.where(f > F_MIN, dg / f, 0.0)
            dk = dk_scr[ch["ci"]][...]
            ch["dz"][...] = dgf * (1.0 - lbv) * sig * (1.0 - sig) - dk * (1.0 - lbv) * sn * (1.0 - sn)
            ch["dlb"][...] += _rowgroups(dgf * (1.0 - sig) - dk * sn)

    def col(c, rev):
        return pl.BlockSpec((b, HBLK, D_HGRN), lambda j: (0, j if rev else (nb - 1 - j), c))

    def st_spec(rev):
        return pl.BlockSpec((b, None, D_HGRN, D_HGRN), lambda j: (0, j if rev else (nb - 1 - j), 0, 0))

    sq = (D_HGRN, D_HGRN)
    blk = (HBLK, D_HGRN)
    pairs_shape = (CHUNK, CHUNK * CHUNK)
    proj3 = proj.reshape(b, s, proj.shape[1])
    do3 = do.reshape(b, s, D_HGRN)
    res = pl.pallas_call(
        body, name=name, grid=(nb,),
        in_specs=[col(COL_HQ, False), col(COL_FF, False), col(COL_HI, False), col(0, False), st_spec(False),
                  col(COL_HQ, True), col(COL_FB, True), col(COL_HI, True), col(0, True), st_spec(True),
                  _full((1, D_HGRN)), _full((1, D_HGRN)), _full((HBLK, HBLK)), _full((HBLK, HBLK)),
                  _full((HBLK, HBLK)), _full((HBLK, HBLK)), _full(sq), _full(sq),
                  _full(pairs_shape), _full(pairs_shape), _full(pairs_shape)],
        out_specs=[col(0, False)] * 3 + [col(0, True)] * 3 + [_full((8, D_HGRN))] * 2,
        out_shape=[_sds((b, s, D_HGRN), F32)] * 6 + [_sds((8, D_HGRN), F32)] * 2,
        scratch_shapes=[pltpu.VMEM(sq, F32)] * n_chain + [pltpu.VMEM((n_ch,) + sq, F32)] * n_chain
        + [pltpu.VMEM(blk, F32)] * (4 * n_chain),
        compiler_params=_params(("arbitrary",)),
    )(proj3, proj3, proj3, do3, st_f, proj3, proj3, proj3, do3, st_b, lb_f, lb_b, hc_f["cs"], hc_b["cs"],
      hc_f["cs_t"], hc_b["cs_t"], hc_f["seg"], hc_f["bd"], hc_f["sel"], hc_f["selt"], hc_f["seld"])
    return [r.reshape(t, D_HGRN) for r in res[:6]] + list(res[6:])


def _lower_bounds(logits):
    n = logits.shape[1]

    def body(x_ref, o_ref):
        x = x_ref[...]
        for d in range(2):
            rows = [x[l * 2 + d:l * 2 + d + 1, :] for l in range(DEPTH)]
            mx = functools.reduce(jnp.maximum, rows)
            ex = [jnp.exp(r - mx) for r in rows]
            tot = functools.reduce(lambda a, c: a + c, ex)
            sm = [e / tot for e in ex]
            run = jnp.zeros_like(sm[0])
            for l in range(DEPTH):
                run = run + sm[l]
                o_ref[l * 2 + d:l * 2 + d + 1, :] = run - sm[0]

    return pl.pallas_call(body, name="hgrn_lower_bounds", out_shape=_sds(logits.shape, F32),
                          in_specs=[_full(logits.shape)], out_specs=_full(logits.shape), grid=(1,),
                          compiler_params=_params(("arbitrary",)))(logits)


def _lower_bounds_bwd(logits, dlb):
    def body(x_ref, g_ref, o_ref):
        x = x_ref[...]
        gv = g_ref[...]
        for d in range(2):
            rows = [x[l * 2 + d:l * 2 + d + 1, :] for l in range(DEPTH)]
            gr = [gv[l * 2 + d:l * 2 + d + 1, :] for l in range(DEPTH)]
            mx = functools.reduce(jnp.maximum, rows)
            ex = [jnp.exp(r - mx) for r in rows]
            tot = functools.reduce(lambda a, c: a + c, ex)
            sm = [e / tot for e in ex]
            dsm = []
            for i in range(DEPTH):
                acc = functools.reduce(lambda a, c: a + c, gr[i:])
                if i == 0:
                    acc = acc - functools.reduce(lambda a, c: a + c, gr)
                dsm.append(acc)
            inner = functools.reduce(lambda a, c: a + c, [sm[i] * dsm[i] for i in range(DEPTH)])
            for i in range(DEPTH):
                o_ref[i * 2 + d:i * 2 + d + 1, :] = sm[i] * (dsm[i] - inner)

    return pl.pallas_call(body, name="hgrn_lower_bounds_bwd", out_shape=_sds(logits.shape, F32),
                          in_specs=[_full(logits.shape), _full(logits.shape)], out_specs=_full(logits.shape),
                          grid=(1,), compiler_params=_params(("arbitrary",)))(logits, dlb)


def _conv_rows(s):
    return s + 2 * (CONV_PAD + 1)


def _conv_fwd(name, proj, dw_w, dw_b, ln_w, ln_b, pw_w, pw_b, b, s):
    t = proj.shape[0]
    pad = CONV_PAD + 1
    nt = s // CONV_TILE

    def body(a_ref, g_ref, w_ref, dwb_ref, lnw_ref, lnb_ref, pw_ref, pwb_ref, y_ref, c_ref, upad, win):
        upad[0:pad, :] = jnp.zeros((pad, D_CONV), F32)
        upad[s + pad:s + 2 * pad, :] = jnp.zeros((pad, D_CONV), F32)

        def fill(i, carry):
            rows = pl.ds(pl.multiple_of(i * CONV_TILE, CONV_TILE), CONV_TILE)
            upad[pl.ds(pl.multiple_of(i * CONV_TILE + pad, pad), CONV_TILE), :] = a_ref[rows, :] * _sigmoid(g_ref[rows, :])
            return carry

        lax.fori_loop(0, nt, fill, 0)

        def tile(i, carry):
            r0 = pl.multiple_of(i * CONV_TILE, CONV_TILE)
            win[...] = upad[pl.ds(r0, CONV_TILE + 2 * pad), :]
            acc = jnp.zeros((CONV_TILE, D_CONV), F32)
            for j in range(CONV_W):
                acc = acc + win[j + 1:j + 1 + CONV_TILE, :] * w_ref[j:j + 1, :]
            c = acc + dwb_ref[...]
            c_ref[pl.ds(r0, CONV_TILE), :] = c
            mu = jnp.mean(c, axis=-1, keepdims=True)
            xc = c - mu
            rstd = lax.rsqrt(jnp.mean(xc * xc, axis=-1, keepdims=True) + LN_EPS)
            n = xc * rstd * lnw_ref[...] + lnb_ref[...]
            y_ref[pl.ds(r0, CONV_TILE), :] = (jnp.dot(_silu(n).astype(BF16), pw_ref[...].astype(BF16),
                                                      preferred_element_type=F32) + pwb_ref[...])
            return carry

        lax.fori_loop(0, nt, tile, 0)

    vec = _full((1, D_CONV))
    return pl.pallas_call(
        body, name=name, grid=(b,),
        in_specs=[pl.BlockSpec((s, D_CONV), lambda bi: (bi, COL_CA)), pl.BlockSpec((s, D_CONV), lambda bi: (bi, COL_CB)),
                  _full((CONV_W + 1, D_CONV)), vec, vec, vec, _full((D_CONV, D_CONV)), vec],
        out_specs=[pl.BlockSpec((s, D_CONV), lambda bi: (bi, 0))] * 2, out_shape=[_sds((t, D_CONV), F32)] * 2,
        scratch_shapes=[pltpu.VMEM((_conv_rows(s), D_CONV), F32), pltpu.VMEM((CONV_TILE + 2 * pad, D_CONV), F32)],
        compiler_params=_params(("parallel",)),
    )(proj, proj, dw_w, dw_b, ln_w, ln_b, pw_w, pw_b)


def _conv_bwd(name, proj, conv_out, dw_w, ln_w, ln_b, pw_w, dy, b, s):
    t = proj.shape[0]
    pad = CONV_PAD + 1
    nt = s // CONV_TILE

    def body(a_ref, g_ref, c_ref, w_ref, lnw_ref, lnb_ref, pw_ref, dy_ref, dab_ref, dpw_ref, ddw_ref, dvec_ref,
             upad, dcpad, tap_acc, win, dwin):
        @pl.when(pl.program_id(0) == 0)
        def _():
            dpw_ref[...] = jnp.zeros_like(dpw_ref)
            ddw_ref[...] = jnp.zeros_like(ddw_ref)
            dvec_ref[...] = jnp.zeros_like(dvec_ref)

        zeros = jnp.zeros((pad, D_CONV), F32)
        upad[0:pad, :] = zeros
        upad[s + pad:s + 2 * pad, :] = zeros
        dcpad[0:pad, :] = zeros
        dcpad[s + pad:s + 2 * pad, :] = zeros
        tap_acc[...] = jnp.zeros_like(tap_acc)

        def inner(i):
            return pl.ds(pl.multiple_of(i * CONV_TILE + pad, pad), CONV_TILE)

        def fill(i, carry):
            rows = pl.ds(pl.multiple_of(i * CONV_TILE, CONV_TILE), CONV_TILE)
            upad[inner(i), :] = a_ref[rows, :] * _sigmoid(g_ref[rows, :])
            return carry

        lax.fori_loop(0, nt, fill, 0)

        def tile_a(i, carry):
            r0 = pl.multiple_of(i * CONV_TILE, CONV_TILE)
            c = c_ref[pl.ds(r0, CONV_TILE), :]
            mu = jnp.mean(c, axis=-1, keepdims=True)
            xc = c - mu
            rstd = lax.rsqrt(jnp.mean(xc * xc, axis=-1, keepdims=True) + LN_EPS)
            xhat = xc * rstd
            n = xhat * lnw_ref[...] + lnb_ref[...]
            dyt = dy_ref[pl.ds(r0, CONV_TILE), :]
            dyb = dyt.astype(BF16)
            dpw_ref[...] += _dot_tn(_silu(n).astype(BF16), dyb)
            dn = _dot_nt(dyb, pw_ref[...].astype(BF16)) * _dsilu(n)
            dxh = dn * lnw_ref[...]
            dc = rstd * (dxh - jnp.mean(dxh, axis=-1, keepdims=True)
                         - xhat * jnp.mean(dxh * xhat, axis=-1, keepdims=True))
            dcpad[inner(i), :] = dc
            dvec_ref[0:1, :] += jnp.sum(dyt, axis=0, keepdims=True)
            dvec_ref[1:2, :] += jnp.sum(dn * xhat, axis=0, keepdims=True)
            dvec_ref[2:3, :] += jnp.sum(dn, axis=0, keepdims=True)
            dvec_ref[3:4, :] += jnp.sum(dc, axis=0, keepdims=True)
            return carry

        lax.fori_loop(0, nt, tile_a, 0)

        def tile_b(i, carry):
            r0 = pl.multiple_of(i * CONV_TILE, CONV_TILE)
            win[...] = upad[pl.ds(r0, CONV_TILE + 2 * pad), :]
            dwin[...] = dcpad[pl.ds(r0, CONV_TILE + 2 * pad), :]
            dct = dwin[pad:pad + CONV_TILE, :]
            du = jnp.zeros((CONV_TILE, D_CONV), F32)
            for j in range(CONV_W):
                du = du + dwin[2 * pad - 1 - j:2 * pad - 1 - j + CONV_TILE, :] * w_ref[j:j + 1, :]
                tap_acc[8 * j:8 * j + 8, :] += _rowgroups(dct * win[j + 1:j + 1 + CONV_TILE, :])
            rows = pl.ds(r0, CONV_TILE)
            sg = _sigmoid(g_ref[rows, :])
            dab_ref[rows, 0:D_CONV] = (du * sg).astype(BF16)
            dab_ref[rows, D_CONV:2 * D_CONV] = (du * a_ref[rows, :] * sg * (1.0 - sg)).astype(BF16)
            return carry

        lax.fori_loop(0, nt, tile_b, 0)
        for j in range(CONV_W):
            ddw_ref[j:j + 1, :] += jnp.sum(tap_acc[8 * j:8 * j + 8, :], axis=0, keepdims=True)

    vec = _full((1, D_CONV))
    return pl.pallas_call(
        body, name=name, grid=(b,),
        in_specs=[pl.BlockSpec((s, D_CONV), lambda bi: (bi, COL_CA)), pl.BlockSpec((s, D_CONV), lambda bi: (bi, COL_CB)),
                  pl.BlockSpec((s, D_CONV), lambda bi: (bi, 0)),
                  _full((CONV_W + 1, D_CONV)), vec, vec, _full((D_CONV, D_CONV)),
                  pl.BlockSpec((s, D_CONV), lambda bi: (bi, 0))],
        out_specs=[pl.BlockSpec((s, 2 * D_CONV), lambda bi: (bi, 0)), _full((D_CONV, D_CONV)),
                   _full((CONV_W + 1, D_CONV)), _full((8, D_CONV))],
        out_shape=[_sds((t, 2 * D_CONV), BF16), _sds((D_CONV, D_CONV), F32), _sds((CONV_W + 1, D_CONV), F32),
                   _sds((8, D_CONV), F32)],
        scratch_shapes=[pltpu.VMEM((_conv_rows(s), D_CONV), F32), pltpu.VMEM((_conv_rows(s), D_CONV), F32),
                        pltpu.VMEM((8 * CONV_W, D_CONV), F32), pltpu.VMEM((CONV_TILE + 2 * pad, D_CONV), F32),
                        pltpu.VMEM((CONV_TILE + 2 * pad, D_CONV), F32)],
        compiler_params=_params(("arbitrary",)),
    )(proj, proj, conv_out, dw_w, ln_w, ln_b, pw_w, dy)


def _mix_fwd(name, y_attn, o_fw, o_bw, proj, y_conv, aw, gw, cw, seg):
    t = y_attn.shape[0]
    tm = _row_tile(t)

    def body(ya_ref, of_ref, ob_ref, hg_ref, yc_ref, aw_ref, gw_ref, cw_ref, seg_ref, o_ref):
        ya = ya_ref[...]
        ra = lax.rsqrt(jnp.mean(ya * ya, axis=-1, keepdims=True) + EPS)
        o_ref[:, 0:D_ATTN] = (ya * ra * aw_ref[...]).astype(BF16)
        o = of_ref[...] + ob_ref[...]
        ro = lax.rsqrt(jnp.dot((o * o).astype(BF16), seg_ref[...], preferred_element_type=F32) + EPS)
        o_ref[:, D_ATTN:D_ATTN + D_HGRN] = (o * ro * gw_ref[...] * _silu(hg_ref[...])).astype(BF16)
        yc = yc_ref[...]
        rc = lax.rsqrt(jnp.mean(yc * yc, axis=-1, keepdims=True) + EPS)
        o_ref[:, D_ATTN + D_HGRN:D_MODEL] = (yc * rc * cw_ref[...]).astype(BF16)

    def tile(w, c=0):
        return pl.BlockSpec((tm, w), lambda i: (i, c))

    return pl.pallas_call(
        body, name=name, grid=(t // tm,),
        in_specs=[tile(D_ATTN), tile(D_HGRN), tile(D_HGRN), tile(D_HGRN, COL_HG), tile(D_CONV),
                  _full((1, D_ATTN)), _full((1, D_HGRN)), _full((1, D_CONV)), _full((D_HGRN, D_HGRN))],
        out_specs=tile(D_MODEL), out_shape=_sds((t, D_MODEL), BF16),
        compiler_params=_params(("parallel",)),
    )(y_attn, o_fw, o_bw, proj, y_conv, aw, gw, cw, seg)


def _mix_bwd(name, dmix, y_attn, o_fw, o_bw, proj, y_conv, aw, gw, cw, seg, deps=()):
    t = y_attn.shape[0]
    tm = _row_tile(t)

    def rms_bwd(x, w, dy):
        r = lax.rsqrt(jnp.mean(x * x, axis=-1, keepdims=True) + EPS)
        gwv = dy * w
        return r * gwv - x * (r * r * r) * jnp.mean(gwv * x, axis=-1, keepdims=True), _rowgroups(dy * x * r)

    def body(dm_ref, ya_ref, of_ref, ob_ref, hg_ref, yc_ref, aw_ref, gw_ref, cw_ref, seg_ref, *rest):
        dya_ref, do_ref, dhg_ref, dyc_ref, daw_ref, dgw_ref, dcw_ref = rest[-7:]

        @pl.when(pl.program_id(0) == 0)
        def _():
            daw_ref[...] = jnp.zeros_like(daw_ref)
            dgw_ref[...] = jnp.zeros_like(dgw_ref)
            dcw_ref[...] = jnp.zeros_like(dcw_ref)

        dya, daw = rms_bwd(ya_ref[...], aw_ref[...], dm_ref[:, 0:D_ATTN])
        dya_ref[...] = dya
        daw_ref[...] += daw
        dyc, dcw = rms_bwd(yc_ref[...], cw_ref[...], dm_ref[:, D_ATTN + D_HGRN:D_MODEL])
        dyc_ref[...] = dyc
        dcw_ref[...] += dcw
        d2 = dm_ref[:, D_ATTN:D_ATTN + D_HGRN]
        o = of_ref[...] + ob_ref[...]
        hg = hg_ref[...]
        ro = lax.rsqrt(jnp.dot((o * o).astype(BF16), seg_ref[...], preferred_element_type=F32) + EPS)
        dn = d2 * _silu(hg)
        dhg_ref[...] = (d2 * o * ro * gw_ref[...] * _dsilu(hg)).astype(BF16)
        gwv = dn * gw_ref[...]
        do_ref[...] = ro * gwv - o * (ro * ro * ro) * _rdot2(gwv * o, seg_ref[...])
        dgw_ref[...] += _rowgroups(dn * o * ro)

    def tile(w, c=0):
        return pl.BlockSpec((tm, w), lambda i: (i, c))

    return pl.pallas_call(
        body, name=name, grid=(t // tm,),
        in_specs=[tile(D_MODEL), tile(D_ATTN), tile(D_HGRN), tile(D_HGRN), tile(D_HGRN, COL_HG), tile(D_CONV),
                  _full((1, D_ATTN)), _full((1, D_HGRN)), _full((1, D_CONV)), _full((D_HGRN, D_HGRN))]
        + [_full(a.shape) for a in deps],
        out_specs=[tile(D_ATTN), tile(D_HGRN), tile(D_HGRN), tile(D_CONV),
                   _full((8, D_ATTN)), _full((8, D_HGRN)), _full((8, D_CONV))],
        out_shape=[_sds((t, D_ATTN), F32), _sds((t, D_HGRN), F32), _sds((t, D_HGRN), BF16), _sds((t, D_CONV), F32),
                   _sds((8, D_ATTN), F32), _sds((8, D_HGRN), F32), _sds((8, D_CONV), F32)],
        compiler_params=_params(("arbitrary",)),
    )(dmix, y_attn, o_fw, o_bw, proj, y_conv, aw, gw, cw, seg, *deps)


def _dproj(name, dp_attn, dq_f, dq_b, dz_fw, dz_bw, dv_f, dv_b, dhg, dp_conv):
    t = dq_f.shape[0]
    tm = _row_tile(t)
    wa, wc = dp_attn.shape[1], dp_conv.shape[1]

    def body(at_ref, qf_ref, qb_ref, zf_ref, zb_ref, vf_ref, vb_ref, hg_ref, cv_ref, o_ref):
        o_ref[:, 0:wa] = at_ref[...]
        cols = (qf_ref[...] + qb_ref[...], zf_ref[...], zb_ref[...], vf_ref[...] + vb_ref[...], hg_ref[...])
        for i, val in enumerate(cols):
            o_ref[:, wa + i * D_HGRN:wa + (i + 1) * D_HGRN] = val.astype(BF16)
        o_ref[:, wa + 5 * D_HGRN:D_IN] = cv_ref[...]

    tile = lambda w: pl.BlockSpec((tm, w), lambda i: (i, 0))
    return pl.pallas_call(
        body, name=name, grid=(t // tm,), in_specs=[tile(wa)] + [tile(D_HGRN)] * 7 + [tile(wc)],
        out_specs=tile(D_IN), out_shape=_sds((t, D_IN), BF16), compiler_params=_params(("parallel",)),
    )(dp_attn, dq_f, dq_b, dz_fw, dz_bw, dv_f, dv_b, dhg, dp_conv)


def _mm_tile(t):
    return min(512, t)


def _resident(shape):
    n = len(shape)
    return pl.BlockSpec(tuple(shape), lambda *_: (0,) * n, pipeline_mode=pl.Buffered(1))


def _w_blk(rows, cols, j_of):
    return pl.BlockSpec((None, rows, cols), lambda *g: (j_of(*g), 0, 0))


def _layer_fwd(l, x, wget, sm, tabs, cst, b, s, deps, target=None):
    t = x.shape[0]
    tm = _mm_tile(t)
    nt = t // tm
    pre = "l%d_" % l
    row = lambda w: pl.BlockSpec((tm, w), lambda i, *_: (i, 0))

    def normed(x_ref, nw_ref):
        xv = x_ref[...]
        r = lax.rsqrt(jnp.mean(xv * xv, axis=-1, keepdims=True) + EPS)
        return (xv * r * nw_ref[...]).astype(BF16)

    def in_body(x_ref, nw_ref, w_ref, *rest):
        o_ref, h_ref = rest[-2:]
        hv = normed(x_ref, nw_ref)
        h_ref[...] = hv
        for j in range(N_CHIP):
            o_ref[:, j * IN_BLK:(j + 1) * IN_BLK] = jnp.dot(hv, w_ref[j], preferred_element_type=F32)

    w_in = wget(l, "w_in", x)
    proj, h1 = pl.pallas_call(
        in_body, name=pre + "in_proj", grid=(nt,),
        in_specs=[row(D_MODEL), _full((1, D_MODEL)), _resident(w_in.shape)] + [_full(a.shape) for a in deps],
        out_specs=[row(D_IN), row(D_MODEL)], out_shape=[_sds((t, D_IN), F32), _sds((t, D_MODEL), BF16)],
        compiler_params=_params(("parallel",)),
    )(x, sm["mix_norm_w"][l], w_in, *deps)
    qn, kr, vr = _attn_prep(pre + "attn_prep", proj, s, tabs, sm["q_norm_w"][l], sm["k_norm_w"][l], cst["attn"])
    y_attn = _attn_fwd(pre + "attn", qn, kr, vr, b, s)
    o_fw, o_bw, st_fw, st_bw = _hgrn_fwd2(pre + "hgrn", proj, sm["lb"][l][0], sm["lb"][l][1], b, s, cst["hg_fw"],
                                          cst["hg_bw"])
    y_conv, conv_out = _conv_fwd(pre + "conv", proj, sm["conv_dw_w"][l], sm["conv_dw_b"][l], sm["conv_ln_w"][l],
                       sm["conv_ln_b"][l], sm["conv_pw_w"][l], sm["conv_pw_b"][l], b, s)
    mixed = _mix_fwd(pre + "mix", y_attn, o_fw, o_bw, proj, y_conv, sm["attn_out_norm_w"][l], sm["gnorm_w"][l],
                     sm["conv_out_norm_w"][l], cst["seg_h"])
    (x1,) = _mm(pre + "out_proj", (nt,),
                [(mixed, row(D_MODEL), wget(l, "w_out", mixed),
                  pl.BlockSpec((N_CHIP, OUT_BLK, D_MODEL), lambda i: (0, 0, 0)), NN)],
                [(x, row(D_MODEL))], [(_sds((t, D_MODEL), F32), row(D_MODEL))],
                lambda tot, xr: (xr + tot,))
    ff3 = pl.BlockSpec((N_CHIP, tm, FF_BLK), lambda i: (0, i, 0))
    ffs = _sds((N_CHIP, t, FF_BLK), BF16)

    def gu_body(x_ref, nw_ref, wg_ref, wu_ref, g_ref, u_ref, a_ref, h_ref):
        hv = normed(x_ref, nw_ref)
        h_ref[...] = hv
        for j in range(N_CHIP):
            gv = jnp.dot(hv, wg_ref[j], preferred_element_type=F32)
            uv = jnp.dot(hv, wu_ref[j], preferred_element_type=F32)
            g_ref[j] = gv.astype(BF16)
            u_ref[j] = uv.astype(BF16)
            a_ref[j] = (_silu(gv) * uv).astype(BF16)

    w_gate, w_up = wget(l, "w_gate", x1), wget(l, "w_up", x1)
    gate, up, act, h2 = pl.pallas_call(
        gu_body, name=pre + "ffn_gate_up", grid=(nt,),
        in_specs=[row(D_MODEL), _full((1, D_MODEL)), _resident(w_gate.shape), _resident(w_up.shape)],
        out_specs=[ff3, ff3, ff3, row(D_MODEL)], out_shape=[ffs, ffs, ffs, _sds((t, D_MODEL), BF16)],
        compiler_params=_params(("parallel",)),
    )(x1, sm["ffn_norm_w"][l], w_gate, w_up)

    def down_body(a_ref, w_ref, x_ref, o_ref):
        tot = x_ref[...]
        for j in range(N_CHIP):
            tot = tot + jnp.dot(a_ref[j], w_ref[j], preferred_element_type=F32)
        o_ref[...] = tot

    def down_loss_body(a_ref, w_ref, x_ref, t_ref, dy_ref, acc_ref):
        tot = x_ref[...]
        for j in range(N_CHIP):
            tot = tot + jnp.dot(a_ref[j], w_ref[j], preferred_element_type=F32)
        e = tot - t_ref[...]
        dy_ref[...] = e * (1.0 / D_MODEL)

        @pl.when(pl.program_id(0) == 0)
        def _():
            acc_ref[...] = jnp.zeros_like(acc_ref)

        acc_ref[...] += _rowgroups(e * e)

    w_down = wget(l, "w_down", act)
    if target is None:
        x2 = pl.pallas_call(
            down_body, name=pre + "ffn_down", grid=(nt,), in_specs=[ff3, _resident(w_down.shape), row(D_MODEL)],
            out_specs=row(D_MODEL), out_shape=_sds((t, D_MODEL), F32), compiler_params=_params(("parallel",)),
        )(act, w_down, x1)
    else:
        x2 = pl.pallas_call(
            down_loss_body, name=pre + "ffn_down_loss", grid=(nt,),
            in_specs=[ff3, _resident(w_down.shape), row(D_MODEL), row(D_MODEL)],
            out_specs=[row(D_MODEL), _full((8, D_MODEL))],
            out_shape=[_sds((t, D_MODEL), F32), _sds((8, D_MODEL), F32)], compiler_params=_params(("arbitrary",)),
        )(act, w_down, x1, target)
    saved = dict(x=x, h1=h1, proj=proj, qn=qn, kr=kr, vr=vr, y_attn=y_attn, o_fw=o_fw, o_bw=o_bw, st_fw=st_fw,
                 st_bw=st_bw, y_conv=y_conv, conv_out=conv_out, mixed=mixed, x1=x1, h2=h2, gate=gate, up=up, act=act)
    return x2, saved


def _layer_bwd(l, dx2, sv, wget, sm, tabs, cst, b, s, on_grads):
    t = dx2.shape[0]
    tm = _mm_tile(t)
    nt = t // tm
    pre = "l%d_" % l
    tk = min(2048, t)
    nk = t // tk
    row = lambda w: pl.BlockSpec((tm, w), lambda i, *_: (i, 0))
    ff3 = pl.BlockSpec((N_CHIP, tm, FF_BLK), lambda i: (0, i, 0))
    ffs = _sds((N_CHIP, t, FF_BLK), BF16)

    w_down, w_gate, w_up = wget(l, "w_down", dx2), wget(l, "w_gate", dx2), wget(l, "w_up", dx2)

    def ddx_body(dx_ref, w_ref, g_ref, u_ref, dg_ref, du_ref):
        dxb = dx_ref[...].astype(BF16)
        for j in range(N_CHIP):
            da = _dot_nt(dxb, w_ref[j])
            g = g_ref[j].astype(F32)
            sg = _sigmoid(g)
            dg_ref[j] = (da * u_ref[j].astype(F32) * (sg * (1.0 + g * (1.0 - sg)))).astype(BF16)
            du_ref[j] = (da * (g * sg)).astype(BF16)

    dgate, dup = pl.pallas_call(
        ddx_body, name=pre + "ffn_down_dx", grid=(nt,), in_specs=[row(D_MODEL), _resident(w_down.shape), ff3, ff3],
        out_specs=[ff3, ff3], out_shape=[ffs, ffs], compiler_params=_params(("parallel",)),
    )(dx2, w_down, sv["gate"], sv["up"])
    colt = lambda w: pl.BlockSpec((tk, w), lambda j, k: (k, 0))
    fft = pl.BlockSpec((None, tk, FF_BLK), lambda j, k: (j, k, 0))
    (g_down,) = _mm(pre + "ffn_down_dw", (N_CHIP, nk), [(sv["act"], fft, dx2, colt(D_MODEL), TN)], [],
                    [(_sds((N_CHIP, FF_BLK, D_MODEL), BF16), pl.BlockSpec((None, FF_BLK, D_MODEL), lambda j, k: (j, 0, 0)))],
                    lambda tot: (tot,), acc=(1, (FF_BLK, D_MODEL)))
    wff = pl.BlockSpec((None, D_MODEL, FF_BLK), lambda j, k: (j, 0, 0))
    (g_gate,) = _mm(pre + "ffn_gate_dw", (N_CHIP, nk), [(sv["h2"], colt(D_MODEL), dgate, fft, TN)], [],
                    [(_sds((N_CHIP, D_MODEL, FF_BLK), BF16), wff)], lambda tot: (tot,), acc=(1, (D_MODEL, FF_BLK)))
    (g_up,) = _mm(pre + "ffn_up_dw", (N_CHIP, nk), [(sv["h2"], colt(D_MODEL), dup, fft, TN)], [],
                  [(_sds((N_CHIP, D_MODEL, FF_BLK), BF16), wff)], lambda tot: (tot,), acc=(1, (D_MODEL, FF_BLK)))

    def norm_bwd_tail(dh, x_ref, nw_ref, dres_ref, dx_ref, dw_ref):
        xv = x_ref[...]
        r = lax.rsqrt(jnp.mean(xv * xv, axis=-1, keepdims=True) + EPS)
        gw = dh * nw_ref[...]
        dx_ref[...] = dres_ref[...] + r * gw - xv * (r * r * r) * jnp.mean(gw * xv, axis=-1, keepdims=True)

        @pl.when(pl.program_id(0) == 0)
        def _():
            dw_ref[...] = jnp.zeros_like(dw_ref)

        dw_ref[...] += _rowgroups(dh * xv * r)

    def dh_body(dg_ref, du_ref, wg_ref, wu_ref, x_ref, nw_ref, dres_ref, *rest):
        tot = None
        for j in range(N_CHIP):
            r = _dot_nt(dg_ref[j], wg_ref[j]) + _dot_nt(du_ref[j], wu_ref[j])
            tot = r if tot is None else tot + r
        norm_bwd_tail(tot, x_ref, nw_ref, dres_ref, *rest[-2:])

    deps = on_grads(l, dict(w_gate=g_gate, w_up=g_up, w_down=g_down))
    dx1, d_ffn_norm = pl.pallas_call(
        dh_body, name=pre + "ffn_dh", grid=(nt,),
        in_specs=[ff3, ff3, _resident(w_gate.shape), _resident(w_up.shape), row(D_MODEL), _full((1, D_MODEL)),
                  row(D_MODEL)] + [_full(a.shape) for a in deps],
        out_specs=[row(D_MODEL), _full((8, D_MODEL))], out_shape=[_sds((t, D_MODEL), F32), _sds((8, D_MODEL), F32)],
        compiler_params=_params(("arbitrary",)),
    )(dgate, dup, w_gate, w_up, sv["x1"], sm["ffn_norm_w"][l], dx2, *deps)

    (dmix,) = _mm(pre + "out_proj_dx", (nt,),
                  [(dx1, row(D_MODEL), wget(l, "w_out", dx2),
                    pl.BlockSpec((N_CHIP, OUT_BLK, D_MODEL), lambda i: (0, 0, 0)), NT)], [],
                  [(_sds((t, D_MODEL), F32), row(D_MODEL))], lambda tot: (tot,))
    (g_out,) = _mm(pre + "out_proj_dw", (N_CHIP, nk),
                   [(sv["mixed"], pl.BlockSpec((tk, OUT_BLK), lambda j, k: (k, j)), dx1, colt(D_MODEL), TN)], [],
                   [(_sds((N_CHIP, OUT_BLK, D_MODEL), BF16), pl.BlockSpec((None, OUT_BLK, D_MODEL), lambda j, k: (j, 0, 0)))],
                   lambda tot: (tot,), acc=(1, (OUT_BLK, D_MODEL)))
    proj = sv["proj"]
    dya, do_h, dhg, dyc, d_aw, d_gw, d_cw = _mix_bwd(
        pre + "mix_bwd", dmix, sv["y_attn"], sv["o_fw"], sv["o_bw"], proj, sv["y_conv"],
        sm["attn_out_norm_w"][l], sm["gnorm_w"][l], sm["conv_out_norm_w"][l], cst["seg_h"],
        on_grads(l, dict(w_out=g_out)))
    dqs, dkr, dvr = _attn_bwd(pre + "attn_bwd", sv["qn"], sv["kr"], sv["vr"], dya, b, s)
    dp_attn, d_qw, d_kw = _attn_prep_bwd(pre + "attn_prep_bwd", proj, s, tabs, sm["q_norm_w"][l], sm["k_norm_w"][l],
                                         cst["attn"], dqs, dkr, dvr)
    dq_f, dv_f, dz_fw, dq_b, dv_b, dz_bw, dlb_fw, dlb_bw = _hgrn_bwd2(
        pre + "hgrn_bwd", proj, sm["lb"][l][0], sm["lb"][l][1], sv["st_fw"], sv["st_bw"], do_h, b, s,
        cst["hg_fw"], cst["hg_bw"])
    dp_conv, d_pw, d_dw, d_cvec = _conv_bwd(pre + "conv_bwd", proj, sv["conv_out"], sm["conv_dw_w"][l],
                                            sm["conv_ln_w"][l], sm["conv_ln_b"][l], sm["conv_pw_w"][l], dyc, b, s)
    dproj = _dproj(pre + "dproj", dp_attn, dq_f, dq_b, dz_fw, dz_bw, dv_f, dv_b, dhg, dp_conv)
    g_pw = d_pw.reshape(N_CHIP, D_CONV // N_CHIP, D_CONV).astype(BF16)

    (g_in,) = _mm(pre + "in_proj_dw", (N_CHIP, nk),
                  [(sv["h1"], colt(D_MODEL), dproj, pl.BlockSpec((tk, IN_BLK), lambda j, k: (k, j)), TN)], [],
                  [(_sds((N_CHIP, D_MODEL, IN_BLK), BF16), pl.BlockSpec((None, D_MODEL, IN_BLK), lambda j, k: (j, 0, 0)))],
                  lambda tot: (tot,), acc=(1, (D_MODEL, IN_BLK)))

    def indx_body(dp_ref, w_ref, x_ref, nw_ref, dres_ref, *rest):
        tot = None
        for j in range(N_CHIP):
            r = _dot_nt(dp_ref[:, j * IN_BLK:(j + 1) * IN_BLK], w_ref[j])
            tot = r if tot is None else tot + r
        norm_bwd_tail(tot, x_ref, nw_ref, dres_ref, *rest[-2:])

    w_in = wget(l, "w_in", dx2)
    deps = on_grads(l, dict(w_in=g_in, conv_pw_w=g_pw))
    dx, d_mix_norm = pl.pallas_call(
        indx_body, name=pre + "in_proj_dx", grid=(nt,),
        in_specs=[row(D_IN), _resident(w_in.shape), row(D_MODEL), _full((1, D_MODEL)), row(D_MODEL)]
        + [_full(a.shape) for a in deps],
        out_specs=[row(D_MODEL), _full((8, D_MODEL))], out_shape=[_sds((t, D_MODEL), F32), _sds((8, D_MODEL), F32)],
        compiler_params=_params(("arbitrary",)),
    )(dproj, w_in, sv["x"], sm["mix_norm_w"][l], dx1, *deps)
    heads = lambda v, n: v.sum(axis=0).reshape(n, HEAD_DIM).sum(axis=0)
    small = dict(
        mix_norm_w=d_mix_norm.sum(axis=0), q_norm_w=heads(d_qw, D_ATTN // HEAD_DIM), k_norm_w=heads(d_kw, N_KV),
        lb=jnp.stack([dlb_fw.sum(axis=0), dlb_bw.sum(axis=0)]), hgrn_gnorm_w=heads(d_gw, D_HGRN // HEAD_DIM),
        conv_dw_w=d_dw[:CONV_W], conv_dw_b=d_cvec[3], conv_ln_w=d_cvec[1], conv_ln_b=d_cvec[2],
        conv_pw_b=d_cvec[0], attn_out_norm_w=d_aw.sum(axis=0), conv_out_norm_w=d_cw.sum(axis=0),
        ffn_norm_w=d_ffn_norm.sum(axis=0))
    return dx, small


SMALL_ORDER = ("mix_norm_w", "q_norm_w", "k_norm_w", "lb", "hgrn_gnorm_w", "conv_dw_w", "conv_dw_b", "conv_ln_w",
               "conv_ln_b", "conv_pw_b", "attn_out_norm_w", "conv_out_norm_w", "ffn_norm_w")
BIG_ORDER = ("w_in", "w_out", "w_gate", "w_up", "w_down")
SCATTER_ORDER = BIG_ORDER + ("conv_pw_w",)


def _local_step(x, target, wget, sm, deps, on_grads):
    b, s, d = x.shape
    t = b * s
    cos, sin = _rope_tables(s)
    tabs = dict(cq=jnp.tile(cos, (1, D_ATTN // HEAD_DIM)), sq=jnp.tile(sin, (1, D_ATTN // HEAD_DIM)),
                ck=jnp.tile(cos, (1, N_KV)), sk=jnp.tile(sin, (1, N_KV)))
    cst = dict(attn=_attn_consts(), hg_fw=_hgrn_consts(False), hg_bw=_hgrn_consts(True),
               seg_h=_bf(_seg_matrix(D_HGRN, HEAD_DIM, 1.0 / HEAD_DIM)))
    vec = lambda a: a.reshape(DEPTH, 1, -1)
    smk = dict(sm)
    for n in ("mix_norm_w", "conv_dw_b", "conv_ln_w", "conv_ln_b", "conv_pw_b", "attn_out_norm_w", "conv_out_norm_w",
              "ffn_norm_w"):
        smk[n] = vec(sm[n])
    smk["q_norm_w"] = vec(jnp.tile(sm["q_norm_w"], (1, D_ATTN // HEAD_DIM)))
    smk["k_norm_w"] = vec(jnp.tile(sm["k_norm_w"], (1, N_KV)))
    smk["gnorm_w"] = vec(jnp.tile(sm["hgrn_gnorm_w"], (1, D_HGRN // HEAD_DIM)))
    smk["lb"] = sm["lb"].reshape(DEPTH, 2, 1, D_HGRN)
    smk["conv_dw_w"] = jnp.pad(sm["conv_dw_w"], ((0, 0), (0, 1), (0, 0)))

    h = x.reshape(t, d)
    saved = []
    for l in range(DEPTH):
        h, sv = _layer_fwd(l, h, wget, smk, tabs, cst, b, s, deps if l == 0 else (),
                           target.reshape(t, d) if l == DEPTH - 1 else None)
        saved.append(sv)
    dy, sq = h
    sq_sum = jnp.sum(sq)
    dh = dy
    smalls = [None] * DEPTH
    for l in reversed(range(DEPTH)):
        dh, smalls[l] = _layer_bwd(l, dh, saved[l], wget, smk, tabs, cst, b, s, on_grads)
    return sq_sum, dh.reshape(b, s, d), smalls


HBM_SPEC = pl.BlockSpec(memory_space=pltpu.HBM)


def _exchange(name, arrs, mode):
    n = len(arrs)
    if mode == "gather8":
        flips = [(fx, fy, fc) for fx in (0, 1) for fy in (0, 1) for fc in (0, 1)][1:]
    elif mode == "sibling":
        flips = [(0, 0, 1)]
    else:
        flips = [(1, 0, 0), (0, 1, 0), (1, 1, 0)]
    n_f = len(flips)

    def body(*refs):
        ins, outs = refs[:n], refs[n:2 * n]
        send_sems, recv_sems, local_sems = refs[2 * n:]
        x, y, c = lax.axis_index("x"), lax.axis_index("y"), lax.axis_index("c")

        def slot_of(px, py, pc):
            return (2 * px + py) if mode != "gather8" else (4 * px + 2 * py + pc)

        me = slot_of(x, y, c)
        started = []
        for i in range(n):
            if mode != "sibling":
                src = ins[i].at[me] if mode == "scatter4" else ins[i]
                loc = pltpu.make_async_copy(src, outs[i].at[me], local_sems.at[i])
                loc.start()
                started.append(loc)
        sends, recvs = [], []
        for i in range(n):
            for f, (fx, fy, fc) in enumerate(flips):
                peer = (x ^ fx, y ^ fy, c ^ fc)
                ps = slot_of(*peer)
                if mode == "sibling":
                    src, dst, landed = ins[i], outs[i], outs[i]
                elif mode == "scatter4":
                    src, dst, landed = ins[i].at[ps], outs[i].at[me], outs[i].at[ps]
                else:
                    src, dst, landed = ins[i], outs[i].at[me], outs[i].at[ps]
                k = i * n_f + f
                cp = pltpu.make_async_remote_copy(src_ref=src, dst_ref=dst, send_sem=send_sems.at[k],
                                                  recv_sem=recv_sems.at[k], device_id=peer,
                                                  device_id_type=pl.DeviceIdType.MESH)
                cp.start()
                sends.append(cp)
                recvs.append(pltpu.make_async_remote_copy(src_ref=src, dst_ref=landed, send_sem=send_sems.at[k],
                                                          recv_sem=recv_sems.at[k], device_id=peer,
                                                          device_id_type=pl.DeviceIdType.MESH))
        for cp in sends:
            cp.wait_send()
        for cp in recvs:
            cp.wait_recv()
        for loc in started:
            loc.wait()

    def out_sds(a):
        if mode == "gather4":
            return _sds((N_CHIP,) + a.shape, a.dtype)
        if mode == "gather8":
            return _sds((N_DEV,) + a.shape, a.dtype)
        return _sds(a.shape, a.dtype)

    res = pl.pallas_call(
        body, name=name, in_specs=[HBM_SPEC] * n, out_specs=[HBM_SPEC] * n, out_shape=[out_sds(a) for a in arrs],
        scratch_shapes=[pltpu.SemaphoreType.DMA((n * n_f,)), pltpu.SemaphoreType.DMA((n * n_f,)),
                        pltpu.SemaphoreType.DMA((max(n, 1),))],
    )(*arrs)
    return list(res)


SEM_SPEC = pl.BlockSpec(memory_space=pltpu.SEMAPHORE)
SPLIT_EFFECT = pltpu.SideEffectType.DATAFLOW_SIDE_EFFECTING
CHIP_FLIPS = ((1, 0), (0, 1), (1, 1))


def _chip_copies(src_refs, land_refs, send_sems, recv_sems, scatter):
    x, y, c = lax.axis_index("x"), lax.axis_index("y"), lax.axis_index("c")
    me = 2 * x + y
    out = []
    for i, land in enumerate(land_refs):
        if scatter == "sibling":
            kw = dict(send_sem=send_sems.at[i], recv_sem=recv_sems.at[i], device_id=(x, y, 1 - c),
                      device_id_type=pl.DeviceIdType.MESH)
            cp = pltpu.make_async_remote_copy(src_ref=src_refs[i], dst_ref=land, **kw)
            out.append((cp, cp))
            continue
        if scatter == "all":
            dev = 4 * x + 2 * y + c
            for f in range(1, N_DEV):
                fx, fy, fc = (f >> 2) & 1, (f >> 1) & 1, f & 1
                peer = (x ^ fx, y ^ fy, c ^ fc)
                ps = 4 * (x ^ fx) + 2 * (y ^ fy) + (c ^ fc)
                k = i * (N_DEV - 1) + f - 1
                kw = dict(send_sem=send_sems.at[k], recv_sem=recv_sems.at[k], device_id=peer,
                          device_id_type=pl.DeviceIdType.MESH)
                out.append((pltpu.make_async_remote_copy(src_ref=land.at[dev], dst_ref=land.at[dev], **kw),
                            pltpu.make_async_remote_copy(src_ref=land.at[dev], dst_ref=land.at[ps], **kw)))
            continue
        for f, (fx, fy) in enumerate(CHIP_FLIPS):
            peer = (x ^ fx, y ^ fy, c)
            ps = 2 * (x ^ fx) + (y ^ fy)
            src = src_refs[i].at[ps] if scatter else land.at[me]
            k = i * len(CHIP_FLIPS) + f
            kw = dict(send_sem=send_sems.at[k], recv_sem=recv_sems.at[k], device_id=peer,
                      device_id_type=pl.DeviceIdType.MESH)
            out.append((pltpu.make_async_remote_copy(src_ref=src, dst_ref=land.at[me], **kw),
                        pltpu.make_async_remote_copy(src_ref=src, dst_ref=land.at[ps], **kw)))
    return out


def _split_start(name, srcs, lands, scatter):
    n = len(lands)
    n_src = len(srcs)
    n_sem = n * {"sibling": 1, "all": N_DEV - 1}.get(scatter, len(CHIP_FLIPS))

    def body(*refs):
        src_refs = refs[:n_src]
        land_refs = refs[n_src:n_src + n]
        send_sems, recv_sems = refs[n_src + n], refs[n_src + n + 1]
        token = refs[-1]
        for start, _ in _chip_copies(src_refs, land_refs, send_sems, recv_sems, scatter):
            start.start()
        token[...] = jnp.zeros_like(token)

    arrs = list(srcs) + list(lands)
    res = pl.pallas_call(
        body, name=name,
        out_shape=(pltpu.SemaphoreType.DMA((n_sem,)), pltpu.SemaphoreType.DMA((n_sem,)),
                   *[pltpu.HBM(a.shape, a.dtype) for a in arrs], _sds((8, LANES), F32)),
        in_specs=[HBM_SPEC] * len(arrs),
        out_specs=(SEM_SPEC, SEM_SPEC, *[HBM_SPEC] * len(arrs), pl.BlockSpec(memory_space=pltpu.VMEM)),
        input_output_aliases={i: 2 + i for i in range(len(arrs))},
        compiler_params=pltpu.CompilerParams(has_side_effects=SPLIT_EFFECT),
    )(*[pltpu.with_memory_space_constraint(a, pltpu.HBM) for a in arrs])
    return dict(send=res[0], recv=res[1], srcs=list(res[2:2 + n_src]), lands=list(res[2 + n_src:2 + n_src + n]),
                token=res[-1], scatter=scatter)


def _split_wait(name, started, after, with_srcs=False):
    srcs, lands, scatter = started["srcs"], started["lands"], started["scatter"]
    n, n_src = len(lands), len(srcs)

    def body(*refs):
        src_refs = refs[:n_src]
        land_refs = refs[n_src:n_src + n]
        send_sems, recv_sems = refs[n_src + n], refs[n_src + n + 1]
        for _, wait in _chip_copies(src_refs, land_refs, send_sems, recv_sems, scatter):
            wait.wait_send()
            wait.wait_recv()

    arrs = list(srcs) + list(lands)
    res = pl.pallas_call(
        body, name=name, out_shape=tuple(pltpu.HBM(a.shape, a.dtype) for a in arrs),
        in_specs=[HBM_SPEC] * len(arrs) + [SEM_SPEC, SEM_SPEC, pl.BlockSpec(memory_space=pl.ANY)],
        out_specs=tuple([HBM_SPEC] * len(arrs)), input_output_aliases={i: i for i in range(len(arrs))},
        compiler_params=pltpu.CompilerParams(has_side_effects=SPLIT_EFFECT),
    )(*arrs, started["send"], started["recv"], after)
    return (list(res[:n_src]), list(res[n_src:])) if with_srcs else list(res[n_src:])


def _flat_tile(rows):
    for cand in (512, 256, 128, 64, 32, 16, 8):
        if rows % cand == 0:
            return cand
    return rows


def _cast_slot(name, a, l, chip, layers=DEPTH, dtype=BF16, deps=(), slots=N_CHIP):
    r, c = a.shape[0] // layers, a.shape[1]
    tr = _flat_tile(r)

    def body(chip_ref, a_ref, *rest):
        rest[-1][...] = a_ref[...].astype(dtype)

    return pl.pallas_call(
        body, name=name, out_shape=_sds((slots, r, c), dtype),
        grid_spec=pltpu.PrefetchScalarGridSpec(
            num_scalar_prefetch=1, grid=(r // tr,),
            in_specs=[pl.BlockSpec((tr, c), lambda i, ch: (l * (r // tr) + i, 0))]
            + [pl.BlockSpec(d.shape, lambda i, ch: (0, 0)) for d in deps],
            out_specs=pl.BlockSpec((None, tr, c), lambda i, ch: (ch[0], i, 0))),
        compiler_params=_params(("parallel",)))(chip, a, *deps)


def _own_slot(name, g, chip):
    n, r, c = g.shape
    tr = _flat_tile(r)

    def body(chip_ref, g_ref, o_ref):
        o_ref[...] = g_ref[...]

    spec = pl.BlockSpec((None, tr, c), lambda i, ch: (ch[0], i, 0))
    return pl.pallas_call(
        body, name=name, out_shape=_sds(g.shape, g.dtype),
        grid_spec=pltpu.PrefetchScalarGridSpec(num_scalar_prefetch=1, grid=(r // tr,), in_specs=[spec], out_specs=spec),
        compiler_params=_params(("parallel",)))(chip, g)


def _sum_layers(name, lands):
    n, r, c = lands[0].shape
    tr = _flat_tile(r)
    nl = len(lands)

    def body(*refs):
        o_ref = refs[-1]
        for k in range(nl):
            @pl.when(pl.program_id(0) == k)
            def _():
                tot = refs[k][0].astype(F32)
                for i in range(1, n):
                    tot = tot + refs[k][i].astype(F32)
                o_ref[...] = tot

    return pl.pallas_call(
        body, name=name, grid=(nl, r // tr),
        in_specs=[pl.BlockSpec((n, tr, c), lambda l, i, k=k: (0, jnp.where(l == k, i, 0), 0)) for k in range(nl)],
        out_specs=pl.BlockSpec((tr, c), lambda l, i: (l * (r // tr) + i, 0)), out_shape=_sds((nl * r, c), F32),
        compiler_params=_params(("arbitrary", "arbitrary")))(*lands)


def _sum_slots(name, a, scale=None):
    n, r, c = a.shape
    tr = _flat_tile(r)

    def body(a_ref, o_ref):
        tot = a_ref[0].astype(F32)
        for i in range(1, n):
            tot = tot + a_ref[i].astype(F32)
        o_ref[...] = tot

    return pl.pallas_call(body, name=name, grid=(r // tr,),
                          in_specs=[pl.BlockSpec((n, tr, c), lambda i: (0, i, 0))],
                          out_specs=pl.BlockSpec((tr, c), lambda i: (i, 0)), out_shape=_sds((r, c), F32),
                          compiler_params=_params(("parallel",)))(a)


def _adamw(name, w, ga, gb, m, v):
    r, c = w.shape
    tr = _flat_tile(r)
    c1 = 1.0 - B1 ** STEP
    c2 = 1.0 - B2 ** STEP
    two = gb is not None

    def body(*refs):
        if two:
            w_ref, ga_ref, gb_ref, m_ref, v_ref, g_out, d_out, m_out, v_out = refs
            g = ga_ref[...] + gb_ref[...]
        else:
            w_ref, ga_ref, m_ref, v_ref, g_out, d_out, m_out, v_out = refs
            g = ga_ref[...]
        mn = B1 * m_ref[...] + (1.0 - B1) * g
        vn = B2 * v_ref[...] + (1.0 - B2) * (g * g)
        g_out[...] = g
        m_out[...] = mn
        v_out[...] = vn
        d_out[...] = -LR * ((mn / c1) / (jnp.sqrt(vn / c2) + ADAM_EPS) + WD * w_ref[...])

    spec = pl.BlockSpec((tr, c), lambda i: (i, 0))
    ins = [w, ga, gb, m, v] if two else [w, ga, m, v]
    return pl.pallas_call(body, name=name, grid=(r // tr,), in_specs=[spec] * len(ins), out_specs=[spec] * 4,
                          out_shape=[_sds((r, c), F32)] * 4, compiler_params=_params(("parallel",)))(*ins)


WEIGHTS = ('mix_norm_w', 'w_in', 'q_norm_w', 'k_norm_w', 'hgrn_lb_logits', 'hgrn_gnorm_w', 'conv_dw_w', 'conv_dw_b',
           'conv_ln_w', 'conv_ln_b', 'conv_pw_w', 'conv_pw_b', 'attn_out_norm_w', 'conv_out_norm_w', 'w_out',
           'ffn_norm_w', 'w_gate', 'w_up', 'w_down')
SHARDED_SMALL = {"hgrn_lb_logits": 2, "conv_dw_w": 2, "conv_pw_w": 1}
LANES = 128
PACK_ROWS = 256


def _pack(parts):
    flat = jnp.concatenate([p.reshape(-1) for p in parts])
    n = flat.shape[0]
    rows = -(-n // (PACK_ROWS * LANES)) * PACK_ROWS
    return jnp.pad(flat, (0, rows * LANES - n)).reshape(rows, LANES)


def _unpack(packed, shapes):
    flat = packed.reshape(-1)
    out, off = [], 0
    for shp in shapes:
        n = int(np.prod(shp))
        out.append(flat[off:off + n].reshape(shp))
        off += n
    return out


def kernel(x, mix_norm_w, w_in, q_norm_w, k_norm_w, hgrn_lb_logits, hgrn_gnorm_w, conv_dw_w, conv_dw_b, conv_ln_w, conv_ln_b, conv_pw_w, conv_pw_b, attn_out_norm_w, conv_out_norm_w, w_out, ffn_norm_w, w_gate, w_up, w_down, loss_target, m_mix_norm_w, m_w_in, m_q_norm_w, m_k_norm_w, m_hgrn_lb_logits, m_hgrn_gnorm_w, m_conv_dw_w, m_conv_dw_b, m_conv_ln_w, m_conv_ln_b, m_conv_pw_w, m_conv_pw_b, m_attn_out_norm_w, m_conv_out_norm_w, m_w_out, m_ffn_norm_w, m_w_gate, m_w_up, m_w_down, v_mix_norm_w, v_w_in, v_q_norm_w, v_k_norm_w, v_hgrn_lb_logits, v_hgrn_gnorm_w, v_conv_dw_w, v_conv_dw_b, v_conv_ln_w, v_conv_ln_b, v_conv_pw_w, v_conv_pw_b, v_attn_out_norm_w, v_conv_out_norm_w, v_w_out, v_ffn_norm_w, v_w_gate, v_w_up, v_w_down):
    w = dict(mix_norm_w=mix_norm_w, w_in=w_in, q_norm_w=q_norm_w, k_norm_w=k_norm_w, hgrn_lb_logits=hgrn_lb_logits,
             hgrn_gnorm_w=hgrn_gnorm_w, conv_dw_w=conv_dw_w, conv_dw_b=conv_dw_b, conv_ln_w=conv_ln_w,
             conv_ln_b=conv_ln_b, conv_pw_w=conv_pw_w, conv_pw_b=conv_pw_b, attn_out_norm_w=attn_out_norm_w,
             conv_out_norm_w=conv_out_norm_w, w_out=w_out, ffn_norm_w=ffn_norm_w, w_gate=w_gate, w_up=w_up,
             w_down=w_down)
    m = dict(mix_norm_w=m_mix_norm_w, w_in=m_w_in, q_norm_w=m_q_norm_w, k_norm_w=m_k_norm_w,
             hgrn_lb_logits=m_hgrn_lb_logits, hgrn_gnorm_w=m_hgrn_gnorm_w, conv_dw_w=m_conv_dw_w,
             conv_dw_b=m_conv_dw_b, conv_ln_w=m_conv_ln_w, conv_ln_b=m_conv_ln_b, conv_pw_w=m_conv_pw_w,
             conv_pw_b=m_conv_pw_b, attn_out_norm_w=m_attn_out_norm_w, conv_out_norm_w=m_conv_out_norm_w,
             w_out=m_w_out, ffn_norm_w=m_ffn_norm_w, w_gate=m_w_gate, w_up=m_w_up, w_down=m_w_down)
    v = dict(mix_norm_w=v_mix_norm_w, w_in=v_w_in, q_norm_w=v_q_norm_w, k_norm_w=v_k_norm_w,
             hgrn_lb_logits=v_hgrn_lb_logits, hgrn_gnorm_w=v_hgrn_gnorm_w, conv_dw_w=v_conv_dw_w,
             conv_dw_b=v_conv_dw_b, conv_ln_w=v_conv_ln_w, conv_ln_b=v_conv_ln_b, conv_pw_w=v_conv_pw_w,
             conv_pw_b=v_conv_pw_b, attn_out_norm_w=v_attn_out_norm_w, conv_out_norm_w=v_conv_out_norm_w,
             w_out=v_w_out, ffn_norm_w=v_ffn_norm_w, w_gate=v_w_gate, w_up=v_w_up, w_down=v_w_down)
    chip = 2 * lax.axis_index("x") + lax.axis_index("y")

    chip1 = chip.reshape(1).astype(jnp.int32)

    flat2 = lambda a: a.reshape(-1, a.shape[-1])
    groups = [[(l, "w_in")] if first else [(l, n) for n in BIG_ORDER[1:]] for l in range(DEPTH) for first in (1, 0)]
    group_of = {key: g for g, keys in enumerate(groups) for key in keys}
    groups[0].append((0, "small"))
    starts = []
    for g, keys in enumerate(groups):
        after_prev = [starts[-1]["token"]] if starts else []
        slots = [_cast_slot("cast_small", _pack([w[k] for k in SHARDED_SMALL]), 0, chip1, 1, F32, after_prev)
                 if n == "small" else _cast_slot("cast_%s_l%d" % (n, l), flat2(w[n]), l, chip1, deps=after_prev)
                 for l, n in keys]
        starts.append(_split_start("gather_start_g%d" % g, [], slots, False))
    got = {}

    def wget(l, name, after):
        if (l, name) not in got:
            g = group_of[(l, name)]
            for key, arr in zip(groups[g], _split_wait("gather_wait_g%d" % g, starts[g], after)):
                got[key] = arr
        return got[(l, name)]

    pending = []

    def on_grads(l, grads):
        names = [n for n in SCATTER_ORDER if n in grads]
        own = [_own_slot("own_%s_l%d" % (n, l), grads[n], chip1) for n in names]
        st = _split_start("scatter_start_l%d_%s" % (l, names[0]), [grads[n] for n in names], own, True)
        pending.append((l, names, st))
        return [st["token"]]

    wget(0, "w_in", starts[-1]["token"])
    gathered_small = got[(0, "small")]
    parts = [_unpack(gathered_small[j], [w[n].shape for n in SHARDED_SMALL]) for j in range(N_CHIP)]
    full_small = {n: jnp.concatenate([parts[j][i] for j in range(N_CHIP)], axis=ax)
                  for i, (n, ax) in enumerate(SHARDED_SMALL.items())}
    sm = {n: w[n] for n in WEIGHTS if n not in BIG_ORDER and n not in SHARDED_SMALL}
    sm["conv_dw_w"] = full_small["conv_dw_w"]
    sm["conv_pw_w"] = full_small["conv_pw_w"]
    logits = full_small["hgrn_lb_logits"].reshape(DEPTH * 2, D_HGRN)
    sm["lb"] = _lower_bounds(logits).reshape(DEPTH, 2, D_HGRN)

    sq_sum, grad_x, smalls = _local_step(x, loss_target, wget, sm, [st["token"] for st in starts], on_grads)
    loss = lax.psum(0.5 * sq_sum / D_MODEL, ("x", "y", "c"))

    dev1 = (4 * lax.axis_index("x") + 2 * lax.axis_index("y") + lax.axis_index("c")).reshape(1).astype(jnp.int32)
    g_pack = _pack([jnp.stack([smalls[l][n] for l in range(DEPTH)]) for n in SMALL_ORDER])
    small_start = _split_start("small_grads_start", [],
                               [_cast_slot("small_grads_slot", g_pack, 0, dev1, 1, F32, slots=N_DEV)], "all")
    landed = {}
    for l, names, st in pending:
        for n, arr in zip(names, _split_wait("scatter_wait_l%d_%s" % (l, names[0]), st, small_start["token"])):
            landed[(l, n)] = arr
    sums = [_sum_layers("sum_" + n, [landed[(l, n)] for l in range(DEPTH)]) for n in SCATTER_ORDER]
    sib_start = _split_start("sibling_start", sums, [lax.empty(a.shape, a.dtype) for a in sums], "sibling")
    out = {}

    small_names = [n for n in WEIGHTS if n not in SCATTER_ORDER]
    g_all = _split_wait("small_grads_wait", small_start, sib_start["token"])[0]
    g_tot = _sum_slots("sum_small", g_all)
    shapes = [(DEPTH,) + tuple(smalls[0][n].shape) for n in SMALL_ORDER]
    g_small = dict(zip(SMALL_ORDER, _unpack(g_tot, shapes)))
    lb_shard = lax.dynamic_slice_in_dim(g_small.pop("lb").reshape(DEPTH * 2, D_HGRN), chip * HEAD_DIM, HEAD_DIM, 1)
    g_small["hgrn_lb_logits"] = _lower_bounds_bwd(hgrn_lb_logits.reshape(DEPTH * 2, HEAD_DIM), lb_shard).reshape(
        hgrn_lb_logits.shape)
    g_small["conv_dw_w"] = lax.dynamic_slice_in_dim(g_small["conv_dw_w"], chip * HEAD_DIM, HEAD_DIM, 2)
    res = _adamw("adamw_small", _pack([w[n] for n in small_names]), _pack([g_small[n] for n in small_names]), None,
                 _pack([m[n] for n in small_names]), _pack([v[n] for n in small_names]))
    unpacked = [_unpack(r, [w[n].shape for n in small_names]) for r in res]
    for i, n in enumerate(small_names):
        out[n] = [unpacked[k][i] for k in range(4)]
    own, sib = _split_wait("sibling_wait", sib_start, res[0], with_srcs=True)
    for n, ga, gb in zip(SCATTER_ORDER, own, sib):
        big = _adamw("adamw_" + n, flat2(w[n]), ga, gb, flat2(m[n]), flat2(v[n]))
        out[n] = [r.reshape(w[n].shape) for r in big]

    return (loss, grad_x, *[out[n][0] for n in WEIGHTS], *[out[n][1] for n in WEIGHTS],
            *[out[n][2] for n in WEIGHTS], *[out[n][3] for n in WEIGHTS])
```

```python
import functools

import numpy as np
import jax
import jax.numpy as jnp
from jax import lax
from jax.experimental import pallas as pl
from jax.experimental.pallas import tpu as pltpu

F32, BF16 = jnp.float32, jnp.bfloat16

D_MODEL = 1024
DEPTH = 2
GRID_W = 64
D_ATTN, D_HGRN, D_CONV = 512, 256, 256
HEAD_DIM = 64
N_KV = 2
KV_LANES = D_ATTN // N_KV
ROPE_THETA = 10000.0
F_MIN = 1e-6
CONV_W = 31
CONV_PAD = 15
D_FF = 2816
D_IN = 2560
N_CHIP = 4
N_DEV = 8
IN_BLK = D_IN // N_CHIP
FF_BLK = D_FF // N_CHIP
OUT_BLK = D_MODEL // N_CHIP
EPS = 1e-6
LN_EPS = 1e-5
LR, B1, B2, ADAM_EPS, WD, STEP = 0.001, 0.9, 0.999, 1e-08, 0.01, 10
CHUNK = 16
HBLK = 256
CONV_TILE = 128
ATTN_FWD_ROWS = 512
ATTN_BWD_ROWS = 256
ATTN_BWD_STACK = 4
BWD_GROUP = 2
VMEM_LIMIT = 56 * 1024 * 1024

COL_Q, COL_K, COL_V = 0, 4, 5
COL_HQ, COL_FF, COL_FB, COL_HI, COL_HG, COL_CA, COL_CB = 3, 4, 5, 6, 7, 8, 9


def _params(sem=None):
    return pltpu.CompilerParams(dimension_semantics=sem, vmem_limit_bytes=VMEM_LIMIT)


def _sds(shape, dtype):
    return jax.ShapeDtypeStruct(tuple(shape), dtype)


def _full(shape):
    n = len(shape)
    return pl.BlockSpec(tuple(shape), lambda *_: (0,) * n)


def _sigmoid(x):
    return 0.5 * jnp.tanh(0.5 * x) + 0.5


def _gate_sigmoid(x):
    return 1.0 / (1.0 + jnp.exp(-x))


def _silu(x):
    return x * _sigmoid(x)


def _dsilu(x):
    s = _sigmoid(x)
    return s * (1.0 + x * (1.0 - s))


def _rowgroups(v):
    m, c = v.shape
    return v.reshape(m // 8, 8, c).sum(axis=0)


def _split2(x):
    hi = x.astype(BF16)
    lo = (x - hi.astype(F32)).astype(BF16)
    return hi, lo


def _rdot2(x, m):
    hi, lo = _split2(x)
    return (jnp.dot(hi, m, preferred_element_type=F32) + jnp.dot(lo, m, preferred_element_type=F32))


def _ldot3(m, x):
    hi = x.astype(BF16)
    r1 = x - hi.astype(F32)
    mid = r1.astype(BF16)
    lo = (r1 - mid.astype(F32)).astype(BF16)
    return (jnp.dot(m, hi, preferred_element_type=F32) + jnp.dot(m, mid, preferred_element_type=F32)
            + jnp.dot(m, lo, preferred_element_type=F32))


def _dot_nt(a, b):
    return lax.dot_general(a, b, (((1,), (1,)), ((), ())), preferred_element_type=F32)


def _dot_tn(a, b):
    return lax.dot_general(a, b, (((0,), (0,)), ((), ())), preferred_element_type=F32)


def _seg_matrix(n, seg, val):
    i = np.arange(n)
    return ((i[:, None] // seg) == (i[None, :] // seg)).astype(np.float32) * val


def _rot_matrix(n):
    r = np.zeros((n, n), np.float32)
    for i in range(n):
        if (i % 32) < 16:
            r[i + 16, i] = -1.0
        else:
            r[i - 16, i] = 1.0
    return r


def _rep_matrix():
    r = np.zeros((N_KV * HEAD_DIM, D_ATTN), np.float32)
    for kv in range(N_KV):
        for g in range(KV_LANES // HEAD_DIM):
            for d in range(HEAD_DIM):
                r[HEAD_DIM * kv + d, KV_LANES * kv + HEAD_DIM * g + d] = 1.0
    return r


def _cumsum_matrix(rev):
    i = np.arange(HBLK)
    same = (i[:, None] // CHUNK) == (i[None, :] // CHUNK)
    tri = (i[None, :] >= i[:, None]) if rev else (i[None, :] <= i[:, None])
    return (same & tri).astype(np.float32)


def _sel_matrices():
    sel = np.zeros((CHUNK, CHUNK * CHUNK), np.float32)
    selt = np.zeros((CHUNK, CHUNK * CHUNK), np.float32)
    for t in range(CHUNK):
        for s in range(CHUNK):
            sel[t, t * CHUNK + s] = 1.0
            selt[s, t * CHUNK + s] = 1.0
    return sel, selt


def _bf(a):
    return jnp.asarray(a, dtype=BF16)


def _mm(name, grid, pairs, extras, outs, epilogue, acc=None, sem=None):
    n_p, n_e, n_o = len(pairs), len(extras), len(outs)

    def body(*refs):
        ab = refs[:2 * n_p]
        ex = refs[2 * n_p:2 * n_p + n_e]
        out = refs[2 * n_p + n_e:2 * n_p + n_e + n_o]
        scr = refs[2 * n_p + n_e + n_o:]
        tot = None
        for i in range(n_p):
            a = ab[2 * i][...]
            b = ab[2 * i + 1][...]
            if a.ndim == 3:
                a = a.reshape(-1, a.shape[-1])
            if b.ndim == 3:
                b = b.reshape(-1, b.shape[-1])
            r = lax.dot_general(a.astype(BF16), b.astype(BF16), pairs[i][4], preferred_element_type=F32)
            tot = r if tot is None else tot + r

        def finish(total):
            res = epilogue(total, *[e[...] for e in ex])
            for o_ref, val in zip(out, res):
                o_ref[...] = val.astype(o_ref.dtype)

        if acc is None:
            finish(tot)
        else:
            k = pl.program_id(acc[0])

            @pl.when(k == 0)
            def _():
                scr[0][...] = tot

            @pl.when(k > 0)
            def _():
                scr[0][...] += tot

            @pl.when(k == grid[acc[0]] - 1)
            def _():
                finish(scr[0][...])

    args, in_specs = [], []
    for a, a_spec, b, b_spec, _ in pairs:
        args += [a, b]
        in_specs += [a_spec, b_spec]
    for e, e_spec in extras:
        args.append(e)
        in_specs.append(e_spec)
    if sem is None:
        sem = tuple("arbitrary" if (acc is not None and i == acc[0]) else "parallel" for i in range(len(grid)))
    return pl.pallas_call(
        body, name=name, grid=grid, in_specs=in_specs,
        out_specs=[o[1] for o in outs], out_shape=[o[0] for o in outs],
        scratch_shapes=[] if acc is None else [pltpu.VMEM(acc[1], F32)],
        compiler_params=_params(sem),
    )(*args)


NN = (((1,), (0,)), ((), ()))
NT = (((1,), (1,)), ((), ()))
TN = (((0,), (0,)), ((), ()))


def _row_tile(t):
    return min(256, t)


def _rms_fwd(name, x, w, deps=()):
    t, d = x.shape
    tm = _row_tile(t)

    def body(x_ref, w_ref, *rest):
        o_ref = rest[-1]
        xv = x_ref[...]
        r = lax.rsqrt(jnp.mean(xv * xv, axis=-1, keepdims=True) + EPS)
        o_ref[...] = (xv * r * w_ref[...]).astype(BF16)

    return pl.pallas_call(
        body, name=name, grid=(t // tm,),
        in_specs=[pl.BlockSpec((tm, d), lambda i: (i, 0)), _full((1, d))] + [_full(a.shape) for a in deps],
        out_specs=pl.BlockSpec((tm, d), lambda i: (i, 0)), out_shape=_sds((t, d), BF16),
        compiler_params=_params(("parallel",)),
    )(x, w, *deps)


def _rms_bwd(name, x, w, dh, dres, deps=()):
    t, d = x.shape
    tm = _row_tile(t)

    def body(x_ref, w_ref, dh_ref, dres_ref, *rest):
        dx_ref, dw_ref = rest[-2:]
        xv = x_ref[...]
        r = lax.rsqrt(jnp.mean(xv * xv, axis=-1, keepdims=True) + EPS)
        dy = dh_ref[...]
        gw = dy * w_ref[...]
        dx_ref[...] = dres_ref[...] + r * gw - xv * (r * r * r) * jnp.mean(gw * xv, axis=-1, keepdims=True)

        @pl.when(pl.program_id(0) == 0)
        def _():
            dw_ref[...] = jnp.zeros_like(dw_ref)

        dw_ref[...] += _rowgroups(dy * xv * r)

    tile = pl.BlockSpec((tm, d), lambda i: (i, 0))
    return pl.pallas_call(
        body, name=name, grid=(t // tm,),
        in_specs=[tile, _full((1, d)), tile, tile] + [_full(a.shape) for a in deps],
        out_specs=[tile, _full((8, d))], out_shape=[_sds((t, d), F32), _sds((8, d), F32)],
        compiler_params=_params(("arbitrary",)),
    )(x, w, dh, dres, *deps)


def _loss_kernel(y, target):
    t, d = y.shape
    tm = _row_tile(t)

    def body(y_ref, t_ref, dy_ref, acc_ref):
        e = y_ref[...] - t_ref[...]
        dy_ref[...] = e * (1.0 / d)

        @pl.when(pl.program_id(0) == 0)
        def _():
            acc_ref[...] = jnp.zeros_like(acc_ref)

        acc_ref[...] += _rowgroups(e * e)

    tile = pl.BlockSpec((tm, d), lambda i: (i, 0))
    return pl.pallas_call(
        body, name="loss_head", grid=(t // tm,), in_specs=[tile, tile],
        out_specs=[tile, _full((8, d))], out_shape=[_sds((t, d), F32), _sds((8, d), F32)],
        compiler_params=_params(("arbitrary",)),
    )(y, target)


def _rope_tables(s):
    rows = s // GRID_W
    row_id = jnp.repeat(jnp.arange(rows, dtype=F32), GRID_W)
    col_id = jnp.tile(jnp.arange(GRID_W, dtype=F32), rows)
    half = HEAD_DIM // 2
    inv_freq = ROPE_THETA ** (-jnp.arange(0, half, 2, dtype=F32) / half)
    ang_r = row_id[:, None] * inv_freq[None, :]
    ang_c = col_id[:, None] * inv_freq[None, :]
    ang = jnp.concatenate([ang_r, ang_r, ang_c, ang_c], axis=-1)
    return jnp.cos(ang).astype(F32), jnp.sin(ang).astype(F32)


def _attn_consts():
    return dict(
        seg_q=_bf(_seg_matrix(D_ATTN, HEAD_DIM, 1.0 / HEAD_DIM)),
        seg_k=_bf(_seg_matrix(N_KV * HEAD_DIM, HEAD_DIM, 1.0 / HEAD_DIM)),
        rot_q=_bf(_rot_matrix(D_ATTN)), rot_k=_bf(_rot_matrix(N_KV * HEAD_DIM)),
        rep=_bf(_rep_matrix()), rep_t=_bf(_rep_matrix().T))


def _attn_prep(name, proj, s, tabs, qw, kw, ac):
    t = proj.shape[0]
    tm = _row_tile(s)
    nst = s // tm
    kw_ = N_KV * HEAD_DIM

    def body(q_ref, k_ref, v_ref, cq_ref, sq_ref, ck_ref, sk_ref, qw_ref, kw_ref,
             segq_ref, segk_ref, rotq_ref, rotk_ref, rep_ref, qn_ref, kr_ref, vr_ref):
        q = q_ref[...]
        r = lax.rsqrt(jnp.dot((q * q).astype(BF16), segq_ref[...], preferred_element_type=F32) + EPS)
        qn = q * r * qw_ref[...]
        qr = qn * cq_ref[...] + _rdot2(qn, rotq_ref[...]) * sq_ref[...]
        qn_ref[...] = (qr * (HEAD_DIM ** -0.5)).astype(BF16)
        k = k_ref[...]
        rk = lax.rsqrt(jnp.dot((k * k).astype(BF16), segk_ref[...], preferred_element_type=F32) + EPS)
        kn = k * rk * kw_ref[...]
        kr = kn * ck_ref[...] + _rdot2(kn, rotk_ref[...]) * sk_ref[...]
        kr_ref[...] = jnp.dot(kr.astype(BF16), rep_ref[...], preferred_element_type=F32).astype(BF16)
        vr_ref[...] = jnp.dot(v_ref[...].astype(BF16), rep_ref[...], preferred_element_type=F32).astype(BF16)

    wide = pl.BlockSpec((tm, D_ATTN), lambda i: (i, 0))
    tabq = pl.BlockSpec((tm, D_ATTN), lambda i: (i % nst, 0))
    tabk = pl.BlockSpec((tm, kw_), lambda i: (i % nst, 0))
    return pl.pallas_call(
        body, name=name, grid=(t // tm,),
        in_specs=[pl.BlockSpec((tm, D_ATTN), lambda i: (i, COL_Q)), pl.BlockSpec((tm, kw_), lambda i: (i, COL_K)),
                  pl.BlockSpec((tm, kw_), lambda i: (i, COL_V)), tabq, tabq, tabk, tabk,
                  _full((1, D_ATTN)), _full((1, kw_)), _full((D_ATTN, D_ATTN)), _full((kw_, kw_)),
                  _full((D_ATTN, D_ATTN)), _full((kw_, kw_)), _full((kw_, D_ATTN))],
        out_specs=[wide, wide, wide], out_shape=[_sds((t, D_ATTN), BF16)] * 3,
        compiler_params=_params(("parallel",)),
    )(proj, proj, proj, tabs["cq"], tabs["sq"], tabs["ck"], tabs["sk"], qw, kw,
      ac["seg_q"], ac["seg_k"], ac["rot_q"], ac["rot_k"], ac["rep"])


def _attn_prep_bwd(name, proj, s, tabs, qw, kw, ac, dqs, dkr, dvr):
    t = proj.shape[0]
    tm = _row_tile(s)
    nst = s // tm
    kw_ = N_KV * HEAD_DIM
    wout = D_ATTN + 2 * kw_

    def norm_rope_bwd(x, w, cos, sin, seg, rot, d_roped):
        dn = d_roped * cos - _rdot2(d_roped * sin, rot)
        r = lax.rsqrt(jnp.dot((x * x).astype(BF16), seg, preferred_element_type=F32) + EPS)
        gw = dn * w
        dx = r * gw - x * (r * r * r) * _rdot2(gw * x, seg)
        return dx, _rowgroups(dn * x * r)

    def body(q_ref, k_ref, cq_ref, sq_ref, ck_ref, sk_ref, qw_ref, kw_ref, segq_ref, segk_ref, rotq_ref, rotk_ref,
             rept_ref, dqs_ref, dkr_ref, dvr_ref, dp_ref, dqw_ref, dkw_ref):
        dq, dqw = norm_rope_bwd(q_ref[...], qw_ref[...], cq_ref[...], sq_ref[...], segq_ref[...], rotq_ref[...],
                                dqs_ref[...] * (HEAD_DIM ** -0.5))
        dk_roped = jnp.dot(dkr_ref[...].astype(BF16), rept_ref[...], preferred_element_type=F32)
        dk, dkw = norm_rope_bwd(k_ref[...], kw_ref[...], ck_ref[...], sk_ref[...], segk_ref[...], rotk_ref[...],
                                dk_roped)
        dv = jnp.dot(dvr_ref[...].astype(BF16), rept_ref[...], preferred_element_type=F32)
        dp_ref[:, 0:D_ATTN] = dq.astype(BF16)
        dp_ref[:, D_ATTN:D_ATTN + kw_] = dk.astype(BF16)
        dp_ref[:, D_ATTN + kw_:wout] = dv.astype(BF16)

        @pl.when(pl.program_id(0) == 0)
        def _():
            dqw_ref[...] = jnp.zeros_like(dqw_ref)
            dkw_ref[...] = jnp.zeros_like(dkw_ref)

        dqw_ref[...] += dqw
        dkw_ref[...] += dkw

    wide = pl.BlockSpec((tm, D_ATTN), lambda i: (i, 0))
    tabq = pl.BlockSpec((tm, D_ATTN), lambda i: (i % nst, 0))
    tabk = pl.BlockSpec((tm, kw_), lambda i: (i % nst, 0))
    return pl.pallas_call(
        body, name=name, grid=(t // tm,),
        in_specs=[pl.BlockSpec((tm, D_ATTN), lambda i: (i, COL_Q)), pl.BlockSpec((tm, kw_), lambda i: (i, COL_K)),
                  tabq, tabq, tabk, tabk, _full((1, D_ATTN)), _full((1, kw_)),
                  _full((D_ATTN, D_ATTN)), _full((kw_, kw_)), _full((D_ATTN, D_ATTN)), _full((kw_, kw_)),
                  _full((D_ATTN, kw_)), wide, wide, wide],
        out_specs=[pl.BlockSpec((tm, wout), lambda i: (i, 0)), _full((8, D_ATTN)), _full((8, kw_))],
        out_shape=[_sds((t, wout), BF16), _sds((8, D_ATTN), F32), _sds((8, kw_), F32)],
        compiler_params=_params(("arbitrary",)),
    )(proj, proj, tabs["cq"], tabs["sq"], tabs["ck"], tabs["sk"], qw, kw,
      ac["seg_q"], ac["seg_k"], ac["rot_q"], ac["rot_k"], ac["rep_t"], dqs, dkr, dvr)


def _attn_tile(s, rows=256):
    return min(rows, s)


def _head_masks(shape):
    lane = lax.broadcasted_iota(jnp.int32, shape, 1)
    return [(lane // HEAD_DIM) == g for g in range(KV_LANES // HEAD_DIM)]


def _attn_fwd(name, qn, kr, vr, b, s):
    t = qn.shape[0]
    tq = _attn_tile(s, ATTN_FWD_ROWS)
    nq = s // tq

    def body(q_ref, k_ref, v_ref, o_ref):
        q = q_ref[...]
        k = k_ref[...]
        v = v_ref[...]
        acc = jnp.zeros((tq, KV_LANES), F32)
        for mask in _head_masks((tq, KV_LANES)):
            sc = _dot_nt(jnp.where(mask, q, jnp.zeros_like(q)), k)
            p = jnp.exp(sc - jnp.max(sc, axis=-1, keepdims=True))
            inv = 1.0 / jnp.sum(p, axis=-1, keepdims=True)
            og = jnp.dot(p.astype(BF16), v, preferred_element_type=F32) * inv
            acc = jnp.where(mask, og, acc)
        o_ref[...] = acc

    return pl.pallas_call(
        body, name=name, grid=(b, N_KV, nq),
        in_specs=[pl.BlockSpec((tq, KV_LANES), lambda bi, kv, i: (bi * nq + i, kv)),
                  pl.BlockSpec((s, KV_LANES), lambda bi, kv, i: (bi, kv)),
                  pl.BlockSpec((s, KV_LANES), lambda bi, kv, i: (bi, kv))],
        out_specs=pl.BlockSpec((tq, KV_LANES), lambda bi, kv, i: (bi * nq + i, kv)),
        out_shape=_sds((t, D_ATTN), F32),
        compiler_params=_params(("parallel", "parallel", "parallel")),
    )(qn, kr, vr)


def _attn_bwd(name, qn, kr, vr, do, b, s):
    t = qn.shape[0]
    tq = _attn_tile(s, ATTN_BWD_ROWS)
    nq = s // tq

    def body(q_ref, k_ref, v_ref, do_ref, dq_ref, dk_ref, dv_ref):
        @pl.when(pl.program_id(2) == 0)
        def _():
            dk_ref[...] = jnp.zeros_like(dk_ref)
            dv_ref[...] = jnp.zeros_like(dv_ref)

        q = q_ref[...]
        k = k_ref[...]
        v = v_ref[...]
        dout = do_ref[...].astype(BF16)
        all_masks = _head_masks((tq, KV_LANES))
        dq = jnp.zeros((tq, KV_LANES), F32)
        for g0 in range(0, len(all_masks), ATTN_BWD_STACK):
            masks = all_masks[g0:g0 + ATTN_BWD_STACK]
            q4 = jnp.concatenate([jnp.where(m, q, jnp.zeros_like(q)) for m in masks], axis=0)
            do4 = jnp.concatenate([jnp.where(m, dout, jnp.zeros_like(dout)) for m in masks], axis=0)
            sc = _dot_nt(q4, k)
            p = jnp.exp(sc - jnp.max(sc, axis=-1, keepdims=True))
            p = p * (1.0 / jnp.sum(p, axis=-1, keepdims=True))
            dp = _dot_nt(do4, v)
            ds = (p * (dp - jnp.sum(p * dp, axis=-1, keepdims=True))).astype(BF16)
            dq4 = jnp.dot(ds, k, preferred_element_type=F32)
            for g, m in enumerate(masks):
                dq = jnp.where(m, dq4[g * tq:(g + 1) * tq, :], dq)
            dk_ref[...] += _dot_tn(ds, q4)
            dv_ref[...] += _dot_tn(p.astype(BF16), do4)
        dq_ref[...] = dq

    qspec = pl.BlockSpec((tq, KV_LANES), lambda bi, kv, i: (bi * nq + i, kv))
    kspec = pl.BlockSpec((s, KV_LANES), lambda bi, kv, i: (bi, kv))
    return pl.pallas_call(
        body, name=name, grid=(b, N_KV, nq),
        in_specs=[qspec, kspec, kspec, qspec],
        out_specs=[qspec, kspec, kspec], out_shape=[_sds((t, D_ATTN), F32)] * 3,
        compiler_params=_params(("parallel", "parallel", "arbitrary")),
    )(qn, kr, vr, do)


def _hgrn_consts(rev):
    sel, selt = _sel_matrices()
    cs = _cumsum_matrix(rev)
    return dict(cs=_bf(cs), cs_t=_bf(cs.T), seg=_bf(_seg_matrix(D_HGRN, HEAD_DIM, 1.0)),
                bd=jnp.asarray(_seg_matrix(D_HGRN, HEAD_DIM, 1.0), F32),
                sel=_bf(sel), selt=_bf(selt), seld=_bf(sel - selt))


def _gates(z, lb):
    sig = _gate_sigmoid(z)
    f = lb + (1.0 - lb) * sig
    g = jnp.log(jnp.maximum(f, F_MIN))
    sn = _gate_sigmoid(-z)
    return sig, f, g, sn, (1.0 - lb) * sn


def _pair_decay(b, rev):
    row = lax.broadcasted_iota(jnp.int32, (CHUNK, D_HGRN), 0)
    parts = []
    for t in range(CHUNK):
        m = (row >= t) if rev else (row <= t)
        parts.append(jnp.where(m, jnp.exp(jnp.minimum(b[t:t + 1, :] - b, 0.0)), 0.0))
    return jnp.concatenate(parts, axis=0)


def _rows_rep(a):
    return jnp.concatenate([jnp.broadcast_to(a[t:t + 1, :], a.shape) for t in range(CHUNK)], axis=0)


def _tile_rows(a):
    return jnp.concatenate([a] * CHUNK, axis=0)


def _hgrn_specs(b, s, rev):
    nb = s // HBLK

    def blk(j):
        return (nb - 1 - j) if rev else j

    def col(c):
        return pl.BlockSpec((HBLK, D_HGRN), lambda bi, j: (bi * nb + blk(j), c))

    return nb, blk, col


def _hgrn_fwd(name, proj, lb, b, s, rev, hc):
    t = proj.shape[0]
    nb, blk, col = _hgrn_specs(b, s, rev)
    n_ch = HBLK // CHUNK
    last = 0 if rev else CHUNK - 1

    def body(q_ref, z_ref, v_ref, lb_ref, cs_ref, seg_ref, bd_ref, sel_ref, o_ref, st_ref, state, b_scr, k_scr):
        @pl.when(pl.program_id(1) == 0)
        def _():
            state[...] = jnp.zeros_like(state)

        st_ref[...] = state[...]
        _, _, g, _, kk = _gates(z_ref[...], lb_ref[...])
        k_scr[...] = kk
        b_scr[...] = _ldot3(cs_ref[...], g)

        def chunk(i, carry):
            c = (n_ch - 1 - i) if rev else i
            rows = pl.ds(pl.multiple_of(c * CHUNK, CHUNK), CHUNK)
            q = q_ref[rows, :]
            k = k_scr[rows, :]
            v = v_ref[rows, :]
            bb = b_scr[rows, :]
            bl = bb[last:last + 1, :]
            pairs = _pair_decay(bb, rev) * _rows_rep(q) * _tile_rows(k)
            a = jnp.dot(pairs.astype(BF16), seg_ref[...], preferred_element_type=F32)
            o_intra = jnp.dot(sel_ref[...], (a * _tile_rows(v)).astype(BF16), preferred_element_type=F32)
            st = state[...]
            o_inter = _dot_nt((q * jnp.exp(bb)).astype(BF16), st.astype(BF16))
            o_ref[rows, :] = o_intra + o_inter
            ke = k * jnp.exp(bl - bb)
            state[...] = st * jnp.exp(bl) + bd_ref[...] * _dot_tn(v.astype(BF16), ke.astype(BF16))
            return carry

        lax.fori_loop(0, n_ch, chunk, 0)

    sq = (D_HGRN, D_HGRN)
    return pl.pallas_call(
        body, name=name, grid=(b, nb),
        in_specs=[col(COL_HQ), col(COL_FB if rev else COL_FF), col(COL_HI), _full((1, D_HGRN)),
                  _full((HBLK, HBLK)), _full(sq), _full(sq), _full((CHUNK, CHUNK * CHUNK))],
        out_specs=[pl.BlockSpec((HBLK, D_HGRN), lambda bi, j: (bi * nb + blk(j), 0)),
                   pl.BlockSpec((None,) + sq, lambda bi, j: (bi * nb + blk(j), 0, 0))],
        out_shape=[_sds((t, D_HGRN), F32), _sds((b * nb,) + sq, F32)],
        scratch_shapes=[pltpu.VMEM(sq, F32), pltpu.VMEM((HBLK, D_HGRN), F32), pltpu.VMEM((HBLK, D_HGRN), F32)],
        compiler_params=_params(("parallel", "arbitrary")),
    )(proj, proj, proj, lb, hc["cs"], hc["seg"], hc["bd"], hc["sel"])


def _hgrn_bwd(name, proj, lb, st_blk, do, dq_prev, dv_prev, b, s, rev, hc):
    t = proj.shape[0]
    nb = s // HBLK
    n_ch = HBLK // CHUNK
    last = 0 if rev else CHUNK - 1

    def blk(j):
        return j if rev else (nb - 1 - j)

    def col(c):
        return pl.BlockSpec((HBLK, D_HGRN), lambda bi, j: (bi * nb + blk(j), c))

    def body(q_ref, z_ref, v_ref, lb_ref, st_ref, do_ref, dqp_ref, dvp_ref, cs_ref, cst_ref, seg_ref, bd_ref,
             sel_ref, selt_ref, seld_ref, dq_ref, dv_ref, dz_ref, dlb_ref,
             dstate, states, b_scr, k_scr, db_scr, dk_scr):
        first = jnp.logical_and(pl.program_id(0) == 0, pl.program_id(1) == 0)

        @pl.when(first)
        def _():
            dlb_ref[...] = jnp.zeros_like(dlb_ref)

        @pl.when(pl.program_id(1) == 0)
        def _():
            dstate[...] = jnp.zeros_like(dstate)

        lbv = lb_ref[...]
        z = z_ref[...]
        sig, f, g, sn, kk = _gates(z, lbv)
        k_scr[...] = kk
        b_scr[...] = _ldot3(cs_ref[...], g)

        def rows_of(c):
            return pl.ds(pl.multiple_of(c * CHUNK, CHUNK), CHUNK)

        def replay(i, st):
            c = (n_ch - 1 - i) if rev else i
            rows = rows_of(c)
            states[c] = st
            bb = b_scr[rows, :]
            bl = bb[last:last + 1, :]
            ke = k_scr[rows, :] * jnp.exp(bl - bb)
            return st * jnp.exp(bl) + bd_ref[...] * _dot_tn(v_ref[rows, :].astype(BF16), ke.astype(BF16))

        lax.fori_loop(0, n_ch, replay, st_ref[...])
        row = lax.broadcasted_iota(jnp.int32, (CHUNK, D_HGRN), 0)

        def chunk(i, carry):
            c = i if rev else (n_ch - 1 - i)
            rows = rows_of(c)
            q = q_ref[rows, :]
            k = k_scr[rows, :]
            v = v_ref[rows, :]
            bb = b_scr[rows, :]
            dout = do_ref[rows, :]
            bl = bb[last:last + 1, :]
            st_p = states[c]
            dst_n = dstate[...]
            eb = jnp.exp(bb)
            ebl = jnp.exp(bl - bb)
            ebl_last = jnp.exp(bl)
            qe = q * eb
            ke = k * ebl
            dob = dout.astype(BF16)
            dstb = dst_n.astype(BF16)
            dqe = jnp.dot(dob, st_p.astype(BF16), preferred_element_type=F32)
            dke = jnp.dot(v.astype(BF16), dstb, preferred_element_type=F32)
            dv = _dot_nt(ke.astype(BF16), dstb)
            dbl = jnp.sum(dst_n * st_p, axis=0, keepdims=True) * ebl_last + jnp.sum(dke * ke, axis=0, keepdims=True)
            dq = dqe * eb
            dk = dke * ebl
            db = dqe * qe - dke * ke
            dec = _pair_decay(bb, rev)
            q_rep = _rows_rep(q)
            k_til = _tile_rows(k)
            do_rep = _rows_rep(dout)
            pairs = dec * q_rep * k_til
            a = jnp.dot(pairs.astype(BF16), seg_ref[...], preferred_element_type=F32)
            wb = jnp.dot((_tile_rows(v) * do_rep).astype(BF16), seg_ref[...], preferred_element_type=F32)
            gdec = wb * dec
            dq = dq + jnp.dot(sel_ref[...], (gdec * k_til).astype(BF16), preferred_element_type=F32)
            dk = dk + jnp.dot(selt_ref[...], (gdec * q_rep).astype(BF16), preferred_element_type=F32)
            dv = dv + jnp.dot(selt_ref[...], (a * do_rep).astype(BF16), preferred_element_type=F32)
            db = db + jnp.dot(seld_ref[...], (wb * pairs).astype(BF16), preferred_element_type=F32)
            db = db + jnp.where(row == last, dbl, 0.0)
            dq_ref[rows, :] = dq + dqp_ref[rows, :]
            dv_ref[rows, :] = dv + dvp_ref[rows, :]
            dk_scr[rows, :] = dk
            db_scr[rows, :] = db
            dstate[...] = dst_n * ebl_last + bd_ref[...] * _dot_tn(dob, qe.astype(BF16))
            return carry

        lax.fori_loop(0, n_ch, chunk, 0)
        hi, lo = _split2(db_scr[...])
        dg = (jnp.dot(cst_ref[...], hi, preferred_element_type=F32)
              + jnp.dot(cst_ref[...], lo, preferred_element_type=F32))
        dgf = jnp.where(f > F_MIN, dg / f, 0.0)
        dk = dk_scr[...]
        dz_ref[...] = dgf * (1.0 - lbv) * sig * (1.0 - sig) - dk * (1.0 - lbv) * sn * (1.0 - sn)
        dlb_ref[...] += _rowgroups(dgf * (1.0 - sig) - dk * sn)

    sq = (D_HGRN, D_HGRN)
    blk0 = pl.BlockSpec((HBLK, D_HGRN), lambda bi, j: (bi * nb + blk(j), 0))
    pairs_shape = (CHUNK, CHUNK * CHUNK)
    return pl.pallas_call(
        body, name=name, grid=(b, nb),
        in_specs=[col(COL_HQ), col(COL_FB if rev else COL_FF), col(COL_HI), _full((1, D_HGRN)),
                  pl.BlockSpec((None,) + sq, lambda bi, j: (bi * nb + blk(j), 0, 0)), blk0, blk0, blk0,
                  _full((HBLK, HBLK)), _full((HBLK, HBLK)), _full(sq), _full(sq),
                  _full(pairs_shape), _full(pairs_shape), _full(pairs_shape)],
        out_specs=[blk0, blk0, blk0, _full((8, D_HGRN))],
        out_shape=[_sds((t, D_HGRN), F32)] * 3 + [_sds((8, D_HGRN), F32)],
        scratch_shapes=[pltpu.VMEM(sq, F32), pltpu.VMEM((n_ch,) + sq, F32)] + [pltpu.VMEM((HBLK, D_HGRN), F32)] * 4,
        compiler_params=_params(("arbitrary", "arbitrary")),
    )(proj, proj, proj, lb, st_blk, do, dq_prev, dv_prev,
      hc["cs"], hc["cs_t"], hc["seg"], hc["bd"], hc["sel"], hc["selt"], hc["seld"])


def _scan_chunk_fwd(c, rev, q_ref, v_ref, k_scr, b_scr, state, o_ref, seg_ref, bd_ref, sel_ref):
    last = 0 if rev else CHUNK - 1
    rows = pl.ds(pl.multiple_of(c * CHUNK, CHUNK), CHUNK)
    q = q_ref[rows, :]
    k = k_scr[rows, :]
    v = v_ref[rows, :]
    bb = b_scr[rows, :]
    bl = bb[last:last + 1, :]
    pairs = _pair_decay(bb, rev) * _rows_rep(q) * _tile_rows(k)
    a = jnp.dot(pairs.astype(BF16), seg_ref[...], preferred_element_type=F32)
    o_intra = jnp.dot(sel_ref[...], (a * _tile_rows(v)).astype(BF16), preferred_element_type=F32)
    st = state[...]
    o_inter = _dot_nt((q * jnp.exp(bb)).astype(BF16), st.astype(BF16))
    o_ref[rows, :] = o_intra + o_inter
    ke = k * jnp.exp(bl - bb)
    state[...] = st * jnp.exp(bl) + bd_ref[...] * _dot_tn(v.astype(BF16), ke.astype(BF16))


def _scan_chunks_fwd(chains, seg_ref, bd_ref, sel_ref):
    work = []
    for c, rev, q_ref, v_ref, k_scr, b_scr, state, o_ref in chains:
        last = 0 if rev else CHUNK - 1
        rows = pl.ds(pl.multiple_of(c * CHUNK, CHUNK), CHUNK)
        q = q_ref[rows, :]
        k = k_scr[rows, :]
        v = v_ref[rows, :]
        bb = b_scr[rows, :]
        bl = bb[last:last + 1, :]
        st = state[...]
        work.append(dict(
            rows=rows, v=v, st=st, state=state, o_ref=o_ref, decay=jnp.exp(bl),
            pairs=(_pair_decay(bb, rev) * _rows_rep(q) * _tile_rows(k)).astype(BF16),
            qe=(q * jnp.exp(bb)).astype(BF16), ke=(k * jnp.exp(bl - bb)).astype(BF16), st_b=st.astype(BF16)))
    for w in work:
        w["a"] = jnp.dot(w["pairs"], seg_ref[...], preferred_element_type=F32)
        w["o_inter"] = _dot_nt(w["qe"], w["st_b"])
        w["upd"] = _dot_tn(w["v"].astype(BF16), w["ke"])
    for w in work:
        w["av"] = (w["a"] * _tile_rows(w["v"])).astype(BF16)
    for w in work:
        w["o_ref"][w["rows"], :] = jnp.dot(sel_ref[...], w["av"], preferred_element_type=F32) + w["o_inter"]
        w["state"][...] = w["st"] * w["decay"] + bd_ref[...] * w["upd"]


def _hgrn_fwd2(name, proj, lb_f, lb_b, b, s, hc_f, hc_b):
    t = proj.shape[0]
    nb = s // HBLK
    n_ch = HBLK // CHUNK
    n_chain = 2 * b

    def body(qf_ref, zf_ref, vf_ref, qb_ref, zb_ref, vb_ref, lbf_ref, lbb_ref, csf_ref, csb_ref, seg_ref, bd_ref,
             sel_ref, of_ref, ob_ref, stf_ref, stb_ref, *scr):
        state, b_scr, k_scr = scr[:n_chain], scr[n_chain:2 * n_chain], scr[2 * n_chain:]

        @pl.when(pl.program_id(0) == 0)
        def _():
            for st0 in state:
                st0[...] = jnp.zeros_like(st0)

        chains = []
        for bi in range(b):
            chains.append((False, qf_ref.at[bi], zf_ref.at[bi], vf_ref.at[bi], lbf_ref, csf_ref, of_ref.at[bi],
                           stf_ref.at[bi], 2 * bi))
            chains.append((True, qb_ref.at[bi], zb_ref.at[bi], vb_ref.at[bi], lbb_ref, csb_ref, ob_ref.at[bi],
                           stb_ref.at[bi], 2 * bi + 1))
        for rev, q, z, v, lb, cs, o, st, ci in chains:
            st[...] = state[ci][...]
            _, _, g, _, kk = _gates(z[...], lb[...])
            k_scr[ci][...] = kk
            b_scr[ci][...] = _ldot3(cs[...], g)

        def chunk(i, carry):
            _scan_chunks_fwd([((n_ch - 1 - i) if rev else i, rev, q, v, k_scr[ci], b_scr[ci], state[ci], o)
                              for rev, q, z, v, lb, cs, o, st, ci in chains], seg_ref, bd_ref, sel_ref)
            return carry

        lax.fori_loop(0, n_ch, chunk, 0)

    def col(c, rev):
        return pl.BlockSpec((b, HBLK, D_HGRN), lambda j: (0, (nb - 1 - j) if rev else j, c))

    def st_spec(rev):
        return pl.BlockSpec((b, None, D_HGRN, D_HGRN), lambda j: (0, (nb - 1 - j) if rev else j, 0, 0))

    sq = (D_HGRN, D_HGRN)
    proj3 = proj.reshape(b, s, proj.shape[1])
    o_fw, o_bw, st_fw, st_bw = pl.pallas_call(
        body, name=name, grid=(nb,),
        in_specs=[col(COL_HQ, False), col(COL_FF, False), col(COL_HI, False),
                  col(COL_HQ, True), col(COL_FB, True), col(COL_HI, True),
                  _full((1, D_HGRN)), _full((1, D_HGRN)), _full((HBLK, HBLK)), _full((HBLK, HBLK)),
                  _full(sq), _full(sq), _full((CHUNK, CHUNK * CHUNK))],
        out_specs=[col(0, False), col(0, True), st_spec(False), st_spec(True)],
        out_shape=[_sds((b, s, D_HGRN), F32)] * 2 + [_sds((b, nb) + sq, F32)] * 2,
        scratch_shapes=[pltpu.VMEM(sq, F32)] * n_chain + [pltpu.VMEM((HBLK, D_HGRN), F32)] * (2 * n_chain),
        compiler_params=_params(("arbitrary",)),
    )(proj3, proj3, proj3, proj3, proj3, proj3, lb_f, lb_b, hc_f["cs"], hc_b["cs"], hc_f["seg"], hc_f["bd"],
      hc_f["sel"])
    return o_fw.reshape(t, D_HGRN), o_bw.reshape(t, D_HGRN), st_fw, st_bw


def _scan_replay(c, rev, st, v_ref, k_scr, b_scr, states, bd_ref):
    last = 0 if rev else CHUNK - 1
    rows = pl.ds(pl.multiple_of(c * CHUNK, CHUNK), CHUNK)
    states[c] = st
    bb = b_scr[rows, :]
    bl = bb[last:last + 1, :]
    ke = k_scr[rows, :] * jnp.exp(bl - bb)
    return st * jnp.exp(bl) + bd_ref[...] * _dot_tn(v_ref[rows, :].astype(BF16), ke.astype(BF16))


def _scan_replays(chains, bd_ref):
    work = []
    for c, p, rev, v_ref, k_scr, b_scr, states in chains:
        last = 0 if rev else CHUNK - 1
        rows = pl.ds(pl.multiple_of(c * CHUNK, CHUNK), CHUNK)
        bb = b_scr[rows, :]
        bl = bb[last:last + 1, :]
        work.append((states, p, jnp.exp(bl), v_ref[rows, :].astype(BF16),
                     (k_scr[rows, :] * jnp.exp(bl - bb)).astype(BF16)))
    upds = [_dot_tn(v, ke) for _, _, _, v, ke in work]
    for (states, p, decay, _, _), upd in zip(work, upds):
        states[p + 1] = states[p] * decay + bd_ref[...] * upd


def _scan_chunks_bwd(chains, seg_ref, bd_ref, sel_ref, selt_ref, seld_ref):
    row = lax.broadcasted_iota(jnp.int32, (CHUNK, D_HGRN), 0)
    work = []
    for c, p, rev, q_ref, v_ref, do_ref, k_scr, b_scr, states, dstate, dq_ref, dv_ref, dk_scr, db_scr in chains:
        last = 0 if rev else CHUNK - 1
        rows = pl.ds(pl.multiple_of(c * CHUNK, CHUNK), CHUNK)
        q = q_ref[rows, :]
        k = k_scr[rows, :]
        v = v_ref[rows, :]
        bb = b_scr[rows, :]
        dout = do_ref[rows, :]
        bl = bb[last:last + 1, :]
        st_p = states[p]
        dst_n = dstate[...]
        eb = jnp.exp(bb)
        ebl = jnp.exp(bl - bb)
        qe = q * eb
        ke = k * ebl
        dec = _pair_decay(bb, rev)
        q_rep = _rows_rep(q)
        k_til = _tile_rows(k)
        do_rep = _rows_rep(dout)
        pairs = dec * q_rep * k_til
        work.append(dict(
            rows=rows, last=last, eb=eb, ebl=ebl, ebl_last=jnp.exp(bl), qe=qe, ke=ke, dec=dec, q_rep=q_rep, k_til=k_til,
            do_rep=do_rep, pairs=pairs, st_p=st_p, dst_n=dst_n, dstate=dstate, dq_ref=dq_ref, dv_ref=dv_ref,
            dk_scr=dk_scr, db_scr=db_scr, dob=dout.astype(BF16), dstb=dst_n.astype(BF16), vb=v.astype(BF16),
            pairs_b=pairs.astype(BF16), vdo_b=(_tile_rows(v) * do_rep).astype(BF16)))
    for w in work:
        w["dqe"] = jnp.dot(w["dob"], w["st_p"].astype(BF16), preferred_element_type=F32)
        w["dke"] = jnp.dot(w["vb"], w["dstb"], preferred_element_type=F32)
        w["dv"] = _dot_nt(w["ke"].astype(BF16), w["dstb"])
        w["a"] = jnp.dot(w["pairs_b"], seg_ref[...], preferred_element_type=F32)
        w["wb"] = jnp.dot(w["vdo_b"], seg_ref[...], preferred_element_type=F32)
        w["dst_upd"] = _dot_tn(w["dob"], w["qe"].astype(BF16))
    for w in work:
        gdec = w["wb"] * w["dec"]
        w["x_dq"] = (gdec * w["k_til"]).astype(BF16)
        w["x_dk"] = (gdec * w["q_rep"]).astype(BF16)
        w["x_dv"] = (w["a"] * w["do_rep"]).astype(BF16)
        w["x_db"] = (w["wb"] * w["pairs"]).astype(BF16)
    for w in work:
        dke, dqe = w["dke"], w["dqe"]
        dbl = (jnp.sum(w["dst_n"] * w["st_p"], axis=0, keepdims=True) * w["ebl_last"]
               + jnp.sum(dke * w["ke"], axis=0, keepdims=True))
        dq = dqe * w["eb"] + jnp.dot(sel_ref[...], w["x_dq"], preferred_element_type=F32)
        dk = dke * w["ebl"] + jnp.dot(selt_ref[...], w["x_dk"], preferred_element_type=F32)
        dv = w["dv"] + jnp.dot(selt_ref[...], w["x_dv"], preferred_element_type=F32)
        db = (dqe * w["qe"] - dke * w["ke"] + jnp.dot(seld_ref[...], w["x_db"], preferred_element_type=F32)
              + jnp.where(row == w["last"], dbl, 0.0))
        w["dq_ref"][w["rows"], :] = dq
        w["dv_ref"][w["rows"], :] = dv
        w["dk_scr"][w["rows"], :] = dk
        w["db_scr"][w["rows"], :] = db
        w["dstate"][...] = w["dst_n"] * w["ebl_last"] + bd_ref[...] * w["dst_upd"]


def _scan_chunk_bwd(c, rev, q_ref, v_ref, do_ref, k_scr, b_scr, states, dstate, dq_ref, dv_ref, dk_scr, db_scr,
                    seg_ref, bd_ref, sel_ref, selt_ref, seld_ref):
    last = 0 if rev else CHUNK - 1
    row = lax.broadcasted_iota(jnp.int32, (CHUNK, D_HGRN), 0)
    rows = pl.ds(pl.multiple_of(c * CHUNK, CHUNK), CHUNK)
    q = q_ref[rows, :]
    k = k_scr[rows, :]
    v = v_ref[rows, :]
    bb = b_scr[rows, :]
    dout = do_ref[rows, :]
    bl = bb[last:last + 1, :]
    st_p = states[c]
    dst_n = dstate[...]
    eb = jnp.exp(bb)
    ebl = jnp.exp(bl - bb)
    ebl_last = jnp.exp(bl)
    qe = q * eb
    ke = k * ebl
    dob = dout.astype(BF16)
    dstb = dst_n.astype(BF16)
    dqe = jnp.dot(dob, st_p.astype(BF16), preferred_element_type=F32)
    dke = jnp.dot(v.astype(BF16), dstb, preferred_element_type=F32)
    dv = _dot_nt(ke.astype(BF16), dstb)
    dbl = jnp.sum(dst_n * st_p, axis=0, keepdims=True) * ebl_last + jnp.sum(dke * ke, axis=0, keepdims=True)
    dq = dqe * eb
    dk = dke * ebl
    db = dqe * qe - dke * ke
    dec = _pair_decay(bb, rev)
    q_rep = _rows_rep(q)
    k_til = _tile_rows(k)
    do_rep = _rows_rep(dout)
    pairs = dec * q_rep * k_til
    a = jnp.dot(pairs.astype(BF16), seg_ref[...], preferred_element_type=F32)
    wb = jnp.dot((_tile_rows(v) * do_rep).astype(BF16), seg_ref[...], preferred_element_type=F32)
    gdec = wb * dec
    dq = dq + jnp.dot(sel_ref[...], (gdec * k_til).astype(BF16), preferred_element_type=F32)
    dk = dk + jnp.dot(selt_ref[...], (gdec * q_rep).astype(BF16), preferred_element_type=F32)
    dv = dv + jnp.dot(selt_ref[...], (a * do_rep).astype(BF16), preferred_element_type=F32)
    db = db + jnp.dot(seld_ref[...], (wb * pairs).astype(BF16), preferred_element_type=F32)
    db = db + jnp.where(row == last, dbl, 0.0)
    dq_ref[rows, :] = dq
    dv_ref[rows, :] = dv
    dk_scr[rows, :] = dk
    db_scr[rows, :] = db
    dstate[...] = dst_n * ebl_last + bd_ref[...] * _dot_tn(dob, qe.astype(BF16))


def _hgrn_bwd2(name, proj, lb_f, lb_b, st_f, st_b, do, b, s, hc_f, hc_b):
    t = proj.shape[0]
    nb = s // HBLK
    n_ch = HBLK // CHUNK

    n_chain = 2 * b

    def body(qf_ref, zf_ref, vf_ref, dof_ref, stf_ref, qb_ref, zb_ref, vb_ref, dob_ref, stb_ref, lbf_ref, lbb_ref,
             csf_ref, csb_ref, cstf_ref, cstb_ref, seg_ref, bd_ref, sel_ref, selt_ref, seld_ref,
             dqf_ref, dvf_ref, dzf_ref, dqb_ref, dvb_ref, dzb_ref, dlbf_ref, dlbb_ref,
             *scr):
        dstate, states, b_scr, k_scr, db_scr, dk_scr = [scr[i * n_chain:(i + 1) * n_chain] for i in range(6)]

        @pl.when(pl.program_id(0) == 0)
        def _():
            dlbf_ref[...] = jnp.zeros_like(dlbf_ref)
            dlbb_ref[...] = jnp.zeros_like(dlbb_ref)
            for d0 in dstate:
                d0[...] = jnp.zeros_like(d0)

        chains = []
        for bi in range(b):
            chains.append(dict(rev=False, q=qf_ref.at[bi], z=zf_ref.at[bi], v=vf_ref.at[bi], do=dof_ref.at[bi],
                               st=stf_ref.at[bi], lb=lbf_ref, cs=csf_ref, cst=cstf_ref, dq=dqf_ref.at[bi],
                               dv=dvf_ref.at[bi], dz=dzf_ref.at[bi], dlb=dlbf_ref, ci=2 * bi))
            chains.append(dict(rev=True, q=qb_ref.at[bi], z=zb_ref.at[bi], v=vb_ref.at[bi], do=dob_ref.at[bi],
                               st=stb_ref.at[bi], lb=lbb_ref, cs=csb_ref, cst=cstb_ref, dq=dqb_ref.at[bi],
                               dv=dvb_ref.at[bi], dz=dzb_ref.at[bi], dlb=dlbb_ref, ci=2 * bi + 1))
        for ch in chains:
            sig, f, g, sn, kk = _gates(ch["z"][...], ch["lb"][...])
            k_scr[ch["ci"]][...] = kk
            b_scr[ch["ci"]][...] = _ldot3(ch["cs"][...], g)
            ch["gates"] = (sig, f, sn)

        for ch in chains:
            states[ch["ci"]][0] = ch["st"][...]

        def replay(i, carry):
            _scan_replays([((n_ch - 1 - i) if ch["rev"] else i, i, ch["rev"], ch["v"], k_scr[ch["ci"]],
                            b_scr[ch["ci"]], states[ch["ci"]]) for ch in chains], bd_ref)
            return carry

        lax.fori_loop(0, n_ch - 1, replay, 0)

        def chunk(i, carry):
            args = [(i if ch["rev"] else (n_ch - 1 - i), n_ch - 1 - i, ch["rev"], ch["q"], ch["v"], ch["do"],
                     k_scr[ch["ci"]], b_scr[ch["ci"]], states[ch["ci"]], dstate[ch["ci"]], ch["dq"],
                     ch["dv"], dk_scr[ch["ci"]], db_scr[ch["ci"]]) for ch in chains]
            for g0 in range(0, n_chain, BWD_GROUP):
                _scan_chunks_bwd(args[g0:g0 + BWD_GROUP], seg_ref, bd_ref, sel_ref, selt_ref, seld_ref)
            return carry

        lax.fori_loop(0, n_ch, chunk, 0)
        for ch in chains:
            sig, f, sn = ch["gates"]
            lbv = ch["lb"][...]
            hi, lo = _split2(db_scr[ch["ci"]][...])
            dg = (jnp.dot(ch["cst"][...], hi, preferred_element_type=F32)
                  + jnp.dot(ch["cst"][...], lo, preferred_element_type=F32))
            dgf = jnp.where(f > F_MIN, dg / f, 0.0)
            dk = dk_scr[ch["ci"]][...]
            ch["dz"][...] = dgf * (1.0 - lbv) * sig * (1.0 - sig) - dk * (1.0 - lbv) * sn * (1.0 - sn)
            ch["dlb"][...] += _rowgroups(dgf * (1.0 - sig) - dk * sn)

    def col(c, rev):
        return pl.BlockSpec((b, HBLK, D_HGRN), lambda j: (0, j if rev else (nb - 1 - j), c))

    def st_spec(rev):
        return pl.BlockSpec((b, None, D_HGRN, D_HGRN), lambda j: (0, j if rev else (nb - 1 - j), 0, 0))

    sq = (D_HGRN, D_HGRN)
    blk = (HBLK, D_HGRN)
    pairs_shape = (CHUNK, CHUNK * CHUNK)
    proj3 = proj.reshape(b, s, proj.shape[1])
    do3 = do.reshape(b, s, D_HGRN)
    res = pl.pallas_call(
        body, name=name, grid=(nb,),
        in_specs=[col(COL_HQ, False), col(COL_FF, False), col(COL_HI, False), col(0, False), st_spec(False),
                  col(COL_HQ, True), col(COL_FB, True), col(COL_HI, True), col(0, True), st_spec(True),
                  _full((1, D_HGRN)), _full((1, D_HGRN)), _full((HBLK, HBLK)), _full((HBLK, HBLK)),
                  _full((HBLK, HBLK)), _full((HBLK, HBLK)), _full(sq), _full(sq),
                  _full(pairs_shape), _full(pairs_shape), _full(pairs_shape)],
        out_specs=[col(0, False)] * 3 + [col(0, True)] * 3 + [_full((8, D_HGRN))] * 2,
        out_shape=[_sds((b, s, D_HGRN), F32)] * 6 + [_sds((8, D_HGRN), F32)] * 2,
        scratch_shapes=[pltpu.VMEM(sq, F32)] * n_chain + [pltpu.VMEM((n_ch,) + sq, F32)] * n_chain
        + [pltpu.VMEM(blk, F32)] * (4 * n_chain),
        compiler_params=_params(("arbitrary",)),
    )(proj3, proj3, proj3, do3, st_f, proj3, proj3, proj3, do3, st_b, lb_f, lb_b, hc_f["cs"], hc_b["cs"],
      hc_f["cs_t"], hc_b["cs_t"], hc_f["seg"], hc_f["bd"], hc_f["sel"], hc_f["selt"], hc_f["seld"])
    return [r.reshape(t, D_HGRN) for r in res[:6]] + list(res[6:])


def _lower_bounds(logits):
    n = logits.shape[1]

    def body(x_ref, o_ref):
        x = x_ref[...]
        for d in range(2):
            rows = [x[l * 2 + d:l * 2 + d + 1, :] for l in range(DEPTH)]
            mx = functools.reduce(jnp.maximum, rows)
            ex = [jnp.exp(r - mx) for r in rows]
            tot = functools.reduce(lambda a, c: a + c, ex)
            sm = [e / tot for e in ex]
            run = jnp.zeros_like(sm[0])
            for l in range(DEPTH):
                run = run + sm[l]
                o_ref[l * 2 + d:l * 2 + d + 1, :] = run - sm[0]

    return pl.pallas_call(body, name="hgrn_lower_bounds", out_shape=_sds(logits.shape, F32),
                          in_specs=[_full(logits.shape)], out_specs=_full(logits.shape), grid=(1,),
                          compiler_params=_params(("arbitrary",)))(logits)


def _lower_bounds_bwd(logits, dlb):
    def body(x_ref, g_ref, o_ref):
        x = x_ref[...]
        gv = g_ref[...]
        for d in range(2):
            rows = [x[l * 2 + d:l * 2 + d + 1, :] for l in range(DEPTH)]
            gr = [gv[l * 2 + d:l * 2 + d + 1, :] for l in range(DEPTH)]
            mx = functools.reduce(jnp.maximum, rows)
            ex = [jnp.exp(r - mx) for r in rows]
            tot = functools.reduce(lambda a, c: a + c, ex)
            sm = [e / tot for e in ex]
            dsm = []
            for i in range(DEPTH):
                acc = functools.reduce(lambda a, c: a + c, gr[i:])
                if i == 0:
                    acc = acc - functools.reduce(lambda a, c: a + c, gr)
                dsm.append(acc)
            inner = functools.reduce(lambda a, c: a + c, [sm[i] * dsm[i] for i in range(DEPTH)])
            for i in range(DEPTH):
                o_ref[i * 2 + d:i * 2 + d + 1, :] = sm[i] * (dsm[i] - inner)

    return pl.pallas_call(body, name="hgrn_lower_bounds_bwd", out_shape=_sds(logits.shape, F32),
                          in_specs=[_full(logits.shape), _full(logits.shape)], out_specs=_full(logits.shape),
                          grid=(1,), compiler_params=_params(("arbitrary",)))(logits, dlb)


def _conv_rows(s):
    return s + 2 * (CONV_PAD + 1)


def _conv_fwd(name, proj, dw_w, dw_b, ln_w, ln_b, pw_w, pw_b, b, s):
    t = proj.shape[0]
    pad = CONV_PAD + 1
    nt = s // CONV_TILE

    def body(a_ref, g_ref, w_ref, dwb_ref, lnw_ref, lnb_ref, pw_ref, pwb_ref, y_ref, c_ref, upad, win):
        upad[0:pad, :] = jnp.zeros((pad, D_CONV), F32)
        upad[s + pad:s + 2 * pad, :] = jnp.zeros((pad, D_CONV), F32)

        def fill(i, carry):
            rows = pl.ds(pl.multiple_of(i * CONV_TILE, CONV_TILE), CONV_TILE)
            upad[pl.ds(pl.multiple_of(i * CONV_TILE + pad, pad), CONV_TILE), :] = a_ref[rows, :] * _sigmoid(g_ref[rows, :])
            return carry

        lax.fori_loop(0, nt, fill, 0)

        def tile(i, carry):
            r0 = pl.multiple_of(i * CONV_TILE, CONV_TILE)
            win[...] = upad[pl.ds(r0, CONV_TILE + 2 * pad), :]
            acc = jnp.zeros((CONV_TILE, D_CONV), F32)
            for j in range(CONV_W):
                acc = acc + win[j + 1:j + 1 + CONV_TILE, :] * w_ref[j:j + 1, :]
            c = acc + dwb_ref[...]
            c_ref[pl.ds(r0, CONV_TILE), :] = c
            mu = jnp.mean(c, axis=-1, keepdims=True)
            xc = c - mu
            rstd = lax.rsqrt(jnp.mean(xc * xc, axis=-1, keepdims=True) + LN_EPS)
            n = xc * rstd * lnw_ref[...] + lnb_ref[...]
            y_ref[pl.ds(r0, CONV_TILE), :] = (jnp.dot(_silu(n).astype(BF16), pw_ref[...].astype(BF16),
                                                      preferred_element_type=F32) + pwb_ref[...])
            return carry

        lax.fori_loop(0, nt, tile, 0)

    vec = _full((1, D_CONV))
    return pl.pallas_call(
        body, name=name, grid=(b,),
        in_specs=[pl.BlockSpec((s, D_CONV), lambda bi: (bi, COL_CA)), pl.BlockSpec((s, D_CONV), lambda bi: (bi, COL_CB)),
                  _full((CONV_W + 1, D_CONV)), vec, vec, vec, _full((D_CONV, D_CONV)), vec],
        out_specs=[pl.BlockSpec((s, D_CONV), lambda bi: (bi, 0))] * 2, out_shape=[_sds((t, D_CONV), F32)] * 2,
        scratch_shapes=[pltpu.VMEM((_conv_rows(s), D_CONV), F32), pltpu.VMEM((CONV_TILE + 2 * pad, D_CONV), F32)],
        compiler_params=_params(("parallel",)),
    )(proj, proj, dw_w, dw_b, ln_w, ln_b, pw_w, pw_b)


def _conv_bwd(name, proj, conv_out, dw_w, ln_w, ln_b, pw_w, dy, b, s):
    t = proj.shape[0]
    pad = CONV_PAD + 1
    nt = s // CONV_TILE

    def body(a_ref, g_ref, c_ref, w_ref, lnw_ref, lnb_ref, pw_ref, dy_ref, dab_ref, dpw_ref, ddw_ref, dvec_ref,
             upad, dcpad, tap_acc, win, dwin):
        @pl.when(pl.program_id(0) == 0)
        def _():
            dpw_ref[...] = jnp.zeros_like(dpw_ref)
            ddw_ref[...] = jnp.zeros_like(ddw_ref)
            dvec_ref[...] = jnp.zeros_like(dvec_ref)

        zeros = jnp.zeros((pad, D_CONV), F32)
        upad[0:pad, :] = zeros
        upad[s + pad:s + 2 * pad, :] = zeros
        dcpad[0:pad, :] = zeros
        dcpad[s + pad:s + 2 * pad, :] = zeros
        tap_acc[...] = jnp.zeros_like(tap_acc)

        def inner(i):
            return pl.ds(pl.multiple_of(i * CONV_TILE + pad, pad), CONV_TILE)

        def fill(i, carry):
            rows = pl.ds(pl.multiple_of(i * CONV_TILE, CONV_TILE), CONV_TILE)
            upad[inner(i), :] = a_ref[rows, :] * _sigmoid(g_ref[rows, :])
            return carry

        lax.fori_loop(0, nt, fill, 0)

        def tile_a(i, carry):
            r0 = pl.multiple_of(i * CONV_TILE, CONV_TILE)
            c = c_ref[pl.ds(r0, CONV_TILE), :]
            mu = jnp.mean(c, axis=-1, keepdims=True)
            xc = c - mu
            rstd = lax.rsqrt(jnp.mean(xc * xc, axis=-1, keepdims=True) + LN_EPS)
            xhat = xc * rstd
            n = xhat * lnw_ref[...] + lnb_ref[...]
            dyt = dy_ref[pl.ds(r0, CONV_TILE), :]
            dyb = dyt.astype(BF16)
            dpw_ref[...] += _dot_tn(_silu(n).astype(BF16), dyb)
            dn = _dot_nt(dyb, pw_ref[...].astype(BF16)) * _dsilu(n)
            dxh = dn * lnw_ref[...]
            dc = rstd * (dxh - jnp.mean(dxh, axis=-1, keepdims=True)
                         - xhat * jnp.mean(dxh * xhat, axis=-1, keepdims=True))
            dcpad[inner(i), :] = dc
            dvec_ref[0:1, :] += jnp.sum(dyt, axis=0, keepdims=True)
            dvec_ref[1:2, :] += jnp.sum(dn * xhat, axis=0, keepdims=True)
            dvec_ref[2:3, :] += jnp.sum(dn, axis=0, keepdims=True)
            dvec_ref[3:4, :] += jnp.sum(dc, axis=0, keepdims=True)
            return carry

        lax.fori_loop(0, nt, tile_a, 0)

        def tile_b(i, carry):
            r0 = pl.multiple_of(i * CONV_TILE, CONV_TILE)
            win[...] = upad[pl.ds(r0, CONV_TILE + 2 * pad), :]
            dwin[...] = dcpad[pl.ds(r0, CONV_TILE + 2 * pad), :]
            dct = dwin[pad:pad + CONV_TILE, :]
            du = jnp.zeros((CONV_TILE, D_CONV), F32)
            for j in range(CONV_W):
                du = du + dwin[2 * pad - 1 - j:2 * pad - 1 - j + CONV_TILE, :] * w_ref[j:j + 1, :]
                tap_acc[8 * j:8 * j + 8, :] += _rowgroups(dct * win[j + 1:j + 1 + CONV_TILE, :])
            rows = pl.ds(r0, CONV_TILE)
            sg = _sigmoid(g_ref[rows, :])
            dab_ref[rows, 0:D_CONV] = (du * sg).astype(BF16)
            dab_ref[rows, D_CONV:2 * D_CONV] = (du * a_ref[rows, :] * sg * (1.0 - sg)).astype(BF16)
            return carry

        lax.fori_loop(0, nt, tile_b, 0)
        for j in range(CONV_W):
            ddw_ref[j:j + 1, :] += jnp.sum(tap_acc[8 * j:8 * j + 8, :], axis=0, keepdims=True)

    vec = _full((1, D_CONV))
    return pl.pallas_call(
        body, name=name, grid=(b,),
        in_specs=[pl.BlockSpec((s, D_CONV), lambda bi: (bi, COL_CA)), pl.BlockSpec((s, D_CONV), lambda bi: (bi, COL_CB)),
                  pl.BlockSpec((s, D_CONV), lambda bi: (bi, 0)),
                  _full((CONV_W + 1, D_CONV)), vec, vec, _full((D_CONV, D_CONV)),
                  pl.BlockSpec((s, D_CONV), lambda bi: (bi, 0))],
        out_specs=[pl.BlockSpec((s, 2 * D_CONV), lambda bi: (bi, 0)), _full((D_CONV, D_CONV)),
                   _full((CONV_W + 1, D_CONV)), _full((8, D_CONV))],
        out_shape=[_sds((t, 2 * D_CONV), BF16), _sds((D_CONV, D_CONV), F32), _sds((CONV_W + 1, D_CONV), F32),
                   _sds((8, D_CONV), F32)],
        scratch_shapes=[pltpu.VMEM((_conv_rows(s), D_CONV), F32), pltpu.VMEM((_conv_rows(s), D_CONV), F32),
                        pltpu.VMEM((8 * CONV_W, D_CONV), F32), pltpu.VMEM((CONV_TILE + 2 * pad, D_CONV), F32),
                        pltpu.VMEM((CONV_TILE + 2 * pad, D_CONV), F32)],
        compiler_params=_params(("arbitrary",)),
    )(proj, proj, conv_out, dw_w, ln_w, ln_b, pw_w, dy)


def _mix_fwd(name, y_attn, o_fw, o_bw, proj, y_conv, aw, gw, cw, seg):
    t = y_attn.shape[0]
    tm = _row_tile(t)

    def body(ya_ref, of_ref, ob_ref, hg_ref, yc_ref, aw_ref, gw_ref, cw_ref, seg_ref, o_ref):
        ya = ya_ref[...]
        ra = lax.rsqrt(jnp.mean(ya * ya, axis=-1, keepdims=True) + EPS)
        o_ref[:, 0:D_ATTN] = (ya * ra * aw_ref[...]).astype(BF16)
        o = of_ref[...] + ob_ref[...]
        ro = lax.rsqrt(jnp.dot((o * o).astype(BF16), seg_ref[...], preferred_element_type=F32) + EPS)
        o_ref[:, D_ATTN:D_ATTN + D_HGRN] = (o * ro * gw_ref[...] * _silu(hg_ref[...])).astype(BF16)
        yc = yc_ref[...]
        rc = lax.rsqrt(jnp.mean(yc * yc, axis=-1, keepdims=True) + EPS)
        o_ref[:, D_ATTN + D_HGRN:D_MODEL] = (yc * rc * cw_ref[...]).astype(BF16)

    def tile(w, c=0):
        return pl.BlockSpec((tm, w), lambda i: (i, c))

    return pl.pallas_call(
        body, name=name, grid=(t // tm,),
        in_specs=[tile(D_ATTN), tile(D_HGRN), tile(D_HGRN), tile(D_HGRN, COL_HG), tile(D_CONV),
                  _full((1, D_ATTN)), _full((1, D_HGRN)), _full((1, D_CONV)), _full((D_HGRN, D_HGRN))],
        out_specs=tile(D_MODEL), out_shape=_sds((t, D_MODEL), BF16),
        compiler_params=_params(("parallel",)),
    )(y_attn, o_fw, o_bw, proj, y_conv, aw, gw, cw, seg)


def _mix_bwd(name, dmix, y_attn, o_fw, o_bw, proj, y_conv, aw, gw, cw, seg, deps=()):
    t = y_attn.shape[0]
    tm = _row_tile(t)

    def rms_bwd(x, w, dy):
        r = lax.rsqrt(jnp.mean(x * x, axis=-1, keepdims=True) + EPS)
        gwv = dy * w
        return r * gwv - x * (r * r * r) * jnp.mean(gwv * x, axis=-1, keepdims=True), _rowgroups(dy * x * r)

    def body(dm_ref, ya_ref, of_ref, ob_ref, hg_ref, yc_ref, aw_ref, gw_ref, cw_ref, seg_ref, *rest):
        dya_ref, do_ref, dhg_ref, dyc_ref, daw_ref, dgw_ref, dcw_ref = rest[-7:]

        @pl.when(pl.program_id(0) == 0)
        def _():
            daw_ref[...] = jnp.zeros_like(daw_ref)
            dgw_ref[...] = jnp.zeros_like(dgw_ref)
            dcw_ref[...] = jnp.zeros_like(dcw_ref)

        dya, daw = rms_bwd(ya_ref[...], aw_ref[...], dm_ref[:, 0:D_ATTN])
        dya_ref[...] = dya
        daw_ref[...] += daw
        dyc, dcw = rms_bwd(yc_ref[...], cw_ref[...], dm_ref[:, D_ATTN + D_HGRN:D_MODEL])
        dyc_ref[...] = dyc
        dcw_ref[...] += dcw
        d2 = dm_ref[:, D_ATTN:D_ATTN + D_HGRN]
        o = of_ref[...] + ob_ref[...]
        hg = hg_ref[...]
        ro = lax.rsqrt(jnp.dot((o * o).astype(BF16), seg_ref[...], preferred_element_type=F32) + EPS)
        dn = d2 * _silu(hg)
        dhg_ref[...] = (d2 * o * ro * gw_ref[...] * _dsilu(hg)).astype(BF16)
        gwv = dn * gw_ref[...]
        do_ref[...] = ro * gwv - o * (ro * ro * ro) * _rdot2(gwv * o, seg_ref[...])
        dgw_ref[...] += _rowgroups(dn * o * ro)

    def tile(w, c=0):
        return pl.BlockSpec((tm, w), lambda i: (i, c))

    return pl.pallas_call(
        body, name=name, grid=(t // tm,),
        in_specs=[tile(D_MODEL), tile(D_ATTN), tile(D_HGRN), tile(D_HGRN), tile(D_HGRN, COL_HG), tile(D_CONV),
                  _full((1, D_ATTN)), _full((1, D_HGRN)), _full((1, D_CONV)), _full((D_HGRN, D_HGRN))]
        + [_full(a.shape) for a in deps],
        out_specs=[tile(D_ATTN), tile(D_HGRN), tile(D_HGRN), tile(D_CONV),
                   _full((8, D_ATTN)), _full((8, D_HGRN)), _full((8, D_CONV))],
        out_shape=[_sds((t, D_ATTN), F32), _sds((t, D_HGRN), F32), _sds((t, D_HGRN), BF16), _sds((t, D_CONV), F32),
                   _sds((8, D_ATTN), F32), _sds((8, D_HGRN), F32), _sds((8, D_CONV), F32)],
        compiler_params=_params(("arbitrary",)),
    )(dmix, y_attn, o_fw, o_bw, proj, y_conv, aw, gw, cw, seg, *deps)


def _dproj(name, dp_attn, dq_f, dq_b, dz_fw, dz_bw, dv_f, dv_b, dhg, dp_conv):
    t = dq_f.shape[0]
    tm = _row_tile(t)
    wa, wc = dp_attn.shape[1], dp_conv.shape[1]

    def body(at_ref, qf_ref, qb_ref, zf_ref, zb_ref, vf_ref, vb_ref, hg_ref, cv_ref, o_ref):
        o_ref[:, 0:wa] = at_ref[...]
        cols = (qf_ref[...] + qb_ref[...], zf_ref[...], zb_ref[...], vf_ref[...] + vb_ref[...], hg_ref[...])
        for i, val in enumerate(cols):
            o_ref[:, wa + i * D_HGRN:wa + (i + 1) * D_HGRN] = val.astype(BF16)
        o_ref[:, wa + 5 * D_HGRN:D_IN] = cv_ref[...]

    tile = lambda w: pl.BlockSpec((tm, w), lambda i: (i, 0))
    return pl.pallas_call(
        body, name=name, grid=(t // tm,), in_specs=[tile(wa)] + [tile(D_HGRN)] * 7 + [tile(wc)],
        out_specs=tile(D_IN), out_shape=_sds((t, D_IN), BF16), compiler_params=_params(("parallel",)),
    )(dp_attn, dq_f, dq_b, dz_fw, dz_bw, dv_f, dv_b, dhg, dp_conv)


def _mm_tile(t):
    return min(512, t)


def _resident(shape):
    n = len(shape)
    return pl.BlockSpec(tuple(shape), lambda *_: (0,) * n, pipeline_mode=pl.Buffered(1))


def _w_blk(rows, cols, j_of):
    return pl.BlockSpec((None, rows, cols), lambda *g: (j_of(*g), 0, 0))


def _layer_fwd(l, x, wget, sm, tabs, cst, b, s, deps, target=None):
    t = x.shape[0]
    tm = _mm_tile(t)
    nt = t // tm
    pre = "l%d_" % l
    row = lambda w: pl.BlockSpec((tm, w), lambda i, *_: (i, 0))

    def normed(x_ref, nw_ref):
        xv = x_ref[...]
        r = lax.rsqrt(jnp.mean(xv * xv, axis=-1, keepdims=True) + EPS)
        return (xv * r * nw_ref[...]).astype(BF16)

    def in_body(x_ref, nw_ref, w_ref, *rest):
        o_ref, h_ref = rest[-2:]
        hv = normed(x_ref, nw_ref)
        h_ref[...] = hv
        for j in range(N_CHIP):
            o_ref[:, j * IN_BLK:(j + 1) * IN_BLK] = jnp.dot(hv, w_ref[j], preferred_element_type=F32)

    w_in = wget(l, "w_in", x)
    proj, h1 = pl.pallas_call(
        in_body, name=pre + "in_proj", grid=(nt,),
        in_specs=[row(D_MODEL), _full((1, D_MODEL)), _resident(w_in.shape)] + [_full(a.shape) for a in deps],
        out_specs=[row(D_IN), row(D_MODEL)], out_shape=[_sds((t, D_IN), F32), _sds((t, D_MODEL), BF16)],
        compiler_params=_params(("parallel",)),
    )(x, sm["mix_norm_w"][l], w_in, *deps)
    qn, kr, vr = _attn_prep(pre + "attn_prep", proj, s, tabs, sm["q_norm_w"][l], sm["k_norm_w"][l], cst["attn"])
    y_attn = _attn_fwd(pre + "attn", qn, kr, vr, b, s)
    o_fw, o_bw, st_fw, st_bw = _hgrn_fwd2(pre + "hgrn", proj, sm["lb"][l][0], sm["lb"][l][1], b, s, cst["hg_fw"],
                                          cst["hg_bw"])
    y_conv, conv_out = _conv_fwd(pre + "conv", proj, sm["conv_dw_w"][l], sm["conv_dw_b"][l], sm["conv_ln_w"][l],
                       sm["conv_ln_b"][l], sm["conv_pw_w"][l], sm["conv_pw_b"][l], b, s)
    mixed = _mix_fwd(pre + "mix", y_attn, o_fw, o_bw, proj, y_conv, sm["attn_out_norm_w"][l], sm["gnorm_w"][l],
                     sm["conv_out_norm_w"][l], cst["seg_h"])
    (x1,) = _mm(pre + "out_proj", (nt,),
                [(mixed, row(D_MODEL), wget(l, "w_out", mixed),
                  pl.BlockSpec((N_CHIP, OUT_BLK, D_MODEL), lambda i: (0, 0, 0)), NN)],
                [(x, row(D_MODEL))], [(_sds((t, D_MODEL), F32), row(D_MODEL))],
                lambda tot, xr: (xr + tot,))
    ff3 = pl.BlockSpec((N_CHIP, tm, FF_BLK), lambda i: (0, i, 0))
    ffs = _sds((N_CHIP, t, FF_BLK), BF16)

    def gu_body(x_ref, nw_ref, wg_ref, wu_ref, g_ref, u_ref, a_ref, h_ref):
        hv = normed(x_ref, nw_ref)
        h_ref[...] = hv
        for j in range(N_CHIP):
            gv = jnp.dot(hv, wg_ref[j], preferred_element_type=F32)
            uv = jnp.dot(hv, wu_ref[j], preferred_element_type=F32)
            g_ref[j] = gv.astype(BF16)
            u_ref[j] = uv.astype(BF16)
            a_ref[j] = (_silu(gv) * uv).astype(BF16)

    w_gate, w_up = wget(l, "w_gate", x1), wget(l, "w_up", x1)
    gate, up, act, h2 = pl.pallas_call(
        gu_body, name=pre + "ffn_gate_up", grid=(nt,),
        in_specs=[row(D_MODEL), _full((1, D_MODEL)), _resident(w_gate.shape), _resident(w_up.shape)],
        out_specs=[ff3, ff3, ff3, row(D_MODEL)], out_shape=[ffs, ffs, ffs, _sds((t, D_MODEL), BF16)],
        compiler_params=_params(("parallel",)),
    )(x1, sm["ffn_norm_w"][l], w_gate, w_up)

    def down_body(a_ref, w_ref, x_ref, o_ref):
        tot = x_ref[...]
        for j in range(N_CHIP):
            tot = tot + jnp.dot(a_ref[j], w_ref[j], preferred_element_type=F32)
        o_ref[...] = tot

    def down_loss_body(a_ref, w_ref, x_ref, t_ref, dy_ref, acc_ref):
        tot = x_ref[...]
        for j in range(N_CHIP):
            tot = tot + jnp.dot(a_ref[j], w_ref[j], preferred_element_type=F32)
        e = tot - t_ref[...]
        dy_ref[...] = e * (1.0 / D_MODEL)

        @pl.when(pl.program_id(0) == 0)
        def _():
            acc_ref[...] = jnp.zeros_like(acc_ref)

        acc_ref[...] += _rowgroups(e * e)

    w_down = wget(l, "w_down", act)
    if target is None:
        x2 = pl.pallas_call(
            down_body, name=pre + "ffn_down", grid=(nt,), in_specs=[ff3, _resident(w_down.shape), row(D_MODEL)],
            out_specs=row(D_MODEL), out_shape=_sds((t, D_MODEL), F32), compiler_params=_params(("parallel",)),
        )(act, w_down, x1)
    else:
        x2 = pl.pallas_call(
            down_loss_body, name=pre + "ffn_down_loss", grid=(nt,),
            in_specs=[ff3, _resident(w_down.shape), row(D_MODEL), row(D_MODEL)],
            out_specs=[row(D_MODEL), _full((8, D_MODEL))],
            out_shape=[_sds((t, D_MODEL), F32), _sds((8, D_MODEL), F32)], compiler_params=_params(("arbitrary",)),
        )(act, w_down, x1, target)
    saved = dict(x=x, h1=h1, proj=proj, qn=qn, kr=kr, vr=vr, y_attn=y_attn, o_fw=o_fw, o_bw=o_bw, st_fw=st_fw,
                 st_bw=st_bw, y_conv=y_conv, conv_out=conv_out, mixed=mixed, x1=x1, h2=h2, gate=gate, up=up, act=act)
    return x2, saved


def _layer_bwd(l, dx2, sv, wget, sm, tabs, cst, b, s, on_grads):
    t = dx2.shape[0]
    tm = _mm_tile(t)
    nt = t // tm
    pre = "l%d_" % l
    tk = min(2048, t)
    nk = t // tk
    row = lambda w: pl.BlockSpec((tm, w), lambda i, *_: (i, 0))
    ff3 = pl.BlockSpec((N_CHIP, tm, FF_BLK), lambda i: (0, i, 0))
    ffs = _sds((N_CHIP, t, FF_BLK), BF16)

    w_down, w_gate, w_up = wget(l, "w_down", dx2), wget(l, "w_gate", dx2), wget(l, "w_up", dx2)

    def ddx_body(dx_ref, w_ref, g_ref, u_ref, dg_ref, du_ref):
        dxb = dx_ref[...].astype(BF16)
        for j in range(N_CHIP):
            da = _dot_nt(dxb, w_ref[j])
            g = g_ref[j].astype(F32)
            sg = _sigmoid(g)
            dg_ref[j] = (da * u_ref[j].astype(F32) * (sg * (1.0 + g * (1.0 - sg)))).astype(BF16)
            du_ref[j] = (da * (g * sg)).astype(BF16)

    tf = min(256, t)
    rowf = lambda w: pl.BlockSpec((tf, w), lambda i: (i, 0))
    fff = pl.BlockSpec((N_CHIP, tf, FF_BLK), lambda i: (0, i, 0))

    def ffn_bwd_body(dx_ref, wd_ref, g_ref, u_ref, wg_ref, wu_ref, x_ref, nw_ref, dg_ref, du_ref, dx1_ref, dw_ref):
        dres = dx_ref[...]
        dxb = dres.astype(BF16)
        tot = None
        for j in range(N_CHIP):
            da = _dot_nt(dxb, wd_ref[j])
            g = g_ref[j].astype(F32)
            sg = _sigmoid(g)
            dg = (da * u_ref[j].astype(F32) * (sg * (1.0 + g * (1.0 - sg)))).astype(BF16)
            du = (da * (g * sg)).astype(BF16)
            dg_ref[j] = dg
            du_ref[j] = du
            r = _dot_nt(dg, wg_ref[j]) + _dot_nt(du, wu_ref[j])
            tot = r if tot is None else tot + r
        xv = x_ref[...]
        rn = lax.rsqrt(jnp.mean(xv * xv, axis=-1, keepdims=True) + EPS)
        gw = tot * nw_ref[...]
        dx1_ref[...] = dres + rn * gw - xv * (rn * rn * rn) * jnp.mean(gw * xv, axis=-1, keepdims=True)

        @pl.when(pl.program_id(0) == 0)
        def _():
            dw_ref[...] = jnp.zeros_like(dw_ref)

        dw_ref[...] += _rowgroups(tot * xv * rn)

    dgate, dup, dx1, d_ffn_norm = pl.pallas_call(
        ffn_bwd_body, name=pre + "ffn_bwd", grid=(t // tf,),
        in_specs=[rowf(D_MODEL), _resident(w_down.shape), fff, fff, _resident(w_gate.shape), _resident(w_up.shape),
                  rowf(D_MODEL), _full((1, D_MODEL))],
        out_specs=[fff, fff, rowf(D_MODEL), _full((8, D_MODEL))],
        out_shape=[ffs, ffs, _sds((t, D_MODEL), F32), _sds((8, D_MODEL), F32)],
        compiler_params=_params(("arbitrary",)),
    )(dx2, w_down, sv["gate"], sv["up"], w_gate, w_up, sv["x1"], sm["ffn_norm_w"][l])
    colt = lambda w: pl.BlockSpec((tk, w), lambda j, k: (k, 0))
    fft = pl.BlockSpec((None, tk, FF_BLK), lambda j, k: (j, k, 0))
    (g_down,) = _mm(pre + "ffn_down_dw", (N_CHIP, nk), [(sv["act"], fft, dx2, colt(D_MODEL), TN)], [],
                    [(_sds((N_CHIP, FF_BLK, D_MODEL), BF16), pl.BlockSpec((None, FF_BLK, D_MODEL), lambda j, k: (j, 0, 0)))],
                    lambda tot: (tot,), acc=(1, (FF_BLK, D_MODEL)))
    wff = pl.BlockSpec((None, D_MODEL, FF_BLK), lambda j, k: (j, 0, 0))
    (g_gate,) = _mm(pre + "ffn_gate_dw", (N_CHIP, nk), [(sv["h2"], colt(D_MODEL), dgate, fft, TN)], [],
                    [(_sds((N_CHIP, D_MODEL, FF_BLK), BF16), wff)], lambda tot: (tot,), acc=(1, (D_MODEL, FF_BLK)))
    (g_up,) = _mm(pre + "ffn_up_dw", (N_CHIP, nk), [(sv["h2"], colt(D_MODEL), dup, fft, TN)], [],
                  [(_sds((N_CHIP, D_MODEL, FF_BLK), BF16), wff)], lambda tot: (tot,), acc=(1, (D_MODEL, FF_BLK)))

    def norm_bwd_tail(dh, x_ref, nw_ref, dres_ref, dx_ref, dw_ref):
        xv = x_ref[...]
        r = lax.rsqrt(jnp.mean(xv * xv, axis=-1, keepdims=True) + EPS)
        gw = dh * nw_ref[...]
        dx_ref[...] = dres_ref[...] + r * gw - xv * (r * r * r) * jnp.mean(gw * xv, axis=-1, keepdims=True)

        @pl.when(pl.program_id(0) == 0)
        def _():
            dw_ref[...] = jnp.zeros_like(dw_ref)

        dw_ref[...] += _rowgroups(dh * xv * r)

    def dh_body(dg_ref, du_ref, wg_ref, wu_ref, x_ref, nw_ref, dres_ref, *rest):
        tot = None
        for j in range(N_CHIP):
            r = _dot_nt(dg_ref[j], wg_ref[j]) + _dot_nt(du_ref[j], wu_ref[j])
            tot = r if tot is None else tot + r
        norm_bwd_tail(tot, x_ref, nw_ref, dres_ref, *rest[-2:])

    ffn_deps = on_grads(l, dict(w_gate=g_gate, w_up=g_up, w_down=g_down))

    (dmix,) = _mm(pre + "out_proj_dx", (nt,),
                  [(dx1, row(D_MODEL), wget(l, "w_out", dx2),
                    pl.BlockSpec((N_CHIP, OUT_BLK, D_MODEL), lambda i: (0, 0, 0)), NT)], [],
                  [(_sds((t, D_MODEL), F32), row(D_MODEL))], lambda tot: (tot,))
    (g_out,) = _mm(pre + "out_proj_dw", (N_CHIP, nk),
                   [(sv["mixed"], pl.BlockSpec((tk, OUT_BLK), lambda j, k: (k, j)), dx1, colt(D_MODEL), TN)], [],
                   [(_sds((N_CHIP, OUT_BLK, D_MODEL), BF16), pl.BlockSpec((None, OUT_BLK, D_MODEL), lambda j, k: (j, 0, 0)))],
                   lambda tot: (tot,), acc=(1, (OUT_BLK, D_MODEL)))
    proj = sv["proj"]
    dya, do_h, dhg, dyc, d_aw, d_gw, d_cw = _mix_bwd(
        pre + "mix_bwd", dmix, sv["y_attn"], sv["o_fw"], sv["o_bw"], proj, sv["y_conv"],
        sm["attn_out_norm_w"][l], sm["gnorm_w"][l], sm["conv_out_norm_w"][l], cst["seg_h"],
        ffn_deps + on_grads(l, dict(w_out=g_out)))
    dqs, dkr, dvr = _attn_bwd(pre + "attn_bwd", sv["qn"], sv["kr"], sv["vr"], dya, b, s)
    dp_attn, d_qw, d_kw = _attn_prep_bwd(pre + "attn_prep_bwd", proj, s, tabs, sm["q_norm_w"][l], sm["k_norm_w"][l],
                                         cst["attn"], dqs, dkr, dvr)
    dq_f, dv_f, dz_fw, dq_b, dv_b, dz_bw, dlb_fw, dlb_bw = _hgrn_bwd2(
        pre + "hgrn_bwd", proj, sm["lb"][l][0], sm["lb"][l][1], sv["st_fw"], sv["st_bw"], do_h, b, s,
        cst["hg_fw"], cst["hg_bw"])
    dp_conv, d_pw, d_dw, d_cvec = _conv_bwd(pre + "conv_bwd", proj, sv["conv_out"], sm["conv_dw_w"][l],
                                            sm["conv_ln_w"][l], sm["conv_ln_b"][l], sm["conv_pw_w"][l], dyc, b, s)
    dproj = _dproj(pre + "dproj", dp_attn, dq_f, dq_b, dz_fw, dz_bw, dv_f, dv_b, dhg, dp_conv)
    g_pw = d_pw.reshape(N_CHIP, D_CONV // N_CHIP, D_CONV).astype(BF16)

    (g_in,) = _mm(pre + "in_proj_dw", (N_CHIP, nk),
                  [(sv["h1"], colt(D_MODEL), dproj, pl.BlockSpec((tk, IN_BLK), lambda j, k: (k, j)), TN)], [],
                  [(_sds((N_CHIP, D_MODEL, IN_BLK), BF16), pl.BlockSpec((None, D_MODEL, IN_BLK), lambda j, k: (j, 0, 0)))],
                  lambda tot: (tot,), acc=(1, (D_MODEL, IN_BLK)))

    def indx_body(dp_ref, w_ref, x_ref, nw_ref, dres_ref, *rest):
        tot = None
        for j in range(N_CHIP):
            r = _dot_nt(dp_ref[:, j * IN_BLK:(j + 1) * IN_BLK], w_ref[j])
            tot = r if tot is None else tot + r
        norm_bwd_tail(tot, x_ref, nw_ref, dres_ref, *rest[-2:])

    w_in = wget(l, "w_in", dx2)
    deps = on_grads(l, dict(w_in=g_in, conv_pw_w=g_pw))
    dx, d_mix_norm = pl.pallas_call(
        indx_body, name=pre + "in_proj_dx", grid=(nt,),
        in_specs=[row(D_IN), _resident(w_in.shape), row(D_MODEL), _full((1, D_MODEL)), row(D_MODEL)]
        + [_full(a.shape) for a in deps],
        out_specs=[row(D_MODEL), _full((8, D_MODEL))], out_shape=[_sds((t, D_MODEL), F32), _sds((8, D_MODEL), F32)],
        compiler_params=_params(("arbitrary",)),
    )(dproj, w_in, sv["x"], sm["mix_norm_w"][l], dx1, *deps)
    heads = lambda v, n: v.sum(axis=0).reshape(n, HEAD_DIM).sum(axis=0)
    small = dict(
        mix_norm_w=d_mix_norm.sum(axis=0), q_norm_w=heads(d_qw, D_ATTN // HEAD_DIM), k_norm_w=heads(d_kw, N_KV),
        lb=jnp.stack([dlb_fw.sum(axis=0), dlb_bw.sum(axis=0)]), hgrn_gnorm_w=heads(d_gw, D_HGRN // HEAD_DIM),
        conv_dw_w=d_dw[:CONV_W], conv_dw_b=d_cvec[3], conv_ln_w=d_cvec[1], conv_ln_b=d_cvec[2],
        conv_pw_b=d_cvec[0], attn_out_norm_w=d_aw.sum(axis=0), conv_out_norm_w=d_cw.sum(axis=0),
        ffn_norm_w=d_ffn_norm.sum(axis=0))
    return dx, small


SMALL_ORDER = ("mix_norm_w", "q_norm_w", "k_norm_w", "lb", "hgrn_gnorm_w", "conv_dw_w", "conv_dw_b", "conv_ln_w",
               "conv_ln_b", "conv_pw_b", "attn_out_norm_w", "conv_out_norm_w", "ffn_norm_w")
BIG_ORDER = ("w_in", "w_out", "w_gate", "w_up", "w_down")
SCATTER_ORDER = BIG_ORDER + ("conv_pw_w",)


def _local_step(x, target, wget, sm, deps, on_grads):
    b, s, d = x.shape
    t = b * s
    cos, sin = _rope_tables(s)
    tabs = dict(cq=jnp.tile(cos, (1, D_ATTN // HEAD_DIM)), sq=jnp.tile(sin, (1, D_ATTN // HEAD_DIM)),
                ck=jnp.tile(cos, (1, N_KV)), sk=jnp.tile(sin, (1, N_KV)))
    cst = dict(attn=_attn_consts(), hg_fw=_hgrn_consts(False), hg_bw=_hgrn_consts(True),
               seg_h=_bf(_seg_matrix(D_HGRN, HEAD_DIM, 1.0 / HEAD_DIM)))
    vec = lambda a: a.reshape(DEPTH, 1, -1)
    smk = dict(sm)
    for n in ("mix_norm_w", "conv_dw_b", "conv_ln_w", "conv_ln_b", "conv_pw_b", "attn_out_norm_w", "conv_out_norm_w",
              "ffn_norm_w"):
        smk[n] = vec(sm[n])
    smk["q_norm_w"] = vec(jnp.tile(sm["q_norm_w"], (1, D_ATTN // HEAD_DIM)))
    smk["k_norm_w"] = vec(jnp.tile(sm["k_norm_w"], (1, N_KV)))
    smk["gnorm_w"] = vec(jnp.tile(sm["hgrn_gnorm_w"], (1, D_HGRN // HEAD_DIM)))
    smk["lb"] = sm["lb"].reshape(DEPTH, 2, 1, D_HGRN)
    smk["conv_dw_w"] = jnp.pad(sm["conv_dw_w"], ((0, 0), (0, 1), (0, 0)))

    h = x.reshape(t, d)
    saved = []
    for l in range(DEPTH):
        h, sv = _layer_fwd(l, h, wget, smk, tabs, cst, b, s, deps if l == 0 else (),
                           target.reshape(t, d) if l == DEPTH - 1 else None)
        saved.append(sv)
    dy, sq = h
    sq_sum = jnp.sum(sq)
    dh = dy
    smalls = [None] * DEPTH
    for l in reversed(range(DEPTH)):
        dh, smalls[l] = _layer_bwd(l, dh, saved[l], wget, smk, tabs, cst, b, s, on_grads)
    return sq_sum, dh.reshape(b, s, d), smalls


HBM_SPEC = pl.BlockSpec(memory_space=pltpu.HBM)


def _exchange(name, arrs, mode):
    n = len(arrs)
    if mode == "gather8":
        flips = [(fx, fy, fc) for fx in (0, 1) for fy in (0, 1) for fc in (0, 1)][1:]
    elif mode == "sibling":
        flips = [(0, 0, 1)]
    else:
        flips = [(1, 0, 0), (0, 1, 0), (1, 1, 0)]
    n_f = len(flips)

    def body(*refs):
        ins, outs = refs[:n], refs[n:2 * n]
        send_sems, recv_sems, local_sems = refs[2 * n:]
        x, y, c = lax.axis_index("x"), lax.axis_index("y"), lax.axis_index("c")

        def slot_of(px, py, pc):
            return (2 * px + py) if mode != "gather8" else (4 * px + 2 * py + pc)

        me = slot_of(x, y, c)
        started = []
        for i in range(n):
            if mode != "sibling":
                src = ins[i].at[me] if mode == "scatter4" else ins[i]
                loc = pltpu.make_async_copy(src, outs[i].at[me], local_sems.at[i])
                loc.start()
                started.append(loc)
        sends, recvs = [], []
        for i in range(n):
            for f, (fx, fy, fc) in enumerate(flips):
                peer = (x ^ fx, y ^ fy, c ^ fc)
                ps = slot_of(*peer)
                if mode == "sibling":
                    src, dst, landed = ins[i], outs[i], outs[i]
                elif mode == "scatter4":
                    src, dst, landed = ins[i].at[ps], outs[i].at[me], outs[i].at[ps]
                else:
                    src, dst, landed = ins[i], outs[i].at[me], outs[i].at[ps]
                k = i * n_f + f
                cp = pltpu.make_async_remote_copy(src_ref=src, dst_ref=dst, send_sem=send_sems.at[k],
                                                  recv_sem=recv_sems.at[k], device_id=peer,
                                                  device_id_type=pl.DeviceIdType.MESH)
                cp.start()
                sends.append(cp)
                recvs.append(pltpu.make_async_remote_copy(src_ref=src, dst_ref=landed, send_sem=send_sems.at[k],
                                                          recv_sem=recv_sems.at[k], device_id=peer,
                                                          device_id_type=pl.DeviceIdType.MESH))
        for cp in sends:
            cp.wait_send()
        for cp in recvs:
            cp.wait_recv()
        for loc in started:
            loc.wait()

    def out_sds(a):
        if mode == "gather4":
            return _sds((N_CHIP,) + a.shape, a.dtype)
        if mode == "gather8":
            return _sds((N_DEV,) + a.shape, a.dtype)
        return _sds(a.shape, a.dtype)

    res = pl.pallas_call(
        body, name=name, in_specs=[HBM_SPEC] * n, out_specs=[HBM_SPEC] * n, out_shape=[out_sds(a) for a in arrs],
        scratch_shapes=[pltpu.SemaphoreType.DMA((n * n_f,)), pltpu.SemaphoreType.DMA((n * n_f,)),
                        pltpu.SemaphoreType.DMA((max(n, 1),))],
    )(*arrs)
    return list(res)


SEM_SPEC = pl.BlockSpec(memory_space=pltpu.SEMAPHORE)
SPLIT_EFFECT = pltpu.SideEffectType.DATAFLOW_SIDE_EFFECTING
CHIP_FLIPS = ((1, 0), (0, 1), (1, 1))


def _chip_copies(src_refs, land_refs, send_sems, recv_sems, scatter):
    x, y, c = lax.axis_index("x"), lax.axis_index("y"), lax.axis_index("c")
    me = 2 * x + y
    out = []
    for i, land in enumerate(land_refs):
        if scatter == "sibling":
            kw = dict(send_sem=send_sems.at[i], recv_sem=recv_sems.at[i], device_id=(x, y, 1 - c),
                      device_id_type=pl.DeviceIdType.MESH)
            cp = pltpu.make_async_remote_copy(src_ref=src_refs[i], dst_ref=land, **kw)
            out.append((cp, cp))
            continue
        if scatter == "all":
            dev = 4 * x + 2 * y + c
            for f in range(1, N_DEV):
                fx, fy, fc = (f >> 2) & 1, (f >> 1) & 1, f & 1
                peer = (x ^ fx, y ^ fy, c ^ fc)
                ps = 4 * (x ^ fx) + 2 * (y ^ fy) + (c ^ fc)
                k = i * (N_DEV - 1) + f - 1
                kw = dict(send_sem=send_sems.at[k], recv_sem=recv_sems.at[k], device_id=peer,
                          device_id_type=pl.DeviceIdType.MESH)
                out.append((pltpu.make_async_remote_copy(src_ref=land.at[dev], dst_ref=land.at[dev], **kw),
                            pltpu.make_async_remote_copy(src_ref=land.at[dev], dst_ref=land.at[ps], **kw)))
            continue
        for f, (fx, fy) in enumerate(CHIP_FLIPS):
            peer = (x ^ fx, y ^ fy, c)
            ps = 2 * (x ^ fx) + (y ^ fy)
            src = src_refs[i].at[ps] if scatter else land.at[me]
            k = i * len(CHIP_FLIPS) + f
            kw = dict(send_sem=send_sems.at[k], recv_sem=recv_sems.at[k], device_id=peer,
                      device_id_type=pl.DeviceIdType.MESH)
            out.append((pltpu.make_async_remote_copy(src_ref=src, dst_ref=land.at[me], **kw),
                        pltpu.make_async_remote_copy(src_ref=src, dst_ref=land.at[ps], **kw)))
    return out


def _split_start(name, srcs, lands, scatter):
    n = len(lands)
    n_src = len(srcs)
    n_sem = n * {"sibling": 1, "all": N_DEV - 1}.get(scatter, len(CHIP_FLIPS))

    def body(*refs):
        src_refs = refs[:n_src]
        land_refs = refs[n_src:n_src + n]
        send_sems, recv_sems = refs[n_src + n], refs[n_src + n + 1]
        token = refs[-1]
        for start, _ in _chip_copies(src_refs, land_refs, send_sems, recv_sems, scatter):
            start.start()
        token[...] = jnp.zeros_like(token)

    arrs = list(srcs) + list(lands)
    res = pl.pallas_call(
        body, name=name,
        out_shape=(pltpu.SemaphoreType.DMA((n_sem,)), pltpu.SemaphoreType.DMA((n_sem,)),
                   *[pltpu.HBM(a.shape, a.dtype) for a in arrs], _sds((8, LANES), F32)),
        in_specs=[HBM_SPEC] * len(arrs),
        out_specs=(SEM_SPEC, SEM_SPEC, *[HBM_SPEC] * len(arrs), pl.BlockSpec(memory_space=pltpu.VMEM)),
        input_output_aliases={i: 2 + i for i in range(len(arrs))},
        compiler_params=pltpu.CompilerParams(has_side_effects=SPLIT_EFFECT),
    )(*[pltpu.with_memory_space_constraint(a, pltpu.HBM) for a in arrs])
    return dict(send=res[0], recv=res[1], srcs=list(res[2:2 + n_src]), lands=list(res[2 + n_src:2 + n_src + n]),
                token=res[-1], scatter=scatter)


def _split_wait(name, started, after, with_srcs=False):
    srcs, lands, scatter = started["srcs"], started["lands"], started["scatter"]
    n, n_src = len(lands), len(srcs)

    def body(*refs):
        src_refs = refs[:n_src]
        land_refs = refs[n_src:n_src + n]
        send_sems, recv_sems = refs[n_src + n], refs[n_src + n + 1]
        for _, wait in _chip_copies(src_refs, land_refs, send_sems, recv_sems, scatter):
            wait.wait_send()
            wait.wait_recv()

    arrs = list(srcs) + list(lands)
    res = pl.pallas_call(
        body, name=name, out_shape=tuple(pltpu.HBM(a.shape, a.dtype) for a in arrs),
        in_specs=[HBM_SPEC] * len(arrs) + [SEM_SPEC, SEM_SPEC, pl.BlockSpec(memory_space=pl.ANY)],
        out_specs=tuple([HBM_SPEC] * len(arrs)), input_output_aliases={i: i for i in range(len(arrs))},
        compiler_params=pltpu.CompilerParams(has_side_effects=SPLIT_EFFECT),
    )(*arrs, started["send"], started["recv"], after)
    return (list(res[:n_src]), list(res[n_src:])) if with_srcs else list(res[n_src:])


def _flat_tile(rows):
    for cand in (512, 256, 128, 64, 32, 16, 8):
        if rows % cand == 0:
            return cand
    return rows


def _cast_slot(name, a, l, chip, layers=DEPTH, dtype=BF16, deps=(), slots=N_CHIP):
    r, c = a.shape[0] // layers, a.shape[1]
    tr = _flat_tile(r)

    def body(chip_ref, a_ref, *rest):
        rest[-1][...] = a_ref[...].astype(dtype)

    return pl.pallas_call(
        body, name=name, out_shape=_sds((slots, r, c), dtype),
        grid_spec=pltpu.PrefetchScalarGridSpec(
            num_scalar_prefetch=1, grid=(r // tr,),
            in_specs=[pl.BlockSpec((tr, c), lambda i, ch: (l * (r // tr) + i, 0))]
            + [pl.BlockSpec(d.shape, lambda i, ch: (0, 0)) for d in deps],
            out_specs=pl.BlockSpec((None, tr, c), lambda i, ch: (ch[0], i, 0))),
        compiler_params=_params(("parallel",)))(chip, a, *deps)


def _own_slot(name, g, chip):
    n, r, c = g.shape
    tr = _flat_tile(r)

    def body(chip_ref, g_ref, o_ref):
        o_ref[...] = g_ref[...]

    spec = pl.BlockSpec((None, tr, c), lambda i, ch: (ch[0], i, 0))
    return pl.pallas_call(
        body, name=name, out_shape=_sds(g.shape, g.dtype),
        grid_spec=pltpu.PrefetchScalarGridSpec(num_scalar_prefetch=1, grid=(r // tr,), in_specs=[spec], out_specs=spec),
        compiler_params=_params(("parallel",)))(chip, g)


def _sum_layers(name, lands):
    n, r, c = lands[0].shape
    tr = _flat_tile(r)
    nl = len(lands)

    def body(*refs):
        o_ref = refs[-1]
        for k in range(nl):
            @pl.when(pl.program_id(0) == k)
            def _():
                tot = refs[k][0].astype(F32)
                for i in range(1, n):
                    tot = tot + refs[k][i].astype(F32)
                o_ref[...] = tot

    return pl.pallas_call(
        body, name=name, grid=(nl, r // tr),
        in_specs=[pl.BlockSpec((n, tr, c), lambda l, i, k=k: (0, jnp.where(l == k, i, 0), 0)) for k in range(nl)],
        out_specs=pl.BlockSpec((tr, c), lambda l, i: (l * (r // tr) + i, 0)), out_shape=_sds((nl * r, c), F32),
        compiler_params=_params(("arbitrary", "arbitrary")))(*lands)


def _sum_slots(name, a, scale=None):
    n, r, c = a.shape
    tr = _flat_tile(r)

    def body(a_ref, o_ref):
        tot = a_ref[0].astype(F32)
        for i in range(1, n):
            tot = tot + a_ref[i].astype(F32)
        o_ref[...] = tot

    return pl.pallas_call(body, name=name, grid=(r // tr,),
                          in_specs=[pl.BlockSpec((n, tr, c), lambda i: (0, i, 0))],
                          out_specs=pl.BlockSpec((tr, c), lambda i: (i, 0)), out_shape=_sds((r, c), F32),
                          compiler_params=_params(("parallel",)))(a)


def _adamw(name, w, ga, gb, m, v):
    r, c = w.shape
    tr = _flat_tile(r)
    c1 = 1.0 - B1 ** STEP
    c2 = 1.0 - B2 ** STEP
    two = gb is not None

    def body(*refs):
        if two:
            w_ref, ga_ref, gb_ref, m_ref, v_ref, g_out, d_out, m_out, v_out = refs
            g = ga_ref[...] + gb_ref[...]
        else:
            w_ref, ga_ref, m_ref, v_ref, g_out, d_out, m_out, v_out = refs
            g = ga_ref[...]
        mn = B1 * m_ref[...] + (1.0 - B1) * g
        vn = B2 * v_ref[...] + (1.0 - B2) * (g * g)
        g_out[...] = g
        m_out[...] = mn
        v_out[...] = vn
        d_out[...] = -LR * ((mn / c1) / (jnp.sqrt(vn / c2) + ADAM_EPS) + WD * w_ref[...])

    spec = pl.BlockSpec((tr, c), lambda i: (i, 0))
    ins = [w, ga, gb, m, v] if two else [w, ga, m, v]
    return pl.pallas_call(body, name=name, grid=(r // tr,), in_specs=[spec] * len(ins), out_specs=[spec] * 4,
                          out_shape=[_sds((r, c), F32)] * 4, compiler_params=_params(("parallel",)))(*ins)


WEIGHTS = ('mix_norm_w', 'w_in', 'q_norm_w', 'k_norm_w', 'hgrn_lb_logits', 'hgrn_gnorm_w', 'conv_dw_w', 'conv_dw_b',
           'conv_ln_w', 'conv_ln_b', 'conv_pw_w', 'conv_pw_b', 'attn_out_norm_w', 'conv_out_norm_w', 'w_out',
           'ffn_norm_w', 'w_gate', 'w_up', 'w_down')
SHARDED_SMALL = {"hgrn_lb_logits": 2, "conv_dw_w": 2, "conv_pw_w": 1}
LANES = 128
PACK_ROWS = 256


def _pack(parts):
    flat = jnp.concatenate([p.reshape(-1) for p in parts])
    n = flat.shape[0]
    rows = -(-n // (PACK_ROWS * LANES)) * PACK_ROWS
    return jnp.pad(flat, (0, rows * LANES - n)).reshape(rows, LANES)


def _unpack(packed, shapes):
    flat = packed.reshape(-1)
    out, off = [], 0
    for shp in shapes:
        n = int(np.prod(shp))
        out.append(flat[off:off + n].reshape(shp))
        off += n
    return out


def kernel(x, mix_norm_w, w_in, q_norm_w, k_norm_w, hgrn_lb_logits, hgrn_gnorm_w, conv_dw_w, conv_dw_b, conv_ln_w, conv_ln_b, conv_pw_w, conv_pw_b, attn_out_norm_w, conv_out_norm_w, w_out, ffn_norm_w, w_gate, w_up, w_down, loss_target, m_mix_norm_w, m_w_in, m_q_norm_w, m_k_norm_w, m_hgrn_lb_logits, m_hgrn_gnorm_w, m_conv_dw_w, m_conv_dw_b, m_conv_ln_w, m_conv_ln_b, m_conv_pw_w, m_conv_pw_b, m_attn_out_norm_w, m_conv_out_norm_w, m_w_out, m_ffn_norm_w, m_w_gate, m_w_up, m_w_down, v_mix_norm_w, v_w_in, v_q_norm_w, v_k_norm_w, v_hgrn_lb_logits, v_hgrn_gnorm_w, v_conv_dw_w, v_conv_dw_b, v_conv_ln_w, v_conv_ln_b, v_conv_pw_w, v_conv_pw_b, v_attn_out_norm_w, v_conv_out_norm_w, v_w_out, v_ffn_norm_w, v_w_gate, v_w_up, v_w_down):
    w = dict(mix_norm_w=mix_norm_w, w_in=w_in, q_norm_w=q_norm_w, k_norm_w=k_norm_w, hgrn_lb_logits=hgrn_lb_logits,
             hgrn_gnorm_w=hgrn_gnorm_w, conv_dw_w=conv_dw_w, conv_dw_b=conv_dw_b, conv_ln_w=conv_ln_w,
             conv_ln_b=conv_ln_b, conv_pw_w=conv_pw_w, conv_pw_b=conv_pw_b, attn_out_norm_w=attn_out_norm_w,
             conv_out_norm_w=conv_out_norm_w, w_out=w_out, ffn_norm_w=ffn_norm_w, w_gate=w_gate, w_up=w_up,
             w_down=w_down)
    m = dict(mix_norm_w=m_mix_norm_w, w_in=m_w_in, q_norm_w=m_q_norm_w, k_norm_w=m_k_norm_w,
             hgrn_lb_logits=m_hgrn_lb_logits, hgrn_gnorm_w=m_hgrn_gnorm_w, conv_dw_w=m_conv_dw_w,
             conv_dw_b=m_conv_dw_b, conv_ln_w=m_conv_ln_w, conv_ln_b=m_conv_ln_b, conv_pw_w=m_conv_pw_w,
             conv_pw_b=m_conv_pw_b, attn_out_norm_w=m_attn_out_norm_w, conv_out_norm_w=m_conv_out_norm_w,
             w_out=m_w_out, ffn_norm_w=m_ffn_norm_w, w_gate=m_w_gate, w_up=m_w_up, w_down=m_w_down)
    v = dict(mix_norm_w=v_mix_norm_w, w_in=v_w_in, q_norm_w=v_q_norm_w, k_norm_w=v_k_norm_w,
             hgrn_lb_logits=v_hgrn_lb_logits, hgrn_gnorm_w=v_hgrn_gnorm_w, conv_dw_w=v_conv_dw_w,
             conv_dw_b=v_conv_dw_b, conv_ln_w=v_conv_ln_w, conv_ln_b=v_conv_ln_b, conv_pw_w=v_conv_pw_w,
             conv_pw_b=v_conv_pw_b, attn_out_norm_w=v_attn_out_norm_w, conv_out_norm_w=v_conv_out_norm_w,
             w_out=v_w_out, ffn_norm_w=v_ffn_norm_w, w_gate=v_w_gate, w_up=v_w_up, w_down=v_w_down)
    chip = 2 * lax.axis_index("x") + lax.axis_index("y")

    chip1 = chip.reshape(1).astype(jnp.int32)

    flat2 = lambda a: a.reshape(-1, a.shape[-1])
    groups = [[(l, "w_in")] if first else [(l, n) for n in BIG_ORDER[1:]] for l in range(DEPTH) for first in (1, 0)]
    group_of = {key: g for g, keys in enumerate(groups) for key in keys}
    groups[0].append((0, "small"))
    starts = []
    for g, keys in enumerate(groups):
        after_prev = [starts[-1]["token"]] if starts else []
        slots = [_cast_slot("cast_small", _pack([w[k] for k in SHARDED_SMALL]), 0, chip1, 1, F32, after_prev)
                 if n == "small" else _cast_slot("cast_%s_l%d" % (n, l), flat2(w[n]), l, chip1, deps=after_prev)
                 for l, n in keys]
        starts.append(_split_start("gather_start_g%d" % g, [], slots, False))
    got = {}

    def wget(l, name, after):
        if (l, name) not in got:
            g = group_of[(l, name)]
            for key, arr in zip(groups[g], _split_wait("gather_wait_g%d" % g, starts[g], after)):
                got[key] = arr
        return got[(l, name)]

    pending = []

    def on_grads(l, grads):
        names = [n for n in SCATTER_ORDER if n in grads]
        own = [_own_slot("own_%s_l%d" % (n, l), grads[n], chip1) for n in names]
        st = _split_start("scatter_start_l%d_%s" % (l, names[0]), [grads[n] for n in names], own, True)
        pending.append((l, names, st))
        return [st["token"]]

    wget(0, "w_in", starts[-1]["token"])
    gathered_small = got[(0, "small")]
    parts = [_unpack(gathered_small[j], [w[n].shape for n in SHARDED_SMALL]) for j in range(N_CHIP)]
    full_small = {n: jnp.concatenate([parts[j][i] for j in range(N_CHIP)], axis=ax)
                  for i, (n, ax) in enumerate(SHARDED_SMALL.items())}
    sm = {n: w[n] for n in WEIGHTS if n not in BIG_ORDER and n not in SHARDED_SMALL}
    sm["conv_dw_w"] = full_small["conv_dw_w"]
    sm["conv_pw_w"] = full_small["conv_pw_w"]
    logits = full_small["hgrn_lb_logits"].reshape(DEPTH * 2, D_HGRN)
    sm["lb"] = _lower_bounds(logits).reshape(DEPTH, 2, D_HGRN)

    sq_sum, grad_x, smalls = _local_step(x, loss_target, wget, sm, [st["token"] for st in starts], on_grads)
    loss = lax.psum(0.5 * sq_sum / D_MODEL, ("x", "y", "c"))

    dev1 = (4 * lax.axis_index("x") + 2 * lax.axis_index("y") + lax.axis_index("c")).reshape(1).astype(jnp.int32)
    g_pack = _pack([jnp.stack([smalls[l][n] for l in range(DEPTH)]) for n in SMALL_ORDER])
    small_start = _split_start("small_grads_start", [],
                               [_cast_slot("small_grads_slot", g_pack, 0, dev1, 1, F32, slots=N_DEV)], "all")
    landed = {}
    for l, names, st in pending:
        for n, arr in zip(names, _split_wait("scatter_wait_l%d_%s" % (l, names[0]), st, small_start["token"])):
            landed[(l, n)] = arr
    sums = [_sum_layers("sum_" + n, [landed[(l, n)] for l in range(DEPTH)]) for n in SCATTER_ORDER]
    sib_start = _split_start("sibling_start", sums, [lax.empty(a.shape, a.dtype) for a in sums], "sibling")
    out = {}

    small_names = [n for n in WEIGHTS if n not in SCATTER_ORDER]
    g_all = _split_wait("small_grads_wait", small_start, sib_start["token"])[0]
    g_tot = _sum_slots("sum_small", g_all)
    shapes = [(DEPTH,) + tuple(smalls[0][n].shape) for n in SMALL_ORDER]
    g_small = dict(zip(SMALL_ORDER, _unpack(g_tot, shapes)))
    lb_shard = lax.dynamic_slice_in_dim(g_small.pop("lb").reshape(DEPTH * 2, D_HGRN), chip * HEAD_DIM, HEAD_DIM, 1)
    g_small["hgrn_lb_logits"] = _lower_bounds_bwd(hgrn_lb_logits.reshape(DEPTH * 2, HEAD_DIM), lb_shard).reshape(
        hgrn_lb_logits.shape)
    g_small["conv_dw_w"] = lax.dynamic_slice_in_dim(g_small["conv_dw_w"], chip * HEAD_DIM, HEAD_DIM, 2)
    res = _adamw("adamw_small", _pack([w[n] for n in small_names]), _pack([g_small[n] for n in small_names]), None,
                 _pack([m[n] for n in small_names]), _pack([v[n] for n in small_names]))
    unpacked = [_unpack(r, [w[n].shape for n in small_names]) for r in res]
    for i, n in enumerate(small_names):
        out[n] = [unpacked[k][i] for k in range(4)]
    own, sib = _split_wait("sibling_wait", sib_start, res[0], with_srcs=True)
    for n, ga, gb in zip(SCATTER_ORDER, own, sib):
        big = _adamw("adamw_" + n, flat2(w[n]), ga, gb, flat2(m[n]), flat2(v[n]))
        out[n] = [r.reshape(w[n].shape) for r in big]

    return (loss, grad_x, *[out[n][0] for n in WEIGHTS], *[out[n][1] for n in WEIGHTS],
            *[out[n][2] for n in WEIGHTS], *[out[n][3] for n in WEIGHTS])
```
